```python
import jax, jax.numpy as jnp
from jax import lax
import numpy as np

D_MODEL = 1024
BATCH = 8
SEQ = 4096
DEPTH = 2

N_EVEN = (DEPTH + 1) // 2
N_ODD = DEPTH // 2
RMS_EPS = 1e-5
LN_EPS = 1e-5
D_FF = 4 * D_MODEL

GM_WIDTH = D_MODEL
GM_GROUPS = 8
GM_CH = GM_WIDTH // GM_GROUPS
GM_CHUNK = 128

SSM_D_INNER = D_MODEL
SSM_HEADDIM = 64
SSM_HEADS = SSM_D_INNER // SSM_HEADDIM
SSM_GROUPS = 4
SSM_STATE = 128
SSM_CONV = 4
SSD_CHUNK = 128
SSM_CONV_DIM = SSM_D_INNER + 2 * SSM_GROUPS * SSM_STATE

IN_EVEN = 2 * GM_WIDTH + SSM_D_INNER + SSM_CONV_DIM + SSM_HEADS
MIX_EVEN = GM_WIDTH + SSM_D_INNER

ATTN_HEADS = 16
ATTN_KV_HEADS = 2
ATTN_HEAD_DIM = 64
ATTN_WINDOW = 128
ATTN_BLOCK = ATTN_WINDOW
QKV_DIM = (ATTN_HEADS + 2 * ATTN_KV_HEADS) * ATTN_HEAD_DIM

kernel_name = "hybrid_gmlp_ssd_swa_sinks"


def rmsnorm(x, g, eps=RMS_EPS):
    xf = x.astype(jnp.float32)
    y = xf * lax.rsqrt(jnp.mean(xf * xf, axis=-1, keepdims=True) + eps)
    return (y * g.astype(jnp.float32)).astype(x.dtype)


def layernorm(x, g, b, eps=LN_EPS):
    xf = x.astype(jnp.float32)
    mu = jnp.mean(xf, axis=-1, keepdims=True)
    var = jnp.mean(jnp.square(xf - mu), axis=-1, keepdims=True)
    y = (xf - mu) * lax.rsqrt(var + eps)
    return (y * g.astype(jnp.float32) + b.astype(jnp.float32)).astype(x.dtype)


def gmlp_spatial_gating(u, v, ln_g, ln_b, w_s, b_s):
    b, s, _ = v.shape
    nc = s // GM_CHUNK
    vn = layernorm(v, ln_g, ln_b).reshape(b, nc, GM_CHUNK, GM_GROUPS, GM_CH)
    causal = jnp.tril(jnp.ones((GM_CHUNK, GM_CHUNK), dtype=bool))
    w = jnp.where(causal, w_s, 0.0).astype(v.dtype)
    mixed = jnp.einsum('gts,bcsgd->bctgd', w, vn) + b_s.T[None, None, :, :, None]
    return u * mixed.reshape(b, s, GM_WIDTH)


def causal_depthwise_conv(x, w, bias):
    k = w.shape[0]
    s = x.shape[1]
    xp = jnp.pad(x, ((0, 0), (k - 1, 0), (0, 0)))
    return sum(xp[:, i:i + s] * w[i] for i in range(k)) + bias


def ssd_chunked(x, dt, a_neg, bm, cm):
    b, s, h, p = x.shape
    g, n = bm.shape[2], bm.shape[3]
    r = h // g
    l = SSD_CHUNK
    c = s // l
    xc = x.reshape(b, c, l, g, r, p)
    bc = bm.reshape(b, c, l, g, n)
    cc = cm.reshape(b, c, l, g, n)
    dtc = dt.reshape(b, c, l, g, r)
    a_cum = jnp.cumsum(dtc * a_neg.reshape(g, r), axis=2)
    causal = jnp.tril(jnp.ones((l, l), dtype=bool))[:, :, None, None]
    seg = a_cum[:, :, :, None] - a_cum[:, :, None, :]
    decay = jnp.exp(jnp.where(causal, seg, -jnp.inf))
    cb = jnp.einsum('bclgn,bcsgn->bclsg', cc, bc)
    w_intra = (cb[..., None] * decay * dtc[:, :, None]).astype(x.dtype)
    y_diag = jnp.einsum('bclsgr,bcsgrp->bclgrp', w_intra, xc)
    to_end = (jnp.exp(a_cum[:, :, -1:] - a_cum) * dtc).astype(x.dtype)
    states = jnp.einsum('bclgn,bclgr,bclgrp->bcgrpn', bc, to_end, xc)
    chunk_decay = jnp.exp(a_cum[:, :, -1]).astype(x.dtype)

    def step(hstate, inp):
        dec, st = inp
        return dec[..., None, None] * hstate + st, hstate

    h0 = jnp.zeros((b, g, r, p, n), x.dtype)
    _, prev = lax.scan(step, h0, (jnp.moveaxis(chunk_decay, 1, 0), jnp.moveaxis(states, 1, 0)))
    prev = jnp.moveaxis(prev, 0, 1)
    y_off = jnp.einsum('bclgn,bcgrpn,bclgr->bclgrp', cc, prev, jnp.exp(a_cum).astype(x.dtype))
    return (y_diag + y_off).reshape(b, s, h, p)


def ssd_mixer(z, xbc, dt_raw, conv_w, conv_b, dt_bias, a_log, d_skip, norm_g):
    b, s, _ = z.shape
    xbc = jax.nn.silu(causal_depthwise_conv(xbc, conv_w, conv_b))
    xs, bm, cm = jnp.split(xbc, [SSM_D_INNER, SSM_D_INNER + SSM_GROUPS * SSM_STATE], axis=-1)
    xs = xs.reshape(b, s, SSM_HEADS, SSM_HEADDIM)
    bm = bm.reshape(b, s, SSM_GROUPS, SSM_STATE)
    cm = cm.reshape(b, s, SSM_GROUPS, SSM_STATE)
    dt = jax.nn.softplus(dt_raw.astype(jnp.float32) + dt_bias.astype(jnp.float32))
    a_neg = -jnp.exp(a_log.astype(jnp.float32))
    y = ssd_chunked(xs, dt, a_neg, bm, cm) + xs * d_skip[:, None]
    y = y.reshape(b, s, SSM_D_INNER) * jax.nn.silu(z)
    y = rmsnorm(y.reshape(b, s, SSM_GROUPS, -1), norm_g.reshape(SSM_GROUPS, -1))
    return y.reshape(b, s, SSM_D_INNER)


def even_mixer(h, w_in, w_out, gm_ln_g, gm_ln_b, gm_w_s, gm_b_s,
               conv_w, conv_b, dt_bias, a_log, d_skip, ssm_norm_g):
    proj = h @ w_in
    u, v, z, xbc, dt_raw = jnp.split(
        proj, [GM_WIDTH, 2 * GM_WIDTH, 2 * GM_WIDTH + SSM_D_INNER,
               2 * GM_WIDTH + SSM_D_INNER + SSM_CONV_DIM], axis=-1)
    a_out = gmlp_spatial_gating(jax.nn.gelu(u), jax.nn.gelu(v), gm_ln_g, gm_ln_b, gm_w_s, gm_b_s)
    b_out = ssd_mixer(z, xbc, dt_raw, conv_w, conv_b, dt_bias, a_log, d_skip, ssm_norm_g)
    return jnp.concatenate([a_out, b_out], axis=-1) @ w_out


def sliding_window_sink_attention(q, k, v, sinks):
    b, s, _, d = q.shape
    blk = ATTN_BLOCK
    nb = s // blk
    grp = ATTN_HEADS // ATTN_KV_HEADS
    qb = q.reshape(b, nb, blk, ATTN_KV_HEADS, grp, d)
    kp = jnp.pad(k, ((0, 0), (blk, 0), (0, 0), (0, 0))).reshape(b, nb + 1, blk, ATTN_KV_HEADS, d)
    vp = jnp.pad(v, ((0, 0), (blk, 0), (0, 0), (0, 0))).reshape(b, nb + 1, blk, ATTN_KV_HEADS, d)
    kband = jnp.concatenate([kp[:, :-1], kp[:, 1:]], axis=2)
    vband = jnp.concatenate([vp[:, :-1], vp[:, 1:]], axis=2)
    scores = jnp.einsum('bnqkgd,bnskd->bnkgqs', qb, kband).astype(jnp.float32) * (d ** -0.5)
    qpos = jnp.arange(nb)[:, None, None] * blk + jnp.arange(blk)[None, :, None]
    kpos = jnp.arange(nb)[:, None, None] * blk - blk + jnp.arange(2 * blk)[None, None, :]
    rel = qpos - kpos
    valid = (rel >= 0) & (rel < ATTN_WINDOW) & (kpos >= 0)
    scores = jnp.where(valid[None, :, None, None], scores, -jnp.inf)
    sink = sinks.astype(jnp.float32).reshape(ATTN_KV_HEADS, grp)[None, None, :, :, None, None]
    m = jnp.maximum(jnp.max(scores, axis=-1, keepdims=True), sink)
    pexp = jnp.exp(scores - m)
    denom = jnp.sum(pexp, axis=-1, keepdims=True) + jnp.exp(sink - m)
    probs = (pexp / denom).astype(v.dtype)
    out = jnp.einsum('bnkgqs,bnskd->bnqkgd', probs, vband)
    return out.reshape(b, s, ATTN_HEADS * d)


def odd_mixer(h, w_qkv, b_qkv, w_o, b_o, sinks):
    b, s, _ = h.shape
    qkv = h @ w_qkv + b_qkv
    q, k, v = jnp.split(qkv, [ATTN_HEADS * ATTN_HEAD_DIM, (ATTN_HEADS + ATTN_KV_HEADS) * ATTN_HEAD_DIM], axis=-1)
    q = q.reshape(b, s, ATTN_HEADS, ATTN_HEAD_DIM)
    k = k.reshape(b, s, ATTN_KV_HEADS, ATTN_HEAD_DIM)
    v = v.reshape(b, s, ATTN_KV_HEADS, ATTN_HEAD_DIM)
    return sliding_window_sink_attention(q, k, v, sinks) @ w_o + b_o


def squared_relu_mlp(h, w_up, w_down):
    return jnp.square(jax.nn.relu(h @ w_up)) @ w_down


def _fwd_setup_inputs(seed: int = 0) -> dict:
    key = jax.random.key(seed)
    ks = jax.random.split(key, 24)
    nrm = jax.random.normal
    f32 = jnp.float32
    dt0 = jnp.exp(jax.random.uniform(ks[10], (N_EVEN, SSM_HEADS), f32, np.log(1e-3), np.log(1e-1)))
    return {
        "x": nrm(ks[0], (BATCH, SEQ, D_MODEL), f32),
        "norm_mix_g": 1.0 + 0.02 * nrm(ks[1], (DEPTH, D_MODEL), f32),
        "norm_mlp_g": 1.0 + 0.02 * nrm(ks[2], (DEPTH, D_MODEL), f32),
        "final_norm_g": 1.0 + 0.02 * nrm(ks[3], (D_MODEL,), f32),
        "w_in_even": nrm(ks[4], (N_EVEN, D_MODEL, IN_EVEN), f32) * D_MODEL ** -0.5,
        "w_out_even": nrm(ks[5], (N_EVEN, MIX_EVEN, D_MODEL), f32) * MIX_EVEN ** -0.5,
        "gm_ln_g": 1.0 + 0.02 * nrm(ks[6], (N_EVEN, GM_WIDTH), f32),
        "gm_ln_b": 0.02 * nrm(ks[7], (N_EVEN, GM_WIDTH), f32),
        "gm_w_s": nrm(ks[8], (N_EVEN, GM_GROUPS, GM_CHUNK, GM_CHUNK), f32) * 0.5 * GM_CHUNK ** -0.5,
        "gm_b_s": 1.0 + 0.02 * nrm(ks[9], (N_EVEN, GM_GROUPS, GM_CHUNK), f32),
        "ssm_conv_w": nrm(ks[11], (N_EVEN, SSM_CONV, SSM_CONV_DIM), f32) * SSM_CONV ** -0.5,
        "ssm_conv_b": 0.02 * nrm(ks[12], (N_EVEN, SSM_CONV_DIM), f32),
        "ssm_dt_bias": dt0 + jnp.log(-jnp.expm1(-dt0)),
        "ssm_a_log": jnp.log(jax.random.uniform(ks[13], (N_EVEN, SSM_HEADS), f32, 1.0, 16.0)),
        "ssm_d": 1.0 + 0.1 * nrm(ks[14], (N_EVEN, SSM_HEADS), f32),
        "ssm_norm_g": 1.0 + 0.02 * nrm(ks[15], (N_EVEN, SSM_D_INNER), f32),
        "w_qkv": nrm(ks[16], (N_ODD, D_MODEL, QKV_DIM), f32) * D_MODEL ** -0.5,
        "b_qkv": 0.02 * nrm(ks[17], (N_ODD, QKV_DIM), f32),
        "w_o": nrm(ks[18], (N_ODD, ATTN_HEADS * ATTN_HEAD_DIM, D_MODEL), f32) * (ATTN_HEADS * ATTN_HEAD_DIM) ** -0.5,
        "b_o": 0.02 * nrm(ks[19], (N_ODD, D_MODEL), f32),
        "attn_sinks": 0.5 * nrm(ks[20], (N_ODD, ATTN_HEADS), f32),
        "w_up": nrm(ks[21], (DEPTH, D_MODEL, D_FF), f32) * D_MODEL ** -0.5,
        "w_down": nrm(ks[22], (DEPTH, D_FF, D_MODEL), f32) * D_FF ** -0.5,
    }


def _fwd_reference(x, norm_mix_g, norm_mlp_g, final_norm_g, w_in_even, w_out_even,
              gm_ln_g, gm_ln_b, gm_w_s, gm_b_s, ssm_conv_w, ssm_conv_b,
              ssm_dt_bias, ssm_a_log, ssm_d, ssm_norm_g,
              w_qkv, b_qkv, w_o, b_o, attn_sinks, w_up, w_down):
    h = x
    for i in range(DEPTH):
        j = i // 2
        y = rmsnorm(h, norm_mix_g[i])
        if i % 2 == 0:
            h = h + even_mixer(y, w_in_even[j], w_out_even[j], gm_ln_g[j], gm_ln_b[j],
                               gm_w_s[j], gm_b_s[j], ssm_conv_w[j], ssm_conv_b[j],
                               ssm_dt_bias[j], ssm_a_log[j], ssm_d[j], ssm_norm_g[j])
        else:
            h = h + odd_mixer(y, w_qkv[j], b_qkv[j], w_o[j], b_o[j], attn_sinks[j])
        y = rmsnorm(h, norm_mlp_g[i])
        h = h + squared_relu_mlp(y, w_up[i], w_down[i])
    return rmsnorm(h, final_norm_g)


import jax as _jax
import jax.numpy as _jnp

TWIN_FORMAT = 'train_step'
FWD_PARAMS = ['x', 'norm_mix_g', 'norm_mlp_g', 'final_norm_g', 'w_in_even', 'w_out_even', 'gm_ln_g', 'gm_ln_b', 'gm_w_s', 'gm_b_s', 'ssm_conv_w', 'ssm_conv_b', 'ssm_dt_bias', 'ssm_a_log', 'ssm_d', 'ssm_norm_g', 'w_qkv', 'b_qkv', 'w_o', 'b_o', 'attn_sinks', 'w_up', 'w_down']
TWIN_WEIGHTS = ['norm_mix_g', 'norm_mlp_g', 'final_norm_g', 'w_in_even', 'w_out_even', 'gm_ln_g', 'gm_ln_b', 'gm_w_s', 'gm_b_s', 'ssm_conv_w', 'ssm_conv_b', 'ssm_dt_bias', 'ssm_a_log', 'ssm_d', 'ssm_norm_g', 'w_qkv', 'b_qkv', 'w_o', 'b_o', 'attn_sinks', 'w_up', 'w_down']
TWIN_DIFF_INPUT = 'x'
TWIN_INPUTS = ['x', 'norm_mix_g', 'norm_mlp_g', 'final_norm_g', 'w_in_even', 'w_out_even', 'gm_ln_g', 'gm_ln_b', 'gm_w_s', 'gm_b_s', 'ssm_conv_w', 'ssm_conv_b', 'ssm_dt_bias', 'ssm_a_log', 'ssm_d', 'ssm_norm_g', 'w_qkv', 'b_qkv', 'w_o', 'b_o', 'attn_sinks', 'w_up', 'w_down', 'loss_target', 'm_norm_mix_g', 'm_norm_mlp_g', 'm_final_norm_g', 'm_w_in_even', 'm_w_out_even', 'm_gm_ln_g', 'm_gm_ln_b', 'm_gm_w_s', 'm_gm_b_s', 'm_ssm_conv_w', 'm_ssm_conv_b', 'm_ssm_dt_bias', 'm_ssm_a_log', 'm_ssm_d', 'm_ssm_norm_g', 'm_w_qkv', 'm_b_qkv', 'm_w_o', 'm_b_o', 'm_attn_sinks', 'm_w_up', 'm_w_down', 'v_norm_mix_g', 'v_norm_mlp_g', 'v_final_norm_g', 'v_w_in_even', 'v_w_out_even', 'v_gm_ln_g', 'v_gm_ln_b', 'v_gm_w_s', 'v_gm_b_s', 'v_ssm_conv_w', 'v_ssm_conv_b', 'v_ssm_dt_bias', 'v_ssm_a_log', 'v_ssm_d', 'v_ssm_norm_g', 'v_w_qkv', 'v_b_qkv', 'v_w_o', 'v_b_o', 'v_attn_sinks', 'v_w_up', 'v_w_down']
TWIN_OUTPUTS = ['loss', 'grad_x', 'grad_norm_mix_g', 'grad_norm_mlp_g', 'grad_final_norm_g', 'grad_w_in_even', 'grad_w_out_even', 'grad_gm_ln_g', 'grad_gm_ln_b', 'grad_gm_w_s', 'grad_gm_b_s', 'grad_ssm_conv_w', 'grad_ssm_conv_b', 'grad_ssm_dt_bias', 'grad_ssm_a_log', 'grad_ssm_d', 'grad_ssm_norm_g', 'grad_w_qkv', 'grad_b_qkv', 'grad_w_o', 'grad_b_o', 'grad_attn_sinks', 'grad_w_up', 'grad_w_down', 'delta_norm_mix_g', 'delta_norm_mlp_g', 'delta_final_norm_g', 'delta_w_in_even', 'delta_w_out_even', 'delta_gm_ln_g', 'delta_gm_ln_b', 'delta_gm_w_s', 'delta_gm_b_s', 'delta_ssm_conv_w', 'delta_ssm_conv_b', 'delta_ssm_dt_bias', 'delta_ssm_a_log', 'delta_ssm_d', 'delta_ssm_norm_g', 'delta_w_qkv', 'delta_b_qkv', 'delta_w_o', 'delta_b_o', 'delta_attn_sinks', 'delta_w_up', 'delta_w_down', 'new_m_norm_mix_g', 'new_m_norm_mlp_g', 'new_m_final_norm_g', 'new_m_w_in_even', 'new_m_w_out_even', 'new_m_gm_ln_g', 'new_m_gm_ln_b', 'new_m_gm_w_s', 'new_m_gm_b_s', 'new_m_ssm_conv_w', 'new_m_ssm_conv_b', 'new_m_ssm_dt_bias', 'new_m_ssm_a_log', 'new_m_ssm_d', 'new_m_ssm_norm_g', 'new_m_w_qkv', 'new_m_b_qkv', 'new_m_w_o', 'new_m_b_o', 'new_m_attn_sinks', 'new_m_w_up', 'new_m_w_down', 'new_v_norm_mix_g', 'new_v_norm_mlp_g', 'new_v_final_norm_g', 'new_v_w_in_even', 'new_v_w_out_even', 'new_v_gm_ln_g', 'new_v_gm_ln_b', 'new_v_gm_w_s', 'new_v_gm_b_s', 'new_v_ssm_conv_w', 'new_v_ssm_conv_b', 'new_v_ssm_dt_bias', 'new_v_ssm_a_log', 'new_v_ssm_d', 'new_v_ssm_norm_g', 'new_v_w_qkv', 'new_v_b_qkv', 'new_v_w_o', 'new_v_b_o', 'new_v_attn_sinks', 'new_v_w_up', 'new_v_w_down']
TWIN_LEAF_KINDS = {'loss': 'loss', 'grad_x': 'grad_x', 'grad_norm_mix_g': 'grad_w', 'grad_norm_mlp_g': 'grad_w', 'grad_final_norm_g': 'grad_w', 'grad_w_in_even': 'grad_w', 'grad_w_out_even': 'grad_w', 'grad_gm_ln_g': 'grad_w', 'grad_gm_ln_b': 'grad_w', 'grad_gm_w_s': 'grad_w', 'grad_gm_b_s': 'grad_w', 'grad_ssm_conv_w': 'grad_w', 'grad_ssm_conv_b': 'grad_w', 'grad_ssm_dt_bias': 'grad_w', 'grad_ssm_a_log': 'grad_w', 'grad_ssm_d': 'grad_w', 'grad_ssm_norm_g': 'grad_w', 'grad_w_qkv': 'grad_w', 'grad_b_qkv': 'grad_w', 'grad_w_o': 'grad_w', 'grad_b_o': 'grad_w', 'grad_attn_sinks': 'grad_w', 'grad_w_up': 'grad_w', 'grad_w_down': 'grad_w', 'delta_norm_mix_g': 'delta_w', 'delta_norm_mlp_g': 'delta_w', 'delta_final_norm_g': 'delta_w', 'delta_w_in_even': 'delta_w', 'delta_w_out_even': 'delta_w', 'delta_gm_ln_g': 'delta_w', 'delta_gm_ln_b': 'delta_w', 'delta_gm_w_s': 'delta_w', 'delta_gm_b_s': 'delta_w', 'delta_ssm_conv_w': 'delta_w', 'delta_ssm_conv_b': 'delta_w', 'delta_ssm_dt_bias': 'delta_w', 'delta_ssm_a_log': 'delta_w', 'delta_ssm_d': 'delta_w', 'delta_ssm_norm_g': 'delta_w', 'delta_w_qkv': 'delta_w', 'delta_b_qkv': 'delta_w', 'delta_w_o': 'delta_w', 'delta_b_o': 'delta_w', 'delta_attn_sinks': 'delta_w', 'delta_w_up': 'delta_w', 'delta_w_down': 'delta_w', 'new_m_norm_mix_g': 'new_m', 'new_m_norm_mlp_g': 'new_m', 'new_m_final_norm_g': 'new_m', 'new_m_w_in_even': 'new_m', 'new_m_w_out_even': 'new_m', 'new_m_gm_ln_g': 'new_m', 'new_m_gm_ln_b': 'new_m', 'new_m_gm_w_s': 'new_m', 'new_m_gm_b_s': 'new_m', 'new_m_ssm_conv_w': 'new_m', 'new_m_ssm_conv_b': 'new_m', 'new_m_ssm_dt_bias': 'new_m', 'new_m_ssm_a_log': 'new_m', 'new_m_ssm_d': 'new_m', 'new_m_ssm_norm_g': 'new_m', 'new_m_w_qkv': 'new_m', 'new_m_b_qkv': 'new_m', 'new_m_w_o': 'new_m', 'new_m_b_o': 'new_m', 'new_m_attn_sinks': 'new_m', 'new_m_w_up': 'new_m', 'new_m_w_down': 'new_m', 'new_v_norm_mix_g': 'new_v', 'new_v_norm_mlp_g': 'new_v', 'new_v_final_norm_g': 'new_v', 'new_v_w_in_even': 'new_v', 'new_v_w_out_even': 'new_v', 'new_v_gm_ln_g': 'new_v', 'new_v_gm_ln_b': 'new_v', 'new_v_gm_w_s': 'new_v', 'new_v_gm_b_s': 'new_v', 'new_v_ssm_conv_w': 'new_v', 'new_v_ssm_conv_b': 'new_v', 'new_v_ssm_dt_bias': 'new_v', 'new_v_ssm_a_log': 'new_v', 'new_v_ssm_d': 'new_v', 'new_v_ssm_norm_g': 'new_v', 'new_v_w_qkv': 'new_v', 'new_v_b_qkv': 'new_v', 'new_v_w_o': 'new_v', 'new_v_b_o': 'new_v', 'new_v_attn_sinks': 'new_v', 'new_v_w_up': 'new_v', 'new_v_w_down': 'new_v'}


def _forward(args):
    return _fwd_reference(*[args[k] for k in FWD_PARAMS])


def _output_shape():
    out = _jax.eval_shape(lambda: _forward(_fwd_setup_inputs(0)))
    return out.shape, out.dtype

N_MICROBATCH = 1
ADAM_LR = 0.001
ADAM_B1 = 0.9
ADAM_B2 = 0.999
ADAM_EPS = 1e-08
ADAM_WD = 0.01
ADAM_STEP = 10
PER_EXAMPLE_BATCH_AXIS = {'x': 0, 'loss_target': 0}
SHARED_INPUTS = []
_WEIGHT_DTYPES = {'norm_mix_g': _jnp.float32, 'norm_mlp_g': _jnp.float32, 'final_norm_g': _jnp.float32, 'w_in_even': _jnp.float32, 'w_out_even': _jnp.float32, 'gm_ln_g': _jnp.float32, 'gm_ln_b': _jnp.float32, 'gm_w_s': _jnp.float32, 'gm_b_s': _jnp.float32, 'ssm_conv_w': _jnp.float32, 'ssm_conv_b': _jnp.float32, 'ssm_dt_bias': _jnp.float32, 'ssm_a_log': _jnp.float32, 'ssm_d': _jnp.float32, 'ssm_norm_g': _jnp.float32, 'w_qkv': _jnp.float32, 'b_qkv': _jnp.float32, 'w_o': _jnp.float32, 'b_o': _jnp.float32, 'attn_sinks': _jnp.float32, 'w_up': _jnp.float32, 'w_down': _jnp.float32}
MOMENT_SCALE = {'norm_mix_g': 1.324498e-01, 'norm_mlp_g': 1.306682e-01, 'final_norm_g': 3.283323e+01, 'w_in_even': 8.117101e-02, 'w_out_even': 1.348168e-01, 'gm_ln_g': 2.444950e-02, 'gm_ln_b': 2.624709e-02, 'gm_w_s': 4.872808e-02, 'gm_b_s': 7.042277e-02, 'ssm_conv_w': 8.323188e-02, 'ssm_conv_b': 1.058691e-01, 'ssm_dt_bias': 1.845934e-01, 'ssm_a_log': 3.338288e-01, 'ssm_d': 5.945961e-01, 'ssm_norm_g': 1.073944e-01, 'w_qkv': 5.466234e-02, 'b_qkv': 1.948423e-01, 'w_o': 5.213790e-02, 'b_o': 1.744706e-01, 'attn_sinks': 2.090606e-02, 'w_up': 6.661486e-02, 'w_down': 1.365244e-01}


def _to_microbatches(a, axis):
    t = _jnp.moveaxis(a, axis, 0)
    t = t.reshape((N_MICROBATCH, t.shape[0] // N_MICROBATCH) + t.shape[1:])
    return _jnp.moveaxis(t, 1, axis + 1)


def setup_inputs(seed: int = 0) -> dict:
    inp = _fwd_setup_inputs(seed)
    key = _jax.random.fold_in(_jax.random.key(seed), 7919)
    shape, _ = _output_shape()
    out = dict(inp)
    out["loss_target"] = _jax.random.normal(_jax.random.fold_in(key, 0), shape, _jnp.float32)
    for i, name in enumerate(TWIN_WEIGHTS):
        w = inp[name].astype(_jnp.float32)
        if MOMENT_SCALE is None:
            s = _jnp.sqrt(_jnp.mean(_jnp.square(w)) + 1e-30)
        else:
            s = MOMENT_SCALE[name]
        km, kv = _jax.random.split(_jax.random.fold_in(key, i + 1))
        out[name] = w
        out["m_" + name] = s * _jax.random.normal(km, w.shape, _jnp.float32)
        out["v_" + name] = (s * s) * _jax.random.uniform(kv, w.shape, _jnp.float32, 0.5, 1.5)
    if N_MICROBATCH > 1:
        for name, axis in PER_EXAMPLE_BATCH_AXIS.items():
            out[name] = _to_microbatches(out[name], axis)
    return {'x': out['x'], 'norm_mix_g': out['norm_mix_g'], 'norm_mlp_g': out['norm_mlp_g'], 'final_norm_g': out['final_norm_g'], 'w_in_even': out['w_in_even'], 'w_out_even': out['w_out_even'], 'gm_ln_g': out['gm_ln_g'], 'gm_ln_b': out['gm_ln_b'], 'gm_w_s': out['gm_w_s'], 'gm_b_s': out['gm_b_s'], 'ssm_conv_w': out['ssm_conv_w'], 'ssm_conv_b': out['ssm_conv_b'], 'ssm_dt_bias': out['ssm_dt_bias'], 'ssm_a_log': out['ssm_a_log'], 'ssm_d': out['ssm_d'], 'ssm_norm_g': out['ssm_norm_g'], 'w_qkv': out['w_qkv'], 'b_qkv': out['b_qkv'], 'w_o': out['w_o'], 'b_o': out['b_o'], 'attn_sinks': out['attn_sinks'], 'w_up': out['w_up'], 'w_down': out['w_down'], 'loss_target': out['loss_target'], 'm_norm_mix_g': out['m_norm_mix_g'], 'm_norm_mlp_g': out['m_norm_mlp_g'], 'm_final_norm_g': out['m_final_norm_g'], 'm_w_in_even': out['m_w_in_even'], 'm_w_out_even': out['m_w_out_even'], 'm_gm_ln_g': out['m_gm_ln_g'], 'm_gm_ln_b': out['m_gm_ln_b'], 'm_gm_w_s': out['m_gm_w_s'], 'm_gm_b_s': out['m_gm_b_s'], 'm_ssm_conv_w': out['m_ssm_conv_w'], 'm_ssm_conv_b': out['m_ssm_conv_b'], 'm_ssm_dt_bias': out['m_ssm_dt_bias'], 'm_ssm_a_log': out['m_ssm_a_log'], 'm_ssm_d': out['m_ssm_d'], 'm_ssm_norm_g': out['m_ssm_norm_g'], 'm_w_qkv': out['m_w_qkv'], 'm_b_qkv': out['m_b_qkv'], 'm_w_o': out['m_w_o'], 'm_b_o': out['m_b_o'], 'm_attn_sinks': out['m_attn_sinks'], 'm_w_up': out['m_w_up'], 'm_w_down': out['m_w_down'], 'v_norm_mix_g': out['v_norm_mix_g'], 'v_norm_mlp_g': out['v_norm_mlp_g'], 'v_final_norm_g': out['v_final_norm_g'], 'v_w_in_even': out['v_w_in_even'], 'v_w_out_even': out['v_w_out_even'], 'v_gm_ln_g': out['v_gm_ln_g'], 'v_gm_ln_b': out['v_gm_ln_b'], 'v_gm_w_s': out['v_gm_w_s'], 'v_gm_b_s': out['v_gm_b_s'], 'v_ssm_conv_w': out['v_ssm_conv_w'], 'v_ssm_conv_b': out['v_ssm_conv_b'], 'v_ssm_dt_bias': out['v_ssm_dt_bias'], 'v_ssm_a_log': out['v_ssm_a_log'], 'v_ssm_d': out['v_ssm_d'], 'v_ssm_norm_g': out['v_ssm_norm_g'], 'v_w_qkv': out['v_w_qkv'], 'v_b_qkv': out['v_b_qkv'], 'v_w_o': out['v_w_o'], 'v_b_o': out['v_b_o'], 'v_attn_sinks': out['v_attn_sinks'], 'v_w_up': out['v_w_up'], 'v_w_down': out['v_w_down']}


def _loss(weights, diff, rest, loss_target):
    with _jax.named_scope("forward"):
        args = {**rest, TWIN_DIFF_INPUT: diff, **{k: w.astype(_WEIGHT_DTYPES[k]) for k, w in weights.items()}}
        y = _forward(args)
    with _jax.named_scope("loss_head"):
        err = _jnp.square(y.astype(_jnp.float32) - loss_target)
        return 0.5 * _jnp.sum(_jnp.mean(err, axis=-1)) if err.ndim else 0.5 * err


def _adamw(w, g, m, v):
    m = ADAM_B1 * m + (1.0 - ADAM_B1) * g
    v = ADAM_B2 * v + (1.0 - ADAM_B2) * _jnp.square(g)
    m_hat = m / (1.0 - ADAM_B1 ** ADAM_STEP)
    v_hat = v / (1.0 - ADAM_B2 ** ADAM_STEP)
    delta = -ADAM_LR * (m_hat / (_jnp.sqrt(v_hat) + ADAM_EPS) + ADAM_WD * w)
    return delta, m, v


def reference(x, norm_mix_g, norm_mlp_g, final_norm_g, w_in_even, w_out_even, gm_ln_g, gm_ln_b, gm_w_s, gm_b_s, ssm_conv_w, ssm_conv_b, ssm_dt_bias, ssm_a_log, ssm_d, ssm_norm_g, w_qkv, b_qkv, w_o, b_o, attn_sinks, w_up, w_down, loss_target, m_norm_mix_g, m_norm_mlp_g, m_final_norm_g, m_w_in_even, m_w_out_even, m_gm_ln_g, m_gm_ln_b, m_gm_w_s, m_gm_b_s, m_ssm_conv_w, m_ssm_conv_b, m_ssm_dt_bias, m_ssm_a_log, m_ssm_d, m_ssm_norm_g, m_w_qkv, m_b_qkv, m_w_o, m_b_o, m_attn_sinks, m_w_up, m_w_down, v_norm_mix_g, v_norm_mlp_g, v_final_norm_g, v_w_in_even, v_w_out_even, v_gm_ln_g, v_gm_ln_b, v_gm_w_s, v_gm_b_s, v_ssm_conv_w, v_ssm_conv_b, v_ssm_dt_bias, v_ssm_a_log, v_ssm_d, v_ssm_norm_g, v_w_qkv, v_b_qkv, v_w_o, v_b_o, v_attn_sinks, v_w_up, v_w_down):
    given = dict(x=x, norm_mix_g=norm_mix_g, norm_mlp_g=norm_mlp_g, final_norm_g=final_norm_g, w_in_even=w_in_even, w_out_even=w_out_even, gm_ln_g=gm_ln_g, gm_ln_b=gm_ln_b, gm_w_s=gm_w_s, gm_b_s=gm_b_s, ssm_conv_w=ssm_conv_w, ssm_conv_b=ssm_conv_b, ssm_dt_bias=ssm_dt_bias, ssm_a_log=ssm_a_log, ssm_d=ssm_d, ssm_norm_g=ssm_norm_g, w_qkv=w_qkv, b_qkv=b_qkv, w_o=w_o, b_o=b_o, attn_sinks=attn_sinks, w_up=w_up, w_down=w_down, loss_target=loss_target, m_norm_mix_g=m_norm_mix_g, m_norm_mlp_g=m_norm_mlp_g, m_final_norm_g=m_final_norm_g, m_w_in_even=m_w_in_even, m_w_out_even=m_w_out_even, m_gm_ln_g=m_gm_ln_g, m_gm_ln_b=m_gm_ln_b, m_gm_w_s=m_gm_w_s, m_gm_b_s=m_gm_b_s, m_ssm_conv_w=m_ssm_conv_w, m_ssm_conv_b=m_ssm_conv_b, m_ssm_dt_bias=m_ssm_dt_bias, m_ssm_a_log=m_ssm_a_log, m_ssm_d=m_ssm_d, m_ssm_norm_g=m_ssm_norm_g, m_w_qkv=m_w_qkv, m_b_qkv=m_b_qkv, m_w_o=m_w_o, m_b_o=m_b_o, m_attn_sinks=m_attn_sinks, m_w_up=m_w_up, m_w_down=m_w_down, v_norm_mix_g=v_norm_mix_g, v_norm_mlp_g=v_norm_mlp_g, v_final_norm_g=v_final_norm_g, v_w_in_even=v_w_in_even, v_w_out_even=v_w_out_even, v_gm_ln_g=v_gm_ln_g, v_gm_ln_b=v_gm_ln_b, v_gm_w_s=v_gm_w_s, v_gm_b_s=v_gm_b_s, v_ssm_conv_w=v_ssm_conv_w, v_ssm_conv_b=v_ssm_conv_b, v_ssm_dt_bias=v_ssm_dt_bias, v_ssm_a_log=v_ssm_a_log, v_ssm_d=v_ssm_d, v_ssm_norm_g=v_ssm_norm_g, v_w_qkv=v_w_qkv, v_b_qkv=v_b_qkv, v_w_o=v_w_o, v_b_o=v_b_o, v_attn_sinks=v_attn_sinks, v_w_up=v_w_up, v_w_down=v_w_down)
    weights = {n: given[n] for n in TWIN_WEIGHTS}
    shared = {n: given[n] for n in SHARED_INPUTS}
    per_example = {n: given[n] for n in ['x']}
    grad_fn = _jax.value_and_grad(_loss, argnums=(0, 1))

    def one_microbatch(ex, loss_target):
        ex = dict(ex)
        diff = ex.pop(TWIN_DIFF_INPUT)
        return grad_fn(weights, diff, {**shared, **ex}, loss_target)

    if N_MICROBATCH == 1:
        loss, (grad_w, grad_x) = one_microbatch(per_example, given["loss_target"])
    else:
        def body(carry, xs):
            loss_sum, grad_sum = carry
            l_k, (gw_k, gx_k) = one_microbatch(xs[0], xs[1])
            with _jax.named_scope("update"):
                return (loss_sum + l_k, _jax.tree.map(_jnp.add, grad_sum, gw_k)), gx_k

        init = (_jnp.zeros((), _jnp.float32), _jax.tree.map(_jnp.zeros_like, weights))
        (loss, grad_w), grad_x = _jax.lax.scan(body, init, (per_example, given["loss_target"]))
    with _jax.named_scope("update"):
        delta_w, new_m, new_v = {}, {}, {}
        for n in TWIN_WEIGHTS:
            delta_w[n], new_m[n], new_v[n] = _adamw(weights[n], grad_w[n], given["m_" + n], given["v_" + n])
    return (loss, grad_x, *[grad_w[n] for n in TWIN_WEIGHTS], *[delta_w[n] for n in TWIN_WEIGHTS],
            *[new_m[n] for n in TWIN_WEIGHTS], *[new_v[n] for n in TWIN_WEIGHTS])
```

```python
import functools

import jax
import jax.numpy as jnp
from jax import lax
from jax.experimental import pallas as pl
from jax.experimental.pallas import tpu as pltpu

F32 = jnp.float32
BF16 = jnp.bfloat16

N_DEV = 8
D_MODEL = 1024
D_FF = 4096
RMS_EPS = 1e-5
LN_EPS = 1e-5
CHUNK = 128
GM_GROUPS = 8
SSM_HEADS = 16
SSM_HEADDIM = 64
SSM_GROUPS = 4
SSM_STATE = 128
SSM_CONV = 4
CONV_DIM = 2048
IN_EVEN = 5136
REST_W = 3200
ATTN_HEADS = 16
ATTN_KV = 2
HEAD_DIM = 64
QKV_DIM = 1280
LANES = 128
HALO = 8

ADAM_LR = 0.001
ADAM_B1 = 0.9
ADAM_B2 = 0.999
ADAM_EPS = 1e-08
ADAM_WD = 0.01
ADAM_STEP = 10

VMEM_LIMIT_BYTES = 56 * 1024 * 1024


def _params(*sem):
    return pltpu.CompilerParams(dimension_semantics=sem, vmem_limit_bytes=VMEM_LIMIT_BYTES)


_NN = (((1,), (0,)), ((), ()))
_NT = (((1,), (1,)), ((), ()))
_TN = (((0,), (0,)), ((), ()))


def _dg(a, b, dims):
    return lax.dot_general(a.astype(BF16), b.astype(BF16), dims, preferred_element_type=F32)


@jax.custom_vjp
def _nn(a, b):
    return _dg(a, b, _NN)


@jax.custom_vjp
def _nt(a, b):
    return _dg(a, b, _NT)


@jax.custom_vjp
def _tn(a, b):
    return _dg(a, b, _TN)


_nn.defvjp(lambda a, b: (_dg(a, b, _NN), (a, b)), lambda r, g: (_nt(g, r[1]), _tn(r[0], g)))
_nt.defvjp(lambda a, b: (_dg(a, b, _NT), (a, b)), lambda r, g: (_nn(g, r[1]), _tn(g, r[0])))
_tn.defvjp(lambda a, b: (_dg(a, b, _TN), (a, b)), lambda r, g: (_nt(r[1], g), _nn(r[0], g)))


def _split3_dot(tri, x):
    x1 = x.astype(BF16)
    r1 = x - x1.astype(F32)
    x2 = r1.astype(BF16)
    x3 = (r1 - x2.astype(F32)).astype(BF16)
    t = tri.astype(BF16)
    dot = lambda p: lax.dot_general(t, p, _NN, preferred_element_type=F32)
    return dot(x1) + dot(x2) + dot(x3)


def _tri(lower):
    r = lax.broadcasted_iota(jnp.int32, (CHUNK, CHUNK), 0)
    c = lax.broadcasted_iota(jnp.int32, (CHUNK, CHUNK), 1)
    return jnp.where((r >= c) if lower else (r <= c), 1.0, 0.0).astype(F32)


@jax.custom_vjp
def _cumsum_rows(x):
    return _split3_dot(_tri(True), x)


_cumsum_rows.defvjp(lambda x: (_split3_dot(_tri(True), x), None), lambda _, g: (_split3_dot(_tri(False), g),))


def _sigmoid(x):
    return 1.0 / (1.0 + jnp.exp(-x))


def _silu(x):
    return x * _sigmoid(x)


def _softplus(x):
    return jnp.maximum(x, 0.0) + jnp.log(1.0 + jnp.exp(-jnp.abs(x)))


def _gelu_tanh(x):
    return 0.5 * x * (1.0 + jnp.tanh(0.7978845608028654 * (x + 0.044715 * (x * x * x))))


def _rmsnorm(x, g):
    return x * lax.rsqrt(jnp.mean(x * x, axis=-1, keepdims=True) + RMS_EPS) * g


def _gmlp_chunk(u, v, ln_g, ln_b, w_s, b_s):
    gu = _gelu_tanh(u)
    gv = _gelu_tanh(v)
    mu = jnp.mean(gv, axis=-1, keepdims=True)
    var = jnp.mean(jnp.square(gv - mu), axis=-1, keepdims=True)
    vn = (gv - mu) * lax.rsqrt(var + LN_EPS) * ln_g + ln_b
    r = lax.broadcasted_iota(jnp.int32, (CHUNK, CHUNK), 0)
    c = lax.broadcasted_iota(jnp.int32, (CHUNK, CHUNK), 1)
    causal = r >= c
    outs = []
    for g in range(GM_GROUPS):
        cols = slice(g * LANES, (g + 1) * LANES)
        mixed = _nn(jnp.where(causal, w_s[g], 0.0), vn[:, cols]) + b_s[g]
        outs.append(gu[:, cols] * mixed)
    return jnp.concatenate(outs, axis=1)


def _lane_pick(row, h):
    lane = lax.broadcasted_iota(jnp.int32, row.shape, 1)
    return jnp.sum(jnp.where(lane == h, row, 0.0), axis=1, keepdims=True)


def _col_pick(m, h):
    lane = lax.broadcasted_iota(jnp.int32, m.shape, 1)
    return jnp.sum(jnp.where(lane == h, m, 0.0), axis=1, keepdims=True)


def _row_pick(m, h):
    sub = lax.broadcasted_iota(jnp.int32, m.shape, 0)
    return jnp.sum(jnp.where(sub == h, m, 0.0), axis=0, keepdims=True)


def _ssd_chunk(pre, z, dt_raw, h_prev, dt_bias, a_log, d_skip, norm_g):
    xbc = _silu(pre)
    dt = _softplus(dt_raw + dt_bias)
    da = dt * (-jnp.exp(a_log))
    a_cum = _cumsum_rows(da)
    a_cum_t = a_cum.T
    dt_t = dt.T
    r = lax.broadcasted_iota(jnp.int32, (CHUNK, CHUNK), 0)
    c = lax.broadcasted_iota(jnp.int32, (CHUNK, CHUNK), 1)
    causal = r >= c
    lane_lo = lax.broadcasted_iota(jnp.int32, (1, LANES), 1) < SSM_HEADDIM
    last_row = lax.broadcasted_iota(jnp.int32, (CHUNK, 1), 0) == CHUNK - 1
    ys, h_next = [], []
    for j in range(SSM_HEADS // 2):
        g = j // 2
        xs = xbc[:, j * LANES:(j + 1) * LANES]
        bm = xbc[:, 1024 + g * SSM_STATE:1024 + (g + 1) * SSM_STATE]
        cm = xbc[:, 1536 + g * SSM_STATE:1536 + (g + 1) * SSM_STATE]
        cb = _nt(cm, bm)
        y_diag, to_end, e_cum, c_dec, d_row = [], [], [], [], []
        for h in (2 * j, 2 * j + 1):
            col = _col_pick(a_cum, h)
            row = _row_pick(a_cum_t, h)
            dt_col = _col_pick(dt, h)
            dt_row = _row_pick(dt_t, h)
            decay = jnp.exp(jnp.where(causal, col - row, -jnp.inf))
            y_diag.append(_nn(cb * decay * dt_row, xs))
            last = jnp.sum(jnp.where(last_row, col, 0.0), axis=0, keepdims=True)
            to_end.append(jnp.exp(last - col) * dt_col)
            e_cum.append(jnp.exp(col))
            c_dec.append(jnp.exp(last))
            d_row.append(_lane_pick(d_skip, h))
        pair = lambda lo_hi: jnp.where(lane_lo, lo_hi[0], lo_hi[1])
        states = _tn(bm, xs * pair(to_end))
        y_off = _nn(cm, h_prev[j]) * pair(e_cum)
        ys.append(pair(y_diag) + y_off + xs * pair(d_row))
        h_next.append(pair(c_dec) * h_prev[j] + states)
    y = jnp.concatenate(ys, axis=1) * _silu(z)
    width = D_MODEL // SSM_GROUPS
    y = jnp.concatenate(
        [_rmsnorm(y[:, g * width:(g + 1) * width], norm_g[:, g * width:(g + 1) * width]) for g in range(SSM_GROUPS)],
        axis=1)
    return y, jnp.stack(h_next)


def _shift_down(prev8, x, k):
    if k == 0:
        return x
    win = jnp.concatenate([prev8, x], axis=0)
    return pltpu.roll(win, k, 0)[HALO:]


def _shift_up(x, next8, k):
    if k == 0:
        return x
    n = x.shape[0]
    win = jnp.concatenate([x, next8], axis=0)
    return pltpu.roll(win, n + HALO - k, 0)[:n]


def _conv_pre(prev8, x, w, b):
    out = b + x * w[SSM_CONV - 1:SSM_CONV]
    for i in range(SSM_CONV - 1):
        out = out + _shift_down(prev8, x, SSM_CONV - 1 - i) * w[i:i + 1]
    return out


def _attn_block(q, k_prev, k_cur, v_prev, v_cur, sink, first):
    k = jnp.concatenate([k_prev, k_cur], axis=0)
    v = jnp.concatenate([v_prev, v_cur], axis=0)
    s = _nt(q, k) * (HEAD_DIM ** -0.5)
    rows = lax.broadcasted_iota(jnp.int32, s.shape, 0) & (CHUNK - 1)
    cols = lax.broadcasted_iota(jnp.int32, s.shape, 1)
    valid = (cols <= rows + CHUNK) & (cols > rows) & (cols >= CHUNK * first.astype(jnp.int32))
    s = jnp.where(valid, s, -jnp.inf)
    m = jnp.maximum(jnp.max(s, axis=-1, keepdims=True), sink)
    p = jnp.exp(s - m)
    denom = jnp.sum(p, axis=-1, keepdims=True) + jnp.exp(sink - m)
    return _nn(p / denom, v)


def _matmul(a, b, mode, *, name, outs, extras=(), epilogue=None, tm=1024, tn=1024, tk=1024):
    if mode == "nn":
        (m, k), (k2, n) = a.shape, b.shape
    elif mode == "nt":
        (m, k), (n, k2) = a.shape, b.shape
    else:
        (k, m), (k2, n) = a.shape, b.shape
    assert k == k2, (a.shape, b.shape, mode)
    tm, tn, tk = min(tm, m), min(tn, n), min(tk, k)
    assert m % tm == 0 and n % tn == 0 and k % tk == 0, (m, n, k, tm, tn, tk)
    nk = k // tk
    dims = {"nn": _NN, "nt": _NT, "tn": _TN}[mode]
    if mode == "tn":
        a_spec = pl.BlockSpec((tk, tm), lambda i, j, kk: (kk, i))
    else:
        a_spec = pl.BlockSpec((tm, tk), lambda i, j, kk: (i, kk))
    if mode == "nt":
        b_spec = pl.BlockSpec((tn, tk), lambda i, j, kk: (j, kk))
    else:
        b_spec = pl.BlockSpec((tk, tn), lambda i, j, kk: (kk, j))
    tile_spec = pl.BlockSpec((tm, tn), lambda i, j, kk: (i, j))
    row_spec = pl.BlockSpec((1, tn), lambda i, j, kk: (0, j))
    extra_specs = [tile_spec if kind == "tile" else row_spec for _, kind in extras]
    n_extra, n_out = len(extras), len(outs)
    if epilogue is None:
        epilogue = lambda acc: (acc,)

    def body(a_ref, b_ref, *rest):
        extra_refs, out_refs, acc_ref = rest[:n_extra], rest[n_extra:n_extra + n_out], rest[-1]
        kk = pl.program_id(2)

        @pl.when(kk == 0)
        def _():
            acc_ref[...] = jnp.zeros_like(acc_ref)

        acc_ref[...] += lax.dot_general(a_ref[...].astype(BF16), b_ref[...].astype(BF16), dims,
                                        preferred_element_type=F32)

        @pl.when(kk == nk - 1)
        def _():
            res = epilogue(acc_ref[...], *[e[...] for e in extra_refs])
            for val, o_ref in zip(res, out_refs):
                o_ref[...] = val.astype(o_ref.dtype)

    res = pl.pallas_call(
        body,
        name=name,
        grid=(m // tm, n // tn, nk),
        in_specs=[a_spec, b_spec] + extra_specs,
        out_specs=[tile_spec] * n_out,
        out_shape=[jax.ShapeDtypeStruct((m, n), dt) for dt in outs],
        scratch_shapes=[pltpu.VMEM((tm, tn), F32)],
        compiler_params=_params("parallel", "parallel", "arbitrary"),
    )(a, b, *[e for e, _ in extras])
    return res[0] if n_out == 1 else res


def _rms_fwd(h, g, *, name, tb=512):
    t, d = h.shape

    def body(h_ref, g_ref, y_ref):
        y_ref[...] = _rmsnorm(h_ref[...], g_ref[...]).astype(y_ref.dtype)

    return pl.pallas_call(
        body, name=name, grid=(t // tb,),
        in_specs=[pl.BlockSpec((tb, d), lambda i: (i, 0)), pl.BlockSpec((1, d), lambda i: (0, 0))],
        out_specs=pl.BlockSpec((tb, d), lambda i: (i, 0)),
        out_shape=jax.ShapeDtypeStruct((t, d), BF16),
        compiler_params=_params("parallel"),
    )(h, g)


def _rms_bwd(dy, h, g, dres, *, name, tb=512):
    t, d = h.shape

    def body(dy_ref, h_ref, g_ref, dres_ref, dh_ref, dg_ref, cs_ref):
        _, vjp = jax.vjp(_rmsnorm, h_ref[...], g_ref[...])
        dh, dg = vjp(dy_ref[...])
        dh = dh + dres_ref[...]
        dh_ref[...] = dh

        @pl.when(pl.program_id(0) == 0)
        def _():
            dg_ref[...] = jnp.zeros_like(dg_ref)
            cs_ref[...] = jnp.zeros_like(cs_ref)

        dg_ref[...] += dg
        cs_ref[...] += jnp.sum(dh, axis=0, keepdims=True)

    blk = pl.BlockSpec((tb, d), lambda i: (i, 0))
    row = pl.BlockSpec((1, d), lambda i: (0, 0))
    return pl.pallas_call(
        body, name=name, grid=(t // tb,),
        in_specs=[blk, blk, row, blk],
        out_specs=[blk, row, row],
        out_shape=[jax.ShapeDtypeStruct((t, d), F32), jax.ShapeDtypeStruct((1, d), F32),
                   jax.ShapeDtypeStruct((1, d), F32)],
        compiler_params=_params("arbitrary"),
    )(dy, h, g, dres)


def _colsum(a, *, name, tb=512):
    t, d = a.shape

    def body(a_ref, o_ref):
        @pl.when(pl.program_id(0) == 0)
        def _():
            o_ref[...] = jnp.zeros_like(o_ref)

        o_ref[...] += jnp.sum(a_ref[...].astype(F32), axis=0, keepdims=True)

    return pl.pallas_call(
        body, name=name, grid=(t // tb,),
        in_specs=[pl.BlockSpec((tb, d), lambda i: (i, 0))],
        out_specs=pl.BlockSpec((1, d), lambda i: (0, 0)),
        out_shape=jax.ShapeDtypeStruct((1, d), F32),
        compiler_params=_params("arbitrary"),
    )(a)


def _final_loss(h, g, target, *, name, tb=512):
    t, d = h.shape

    def body(h_ref, g_ref, tgt_ref, loss_ref, dh_ref, dg_ref):
        def f(hh, gg):
            err = jnp.square(_rmsnorm(hh, gg) - tgt_ref[...])
            return 0.5 * jnp.sum(jnp.mean(err, axis=-1, keepdims=True), axis=0, keepdims=True)

        val, vjp = jax.vjp(f, h_ref[...], g_ref[...])
        dh, dg = vjp(jnp.ones((1, 1), F32))
        dh_ref[...] = dh

        @pl.when(pl.program_id(0) == 0)
        def _():
            loss_ref[...] = jnp.zeros_like(loss_ref)
            dg_ref[...] = jnp.zeros_like(dg_ref)

        loss_ref[...] += val
        dg_ref[...] += dg

    blk = pl.BlockSpec((tb, d), lambda i: (i, 0))
    row = pl.BlockSpec((1, d), lambda i: (0, 0))
    return pl.pallas_call(
        body, name=name, grid=(t // tb,),
        in_specs=[blk, row, blk],
        out_specs=[pl.BlockSpec((8, LANES), lambda i: (0, 0)), blk, row],
        out_shape=[jax.ShapeDtypeStruct((8, LANES), F32), jax.ShapeDtypeStruct((t, d), F32),
                   jax.ShapeDtypeStruct((1, d), F32)],
        compiler_params=_params("arbitrary"),
    )(h, g, target)


def _gmlp_fwd(proj_uv, ln_g, ln_b, w_s, b_s, *, name):
    t = proj_uv.shape[0]
    w = D_MODEL

    def body(u_ref, v_ref, g_ref, b_ref, w_ref, bs_ref, o_ref):
        o_ref[...] = _gmlp_chunk(u_ref[...], v_ref[...], g_ref[...], b_ref[...], w_ref[...],
                                 bs_ref[...]).astype(o_ref.dtype)

    row = pl.BlockSpec((1, w), lambda i: (0, 0))
    return pl.pallas_call(
        body, name=name, grid=(t // CHUNK,),
        in_specs=[pl.BlockSpec((CHUNK, w), lambda i: (i, 0)), pl.BlockSpec((CHUNK, w), lambda i: (i, 1)), row, row,
                  pl.BlockSpec((GM_GROUPS, CHUNK, CHUNK), lambda i: (0, 0, 0)),
                  pl.BlockSpec((GM_GROUPS, CHUNK, 1), lambda i: (0, 0, 0))],
        out_specs=pl.BlockSpec((CHUNK, w), lambda i: (i, 0)),
        out_shape=jax.ShapeDtypeStruct((t, w), BF16),
        compiler_params=_params("parallel"),
    )(proj_uv, proj_uv, ln_g, ln_b, w_s, b_s)


def _gmlp_bwd(proj_uv, d_mix, ln_g, ln_b, w_s, b_s, *, name):
    t = proj_uv.shape[0]
    w = D_MODEL

    def body(u_ref, v_ref, da_ref, g_ref, b_ref, w_ref, bs_ref, duv_ref, dg_ref, db_ref, dw_ref, dbs_ref):
        _, vjp = jax.vjp(_gmlp_chunk, u_ref[...], v_ref[...], g_ref[...], b_ref[...], w_ref[...], bs_ref[...])
        du, dv, dg, db, dw, dbs = vjp(da_ref[...])
        duv_ref[:, :w] = du.astype(duv_ref.dtype)
        duv_ref[:, w:] = dv.astype(duv_ref.dtype)

        @pl.when(pl.program_id(0) == 0)
        def _():
            dg_ref[...] = jnp.zeros_like(dg_ref)
            db_ref[...] = jnp.zeros_like(db_ref)
            dw_ref[...] = jnp.zeros_like(dw_ref)
            dbs_ref[...] = jnp.zeros_like(dbs_ref)

        dg_ref[...] += dg
        db_ref[...] += db
        dw_ref[...] += dw
        dbs_ref[...] += dbs

    row = pl.BlockSpec((1, w), lambda i: (0, 0))
    ws = pl.BlockSpec((GM_GROUPS, CHUNK, CHUNK), lambda i: (0, 0, 0))
    bs = pl.BlockSpec((GM_GROUPS, CHUNK, 1), lambda i: (0, 0, 0))
    return pl.pallas_call(
        body, name=name, grid=(t // CHUNK,),
        in_specs=[pl.BlockSpec((CHUNK, w), lambda i: (i, 0)), pl.BlockSpec((CHUNK, w), lambda i: (i, 1)),
                  pl.BlockSpec((CHUNK, w), lambda i: (i, 0)), row, row, ws, bs],
        out_specs=[pl.BlockSpec((CHUNK, 2 * w), lambda i: (i, 0)), row, row, ws, bs],
        out_shape=[jax.ShapeDtypeStruct((t, 2 * w), BF16), jax.ShapeDtypeStruct((1, w), F32),
                   jax.ShapeDtypeStruct((1, w), F32), jax.ShapeDtypeStruct((GM_GROUPS, CHUNK, CHUNK), F32),
                   jax.ShapeDtypeStruct((GM_GROUPS, CHUNK, 1), F32)],
        compiler_params=_params("arbitrary"),
    )(proj_uv, proj_uv, d_mix, ln_g, ln_b, w_s, b_s)


_HALO_PER_CHUNK = CHUNK // HALO
_DT_BLOCK = (CONV_DIM + D_MODEL) // LANES


def _ssd_fwd(proj_rest, conv_w, conv_b, dt_bias, a_log, d_skip, norm_g, *, name):
    t = proj_rest.shape[0]
    nc = t // CHUNK
    pairs = SSM_HEADS // 2

    def body(x_ref, prev_ref, z_ref, dt_ref, cw_ref, cb_ref, dtb_ref, al_ref, ds_ref, ng_ref, y_ref, hs_ref, h_scr):
        i = pl.program_id(0)

        @pl.when(i == 0)
        def _():
            h_scr[...] = jnp.zeros_like(h_scr)

        prev8 = jnp.where(i == 0, 0.0, prev_ref[...])
        pre = _conv_pre(prev8, x_ref[...], cw_ref[...], cb_ref[...])
        h_prev = h_scr[...]
        hs_ref[0] = h_prev
        y, h_next = _ssd_chunk(pre, z_ref[...], dt_ref[...], h_prev, dtb_ref[...], al_ref[...], ds_ref[...],
                               ng_ref[...])
        y_ref[...] = y.astype(y_ref.dtype)
        h_scr[...] = h_next

    small = pl.BlockSpec((1, LANES), lambda i: (0, 0))
    return pl.pallas_call(
        body, name=name, grid=(nc,),
        in_specs=[pl.BlockSpec((CHUNK, CONV_DIM), lambda i: (i, 0)),
                  pl.BlockSpec((HALO, CONV_DIM), lambda i: (jnp.maximum(i * _HALO_PER_CHUNK - 1, 0), 0)),
                  pl.BlockSpec((CHUNK, D_MODEL), lambda i: (i, CONV_DIM // D_MODEL)),
                  pl.BlockSpec((CHUNK, LANES), lambda i: (i, _DT_BLOCK)),
                  pl.BlockSpec((SSM_CONV, CONV_DIM), lambda i: (0, 0)),
                  pl.BlockSpec((1, CONV_DIM), lambda i: (0, 0)),
                  small, small, small, pl.BlockSpec((1, D_MODEL), lambda i: (0, 0))],
        out_specs=[pl.BlockSpec((CHUNK, D_MODEL), lambda i: (i, 0)),
                   pl.BlockSpec((1, pairs, SSM_STATE, LANES), lambda i: (i, 0, 0, 0))],
        out_shape=[jax.ShapeDtypeStruct((t, D_MODEL), BF16),
                   jax.ShapeDtypeStruct((nc, pairs, SSM_STATE, LANES), F32)],
        scratch_shapes=[pltpu.VMEM((pairs, SSM_STATE, LANES), F32)],
        compiler_params=_params("arbitrary"),
    )(proj_rest, proj_rest, proj_rest, proj_rest, conv_w, conv_b, dt_bias, a_log, d_skip, norm_g)


def _ssd_bwd(proj_rest, h_states, d_mix, conv_w, conv_b, dt_bias, a_log, d_skip, norm_g, *, name):
    t = proj_rest.shape[0]
    nc = t // CHUNK
    pairs = SSM_HEADS // 2

    def body(x_ref, prev_ref, z_ref, dt_ref, hs_ref, dy_ref, cw_ref, cb_ref, dtb_ref, al_ref, ds_ref, ng_ref,
             dpre_ref, dz_ref, ddt_ref, ddtb_ref, dal_ref, dds_ref, dng_ref, dh_scr):
        i = pl.program_id(0)
        chunk = nc - 1 - i

        @pl.when(i == 0)
        def _():
            dh_scr[...] = jnp.zeros_like(dh_scr)
            ddtb_ref[...] = jnp.zeros_like(ddtb_ref)
            dal_ref[...] = jnp.zeros_like(dal_ref)
            dds_ref[...] = jnp.zeros_like(dds_ref)
            dng_ref[...] = jnp.zeros_like(dng_ref)

        prev8 = jnp.where(chunk == 0, 0.0, prev_ref[...])
        pre = _conv_pre(prev8, x_ref[...], cw_ref[...], cb_ref[...])
        _, vjp = jax.vjp(_ssd_chunk, pre, z_ref[...], dt_ref[...], hs_ref[0], dtb_ref[...], al_ref[...],
                         ds_ref[...], ng_ref[...])
        dpre, dz, ddt, dh_prev, ddtb, dal, dds, dng = vjp((dy_ref[...], dh_scr[...]))
        dpre_ref[...] = dpre
        dz_ref[...] = dz
        ddt_ref[...] = ddt
        dh_scr[...] = dh_prev
        ddtb_ref[...] += ddtb
        dal_ref[...] += dal
        dds_ref[...] += dds
        dng_ref[...] += dng

    rev = lambda i: nc - 1 - i
    small = pl.BlockSpec((1, LANES), lambda i: (0, 0))
    wide = pl.BlockSpec((1, D_MODEL), lambda i: (0, 0))
    return pl.pallas_call(
        body, name=name, grid=(nc,),
        in_specs=[pl.BlockSpec((CHUNK, CONV_DIM), lambda i: (rev(i), 0)),
                  pl.BlockSpec((HALO, CONV_DIM), lambda i: (jnp.maximum(rev(i) * _HALO_PER_CHUNK - 1, 0), 0)),
                  pl.BlockSpec((CHUNK, D_MODEL), lambda i: (rev(i), CONV_DIM // D_MODEL)),
                  pl.BlockSpec((CHUNK, LANES), lambda i: (rev(i), _DT_BLOCK)),
                  pl.BlockSpec((1, pairs, SSM_STATE, LANES), lambda i: (rev(i), 0, 0, 0)),
                  pl.BlockSpec((CHUNK, D_MODEL), lambda i: (rev(i), 1)),
                  pl.BlockSpec((SSM_CONV, CONV_DIM), lambda i: (0, 0)),
                  pl.BlockSpec((1, CONV_DIM), lambda i: (0, 0)),
                  small, small, small, wide],
        out_specs=[pl.BlockSpec((CHUNK, CONV_DIM), lambda i: (rev(i), 0)),
                   pl.BlockSpec((CHUNK, D_MODEL), lambda i: (rev(i), 0)),
                   pl.BlockSpec((CHUNK, LANES), lambda i: (rev(i), 0)),
                   small, small, small, wide],
        out_shape=[jax.ShapeDtypeStruct((t, CONV_DIM), F32), jax.ShapeDtypeStruct((t, D_MODEL), F32),
                   jax.ShapeDtypeStruct((t, LANES), F32),
                   jax.ShapeDtypeStruct((1, LANES), F32), jax.ShapeDtypeStruct((1, LANES), F32),
                   jax.ShapeDtypeStruct((1, LANES), F32), jax.ShapeDtypeStruct((1, D_MODEL), F32)],
        scratch_shapes=[pltpu.VMEM((pairs, SSM_STATE, LANES), F32)],
        compiler_params=_params("arbitrary"),
    )(proj_rest, proj_rest, proj_rest, proj_rest, h_states, d_mix, conv_w, conv_b, dt_bias, a_log, d_skip, norm_g)


def _conv_bwd(proj_rest, dpre, dz, ddt, conv_w, *, name, tb=256):
    t = proj_rest.shape[0]
    nb = t // tb
    per = tb // HALO

    def body(x_ref, prev_ref, dpre_ref, next_ref, dz_ref, ddt_ref, cw_ref, drest_ref, dcw_ref, dcb_ref):
        i = pl.program_id(0)

        @pl.when(i == 0)
        def _():
            dcw_ref[...] = jnp.zeros_like(dcw_ref)
            dcb_ref[...] = jnp.zeros_like(dcb_ref)

        x = x_ref[...]
        dp = dpre_ref[...]
        w = cw_ref[...]
        prev8 = jnp.where(i == 0, 0.0, prev_ref[...])
        next8 = jnp.where(i == nb - 1, 0.0, next_ref[...])
        dx = dp * w[SSM_CONV - 1:SSM_CONV]
        for j in range(SSM_CONV - 1):
            dx = dx + _shift_up(dp, next8, SSM_CONV - 1 - j) * w[j:j + 1]
        drest_ref[:, :CONV_DIM] = dx.astype(drest_ref.dtype)
        drest_ref[:, CONV_DIM:CONV_DIM + D_MODEL] = dz_ref[...].astype(drest_ref.dtype)
        drest_ref[:, CONV_DIM + D_MODEL:] = ddt_ref[...].astype(drest_ref.dtype)
        for j in range(SSM_CONV):
            dcw_ref[j:j + 1, :] += jnp.sum(dp * _shift_down(prev8, x, SSM_CONV - 1 - j), axis=0, keepdims=True)
        dcb_ref[...] += jnp.sum(dp, axis=0, keepdims=True)

    return pl.pallas_call(
        body, name=name, grid=(nb,),
        in_specs=[pl.BlockSpec((tb, CONV_DIM), lambda i: (i, 0)),
                  pl.BlockSpec((HALO, CONV_DIM), lambda i: (jnp.maximum(i * per - 1, 0), 0)),
                  pl.BlockSpec((tb, CONV_DIM), lambda i: (i, 0)),
                  pl.BlockSpec((HALO, CONV_DIM), lambda i: (jnp.minimum((i + 1) * per, nb * per - 1), 0)),
                  pl.BlockSpec((tb, D_MODEL), lambda i: (i, 0)),
                  pl.BlockSpec((tb, LANES), lambda i: (i, 0)),
                  pl.BlockSpec((SSM_CONV, CONV_DIM), lambda i: (0, 0))],
        out_specs=[pl.BlockSpec((tb, REST_W), lambda i: (i, 0)),
                   pl.BlockSpec((SSM_CONV, CONV_DIM), lambda i: (0, 0)),
                   pl.BlockSpec((1, CONV_DIM), lambda i: (0, 0))],
        out_shape=[jax.ShapeDtypeStruct((t, REST_W), BF16), jax.ShapeDtypeStruct((SSM_CONV, CONV_DIM), F32),
                   jax.ShapeDtypeStruct((1, CONV_DIM), F32)],
        compiler_params=_params("arbitrary"),
    )(proj_rest, proj_rest, dpre, dpre, dz, ddt, conv_w)


_Q_PER_KV = ATTN_HEADS // ATTN_KV


def _attn_fwd(q, k, v, sink_col, *, name):
    t = q.shape[1]
    nb = t // CHUNK

    def body(q_ref, kp_ref, kc_ref, vp_ref, vc_ref, s_ref, o_ref):
        first = pl.program_id(0) == 0
        for j in range(ATTN_KV):
            qj = q_ref[j * _Q_PER_KV:(j + 1) * _Q_PER_KV].reshape(_Q_PER_KV * CHUNK, HEAD_DIM)
            o = _attn_block(qj, kp_ref[j], kc_ref[j], vp_ref[j], vc_ref[j], s_ref[j], first)
            o_ref[j * _Q_PER_KV:(j + 1) * _Q_PER_KV] = o.reshape(_Q_PER_KV, CHUNK, HEAD_DIM).astype(o_ref.dtype)

    cur = lambda i: (0, i, 0)
    prev = lambda i: (0, jnp.maximum(i - 1, 0), 0)
    kv = (ATTN_KV, CHUNK, HEAD_DIM)
    return pl.pallas_call(
        body, name=name, grid=(nb,),
        in_specs=[pl.BlockSpec((ATTN_HEADS, CHUNK, HEAD_DIM), cur), pl.BlockSpec(kv, prev), pl.BlockSpec(kv, cur),
                  pl.BlockSpec(kv, prev), pl.BlockSpec(kv, cur),
                  pl.BlockSpec((ATTN_KV, _Q_PER_KV * CHUNK, 1), lambda i: (0, 0, 0))],
        out_specs=pl.BlockSpec((ATTN_HEADS, CHUNK, HEAD_DIM), cur),
        out_shape=jax.ShapeDtypeStruct((ATTN_HEADS, t, HEAD_DIM), BF16),
        compiler_params=_params("parallel"),
    )(q, k, k, v, v, sink_col)


def _attn_bwd(q, k, v, sink_col, d_o, *, name):
    t = q.shape[1]
    nb = t // CHUNK

    def body(q_ref, kp_ref, kc_ref, vp_ref, vc_ref, s_ref, do_ref, dq_ref, dk_ref, dv_ref, ds_ref, dk_scr, dv_scr):
        i = pl.program_id(0)
        first = i == nb - 1

        @pl.when(i == 0)
        def _():
            dk_scr[...] = jnp.zeros_like(dk_scr)
            dv_scr[...] = jnp.zeros_like(dv_scr)
            ds_ref[...] = jnp.zeros_like(ds_ref)

        for j in range(ATTN_KV):
            heads = slice(j * _Q_PER_KV, (j + 1) * _Q_PER_KV)
            qj = q_ref[heads].reshape(_Q_PER_KV * CHUNK, HEAD_DIM)
            doj = do_ref[heads].reshape(_Q_PER_KV * CHUNK, HEAD_DIM)
            _, vjp = jax.vjp(functools.partial(_attn_block, first=first), qj, kp_ref[j], kc_ref[j], vp_ref[j],
                             vc_ref[j], s_ref[j])
            dq, dkp, dkc, dvp, dvc, dsink = vjp(doj)
            dq_ref[heads] = dq.reshape(_Q_PER_KV, CHUNK, HEAD_DIM)
            dk_ref[j] = dkc + dk_scr[j]
            dv_ref[j] = dvc + dv_scr[j]
            dk_scr[j] = dkp
            dv_scr[j] = dvp
            ds_ref[j] += dsink

    cur = lambda i: (0, nb - 1 - i, 0)
    prev = lambda i: (0, jnp.maximum(nb - 2 - i, 0), 0)
    kv = (ATTN_KV, CHUNK, HEAD_DIM)
    qs = (ATTN_HEADS, CHUNK, HEAD_DIM)
    sk = pl.BlockSpec((ATTN_KV, _Q_PER_KV * CHUNK, 1), lambda i: (0, 0, 0))
    return pl.pallas_call(
        body, name=name, grid=(nb,),
        in_specs=[pl.BlockSpec(qs, cur), pl.BlockSpec(kv, prev), pl.BlockSpec(kv, cur), pl.BlockSpec(kv, prev),
                  pl.BlockSpec(kv, cur), sk, pl.BlockSpec(qs, cur)],
        out_specs=[pl.BlockSpec(qs, cur), pl.BlockSpec(kv, cur), pl.BlockSpec(kv, cur), sk],
        out_shape=[jax.ShapeDtypeStruct((ATTN_HEADS, t, HEAD_DIM), F32), jax.ShapeDtypeStruct((ATTN_KV, t, HEAD_DIM), F32),
                   jax.ShapeDtypeStruct((ATTN_KV, t, HEAD_DIM), F32),
                   jax.ShapeDtypeStruct((ATTN_KV, _Q_PER_KV * CHUNK, 1), F32)],
        scratch_shapes=[pltpu.VMEM(kv, F32), pltpu.VMEM(kv, F32)],
        compiler_params=_params("arbitrary"),
    )(q, k, k, v, v, sink_col, d_o)


def _adamw(parts, w, m, v, *, name, tb=256):
    n, r, c = parts.shape
    tb = min(tb, r)
    assert r % tb == 0, (r, tb)

    def body(p_ref, w_ref, m_ref, v_ref, g_ref, d_ref, nm_ref, nv_ref):
        g = p_ref[0].astype(F32)
        for s in range(1, n):
            g = g + p_ref[s].astype(F32)
        m_new = ADAM_B1 * m_ref[...] + (1.0 - ADAM_B1) * g
        v_new = ADAM_B2 * v_ref[...] + (1.0 - ADAM_B2) * jnp.square(g)
        m_hat = m_new / (1.0 - ADAM_B1 ** ADAM_STEP)
        v_hat = v_new / (1.0 - ADAM_B2 ** ADAM_STEP)
        g_ref[...] = g
        d_ref[...] = -ADAM_LR * (m_hat / (jnp.sqrt(v_hat) + ADAM_EPS) + ADAM_WD * w_ref[...])
        nm_ref[...] = m_new
        nv_ref[...] = v_new

    blk = pl.BlockSpec((tb, c), lambda i: (i, 0))
    return pl.pallas_call(
        body, name=name, grid=(r // tb,),
        in_specs=[pl.BlockSpec((n, tb, c), lambda i: (0, i, 0)), blk, blk, blk],
        out_specs=[blk] * 4,
        out_shape=[jax.ShapeDtypeStruct((r, c), F32)] * 4,
        compiler_params=_params("parallel"),
    )(parts, w, m, v)


def _peer(x, y, c, k):
    return (1 - x if k & 4 else x, 1 - y if k & 2 else y, 1 - c if k & 1 else c)


def _all_gather(shard, *, name):
    hbm = pl.BlockSpec(memory_space=pltpu.HBM)

    def body(x_ref, out_ref, send_sems, recv_sems, local_sem):
        x, y, c = lax.axis_index("x"), lax.axis_index("y"), lax.axis_index("c")
        me = 4 * x + 2 * y + c
        mine = pltpu.make_async_copy(x_ref, out_ref.at[me], local_sem)
        mine.start()
        copies = [
            pltpu.make_async_remote_copy(src_ref=x_ref, dst_ref=out_ref.at[me], send_sem=send_sems.at[k - 1],
                                         recv_sem=recv_sems.at[k - 1], device_id=_peer(x, y, c, k),
                                         device_id_type=pl.DeviceIdType.MESH)
            for k in range(1, N_DEV)]
        for cp in copies:
            cp.start()
        for cp in copies:
            cp.wait()
        mine.wait()

    return pl.pallas_call(
        body, name=name,
        in_specs=[hbm], out_specs=hbm,
        out_shape=jax.ShapeDtypeStruct((N_DEV,) + shard.shape, shard.dtype),
        scratch_shapes=[pltpu.SemaphoreType.DMA((N_DEV - 1,)), pltpu.SemaphoreType.DMA((N_DEV - 1,)),
                        pltpu.SemaphoreType.DMA],
    )(shard)


def _all_to_all(rows, *, name):
    hbm = pl.BlockSpec(memory_space=pltpu.HBM)

    def body(x_ref, out_ref, send_sems, recv_sems, local_sem):
        x, y, c = lax.axis_index("x"), lax.axis_index("y"), lax.axis_index("c")
        me = 4 * x + 2 * y + c
        mine = pltpu.make_async_copy(x_ref.at[me], out_ref.at[me], local_sem)
        mine.start()
        copies = []
        for k in range(1, N_DEV):
            px, py, pc = _peer(x, y, c, k)
            copies.append(pltpu.make_async_remote_copy(
                src_ref=x_ref.at[4 * px + 2 * py + pc], dst_ref=out_ref.at[me], send_sem=send_sems.at[k - 1],
                recv_sem=recv_sems.at[k - 1], device_id=(px, py, pc), device_id_type=pl.DeviceIdType.MESH))
        for cp in copies:
            cp.start()
        for cp in copies:
            cp.wait()
        mine.wait()

    return pl.pallas_call(
        body, name=name,
        in_specs=[hbm], out_specs=hbm,
        out_shape=jax.ShapeDtypeStruct(rows.shape, rows.dtype),
        scratch_shapes=[pltpu.SemaphoreType.DMA((N_DEV - 1,)), pltpu.SemaphoreType.DMA((N_DEV - 1,)),
                        pltpu.SemaphoreType.DMA],
    )(rows)


PACK_W = 1024


def _pad_rows(a, mult):
    r = (-a.shape[0]) % mult
    return jnp.pad(a, ((0, r), (0, 0))) if r else a


def _as_rows(a):
    flat = a.reshape(-1)
    pad = (-flat.shape[0]) % PACK_W
    if pad:
        flat = jnp.pad(flat, (0, pad))
    return flat.reshape(-1, PACK_W)


def _f32_as_bf16_rows(a):
    return _as_rows(lax.bitcast_convert_type(a.reshape(-1), BF16))


def _bf16_rows_as_f32(rows, n):
    return lax.bitcast_convert_type(rows.reshape(-1, 2), F32)[:n]


def _cols_from_shards(g):
    return jnp.transpose(g, (1, 0, 2)).reshape(g.shape[1], -1)


def _cols_to_shards(a):
    return jnp.transpose(a.reshape(a.shape[0], N_DEV, -1), (1, 0, 2))


def _pad_lanes(a):
    return jnp.pad(a, ((0, 0), (0, LANES - a.shape[1])))


def kernel(x, norm_mix_g, norm_mlp_g, final_norm_g, w_in_even, w_out_even, gm_ln_g, gm_ln_b, gm_w_s, gm_b_s, ssm_conv_w, ssm_conv_b, ssm_dt_bias, ssm_a_log, ssm_d, ssm_norm_g, w_qkv, b_qkv, w_o, b_o, attn_sinks, w_up, w_down, loss_target, m_norm_mix_g, m_norm_mlp_g, m_final_norm_g, m_w_in_even, m_w_out_even, m_gm_ln_g, m_gm_ln_b, m_gm_w_s, m_gm_b_s, m_ssm_conv_w, m_ssm_conv_b, m_ssm_dt_bias, m_ssm_a_log, m_ssm_d, m_ssm_norm_g, m_w_qkv, m_b_qkv, m_w_o, m_b_o, m_attn_sinks, m_w_up, m_w_down, v_norm_mix_g, v_norm_mlp_g, v_final_norm_g, v_w_in_even, v_w_out_even, v_gm_ln_g, v_gm_ln_b, v_gm_w_s, v_gm_b_s, v_ssm_conv_w, v_ssm_conv_b, v_ssm_dt_bias, v_ssm_a_log, v_ssm_d, v_ssm_norm_g, v_w_qkv, v_b_qkv, v_w_o, v_b_o, v_attn_sinks, v_w_up, v_w_down):
    names = ["norm_mix_g", "norm_mlp_g", "final_norm_g", "w_in_even", "w_out_even", "gm_ln_g", "gm_ln_b", "gm_w_s",
             "gm_b_s", "ssm_conv_w", "ssm_conv_b", "ssm_dt_bias", "ssm_a_log", "ssm_d", "ssm_norm_g", "w_qkv",
             "b_qkv", "w_o", "b_o", "attn_sinks", "w_up", "w_down"]
    env = locals()
    W = {n: env[n] for n in names}
    M = {n: env["m_" + n] for n in names}
    V = {n: env["v_" + n] for n in names}
    big = ["w_in_even", "w_out_even", "w_qkv", "w_o", "w_up", "w_down"]
    small_sharded = ["ssm_conv_w", "b_qkv", "b_o"]
    replicated = [n for n in names if n not in big and n not in small_sharded]
    me = 4 * lax.axis_index("x") + 2 * lax.axis_index("y") + lax.axis_index("c")
    t = x.shape[1]
    xs = x.reshape(t, D_MODEL)
    target = loss_target.reshape(t, D_MODEL)

    big_rows = [_as_rows(W[n].astype(BF16)) for n in big]
    small_flat = jnp.concatenate([W[n].reshape(-1) for n in small_sharded])
    n_small = small_flat.shape[0]
    shard_rows = _pad_rows(jnp.concatenate(big_rows + [_f32_as_bf16_rows(small_flat)], axis=0), 16)
    gathered = _all_gather(shard_rows, name="gather_weights")
    offs, o = {}, 0
    for n, rws in zip(big, big_rows):
        offs[n] = (o, rws.shape[0])
        o += rws.shape[0]
    take = lambda n, shape: gathered[:, offs[n][0]:offs[n][0] + offs[n][1]].reshape((N_DEV,) + shape)
    w_in = _cols_from_shards(take("w_in_even", (D_MODEL, IN_EVEN // N_DEV)))
    w_uv = w_in[:, :2 * D_MODEL]
    w_rest = jnp.concatenate([w_in[:, 3 * D_MODEL:3 * D_MODEL + CONV_DIM], w_in[:, 2 * D_MODEL:3 * D_MODEL],
                              w_in[:, 3 * D_MODEL + CONV_DIM:],
                              jnp.zeros((D_MODEL, LANES - SSM_HEADS), BF16)], axis=1)
    w_out = take("w_out_even", (2 * D_MODEL // N_DEV, D_MODEL)).reshape(2 * D_MODEL, D_MODEL)
    wqkv = _cols_from_shards(take("w_qkv", (D_MODEL, QKV_DIM // N_DEV)))
    wo = take("w_o", (D_MODEL // N_DEV, D_MODEL)).reshape(D_MODEL, D_MODEL)
    wup = take("w_up", (2, D_MODEL, D_FF // N_DEV))
    wup = [_cols_from_shards(wup[:, l]) for l in range(2)]
    wdown = take("w_down", (2, D_FF // N_DEV, D_MODEL))
    wdown = [wdown[:, l].reshape(D_FF, D_MODEL) for l in range(2)]
    small_all = jnp.stack([_bf16_rows_as_f32(gathered[d, o:], n_small) for d in range(N_DEV)])
    n_cw = SSM_CONV * CONV_DIM // N_DEV
    n_bq = QKV_DIM // N_DEV
    conv_w = _cols_from_shards(small_all[:, :n_cw].reshape(N_DEV, SSM_CONV, CONV_DIM // N_DEV))
    bqkv = small_all[:, n_cw:n_cw + n_bq].reshape(1, QKV_DIM)
    bo = small_all[:, n_cw + n_bq:].reshape(1, D_MODEL)

    conv_b = ssm_conv_b.reshape(1, CONV_DIM)
    dt_bias, a_log, d_skip = _pad_lanes(ssm_dt_bias), _pad_lanes(ssm_a_log), _pad_lanes(ssm_d)
    gm_w = gm_w_s[0]
    gm_b = gm_b_s[0].reshape(GM_GROUPS, CHUNK, 1)
    sink_col = jnp.repeat(attn_sinks.reshape(ATTN_HEADS), CHUNK).reshape(ATTN_KV, _Q_PER_KV * CHUNK, 1)
    row = lambda a: a.reshape(1, D_MODEL)

    y0 = _rms_fwd(xs, row(norm_mix_g[0]), name="rms_mix0")
    proj_uv = _matmul(y0, w_uv, "nn", name="proj_uv", outs=[F32])
    proj_rest = _matmul(y0, w_rest, "nn", name="proj_rest", outs=[F32], tn=640)
    a_out = _gmlp_fwd(proj_uv, gm_ln_g, gm_ln_b, gm_w, gm_b, name="gmlp_fwd")
    b_out, h_states = _ssd_fwd(proj_rest, conv_w, conv_b, dt_bias, a_log, d_skip, ssm_norm_g, name="ssd_fwd")
    add = lambda acc, res: (acc + res,)
    h1 = _matmul(a_out, w_out[:D_MODEL], "nn", name="mix_out_a", outs=[F32], extras=[(xs, "tile")], epilogue=add)
    h1 = _matmul(b_out, w_out[D_MODEL:], "nn", name="mix_out_b", outs=[F32], extras=[(h1, "tile")], epilogue=add)

    def mlp_fwd(h, layer):
        y = _rms_fwd(h, row(norm_mlp_g[layer]), name=f"rms_mlp{layer}")
        up, act = _matmul(y, wup[layer], "nn", name=f"mlp_up{layer}", outs=[F32, BF16],
                          epilogue=lambda acc: (acc, jnp.square(jnp.maximum(acc, 0.0))))
        h_new = _matmul(act, wdown[layer], "nn", name=f"mlp_down{layer}", outs=[F32], extras=[(h, "tile")],
                        epilogue=add)
        return y, up, act, h_new

    y1, up0, act0, h2 = mlp_fwd(h1, 0)
    y2 = _rms_fwd(h2, row(norm_mix_g[1]), name="rms_mix1")
    qkv = _matmul(y2, wqkv, "nn", name="qkv", outs=[F32], extras=[(bqkv, "row")], epilogue=lambda acc, b: (acc + b,),
                  tn=640)
    heads = lambda a, n: jnp.transpose(a.reshape(t, n, HEAD_DIM), (1, 0, 2))
    q = heads(qkv[:, :D_MODEL], ATTN_HEADS)
    k = heads(qkv[:, D_MODEL:D_MODEL + ATTN_KV * HEAD_DIM], ATTN_KV)
    v = heads(qkv[:, D_MODEL + ATTN_KV * HEAD_DIM:], ATTN_KV)
    attn = _attn_fwd(q, k, v, sink_col, name="attn_fwd")
    attn = jnp.transpose(attn, (1, 0, 2)).reshape(t, D_MODEL)
    h3 = _matmul(attn, wo, "nn", name="attn_out", outs=[F32], extras=[(h2, "tile"), (bo, "row")],
                 epilogue=lambda acc, res, b: (acc + res + b,))
    y3, up1, act1, h4 = mlp_fwd(h3, 1)
    loss_part, dh4, d_final_g = _final_loss(h4, row(final_norm_g), target, name="final_loss")

    def mlp_bwd(dh, h, y, up, act, layer):
        d_up = _matmul(dh, wdown[layer], "nt", name=f"mlp_down_dx{layer}", outs=[BF16], extras=[(up, "tile")],
                       epilogue=lambda acc, u: (acc * (2.0 * jnp.maximum(u, 0.0)),))
        g_down = _matmul(act, dh, "tn", name=f"mlp_down_dw{layer}", outs=[BF16])
        g_up = _matmul(y, d_up, "tn", name=f"mlp_up_dw{layer}", outs=[BF16])
        dy = _matmul(d_up, wup[layer], "nt", name=f"mlp_up_dx{layer}", outs=[F32])
        dh_new, dg, cs = _rms_bwd(dy, h, row(norm_mlp_g[layer]), dh, name=f"rms_mlp_bwd{layer}")
        return dh_new, cs, g_up, g_down, dg

    dh3, cs3, g_up1, g_down1, g_nmlp1 = mlp_bwd(dh4, h3, y3, up1, act1, 1)
    g_bo = cs3
    g_wo = _matmul(attn, dh3, "tn", name="attn_out_dw", outs=[BF16])
    d_attn = _matmul(dh3, wo, "nt", name="attn_out_dx", outs=[F32])
    d_o = jnp.transpose(d_attn.reshape(t, ATTN_HEADS, HEAD_DIM), (1, 0, 2))
    dq, dk, dv, d_sink = _attn_bwd(q, k, v, sink_col, d_o, name="attn_bwd")
    unheads = lambda a: jnp.transpose(a, (1, 0, 2)).reshape(t, -1)
    dqkv = jnp.concatenate([unheads(dq), unheads(dk), unheads(dv)], axis=1)
    g_bqkv = _colsum(dqkv, name="qkv_db")
    g_wqkv = _matmul(y2, dqkv, "tn", name="qkv_dw", outs=[BF16], tn=640)
    dy2 = _matmul(dqkv, wqkv, "nt", name="qkv_dx", outs=[F32], tk=640)
    dh2, g_nmix1, _ = _rms_bwd(dy2, h2, row(norm_mix_g[1]), dh3, name="rms_mix1_bwd")
    dh1, _, g_up0, g_down0, g_nmlp0 = mlp_bwd(dh2, h1, y1, up0, act0, 0)

    d_mix = _matmul(dh1, w_out, "nt", name="mix_out_dx", outs=[F32])
    g_wout = jnp.concatenate([_matmul(a_out, dh1, "tn", name="mix_out_dw_a", outs=[BF16]),
                              _matmul(b_out, dh1, "tn", name="mix_out_dw_b", outs=[BF16])], axis=0)
    d_uv, g_ln_g, g_ln_b, g_gm_w, g_gm_b = _gmlp_bwd(proj_uv, d_mix, gm_ln_g, gm_ln_b, gm_w, gm_b, name="gmlp_bwd")
    dpre, dz, ddt, g_dtb, g_alog, g_dskip, g_ssm_ng = _ssd_bwd(
        proj_rest, h_states, d_mix, conv_w, conv_b, dt_bias, a_log, d_skip, ssm_norm_g, name="ssd_bwd")
    d_rest, g_conv_w, g_conv_b = _conv_bwd(proj_rest, dpre, dz, ddt, conv_w, name="conv_bwd")
    g_w_uv = _matmul(y0, d_uv, "tn", name="proj_uv_dw", outs=[BF16])
    g_w_rest = _matmul(y0, d_rest, "tn", name="proj_rest_dw", outs=[BF16], tn=640)
    dy0 = _matmul(d_uv, w_uv, "nt", name="proj_uv_dx", outs=[F32])
    dy0 = _matmul(d_rest, w_rest, "nt", name="proj_rest_dx", outs=[F32], extras=[(dy0, "tile")], epilogue=add,
                  tk=640)
    dx, g_nmix0, _ = _rms_bwd(dy0, xs, row(norm_mix_g[0]), dh1, name="rms_mix0_bwd")

    g_w_in = jnp.concatenate([g_w_uv, g_w_rest[:, CONV_DIM:CONV_DIM + D_MODEL], g_w_rest[:, :CONV_DIM],
                              g_w_rest[:, CONV_DIM + D_MODEL:CONV_DIM + D_MODEL + SSM_HEADS]], axis=1)
    per_dev = lambda a: a.reshape(N_DEV, -1, PACK_W)
    big_parts = {
        "w_in_even": per_dev(_cols_to_shards(g_w_in)),
        "w_out_even": per_dev(g_wout),
        "w_qkv": per_dev(_cols_to_shards(g_wqkv)),
        "w_o": per_dev(g_wo),
        "w_up": per_dev(jnp.stack([_cols_to_shards(g_up0), _cols_to_shards(g_up1)], axis=1)),
        "w_down": per_dev(jnp.stack([g_down0.reshape(N_DEV, -1, D_MODEL), g_down1.reshape(N_DEV, -1, D_MODEL)],
                                    axis=1)),
    }
    small_grads = {
        "norm_mix_g": jnp.concatenate([g_nmix0, g_nmix1], axis=0),
        "norm_mlp_g": jnp.concatenate([g_nmlp0, g_nmlp1], axis=0),
        "final_norm_g": d_final_g,
        "gm_ln_g": g_ln_g, "gm_ln_b": g_ln_b, "gm_w_s": g_gm_w, "gm_b_s": g_gm_b,
        "ssm_conv_b": g_conv_b,
        "ssm_dt_bias": g_dtb[:, :SSM_HEADS], "ssm_a_log": g_alog[:, :SSM_HEADS], "ssm_d": g_dskip[:, :SSM_HEADS],
        "ssm_norm_g": g_ssm_ng,
        "attn_sinks": jnp.sum(d_sink.reshape(ATTN_HEADS, CHUNK), axis=1),
        "ssm_conv_w": g_conv_w, "b_qkv": g_bqkv, "b_o": g_bo,
    }
    small_order = replicated + small_sharded
    small_pack = jnp.concatenate([small_grads[n].reshape(-1) for n in small_order])
    n_small_g = small_pack.shape[0]
    small_rows = _f32_as_bf16_rows(small_pack)
    n_big_rows = sum(big_parts[n].shape[1] for n in big)
    send = jnp.concatenate([big_parts[n] for n in big]
                           + [jnp.broadcast_to(small_rows[None], (N_DEV,) + small_rows.shape)], axis=1)
    pad = (-send.shape[1]) % 16
    send = jnp.pad(send, ((0, 0), (0, pad), (0, 0)))
    recv = _all_to_all(send, name="exchange_grads")

    pack3 = lambda tree, ns: jnp.concatenate([_as_rows(tree[n]) for n in ns], axis=0)
    n_rows = n_big_rows
    pad_big = (-n_rows) % 256
    padr = lambda a: jnp.pad(a, ((0, pad_big), (0, 0)))
    big_recv = jnp.pad(recv[:, :n_rows], ((0, 0), (0, pad_big), (0, 0)))
    gb, db, mb, vb = _adamw(big_recv, padr(pack3(W, big)), padr(pack3(M, big)), padr(pack3(V, big)),
                            name="adamw_big")
    small_recv = jnp.stack([_bf16_rows_as_f32(recv[d, n_rows:], n_small_g) for d in range(N_DEV)])
    sizes = [small_grads[n].size for n in small_order]
    starts = [sum(sizes[:i]) for i in range(len(sizes))]
    n_rep = sum(small_grads[n].size for n in replicated)
    rep_parts = jnp.stack([_as_rows(small_recv[d, :n_rep]) for d in range(N_DEV)])
    flat_rep = lambda tree: _as_rows(jnp.concatenate([tree[n].reshape(-1) for n in replicated]))
    gr, dr, mr, vr = _adamw(rep_parts, flat_rep(W), flat_rep(M), flat_rep(V), name="adamw_replicated")
    shard_parts = []
    for n, st, sz in zip(small_order, starts, sizes):
        if n in small_sharded:
            full = small_recv[:, st:st + sz].reshape((N_DEV,) + small_grads[n].shape)
            c = full.shape[-1] // N_DEV
            shard_parts.append(lax.dynamic_slice_in_dim(full, me * c, c, axis=full.ndim - 1).reshape(N_DEV, -1))
    shard_parts = jnp.concatenate(shard_parts, axis=1)
    n_sh = shard_parts.shape[1]
    sh_rows = jnp.stack([_as_rows(shard_parts[d]) for d in range(N_DEV)])
    flat_sh = lambda tree: _as_rows(jnp.concatenate([tree[n].reshape(-1) for n in small_sharded]))
    gs, ds_, ms, vs = _adamw(sh_rows, flat_sh(W), flat_sh(M), flat_sh(V), name="adamw_small_sharded")

    def unpack(rows, ns, tree):
        flat, out, o = rows.reshape(-1), {}, 0
        for n in ns:
            out[n] = flat[o:o + tree[n].size].reshape(tree[n].shape)
            o += tree[n].size
        return out

    def unpack_rows(rows, ns, tree):
        out, o = {}, 0
        for n in ns:
            r = -(-tree[n].size // PACK_W)
            out[n] = rows[o:o + r].reshape(-1)[:tree[n].size].reshape(tree[n].shape)
            o += r
        return out

    results = []
    for big_r, rep_r, sh_r in ((gb, gr, gs), (db, dr, ds_), (mb, mr, ms), (vb, vr, vs)):
        d = {}
        d.update(unpack_rows(big_r, big, W))
        d.update(unpack(rep_r, replicated, W))
        d.update(unpack(sh_r, small_sharded, W))
        results.append(d)

    loss = lax.psum(loss_part[0, 0], ("x", "y", "c"))
    grad_x = dx.reshape(x.shape)
    out = [loss, grad_x]
    for d in results:
        out.extend(d[n] for n in names)
    return tuple(out)
```

```python
import dataclasses
import functools

import jax
import jax.numpy as jnp
from jax import lax
from jax.experimental import pallas as pl
from jax.experimental.pallas import tpu as pltpu

F32 = jnp.float32
BF16 = jnp.bfloat16

N_DEV = 8
D_MODEL = 1024
D_FF = 4096
RMS_EPS = 1e-5
LN_EPS = 1e-5
CHUNK = 128
GM_GROUPS = 8
SSM_HEADS = 16
SSM_HEADDIM = 64
SSM_GROUPS = 4
SSM_STATE = 128
SSM_CONV = 4
CONV_DIM = 2048
IN_EVEN = 5136
REST_W = 3200
ATTN_HEADS = 16
ATTN_KV = 2
HEAD_DIM = 64
QKV_DIM = 1280
LANES = 128
HALO = 8
PACK_W = 1024

ADAM_LR = 0.001
ADAM_B1 = 0.9
ADAM_B2 = 0.999
ADAM_EPS = 1e-08
ADAM_WD = 0.01
ADAM_STEP = 10

VMEM_LIMIT_BYTES = 56 * 1024 * 1024


_NN = (((1,), (0,)), ((), ()))
_NT = (((1,), (1,)), ((), ()))
_TN = (((0,), (0,)), ((), ()))


def _dg(a, b, dims):
    return lax.dot_general(a.astype(BF16), b.astype(BF16), dims, preferred_element_type=F32)


@jax.custom_vjp
def _nn(a, b):
    return _dg(a, b, _NN)


@jax.custom_vjp
def _nt(a, b):
    return _dg(a, b, _NT)


@jax.custom_vjp
def _tn(a, b):
    return _dg(a, b, _TN)


_nn.defvjp(lambda a, b: (_dg(a, b, _NN), (a, b)), lambda r, g: (_nt(g, r[1]), _tn(r[0], g)))
_nt.defvjp(lambda a, b: (_dg(a, b, _NT), (a, b)), lambda r, g: (_nn(g, r[1]), _tn(g, r[0])))
_tn.defvjp(lambda a, b: (_dg(a, b, _TN), (a, b)), lambda r, g: (_nt(r[1], g), _nn(r[0], g)))


def _split3_dot(tri, x):
    x1 = x.astype(BF16)
    r1 = x - x1.astype(F32)
    x2 = r1.astype(BF16)
    x3 = (r1 - x2.astype(F32)).astype(BF16)
    t = tri.astype(BF16)
    dot = lambda p: lax.dot_general(t, p, _NN, preferred_element_type=F32)
    return dot(x1) + dot(x2) + dot(x3)


def _tri(lower):
    r = lax.broadcasted_iota(jnp.int32, (CHUNK, CHUNK), 0)
    c = lax.broadcasted_iota(jnp.int32, (CHUNK, CHUNK), 1)
    return jnp.where((r >= c) if lower else (r <= c), 1.0, 0.0).astype(F32)


@jax.custom_vjp
def _cumsum_rows(x):
    return _split3_dot(_tri(True), x)


_cumsum_rows.defvjp(lambda x: (_split3_dot(_tri(True), x), None), lambda _, g: (_split3_dot(_tri(False), g),))


def _sigmoid(x):
    return 1.0 / (1.0 + jnp.exp(-x))


def _silu(x):
    return x * _sigmoid(x)


def _softplus(x):
    return jnp.maximum(x, 0.0) + jnp.log(1.0 + jnp.exp(-jnp.abs(x)))


def _gelu_tanh(x):
    return 0.5 * x * (1.0 + jnp.tanh(0.7978845608028654 * (x + 0.044715 * (x * x * x))))


def _rmsnorm(x, g):
    return x * lax.rsqrt(jnp.mean(x * x, axis=-1, keepdims=True) + RMS_EPS) * g


def _gmlp_chunk(u, v, ln_g, ln_b, w_s, b_s):
    gu = _gelu_tanh(u)
    gv = _gelu_tanh(v)
    mu = jnp.mean(gv, axis=-1, keepdims=True)
    var = jnp.mean(jnp.square(gv - mu), axis=-1, keepdims=True)
    vn = (gv - mu) * lax.rsqrt(var + LN_EPS) * ln_g + ln_b
    r = lax.broadcasted_iota(jnp.int32, (CHUNK, CHUNK), 0)
    c = lax.broadcasted_iota(jnp.int32, (CHUNK, CHUNK), 1)
    causal = r >= c
    outs = []
    for g in range(GM_GROUPS):
        cols = slice(g * LANES, (g + 1) * LANES)
        mixed = _nn(jnp.where(causal, w_s[g], 0.0), vn[:, cols]) + b_s[g]
        outs.append(gu[:, cols] * mixed)
    return jnp.concatenate(outs, axis=1)


def _lane_pick(row, h):
    lane = lax.broadcasted_iota(jnp.int32, row.shape, 1)
    return jnp.sum(jnp.where(lane == h, row, 0.0), axis=1, keepdims=True)


def _col_pick(m, h):
    lane = lax.broadcasted_iota(jnp.int32, m.shape, 1)
    return jnp.sum(jnp.where(lane == h, m, 0.0), axis=1, keepdims=True)


def _row_pick(m, h):
    sub = lax.broadcasted_iota(jnp.int32, m.shape, 0)
    return jnp.sum(jnp.where(sub == h, m, 0.0), axis=0, keepdims=True)


_PAIRS = SSM_HEADS // 2


def _ssd_chunk(pre, z, dt_raw, h_prev, dt_bias, a_log, d_skip, norm_g):
    xbc = _silu(pre)
    dt = _softplus(dt_raw + dt_bias)
    da = dt * (-jnp.exp(a_log))
    a_cum = _cumsum_rows(da)
    a_cum_t = a_cum.T
    dt_t = dt.T
    r = lax.broadcasted_iota(jnp.int32, (CHUNK, CHUNK), 0)
    c = lax.broadcasted_iota(jnp.int32, (CHUNK, CHUNK), 1)
    causal = r >= c
    lane_lo = lax.broadcasted_iota(jnp.int32, (1, LANES), 1) < SSM_HEADDIM
    last_row = lax.broadcasted_iota(jnp.int32, (CHUNK, 1), 0) == CHUNK - 1
    ys, h_next = [], []
    for j in range(_PAIRS):
        g = j // 2
        xs = xbc[:, j * LANES:(j + 1) * LANES]
        bm = xbc[:, 1024 + g * SSM_STATE:1024 + (g + 1) * SSM_STATE]
        cm = xbc[:, 1536 + g * SSM_STATE:1536 + (g + 1) * SSM_STATE]
        cb = _nt(cm, bm)
        y_diag, to_end, e_cum, c_dec, d_row = [], [], [], [], []
        for h in (2 * j, 2 * j + 1):
            col = _col_pick(a_cum, h)
            row = _row_pick(a_cum_t, h)
            dt_col = _col_pick(dt, h)
            dt_row = _row_pick(dt_t, h)
            decay = jnp.exp(jnp.where(causal, col - row, -jnp.inf))
            y_diag.append(_nn(cb * decay * dt_row, xs))
            last = jnp.sum(jnp.where(last_row, col, 0.0), axis=0, keepdims=True)
            to_end.append(jnp.exp(last - col) * dt_col)
            e_cum.append(jnp.exp(col))
            c_dec.append(jnp.exp(last))
            d_row.append(_lane_pick(d_skip, h))
        pair = lambda lo_hi: jnp.where(lane_lo, lo_hi[0], lo_hi[1])
        states = _tn(bm, xs * pair(to_end))
        y_off = _nn(cm, h_prev[j]) * pair(e_cum)
        ys.append(pair(y_diag) + y_off + xs * pair(d_row))
        h_next.append(pair(c_dec) * h_prev[j] + states)
    y = jnp.concatenate(ys, axis=1) * _silu(z)
    width = D_MODEL // SSM_GROUPS
    y = jnp.concatenate(
        [_rmsnorm(y[:, g * width:(g + 1) * width], norm_g[:, g * width:(g + 1) * width]) for g in range(SSM_GROUPS)],
        axis=1)
    return y, tuple(h_next)


def _shift_down(prev8, x, k):
    if k == 0:
        return x
    win = jnp.concatenate([prev8, x], axis=0)
    return pltpu.roll(win, k, 0)[HALO:]


def _shift_up(x, next8, k):
    if k == 0:
        return x
    n = x.shape[0]
    win = jnp.concatenate([x, next8], axis=0)
    return pltpu.roll(win, n + HALO - k, 0)[:n]


def _conv_pre(prev8, x, w, b):
    out = b + x * w[SSM_CONV - 1:SSM_CONV]
    for i in range(SSM_CONV - 1):
        out = out + _shift_down(prev8, x, SSM_CONV - 1 - i) * w[i:i + 1]
    return out


def _attn_block(q, k_prev, k_cur, v_prev, v_cur, sink, first):
    k = jnp.concatenate([k_prev, k_cur], axis=0)
    v = jnp.concatenate([v_prev, v_cur], axis=0)
    s = _nt(q, k) * (HEAD_DIM ** -0.5)
    rows = lax.broadcasted_iota(jnp.int32, s.shape, 0) & (CHUNK - 1)
    cols = lax.broadcasted_iota(jnp.int32, s.shape, 1)
    valid = (cols <= rows + CHUNK) & (cols > rows) & (cols >= CHUNK * first.astype(jnp.int32))
    s = jnp.where(valid, s, -jnp.inf)
    m = jnp.maximum(jnp.max(s, axis=-1, keepdims=True), sink)
    p = jnp.exp(s - m)
    denom = jnp.sum(p, axis=-1, keepdims=True) + jnp.exp(sink - m)
    return _nn(p / denom, v)


@dataclasses.dataclass
class _Move:
    kind: str
    src: jax.Array

    def dst_shape(self):
        inner = self.src.shape if self.kind == "gather" else self.src.shape[1:]
        return jax.ShapeDtypeStruct((N_DEV,) + tuple(inner), self.src.dtype)


def _peer(x, y, c, k):
    return (1 - x if k & 4 else x, 1 - y if k & 2 else y, 1 - c if k & 1 else c)


def _move_copies(moves, srcs, dsts, send_sems, recv_sems, local_sems):
    x, y, c = lax.axis_index("x"), lax.axis_index("y"), lax.axis_index("c")
    me = 4 * x + 2 * y + c
    local, remote = [], []
    for n, mv in enumerate(moves):
        landing = dsts[n].at[me]
        own = srcs[n] if mv.kind == "gather" else srcs[n].at[me]
        local.append(pltpu.make_async_copy(own, landing, local_sems.at[n]))
        for k in range(1, N_DEV):
            px, py, pc = _peer(x, y, c, k)
            out = srcs[n] if mv.kind == "gather" else srcs[n].at[4 * px + 2 * py + pc]
            remote.append(pltpu.make_async_remote_copy(
                src_ref=out, dst_ref=landing, send_sem=send_sems.at[n, k - 1], recv_sem=recv_sems.at[n, k - 1],
                device_id=(px, py, pc), device_id_type=pl.DeviceIdType.MESH))
    return local, remote


def _pcall(body, *, name, grid, in_specs, out_specs, out_shape, scratch_shapes=(), semantics=(), moves=(),
           aliases=None):
    out_shape, out_specs = list(out_shape), list(out_specs)
    in_specs = list(in_specs)
    if not moves:
        call = pl.pallas_call(
            body, name=name, grid=grid, in_specs=in_specs, out_specs=out_specs, out_shape=out_shape,
            scratch_shapes=list(scratch_shapes), input_output_aliases=aliases or {},
            compiler_params=pltpu.CompilerParams(dimension_semantics=tuple(semantics),
                                                 vmem_limit_bytes=VMEM_LIMIT_BYTES))
        return (lambda *args: (list(call(*args)), []))
    n_in, n_out, n_scr, n_mv = len(in_specs), len(out_shape), len(scratch_shapes), len(moves)
    hbm = pl.BlockSpec(memory_space=pltpu.HBM)

    def carrier(*refs):
        ins, rest = refs[:n_in], refs[n_in:]
        srcs, rest = rest[:n_mv], rest[n_mv:]
        outs, rest = rest[:n_out], rest[n_out:]
        dsts, rest = rest[:n_mv], rest[n_mv:]
        scr, (send_sems, recv_sems, local_sems) = rest[:n_scr], rest[n_scr:]
        first = functools.reduce(jnp.logical_and, [pl.program_id(d) == 0 for d in range(len(grid))])
        last = functools.reduce(jnp.logical_and, [pl.program_id(d) == grid[d] - 1 for d in range(len(grid))])

        @pl.when(first)
        def _():
            local, remote = _move_copies(moves, srcs, dsts, send_sems, recv_sems, local_sems)
            for cp in local + remote:
                cp.start()

        body(*ins, *outs, *scr)

        @pl.when(last)
        def _():
            local, remote = _move_copies(moves, srcs, dsts, send_sems, recv_sems, local_sems)
            for cp in remote + local:
                cp.wait()

    call = pl.pallas_call(
        carrier, name=name, grid=grid,
        in_specs=in_specs + [hbm] * n_mv,
        out_specs=out_specs + [hbm] * n_mv,
        out_shape=out_shape + [mv.dst_shape() for mv in moves],
        scratch_shapes=list(scratch_shapes) + [pltpu.SemaphoreType.DMA((n_mv, N_DEV - 1)),
                                               pltpu.SemaphoreType.DMA((n_mv, N_DEV - 1)),
                                               pltpu.SemaphoreType.DMA((n_mv,))],
        input_output_aliases=aliases or {},
        compiler_params=pltpu.CompilerParams(dimension_semantics=("arbitrary",) * len(grid),
                                             vmem_limit_bytes=VMEM_LIMIT_BYTES))

    def run(*args):
        res = list(call(*args, *[mv.src for mv in moves]))
        return res[:n_out], res[n_out:]

    return run


def _exchange(moves, *, name):
    n_mv = len(moves)
    hbm = pl.BlockSpec(memory_space=pltpu.HBM)

    def body(*refs):
        srcs, dsts, (send_sems, recv_sems, local_sems) = refs[:n_mv], refs[n_mv:2 * n_mv], refs[2 * n_mv:]
        local, remote = _move_copies(moves, srcs, dsts, send_sems, recv_sems, local_sems)
        for cp in local + remote:
            cp.start()
        for cp in remote + local:
            cp.wait()

    return list(pl.pallas_call(
        body, name=name, in_specs=[hbm] * n_mv, out_specs=[hbm] * n_mv,
        out_shape=[mv.dst_shape() for mv in moves],
        scratch_shapes=[pltpu.SemaphoreType.DMA((n_mv, N_DEV - 1)), pltpu.SemaphoreType.DMA((n_mv, N_DEV - 1)),
                        pltpu.SemaphoreType.DMA((n_mv,))],
    )(*[mv.src for mv in moves]))


def _matmul(a, b, mode, *, name, outs, extras=(), epilogue=None, tm=1024, tn=1024, tk=1024, dims=None,
            b_spec=None, out_spec=None, out_shape=None, moves=()):
    if dims is not None:
        m, n, k = dims
    elif mode == "nn":
        (m, k), (_, n) = a.shape, b.shape
    elif mode == "nt":
        (m, k), (n, _) = a.shape, b.shape
    else:
        (k, m), (_, n) = a.shape, b.shape
    tm, tn, tk = min(tm, m), min(tn, n), min(tk, k)
    assert m % tm == 0 and n % tn == 0 and k % tk == 0, (name, m, n, k, tm, tn, tk)
    nk = k // tk
    contract = {"nn": _NN, "nt": _NT, "tn": _TN}[mode]
    if mode == "tn":
        a_spec = pl.BlockSpec((tk, tm), lambda i, j, kk: (kk, i))
    else:
        a_spec = pl.BlockSpec((tm, tk), lambda i, j, kk: (i, kk))
    if b_spec is None:
        if mode == "nt":
            b_spec = pl.BlockSpec((tn, tk), lambda i, j, kk: (j, kk))
        else:
            b_spec = pl.BlockSpec((tk, tn), lambda i, j, kk: (kk, j))
    tile_spec = pl.BlockSpec((tm, tn), lambda i, j, kk: (i, j))
    row_spec = pl.BlockSpec((1, tn), lambda i, j, kk: (0, j))
    extra_specs = [tile_spec if kind == "tile" else row_spec for _, kind in extras]
    n_extra, n_out = len(extras), len(outs)
    if epilogue is None:
        epilogue = lambda acc: (acc,)
    if out_spec is None:
        out_spec, out_shape = tile_spec, (m, n)

    def body(a_ref, b_ref, *rest):
        extra_refs, out_refs, acc_ref = rest[:n_extra], rest[n_extra:n_extra + n_out], rest[-1]
        kk = pl.program_id(2)

        @pl.when(kk == 0)
        def _():
            acc_ref[...] = jnp.zeros_like(acc_ref)

        acc_ref[...] += lax.dot_general(a_ref[...].astype(BF16), b_ref[...].astype(BF16), contract,
                                        preferred_element_type=F32)

        @pl.when(kk == nk - 1)
        def _():
            res = epilogue(acc_ref[...], *[e[...] for e in extra_refs])
            for val, o_ref in zip(res, out_refs):
                o_ref[...] = val.astype(o_ref.dtype)

    res, landed = _pcall(
        body, name=name, grid=(m // tm, n // tn, nk),
        in_specs=[a_spec, b_spec] + extra_specs,
        out_specs=[out_spec] * n_out,
        out_shape=[jax.ShapeDtypeStruct(out_shape, dt) for dt in outs],
        scratch_shapes=[pltpu.VMEM((tm, tn), F32)],
        semantics=("parallel", "parallel", "arbitrary"), moves=moves,
    )(a, b, *[e for e, _ in extras])
    res = res[0] if n_out == 1 else res
    return (res, landed) if moves else res


def _rms_fwd(h, g, *, name, tb=512):
    t, d = h.shape

    def body(h_ref, g_ref, y_ref):
        y_ref[...] = _rmsnorm(h_ref[...], g_ref[...]).astype(y_ref.dtype)

    return _pcall(
        body, name=name, grid=(t // tb,),
        in_specs=[pl.BlockSpec((tb, d), lambda i: (i, 0)), pl.BlockSpec((1, d), lambda i: (0, 0))],
        out_specs=[pl.BlockSpec((tb, d), lambda i: (i, 0))],
        out_shape=[jax.ShapeDtypeStruct((t, d), BF16)],
        semantics=("parallel",),
    )(h, g)[0][0]


def _rms_bwd(dy, h, g, dres, *, name, tb=512, moves=()):
    t, d = h.shape

    def body(dy_ref, h_ref, g_ref, dres_ref, dh_ref, dg_ref, cs_ref):
        _, vjp = jax.vjp(_rmsnorm, h_ref[...], g_ref[...])
        dh, dg = vjp(dy_ref[...])
        dh = dh + dres_ref[...]
        dh_ref[...] = dh

        @pl.when(pl.program_id(0) == 0)
        def _():
            dg_ref[...] = jnp.zeros_like(dg_ref)
            cs_ref[...] = jnp.zeros_like(cs_ref)

        dg_ref[...] += dg
        cs_ref[...] += jnp.sum(dh, axis=0, keepdims=True)

    blk = pl.BlockSpec((tb, d), lambda i: (i, 0))
    row = pl.BlockSpec((1, d), lambda i: (0, 0))
    res, landed = _pcall(
        body, name=name, grid=(t // tb,),
        in_specs=[blk, blk, row, blk],
        out_specs=[blk, row, row],
        out_shape=[jax.ShapeDtypeStruct((t, d), F32), jax.ShapeDtypeStruct((1, d), F32),
                   jax.ShapeDtypeStruct((1, d), F32)],
        semantics=("arbitrary",), moves=moves,
    )(dy, h, g, dres)
    return (res, landed) if moves else res


def _colsum(a, *, name, tb=512):
    t, d = a.shape

    def body(a_ref, o_ref):
        @pl.when(pl.program_id(0) == 0)
        def _():
            o_ref[...] = jnp.zeros_like(o_ref)

        o_ref[...] += jnp.sum(a_ref[...].astype(F32), axis=0, keepdims=True)

    return _pcall(
        body, name=name, grid=(t // tb,),
        in_specs=[pl.BlockSpec((tb, d), lambda i: (i, 0))],
        out_specs=[pl.BlockSpec((1, d), lambda i: (0, 0))],
        out_shape=[jax.ShapeDtypeStruct((1, d), F32)],
        semantics=("arbitrary",),
    )(a)[0][0]


def _final_loss(h, g, target, *, name, tb=512):
    t, d = h.shape

    def body(h_ref, g_ref, tgt_ref, loss_ref, dh_ref, dg_ref):
        def f(hh, gg):
            err = jnp.square(_rmsnorm(hh, gg) - tgt_ref[...])
            return 0.5 * jnp.sum(jnp.mean(err, axis=-1, keepdims=True), axis=0, keepdims=True)

        val, vjp = jax.vjp(f, h_ref[...], g_ref[...])
        dh, dg = vjp(jnp.ones((1, 1), F32))
        dh_ref[...] = dh

        @pl.when(pl.program_id(0) == 0)
        def _():
            loss_ref[...] = jnp.zeros_like(loss_ref)
            dg_ref[...] = jnp.zeros_like(dg_ref)

        loss_ref[...] += val
        dg_ref[...] += dg

    blk = pl.BlockSpec((tb, d), lambda i: (i, 0))
    row = pl.BlockSpec((1, d), lambda i: (0, 0))
    return _pcall(
        body, name=name, grid=(t // tb,),
        in_specs=[blk, row, blk],
        out_specs=[pl.BlockSpec((8, LANES), lambda i: (0, 0)), blk, row],
        out_shape=[jax.ShapeDtypeStruct((8, LANES), F32), jax.ShapeDtypeStruct((t, d), F32),
                   jax.ShapeDtypeStruct((1, d), F32)],
        semantics=("arbitrary",),
    )(h, g, target)[0]


def _gmlp_fwd(proj_uv, ln_g, ln_b, w_s, b_s, *, name, moves=()):
    t = proj_uv.shape[0]
    w = D_MODEL

    def body(u_ref, v_ref, g_ref, b_ref, w_ref, bs_ref, o_ref):
        o_ref[...] = _gmlp_chunk(u_ref[...], v_ref[...], g_ref[...], b_ref[...], w_ref[...],
                                 bs_ref[...]).astype(o_ref.dtype)

    row = pl.BlockSpec((1, w), lambda i: (0, 0))
    res, landed = _pcall(
        body, name=name, grid=(t // CHUNK,),
        in_specs=[pl.BlockSpec((CHUNK, w), lambda i: (i, 0)), pl.BlockSpec((CHUNK, w), lambda i: (i, 1)), row, row,
                  pl.BlockSpec((GM_GROUPS, CHUNK, CHUNK), lambda i: (0, 0, 0)),
                  pl.BlockSpec((GM_GROUPS, CHUNK, 1), lambda i: (0, 0, 0))],
        out_specs=[pl.BlockSpec((CHUNK, w), lambda i: (i, 0))],
        out_shape=[jax.ShapeDtypeStruct((t, 2 * w), BF16)],
        semantics=("parallel",), moves=moves,
    )(proj_uv, proj_uv, ln_g, ln_b, w_s, b_s)
    return (res[0], landed) if moves else res[0]


def _gmlp_bwd(proj_uv, d_mix, ln_g, ln_b, w_s, b_s, *, name, moves=()):
    t = proj_uv.shape[0]
    w = D_MODEL

    def body(u_ref, v_ref, da_ref, g_ref, b_ref, w_ref, bs_ref, duv_ref, dg_ref, db_ref, dw_ref, dbs_ref):
        _, vjp = jax.vjp(_gmlp_chunk, u_ref[...], v_ref[...], g_ref[...], b_ref[...], w_ref[...], bs_ref[...])
        du, dv, dg, db, dw, dbs = vjp(da_ref[...])
        duv_ref[:, :w] = du.astype(duv_ref.dtype)
        duv_ref[:, w:] = dv.astype(duv_ref.dtype)

        @pl.when(pl.program_id(0) == 0)
        def _():
            dg_ref[...] = jnp.zeros_like(dg_ref)
            db_ref[...] = jnp.zeros_like(db_ref)
            dw_ref[...] = jnp.zeros_like(dw_ref)
            dbs_ref[...] = jnp.zeros_like(dbs_ref)

        dg_ref[...] += dg
        db_ref[...] += db
        dw_ref[...] += dw
        dbs_ref[...] += dbs

    row = pl.BlockSpec((1, w), lambda i: (0, 0))
    ws = pl.BlockSpec((GM_GROUPS, CHUNK, CHUNK), lambda i: (0, 0, 0))
    bs = pl.BlockSpec((GM_GROUPS, CHUNK, 1), lambda i: (0, 0, 0))
    res, landed = _pcall(
        body, name=name, grid=(t // CHUNK,),
        in_specs=[pl.BlockSpec((CHUNK, w), lambda i: (i, 0)), pl.BlockSpec((CHUNK, w), lambda i: (i, 1)),
                  pl.BlockSpec((CHUNK, w), lambda i: (i, 0)), row, row, ws, bs],
        out_specs=[pl.BlockSpec((CHUNK, 2 * w), lambda i: (i, 0)), row, row, ws, bs],
        out_shape=[jax.ShapeDtypeStruct((t, 2 * w), BF16), jax.ShapeDtypeStruct((1, w), F32),
                   jax.ShapeDtypeStruct((1, w), F32), jax.ShapeDtypeStruct((GM_GROUPS, CHUNK, CHUNK), F32),
                   jax.ShapeDtypeStruct((GM_GROUPS, CHUNK, 1), F32)],
        semantics=("arbitrary",), moves=moves,
    )(proj_uv, proj_uv, d_mix, ln_g, ln_b, w_s, b_s)
    return (res, landed) if moves else res


_HALO_PER_CHUNK = CHUNK // HALO
_DT_BLOCK = (CONV_DIM + D_MODEL) // LANES


def _ssd_fwd(proj_rest, mix, conv_w, conv_b, dt_bias, a_log, d_skip, norm_g, *, name, moves=()):
    t = proj_rest.shape[0]
    nc = t // CHUNK

    def body(x_ref, prev_ref, z_ref, dt_ref, mix_ref, cw_ref, cb_ref, dtb_ref, al_ref, ds_ref, ng_ref, y_ref, hs_ref,
             h_scr):
        del mix_ref
        i = pl.program_id(0)

        @pl.when(i == 0)
        def _():
            h_scr[...] = jnp.zeros_like(h_scr)

        prev8 = jnp.where(i == 0, 0.0, prev_ref[...])
        pre = _conv_pre(prev8, x_ref[...], cw_ref[...], cb_ref[...])
        hs_ref[0] = h_scr[...]
        h_prev = tuple(h_scr[j] for j in range(_PAIRS))
        y, h_next = _ssd_chunk(pre, z_ref[...], dt_ref[...], h_prev, dtb_ref[...], al_ref[...], ds_ref[...],
                               ng_ref[...])
        y_ref[...] = y.astype(y_ref.dtype)
        for j in range(_PAIRS):
            h_scr[j] = h_next[j]

    small = pl.BlockSpec((1, LANES), lambda i: (0, 0))
    res, landed = _pcall(
        body, name=name, grid=(nc,),
        in_specs=[pl.BlockSpec((CHUNK, CONV_DIM), lambda i: (i, 0)),
                  pl.BlockSpec((HALO, CONV_DIM), lambda i: (jnp.maximum(i * _HALO_PER_CHUNK - 1, 0), 0)),
                  pl.BlockSpec((CHUNK, D_MODEL), lambda i: (i, CONV_DIM // D_MODEL)),
                  pl.BlockSpec((CHUNK, LANES), lambda i: (i, _DT_BLOCK)),
                  pl.BlockSpec(memory_space=pl.ANY),
                  pl.BlockSpec((SSM_CONV, CONV_DIM), lambda i: (0, 0)),
                  pl.BlockSpec((1, CONV_DIM), lambda i: (0, 0)),
                  small, small, small, pl.BlockSpec((1, D_MODEL), lambda i: (0, 0))],
        out_specs=[pl.BlockSpec((CHUNK, D_MODEL), lambda i: (i, 1)),
                   pl.BlockSpec((1, _PAIRS, SSM_STATE, LANES), lambda i: (i, 0, 0, 0))],
        out_shape=[jax.ShapeDtypeStruct((t, 2 * D_MODEL), BF16),
                   jax.ShapeDtypeStruct((nc, _PAIRS, SSM_STATE, LANES), F32)],
        scratch_shapes=[pltpu.VMEM((_PAIRS, SSM_STATE, LANES), F32)],
        semantics=("arbitrary",), moves=moves, aliases={4: 0},
    )(proj_rest, proj_rest, proj_rest, proj_rest, mix, conv_w, conv_b, dt_bias, a_log, d_skip, norm_g)
    return (res, landed) if moves else res


def _ssd_bwd(proj_rest, h_states, d_mix, conv_w, conv_b, dt_bias, a_log, d_skip, norm_g, *, name):
    t = proj_rest.shape[0]
    nc = t // CHUNK

    def body(x_ref, prev_ref, z_ref, dt_ref, hs_ref, dy_ref, cw_ref, cb_ref, dtb_ref, al_ref, ds_ref, ng_ref,
             dpre_ref, dz_ref, ddt_ref, ddtb_ref, dal_ref, dds_ref, dng_ref, dh_scr):
        i = pl.program_id(0)
        chunk = nc - 1 - i

        @pl.when(i == 0)
        def _():
            dh_scr[...] = jnp.zeros_like(dh_scr)
            ddtb_ref[...] = jnp.zeros_like(ddtb_ref)
            dal_ref[...] = jnp.zeros_like(dal_ref)
            dds_ref[...] = jnp.zeros_like(dds_ref)
            dng_ref[...] = jnp.zeros_like(dng_ref)

        prev8 = jnp.where(chunk == 0, 0.0, prev_ref[...])
        pre = _conv_pre(prev8, x_ref[...], cw_ref[...], cb_ref[...])
        h_prev = tuple(hs_ref[0, j] for j in range(_PAIRS))
        _, vjp = jax.vjp(_ssd_chunk, pre, z_ref[...], dt_ref[...], h_prev, dtb_ref[...], al_ref[...],
                         ds_ref[...], ng_ref[...])
        dpre, dz, ddt, dh_prev, ddtb, dal, dds, dng = vjp((dy_ref[...], tuple(dh_scr[j] for j in range(_PAIRS))))
        dpre_ref[...] = dpre
        dz_ref[...] = dz
        ddt_ref[...] = ddt
        for j in range(_PAIRS):
            dh_scr[j] = dh_prev[j]
        ddtb_ref[...] += ddtb
        dal_ref[...] += dal
        dds_ref[...] += dds
        dng_ref[...] += dng

    rev = lambda i: nc - 1 - i
    small = pl.BlockSpec((1, LANES), lambda i: (0, 0))
    wide = pl.BlockSpec((1, D_MODEL), lambda i: (0, 0))
    return _pcall(
        body, name=name, grid=(nc,),
        in_specs=[pl.BlockSpec((CHUNK, CONV_DIM), lambda i: (rev(i), 0)),
                  pl.BlockSpec((HALO, CONV_DIM), lambda i: (jnp.maximum(rev(i) * _HALO_PER_CHUNK - 1, 0), 0)),
                  pl.BlockSpec((CHUNK, D_MODEL), lambda i: (rev(i), CONV_DIM // D_MODEL)),
                  pl.BlockSpec((CHUNK, LANES), lambda i: (rev(i), _DT_BLOCK)),
                  pl.BlockSpec((1, _PAIRS, SSM_STATE, LANES), lambda i: (rev(i), 0, 0, 0)),
                  pl.BlockSpec((CHUNK, D_MODEL), lambda i: (rev(i), 1)),
                  pl.BlockSpec((SSM_CONV, CONV_DIM), lambda i: (0, 0)),
                  pl.BlockSpec((1, CONV_DIM), lambda i: (0, 0)),
                  small, small, small, wide],
        out_specs=[pl.BlockSpec((CHUNK, CONV_DIM), lambda i: (rev(i), 0)),
                   pl.BlockSpec((CHUNK, D_MODEL), lambda i: (rev(i), 0)),
                   pl.BlockSpec((CHUNK, LANES), lambda i: (rev(i), 0)),
                   small, small, small, wide],
        out_shape=[jax.ShapeDtypeStruct((t, CONV_DIM), F32), jax.ShapeDtypeStruct((t, D_MODEL), F32),
                   jax.ShapeDtypeStruct((t, LANES), F32),
                   jax.ShapeDtypeStruct((1, LANES), F32), jax.ShapeDtypeStruct((1, LANES), F32),
                   jax.ShapeDtypeStruct((1, LANES), F32), jax.ShapeDtypeStruct((1, D_MODEL), F32)],
        scratch_shapes=[pltpu.VMEM((_PAIRS, SSM_STATE, LANES), F32)],
        semantics=("arbitrary",),
    )(proj_rest, proj_rest, proj_rest, proj_rest, h_states, d_mix, conv_w, conv_b, dt_bias, a_log, d_skip,
      norm_g)[0]


def _conv_bwd(proj_rest, dpre, dz, ddt, conv_w, *, name, tb=256, moves=()):
    t = proj_rest.shape[0]
    nb = t // tb
    per = tb // HALO

    def body(x_ref, prev_ref, dpre_ref, next_ref, dz_ref, ddt_ref, cw_ref, drest_ref, dcw_ref, dcb_ref):
        i = pl.program_id(0)

        @pl.when(i == 0)
        def _():
            dcw_ref[...] = jnp.zeros_like(dcw_ref)
            dcb_ref[...] = jnp.zeros_like(dcb_ref)

        x = x_ref[...]
        dp = dpre_ref[...]
        w = cw_ref[...]
        prev8 = jnp.where(i == 0, 0.0, prev_ref[...])
        next8 = jnp.where(i == nb - 1, 0.0, next_ref[...])
        dx = dp * w[SSM_CONV - 1:SSM_CONV]
        for j in range(SSM_CONV - 1):
            dx = dx + _shift_up(dp, next8, SSM_CONV - 1 - j) * w[j:j + 1]
        drest_ref[:, :CONV_DIM] = dx.astype(drest_ref.dtype)
        drest_ref[:, CONV_DIM:CONV_DIM + D_MODEL] = dz_ref[...].astype(drest_ref.dtype)
        drest_ref[:, CONV_DIM + D_MODEL:] = ddt_ref[...].astype(drest_ref.dtype)
        for j in range(SSM_CONV):
            dcw_ref[j:j + 1, :] += jnp.sum(dp * _shift_down(prev8, x, SSM_CONV - 1 - j), axis=0, keepdims=True)
        dcb_ref[...] += jnp.sum(dp, axis=0, keepdims=True)

    res, landed = _pcall(
        body, name=name, grid=(nb,),
        in_specs=[pl.BlockSpec((tb, CONV_DIM), lambda i: (i, 0)),
                  pl.BlockSpec((HALO, CONV_DIM), lambda i: (jnp.maximum(i * per - 1, 0), 0)),
                  pl.BlockSpec((tb, CONV_DIM), lambda i: (i, 0)),
                  pl.BlockSpec((HALO, CONV_DIM), lambda i: (jnp.minimum((i + 1) * per, nb * per - 1), 0)),
                  pl.BlockSpec((tb, D_MODEL), lambda i: (i, 0)),
                  pl.BlockSpec((tb, LANES), lambda i: (i, 0)),
                  pl.BlockSpec((SSM_CONV, CONV_DIM), lambda i: (0, 0))],
        out_specs=[pl.BlockSpec((tb, REST_W), lambda i: (i, 0)),
                   pl.BlockSpec((SSM_CONV, CONV_DIM), lambda i: (0, 0)),
                   pl.BlockSpec((1, CONV_DIM), lambda i: (0, 0))],
        out_shape=[jax.ShapeDtypeStruct((t, REST_W), BF16), jax.ShapeDtypeStruct((SSM_CONV, CONV_DIM), F32),
                   jax.ShapeDtypeStruct((1, CONV_DIM), F32)],
        semantics=("arbitrary",), moves=moves,
    )(proj_rest, proj_rest, dpre, dpre, dz, ddt, conv_w)
    return (res, landed) if moves else res


_Q_PER_KV = ATTN_HEADS // ATTN_KV


def _attn_fwd(q, k, v, sink_col, *, name):
    t = q.shape[1]
    nb = t // CHUNK

    def body(q_ref, kp_ref, kc_ref, vp_ref, vc_ref, s_ref, o_ref):
        first = pl.program_id(0) == 0
        for j in range(ATTN_KV):
            qj = q_ref[j * _Q_PER_KV:(j + 1) * _Q_PER_KV].reshape(_Q_PER_KV * CHUNK, HEAD_DIM)
            o = _attn_block(qj, kp_ref[j], kc_ref[j], vp_ref[j], vc_ref[j], s_ref[j], first)
            o_ref[j * _Q_PER_KV:(j + 1) * _Q_PER_KV] = o.reshape(_Q_PER_KV, CHUNK, HEAD_DIM).astype(o_ref.dtype)

    cur = lambda i: (0, i, 0)
    prev = lambda i: (0, jnp.maximum(i - 1, 0), 0)
    kv = (ATTN_KV, CHUNK, HEAD_DIM)
    return _pcall(
        body, name=name, grid=(nb,),
        in_specs=[pl.BlockSpec((ATTN_HEADS, CHUNK, HEAD_DIM), cur), pl.BlockSpec(kv, prev), pl.BlockSpec(kv, cur),
                  pl.BlockSpec(kv, prev), pl.BlockSpec(kv, cur),
                  pl.BlockSpec((ATTN_KV, _Q_PER_KV * CHUNK, 1), lambda i: (0, 0, 0))],
        out_specs=[pl.BlockSpec((ATTN_HEADS, CHUNK, HEAD_DIM), cur)],
        out_shape=[jax.ShapeDtypeStruct((ATTN_HEADS, t, HEAD_DIM), BF16)],
        semantics=("parallel",),
    )(q, k, k, v, v, sink_col)[0][0]


def _attn_bwd(q, k, v, sink_col, d_o, *, name, moves=()):
    t = q.shape[1]
    nb = t // CHUNK

    def body(q_ref, kp_ref, kc_ref, vp_ref, vc_ref, s_ref, do_ref, dq_ref, dk_ref, dv_ref, ds_ref, dk_scr, dv_scr):
        i = pl.program_id(0)
        first = i == nb - 1

        @pl.when(i == 0)
        def _():
            dk_scr[...] = jnp.zeros_like(dk_scr)
            dv_scr[...] = jnp.zeros_like(dv_scr)
            ds_ref[...] = jnp.zeros_like(ds_ref)

        for j in range(ATTN_KV):
            heads = slice(j * _Q_PER_KV, (j + 1) * _Q_PER_KV)
            qj = q_ref[heads].reshape(_Q_PER_KV * CHUNK, HEAD_DIM)
            doj = do_ref[heads].reshape(_Q_PER_KV * CHUNK, HEAD_DIM)
            _, vjp = jax.vjp(functools.partial(_attn_block, first=first), qj, kp_ref[j], kc_ref[j], vp_ref[j],
                             vc_ref[j], s_ref[j])
            dq, dkp, dkc, dvp, dvc, dsink = vjp(doj)
            dq_ref[heads] = dq.reshape(_Q_PER_KV, CHUNK, HEAD_DIM)
            dk_ref[j] = dkc + dk_scr[j]
            dv_ref[j] = dvc + dv_scr[j]
            dk_scr[j] = dkp
            dv_scr[j] = dvp
            ds_ref[j] += dsink

    cur = lambda i: (0, nb - 1 - i, 0)
    prev = lambda i: (0, jnp.maximum(nb - 2 - i, 0), 0)
    kv = (ATTN_KV, CHUNK, HEAD_DIM)
    qs = (ATTN_HEADS, CHUNK, HEAD_DIM)
    sk = pl.BlockSpec((ATTN_KV, _Q_PER_KV * CHUNK, 1), lambda i: (0, 0, 0))
    res, landed = _pcall(
        body, name=name, grid=(nb,),
        in_specs=[pl.BlockSpec(qs, cur), pl.BlockSpec(kv, prev), pl.BlockSpec(kv, cur), pl.BlockSpec(kv, prev),
                  pl.BlockSpec(kv, cur), sk, pl.BlockSpec(qs, cur)],
        out_specs=[pl.BlockSpec(qs, cur), pl.BlockSpec(kv, cur), pl.BlockSpec(kv, cur), sk],
        out_shape=[jax.ShapeDtypeStruct((ATTN_HEADS, t, HEAD_DIM), F32), jax.ShapeDtypeStruct((ATTN_KV, t, HEAD_DIM), F32),
                   jax.ShapeDtypeStruct((ATTN_KV, t, HEAD_DIM), F32),
                   jax.ShapeDtypeStruct((ATTN_KV, _Q_PER_KV * CHUNK, 1), F32)],
        scratch_shapes=[pltpu.VMEM(kv, F32), pltpu.VMEM(kv, F32)],
        semantics=("arbitrary",), moves=moves,
    )(q, k, k, v, v, sink_col, d_o)
    return (res, landed) if moves else res


def _adamw(parts, w, m, v, *, name, tb=256, moves=()):
    layers, r, c = w.shape
    n = parts[0].shape[0]
    tb = min(tb, r)
    assert r % tb == 0 and len(parts) == layers, (name, r, tb)
    nb = r // tb

    def body(*refs):
        p_refs = refs[:layers]
        w_ref, m_ref, v_ref, g_ref, d_ref, nm_ref, nv_ref = refs[layers:]
        for layer in range(layers):
            @pl.when(pl.program_id(0) == layer)
            def _(p_ref=p_refs[layer]):
                g = p_ref[0].astype(F32)
                for s in range(1, n):
                    g = g + p_ref[s].astype(F32)
                m_new = ADAM_B1 * m_ref[...] + (1.0 - ADAM_B1) * g
                v_new = ADAM_B2 * v_ref[...] + (1.0 - ADAM_B2) * jnp.square(g)
                m_hat = m_new / (1.0 - ADAM_B1 ** ADAM_STEP)
                v_hat = v_new / (1.0 - ADAM_B2 ** ADAM_STEP)
                g_ref[...] = g
                d_ref[...] = -ADAM_LR * (m_hat / (jnp.sqrt(v_hat) + ADAM_EPS) + ADAM_WD * w_ref[...])
                nm_ref[...] = m_new
                nv_ref[...] = v_new

    part_spec = lambda layer: pl.BlockSpec(
        (n, tb, c), lambda l, i: (0, jnp.clip(i + (l - layer) * nb, 0, nb - 1), 0))
    blk = pl.BlockSpec((None, tb, c), lambda l, i: (l, i, 0))
    res, landed = _pcall(
        body, name=name, grid=(layers, nb),
        in_specs=[part_spec(layer) for layer in range(layers)] + [blk, blk, blk],
        out_specs=[blk] * 4,
        out_shape=[jax.ShapeDtypeStruct((layers, r, c), F32)] * 4,
        semantics=("arbitrary", "arbitrary"), moves=moves,
    )(*parts, w, m, v)
    return (res, landed) if moves else res


def _as_rows(a):
    flat = a.reshape(-1)
    pad = (-flat.shape[0]) % PACK_W
    if pad:
        flat = jnp.pad(flat, (0, pad))
    return flat.reshape(-1, PACK_W)


def _cols_from_shards(g):
    return jnp.transpose(g, (1, 0, 2)).reshape(g.shape[1], -1)


def _cols_to_shards(a):
    return jnp.transpose(a.reshape(a.shape[0], N_DEV, -1), (1, 0, 2))


def _pad_lanes(a):
    return jnp.pad(a, ((0, 0), (0, LANES - a.shape[1])))


def kernel(x, norm_mix_g, norm_mlp_g, final_norm_g, w_in_even, w_out_even, gm_ln_g, gm_ln_b, gm_w_s, gm_b_s, ssm_conv_w, ssm_conv_b, ssm_dt_bias, ssm_a_log, ssm_d, ssm_norm_g, w_qkv, b_qkv, w_o, b_o, attn_sinks, w_up, w_down, loss_target, m_norm_mix_g, m_norm_mlp_g, m_final_norm_g, m_w_in_even, m_w_out_even, m_gm_ln_g, m_gm_ln_b, m_gm_w_s, m_gm_b_s, m_ssm_conv_w, m_ssm_conv_b, m_ssm_dt_bias, m_ssm_a_log, m_ssm_d, m_ssm_norm_g, m_w_qkv, m_b_qkv, m_w_o, m_b_o, m_attn_sinks, m_w_up, m_w_down, v_norm_mix_g, v_norm_mlp_g, v_final_norm_g, v_w_in_even, v_w_out_even, v_gm_ln_g, v_gm_ln_b, v_gm_w_s, v_gm_b_s, v_ssm_conv_w, v_ssm_conv_b, v_ssm_dt_bias, v_ssm_a_log, v_ssm_d, v_ssm_norm_g, v_w_qkv, v_b_qkv, v_w_o, v_b_o, v_attn_sinks, v_w_up, v_w_down):
    names = ["norm_mix_g", "norm_mlp_g", "final_norm_g", "w_in_even", "w_out_even", "gm_ln_g", "gm_ln_b", "gm_w_s",
             "gm_b_s", "ssm_conv_w", "ssm_conv_b", "ssm_dt_bias", "ssm_a_log", "ssm_d", "ssm_norm_g", "w_qkv",
             "b_qkv", "w_o", "b_o", "attn_sinks", "w_up", "w_down"]
    env = locals()
    W = {n: env[n] for n in names}
    M = {n: env["m_" + n] for n in names}
    V = {n: env["v_" + n] for n in names}
    big = ["w_in_even", "w_out_even", "w_qkv", "w_o", "w_up", "w_down"]
    small_sharded = ["ssm_conv_w", "b_qkv", "b_o"]
    replicated = [n for n in names if n not in big and n not in small_sharded]
    me = 4 * lax.axis_index("x") + 2 * lax.axis_index("y") + lax.axis_index("c")
    t = x.shape[1]
    xs = x.reshape(t, D_MODEL)
    target = loss_target.reshape(t, D_MODEL)
    gather = lambda a: _Move("gather", a)
    scatter = lambda a: _Move("scatter", a)
    row = lambda a: a.reshape(1, D_MODEL)
    add = lambda acc, res: (acc + res,)

    small_flat = jnp.concatenate([W[n].reshape(-1) for n in small_sharded])
    w_in_g, small_g = _exchange([gather(w_in_even[0].astype(BF16)), gather(_as_rows(small_flat))],
                                name="gather_w_in")
    w_in = _cols_from_shards(w_in_g)
    w_uv = w_in[:, :2 * D_MODEL]
    w_rest = jnp.concatenate([w_in[:, 3 * D_MODEL:3 * D_MODEL + CONV_DIM], w_in[:, 2 * D_MODEL:3 * D_MODEL],
                              w_in[:, 3 * D_MODEL + CONV_DIM:],
                              jnp.zeros((D_MODEL, LANES - SSM_HEADS), BF16)], axis=1)
    small_all = small_g.reshape(N_DEV, -1)
    n_cw = SSM_CONV * CONV_DIM // N_DEV
    n_bq = QKV_DIM // N_DEV
    conv_w = _cols_from_shards(small_all[:, :n_cw].reshape(N_DEV, SSM_CONV, CONV_DIM // N_DEV))
    bqkv = small_all[:, n_cw:n_cw + n_bq].reshape(1, QKV_DIM)
    bo = small_all[:, n_cw + n_bq:n_cw + n_bq + D_MODEL // N_DEV].reshape(1, D_MODEL)

    conv_b = ssm_conv_b.reshape(1, CONV_DIM)
    dt_bias, a_log, d_skip = _pad_lanes(ssm_dt_bias), _pad_lanes(ssm_a_log), _pad_lanes(ssm_d)
    gm_w = gm_w_s[0]
    gm_b = gm_b_s[0].reshape(GM_GROUPS, CHUNK, 1)
    sink_col = jnp.repeat(attn_sinks.reshape(ATTN_HEADS), CHUNK).reshape(ATTN_KV, _Q_PER_KV * CHUNK, 1)
    w_up_b, w_down_b = w_up.astype(BF16), w_down.astype(BF16)

    up_cols = pl.BlockSpec((None, D_MODEL, D_FF // N_DEV), lambda i, j, kk: (j, kk, 0))
    up_cols_t = pl.BlockSpec((None, D_MODEL, D_FF // N_DEV), lambda i, j, kk: (kk, j, 0))
    down_rows = pl.BlockSpec((None, D_FF // N_DEV, D_MODEL), lambda i, j, kk: (kk, 0, j))
    down_rows_t = pl.BlockSpec((None, D_FF // N_DEV, D_MODEL), lambda i, j, kk: (j, 0, kk))
    ff_n = D_FF // N_DEV

    y0 = _rms_fwd(xs, row(norm_mix_g[0]), name="rms_mix0")
    proj_uv, (w_out_g,) = _matmul(y0, w_uv, "nn", name="proj_uv", outs=[F32],
                                  moves=[gather(w_out_even[0].astype(BF16))])
    proj_rest, (w_up0_g,) = _matmul(y0, w_rest, "nn", name="proj_rest", outs=[F32], tn=640,
                                    moves=[gather(w_up_b[0])])
    mix, (w_down0_g,) = _gmlp_fwd(proj_uv, gm_ln_g, gm_ln_b, gm_w, gm_b, name="gmlp_fwd",
                                  moves=[gather(w_down_b[0])])
    (mix, h_states), (w_qkv_g, w_o_g) = _ssd_fwd(
        proj_rest, mix, conv_w, conv_b, dt_bias, a_log, d_skip, ssm_norm_g, name="ssd_fwd",
        moves=[gather(w_qkv[0].astype(BF16)), gather(w_o[0].astype(BF16))])
    w_out_f = w_out_g.reshape(2 * D_MODEL, D_MODEL)
    h1, (w_up1_g,) = _matmul(mix, w_out_f, "nn", name="mix_out", outs=[F32], extras=[(xs, "tile")], epilogue=add,
                             moves=[gather(w_up_b[1])])
    w_up_g = [w_up0_g, w_up1_g]

    def mlp_fwd(h, layer, w_down_layer, moves=()):
        y = _rms_fwd(h, row(norm_mlp_g[layer]), name=f"rms_mlp{layer}")
        res = _matmul(y, w_up_g[layer], "nn", name=f"mlp_up{layer}", outs=[F32, BF16],
                      epilogue=lambda acc: (acc, jnp.square(jnp.maximum(acc, 0.0))),
                      dims=(t, D_FF, D_MODEL), tn=ff_n, b_spec=up_cols, moves=moves)
        (up, act), landed = res if moves else (res, [])
        h_new = _matmul(act, w_down_layer(landed), "nn", name=f"mlp_down{layer}", outs=[F32], extras=[(h, "tile")],
                        epilogue=add, dims=(t, D_MODEL, D_FF), tk=ff_n, b_spec=down_rows)
        return y, up, act, h_new, landed

    y1, up0, act0, h2, (w_down1_g,) = mlp_fwd(h1, 0, lambda _: w_down0_g, moves=[gather(w_down_b[1])])
    w_down_g = [w_down0_g, w_down1_g]
    wqkv = _cols_from_shards(w_qkv_g)
    wo = w_o_g.reshape(D_MODEL, D_MODEL)
    y2 = _rms_fwd(h2, row(norm_mix_g[1]), name="rms_mix1")
    qkv = _matmul(y2, wqkv, "nn", name="qkv", outs=[F32], extras=[(bqkv, "row")], epilogue=lambda acc, b: (acc + b,),
                  tn=640)
    heads = lambda a, n: jnp.transpose(a.reshape(t, n, HEAD_DIM), (1, 0, 2))
    q = heads(qkv[:, :D_MODEL], ATTN_HEADS)
    k = heads(qkv[:, D_MODEL:D_MODEL + ATTN_KV * HEAD_DIM], ATTN_KV)
    v = heads(qkv[:, D_MODEL + ATTN_KV * HEAD_DIM:], ATTN_KV)
    attn = _attn_fwd(q, k, v, sink_col, name="attn_fwd")
    attn = jnp.transpose(attn, (1, 0, 2)).reshape(t, D_MODEL)
    h3 = _matmul(attn, wo, "nn", name="attn_out", outs=[F32], extras=[(h2, "tile"), (bo, "row")],
                 epilogue=lambda acc, res, b: (acc + res + b,))
    y3, up1, act1, h4, _ = mlp_fwd(h3, 1, lambda _: w_down1_g)
    loss_part, dh4, d_final_g = _final_loss(h4, row(final_norm_g), target, name="final_loss")

    by_dev_rows = lambda a: a.reshape((N_DEV, a.shape[0] // N_DEV) + a.shape[1:])

    def mlp_bwd(dh, h, y, up, act, layer, moves_a=()):
        res = _matmul(dh, w_down_g[layer], "nt", name=f"mlp_down_dx{layer}", outs=[BF16], extras=[(up, "tile")],
                      epilogue=lambda acc, u: (acc * (2.0 * jnp.maximum(u, 0.0)),),
                      dims=(t, D_FF, D_MODEL), tn=ff_n, b_spec=down_rows_t, moves=moves_a)
        d_up, landed_a = res if moves_a else (res, [])
        g_down = _matmul(act, dh, "tn", name=f"mlp_down_dw{layer}", outs=[BF16])
        g_up, (r_down,) = _matmul(y, d_up, "tn", name=f"mlp_up_dw{layer}", outs=[BF16],
                                  dims=(D_MODEL, D_FF, t), tn=ff_n,
                                  out_spec=pl.BlockSpec((None, D_MODEL, ff_n), lambda i, j, kk: (j, i, 0)),
                                  out_shape=(N_DEV, D_MODEL, ff_n), moves=[scatter(by_dev_rows(g_down))])
        dy, (r_up,) = _matmul(d_up, w_up_g[layer], "nt", name=f"mlp_up_dx{layer}", outs=[F32],
                              dims=(t, D_MODEL, D_FF), tk=ff_n, b_spec=up_cols_t, moves=[scatter(g_up)])
        dh_new, dg, cs = _rms_bwd(dy, h, row(norm_mlp_g[layer]), dh, name=f"rms_mlp_bwd{layer}")
        return dh_new, cs, dg, r_up, r_down, landed_a

    dh3, cs3, g_nmlp1, r_up1, r_down1, _ = mlp_bwd(dh4, h3, y3, up1, act1, 1)
    g_bo = cs3
    g_wo = _matmul(attn, dh3, "tn", name="attn_out_dw", outs=[BF16])
    d_attn = _matmul(dh3, wo, "nt", name="attn_out_dx", outs=[F32])
    d_o = jnp.transpose(d_attn.reshape(t, ATTN_HEADS, HEAD_DIM), (1, 0, 2))
    (dq, dk, dv, d_sink), (r_wo,) = _attn_bwd(q, k, v, sink_col, d_o, name="attn_bwd",
                                              moves=[scatter(by_dev_rows(g_wo))])
    unheads = lambda a: jnp.transpose(a, (1, 0, 2)).reshape(t, -1)
    dqkv = jnp.concatenate([unheads(dq), unheads(dk), unheads(dv)], axis=1)
    g_bqkv = _colsum(dqkv, name="qkv_db")
    g_wqkv = _matmul(y2, dqkv, "tn", name="qkv_dw", outs=[BF16], tn=640)
    dy2 = _matmul(dqkv, wqkv, "nt", name="qkv_dx", outs=[F32], tk=640)
    dh2, g_nmix1, _ = _rms_bwd(dy2, h2, row(norm_mix_g[1]), dh3, name="rms_mix1_bwd")
    dh1, _, g_nmlp0, r_up0, r_down0, (r_wqkv,) = mlp_bwd(dh2, h1, y1, up0, act0, 0,
                                                         moves_a=[scatter(_cols_to_shards(g_wqkv))])

    d_mix = _matmul(dh1, w_out_f, "nt", name="mix_out_dx", outs=[F32])
    g_wout = _matmul(mix, dh1, "tn", name="mix_out_dw", outs=[BF16])
    (d_uv, g_ln_g, g_ln_b, g_gm_w, g_gm_b), (r_wout,) = _gmlp_bwd(
        proj_uv, d_mix, gm_ln_g, gm_ln_b, gm_w, gm_b, name="gmlp_bwd", moves=[scatter(by_dev_rows(g_wout))])
    dpre, dz, ddt, g_dtb, g_alog, g_dskip, g_ssm_ng = _ssd_bwd(
        proj_rest, h_states, d_mix, conv_w, conv_b, dt_bias, a_log, d_skip, ssm_norm_g, name="ssd_bwd")
    d_rest, g_conv_w, g_conv_b = _conv_bwd(proj_rest, dpre, dz, ddt, conv_w, name="conv_bwd")
    g_w_uv = _matmul(y0, d_uv, "tn", name="proj_uv_dw", outs=[BF16])
    g_w_rest = _matmul(y0, d_rest, "tn", name="proj_rest_dw", outs=[BF16], tn=640)
    g_w_in = jnp.concatenate([g_w_uv, g_w_rest[:, CONV_DIM:CONV_DIM + D_MODEL], g_w_rest[:, :CONV_DIM],
                              g_w_rest[:, CONV_DIM + D_MODEL:CONV_DIM + D_MODEL + SSM_HEADS]], axis=1)
    dy0 = _matmul(d_uv, w_uv, "nt", name="proj_uv_dx", outs=[F32])
    dy0, (r_w_in,) = _matmul(d_rest, w_rest, "nt", name="proj_rest_dx", outs=[F32], extras=[(dy0, "tile")],
                             epilogue=add, tk=640, moves=[scatter(_cols_to_shards(g_w_in))])
    dx, g_nmix0, _ = _rms_bwd(dy0, xs, row(norm_mix_g[0]), dh1, name="rms_mix0_bwd")

    small_grads = {
        "norm_mix_g": jnp.concatenate([g_nmix0, g_nmix1], axis=0),
        "norm_mlp_g": jnp.concatenate([g_nmlp0, g_nmlp1], axis=0),
        "final_norm_g": d_final_g,
        "gm_ln_g": g_ln_g, "gm_ln_b": g_ln_b, "gm_w_s": g_gm_w, "gm_b_s": g_gm_b,
        "ssm_conv_b": g_conv_b,
        "ssm_dt_bias": g_dtb[:, :SSM_HEADS], "ssm_a_log": g_alog[:, :SSM_HEADS], "ssm_d": g_dskip[:, :SSM_HEADS],
        "ssm_norm_g": g_ssm_ng,
        "attn_sinks": jnp.sum(d_sink.reshape(ATTN_HEADS, CHUNK), axis=1),
        "ssm_conv_w": g_conv_w, "b_qkv": g_bqkv, "b_o": g_bo,
    }
    small_order = replicated + small_sharded
    small_pack = _as_rows(jnp.concatenate([small_grads[n].reshape(-1) for n in small_order]))

    def update(n, parts, moves=()):
        shape = W[n].shape
        as3 = lambda a: a.reshape((len(parts),) + parts[0].shape[1:])
        res = _adamw(parts, as3(W[n]), as3(M[n]), as3(V[n]), name="adamw_" + n, moves=moves)
        res, landed = res if moves else (res, [])
        return [a.reshape(shape) for a in res], landed

    out = {}
    out["w_o"], (small_recv,) = update("w_o", [r_wo], moves=[gather(small_pack)])
    out["w_down"], _ = update("w_down", [r_down0, r_down1])
    out["w_up"], _ = update("w_up", [r_up0, r_up1])
    out["w_out_even"], _ = update("w_out_even", [r_wout])
    out["w_qkv"], _ = update("w_qkv", [r_wqkv])
    out["w_in_even"], _ = update("w_in_even", [r_w_in])

    sizes = [small_grads[n].size for n in small_order]
    starts = [sum(sizes[:i]) for i in range(len(sizes))]
    n_rep = sum(small_grads[n].size for n in replicated)
    small_flat_recv = small_recv.reshape(N_DEV, -1)
    rep_parts = jnp.stack([_as_rows(small_flat_recv[d, :n_rep]) for d in range(N_DEV)])
    flat_rep = lambda tree: _as_rows(jnp.concatenate([tree[n].reshape(-1) for n in replicated]))[None]
    rep_res = _adamw([rep_parts], flat_rep(W), flat_rep(M), flat_rep(V), name="adamw_replicated")
    shard_parts = []
    for n, st, sz in zip(small_order, starts, sizes):
        if n in small_sharded:
            full = small_flat_recv[:, st:st + sz].reshape((N_DEV,) + small_grads[n].shape)
            c = full.shape[-1] // N_DEV
            shard_parts.append(lax.dynamic_slice_in_dim(full, me * c, c, axis=full.ndim - 1).reshape(N_DEV, -1))
    shard_parts = jnp.concatenate(shard_parts, axis=1)
    sh_rows = jnp.stack([_as_rows(shard_parts[d]) for d in range(N_DEV)])
    flat_sh = lambda tree: _as_rows(jnp.concatenate([tree[n].reshape(-1) for n in small_sharded]))[None]
    sh_res = _adamw([sh_rows], flat_sh(W), flat_sh(M), flat_sh(V), name="adamw_small_sharded")

    def unpack(rows, ns):
        flat, res, o = rows.reshape(-1), {}, 0
        for n in ns:
            res[n] = flat[o:o + W[n].size].reshape(W[n].shape)
            o += W[n].size
        return res

    results = []
    for idx in range(4):
        d = {n: out[n][idx] for n in big}
        d.update(unpack(rep_res[idx], replicated))
        d.update(unpack(sh_res[idx], small_sharded))
        results.append(d)

    loss = lax.psum(loss_part[0, 0], ("x", "y", "c"))
    grad_x = dx.reshape(x.shape)
    final = [loss, grad_x]
    for d in results:
        final.extend(d[n] for n in names)
    return tuple(final)
```

```python
import dataclasses
import functools

import jax
import jax.numpy as jnp
from jax import lax
from jax.experimental import pallas as pl
from jax.experimental.pallas import tpu as pltpu

F32 = jnp.float32
BF16 = jnp.bfloat16

N_DEV = 8
D_MODEL = 1024
D_FF = 4096
RMS_EPS = 1e-5
LN_EPS = 1e-5
CHUNK = 128
GM_GROUPS = 8
SSM_HEADS = 16
SSM_HEADDIM = 64
SSM_GROUPS = 4
SSM_STATE = 128
SSM_CONV = 4
CONV_DIM = 2048
IN_EVEN = 5136
REST_W = 3200
ATTN_HEADS = 16
ATTN_KV = 2
HEAD_DIM = 64
QKV_DIM = 1280
LANES = 128
HALO = 8
PACK_W = 1024

ADAM_LR = 0.001
ADAM_B1 = 0.9
ADAM_B2 = 0.999
ADAM_EPS = 1e-08
ADAM_WD = 0.01
ADAM_STEP = 10

VMEM_LIMIT_BYTES = 56 * 1024 * 1024


_NN = (((1,), (0,)), ((), ()))
_NT = (((1,), (1,)), ((), ()))
_TN = (((0,), (0,)), ((), ()))


def _dg(a, b, dims):
    return lax.dot_general(a.astype(BF16), b.astype(BF16), dims, preferred_element_type=F32)


@jax.custom_vjp
def _nn(a, b):
    return _dg(a, b, _NN)


@jax.custom_vjp
def _nt(a, b):
    return _dg(a, b, _NT)


@jax.custom_vjp
def _tn(a, b):
    return _dg(a, b, _TN)


_nn.defvjp(lambda a, b: (_dg(a, b, _NN), (a, b)), lambda r, g: (_nt(g, r[1]), _tn(r[0], g)))
_nt.defvjp(lambda a, b: (_dg(a, b, _NT), (a, b)), lambda r, g: (_nn(g, r[1]), _tn(g, r[0])))
_tn.defvjp(lambda a, b: (_dg(a, b, _TN), (a, b)), lambda r, g: (_nt(r[1], g), _nn(r[0], g)))


def _split3_dot(tri, x):
    x1 = x.astype(BF16)
    r1 = x - x1.astype(F32)
    x2 = r1.astype(BF16)
    x3 = (r1 - x2.astype(F32)).astype(BF16)
    t = tri.astype(BF16)
    dot = lambda p: lax.dot_general(t, p, _NN, preferred_element_type=F32)
    return dot(x1) + dot(x2) + dot(x3)


def _tri(lower):
    r = lax.broadcasted_iota(jnp.int32, (CHUNK, CHUNK), 0)
    c = lax.broadcasted_iota(jnp.int32, (CHUNK, CHUNK), 1)
    return jnp.where((r >= c) if lower else (r <= c), 1.0, 0.0).astype(F32)


@jax.custom_vjp
def _cumsum_rows(x):
    return _split3_dot(_tri(True), x)


_cumsum_rows.defvjp(lambda x: (_split3_dot(_tri(True), x), None), lambda _, g: (_split3_dot(_tri(False), g),))


def _sigmoid(x):
    return 1.0 / (1.0 + jnp.exp(-x))


def _silu(x):
    return x * _sigmoid(x)


def _softplus(x):
    return jnp.maximum(x, 0.0) + jnp.log(1.0 + jnp.exp(-jnp.abs(x)))


def _gelu_tanh(x):
    return 0.5 * x * (1.0 + jnp.tanh(0.7978845608028654 * (x + 0.044715 * (x * x * x))))


def _rmsnorm(x, g):
    return x * lax.rsqrt(jnp.mean(x * x, axis=-1, keepdims=True) + RMS_EPS) * g


def _gmlp_chunk(u, v, ln_g, ln_b, w_s, b_s):
    gu = _gelu_tanh(u)
    gv = _gelu_tanh(v)
    mu = jnp.mean(gv, axis=-1, keepdims=True)
    var = jnp.mean(jnp.square(gv - mu), axis=-1, keepdims=True)
    vn = (gv - mu) * lax.rsqrt(var + LN_EPS) * ln_g + ln_b
    r = lax.broadcasted_iota(jnp.int32, (CHUNK, CHUNK), 0)
    c = lax.broadcasted_iota(jnp.int32, (CHUNK, CHUNK), 1)
    causal = r >= c
    outs = []
    for g in range(GM_GROUPS):
        cols = slice(g * LANES, (g + 1) * LANES)
        mixed = _nn(jnp.where(causal, w_s[g], 0.0), vn[:, cols]) + b_s[g]
        outs.append(gu[:, cols] * mixed)
    return jnp.concatenate(outs, axis=1)


def _lane_pick(row, h):
    lane = lax.broadcasted_iota(jnp.int32, row.shape, 1)
    return jnp.sum(jnp.where(lane == h, row, 0.0), axis=1, keepdims=True)


def _col_pick(m, h):
    lane = lax.broadcasted_iota(jnp.int32, m.shape, 1)
    return jnp.sum(jnp.where(lane == h, m, 0.0), axis=1, keepdims=True)


def _row_pick(m, h):
    sub = lax.broadcasted_iota(jnp.int32, m.shape, 0)
    return jnp.sum(jnp.where(sub == h, m, 0.0), axis=0, keepdims=True)


_PAIRS = SSM_HEADS // 2


def _ssd_chunk(pre, z, dt_raw, h_prev, dt_bias, a_log, d_skip, norm_g):
    xbc = _silu(pre)
    dt = _softplus(dt_raw + dt_bias)
    da = dt * (-jnp.exp(a_log))
    a_cum = _cumsum_rows(da)
    a_cum_t = a_cum.T
    dt_t = dt.T
    r = lax.broadcasted_iota(jnp.int32, (CHUNK, CHUNK), 0)
    c = lax.broadcasted_iota(jnp.int32, (CHUNK, CHUNK), 1)
    causal = r >= c
    lane_lo = lax.broadcasted_iota(jnp.int32, (1, LANES), 1) < SSM_HEADDIM
    last_row = lax.broadcasted_iota(jnp.int32, (CHUNK, 1), 0) == CHUNK - 1
    ys, h_next = [], []
    for j in range(_PAIRS):
        g = j // 2
        xs = xbc[:, j * LANES:(j + 1) * LANES]
        bm = xbc[:, 1024 + g * SSM_STATE:1024 + (g + 1) * SSM_STATE]
        cm = xbc[:, 1536 + g * SSM_STATE:1536 + (g + 1) * SSM_STATE]
        cb = _nt(cm, bm)
        y_diag, to_end, e_cum, c_dec, d_row = [], [], [], [], []
        for h in (2 * j, 2 * j + 1):
            col = _col_pick(a_cum, h)
            row = _row_pick(a_cum_t, h)
            dt_col = _col_pick(dt, h)
            dt_row = _row_pick(dt_t, h)
            decay = jnp.exp(jnp.where(causal, col - row, -jnp.inf))
            y_diag.append(_nn(cb * decay * dt_row, xs))
            last = jnp.sum(jnp.where(last_row, col, 0.0), axis=0, keepdims=True)
            to_end.append(jnp.exp(last - col) * dt_col)
            e_cum.append(jnp.exp(col))
            c_dec.append(jnp.exp(last))
            d_row.append(_lane_pick(d_skip, h))
        pair = lambda lo_hi: jnp.where(lane_lo, lo_hi[0], lo_hi[1])
        states = _tn(bm, xs * pair(to_end))
        y_off = _nn(cm, h_prev[j]) * pair(e_cum)
        ys.append(pair(y_diag) + y_off + xs * pair(d_row))
        h_next.append(pair(c_dec) * h_prev[j] + states)
    y = jnp.concatenate(ys, axis=1) * _silu(z)
    width = D_MODEL // SSM_GROUPS
    y = jnp.concatenate(
        [_rmsnorm(y[:, g * width:(g + 1) * width], norm_g[:, g * width:(g + 1) * width]) for g in range(SSM_GROUPS)],
        axis=1)
    return y, tuple(h_next)


def _shift_down(prev8, x, k):
    if k == 0:
        return x
    win = jnp.concatenate([prev8, x], axis=0)
    return pltpu.roll(win, k, 0)[HALO:]


def _shift_up(x, next8, k):
    if k == 0:
        return x
    n = x.shape[0]
    win = jnp.concatenate([x, next8], axis=0)
    return pltpu.roll(win, n + HALO - k, 0)[:n]


def _conv_pre(prev8, x, w, b):
    out = b + x * w[SSM_CONV - 1:SSM_CONV]
    for i in range(SSM_CONV - 1):
        out = out + _shift_down(prev8, x, SSM_CONV - 1 - i) * w[i:i + 1]
    return out


def _attn_block(q, k_prev, k_cur, v_prev, v_cur, sink, first):
    k = jnp.concatenate([k_prev, k_cur], axis=0)
    v = jnp.concatenate([v_prev, v_cur], axis=0)
    s = _nt(q, k) * (HEAD_DIM ** -0.5)
    rows = lax.broadcasted_iota(jnp.int32, s.shape, 0) & (CHUNK - 1)
    cols = lax.broadcasted_iota(jnp.int32, s.shape, 1)
    valid = (cols <= rows + CHUNK) & (cols > rows) & (cols >= CHUNK * first.astype(jnp.int32))
    s = jnp.where(valid, s, -jnp.inf)
    m = jnp.maximum(jnp.max(s, axis=-1, keepdims=True), sink)
    p = jnp.exp(s - m)
    denom = jnp.sum(p, axis=-1, keepdims=True) + jnp.exp(sink - m)
    return _nn(p / denom, v)


N_CHIP = 4
N_CORE = 2
_OTHER_CHIPS = (2, 4, 6)


@dataclasses.dataclass
class _Move:
    kind: str
    src: jax.Array

    def dst_shape(self):
        s = self.src.shape
        shape = {"gather": (N_DEV,) + s, "gather_ici": (N_CHIP, N_CORE) + s, "gather_d2d": s,
                 "scatter_d2d": (N_CHIP,) + s[2:], "keep": (N_CHIP,) + s[2:], "scatter_ici": s}[self.kind]
        return jax.ShapeDtypeStruct(tuple(shape), self.src.dtype)


def _peer(x, y, c, k):
    return (1 - x if k & 4 else x, 1 - y if k & 2 else y, 1 - c if k & 1 else c)


def _move_copies(moves, srcs, dsts, send_sems, recv_sems, local_sems):
    x, y, c = lax.axis_index("x"), lax.axis_index("y"), lax.axis_index("c")
    chip = 2 * x + y
    me = 2 * chip + c
    sibling = (x, y, 1 - c)
    all_chips = pl.ds(0, N_CHIP)
    local, remote = [], []

    def push(n, k, src, dst, device):
        remote.append(pltpu.make_async_remote_copy(
            src_ref=src, dst_ref=dst, send_sem=send_sems.at[n, k], recv_sem=recv_sems.at[n, k],
            device_id=device, device_id_type=pl.DeviceIdType.MESH))

    for n, mv in enumerate(moves):
        s, d = srcs[n], dsts[n]
        if mv.kind == "gather":
            local.append(pltpu.make_async_copy(s, d.at[me], local_sems.at[n]))
            for k in range(1, N_DEV):
                push(n, k - 1, s, d.at[me], _peer(x, y, c, k))
        elif mv.kind == "gather_ici":
            local.append(pltpu.make_async_copy(s, d.at[chip, c], local_sems.at[n]))
            for k in _OTHER_CHIPS:
                push(n, k - 1, s, d.at[chip, c], _peer(x, y, c, k))
        elif mv.kind == "gather_d2d":
            push(n, 0, d.at[all_chips, c], d.at[all_chips, c], sibling)
        elif mv.kind == "scatter_d2d":
            push(n, 0, s.at[all_chips, 1 - c], d, sibling)
        elif mv.kind == "keep":
            local.append(pltpu.make_async_copy(s.at[all_chips, c], d, local_sems.at[n]))
        else:
            assert mv.kind == "scatter_ici", mv.kind
            local.append(pltpu.make_async_copy(s.at[chip], d.at[chip], local_sems.at[n]))
            for k in _OTHER_CHIPS:
                px, py, _ = _peer(x, y, c, k)
                push(n, k - 1, s.at[2 * px + py], d.at[chip], (px, py, c))
    return local, remote


def _move_aliases(moves, n_in, n_out):
    return {n_in + n: n_out + n for n, mv in enumerate(moves) if mv.kind == "gather_d2d"}


def _pcall(body, *, name, grid, in_specs, out_specs, out_shape, scratch_shapes=(), semantics=(), moves=(),
           aliases=None):
    out_shape, out_specs = list(out_shape), list(out_specs)
    in_specs = list(in_specs)
    if not moves:
        call = pl.pallas_call(
            body, name=name, grid=grid, in_specs=in_specs, out_specs=out_specs, out_shape=out_shape,
            scratch_shapes=list(scratch_shapes), input_output_aliases=aliases or {},
            compiler_params=pltpu.CompilerParams(dimension_semantics=tuple(semantics),
                                                 vmem_limit_bytes=VMEM_LIMIT_BYTES))
        return (lambda *args: (list(call(*args)), []))
    n_in, n_out, n_scr, n_mv = len(in_specs), len(out_shape), len(scratch_shapes), len(moves)
    hbm = pl.BlockSpec(memory_space=pltpu.HBM)

    def carrier(*refs):
        ins, rest = refs[:n_in], refs[n_in:]
        srcs, rest = rest[:n_mv], rest[n_mv:]
        outs, rest = rest[:n_out], rest[n_out:]
        dsts, rest = rest[:n_mv], rest[n_mv:]
        scr, (send_sems, recv_sems, local_sems) = rest[:n_scr], rest[n_scr:]
        first = functools.reduce(jnp.logical_and, [pl.program_id(d) == 0 for d in range(len(grid))])
        last = functools.reduce(jnp.logical_and, [pl.program_id(d) == grid[d] - 1 for d in range(len(grid))])

        @pl.when(first)
        def _():
            local, remote = _move_copies(moves, srcs, dsts, send_sems, recv_sems, local_sems)
            for cp in local + remote:
                cp.start()

        body(*ins, *outs, *scr)

        @pl.when(last)
        def _():
            local, remote = _move_copies(moves, srcs, dsts, send_sems, recv_sems, local_sems)
            for cp in remote + local:
                cp.wait()

    call = pl.pallas_call(
        carrier, name=name, grid=grid,
        in_specs=in_specs + [hbm] * n_mv,
        out_specs=out_specs + [hbm] * n_mv,
        out_shape=out_shape + [mv.dst_shape() for mv in moves],
        scratch_shapes=list(scratch_shapes) + [pltpu.SemaphoreType.DMA((n_mv, N_DEV - 1)),
                                               pltpu.SemaphoreType.DMA((n_mv, N_DEV - 1)),
                                               pltpu.SemaphoreType.DMA((n_mv,))],
        input_output_aliases={**(aliases or {}), **_move_aliases(moves, n_in, n_out)},
        compiler_params=pltpu.CompilerParams(dimension_semantics=("arbitrary",) * len(grid),
                                             vmem_limit_bytes=VMEM_LIMIT_BYTES))

    def run(*args):
        res = list(call(*args, *[mv.src for mv in moves]))
        return res[:n_out], res[n_out:]

    return run


def _exchange(moves, *, name):
    n_mv = len(moves)
    hbm = pl.BlockSpec(memory_space=pltpu.HBM)

    def body(*refs):
        srcs, dsts, (send_sems, recv_sems, local_sems) = refs[:n_mv], refs[n_mv:2 * n_mv], refs[2 * n_mv:]
        local, remote = _move_copies(moves, srcs, dsts, send_sems, recv_sems, local_sems)
        for cp in local + remote:
            cp.start()
        for cp in remote + local:
            cp.wait()

    return list(pl.pallas_call(
        body, name=name, in_specs=[hbm] * n_mv, out_specs=[hbm] * n_mv,
        out_shape=[mv.dst_shape() for mv in moves],
        scratch_shapes=[pltpu.SemaphoreType.DMA((n_mv, N_DEV - 1)), pltpu.SemaphoreType.DMA((n_mv, N_DEV - 1)),
                        pltpu.SemaphoreType.DMA((n_mv,))],
    )(*[mv.src for mv in moves]))


def _matmul(a, b, mode, *, name, outs, extras=(), epilogue=None, tm=1024, tn=1024, tk=1024, dims=None,
            b_spec=None, out_spec=None, out_shape=None, moves=()):
    if dims is not None:
        m, n, k = dims
    elif mode == "nn":
        (m, k), (_, n) = a.shape, b.shape
    elif mode == "nt":
        (m, k), (n, _) = a.shape, b.shape
    else:
        (k, m), (_, n) = a.shape, b.shape
    tm, tn, tk = min(tm, m), min(tn, n), min(tk, k)
    assert m % tm == 0 and n % tn == 0 and k % tk == 0, (name, m, n, k, tm, tn, tk)
    nk = k // tk
    contract = {"nn": _NN, "nt": _NT, "tn": _TN}[mode]
    if mode == "tn":
        a_spec = pl.BlockSpec((tk, tm), lambda i, j, kk: (kk, i))
    else:
        a_spec = pl.BlockSpec((tm, tk), lambda i, j, kk: (i, kk))
    if b_spec is None:
        if mode == "nt":
            b_spec = pl.BlockSpec((tn, tk), lambda i, j, kk: (j, kk))
        else:
            b_spec = pl.BlockSpec((tk, tn), lambda i, j, kk: (kk, j))
    tile_spec = pl.BlockSpec((tm, tn), lambda i, j, kk: (i, j))
    row_spec = pl.BlockSpec((1, tn), lambda i, j, kk: (0, j))
    extra_specs = [tile_spec if kind == "tile" else row_spec for _, kind in extras]
    n_extra, n_out = len(extras), len(outs)
    if epilogue is None:
        epilogue = lambda acc: (acc,)
    if out_spec is None:
        out_spec, out_shape = tile_spec, (m, n)

    def body(a_ref, b_ref, *rest):
        extra_refs, out_refs, acc_ref = rest[:n_extra], rest[n_extra:n_extra + n_out], rest[-1]
        kk = pl.program_id(2)

        @pl.when(kk == 0)
        def _():
            acc_ref[...] = jnp.zeros_like(acc_ref)

        acc_ref[...] += lax.dot_general(a_ref[...].astype(BF16), b_ref[...].astype(BF16), contract,
                                        preferred_element_type=F32)

        @pl.when(kk == nk - 1)
        def _():
            res = epilogue(acc_ref[...], *[e[...] for e in extra_refs])
            for val, o_ref in zip(res, out_refs):
                o_ref[...] = val.astype(o_ref.dtype)

    res, landed = _pcall(
        body, name=name, grid=(m // tm, n // tn, nk),
        in_specs=[a_spec, b_spec] + extra_specs,
        out_specs=[out_spec] * n_out,
        out_shape=[jax.ShapeDtypeStruct(out_shape, dt) for dt in outs],
        scratch_shapes=[pltpu.VMEM((tm, tn), F32)],
        semantics=("parallel", "parallel", "arbitrary"), moves=moves,
    )(a, b, *[e for e, _ in extras])
    res = res[0] if n_out == 1 else res
    return (res, landed) if moves else res


def _rms_fwd(h, g, *, name, tb=512, moves=()):
    t, d = h.shape

    def body(h_ref, g_ref, y_ref):
        y_ref[...] = _rmsnorm(h_ref[...], g_ref[...]).astype(y_ref.dtype)

    res, landed = _pcall(
        body, name=name, grid=(t // tb,),
        in_specs=[pl.BlockSpec((tb, d), lambda i: (i, 0)), pl.BlockSpec((1, d), lambda i: (0, 0))],
        out_specs=[pl.BlockSpec((tb, d), lambda i: (i, 0))],
        out_shape=[jax.ShapeDtypeStruct((t, d), BF16)],
        semantics=("parallel",), moves=moves,
    )(h, g)
    return (res[0], landed) if moves else res[0]


def _pair_add(a, b, *, name, tb=512):
    shape = a.shape
    a2, b2 = a.reshape(-1, shape[-1]), b.reshape(-1, shape[-1])
    r, c = a2.shape
    tb = min(tb, r)
    assert r % tb == 0, (name, r, tb)

    def body(a_ref, b_ref, o_ref):
        o_ref[...] = (a_ref[...].astype(F32) + b_ref[...].astype(F32)).astype(o_ref.dtype)

    blk = pl.BlockSpec((tb, c), lambda i: (i, 0))
    return _pcall(
        body, name=name, grid=(r // tb,), in_specs=[blk, blk], out_specs=[blk],
        out_shape=[jax.ShapeDtypeStruct((r, c), a.dtype)], semantics=("parallel",),
    )(a2, b2)[0][0].reshape(shape)


def _rms_bwd(dy, h, g, dres, *, name, tb=512, moves=()):
    t, d = h.shape

    def body(dy_ref, h_ref, g_ref, dres_ref, dh_ref, dg_ref, cs_ref):
        _, vjp = jax.vjp(_rmsnorm, h_ref[...], g_ref[...])
        dh, dg = vjp(dy_ref[...])
        dh = dh + dres_ref[...]
        dh_ref[...] = dh

        @pl.when(pl.program_id(0) == 0)
        def _():
            dg_ref[...] = jnp.zeros_like(dg_ref)
            cs_ref[...] = jnp.zeros_like(cs_ref)

        dg_ref[...] += dg
        cs_ref[...] += jnp.sum(dh, axis=0, keepdims=True)

    blk = pl.BlockSpec((tb, d), lambda i: (i, 0))
    row = pl.BlockSpec((1, d), lambda i: (0, 0))
    res, landed = _pcall(
        body, name=name, grid=(t // tb,),
        in_specs=[blk, blk, row, blk],
        out_specs=[blk, row, row],
        out_shape=[jax.ShapeDtypeStruct((t, d), F32), jax.ShapeDtypeStruct((1, d), F32),
                   jax.ShapeDtypeStruct((1, d), F32)],
        semantics=("arbitrary",), moves=moves,
    )(dy, h, g, dres)
    return (res, landed) if moves else res


def _colsum(a, *, name, tb=512):
    t, d = a.shape

    def body(a_ref, o_ref):
        @pl.when(pl.program_id(0) == 0)
        def _():
            o_ref[...] = jnp.zeros_like(o_ref)

        o_ref[...] += jnp.sum(a_ref[...].astype(F32), axis=0, keepdims=True)

    return _pcall(
        body, name=name, grid=(t // tb,),
        in_specs=[pl.BlockSpec((tb, d), lambda i: (i, 0))],
        out_specs=[pl.BlockSpec((1, d), lambda i: (0, 0))],
        out_shape=[jax.ShapeDtypeStruct((1, d), F32)],
        semantics=("arbitrary",),
    )(a)[0][0]


def _final_loss(h, g, target, *, name, tb=512):
    t, d = h.shape

    def body(h_ref, g_ref, tgt_ref, loss_ref, dh_ref, dg_ref):
        def f(hh, gg):
            err = jnp.square(_rmsnorm(hh, gg) - tgt_ref[...])
            return 0.5 * jnp.sum(jnp.mean(err, axis=-1, keepdims=True), axis=0, keepdims=True)

        val, vjp = jax.vjp(f, h_ref[...], g_ref[...])
        dh, dg = vjp(jnp.ones((1, 1), F32))
        dh_ref[...] = dh

        @pl.when(pl.program_id(0) == 0)
        def _():
            loss_ref[...] = jnp.zeros_like(loss_ref)
            dg_ref[...] = jnp.zeros_like(dg_ref)

        loss_ref[...] += val
        dg_ref[...] += dg

    blk = pl.BlockSpec((tb, d), lambda i: (i, 0))
    row = pl.BlockSpec((1, d), lambda i: (0, 0))
    return _pcall(
        body, name=name, grid=(t // tb,),
        in_specs=[blk, row, blk],
        out_specs=[pl.BlockSpec((8, LANES), lambda i: (0, 0)), blk, row],
        out_shape=[jax.ShapeDtypeStruct((8, LANES), F32), jax.ShapeDtypeStruct((t, d), F32),
                   jax.ShapeDtypeStruct((1, d), F32)],
        semantics=("arbitrary",),
    )(h, g, target)[0]


def _gmlp_fwd(proj_uv, ln_g, ln_b, w_s, b_s, *, name, moves=()):
    t = proj_uv.shape[0]
    w = D_MODEL

    def body(u_ref, v_ref, g_ref, b_ref, w_ref, bs_ref, o_ref):
        o_ref[...] = _gmlp_chunk(u_ref[...], v_ref[...], g_ref[...], b_ref[...], w_ref[...],
                                 bs_ref[...]).astype(o_ref.dtype)

    row = pl.BlockSpec((1, w), lambda i: (0, 0))
    res, landed = _pcall(
        body, name=name, grid=(t // CHUNK,),
        in_specs=[pl.BlockSpec((CHUNK, w), lambda i: (i, 0)), pl.BlockSpec((CHUNK, w), lambda i: (i, 1)), row, row,
                  pl.BlockSpec((GM_GROUPS, CHUNK, CHUNK), lambda i: (0, 0, 0)),
                  pl.BlockSpec((GM_GROUPS, CHUNK, 1), lambda i: (0, 0, 0))],
        out_specs=[pl.BlockSpec((CHUNK, w), lambda i: (i, 0))],
        out_shape=[jax.ShapeDtypeStruct((t, 2 * w), BF16)],
        semantics=("parallel",), moves=moves,
    )(proj_uv, proj_uv, ln_g, ln_b, w_s, b_s)
    return (res[0], landed) if moves else res[0]


def _gmlp_bwd(proj_uv, d_mix, ln_g, ln_b, w_s, b_s, *, name, moves=()):
    t = proj_uv.shape[0]
    w = D_MODEL

    def body(u_ref, v_ref, da_ref, g_ref, b_ref, w_ref, bs_ref, duv_ref, dg_ref, db_ref, dw_ref, dbs_ref):
        _, vjp = jax.vjp(_gmlp_chunk, u_ref[...], v_ref[...], g_ref[...], b_ref[...], w_ref[...], bs_ref[...])
        du, dv, dg, db, dw, dbs = vjp(da_ref[...])
        duv_ref[:, :w] = du.astype(duv_ref.dtype)
        duv_ref[:, w:] = dv.astype(duv_ref.dtype)

        @pl.when(pl.program_id(0) == 0)
        def _():
            dg_ref[...] = jnp.zeros_like(dg_ref)
            db_ref[...] = jnp.zeros_like(db_ref)
            dw_ref[...] = jnp.zeros_like(dw_ref)
            dbs_ref[...] = jnp.zeros_like(dbs_ref)

        dg_ref[...] += dg
        db_ref[...] += db
        dw_ref[...] += dw
        dbs_ref[...] += dbs

    row = pl.BlockSpec((1, w), lambda i: (0, 0))
    ws = pl.BlockSpec((GM_GROUPS, CHUNK, CHUNK), lambda i: (0, 0, 0))
    bs = pl.BlockSpec((GM_GROUPS, CHUNK, 1), lambda i: (0, 0, 0))
    res, landed = _pcall(
        body, name=name, grid=(t // CHUNK,),
        in_specs=[pl.BlockSpec((CHUNK, w), lambda i: (i, 0)), pl.BlockSpec((CHUNK, w), lambda i: (i, 1)),
                  pl.BlockSpec((CHUNK, w), lambda i: (i, 0)), row, row, ws, bs],
        out_specs=[pl.BlockSpec((CHUNK, 2 * w), lambda i: (i, 0)), row, row, ws, bs],
        out_shape=[jax.ShapeDtypeStruct((t, 2 * w), BF16), jax.ShapeDtypeStruct((1, w), F32),
                   jax.ShapeDtypeStruct((1, w), F32), jax.ShapeDtypeStruct((GM_GROUPS, CHUNK, CHUNK), F32),
                   jax.ShapeDtypeStruct((GM_GROUPS, CHUNK, 1), F32)],
        semantics=("arbitrary",), moves=moves,
    )(proj_uv, proj_uv, d_mix, ln_g, ln_b, w_s, b_s)
    return (res, landed) if moves else res


_HALO_PER_CHUNK = CHUNK // HALO
_DT_BLOCK = (CONV_DIM + D_MODEL) // LANES


def _ssd_fwd(proj_rest, mix, conv_w, conv_b, dt_bias, a_log, d_skip, norm_g, *, name, moves=()):
    t = proj_rest.shape[0]
    nc = t // CHUNK

    def body(x_ref, prev_ref, z_ref, dt_ref, mix_ref, cw_ref, cb_ref, dtb_ref, al_ref, ds_ref, ng_ref, y_ref, hs_ref,
             h_scr):
        del mix_ref
        i = pl.program_id(0)

        @pl.when(i == 0)
        def _():
            h_scr[...] = jnp.zeros_like(h_scr)

        prev8 = jnp.where(i == 0, 0.0, prev_ref[...])
        pre = _conv_pre(prev8, x_ref[...], cw_ref[...], cb_ref[...])
        hs_ref[0] = h_scr[...]
        h_prev = tuple(h_scr[j] for j in range(_PAIRS))
        y, h_next = _ssd_chunk(pre, z_ref[...], dt_ref[...], h_prev, dtb_ref[...], al_ref[...], ds_ref[...],
                               ng_ref[...])
        y_ref[...] = y.astype(y_ref.dtype)
        for j in range(_PAIRS):
            h_scr[j] = h_next[j]

    small = pl.BlockSpec((1, LANES), lambda i: (0, 0))
    res, landed = _pcall(
        body, name=name, grid=(nc,),
        in_specs=[pl.BlockSpec((CHUNK, CONV_DIM), lambda i: (i, 0)),
                  pl.BlockSpec((HALO, CONV_DIM), lambda i: (jnp.maximum(i * _HALO_PER_CHUNK - 1, 0), 0)),
                  pl.BlockSpec((CHUNK, D_MODEL), lambda i: (i, CONV_DIM // D_MODEL)),
                  pl.BlockSpec((CHUNK, LANES), lambda i: (i, _DT_BLOCK)),
                  pl.BlockSpec(memory_space=pl.ANY),
                  pl.BlockSpec((SSM_CONV, CONV_DIM), lambda i: (0, 0)),
                  pl.BlockSpec((1, CONV_DIM), lambda i: (0, 0)),
                  small, small, small, pl.BlockSpec((1, D_MODEL), lambda i: (0, 0))],
        out_specs=[pl.BlockSpec((CHUNK, D_MODEL), lambda i: (i, 1)),
                   pl.BlockSpec((1, _PAIRS, SSM_STATE, LANES), lambda i: (i, 0, 0, 0))],
        out_shape=[jax.ShapeDtypeStruct((t, 2 * D_MODEL), BF16),
                   jax.ShapeDtypeStruct((nc, _PAIRS, SSM_STATE, LANES), F32)],
        scratch_shapes=[pltpu.VMEM((_PAIRS, SSM_STATE, LANES), F32)],
        semantics=("arbitrary",), moves=moves, aliases={4: 0},
    )(proj_rest, proj_rest, proj_rest, proj_rest, mix, conv_w, conv_b, dt_bias, a_log, d_skip, norm_g)
    return (res, landed) if moves else res


def _ssd_bwd(proj_rest, h_states, d_mix, conv_w, conv_b, dt_bias, a_log, d_skip, norm_g, *, name, moves=()):
    t = proj_rest.shape[0]
    nc = t // CHUNK

    def body(x_ref, prev_ref, z_ref, dt_ref, hs_ref, dy_ref, cw_ref, cb_ref, dtb_ref, al_ref, ds_ref, ng_ref,
             dpre_ref, dz_ref, ddt_ref, ddtb_ref, dal_ref, dds_ref, dng_ref, dh_scr):
        i = pl.program_id(0)
        chunk = nc - 1 - i

        @pl.when(i == 0)
        def _():
            dh_scr[...] = jnp.zeros_like(dh_scr)
            ddtb_ref[...] = jnp.zeros_like(ddtb_ref)
            dal_ref[...] = jnp.zeros_like(dal_ref)
            dds_ref[...] = jnp.zeros_like(dds_ref)
            dng_ref[...] = jnp.zeros_like(dng_ref)

        prev8 = jnp.where(chunk == 0, 0.0, prev_ref[...])
        pre = _conv_pre(prev8, x_ref[...], cw_ref[...], cb_ref[...])
        h_prev = tuple(hs_ref[0, j] for j in range(_PAIRS))
        _, vjp = jax.vjp(_ssd_chunk, pre, z_ref[...], dt_ref[...], h_prev, dtb_ref[...], al_ref[...],
                         ds_ref[...], ng_ref[...])
        dpre, dz, ddt, dh_prev, ddtb, dal, dds, dng = vjp((dy_ref[...], tuple(dh_scr[j] for j in range(_PAIRS))))
        dpre_ref[...] = dpre
        dz_ref[...] = dz
        ddt_ref[...] = ddt
        for j in range(_PAIRS):
            dh_scr[j] = dh_prev[j]
        ddtb_ref[...] += ddtb
        dal_ref[...] += dal
        dds_ref[...] += dds
        dng_ref[...] += dng

    rev = lambda i: nc - 1 - i
    small = pl.BlockSpec((1, LANES), lambda i: (0, 0))
    wide = pl.BlockSpec((1, D_MODEL), lambda i: (0, 0))
    res, landed = _pcall(
        body, name=name, grid=(nc,),
        in_specs=[pl.BlockSpec((CHUNK, CONV_DIM), lambda i: (rev(i), 0)),
                  pl.BlockSpec((HALO, CONV_DIM), lambda i: (jnp.maximum(rev(i) * _HALO_PER_CHUNK - 1, 0), 0)),
                  pl.BlockSpec((CHUNK, D_MODEL), lambda i: (rev(i), CONV_DIM // D_MODEL)),
                  pl.BlockSpec((CHUNK, LANES), lambda i: (rev(i), _DT_BLOCK)),
                  pl.BlockSpec((1, _PAIRS, SSM_STATE, LANES), lambda i: (rev(i), 0, 0, 0)),
                  pl.BlockSpec((CHUNK, D_MODEL), lambda i: (rev(i), 1)),
                  pl.BlockSpec((SSM_CONV, CONV_DIM), lambda i: (0, 0)),
                  pl.BlockSpec((1, CONV_DIM), lambda i: (0, 0)),
                  small, small, small, wide],
        out_specs=[pl.BlockSpec((CHUNK, CONV_DIM), lambda i: (rev(i), 0)),
                   pl.BlockSpec((CHUNK, D_MODEL), lambda i: (rev(i), 0)),
                   pl.BlockSpec((CHUNK, LANES), lambda i: (rev(i), 0)),
                   small, small, small, wide],
        out_shape=[jax.ShapeDtypeStruct((t, CONV_DIM), F32), jax.ShapeDtypeStruct((t, D_MODEL), F32),
                   jax.ShapeDtypeStruct((t, LANES), F32),
                   jax.ShapeDtypeStruct((1, LANES), F32), jax.ShapeDtypeStruct((1, LANES), F32),
                   jax.ShapeDtypeStruct((1, LANES), F32), jax.ShapeDtypeStruct((1, D_MODEL), F32)],
        scratch_shapes=[pltpu.VMEM((_PAIRS, SSM_STATE, LANES), F32)],
        semantics=("arbitrary",), moves=moves,
    )(proj_rest, proj_rest, proj_rest, proj_rest, h_states, d_mix, conv_w, conv_b, dt_bias, a_log, d_skip, norm_g)
    return (res, landed) if moves else res


def _conv_bwd(proj_rest, dpre, dz, ddt, conv_w, *, name, tb=256, moves=()):
    t = proj_rest.shape[0]
    nb = t // tb
    per = tb // HALO

    def body(x_ref, prev_ref, dpre_ref, next_ref, dz_ref, ddt_ref, cw_ref, drest_ref, dcw_ref, dcb_ref):
        i = pl.program_id(0)

        @pl.when(i == 0)
        def _():
            dcw_ref[...] = jnp.zeros_like(dcw_ref)
            dcb_ref[...] = jnp.zeros_like(dcb_ref)

        x = x_ref[...]
        dp = dpre_ref[...]
        w = cw_ref[...]
        prev8 = jnp.where(i == 0, 0.0, prev_ref[...])
        next8 = jnp.where(i == nb - 1, 0.0, next_ref[...])
        dx = dp * w[SSM_CONV - 1:SSM_CONV]
        for j in range(SSM_CONV - 1):
            dx = dx + _shift_up(dp, next8, SSM_CONV - 1 - j) * w[j:j + 1]
        drest_ref[:, :CONV_DIM] = dx.astype(drest_ref.dtype)
        drest_ref[:, CONV_DIM:CONV_DIM + D_MODEL] = dz_ref[...].astype(drest_ref.dtype)
        drest_ref[:, CONV_DIM + D_MODEL:] = ddt_ref[...].astype(drest_ref.dtype)
        for j in range(SSM_CONV):
            dcw_ref[j:j + 1, :] += jnp.sum(dp * _shift_down(prev8, x, SSM_CONV - 1 - j), axis=0, keepdims=True)
        dcb_ref[...] += jnp.sum(dp, axis=0, keepdims=True)

    res, landed = _pcall(
        body, name=name, grid=(nb,),
        in_specs=[pl.BlockSpec((tb, CONV_DIM), lambda i: (i, 0)),
                  pl.BlockSpec((HALO, CONV_DIM), lambda i: (jnp.maximum(i * per - 1, 0), 0)),
                  pl.BlockSpec((tb, CONV_DIM), lambda i: (i, 0)),
                  pl.BlockSpec((HALO, CONV_DIM), lambda i: (jnp.minimum((i + 1) * per, nb * per - 1), 0)),
                  pl.BlockSpec((tb, D_MODEL), lambda i: (i, 0)),
                  pl.BlockSpec((tb, LANES), lambda i: (i, 0)),
                  pl.BlockSpec((SSM_CONV, CONV_DIM), lambda i: (0, 0))],
        out_specs=[pl.BlockSpec((tb, REST_W), lambda i: (i, 0)),
                   pl.BlockSpec((SSM_CONV, CONV_DIM), lambda i: (0, 0)),
                   pl.BlockSpec((1, CONV_DIM), lambda i: (0, 0))],
        out_shape=[jax.ShapeDtypeStruct((t, REST_W), BF16), jax.ShapeDtypeStruct((SSM_CONV, CONV_DIM), F32),
                   jax.ShapeDtypeStruct((1, CONV_DIM), F32)],
        semantics=("arbitrary",), moves=moves,
    )(proj_rest, proj_rest, dpre, dpre, dz, ddt, conv_w)
    return (res, landed) if moves else res


_Q_PER_KV = ATTN_HEADS // ATTN_KV


def _attn_fwd(q, k, v, sink_col, *, name):
    t = q.shape[1]
    nb = t // CHUNK

    def body(q_ref, kp_ref, kc_ref, vp_ref, vc_ref, s_ref, o_ref):
        first = pl.program_id(0) == 0
        for j in range(ATTN_KV):
            qj = q_ref[j * _Q_PER_KV:(j + 1) * _Q_PER_KV].reshape(_Q_PER_KV * CHUNK, HEAD_DIM)
            o = _attn_block(qj, kp_ref[j], kc_ref[j], vp_ref[j], vc_ref[j], s_ref[j], first)
            o_ref[j * _Q_PER_KV:(j + 1) * _Q_PER_KV] = o.reshape(_Q_PER_KV, CHUNK, HEAD_DIM).astype(o_ref.dtype)

    cur = lambda i: (0, i, 0)
    prev = lambda i: (0, jnp.maximum(i - 1, 0), 0)
    kv = (ATTN_KV, CHUNK, HEAD_DIM)
    return _pcall(
        body, name=name, grid=(nb,),
        in_specs=[pl.BlockSpec((ATTN_HEADS, CHUNK, HEAD_DIM), cur), pl.BlockSpec(kv, prev), pl.BlockSpec(kv, cur),
                  pl.BlockSpec(kv, prev), pl.BlockSpec(kv, cur),
                  pl.BlockSpec((ATTN_KV, _Q_PER_KV * CHUNK, 1), lambda i: (0, 0, 0))],
        out_specs=[pl.BlockSpec((ATTN_HEADS, CHUNK, HEAD_DIM), cur)],
        out_shape=[jax.ShapeDtypeStruct((ATTN_HEADS, t, HEAD_DIM), BF16)],
        semantics=("parallel",),
    )(q, k, k, v, v, sink_col)[0][0]


def _attn_bwd(q, k, v, sink_col, d_o, *, name, moves=()):
    t = q.shape[1]
    nb = t // CHUNK

    def body(q_ref, kp_ref, kc_ref, vp_ref, vc_ref, s_ref, do_ref, dq_ref, dk_ref, dv_ref, ds_ref, dk_scr, dv_scr):
        i = pl.program_id(0)
        first = i == nb - 1

        @pl.when(i == 0)
        def _():
            dk_scr[...] = jnp.zeros_like(dk_scr)
            dv_scr[...] = jnp.zeros_like(dv_scr)
            ds_ref[...] = jnp.zeros_like(ds_ref)

        for j in range(ATTN_KV):
            heads = slice(j * _Q_PER_KV, (j + 1) * _Q_PER_KV)
            qj = q_ref[heads].reshape(_Q_PER_KV * CHUNK, HEAD_DIM)
            doj = do_ref[heads].reshape(_Q_PER_KV * CHUNK, HEAD_DIM)
            _, vjp = jax.vjp(functools.partial(_attn_block, first=first), qj, kp_ref[j], kc_ref[j], vp_ref[j],
                             vc_ref[j], s_ref[j])
            dq, dkp, dkc, dvp, dvc, dsink = vjp(doj)
            dq_ref[heads] = dq.reshape(_Q_PER_KV, CHUNK, HEAD_DIM)
            dk_ref[j] = dkc + dk_scr[j]
            dv_ref[j] = dvc + dv_scr[j]
            dk_scr[j] = dkp
            dv_scr[j] = dvp
            ds_ref[j] += dsink

    cur = lambda i: (0, nb - 1 - i, 0)
    prev = lambda i: (0, jnp.maximum(nb - 2 - i, 0), 0)
    kv = (ATTN_KV, CHUNK, HEAD_DIM)
    qs = (ATTN_HEADS, CHUNK, HEAD_DIM)
    sk = pl.BlockSpec((ATTN_KV, _Q_PER_KV * CHUNK, 1), lambda i: (0, 0, 0))
    res, landed = _pcall(
        body, name=name, grid=(nb,),
        in_specs=[pl.BlockSpec(qs, cur), pl.BlockSpec(kv, prev), pl.BlockSpec(kv, cur), pl.BlockSpec(kv, prev),
                  pl.BlockSpec(kv, cur), sk, pl.BlockSpec(qs, cur)],
        out_specs=[pl.BlockSpec(qs, cur), pl.BlockSpec(kv, cur), pl.BlockSpec(kv, cur), sk],
        out_shape=[jax.ShapeDtypeStruct((ATTN_HEADS, t, HEAD_DIM), F32), jax.ShapeDtypeStruct((ATTN_KV, t, HEAD_DIM), F32),
                   jax.ShapeDtypeStruct((ATTN_KV, t, HEAD_DIM), F32),
                   jax.ShapeDtypeStruct((ATTN_KV, _Q_PER_KV * CHUNK, 1), F32)],
        scratch_shapes=[pltpu.VMEM(kv, F32), pltpu.VMEM(kv, F32)],
        semantics=("arbitrary",), moves=moves,
    )(q, k, k, v, v, sink_col, d_o)
    return (res, landed) if moves else res


def _adamw(parts, w, m, v, *, name, tb=256, moves=()):
    layers, r, c = w.shape
    n = parts[0].shape[0]
    tb = min(tb, r)
    assert r % tb == 0 and len(parts) == layers, (name, r, tb)
    nb = r // tb

    def body(*refs):
        p_refs = refs[:layers]
        w_ref, m_ref, v_ref, g_ref, d_ref, nm_ref, nv_ref = refs[layers:]
        for layer in range(layers):
            @pl.when(pl.program_id(0) == layer)
            def _(p_ref=p_refs[layer]):
                g = p_ref[0].astype(F32)
                for s in range(1, n):
                    g = g + p_ref[s].astype(F32)
                m_new = ADAM_B1 * m_ref[...] + (1.0 - ADAM_B1) * g
                v_new = ADAM_B2 * v_ref[...] + (1.0 - ADAM_B2) * jnp.square(g)
                m_hat = m_new / (1.0 - ADAM_B1 ** ADAM_STEP)
                v_hat = v_new / (1.0 - ADAM_B2 ** ADAM_STEP)
                g_ref[...] = g
                d_ref[...] = -ADAM_LR * (m_hat / (jnp.sqrt(v_hat) + ADAM_EPS) + ADAM_WD * w_ref[...])
                nm_ref[...] = m_new
                nv_ref[...] = v_new

    part_spec = lambda layer: pl.BlockSpec(
        (n, tb, c), lambda l, i: (0, jnp.clip(i + (l - layer) * nb, 0, nb - 1), 0))
    blk = pl.BlockSpec((None, tb, c), lambda l, i: (l, i, 0))
    res, landed = _pcall(
        body, name=name, grid=(layers, nb),
        in_specs=[part_spec(layer) for layer in range(layers)] + [blk, blk, blk],
        out_specs=[blk] * 4,
        out_shape=[jax.ShapeDtypeStruct((layers, r, c), F32)] * 4,
        semantics=("arbitrary", "arbitrary"), moves=moves,
    )(*parts, w, m, v)
    return (res, landed) if moves else res


def _as_rows(a):
    flat = a.reshape(-1)
    pad = (-flat.shape[0]) % PACK_W
    if pad:
        flat = jnp.pad(flat, (0, pad))
    return flat.reshape(-1, PACK_W)


def _cols_from_shards(g):
    return jnp.transpose(g, (1, 0, 2)).reshape(g.shape[1], -1)


def _cols_to_shards(a):
    return jnp.transpose(a.reshape(a.shape[0], N_DEV, -1), (1, 0, 2))


def _pad_lanes(a):
    return jnp.pad(a, ((0, 0), (0, LANES - a.shape[1])))


def kernel(x, norm_mix_g, norm_mlp_g, final_norm_g, w_in_even, w_out_even, gm_ln_g, gm_ln_b, gm_w_s, gm_b_s, ssm_conv_w, ssm_conv_b, ssm_dt_bias, ssm_a_log, ssm_d, ssm_norm_g, w_qkv, b_qkv, w_o, b_o, attn_sinks, w_up, w_down, loss_target, m_norm_mix_g, m_norm_mlp_g, m_final_norm_g, m_w_in_even, m_w_out_even, m_gm_ln_g, m_gm_ln_b, m_gm_w_s, m_gm_b_s, m_ssm_conv_w, m_ssm_conv_b, m_ssm_dt_bias, m_ssm_a_log, m_ssm_d, m_ssm_norm_g, m_w_qkv, m_b_qkv, m_w_o, m_b_o, m_attn_sinks, m_w_up, m_w_down, v_norm_mix_g, v_norm_mlp_g, v_final_norm_g, v_w_in_even, v_w_out_even, v_gm_ln_g, v_gm_ln_b, v_gm_w_s, v_gm_b_s, v_ssm_conv_w, v_ssm_conv_b, v_ssm_dt_bias, v_ssm_a_log, v_ssm_d, v_ssm_norm_g, v_w_qkv, v_b_qkv, v_w_o, v_b_o, v_attn_sinks, v_w_up, v_w_down):
    names = ["norm_mix_g", "norm_mlp_g", "final_norm_g", "w_in_even", "w_out_even", "gm_ln_g", "gm_ln_b", "gm_w_s",
             "gm_b_s", "ssm_conv_w", "ssm_conv_b", "ssm_dt_bias", "ssm_a_log", "ssm_d", "ssm_norm_g", "w_qkv",
             "b_qkv", "w_o", "b_o", "attn_sinks", "w_up", "w_down"]
    env = locals()
    W = {n: env[n] for n in names}
    M = {n: env["m_" + n] for n in names}
    V = {n: env["v_" + n] for n in names}
    big = ["w_in_even", "w_out_even", "w_qkv", "w_o", "w_up", "w_down"]
    small_sharded = ["ssm_conv_w", "b_qkv", "b_o"]
    replicated = [n for n in names if n not in big and n not in small_sharded]
    me = 4 * lax.axis_index("x") + 2 * lax.axis_index("y") + lax.axis_index("c")
    t = x.shape[1]
    xs = x.reshape(t, D_MODEL)
    target = loss_target.reshape(t, D_MODEL)
    gather = lambda a: _Move("gather", a)
    over_ici = lambda a: _Move("gather_ici", a)
    over_d2d = lambda a: _Move("gather_d2d", a)
    by_core = lambda a: a.reshape((N_CHIP, N_CORE) + a.shape[1:])
    to_sibling = lambda a: [_Move("scatter_d2d", by_core(a)), _Move("keep", by_core(a))]
    to_chips = lambda a: _Move("scatter_ici", a)
    whole = lambda a: a.reshape((N_DEV,) + a.shape[2:])
    row = lambda a: a.reshape(1, D_MODEL)
    add = lambda acc, res: (acc + res,)

    small_flat = jnp.concatenate([W[n].reshape(-1) for n in small_sharded])
    w_in_g, small_g = _exchange([over_ici(w_in_even[0].astype(BF16)), gather(_as_rows(small_flat))],
                                name="gather_w_in")
    y0, (w_in_g,) = _rms_fwd(xs, row(norm_mix_g[0]), name="rms_mix0", moves=[over_d2d(w_in_g)])
    w_in = _cols_from_shards(whole(w_in_g))
    w_uv = w_in[:, :2 * D_MODEL]
    w_rest = jnp.concatenate([w_in[:, 3 * D_MODEL:3 * D_MODEL + CONV_DIM], w_in[:, 2 * D_MODEL:3 * D_MODEL],
                              w_in[:, 3 * D_MODEL + CONV_DIM:],
                              jnp.zeros((D_MODEL, LANES - SSM_HEADS), BF16)], axis=1)
    small_all = small_g.reshape(N_DEV, -1)
    n_cw = SSM_CONV * CONV_DIM // N_DEV
    n_bq = QKV_DIM // N_DEV
    conv_w = _cols_from_shards(small_all[:, :n_cw].reshape(N_DEV, SSM_CONV, CONV_DIM // N_DEV))
    bqkv = small_all[:, n_cw:n_cw + n_bq].reshape(1, QKV_DIM)
    bo = small_all[:, n_cw + n_bq:n_cw + n_bq + D_MODEL // N_DEV].reshape(1, D_MODEL)

    conv_b = ssm_conv_b.reshape(1, CONV_DIM)
    dt_bias, a_log, d_skip = _pad_lanes(ssm_dt_bias), _pad_lanes(ssm_a_log), _pad_lanes(ssm_d)
    gm_w = gm_w_s[0]
    gm_b = gm_b_s[0].reshape(GM_GROUPS, CHUNK, 1)
    sink_col = jnp.repeat(attn_sinks.reshape(ATTN_HEADS), CHUNK).reshape(ATTN_KV, _Q_PER_KV * CHUNK, 1)
    w_up_b, w_down_b = w_up.astype(BF16), w_down.astype(BF16)

    up_cols = pl.BlockSpec((None, D_MODEL, D_FF // N_DEV), lambda i, j, kk: (j, kk, 0))
    up_cols_t = pl.BlockSpec((None, D_MODEL, D_FF // N_DEV), lambda i, j, kk: (kk, j, 0))
    down_rows = pl.BlockSpec((None, D_FF // N_DEV, D_MODEL), lambda i, j, kk: (kk, 0, j))
    down_rows_t = pl.BlockSpec((None, D_FF // N_DEV, D_MODEL), lambda i, j, kk: (j, 0, kk))
    ff_n = D_FF // N_DEV

    proj_uv, (w_out_g,) = _matmul(y0, w_uv, "nn", name="proj_uv", outs=[F32],
                                  moves=[over_ici(w_out_even[0].astype(BF16))])
    proj_rest, (w_up0_g,) = _matmul(y0, w_rest, "nn", name="proj_rest", outs=[F32], tn=640,
                                    moves=[over_ici(w_up_b[0])])
    mix, (w_qkv_g, w_o_g, w_out_g, w_up0_g) = _gmlp_fwd(
        proj_uv, gm_ln_g, gm_ln_b, gm_w, gm_b, name="gmlp_fwd",
        moves=[over_ici(w_qkv[0].astype(BF16)), over_ici(w_o[0].astype(BF16)), over_d2d(w_out_g), over_d2d(w_up0_g)])
    (mix, h_states), (w_down0_g,) = _ssd_fwd(
        proj_rest, mix, conv_w, conv_b, dt_bias, a_log, d_skip, ssm_norm_g, name="ssd_fwd",
        moves=[over_ici(w_down_b[0])])
    w_out_f = whole(w_out_g).reshape(2 * D_MODEL, D_MODEL)
    h1, (w_down0_g, w_qkv_g, w_o_g) = _matmul(
        mix, w_out_f, "nn", name="mix_out", outs=[F32], extras=[(xs, "tile")], epilogue=add,
        moves=[over_d2d(w_down0_g), over_d2d(w_qkv_g), over_d2d(w_o_g)])

    def mlp_fwd(h, layer, w_up_layer, w_down_layer, up_moves=(), down_moves=lambda landed: ()):
        y = _rms_fwd(h, row(norm_mlp_g[layer]), name=f"rms_mlp{layer}")
        res = _matmul(y, whole(w_up_layer), "nn", name=f"mlp_up{layer}", outs=[F32, BF16],
                      epilogue=lambda acc: (acc, jnp.square(jnp.maximum(acc, 0.0))),
                      dims=(t, D_FF, D_MODEL), tn=ff_n, b_spec=up_cols, moves=up_moves)
        (up, act), up_landed = res if up_moves else (res, [])
        res = _matmul(act, whole(w_down_layer), "nn", name=f"mlp_down{layer}", outs=[F32], extras=[(h, "tile")],
                      epilogue=add, dims=(t, D_MODEL, D_FF), tk=ff_n, b_spec=down_rows, moves=down_moves(up_landed))
        h_new, down_landed = res if down_moves(up_landed) else (res, [])
        return y, up, act, h_new, up_landed, down_landed

    y1, up0, act0, h2, (w_up1_g,), (w_down1_g, w_up1_g) = mlp_fwd(
        h1, 0, w_up0_g, w_down0_g, up_moves=[over_ici(w_up_b[1])],
        down_moves=lambda landed: [over_ici(w_down_b[1]), over_d2d(landed[0])])
    y2, (w_down1_g,) = _rms_fwd(h2, row(norm_mix_g[1]), name="rms_mix1", moves=[over_d2d(w_down1_g)])
    w_up_g = [whole(w_up0_g), whole(w_up1_g)]
    w_down_g = [whole(w_down0_g), whole(w_down1_g)]
    wqkv = _cols_from_shards(whole(w_qkv_g))
    wo = whole(w_o_g).reshape(D_MODEL, D_MODEL)
    qkv = _matmul(y2, wqkv, "nn", name="qkv", outs=[F32], extras=[(bqkv, "row")], epilogue=lambda acc, b: (acc + b,),
                  tn=640)
    heads = lambda a, n: jnp.transpose(a.reshape(t, n, HEAD_DIM), (1, 0, 2))
    q = heads(qkv[:, :D_MODEL], ATTN_HEADS)
    k = heads(qkv[:, D_MODEL:D_MODEL + ATTN_KV * HEAD_DIM], ATTN_KV)
    v = heads(qkv[:, D_MODEL + ATTN_KV * HEAD_DIM:], ATTN_KV)
    attn = _attn_fwd(q, k, v, sink_col, name="attn_fwd")
    attn = jnp.transpose(attn, (1, 0, 2)).reshape(t, D_MODEL)
    h3 = _matmul(attn, wo, "nn", name="attn_out", outs=[F32], extras=[(h2, "tile"), (bo, "row")],
                 epilogue=lambda acc, res, b: (acc + res + b,))
    y3, up1, act1, h4, _, _ = mlp_fwd(h3, 1, w_up1_g, w_down1_g)
    loss_part, dh4, d_final_g = _final_loss(h4, row(final_norm_g), target, name="final_loss")

    by_dev_rows = lambda a: a.reshape((N_DEV, a.shape[0] // N_DEV) + a.shape[1:])

    def mlp_bwd(dh, h, y, up, act, layer, first_moves=()):
        res = _matmul(dh, w_down_g[layer], "nt", name=f"mlp_down_dx{layer}", outs=[BF16], extras=[(up, "tile")],
                      epilogue=lambda acc, u: (acc * (2.0 * jnp.maximum(u, 0.0)),),
                      dims=(t, D_FF, D_MODEL), tn=ff_n, b_spec=down_rows_t, moves=first_moves)
        d_up, first_landed = res if first_moves else (res, [])
        g_down = _matmul(act, dh, "tn", name=f"mlp_down_dw{layer}", outs=[BF16])
        g_up, (theirs, mine) = _matmul(y, d_up, "tn", name=f"mlp_up_dw{layer}", outs=[BF16],
                                       dims=(D_MODEL, D_FF, t), tn=ff_n,
                                       out_spec=pl.BlockSpec((None, D_MODEL, ff_n), lambda i, j, kk: (j, i, 0)),
                                       out_shape=(N_DEV, D_MODEL, ff_n), moves=to_sibling(by_dev_rows(g_down)))
        q_down = _pair_add(mine, theirs, name=f"mlp_down_pair{layer}")
        dy, (r_down, theirs, mine) = _matmul(d_up, w_up_g[layer], "nt", name=f"mlp_up_dx{layer}", outs=[F32],
                                             dims=(t, D_MODEL, D_FF), tk=ff_n, b_spec=up_cols_t,
                                             moves=[to_chips(q_down)] + to_sibling(g_up))
        q_up = _pair_add(mine, theirs, name=f"mlp_up_pair{layer}")
        dh_new, dg, cs = _rms_bwd(dy, h, row(norm_mlp_g[layer]), dh, name=f"rms_mlp_bwd{layer}")
        return dh_new, cs, dg, q_up, r_down, first_landed

    dh3, cs3, g_nmlp1, q_up1, r_down1, _ = mlp_bwd(dh4, h3, y3, up1, act1, 1)
    g_bo = cs3
    g_wo = _matmul(attn, dh3, "tn", name="attn_out_dw", outs=[BF16])
    d_attn, (theirs, mine) = _matmul(dh3, wo, "nt", name="attn_out_dx", outs=[F32],
                                     moves=to_sibling(by_dev_rows(g_wo)))
    q_wo = _pair_add(mine, theirs, name="attn_out_pair")
    d_o = jnp.transpose(d_attn.reshape(t, ATTN_HEADS, HEAD_DIM), (1, 0, 2))
    (dq, dk, dv, d_sink), (r_up1, r_wo) = _attn_bwd(q, k, v, sink_col, d_o, name="attn_bwd",
                                                    moves=[to_chips(q_up1), to_chips(q_wo)])
    unheads = lambda a: jnp.transpose(a, (1, 0, 2)).reshape(t, -1)
    dqkv = jnp.concatenate([unheads(dq), unheads(dk), unheads(dv)], axis=1)
    g_bqkv = _colsum(dqkv, name="qkv_db")
    g_wqkv = _matmul(y2, dqkv, "tn", name="qkv_dw", outs=[BF16], tn=640)
    dy2, (theirs, mine) = _matmul(dqkv, wqkv, "nt", name="qkv_dx", outs=[F32], tk=640,
                                  moves=to_sibling(_cols_to_shards(g_wqkv)))
    q_wqkv = _pair_add(mine, theirs, name="qkv_pair")
    dh2, g_nmix1, _ = _rms_bwd(dy2, h2, row(norm_mix_g[1]), dh3, name="rms_mix1_bwd")
    dh1, _, g_nmlp0, q_up0, r_down0, (r_wqkv,) = mlp_bwd(dh2, h1, y1, up0, act0, 0, first_moves=[to_chips(q_wqkv)])

    d_mix = _matmul(dh1, w_out_f, "nt", name="mix_out_dx", outs=[F32])
    g_wout = _matmul(mix, dh1, "tn", name="mix_out_dw", outs=[BF16])
    (d_uv, g_ln_g, g_ln_b, g_gm_w, g_gm_b), (r_up0, theirs, mine) = _gmlp_bwd(
        proj_uv, d_mix, gm_ln_g, gm_ln_b, gm_w, gm_b, name="gmlp_bwd",
        moves=[to_chips(q_up0)] + to_sibling(by_dev_rows(g_wout)))
    q_wout = _pair_add(mine, theirs, name="mix_out_pair")

    early = [("norm_mlp_g", None), ("final_norm_g", None), ("norm_mix_g", 1), ("gm_ln_g", None), ("gm_ln_b", None),
             ("gm_w_s", None), ("gm_b_s", None), ("attn_sinks", None)]
    late = [("norm_mix_g", 0), ("ssm_conv_b", None), ("ssm_dt_bias", None), ("ssm_a_log", None), ("ssm_d", None),
            ("ssm_norm_g", None)]
    early_sharded, late_sharded = ["b_qkv", "b_o"], ["ssm_conv_w"]
    small_grads = {
        ("norm_mlp_g", None): jnp.concatenate([g_nmlp0, g_nmlp1], axis=0),
        ("final_norm_g", None): d_final_g, ("norm_mix_g", 1): g_nmix1,
        ("gm_ln_g", None): g_ln_g, ("gm_ln_b", None): g_ln_b, ("gm_w_s", None): g_gm_w, ("gm_b_s", None): g_gm_b,
        ("attn_sinks", None): jnp.sum(d_sink.reshape(ATTN_HEADS, CHUNK), axis=1),
        "b_qkv": g_bqkv, "b_o": g_bo,
    }
    pack = lambda keys: _as_rows(jnp.concatenate([small_grads[key].reshape(-1) for key in keys]))
    (dpre, dz, ddt, g_dtb, g_alog, g_dskip, g_ssm_ng), (r_wout, early_recv) = _ssd_bwd(
        proj_rest, h_states, d_mix, conv_w, conv_b, dt_bias, a_log, d_skip, ssm_norm_g, name="ssd_bwd",
        moves=[to_chips(q_wout), gather(pack(early + early_sharded))])
    d_rest, g_conv_w, g_conv_b = _conv_bwd(proj_rest, dpre, dz, ddt, conv_w, name="conv_bwd")
    g_w_uv = _matmul(y0, d_uv, "tn", name="proj_uv_dw", outs=[BF16])
    g_w_rest = _matmul(y0, d_rest, "tn", name="proj_rest_dw", outs=[BF16], tn=640)
    g_w_in = jnp.concatenate([g_w_uv, g_w_rest[:, CONV_DIM:CONV_DIM + D_MODEL], g_w_rest[:, :CONV_DIM],
                              g_w_rest[:, CONV_DIM + D_MODEL:CONV_DIM + D_MODEL + SSM_HEADS]], axis=1)
    dy0, (theirs, mine) = _matmul(d_uv, w_uv, "nt", name="proj_uv_dx", outs=[F32],
                                  moves=to_sibling(_cols_to_shards(g_w_in)))
    q_w_in = _pair_add(mine, theirs, name="proj_pair")
    dy0, (r_w_in,) = _matmul(d_rest, w_rest, "nt", name="proj_rest_dx", outs=[F32], extras=[(dy0, "tile")],
                             epilogue=add, tk=640, moves=[to_chips(q_w_in)])
    dx, g_nmix0, _ = _rms_bwd(dy0, xs, row(norm_mix_g[0]), dh1, name="rms_mix0_bwd")
    small_grads.update({
        ("norm_mix_g", 0): g_nmix0, ("ssm_conv_b", None): g_conv_b,
        ("ssm_dt_bias", None): g_dtb[:, :SSM_HEADS], ("ssm_a_log", None): g_alog[:, :SSM_HEADS],
        ("ssm_d", None): g_dskip[:, :SSM_HEADS], ("ssm_norm_g", None): g_ssm_ng, "ssm_conv_w": g_conv_w,
    })


    def update(n, parts, moves=()):
        shape = W[n].shape
        as3 = lambda a: a.reshape((len(parts),) + parts[0].shape[1:])
        res = _adamw(parts, as3(W[n]), as3(M[n]), as3(V[n]), name="adamw_" + n, moves=moves)
        res, landed = res if moves else (res, [])
        return [a.reshape(shape) for a in res], landed

    out = {}
    out["w_o"], (late_recv,) = update("w_o", [r_wo], moves=[gather(pack(late + late_sharded))])
    out["w_down"], _ = update("w_down", [r_down0, r_down1])
    out["w_up"], _ = update("w_up", [r_up0, r_up1])
    out["w_out_even"], _ = update("w_out_even", [r_wout])
    out["w_qkv"], _ = update("w_qkv", [r_wqkv])
    out["w_in_even"], _ = update("w_in_even", [r_w_in])

    def unpacked(recv, keys):
        flat, res, o = recv.reshape(N_DEV, -1), {}, 0
        for key in keys:
            res[key] = flat[:, o:o + small_grads[key].size]
            o += small_grads[key].size
        return res

    arrived = {**unpacked(early_recv, early + early_sharded), **unpacked(late_recv, late + late_sharded)}
    piece = lambda tree, key: tree[key[0]] if key[1] is None else tree[key[0]][key[1]]

    def rows_by_device(cat):
        pad = (-cat.shape[1]) % PACK_W
        return jnp.pad(cat, ((0, 0), (0, pad))).reshape(N_DEV, -1, PACK_W)

    rep_keys = early + late
    rep_parts = rows_by_device(jnp.concatenate([arrived[key] for key in rep_keys], axis=1))
    flat_rep = lambda tree: _as_rows(jnp.concatenate([piece(tree, key).reshape(-1) for key in rep_keys]))[None]
    rep_res = _adamw([rep_parts], flat_rep(W), flat_rep(M), flat_rep(V), name="adamw_replicated")
    sh_keys = early_sharded + late_sharded
    shard_parts = []
    for n in sh_keys:
        full = arrived[n].reshape((N_DEV,) + small_grads[n].shape)
        c = full.shape[-1] // N_DEV
        shard_parts.append(lax.dynamic_slice_in_dim(full, me * c, c, axis=full.ndim - 1).reshape(N_DEV, -1))
    sh_rows = rows_by_device(jnp.concatenate(shard_parts, axis=1))
    flat_sh = lambda tree: _as_rows(jnp.concatenate([tree[n].reshape(-1) for n in sh_keys]))[None]
    sh_res = _adamw([sh_rows], flat_sh(W), flat_sh(M), flat_sh(V), name="adamw_small_sharded")

    def unpack_replicated(rows):
        flat, vals, o = rows.reshape(-1), {}, 0
        for key in rep_keys:
            size = piece(W, key).size
            vals[key] = flat[o:o + size]
            o += size
        res = {}
        for n in replicated:
            if (n, None) in vals:
                res[n] = vals[(n, None)].reshape(W[n].shape)
            else:
                res[n] = jnp.stack([vals[(n, r)] for r in range(W[n].shape[0])]).reshape(W[n].shape)
        return res

    def unpack_sharded(rows):
        flat, res, o = rows.reshape(-1), {}, 0
        for n in sh_keys:
            res[n] = flat[o:o + W[n].size].reshape(W[n].shape)
            o += W[n].size
        return res

    results = []
    for idx in range(4):
        d = {n: out[n][idx] for n in big}
        d.update(unpack_replicated(rep_res[idx]))
        d.update(unpack_sharded(sh_res[idx]))
        results.append(d)

    loss = lax.psum(loss_part[0, 0], ("x", "y", "c"))
    grad_x = dx.reshape(x.shape)
    final = [loss, grad_x]
    for d in results:
        final.extend(d[n] for n in names)
    return tuple(final)
```

```python
import dataclasses
import functools

import jax
import jax.numpy as jnp
from jax import lax
from jax.experimental import pallas as pl
from jax.experimental.pallas import tpu as pltpu

F32 = jnp.float32
BF16 = jnp.bfloat16

N_DEV = 8
D_MODEL = 1024
D_FF = 4096
RMS_EPS = 1e-5
LN_EPS = 1e-5
CHUNK = 128
GM_GROUPS = 8
SSM_HEADS = 16
SSM_HEADDIM = 64
SSM_GROUPS = 4
SSM_STATE = 128
SSM_CONV = 4
CONV_DIM = 2048
IN_EVEN = 5136
REST_W = 3200
ATTN_HEADS = 16
ATTN_KV = 2
HEAD_DIM = 64
QKV_DIM = 1280
LANES = 128
HALO = 8
PACK_W = 1024

ADAM_LR = 0.001
ADAM_B1 = 0.9
ADAM_B2 = 0.999
ADAM_EPS = 1e-08
ADAM_WD = 0.01
ADAM_STEP = 10

VMEM_LIMIT_BYTES = 56 * 1024 * 1024


_NN = (((1,), (0,)), ((), ()))
_NT = (((1,), (1,)), ((), ()))
_TN = (((0,), (0,)), ((), ()))


def _dg(a, b, dims):
    return lax.dot_general(a.astype(BF16), b.astype(BF16), dims, preferred_element_type=F32)


@jax.custom_vjp
def _nn(a, b):
    return _dg(a, b, _NN)


@jax.custom_vjp
def _nt(a, b):
    return _dg(a, b, _NT)


@jax.custom_vjp
def _tn(a, b):
    return _dg(a, b, _TN)


_nn.defvjp(lambda a, b: (_dg(a, b, _NN), (a, b)), lambda r, g: (_nt(g, r[1]), _tn(r[0], g)))
_nt.defvjp(lambda a, b: (_dg(a, b, _NT), (a, b)), lambda r, g: (_nn(g, r[1]), _tn(g, r[0])))
_tn.defvjp(lambda a, b: (_dg(a, b, _TN), (a, b)), lambda r, g: (_nt(r[1], g), _nn(r[0], g)))


def _split3_dot(tri, x):
    x1 = x.astype(BF16)
    r1 = x - x1.astype(F32)
    x2 = r1.astype(BF16)
    x3 = (r1 - x2.astype(F32)).astype(BF16)
    t = tri.astype(BF16)
    dot = lambda p: lax.dot_general(t, p, _NN, preferred_element_type=F32)
    return dot(x1) + dot(x2) + dot(x3)


def _tri(lower):
    r = lax.broadcasted_iota(jnp.int32, (CHUNK, CHUNK), 0)
    c = lax.broadcasted_iota(jnp.int32, (CHUNK, CHUNK), 1)
    return jnp.where((r >= c) if lower else (r <= c), 1.0, 0.0).astype(F32)


@jax.custom_vjp
def _cumsum_rows(x):
    return _split3_dot(_tri(True), x)


_cumsum_rows.defvjp(lambda x: (_split3_dot(_tri(True), x), None), lambda _, g: (_split3_dot(_tri(False), g),))


def _sigmoid(x):
    return 1.0 / (1.0 + jnp.exp(-x))


def _silu(x):
    return x * _sigmoid(x)


def _softplus(x):
    return jnp.maximum(x, 0.0) + jnp.log(1.0 + jnp.exp(-jnp.abs(x)))


def _gelu_tanh(x):
    return 0.5 * x * (1.0 + jnp.tanh(0.7978845608028654 * (x + 0.044715 * (x * x * x))))


def _rmsnorm(x, g):
    return x * lax.rsqrt(jnp.mean(x * x, axis=-1, keepdims=True) + RMS_EPS) * g


def _gmlp_chunk(u, v, ln_g, ln_b, w_s, b_s):
    gu = _gelu_tanh(u)
    gv = _gelu_tanh(v)
    mu = jnp.mean(gv, axis=-1, keepdims=True)
    var = jnp.mean(jnp.square(gv - mu), axis=-1, keepdims=True)
    vn = (gv - mu) * lax.rsqrt(var + LN_EPS) * ln_g + ln_b
    r = lax.broadcasted_iota(jnp.int32, (CHUNK, CHUNK), 0)
    c = lax.broadcasted_iota(jnp.int32, (CHUNK, CHUNK), 1)
    causal = r >= c
    outs = []
    for g in range(GM_GROUPS):
        cols = slice(g * LANES, (g + 1) * LANES)
        mixed = _nn(jnp.where(causal, w_s[g], 0.0), vn[:, cols]) + b_s[g]
        outs.append(gu[:, cols] * mixed)
    return jnp.concatenate(outs, axis=1)


def _lane_pick(row, h):
    lane = lax.broadcasted_iota(jnp.int32, row.shape, 1)
    return jnp.sum(jnp.where(lane == h, row, 0.0), axis=1, keepdims=True)


def _col_pick(m, h):
    lane = lax.broadcasted_iota(jnp.int32, m.shape, 1)
    return jnp.sum(jnp.where(lane == h, m, 0.0), axis=1, keepdims=True)


def _row_pick(m, h):
    sub = lax.broadcasted_iota(jnp.int32, m.shape, 0)
    return jnp.sum(jnp.where(sub == h, m, 0.0), axis=0, keepdims=True)


_PAIRS = SSM_HEADS // 2


def _ssd_chunk(pre, z, dt_raw, h_prev, dt_bias, a_log, d_skip, norm_g):
    xbc = _silu(pre)
    dt = _softplus(dt_raw + dt_bias)
    da = dt * (-jnp.exp(a_log))
    a_cum = _cumsum_rows(da)
    a_cum_t = a_cum.T
    dt_t = dt.T
    r = lax.broadcasted_iota(jnp.int32, (CHUNK, CHUNK), 0)
    c = lax.broadcasted_iota(jnp.int32, (CHUNK, CHUNK), 1)
    causal = r >= c
    lane_lo = lax.broadcasted_iota(jnp.int32, (1, LANES), 1) < SSM_HEADDIM
    last_row = lax.broadcasted_iota(jnp.int32, (CHUNK, 1), 0) == CHUNK - 1
    ys, h_next = [], []
    for j in range(_PAIRS):
        g = j // 2
        xs = xbc[:, j * LANES:(j + 1) * LANES]
        bm = xbc[:, 1024 + g * SSM_STATE:1024 + (g + 1) * SSM_STATE]
        cm = xbc[:, 1536 + g * SSM_STATE:1536 + (g + 1) * SSM_STATE]
        cb = _nt(cm, bm)
        y_diag, to_end, e_cum, c_dec, d_row = [], [], [], [], []
        for h in (2 * j, 2 * j + 1):
            col = _col_pick(a_cum, h)
            row = _row_pick(a_cum_t, h)
            dt_col = _col_pick(dt, h)
            dt_row = _row_pick(dt_t, h)
            decay = jnp.exp(jnp.where(causal, col - row, -jnp.inf))
            y_diag.append(_nn(cb * decay * dt_row, xs))
            last = jnp.sum(jnp.where(last_row, col, 0.0), axis=0, keepdims=True)
            to_end.append(jnp.exp(last - col) * dt_col)
            e_cum.append(jnp.exp(col))
            c_dec.append(jnp.exp(last))
            d_row.append(_lane_pick(d_skip, h))
        pair = lambda lo_hi: jnp.where(lane_lo, lo_hi[0], lo_hi[1])
        states = _tn(bm, xs * pair(to_end))
        y_off = _nn(cm, h_prev[j]) * pair(e_cum)
        ys.append(pair(y_diag) + y_off + xs * pair(d_row))
        h_next.append(pair(c_dec) * h_prev[j] + states)
    y = jnp.concatenate(ys, axis=1) * _silu(z)
    width = D_MODEL // SSM_GROUPS
    y = jnp.concatenate(
        [_rmsnorm(y[:, g * width:(g + 1) * width], norm_g[:, g * width:(g + 1) * width]) for g in range(SSM_GROUPS)],
        axis=1)
    return y, tuple(h_next)


def _shift_down(prev8, x, k):
    if k == 0:
        return x
    win = jnp.concatenate([prev8, x], axis=0)
    return pltpu.roll(win, k, 0)[HALO:]


def _shift_up(x, next8, k):
    if k == 0:
        return x
    n = x.shape[0]
    win = jnp.concatenate([x, next8], axis=0)
    return pltpu.roll(win, n + HALO - k, 0)[:n]


def _conv_pre(prev8, x, w, b):
    out = b + x * w[SSM_CONV - 1:SSM_CONV]
    for i in range(SSM_CONV - 1):
        out = out + _shift_down(prev8, x, SSM_CONV - 1 - i) * w[i:i + 1]
    return out


def _attn_block(q, k_prev, k_cur, v_prev, v_cur, sink, first):
    k = jnp.concatenate([k_prev, k_cur], axis=0)
    v = jnp.concatenate([v_prev, v_cur], axis=0)
    s = _nt(q, k) * (HEAD_DIM ** -0.5)
    rows = lax.broadcasted_iota(jnp.int32, s.shape, 0) & (CHUNK - 1)
    cols = lax.broadcasted_iota(jnp.int32, s.shape, 1)
    valid = (cols <= rows + CHUNK) & (cols > rows) & (cols >= CHUNK * first.astype(jnp.int32))
    s = jnp.where(valid, s, -jnp.inf)
    m = jnp.maximum(jnp.max(s, axis=-1, keepdims=True), sink)
    p = jnp.exp(s - m)
    denom = jnp.sum(p, axis=-1, keepdims=True) + jnp.exp(sink - m)
    return _nn(p / denom, v)


N_CHIP = 4
N_CORE = 2
_OTHER_CHIPS = (2, 4, 6)


@dataclasses.dataclass
class _Move:
    kind: str
    src: jax.Array

    def dst_shape(self):
        s = self.src.shape
        shape = {"gather": (N_DEV,) + s, "gather_ici": (N_CHIP, N_CORE) + s, "gather_d2d": s,
                 "scatter_d2d": (N_CHIP,) + s[2:], "scatter_ici": s}[self.kind]
        return jax.ShapeDtypeStruct(tuple(shape), self.src.dtype)


def _peer(x, y, c, k):
    return (1 - x if k & 4 else x, 1 - y if k & 2 else y, 1 - c if k & 1 else c)


def _move_copies(moves, srcs, dsts, send_sems, recv_sems, local_sems):
    x, y, c = lax.axis_index("x"), lax.axis_index("y"), lax.axis_index("c")
    chip = 2 * x + y
    me = 2 * chip + c
    sibling = (x, y, 1 - c)
    all_chips = pl.ds(0, N_CHIP)
    local, remote = [], []

    def push(n, k, src, dst, device):
        remote.append(pltpu.make_async_remote_copy(
            src_ref=src, dst_ref=dst, send_sem=send_sems.at[n, k], recv_sem=recv_sems.at[n, k],
            device_id=device, device_id_type=pl.DeviceIdType.MESH))

    for n, mv in enumerate(moves):
        s, d = srcs[n], dsts[n]
        if mv.kind == "gather":
            local.append(pltpu.make_async_copy(s, d.at[me], local_sems.at[n]))
            for k in range(1, N_DEV):
                push(n, k - 1, s, d.at[me], _peer(x, y, c, k))
        elif mv.kind == "gather_ici":
            local.append(pltpu.make_async_copy(s, d.at[chip, c], local_sems.at[n]))
            for k in _OTHER_CHIPS:
                push(n, k - 1, s, d.at[chip, c], _peer(x, y, c, k))
        elif mv.kind == "gather_d2d":
            push(n, 0, d.at[all_chips, c], d.at[all_chips, c], sibling)
        elif mv.kind == "scatter_d2d":
            push(n, 0, s.at[all_chips, 1 - c], d, sibling)
        else:
            assert mv.kind == "scatter_ici", mv.kind
            local.append(pltpu.make_async_copy(s.at[chip], d.at[chip], local_sems.at[n]))
            for k in _OTHER_CHIPS:
                px, py, _ = _peer(x, y, c, k)
                push(n, k - 1, s.at[2 * px + py], d.at[chip], (px, py, c))
    return local, remote


def _move_aliases(moves, n_in, n_out):
    return {n_in + n: n_out + n for n, mv in enumerate(moves) if mv.kind == "gather_d2d"}


def _pcall(body, *, name, grid, in_specs, out_specs, out_shape, scratch_shapes=(), semantics=(), moves=(),
           aliases=None):
    out_shape, out_specs = list(out_shape), list(out_specs)
    in_specs = list(in_specs)
    if not moves:
        call = pl.pallas_call(
            body, name=name, grid=grid, in_specs=in_specs, out_specs=out_specs, out_shape=out_shape,
            scratch_shapes=list(scratch_shapes), input_output_aliases=aliases or {},
            compiler_params=pltpu.CompilerParams(dimension_semantics=tuple(semantics),
                                                 vmem_limit_bytes=VMEM_LIMIT_BYTES))
        return (lambda *args: (list(call(*args)), []))
    n_in, n_out, n_scr, n_mv = len(in_specs), len(out_shape), len(scratch_shapes), len(moves)
    hbm = pl.BlockSpec(memory_space=pltpu.HBM)

    def carrier(*refs):
        ins, rest = refs[:n_in], refs[n_in:]
        srcs, rest = rest[:n_mv], rest[n_mv:]
        outs, rest = rest[:n_out], rest[n_out:]
        dsts, rest = rest[:n_mv], rest[n_mv:]
        scr, (send_sems, recv_sems, local_sems) = rest[:n_scr], rest[n_scr:]
        first = functools.reduce(jnp.logical_and, [pl.program_id(d) == 0 for d in range(len(grid))])
        last = functools.reduce(jnp.logical_and, [pl.program_id(d) == grid[d] - 1 for d in range(len(grid))])

        @pl.when(first)
        def _():
            local, remote = _move_copies(moves, srcs, dsts, send_sems, recv_sems, local_sems)
            for cp in local + remote:
                cp.start()

        body(*ins, *outs, *scr)

        @pl.when(last)
        def _():
            local, remote = _move_copies(moves, srcs, dsts, send_sems, recv_sems, local_sems)
            for cp in remote + local:
                cp.wait()

    call = pl.pallas_call(
        carrier, name=name, grid=grid,
        in_specs=in_specs + [hbm] * n_mv,
        out_specs=out_specs + [hbm] * n_mv,
        out_shape=out_shape + [mv.dst_shape() for mv in moves],
        scratch_shapes=list(scratch_shapes) + [pltpu.SemaphoreType.DMA((n_mv, N_DEV - 1)),
                                               pltpu.SemaphoreType.DMA((n_mv, N_DEV - 1)),
                                               pltpu.SemaphoreType.DMA((n_mv,))],
        input_output_aliases={**(aliases or {}), **_move_aliases(moves, n_in, n_out)},
        compiler_params=pltpu.CompilerParams(dimension_semantics=("arbitrary",) * len(grid),
                                             vmem_limit_bytes=VMEM_LIMIT_BYTES))

    def run(*args):
        res = list(call(*args, *[mv.src for mv in moves]))
        return res[:n_out], res[n_out:]

    return run


def _exchange(moves, *, name):
    n_mv = len(moves)
    hbm = pl.BlockSpec(memory_space=pltpu.HBM)

    def body(*refs):
        srcs, dsts, (send_sems, recv_sems, local_sems) = refs[:n_mv], refs[n_mv:2 * n_mv], refs[2 * n_mv:]
        local, remote = _move_copies(moves, srcs, dsts, send_sems, recv_sems, local_sems)
        for cp in local + remote:
            cp.start()
        for cp in remote + local:
            cp.wait()

    return list(pl.pallas_call(
        body, name=name, in_specs=[hbm] * n_mv, out_specs=[hbm] * n_mv,
        out_shape=[mv.dst_shape() for mv in moves],
        scratch_shapes=[pltpu.SemaphoreType.DMA((n_mv, N_DEV - 1)), pltpu.SemaphoreType.DMA((n_mv, N_DEV - 1)),
                        pltpu.SemaphoreType.DMA((n_mv,))],
    )(*[mv.src for mv in moves]))


def _matmul(a, b, mode, *, name, outs, extras=(), epilogue=None, tm=1024, tn=1024, tk=1024, dims=None,
            b_spec=None, out_spec=None, out_shape=None, moves=()):
    if dims is not None:
        m, n, k = dims
    elif mode == "nn":
        (m, k), (_, n) = a.shape, b.shape
    elif mode == "nt":
        (m, k), (n, _) = a.shape, b.shape
    else:
        (k, m), (_, n) = a.shape, b.shape
    tm, tn, tk = min(tm, m), min(tn, n), min(tk, k)
    assert m % tm == 0 and n % tn == 0 and k % tk == 0, (name, m, n, k, tm, tn, tk)
    nk = k // tk
    contract = {"nn": _NN, "nt": _NT, "tn": _TN}[mode]
    if mode == "tn":
        a_spec = pl.BlockSpec((tk, tm), lambda i, j, kk: (kk, i))
    else:
        a_spec = pl.BlockSpec((tm, tk), lambda i, j, kk: (i, kk))
    if b_spec is None:
        if mode == "nt":
            b_spec = pl.BlockSpec((tn, tk), lambda i, j, kk: (j, kk))
        else:
            b_spec = pl.BlockSpec((tk, tn), lambda i, j, kk: (kk, j))
    tile_spec = pl.BlockSpec((tm, tn), lambda i, j, kk: (i, j))
    row_spec = pl.BlockSpec((1, tn), lambda i, j, kk: (0, j))
    extra_specs = [tile_spec if kind == "tile" else row_spec for _, kind in extras]
    n_extra, n_out = len(extras), len(outs)
    if epilogue is None:
        epilogue = lambda acc: (acc,)
    if out_spec is None:
        out_spec, out_shape = tile_spec, (m, n)

    def body(a_ref, b_ref, *rest):
        extra_refs, out_refs, acc_ref = rest[:n_extra], rest[n_extra:n_extra + n_out], rest[-1]
        kk = pl.program_id(2)

        @pl.when(kk == 0)
        def _():
            acc_ref[...] = jnp.zeros_like(acc_ref)

        acc_ref[...] += lax.dot_general(a_ref[...].astype(BF16), b_ref[...].astype(BF16), contract,
                                        preferred_element_type=F32)

        @pl.when(kk == nk - 1)
        def _():
            res = epilogue(acc_ref[...], *[e[...] for e in extra_refs])
            for val, o_ref in zip(res, out_refs):
                o_ref[...] = val.astype(o_ref.dtype)

    res, landed = _pcall(
        body, name=name, grid=(m // tm, n // tn, nk),
        in_specs=[a_spec, b_spec] + extra_specs,
        out_specs=[out_spec] * n_out,
        out_shape=[jax.ShapeDtypeStruct(out_shape, dt) for dt in outs],
        scratch_shapes=[pltpu.VMEM((tm, tn), F32)],
        semantics=("parallel", "parallel", "arbitrary"), moves=moves,
    )(a, b, *[e for e, _ in extras])
    res = res[0] if n_out == 1 else res
    return (res, landed) if moves else res


def _rms_fwd(h, g, *, name, tb=512, moves=()):
    t, d = h.shape

    def body(h_ref, g_ref, y_ref):
        y_ref[...] = _rmsnorm(h_ref[...], g_ref[...]).astype(y_ref.dtype)

    res, landed = _pcall(
        body, name=name, grid=(t // tb,),
        in_specs=[pl.BlockSpec((tb, d), lambda i: (i, 0)), pl.BlockSpec((1, d), lambda i: (0, 0))],
        out_specs=[pl.BlockSpec((tb, d), lambda i: (i, 0))],
        out_shape=[jax.ShapeDtypeStruct((t, d), BF16)],
        semantics=("parallel",), moves=moves,
    )(h, g)
    return (res[0], landed) if moves else res[0]


def _pair_add(by_core, theirs, core, *, name, tb=512):
    n_chip, _, r, c = by_core.shape
    tb = min(tb, r)
    assert r % tb == 0, (name, r, tb)

    def body(core_ref, a_ref, b_ref, o_ref):
        del core_ref
        o_ref[...] = (a_ref[...].astype(F32) + b_ref[...].astype(F32)).astype(o_ref.dtype)

    blk = pl.BlockSpec((None, tb, c), lambda ch, i, core_ref: (ch, i, 0))
    return pl.pallas_call(
        body, name=name,
        grid_spec=pltpu.PrefetchScalarGridSpec(
            num_scalar_prefetch=1, grid=(n_chip, r // tb),
            in_specs=[pl.BlockSpec((None, None, tb, c), lambda ch, i, core_ref: (ch, core_ref[0], i, 0)), blk],
            out_specs=blk),
        out_shape=jax.ShapeDtypeStruct((n_chip, r, c), by_core.dtype),
        compiler_params=pltpu.CompilerParams(dimension_semantics=("parallel", "parallel"),
                                             vmem_limit_bytes=VMEM_LIMIT_BYTES),
    )(core, by_core, theirs)


def _rms_bwd(dy, h, g, dres, *, name, tb=512, moves=()):
    t, d = h.shape

    def body(dy_ref, h_ref, g_ref, dres_ref, dh_ref, dg_ref, cs_ref):
        _, vjp = jax.vjp(_rmsnorm, h_ref[...], g_ref[...])
        dh, dg = vjp(dy_ref[...])
        dh = dh + dres_ref[...]
        dh_ref[...] = dh

        @pl.when(pl.program_id(0) == 0)
        def _():
            dg_ref[...] = jnp.zeros_like(dg_ref)
            cs_ref[...] = jnp.zeros_like(cs_ref)

        dg_ref[...] += dg
        cs_ref[...] += jnp.sum(dh, axis=0, keepdims=True)

    blk = pl.BlockSpec((tb, d), lambda i: (i, 0))
    row = pl.BlockSpec((1, d), lambda i: (0, 0))
    res, landed = _pcall(
        body, name=name, grid=(t // tb,),
        in_specs=[blk, blk, row, blk],
        out_specs=[blk, row, row],
        out_shape=[jax.ShapeDtypeStruct((t, d), F32), jax.ShapeDtypeStruct((1, d), F32),
                   jax.ShapeDtypeStruct((1, d), F32)],
        semantics=("arbitrary",), moves=moves,
    )(dy, h, g, dres)
    return (res, landed) if moves else res


def _colsum(a, *, name, tb=512):
    t, d = a.shape

    def body(a_ref, o_ref):
        @pl.when(pl.program_id(0) == 0)
        def _():
            o_ref[...] = jnp.zeros_like(o_ref)

        o_ref[...] += jnp.sum(a_ref[...].astype(F32), axis=0, keepdims=True)

    return _pcall(
        body, name=name, grid=(t // tb,),
        in_specs=[pl.BlockSpec((tb, d), lambda i: (i, 0))],
        out_specs=[pl.BlockSpec((1, d), lambda i: (0, 0))],
        out_shape=[jax.ShapeDtypeStruct((1, d), F32)],
        semantics=("arbitrary",),
    )(a)[0][0]


def _final_loss(h, g, target, *, name, tb=512):
    t, d = h.shape

    def body(h_ref, g_ref, tgt_ref, loss_ref, dh_ref, dg_ref):
        def f(hh, gg):
            err = jnp.square(_rmsnorm(hh, gg) - tgt_ref[...])
            return 0.5 * jnp.sum(jnp.mean(err, axis=-1, keepdims=True), axis=0, keepdims=True)

        val, vjp = jax.vjp(f, h_ref[...], g_ref[...])
        dh, dg = vjp(jnp.ones((1, 1), F32))
        dh_ref[...] = dh

        @pl.when(pl.program_id(0) == 0)
        def _():
            loss_ref[...] = jnp.zeros_like(loss_ref)
            dg_ref[...] = jnp.zeros_like(dg_ref)

        loss_ref[...] += val
        dg_ref[...] += dg

    blk = pl.BlockSpec((tb, d), lambda i: (i, 0))
    row = pl.BlockSpec((1, d), lambda i: (0, 0))
    return _pcall(
        body, name=name, grid=(t // tb,),
        in_specs=[blk, row, blk],
        out_specs=[pl.BlockSpec((8, LANES), lambda i: (0, 0)), blk, row],
        out_shape=[jax.ShapeDtypeStruct((8, LANES), F32), jax.ShapeDtypeStruct((t, d), F32),
                   jax.ShapeDtypeStruct((1, d), F32)],
        semantics=("arbitrary",),
    )(h, g, target)[0]


def _gmlp_fwd(proj_uv, ln_g, ln_b, w_s, b_s, *, name, moves=()):
    t = proj_uv.shape[0]
    w = D_MODEL

    def body(u_ref, v_ref, g_ref, b_ref, w_ref, bs_ref, o_ref):
        o_ref[...] = _gmlp_chunk(u_ref[...], v_ref[...], g_ref[...], b_ref[...], w_ref[...],
                                 bs_ref[...]).astype(o_ref.dtype)

    row = pl.BlockSpec((1, w), lambda i: (0, 0))
    res, landed = _pcall(
        body, name=name, grid=(t // CHUNK,),
        in_specs=[pl.BlockSpec((CHUNK, w), lambda i: (i, 0)), pl.BlockSpec((CHUNK, w), lambda i: (i, 1)), row, row,
                  pl.BlockSpec((GM_GROUPS, CHUNK, CHUNK), lambda i: (0, 0, 0)),
                  pl.BlockSpec((GM_GROUPS, CHUNK, 1), lambda i: (0, 0, 0))],
        out_specs=[pl.BlockSpec((CHUNK, w), lambda i: (i, 0))],
        out_shape=[jax.ShapeDtypeStruct((t, 2 * w), BF16)],
        semantics=("parallel",), moves=moves,
    )(proj_uv, proj_uv, ln_g, ln_b, w_s, b_s)
    return (res[0], landed) if moves else res[0]


def _gmlp_bwd(proj_uv, d_mix, ln_g, ln_b, w_s, b_s, *, name, moves=()):
    t = proj_uv.shape[0]
    w = D_MODEL

    def body(u_ref, v_ref, da_ref, g_ref, b_ref, w_ref, bs_ref, duv_ref, dg_ref, db_ref, dw_ref, dbs_ref):
        _, vjp = jax.vjp(_gmlp_chunk, u_ref[...], v_ref[...], g_ref[...], b_ref[...], w_ref[...], bs_ref[...])
        du, dv, dg, db, dw, dbs = vjp(da_ref[...])
        duv_ref[:, :w] = du.astype(duv_ref.dtype)
        duv_ref[:, w:] = dv.astype(duv_ref.dtype)

        @pl.when(pl.program_id(0) == 0)
        def _():
            dg_ref[...] = jnp.zeros_like(dg_ref)
            db_ref[...] = jnp.zeros_like(db_ref)
            dw_ref[...] = jnp.zeros_like(dw_ref)
            dbs_ref[...] = jnp.zeros_like(dbs_ref)

        dg_ref[...] += dg
        db_ref[...] += db
        dw_ref[...] += dw
        dbs_ref[...] += dbs

    row = pl.BlockSpec((1, w), lambda i: (0, 0))
    ws = pl.BlockSpec((GM_GROUPS, CHUNK, CHUNK), lambda i: (0, 0, 0))
    bs = pl.BlockSpec((GM_GROUPS, CHUNK, 1), lambda i: (0, 0, 0))
    res, landed = _pcall(
        body, name=name, grid=(t // CHUNK,),
        in_specs=[pl.BlockSpec((CHUNK, w), lambda i: (i, 0)), pl.BlockSpec((CHUNK, w), lambda i: (i, 1)),
                  pl.BlockSpec((CHUNK, w), lambda i: (i, 0)), row, row, ws, bs],
        out_specs=[pl.BlockSpec((CHUNK, 2 * w), lambda i: (i, 0)), row, row, ws, bs],
        out_shape=[jax.ShapeDtypeStruct((t, 2 * w), BF16), jax.ShapeDtypeStruct((1, w), F32),
                   jax.ShapeDtypeStruct((1, w), F32), jax.ShapeDtypeStruct((GM_GROUPS, CHUNK, CHUNK), F32),
                   jax.ShapeDtypeStruct((GM_GROUPS, CHUNK, 1), F32)],
        semantics=("arbitrary",), moves=moves,
    )(proj_uv, proj_uv, d_mix, ln_g, ln_b, w_s, b_s)
    return (res, landed) if moves else res


_HALO_PER_CHUNK = CHUNK // HALO
_DT_BLOCK = (CONV_DIM + D_MODEL) // LANES


def _ssd_fwd(proj_rest, mix, conv_w, conv_b, dt_bias, a_log, d_skip, norm_g, *, name, moves=()):
    t = proj_rest.shape[0]
    nc = t // CHUNK

    def body(x_ref, prev_ref, z_ref, dt_ref, mix_ref, cw_ref, cb_ref, dtb_ref, al_ref, ds_ref, ng_ref, y_ref, hs_ref,
             h_scr):
        del mix_ref
        i = pl.program_id(0)

        @pl.when(i == 0)
        def _():
            h_scr[...] = jnp.zeros_like(h_scr)

        prev8 = jnp.where(i == 0, 0.0, prev_ref[...])
        pre = _conv_pre(prev8, x_ref[...], cw_ref[...], cb_ref[...])
        hs_ref[0] = h_scr[...]
        h_prev = tuple(h_scr[j] for j in range(_PAIRS))
        y, h_next = _ssd_chunk(pre, z_ref[...], dt_ref[...], h_prev, dtb_ref[...], al_ref[...], ds_ref[...],
                               ng_ref[...])
        y_ref[...] = y.astype(y_ref.dtype)
        for j in range(_PAIRS):
            h_scr[j] = h_next[j]

    small = pl.BlockSpec((1, LANES), lambda i: (0, 0))
    res, landed = _pcall(
        body, name=name, grid=(nc,),
        in_specs=[pl.BlockSpec((CHUNK, CONV_DIM), lambda i: (i, 0)),
                  pl.BlockSpec((HALO, CONV_DIM), lambda i: (jnp.maximum(i * _HALO_PER_CHUNK - 1, 0), 0)),
                  pl.BlockSpec((CHUNK, D_MODEL), lambda i: (i, CONV_DIM // D_MODEL)),
                  pl.BlockSpec((CHUNK, LANES), lambda i: (i, _DT_BLOCK)),
                  pl.BlockSpec(memory_space=pl.ANY),
                  pl.BlockSpec((SSM_CONV, CONV_DIM), lambda i: (0, 0)),
                  pl.BlockSpec((1, CONV_DIM), lambda i: (0, 0)),
                  small, small, small, pl.BlockSpec((1, D_MODEL), lambda i: (0, 0))],
        out_specs=[pl.BlockSpec((CHUNK, D_MODEL), lambda i: (i, 1)),
                   pl.BlockSpec((1, _PAIRS, SSM_STATE, LANES), lambda i: (i, 0, 0, 0))],
        out_shape=[jax.ShapeDtypeStruct((t, 2 * D_MODEL), BF16),
                   jax.ShapeDtypeStruct((nc, _PAIRS, SSM_STATE, LANES), F32)],
        scratch_shapes=[pltpu.VMEM((_PAIRS, SSM_STATE, LANES), F32)],
        semantics=("arbitrary",), moves=moves, aliases={4: 0},
    )(proj_rest, proj_rest, proj_rest, proj_rest, mix, conv_w, conv_b, dt_bias, a_log, d_skip, norm_g)
    return (res, landed) if moves else res


def _ssd_bwd(proj_rest, h_states, d_mix, conv_w, conv_b, dt_bias, a_log, d_skip, norm_g, *, name, moves=()):
    t = proj_rest.shape[0]
    nc = t // CHUNK

    def body(x_ref, prev_ref, z_ref, dt_ref, hs_ref, dy_ref, cw_ref, cb_ref, dtb_ref, al_ref, ds_ref, ng_ref,
             dpre_ref, dz_ref, ddt_ref, ddtb_ref, dal_ref, dds_ref, dng_ref, dh_scr):
        i = pl.program_id(0)
        chunk = nc - 1 - i

        @pl.when(i == 0)
        def _():
            dh_scr[...] = jnp.zeros_like(dh_scr)
            ddtb_ref[...] = jnp.zeros_like(ddtb_ref)
            dal_ref[...] = jnp.zeros_like(dal_ref)
            dds_ref[...] = jnp.zeros_like(dds_ref)
            dng_ref[...] = jnp.zeros_like(dng_ref)

        prev8 = jnp.where(chunk == 0, 0.0, prev_ref[...])
        pre = _conv_pre(prev8, x_ref[...], cw_ref[...], cb_ref[...])
        h_prev = tuple(hs_ref[0, j] for j in range(_PAIRS))
        _, vjp = jax.vjp(_ssd_chunk, pre, z_ref[...], dt_ref[...], h_prev, dtb_ref[...], al_ref[...],
                         ds_ref[...], ng_ref[...])
        dpre, dz, ddt, dh_prev, ddtb, dal, dds, dng = vjp((dy_ref[...], tuple(dh_scr[j] for j in range(_PAIRS))))
        dpre_ref[...] = dpre
        dz_ref[...] = dz
        ddt_ref[...] = ddt
        for j in range(_PAIRS):
            dh_scr[j] = dh_prev[j]
        ddtb_ref[...] += ddtb
        dal_ref[...] += dal
        dds_ref[...] += dds
        dng_ref[...] += dng

    rev = lambda i: nc - 1 - i
    small = pl.BlockSpec((1, LANES), lambda i: (0, 0))
    wide = pl.BlockSpec((1, D_MODEL), lambda i: (0, 0))
    res, landed = _pcall(
        body, name=name, grid=(nc,),
        in_specs=[pl.BlockSpec((CHUNK, CONV_DIM), lambda i: (rev(i), 0)),
                  pl.BlockSpec((HALO, CONV_DIM), lambda i: (jnp.maximum(rev(i) * _HALO_PER_CHUNK - 1, 0), 0)),
                  pl.BlockSpec((CHUNK, D_MODEL), lambda i: (rev(i), CONV_DIM // D_MODEL)),
                  pl.BlockSpec((CHUNK, LANES), lambda i: (rev(i), _DT_BLOCK)),
                  pl.BlockSpec((1, _PAIRS, SSM_STATE, LANES), lambda i: (rev(i), 0, 0, 0)),
                  pl.BlockSpec((CHUNK, D_MODEL), lambda i: (rev(i), 1)),
                  pl.BlockSpec((SSM_CONV, CONV_DIM), lambda i: (0, 0)),
                  pl.BlockSpec((1, CONV_DIM), lambda i: (0, 0)),
                  small, small, small, wide],
        out_specs=[pl.BlockSpec((CHUNK, CONV_DIM), lambda i: (rev(i), 0)),
                   pl.BlockSpec((CHUNK, D_MODEL), lambda i: (rev(i), 0)),
                   pl.BlockSpec((CHUNK, LANES), lambda i: (rev(i), 0)),
                   small, small, small, wide],
        out_shape=[jax.ShapeDtypeStruct((t, CONV_DIM), F32), jax.ShapeDtypeStruct((t, D_MODEL), F32),
                   jax.ShapeDtypeStruct((t, LANES), F32),
                   jax.ShapeDtypeStruct((1, LANES), F32), jax.ShapeDtypeStruct((1, LANES), F32),
                   jax.ShapeDtypeStruct((1, LANES), F32), jax.ShapeDtypeStruct((1, D_MODEL), F32)],
        scratch_shapes=[pltpu.VMEM((_PAIRS, SSM_STATE, LANES), F32)],
        semantics=("arbitrary",), moves=moves,
    )(proj_rest, proj_rest, proj_rest, proj_rest, h_states, d_mix, conv_w, conv_b, dt_bias, a_log, d_skip, norm_g)
    return (res, landed) if moves else res


def _conv_bwd(proj_rest, dpre, dz, ddt, conv_w, *, name, tb=256, moves=()):
    t = proj_rest.shape[0]
    nb = t // tb
    per = tb // HALO

    def body(x_ref, prev_ref, dpre_ref, next_ref, dz_ref, ddt_ref, cw_ref, drest_ref, dcw_ref, dcb_ref):
        i = pl.program_id(0)

        @pl.when(i == 0)
        def _():
            dcw_ref[...] = jnp.zeros_like(dcw_ref)
            dcb_ref[...] = jnp.zeros_like(dcb_ref)

        x = x_ref[...]
        dp = dpre_ref[...]
        w = cw_ref[...]
        prev8 = jnp.where(i == 0, 0.0, prev_ref[...])
        next8 = jnp.where(i == nb - 1, 0.0, next_ref[...])
        dx = dp * w[SSM_CONV - 1:SSM_CONV]
        for j in range(SSM_CONV - 1):
            dx = dx + _shift_up(dp, next8, SSM_CONV - 1 - j) * w[j:j + 1]
        drest_ref[:, :CONV_DIM] = dx.astype(drest_ref.dtype)
        drest_ref[:, CONV_DIM:CONV_DIM + D_MODEL] = dz_ref[...].astype(drest_ref.dtype)
        drest_ref[:, CONV_DIM + D_MODEL:] = ddt_ref[...].astype(drest_ref.dtype)
        for j in range(SSM_CONV):
            dcw_ref[j:j + 1, :] += jnp.sum(dp * _shift_down(prev8, x, SSM_CONV - 1 - j), axis=0, keepdims=True)
        dcb_ref[...] += jnp.sum(dp, axis=0, keepdims=True)

    res, landed = _pcall(
        body, name=name, grid=(nb,),
        in_specs=[pl.BlockSpec((tb, CONV_DIM), lambda i: (i, 0)),
                  pl.BlockSpec((HALO, CONV_DIM), lambda i: (jnp.maximum(i * per - 1, 0), 0)),
                  pl.BlockSpec((tb, CONV_DIM), lambda i: (i, 0)),
                  pl.BlockSpec((HALO, CONV_DIM), lambda i: (jnp.minimum((i + 1) * per, nb * per - 1), 0)),
                  pl.BlockSpec((tb, D_MODEL), lambda i: (i, 0)),
                  pl.BlockSpec((tb, LANES), lambda i: (i, 0)),
                  pl.BlockSpec((SSM_CONV, CONV_DIM), lambda i: (0, 0))],
        out_specs=[pl.BlockSpec((tb, REST_W), lambda i: (i, 0)),
                   pl.BlockSpec((SSM_CONV, CONV_DIM), lambda i: (0, 0)),
                   pl.BlockSpec((1, CONV_DIM), lambda i: (0, 0))],
        out_shape=[jax.ShapeDtypeStruct((t, REST_W), BF16), jax.ShapeDtypeStruct((SSM_CONV, CONV_DIM), F32),
                   jax.ShapeDtypeStruct((1, CONV_DIM), F32)],
        semantics=("arbitrary",), moves=moves,
    )(proj_rest, proj_rest, dpre, dpre, dz, ddt, conv_w)
    return (res, landed) if moves else res


_Q_PER_KV = ATTN_HEADS // ATTN_KV


def _attn_fwd(q, k, v, sink_col, *, name):
    t = q.shape[1]
    nb = t // CHUNK

    def body(q_ref, kp_ref, kc_ref, vp_ref, vc_ref, s_ref, o_ref):
        first = pl.program_id(0) == 0
        for j in range(ATTN_KV):
            qj = q_ref[j * _Q_PER_KV:(j + 1) * _Q_PER_KV].reshape(_Q_PER_KV * CHUNK, HEAD_DIM)
            o = _attn_block(qj, kp_ref[j], kc_ref[j], vp_ref[j], vc_ref[j], s_ref[j], first)
            o_ref[j * _Q_PER_KV:(j + 1) * _Q_PER_KV] = o.reshape(_Q_PER_KV, CHUNK, HEAD_DIM).astype(o_ref.dtype)

    cur = lambda i: (0, i, 0)
    prev = lambda i: (0, jnp.maximum(i - 1, 0), 0)
    kv = (ATTN_KV, CHUNK, HEAD_DIM)
    return _pcall(
        body, name=name, grid=(nb,),
        in_specs=[pl.BlockSpec((ATTN_HEADS, CHUNK, HEAD_DIM), cur), pl.BlockSpec(kv, prev), pl.BlockSpec(kv, cur),
                  pl.BlockSpec(kv, prev), pl.BlockSpec(kv, cur),
                  pl.BlockSpec((ATTN_KV, _Q_PER_KV * CHUNK, 1), lambda i: (0, 0, 0))],
        out_specs=[pl.BlockSpec((ATTN_HEADS, CHUNK, HEAD_DIM), cur)],
        out_shape=[jax.ShapeDtypeStruct((ATTN_HEADS, t, HEAD_DIM), BF16)],
        semantics=("parallel",),
    )(q, k, k, v, v, sink_col)[0][0]


def _attn_bwd(q, k, v, sink_col, d_o, *, name, moves=()):
    t = q.shape[1]
    nb = t // CHUNK

    def body(q_ref, kp_ref, kc_ref, vp_ref, vc_ref, s_ref, do_ref, dq_ref, dk_ref, dv_ref, ds_ref, dk_scr, dv_scr):
        i = pl.program_id(0)
        first = i == nb - 1

        @pl.when(i == 0)
        def _():
            dk_scr[...] = jnp.zeros_like(dk_scr)
            dv_scr[...] = jnp.zeros_like(dv_scr)
            ds_ref[...] = jnp.zeros_like(ds_ref)

        for j in range(ATTN_KV):
            heads = slice(j * _Q_PER_KV, (j + 1) * _Q_PER_KV)
            qj = q_ref[heads].reshape(_Q_PER_KV * CHUNK, HEAD_DIM)
            doj = do_ref[heads].reshape(_Q_PER_KV * CHUNK, HEAD_DIM)
            _, vjp = jax.vjp(functools.partial(_attn_block, first=first), qj, kp_ref[j], kc_ref[j], vp_ref[j],
                             vc_ref[j], s_ref[j])
            dq, dkp, dkc, dvp, dvc, dsink = vjp(doj)
            dq_ref[heads] = dq.reshape(_Q_PER_KV, CHUNK, HEAD_DIM)
            dk_ref[j] = dkc + dk_scr[j]
            dv_ref[j] = dvc + dv_scr[j]
            dk_scr[j] = dkp
            dv_scr[j] = dvp
            ds_ref[j] += dsink

    cur = lambda i: (0, nb - 1 - i, 0)
    prev = lambda i: (0, jnp.maximum(nb - 2 - i, 0), 0)
    kv = (ATTN_KV, CHUNK, HEAD_DIM)
    qs = (ATTN_HEADS, CHUNK, HEAD_DIM)
    sk = pl.BlockSpec((ATTN_KV, _Q_PER_KV * CHUNK, 1), lambda i: (0, 0, 0))
    res, landed = _pcall(
        body, name=name, grid=(nb,),
        in_specs=[pl.BlockSpec(qs, cur), pl.BlockSpec(kv, prev), pl.BlockSpec(kv, cur), pl.BlockSpec(kv, prev),
                  pl.BlockSpec(kv, cur), sk, pl.BlockSpec(qs, cur)],
        out_specs=[pl.BlockSpec(qs, cur), pl.BlockSpec(kv, cur), pl.BlockSpec(kv, cur), sk],
        out_shape=[jax.ShapeDtypeStruct((ATTN_HEADS, t, HEAD_DIM), F32), jax.ShapeDtypeStruct((ATTN_KV, t, HEAD_DIM), F32),
                   jax.ShapeDtypeStruct((ATTN_KV, t, HEAD_DIM), F32),
                   jax.ShapeDtypeStruct((ATTN_KV, _Q_PER_KV * CHUNK, 1), F32)],
        scratch_shapes=[pltpu.VMEM(kv, F32), pltpu.VMEM(kv, F32)],
        semantics=("arbitrary",), moves=moves,
    )(q, k, k, v, v, sink_col, d_o)
    return (res, landed) if moves else res


def _adamw(parts, w, m, v, *, name, tb=256, moves=()):
    layers, r, c = w.shape
    n = parts[0].shape[0]
    tb = min(tb, r)
    assert r % tb == 0 and len(parts) == layers, (name, r, tb)
    nb = r // tb

    def body(*refs):
        p_refs = refs[:layers]
        w_ref, m_ref, v_ref, g_ref, d_ref, nm_ref, nv_ref = refs[layers:]
        for layer in range(layers):
            @pl.when(pl.program_id(0) == layer)
            def _(p_ref=p_refs[layer]):
                g = p_ref[0].astype(F32)
                for s in range(1, n):
                    g = g + p_ref[s].astype(F32)
                m_new = ADAM_B1 * m_ref[...] + (1.0 - ADAM_B1) * g
                v_new = ADAM_B2 * v_ref[...] + (1.0 - ADAM_B2) * jnp.square(g)
                m_hat = m_new / (1.0 - ADAM_B1 ** ADAM_STEP)
                v_hat = v_new / (1.0 - ADAM_B2 ** ADAM_STEP)
                g_ref[...] = g
                d_ref[...] = -ADAM_LR * (m_hat / (jnp.sqrt(v_hat) + ADAM_EPS) + ADAM_WD * w_ref[...])
                nm_ref[...] = m_new
                nv_ref[...] = v_new

    part_spec = lambda layer: pl.BlockSpec(
        (n, tb, c), lambda l, i: (0, jnp.clip(i + (l - layer) * nb, 0, nb - 1), 0))
    blk = pl.BlockSpec((None, tb, c), lambda l, i: (l, i, 0))
    res, landed = _pcall(
        body, name=name, grid=(layers, nb),
        in_specs=[part_spec(layer) for layer in range(layers)] + [blk, blk, blk],
        out_specs=[blk] * 4,
        out_shape=[jax.ShapeDtypeStruct((layers, r, c), F32)] * 4,
        semantics=("arbitrary", "arbitrary"), moves=moves,
    )(*parts, w, m, v)
    return (res, landed) if moves else res


def _as_rows(a):
    flat = a.reshape(-1)
    pad = (-flat.shape[0]) % PACK_W
    if pad:
        flat = jnp.pad(flat, (0, pad))
    return flat.reshape(-1, PACK_W)


def _cols_from_shards(g):
    return jnp.transpose(g, (1, 0, 2)).reshape(g.shape[1], -1)


def _cols_to_shards(a):
    return jnp.transpose(a.reshape(a.shape[0], N_DEV, -1), (1, 0, 2))


def _pad_lanes(a):
    return jnp.pad(a, ((0, 0), (0, LANES - a.shape[1])))


def kernel(x, norm_mix_g, norm_mlp_g, final_norm_g, w_in_even, w_out_even, gm_ln_g, gm_ln_b, gm_w_s, gm_b_s, ssm_conv_w, ssm_conv_b, ssm_dt_bias, ssm_a_log, ssm_d, ssm_norm_g, w_qkv, b_qkv, w_o, b_o, attn_sinks, w_up, w_down, loss_target, m_norm_mix_g, m_norm_mlp_g, m_final_norm_g, m_w_in_even, m_w_out_even, m_gm_ln_g, m_gm_ln_b, m_gm_w_s, m_gm_b_s, m_ssm_conv_w, m_ssm_conv_b, m_ssm_dt_bias, m_ssm_a_log, m_ssm_d, m_ssm_norm_g, m_w_qkv, m_b_qkv, m_w_o, m_b_o, m_attn_sinks, m_w_up, m_w_down, v_norm_mix_g, v_norm_mlp_g, v_final_norm_g, v_w_in_even, v_w_out_even, v_gm_ln_g, v_gm_ln_b, v_gm_w_s, v_gm_b_s, v_ssm_conv_w, v_ssm_conv_b, v_ssm_dt_bias, v_ssm_a_log, v_ssm_d, v_ssm_norm_g, v_w_qkv, v_b_qkv, v_w_o, v_b_o, v_attn_sinks, v_w_up, v_w_down):
    names = ["norm_mix_g", "norm_mlp_g", "final_norm_g", "w_in_even", "w_out_even", "gm_ln_g", "gm_ln_b", "gm_w_s",
             "gm_b_s", "ssm_conv_w", "ssm_conv_b", "ssm_dt_bias", "ssm_a_log", "ssm_d", "ssm_norm_g", "w_qkv",
             "b_qkv", "w_o", "b_o", "attn_sinks", "w_up", "w_down"]
    env = locals()
    W = {n: env[n] for n in names}
    M = {n: env["m_" + n] for n in names}
    V = {n: env["v_" + n] for n in names}
    big = ["w_in_even", "w_out_even", "w_qkv", "w_o", "w_up", "w_down"]
    small_sharded = ["ssm_conv_w", "b_qkv", "b_o"]
    replicated = [n for n in names if n not in big and n not in small_sharded]
    me = 4 * lax.axis_index("x") + 2 * lax.axis_index("y") + lax.axis_index("c")
    t = x.shape[1]
    xs = x.reshape(t, D_MODEL)
    target = loss_target.reshape(t, D_MODEL)
    gather = lambda a: _Move("gather", a)
    over_ici = lambda a: _Move("gather_ici", a)
    over_d2d = lambda a: _Move("gather_d2d", a)
    by_core = lambda a: a.reshape((N_CHIP, N_CORE) + a.shape[1:])
    to_sibling = lambda a: [_Move("scatter_d2d", by_core(a))]
    my_core = lax.axis_index("c").astype(jnp.int32).reshape(1)
    pair = lambda a, theirs, name: _pair_add(by_core(a), theirs, my_core, name=name)
    to_chips = lambda a: _Move("scatter_ici", a)
    whole = lambda a: a.reshape((N_DEV,) + a.shape[2:])
    row = lambda a: a.reshape(1, D_MODEL)
    add = lambda acc, res: (acc + res,)

    small_flat = jnp.concatenate([W[n].reshape(-1) for n in small_sharded])
    w_in_g, small_g = _exchange([over_ici(w_in_even[0].astype(BF16)), gather(_as_rows(small_flat))],
                                name="gather_w_in")
    y0, (w_in_g,) = _rms_fwd(xs, row(norm_mix_g[0]), name="rms_mix0", moves=[over_d2d(w_in_g)])
    w_in = _cols_from_shards(whole(w_in_g))
    w_uv = w_in[:, :2 * D_MODEL]
    w_rest = jnp.concatenate([w_in[:, 3 * D_MODEL:3 * D_MODEL + CONV_DIM], w_in[:, 2 * D_MODEL:3 * D_MODEL],
                              w_in[:, 3 * D_MODEL + CONV_DIM:],
                              jnp.zeros((D_MODEL, LANES - SSM_HEADS), BF16)], axis=1)
    small_all = small_g.reshape(N_DEV, -1)
    n_cw = SSM_CONV * CONV_DIM // N_DEV
    n_bq = QKV_DIM // N_DEV
    conv_w = _cols_from_shards(small_all[:, :n_cw].reshape(N_DEV, SSM_CONV, CONV_DIM // N_DEV))
    bqkv = small_all[:, n_cw:n_cw + n_bq].reshape(1, QKV_DIM)
    bo = small_all[:, n_cw + n_bq:n_cw + n_bq + D_MODEL // N_DEV].reshape(1, D_MODEL)

    conv_b = ssm_conv_b.reshape(1, CONV_DIM)
    dt_bias, a_log, d_skip = _pad_lanes(ssm_dt_bias), _pad_lanes(ssm_a_log), _pad_lanes(ssm_d)
    gm_w = gm_w_s[0]
    gm_b = gm_b_s[0].reshape(GM_GROUPS, CHUNK, 1)
    sink_col = jnp.repeat(attn_sinks.reshape(ATTN_HEADS), CHUNK).reshape(ATTN_KV, _Q_PER_KV * CHUNK, 1)
    w_up_b, w_down_b = w_up.astype(BF16), w_down.astype(BF16)

    up_cols = pl.BlockSpec((None, D_MODEL, D_FF // N_DEV), lambda i, j, kk: (j, kk, 0))
    up_cols_t = pl.BlockSpec((None, D_MODEL, D_FF // N_DEV), lambda i, j, kk: (kk, j, 0))
    down_rows = pl.BlockSpec((None, D_FF // N_DEV, D_MODEL), lambda i, j, kk: (kk, 0, j))
    down_rows_t = pl.BlockSpec((None, D_FF // N_DEV, D_MODEL), lambda i, j, kk: (j, 0, kk))
    ff_n = D_FF // N_DEV

    proj_uv, (w_out_g,) = _matmul(y0, w_uv, "nn", name="proj_uv", outs=[F32],
                                  moves=[over_ici(w_out_even[0].astype(BF16))])
    proj_rest, (w_up0_g,) = _matmul(y0, w_rest, "nn", name="proj_rest", outs=[F32], tn=640,
                                    moves=[over_ici(w_up_b[0])])
    mix, (w_qkv_g, w_o_g, w_out_g, w_up0_g) = _gmlp_fwd(
        proj_uv, gm_ln_g, gm_ln_b, gm_w, gm_b, name="gmlp_fwd",
        moves=[over_ici(w_qkv[0].astype(BF16)), over_ici(w_o[0].astype(BF16)), over_d2d(w_out_g), over_d2d(w_up0_g)])
    (mix, h_states), (w_down0_g,) = _ssd_fwd(
        proj_rest, mix, conv_w, conv_b, dt_bias, a_log, d_skip, ssm_norm_g, name="ssd_fwd",
        moves=[over_ici(w_down_b[0])])
    w_out_f = whole(w_out_g).reshape(2 * D_MODEL, D_MODEL)
    h1, (w_down0_g, w_qkv_g, w_o_g) = _matmul(
        mix, w_out_f, "nn", name="mix_out", outs=[F32], extras=[(xs, "tile")], epilogue=add,
        moves=[over_d2d(w_down0_g), over_d2d(w_qkv_g), over_d2d(w_o_g)])

    def mlp_fwd(h, layer, w_up_layer, w_down_layer, up_moves=(), down_moves=lambda landed: ()):
        y = _rms_fwd(h, row(norm_mlp_g[layer]), name=f"rms_mlp{layer}")
        res = _matmul(y, whole(w_up_layer), "nn", name=f"mlp_up{layer}", outs=[F32, BF16],
                      epilogue=lambda acc: (acc, jnp.square(jnp.maximum(acc, 0.0))),
                      dims=(t, D_FF, D_MODEL), tn=ff_n, b_spec=up_cols, moves=up_moves)
        (up, act), up_landed = res if up_moves else (res, [])
        res = _matmul(act, whole(w_down_layer), "nn", name=f"mlp_down{layer}", outs=[F32], extras=[(h, "tile")],
                      epilogue=add, dims=(t, D_MODEL, D_FF), tk=ff_n, b_spec=down_rows, moves=down_moves(up_landed))
        h_new, down_landed = res if down_moves(up_landed) else (res, [])
        return y, up, act, h_new, up_landed, down_landed

    y1, up0, act0, h2, (w_up1_g,), (w_down1_g, w_up1_g) = mlp_fwd(
        h1, 0, w_up0_g, w_down0_g, up_moves=[over_ici(w_up_b[1])],
        down_moves=lambda landed: [over_ici(w_down_b[1]), over_d2d(landed[0])])
    y2, (w_down1_g,) = _rms_fwd(h2, row(norm_mix_g[1]), name="rms_mix1", moves=[over_d2d(w_down1_g)])
    w_up_g = [whole(w_up0_g), whole(w_up1_g)]
    w_down_g = [whole(w_down0_g), whole(w_down1_g)]
    wqkv = _cols_from_shards(whole(w_qkv_g))
    wo = whole(w_o_g).reshape(D_MODEL, D_MODEL)
    qkv = _matmul(y2, wqkv, "nn", name="qkv", outs=[F32], extras=[(bqkv, "row")], epilogue=lambda acc, b: (acc + b,),
                  tn=640)
    heads = lambda a, n: jnp.transpose(a.reshape(t, n, HEAD_DIM), (1, 0, 2))
    q = heads(qkv[:, :D_MODEL], ATTN_HEADS)
    k = heads(qkv[:, D_MODEL:D_MODEL + ATTN_KV * HEAD_DIM], ATTN_KV)
    v = heads(qkv[:, D_MODEL + ATTN_KV * HEAD_DIM:], ATTN_KV)
    attn = _attn_fwd(q, k, v, sink_col, name="attn_fwd")
    attn = jnp.transpose(attn, (1, 0, 2)).reshape(t, D_MODEL)
    h3 = _matmul(attn, wo, "nn", name="attn_out", outs=[F32], extras=[(h2, "tile"), (bo, "row")],
                 epilogue=lambda acc, res, b: (acc + res + b,))
    y3, up1, act1, h4, _, _ = mlp_fwd(h3, 1, w_up1_g, w_down1_g)
    loss_part, dh4, d_final_g = _final_loss(h4, row(final_norm_g), target, name="final_loss")

    by_dev_rows = lambda a: a.reshape((N_DEV, a.shape[0] // N_DEV) + a.shape[1:])

    def mlp_bwd(dh, h, y, up, act, layer, first_moves=()):
        res = _matmul(dh, w_down_g[layer], "nt", name=f"mlp_down_dx{layer}", outs=[BF16], extras=[(up, "tile")],
                      epilogue=lambda acc, u: (acc * (2.0 * jnp.maximum(u, 0.0)),),
                      dims=(t, D_FF, D_MODEL), tn=ff_n, b_spec=down_rows_t, moves=first_moves)
        d_up, first_landed = res if first_moves else (res, [])
        g_down = _matmul(act, dh, "tn", name=f"mlp_down_dw{layer}", outs=[BF16])
        g_down = by_dev_rows(g_down)
        g_up, (theirs,) = _matmul(y, d_up, "tn", name=f"mlp_up_dw{layer}", outs=[BF16],
                                  dims=(D_MODEL, D_FF, t), tn=ff_n,
                                  out_spec=pl.BlockSpec((None, D_MODEL, ff_n), lambda i, j, kk: (j, i, 0)),
                                  out_shape=(N_DEV, D_MODEL, ff_n), moves=to_sibling(g_down))
        q_down = pair(g_down, theirs, f"mlp_down_pair{layer}")
        dy, (r_down, theirs) = _matmul(d_up, w_up_g[layer], "nt", name=f"mlp_up_dx{layer}", outs=[F32],
                                       dims=(t, D_MODEL, D_FF), tk=ff_n, b_spec=up_cols_t,
                                       moves=[to_chips(q_down)] + to_sibling(g_up))
        q_up = pair(g_up, theirs, f"mlp_up_pair{layer}")
        dh_new, dg, cs = _rms_bwd(dy, h, row(norm_mlp_g[layer]), dh, name=f"rms_mlp_bwd{layer}")
        return dh_new, cs, dg, q_up, r_down, first_landed

    dh3, cs3, g_nmlp1, q_up1, r_down1, _ = mlp_bwd(dh4, h3, y3, up1, act1, 1)
    g_bo = cs3
    g_wo = _matmul(attn, dh3, "tn", name="attn_out_dw", outs=[BF16])
    g_wo = by_dev_rows(g_wo)
    d_attn, (theirs,) = _matmul(dh3, wo, "nt", name="attn_out_dx", outs=[F32], moves=to_sibling(g_wo))
    q_wo = pair(g_wo, theirs, "attn_out_pair")
    d_o = jnp.transpose(d_attn.reshape(t, ATTN_HEADS, HEAD_DIM), (1, 0, 2))
    (dq, dk, dv, d_sink), (r_up1, r_wo) = _attn_bwd(q, k, v, sink_col, d_o, name="attn_bwd",
                                                    moves=[to_chips(q_up1), to_chips(q_wo)])
    unheads = lambda a: jnp.transpose(a, (1, 0, 2)).reshape(t, -1)
    dqkv = jnp.concatenate([unheads(dq), unheads(dk), unheads(dv)], axis=1)
    g_bqkv = _colsum(dqkv, name="qkv_db")
    g_wqkv = _matmul(y2, dqkv, "tn", name="qkv_dw", outs=[BF16], tn=640)
    g_wqkv = _cols_to_shards(g_wqkv)
    dy2, (theirs,) = _matmul(dqkv, wqkv, "nt", name="qkv_dx", outs=[F32], tk=640, moves=to_sibling(g_wqkv))
    q_wqkv = pair(g_wqkv, theirs, "qkv_pair")
    dh2, g_nmix1, _ = _rms_bwd(dy2, h2, row(norm_mix_g[1]), dh3, name="rms_mix1_bwd")
    dh1, _, g_nmlp0, q_up0, r_down0, (r_wqkv,) = mlp_bwd(dh2, h1, y1, up0, act0, 0, first_moves=[to_chips(q_wqkv)])

    d_mix = _matmul(dh1, w_out_f, "nt", name="mix_out_dx", outs=[F32])
    g_wout = _matmul(mix, dh1, "tn", name="mix_out_dw", outs=[BF16])
    g_wout = by_dev_rows(g_wout)
    (d_uv, g_ln_g, g_ln_b, g_gm_w, g_gm_b), (r_up0, theirs) = _gmlp_bwd(
        proj_uv, d_mix, gm_ln_g, gm_ln_b, gm_w, gm_b, name="gmlp_bwd", moves=[to_chips(q_up0)] + to_sibling(g_wout))
    q_wout = pair(g_wout, theirs, "mix_out_pair")

    early = [("norm_mlp_g", None), ("final_norm_g", None), ("norm_mix_g", 1), ("gm_ln_g", None), ("gm_ln_b", None),
             ("gm_w_s", None), ("gm_b_s", None), ("attn_sinks", None)]
    late = [("norm_mix_g", 0), ("ssm_conv_b", None), ("ssm_dt_bias", None), ("ssm_a_log", None), ("ssm_d", None),
            ("ssm_norm_g", None)]
    early_sharded, late_sharded = ["b_qkv", "b_o"], ["ssm_conv_w"]
    small_grads = {
        ("norm_mlp_g", None): jnp.concatenate([g_nmlp0, g_nmlp1], axis=0),
        ("final_norm_g", None): d_final_g, ("norm_mix_g", 1): g_nmix1,
        ("gm_ln_g", None): g_ln_g, ("gm_ln_b", None): g_ln_b, ("gm_w_s", None): g_gm_w, ("gm_b_s", None): g_gm_b,
        ("attn_sinks", None): jnp.sum(d_sink.reshape(ATTN_HEADS, CHUNK), axis=1),
        "b_qkv": g_bqkv, "b_o": g_bo,
    }
    pack = lambda keys: _as_rows(jnp.concatenate([small_grads[key].reshape(-1) for key in keys]))
    (dpre, dz, ddt, g_dtb, g_alog, g_dskip, g_ssm_ng), (r_wout, early_recv) = _ssd_bwd(
        proj_rest, h_states, d_mix, conv_w, conv_b, dt_bias, a_log, d_skip, ssm_norm_g, name="ssd_bwd",
        moves=[to_chips(q_wout), gather(pack(early + early_sharded))])
    d_rest, g_conv_w, g_conv_b = _conv_bwd(proj_rest, dpre, dz, ddt, conv_w, name="conv_bwd")
    g_w_uv = _matmul(y0, d_uv, "tn", name="proj_uv_dw", outs=[BF16])
    g_w_rest = _matmul(y0, d_rest, "tn", name="proj_rest_dw", outs=[BF16], tn=640)
    g_w_in = jnp.concatenate([g_w_uv, g_w_rest[:, CONV_DIM:CONV_DIM + D_MODEL], g_w_rest[:, :CONV_DIM],
                              g_w_rest[:, CONV_DIM + D_MODEL:CONV_DIM + D_MODEL + SSM_HEADS]], axis=1)
    g_w_in = _cols_to_shards(g_w_in)
    dy0, (theirs,) = _matmul(d_uv, w_uv, "nt", name="proj_uv_dx", outs=[F32], moves=to_sibling(g_w_in))
    q_w_in = pair(g_w_in, theirs, "proj_pair")
    dy0, (r_w_in,) = _matmul(d_rest, w_rest, "nt", name="proj_rest_dx", outs=[F32], extras=[(dy0, "tile")],
                             epilogue=add, tk=640, moves=[to_chips(q_w_in)])
    dx, g_nmix0, _ = _rms_bwd(dy0, xs, row(norm_mix_g[0]), dh1, name="rms_mix0_bwd")
    small_grads.update({
        ("norm_mix_g", 0): g_nmix0, ("ssm_conv_b", None): g_conv_b,
        ("ssm_dt_bias", None): g_dtb[:, :SSM_HEADS], ("ssm_a_log", None): g_alog[:, :SSM_HEADS],
        ("ssm_d", None): g_dskip[:, :SSM_HEADS], ("ssm_norm_g", None): g_ssm_ng, "ssm_conv_w": g_conv_w,
    })


    def update(n, parts, moves=()):
        shape = W[n].shape
        as3 = lambda a: a.reshape((len(parts),) + parts[0].shape[1:])
        res = _adamw(parts, as3(W[n]), as3(M[n]), as3(V[n]), name="adamw_" + n, moves=moves)
        res, landed = res if moves else (res, [])
        return [a.reshape(shape) for a in res], landed

    out = {}
    out["w_o"], (late_recv,) = update("w_o", [r_wo], moves=[gather(pack(late + late_sharded))])
    out["w_down"], _ = update("w_down", [r_down0, r_down1])
    out["w_up"], _ = update("w_up", [r_up0, r_up1])
    out["w_out_even"], _ = update("w_out_even", [r_wout])
    out["w_qkv"], _ = update("w_qkv", [r_wqkv])
    out["w_in_even"], _ = update("w_in_even", [r_w_in])

    def unpacked(recv, keys):
        flat, res, o = recv.reshape(N_DEV, -1), {}, 0
        for key in keys:
            res[key] = flat[:, o:o + small_grads[key].size]
            o += small_grads[key].size
        return res

    arrived = {**unpacked(early_recv, early + early_sharded), **unpacked(late_recv, late + late_sharded)}
    piece = lambda tree, key: tree[key[0]] if key[1] is None else tree[key[0]][key[1]]

    def rows_by_device(cat):
        pad = (-cat.shape[1]) % PACK_W
        return jnp.pad(cat, ((0, 0), (0, pad))).reshape(N_DEV, -1, PACK_W)

    rep_keys = early + late
    rep_parts = rows_by_device(jnp.concatenate([arrived[key] for key in rep_keys], axis=1))
    flat_rep = lambda tree: _as_rows(jnp.concatenate([piece(tree, key).reshape(-1) for key in rep_keys]))[None]
    rep_res = _adamw([rep_parts], flat_rep(W), flat_rep(M), flat_rep(V), name="adamw_replicated")
    sh_keys = early_sharded + late_sharded
    shard_parts = []
    for n in sh_keys:
        full = arrived[n].reshape((N_DEV,) + small_grads[n].shape)
        c = full.shape[-1] // N_DEV
        shard_parts.append(lax.dynamic_slice_in_dim(full, me * c, c, axis=full.ndim - 1).reshape(N_DEV, -1))
    sh_rows = rows_by_device(jnp.concatenate(shard_parts, axis=1))
    flat_sh = lambda tree: _as_rows(jnp.concatenate([tree[n].reshape(-1) for n in sh_keys]))[None]
    sh_res = _adamw([sh_rows], flat_sh(W), flat_sh(M), flat_sh(V), name="adamw_small_sharded")

    def unpack_replicated(rows):
        flat, vals, o = rows.reshape(-1), {}, 0
        for key in rep_keys:
            size = piece(W, key).size
            vals[key] = flat[o:o + size]
            o += size
        res = {}
        for n in replicated:
            if (n, None) in vals:
                res[n] = vals[(n, None)].reshape(W[n].shape)
            else:
                res[n] = jnp.stack([vals[(n, r)] for r in range(W[n].shape[0])]).reshape(W[n].shape)
        return res

    def unpack_sharded(rows):
        flat, res, o = rows.reshape(-1), {}, 0
        for n in sh_keys:
            res[n] = flat[o:o + W[n].size].reshape(W[n].shape)
            o += W[n].size
        return res

    results = []
    for idx in range(4):
        d = {n: out[n][idx] for n in big}
        d.update(unpack_replicated(rep_res[idx]))
        d.update(unpack_sharded(sh_res[idx]))
        results.append(d)

    loss = lax.psum(loss_part[0, 0], ("x", "y", "c"))
    grad_x = dx.reshape(x.shape)
    final = [loss, grad_x]
    for d in results:
        final.extend(d[n] for n in names)
    return tuple(final)
```

```python
import dataclasses
import functools

import jax
import jax.numpy as jnp
from jax import lax
from jax.experimental import pallas as pl
from jax.experimental.pallas import tpu as pltpu

F32 = jnp.float32
BF16 = jnp.bfloat16

N_DEV = 8
D_MODEL = 1024
D_FF = 4096
RMS_EPS = 1e-5
LN_EPS = 1e-5
CHUNK = 128
GM_GROUPS = 8
SSM_HEADS = 16
SSM_HEADDIM = 64
SSM_GROUPS = 4
SSM_STATE = 128
SSM_CONV = 4
CONV_DIM = 2048
IN_EVEN = 5136
REST_W = 3200
ATTN_HEADS = 16
ATTN_KV = 2
HEAD_DIM = 64
QKV_DIM = 1280
LANES = 128
HALO = 8
PACK_W = 1024

ADAM_LR = 0.001
ADAM_B1 = 0.9
ADAM_B2 = 0.999
ADAM_EPS = 1e-08
ADAM_WD = 0.01
ADAM_STEP = 10

VMEM_LIMIT_BYTES = 56 * 1024 * 1024


_NN = (((1,), (0,)), ((), ()))
_NT = (((1,), (1,)), ((), ()))
_TN = (((0,), (0,)), ((), ()))


def _dg(a, b, dims):
    return lax.dot_general(a.astype(BF16), b.astype(BF16), dims, preferred_element_type=F32)


@jax.custom_vjp
def _nn(a, b):
    return _dg(a, b, _NN)


@jax.custom_vjp
def _nt(a, b):
    return _dg(a, b, _NT)


@jax.custom_vjp
def _tn(a, b):
    return _dg(a, b, _TN)


_nn.defvjp(lambda a, b: (_dg(a, b, _NN), (a, b)), lambda r, g: (_nt(g, r[1]), _tn(r[0], g)))
_nt.defvjp(lambda a, b: (_dg(a, b, _NT), (a, b)), lambda r, g: (_nn(g, r[1]), _tn(g, r[0])))
_tn.defvjp(lambda a, b: (_dg(a, b, _TN), (a, b)), lambda r, g: (_nt(r[1], g), _nn(r[0], g)))


def _split3_dot(tri, x):
    x1 = x.astype(BF16)
    r1 = x - x1.astype(F32)
    x2 = r1.astype(BF16)
    x3 = (r1 - x2.astype(F32)).astype(BF16)
    t = tri.astype(BF16)
    dot = lambda p: lax.dot_general(t, p, _NN, preferred_element_type=F32)
    return dot(x1) + dot(x2) + dot(x3)


def _tri(lower):
    r = lax.broadcasted_iota(jnp.int32, (CHUNK, CHUNK), 0)
    c = lax.broadcasted_iota(jnp.int32, (CHUNK, CHUNK), 1)
    return jnp.where((r >= c) if lower else (r <= c), 1.0, 0.0).astype(F32)


@jax.custom_vjp
def _cumsum_rows(x):
    return _split3_dot(_tri(True), x)


_cumsum_rows.defvjp(lambda x: (_split3_dot(_tri(True), x), None), lambda _, g: (_split3_dot(_tri(False), g),))


def _sigmoid(x):
    return 1.0 / (1.0 + jnp.exp(-x))


def _silu(x):
    return x * _sigmoid(x)


def _softplus(x):
    return jnp.maximum(x, 0.0) + jnp.log(1.0 + jnp.exp(-jnp.abs(x)))


def _gelu_tanh(x):
    return 0.5 * x * (1.0 + jnp.tanh(0.7978845608028654 * (x + 0.044715 * (x * x * x))))


def _rmsnorm(x, g):
    return x * lax.rsqrt(jnp.mean(x * x, axis=-1, keepdims=True) + RMS_EPS) * g


def _gmlp_chunk(u, v, ln_g, ln_b, w_s, b_s):
    gu = _gelu_tanh(u)
    gv = _gelu_tanh(v)
    mu = jnp.mean(gv, axis=-1, keepdims=True)
    var = jnp.mean(jnp.square(gv - mu), axis=-1, keepdims=True)
    vn = (gv - mu) * lax.rsqrt(var + LN_EPS) * ln_g + ln_b
    r = lax.broadcasted_iota(jnp.int32, (CHUNK, CHUNK), 0)
    c = lax.broadcasted_iota(jnp.int32, (CHUNK, CHUNK), 1)
    causal = r >= c
    outs = []
    for g in range(GM_GROUPS):
        cols = slice(g * LANES, (g + 1) * LANES)
        mixed = _nn(jnp.where(causal, w_s[g], 0.0), vn[:, cols]) + b_s[g]
        outs.append(gu[:, cols] * mixed)
    return jnp.concatenate(outs, axis=1)


def _lane_pick(row, h):
    lane = lax.broadcasted_iota(jnp.int32, row.shape, 1)
    return jnp.sum(jnp.where(lane == h, row, 0.0), axis=1, keepdims=True)


def _col_pick(m, h):
    lane = lax.broadcasted_iota(jnp.int32, m.shape, 1)
    return jnp.sum(jnp.where(lane == h, m, 0.0), axis=1, keepdims=True)


def _row_pick(m, h):
    sub = lax.broadcasted_iota(jnp.int32, m.shape, 0)
    return jnp.sum(jnp.where(sub == h, m, 0.0), axis=0, keepdims=True)


_PAIRS = SSM_HEADS // 2


def _ssd_chunk(pre, z, dt_raw, h_prev, dt_bias, a_log, d_skip, norm_g):
    xbc = _silu(pre)
    dt = _softplus(dt_raw + dt_bias)
    da = dt * (-jnp.exp(a_log))
    a_cum = _cumsum_rows(da)
    a_cum_t = a_cum.T
    dt_t = dt.T
    r = lax.broadcasted_iota(jnp.int32, (CHUNK, CHUNK), 0)
    c = lax.broadcasted_iota(jnp.int32, (CHUNK, CHUNK), 1)
    causal = r >= c
    lane_lo = lax.broadcasted_iota(jnp.int32, (1, LANES), 1) < SSM_HEADDIM
    last_row = lax.broadcasted_iota(jnp.int32, (CHUNK, 1), 0) == CHUNK - 1
    ys, h_next = [], []
    for j in range(_PAIRS):
        g = j // 2
        xs = xbc[:, j * LANES:(j + 1) * LANES]
        bm = xbc[:, 1024 + g * SSM_STATE:1024 + (g + 1) * SSM_STATE]
        cm = xbc[:, 1536 + g * SSM_STATE:1536 + (g + 1) * SSM_STATE]
        cb = _nt(cm, bm)
        y_diag, to_end, e_cum, c_dec, d_row = [], [], [], [], []
        for h in (2 * j, 2 * j + 1):
            col = _col_pick(a_cum, h)
            row = _row_pick(a_cum_t, h)
            dt_col = _col_pick(dt, h)
            dt_row = _row_pick(dt_t, h)
            decay = jnp.exp(jnp.where(causal, col - row, -jnp.inf))
            y_diag.append(_nn(cb * decay * dt_row, xs))
            last = jnp.sum(jnp.where(last_row, col, 0.0), axis=0, keepdims=True)
            to_end.append(jnp.exp(last - col) * dt_col)
            e_cum.append(jnp.exp(col))
            c_dec.append(jnp.exp(last))
            d_row.append(_lane_pick(d_skip, h))
        pair = lambda lo_hi: jnp.where(lane_lo, lo_hi[0], lo_hi[1])
        states = _tn(bm, xs * pair(to_end))
        y_off = _nn(cm, h_prev[j]) * pair(e_cum)
        ys.append(pair(y_diag) + y_off + xs * pair(d_row))
        h_next.append(pair(c_dec) * h_prev[j] + states)
    y = jnp.concatenate(ys, axis=1) * _silu(z)
    width = D_MODEL // SSM_GROUPS
    y = jnp.concatenate(
        [_rmsnorm(y[:, g * width:(g + 1) * width], norm_g[:, g * width:(g + 1) * width]) for g in range(SSM_GROUPS)],
        axis=1)
    return y, tuple(h_next)


def _shift_down(prev8, x, k):
    if k == 0:
        return x
    win = jnp.concatenate([prev8, x], axis=0)
    return pltpu.roll(win, k, 0)[HALO:]


def _shift_up(x, next8, k):
    if k == 0:
        return x
    n = x.shape[0]
    win = jnp.concatenate([x, next8], axis=0)
    return pltpu.roll(win, n + HALO - k, 0)[:n]


def _conv_pre(prev8, x, w, b):
    out = b + x * w[SSM_CONV - 1:SSM_CONV]
    for i in range(SSM_CONV - 1):
        out = out + _shift_down(prev8, x, SSM_CONV - 1 - i) * w[i:i + 1]
    return out


def _attn_block(q, k_prev, k_cur, v_prev, v_cur, sink, first):
    k = jnp.concatenate([k_prev, k_cur], axis=0)
    v = jnp.concatenate([v_prev, v_cur], axis=0)
    s = _nt(q, k) * (HEAD_DIM ** -0.5)
    rows = lax.broadcasted_iota(jnp.int32, s.shape, 0) & (CHUNK - 1)
    cols = lax.broadcasted_iota(jnp.int32, s.shape, 1)
    valid = (cols <= rows + CHUNK) & (cols > rows) & (cols >= CHUNK * first.astype(jnp.int32))
    s = jnp.where(valid, s, -jnp.inf)
    m = jnp.maximum(jnp.max(s, axis=-1, keepdims=True), sink)
    p = jnp.exp(s - m)
    denom = jnp.sum(p, axis=-1, keepdims=True) + jnp.exp(sink - m)
    return _nn(p / denom, v)


N_CHIP = 4
N_CORE = 2
_OTHER_CHIPS = (2, 4, 6)


@dataclasses.dataclass
class _Move:
    kind: str
    src: jax.Array

    def dst_shape(self):
        s = self.src.shape
        shape = {"gather": (N_DEV,) + s, "gather_ici": (N_CHIP, N_CORE) + s, "gather_d2d": s,
                 "scatter_d2d": (N_CHIP,) + s[2:], "scatter_ici": s}[self.kind]
        return jax.ShapeDtypeStruct(tuple(shape), self.src.dtype)


def _peer(x, y, c, k):
    return (1 - x if k & 4 else x, 1 - y if k & 2 else y, 1 - c if k & 1 else c)


def _move_copies(moves, srcs, dsts, send_sems, recv_sems, local_sems):
    x, y, c = lax.axis_index("x"), lax.axis_index("y"), lax.axis_index("c")
    chip = 2 * x + y
    me = 2 * chip + c
    sibling = (x, y, 1 - c)
    all_chips = pl.ds(0, N_CHIP)
    local, remote = [], []

    def push(n, k, src, dst, device):
        remote.append(pltpu.make_async_remote_copy(
            src_ref=src, dst_ref=dst, send_sem=send_sems.at[n, k], recv_sem=recv_sems.at[n, k],
            device_id=device, device_id_type=pl.DeviceIdType.MESH))

    for n, mv in enumerate(moves):
        s, d = srcs[n], dsts[n]
        if mv.kind == "gather":
            local.append(pltpu.make_async_copy(s, d.at[me], local_sems.at[n]))
            for k in range(1, N_DEV):
                push(n, k - 1, s, d.at[me], _peer(x, y, c, k))
        elif mv.kind == "gather_ici":
            local.append(pltpu.make_async_copy(s, d.at[chip, c], local_sems.at[n]))
            for k in _OTHER_CHIPS:
                push(n, k - 1, s, d.at[chip, c], _peer(x, y, c, k))
        elif mv.kind == "gather_d2d":
            push(n, 0, d.at[all_chips, c], d.at[all_chips, c], sibling)
        elif mv.kind == "scatter_d2d":
            push(n, 0, s.at[all_chips, 1 - c], d, sibling)
        else:
            assert mv.kind == "scatter_ici", mv.kind
            local.append(pltpu.make_async_copy(s.at[chip], d.at[chip], local_sems.at[n]))
            for k in _OTHER_CHIPS:
                px, py, _ = _peer(x, y, c, k)
                push(n, k - 1, s.at[2 * px + py], d.at[chip], (px, py, c))
    return local, remote


def _move_aliases(moves, n_in, n_out):
    return {n_in + n: n_out + n for n, mv in enumerate(moves) if mv.kind == "gather_d2d"}


def _pcall(body, *, name, grid, in_specs, out_specs, out_shape, scratch_shapes=(), semantics=(), moves=(),
           aliases=None):
    out_shape, out_specs = list(out_shape), list(out_specs)
    in_specs = list(in_specs)
    if not moves:
        call = pl.pallas_call(
            body, name=name, grid=grid, in_specs=in_specs, out_specs=out_specs, out_shape=out_shape,
            scratch_shapes=list(scratch_shapes), input_output_aliases=aliases or {},
            compiler_params=pltpu.CompilerParams(dimension_semantics=tuple(semantics),
                                                 vmem_limit_bytes=VMEM_LIMIT_BYTES))
        return (lambda *args: (list(call(*args)), []))
    n_in, n_out, n_scr, n_mv = len(in_specs), len(out_shape), len(scratch_shapes), len(moves)
    hbm = pl.BlockSpec(memory_space=pltpu.HBM)

    def carrier(*refs):
        ins, rest = refs[:n_in], refs[n_in:]
        srcs, rest = rest[:n_mv], rest[n_mv:]
        outs, rest = rest[:n_out], rest[n_out:]
        dsts, rest = rest[:n_mv], rest[n_mv:]
        scr, (send_sems, recv_sems, local_sems) = rest[:n_scr], rest[n_scr:]
        first = functools.reduce(jnp.logical_and, [pl.program_id(d) == 0 for d in range(len(grid))])
        last = functools.reduce(jnp.logical_and, [pl.program_id(d) == grid[d] - 1 for d in range(len(grid))])

        @pl.when(first)
        def _():
            local, remote = _move_copies(moves, srcs, dsts, send_sems, recv_sems, local_sems)
            for cp in local + remote:
                cp.start()

        body(*ins, *outs, *scr)

        @pl.when(last)
        def _():
            local, remote = _move_copies(moves, srcs, dsts, send_sems, recv_sems, local_sems)
            for cp in remote + local:
                cp.wait()

    call = pl.pallas_call(
        carrier, name=name, grid=grid,
        in_specs=in_specs + [hbm] * n_mv,
        out_specs=out_specs + [hbm] * n_mv,
        out_shape=out_shape + [mv.dst_shape() for mv in moves],
        scratch_shapes=list(scratch_shapes) + [pltpu.SemaphoreType.DMA((n_mv, N_DEV - 1)),
                                               pltpu.SemaphoreType.DMA((n_mv, N_DEV - 1)),
                                               pltpu.SemaphoreType.DMA((n_mv,))],
        input_output_aliases={**(aliases or {}), **_move_aliases(moves, n_in, n_out)},
        compiler_params=pltpu.CompilerParams(dimension_semantics=("arbitrary",) * len(grid),
                                             vmem_limit_bytes=VMEM_LIMIT_BYTES))

    def run(*args):
        res = list(call(*args, *[mv.src for mv in moves]))
        return res[:n_out], res[n_out:]

    return run


def _exchange(moves, *, name, then_d2d=()):
    n_mv, n_fwd = len(moves), len(then_d2d)
    hbm = pl.BlockSpec(memory_space=pltpu.HBM)
    second = [_Move("gather_d2d", moves[n].src) for n in then_d2d]

    def body(*refs):
        srcs, dsts, sems = refs[:n_mv], refs[n_mv:2 * n_mv], refs[2 * n_mv:]
        local, remote = _move_copies(moves, srcs, dsts, *sems[:3])
        for cp in local + remote:
            cp.start()
        for cp in remote + local:
            cp.wait()
        if second:
            landed = [dsts[n] for n in then_d2d]
            _, remote = _move_copies(second, landed, landed, sems[3], sems[4], None)
            for cp in remote:
                cp.start()
            for cp in remote:
                cp.wait()

    sems = [pltpu.SemaphoreType.DMA((n_mv, N_DEV - 1)), pltpu.SemaphoreType.DMA((n_mv, N_DEV - 1)),
            pltpu.SemaphoreType.DMA((n_mv,))]
    if second:
        sems += [pltpu.SemaphoreType.DMA((n_fwd, N_DEV - 1)), pltpu.SemaphoreType.DMA((n_fwd, N_DEV - 1))]
    return list(pl.pallas_call(
        body, name=name, in_specs=[hbm] * n_mv, out_specs=[hbm] * n_mv,
        out_shape=[mv.dst_shape() for mv in moves], scratch_shapes=sems,
    )(*[mv.src for mv in moves]))


TM = 512
FF_SHARD = D_FF // N_DEV


def _whole(a):
    nd = a.ndim
    return pl.BlockSpec(a.shape, lambda i: (0,) * nd)


def _rows(width, col=0):
    return pl.BlockSpec((TM, width), lambda i: (i, col))


def _acc_row(width):
    return pl.BlockSpec((1, width), lambda i: (0, 0))


def _unpack(res_landed, moves, n_out):
    res, landed = res_landed
    res = res[0] if n_out == 1 else res
    return (res, landed) if moves else res


def _norm_matmul(x, g, w, *, name, emit_y, moves=()):
    t, d = x.shape
    n = w.shape[1]

    def body(x_ref, g_ref, w_ref, *outs):
        y = _rmsnorm(x_ref[...], g_ref[...]).astype(BF16)
        if emit_y:
            outs[0][...] = y
        outs[-1][...] = lax.dot_general(y, w_ref[...], _NN, preferred_element_type=F32)

    shapes = ([jax.ShapeDtypeStruct((t, d), BF16)] if emit_y else []) + [jax.ShapeDtypeStruct((t, n), F32)]
    specs = ([_rows(d)] if emit_y else []) + [_rows(n)]
    return _unpack(_pcall(body, name=name, grid=(t // TM,), in_specs=[_rows(d), _acc_row(d), _whole(w)],
                          out_specs=specs, out_shape=shapes, semantics=("parallel",), moves=moves)(x, g, w),
                   moves, len(shapes))


def _residual_matmul(a, w, res, *, name, bias=None, norm_g=None, moves=()):
    t, k = a.shape
    n = w.shape[1]
    has_res, has_bias, has_norm = res is not None, bias is not None, norm_g is not None

    def body(a_ref, w_ref, *rest):
        rest = list(rest)
        res_ref = rest.pop(0) if has_res else None
        b_ref = rest.pop(0) if has_bias else None
        g_ref = rest.pop(0) if has_norm else None
        h = lax.dot_general(a_ref[...].astype(BF16), w_ref[...], _NN, preferred_element_type=F32)
        if has_res:
            h = h + res_ref[...]
        if has_bias:
            h = h + b_ref[...]
        rest[0][...] = h
        if has_norm:
            rest[1][...] = _rmsnorm(h, g_ref[...]).astype(BF16)

    rows_in = [res] if has_res else []
    extra = ([bias] if has_bias else []) + ([norm_g] if has_norm else [])
    shapes = [jax.ShapeDtypeStruct((t, n), F32)] + ([jax.ShapeDtypeStruct((t, n), BF16)] if has_norm else [])
    return _unpack(_pcall(body, name=name, grid=(t // TM,),
                          in_specs=[_rows(k), _whole(w)] + [_rows(n)] * len(rows_in) + [_acc_row(n)] * len(extra),
                          out_specs=[_rows(n)] * len(shapes), out_shape=shapes, semantics=("parallel",),
                          moves=moves)(a, w, *rows_in, *extra), moves, len(shapes))


def _mlp_up(y, w_cols, *, name, moves=()):
    t, d = y.shape

    def body(y_ref, w_ref, up_ref):
        yv = y_ref[...]
        for j in range(N_DEV):
            up_ref[:, j * FF_SHARD:(j + 1) * FF_SHARD] = lax.dot_general(
                yv, w_ref[j], _NN, preferred_element_type=F32).astype(up_ref.dtype)

    return _unpack(_pcall(body, name=name, grid=(t // TM,), in_specs=[_rows(d), _whole(w_cols)],
                          out_specs=[_rows(D_FF)], out_shape=[jax.ShapeDtypeStruct((t, D_FF), BF16)],
                          semantics=("parallel",), moves=moves)(y, w_cols), moves, 1)


def _sq_relu(u):
    return jnp.square(jnp.maximum(u.astype(F32), 0.0))


def _mlp_down(up, w_rows, res, *, name, norm_g=None, moves=()):
    t = up.shape[0]
    has_norm = norm_g is not None

    def body(up_ref, w_ref, res_ref, *rest):
        h = res_ref[...]
        for j in range(N_DEV):
            act = _sq_relu(up_ref[:, j * FF_SHARD:(j + 1) * FF_SHARD]).astype(BF16)
            h = h + lax.dot_general(act, w_ref[j], _NN, preferred_element_type=F32)
        if has_norm:
            g_ref, h_ref, y_ref = rest
            y_ref[...] = _rmsnorm(h, g_ref[...]).astype(BF16)
        else:
            (h_ref,) = rest
        h_ref[...] = h

    shapes = [jax.ShapeDtypeStruct((t, D_MODEL), F32)] + ([jax.ShapeDtypeStruct((t, D_MODEL), BF16)] if has_norm else [])
    return _unpack(_pcall(body, name=name, grid=(t // TM,),
                          in_specs=[_rows(D_FF), _whole(w_rows), _rows(D_MODEL)] + ([_acc_row(D_MODEL)] if has_norm else []),
                          out_specs=[_rows(D_MODEL)] * len(shapes), out_shape=shapes, semantics=("parallel",),
                          moves=moves)(up, w_rows, res, *([norm_g] if has_norm else [])), moves, len(shapes))


def _mlp_down_dx(dh, w_rows, up, *, name, moves=()):
    t = up.shape[0]

    def body(dh_ref, w_ref, up_ref, o_ref):
        dhv = dh_ref[...]
        for j in range(N_DEV):
            cols = slice(j * FF_SHARD, (j + 1) * FF_SHARD)
            d_act = lax.dot_general(dhv, w_ref[j], _NT, preferred_element_type=F32)
            o_ref[:, cols] = (d_act * (2.0 * jnp.maximum(up_ref[:, cols].astype(F32), 0.0))).astype(o_ref.dtype)

    return _unpack(_pcall(body, name=name, grid=(t // TM,),
                          in_specs=[_rows(D_MODEL), _whole(w_rows), _rows(D_FF)],
                          out_specs=[_rows(D_FF)], out_shape=[jax.ShapeDtypeStruct((t, D_FF), BF16)],
                          semantics=("parallel",), moves=moves)(dh, w_rows, up), moves, 1)


def _dw_by_cols(x, dy, *, name, tn, by_device=False, moves=()):
    t, k = x.shape
    n = dy.shape[1]
    assert n % tn == 0, (name, n, tn)

    def body(x_ref, dy_ref, o_ref):
        o_ref[...] = lax.dot_general(x_ref[...].astype(BF16), dy_ref[...].astype(BF16), _TN,
                                     preferred_element_type=F32).astype(o_ref.dtype)

    if by_device:
        out_spec, out_shape = pl.BlockSpec((None, k, tn), lambda j: (j, 0, 0)), (n // tn, k, tn)
    else:
        out_spec, out_shape = pl.BlockSpec((k, tn), lambda j: (0, j)), (k, n)
    return _unpack(_pcall(body, name=name, grid=(n // tn,),
                          in_specs=[_whole(x), pl.BlockSpec((t, tn), lambda j: (0, j))],
                          out_specs=[out_spec], out_shape=[jax.ShapeDtypeStruct(out_shape, BF16)],
                          semantics=("parallel",), moves=moves)(x, dy), moves, 1)


def _dw_by_rows(x, dy, *, name, tk, square_relu=False, moves=()):
    t, k = x.shape
    n = dy.shape[1]
    assert k % tk == 0, (name, k, tk)

    def body(x_ref, dy_ref, o_ref):
        xv = _sq_relu(x_ref[...]) if square_relu else x_ref[...]
        o_ref[...] = lax.dot_general(xv.astype(BF16), dy_ref[...].astype(BF16), _TN,
                                     preferred_element_type=F32).astype(o_ref.dtype)

    return _unpack(_pcall(body, name=name, grid=(k // tk,),
                          in_specs=[pl.BlockSpec((t, tk), lambda j: (0, j)), _whole(dy)],
                          out_specs=[pl.BlockSpec((tk, n), lambda j: (j, 0))],
                          out_shape=[jax.ShapeDtypeStruct((k, n), BF16)],
                          semantics=("parallel",), moves=moves)(x, dy), moves, 1)


def _dx(dy, w, *, name, partial=None, moves=()):
    t, k = dy.shape
    n = w.shape[0]
    has_partial = partial is not None

    def body(dy_ref, w_ref, *rest):
        out = lax.dot_general(dy_ref[...].astype(BF16), w_ref[...], _NT, preferred_element_type=F32)
        if has_partial:
            out = out + rest[0][...]
        rest[-1][...] = out

    return _unpack(_pcall(body, name=name, grid=(t // TM,),
                          in_specs=[_rows(k), _whole(w)] + ([_rows(n)] if has_partial else []),
                          out_specs=[_rows(n)], out_shape=[jax.ShapeDtypeStruct((t, n), F32)],
                          semantics=("parallel",), moves=moves)(dy, w, *([partial] if has_partial else [])),
                   moves, 1)


def _dx_norm(dy, w, h, g, dres, *, name, partial=None, by_device_cols=False, moves=()):
    t, k = dy.shape
    d = h.shape[1]
    has_partial = partial is not None

    def body(dy_ref, w_ref, h_ref, g_ref, dres_ref, *rest):
        if by_device_cols:
            kc = k // N_DEV
            d_y = jnp.zeros((TM, d), F32)
            for j in range(N_DEV):
                d_y = d_y + lax.dot_general(dy_ref[:, j * kc:(j + 1) * kc].astype(BF16), w_ref[j], _NT,
                                            preferred_element_type=F32)
        else:
            d_y = lax.dot_general(dy_ref[...].astype(BF16), w_ref[...], _NT, preferred_element_type=F32)
        if has_partial:
            d_y = d_y + rest[0][...]
        dh_ref, dhb_ref, dg_ref, cs_ref = rest[-4:]
        _, vjp = jax.vjp(_rmsnorm, h_ref[...], g_ref[...])
        dh, dg = vjp(d_y)
        dh = dh + dres_ref[...]
        dh_ref[...] = dh
        dhb_ref[...] = dh.astype(BF16)

        @pl.when(pl.program_id(0) == 0)
        def _():
            dg_ref[...] = jnp.zeros_like(dg_ref)
            cs_ref[...] = jnp.zeros_like(cs_ref)

        dg_ref[...] += dg
        cs_ref[...] += jnp.sum(dh, axis=0, keepdims=True)

    shapes = [jax.ShapeDtypeStruct((t, d), F32), jax.ShapeDtypeStruct((t, d), BF16),
              jax.ShapeDtypeStruct((1, d), F32), jax.ShapeDtypeStruct((1, d), F32)]
    return _unpack(_pcall(body, name=name, grid=(t // TM,),
                          in_specs=[_rows(k), _whole(w), _rows(d), _acc_row(d), _rows(d)]
                          + ([_rows(d)] if has_partial else []),
                          out_specs=[_rows(d), _rows(d), _acc_row(d), _acc_row(d)], out_shape=shapes,
                          semantics=("arbitrary",), moves=moves)(dy, w, h, g, dres, *([partial] if has_partial else [])),
                   moves, 4)


def _pair_add(by_core, theirs, core, *, name, tb=512):
    n_chip, _, r, c = by_core.shape
    tb = min(tb, r)
    assert r % tb == 0, (name, r, tb)

    def body(core_ref, a_ref, b_ref, o_ref):
        del core_ref
        o_ref[...] = (a_ref[...].astype(F32) + b_ref[...].astype(F32)).astype(o_ref.dtype)

    blk = pl.BlockSpec((None, tb, c), lambda ch, i, core_ref: (ch, i, 0))
    return pl.pallas_call(
        body, name=name,
        grid_spec=pltpu.PrefetchScalarGridSpec(
            num_scalar_prefetch=1, grid=(n_chip, r // tb),
            in_specs=[pl.BlockSpec((None, None, tb, c), lambda ch, i, core_ref: (ch, core_ref[0], i, 0)), blk],
            out_specs=blk),
        out_shape=jax.ShapeDtypeStruct((n_chip, r, c), by_core.dtype),
        compiler_params=pltpu.CompilerParams(dimension_semantics=("parallel", "parallel"),
                                             vmem_limit_bytes=VMEM_LIMIT_BYTES),
    )(core, by_core, theirs)


def _colsum(a, *, name, tb=512):
    t, d = a.shape

    def body(a_ref, o_ref):
        @pl.when(pl.program_id(0) == 0)
        def _():
            o_ref[...] = jnp.zeros_like(o_ref)

        o_ref[...] += jnp.sum(a_ref[...].astype(F32), axis=0, keepdims=True)

    return _pcall(
        body, name=name, grid=(t // tb,),
        in_specs=[pl.BlockSpec((tb, d), lambda i: (i, 0))],
        out_specs=[pl.BlockSpec((1, d), lambda i: (0, 0))],
        out_shape=[jax.ShapeDtypeStruct((1, d), F32)],
        semantics=("arbitrary",),
    )(a)[0][0]


def _final_loss(h, g, target, *, name, tb=512):
    t, d = h.shape

    def body(h_ref, g_ref, tgt_ref, loss_ref, dh_ref, dhb_ref, dg_ref):
        def f(hh, gg):
            err = jnp.square(_rmsnorm(hh, gg) - tgt_ref[...])
            return 0.5 * jnp.sum(jnp.mean(err, axis=-1, keepdims=True), axis=0, keepdims=True)

        val, vjp = jax.vjp(f, h_ref[...], g_ref[...])
        dh, dg = vjp(jnp.ones((1, 1), F32))
        dh_ref[...] = dh
        dhb_ref[...] = dh.astype(BF16)

        @pl.when(pl.program_id(0) == 0)
        def _():
            loss_ref[...] = jnp.zeros_like(loss_ref)
            dg_ref[...] = jnp.zeros_like(dg_ref)

        loss_ref[...] += val
        dg_ref[...] += dg

    blk = pl.BlockSpec((tb, d), lambda i: (i, 0))
    row = pl.BlockSpec((1, d), lambda i: (0, 0))
    return _pcall(
        body, name=name, grid=(t // tb,),
        in_specs=[blk, row, blk],
        out_specs=[pl.BlockSpec((8, LANES), lambda i: (0, 0)), blk, blk, row],
        out_shape=[jax.ShapeDtypeStruct((8, LANES), F32), jax.ShapeDtypeStruct((t, d), F32),
                   jax.ShapeDtypeStruct((t, d), BF16), jax.ShapeDtypeStruct((1, d), F32)],
        semantics=("arbitrary",),
    )(h, g, target)[0]


def _gmlp_fwd(proj_uv, ln_g, ln_b, w_s, b_s, *, name, moves=()):
    t = proj_uv.shape[0]
    w = D_MODEL

    def body(u_ref, v_ref, g_ref, b_ref, w_ref, bs_ref, o_ref):
        o_ref[...] = _gmlp_chunk(u_ref[...], v_ref[...], g_ref[...], b_ref[...], w_ref[...],
                                 bs_ref[...]).astype(o_ref.dtype)

    row = pl.BlockSpec((1, w), lambda i: (0, 0))
    res, landed = _pcall(
        body, name=name, grid=(t // CHUNK,),
        in_specs=[pl.BlockSpec((CHUNK, w), lambda i: (i, 0)), pl.BlockSpec((CHUNK, w), lambda i: (i, 1)), row, row,
                  pl.BlockSpec((GM_GROUPS, CHUNK, CHUNK), lambda i: (0, 0, 0)),
                  pl.BlockSpec((GM_GROUPS, CHUNK, 1), lambda i: (0, 0, 0))],
        out_specs=[pl.BlockSpec((CHUNK, w), lambda i: (i, 0))],
        out_shape=[jax.ShapeDtypeStruct((t, 2 * w), BF16)],
        semantics=("parallel",), moves=moves,
    )(proj_uv, proj_uv, ln_g, ln_b, w_s, b_s)
    return (res[0], landed) if moves else res[0]


def _gmlp_bwd(proj_uv, d_mix, ln_g, ln_b, w_s, b_s, *, name, moves=()):
    t = proj_uv.shape[0]
    w = D_MODEL

    def body(u_ref, v_ref, da_ref, g_ref, b_ref, w_ref, bs_ref, duv_ref, dg_ref, db_ref, dw_ref, dbs_ref):
        _, vjp = jax.vjp(_gmlp_chunk, u_ref[...], v_ref[...], g_ref[...], b_ref[...], w_ref[...], bs_ref[...])
        du, dv, dg, db, dw, dbs = vjp(da_ref[...])
        duv_ref[:, :w] = du.astype(duv_ref.dtype)
        duv_ref[:, w:] = dv.astype(duv_ref.dtype)

        @pl.when(pl.program_id(0) == 0)
        def _():
            dg_ref[...] = jnp.zeros_like(dg_ref)
            db_ref[...] = jnp.zeros_like(db_ref)
            dw_ref[...] = jnp.zeros_like(dw_ref)
            dbs_ref[...] = jnp.zeros_like(dbs_ref)

        dg_ref[...] += dg
        db_ref[...] += db
        dw_ref[...] += dw
        dbs_ref[...] += dbs

    row = pl.BlockSpec((1, w), lambda i: (0, 0))
    ws = pl.BlockSpec((GM_GROUPS, CHUNK, CHUNK), lambda i: (0, 0, 0))
    bs = pl.BlockSpec((GM_GROUPS, CHUNK, 1), lambda i: (0, 0, 0))
    res, landed = _pcall(
        body, name=name, grid=(t // CHUNK,),
        in_specs=[pl.BlockSpec((CHUNK, w), lambda i: (i, 0)), pl.BlockSpec((CHUNK, w), lambda i: (i, 1)),
                  pl.BlockSpec((CHUNK, w), lambda i: (i, 0)), row, row, ws, bs],
        out_specs=[pl.BlockSpec((CHUNK, 2 * w), lambda i: (i, 0)), row, row, ws, bs],
        out_shape=[jax.ShapeDtypeStruct((t, 2 * w), BF16), jax.ShapeDtypeStruct((1, w), F32),
                   jax.ShapeDtypeStruct((1, w), F32), jax.ShapeDtypeStruct((GM_GROUPS, CHUNK, CHUNK), F32),
                   jax.ShapeDtypeStruct((GM_GROUPS, CHUNK, 1), F32)],
        semantics=("arbitrary",), moves=moves,
    )(proj_uv, proj_uv, d_mix, ln_g, ln_b, w_s, b_s)
    return (res, landed) if moves else res


_HALO_PER_CHUNK = CHUNK // HALO
_DT_BLOCK = (CONV_DIM + D_MODEL) // LANES


def _ssd_fwd(proj_rest, mix, conv_w, conv_b, dt_bias, a_log, d_skip, norm_g, *, name, moves=()):
    t = proj_rest.shape[0]
    nc = t // CHUNK

    def body(x_ref, prev_ref, z_ref, dt_ref, mix_ref, cw_ref, cb_ref, dtb_ref, al_ref, ds_ref, ng_ref, y_ref, hs_ref,
             h_scr):
        del mix_ref
        i = pl.program_id(0)

        @pl.when(i == 0)
        def _():
            h_scr[...] = jnp.zeros_like(h_scr)

        prev8 = jnp.where(i == 0, 0.0, prev_ref[...])
        pre = _conv_pre(prev8, x_ref[...], cw_ref[...], cb_ref[...])
        hs_ref[0] = h_scr[...]
        h_prev = tuple(h_scr[j] for j in range(_PAIRS))
        y, h_next = _ssd_chunk(pre, z_ref[...], dt_ref[...], h_prev, dtb_ref[...], al_ref[...], ds_ref[...],
                               ng_ref[...])
        y_ref[...] = y.astype(y_ref.dtype)
        for j in range(_PAIRS):
            h_scr[j] = h_next[j]

    small = pl.BlockSpec((1, LANES), lambda i: (0, 0))
    res, landed = _pcall(
        body, name=name, grid=(nc,),
        in_specs=[pl.BlockSpec((CHUNK, CONV_DIM), lambda i: (i, 0)),
                  pl.BlockSpec((HALO, CONV_DIM), lambda i: (jnp.maximum(i * _HALO_PER_CHUNK - 1, 0), 0)),
                  pl.BlockSpec((CHUNK, D_MODEL), lambda i: (i, CONV_DIM // D_MODEL)),
                  pl.BlockSpec((CHUNK, LANES), lambda i: (i, _DT_BLOCK)),
                  pl.BlockSpec(memory_space=pl.ANY),
                  pl.BlockSpec((SSM_CONV, CONV_DIM), lambda i: (0, 0)),
                  pl.BlockSpec((1, CONV_DIM), lambda i: (0, 0)),
                  small, small, small, pl.BlockSpec((1, D_MODEL), lambda i: (0, 0))],
        out_specs=[pl.BlockSpec((CHUNK, D_MODEL), lambda i: (i, 1)),
                   pl.BlockSpec((1, _PAIRS, SSM_STATE, LANES), lambda i: (i, 0, 0, 0))],
        out_shape=[jax.ShapeDtypeStruct((t, 2 * D_MODEL), BF16),
                   jax.ShapeDtypeStruct((nc, _PAIRS, SSM_STATE, LANES), F32)],
        scratch_shapes=[pltpu.VMEM((_PAIRS, SSM_STATE, LANES), F32)],
        semantics=("arbitrary",), moves=moves, aliases={4: 0},
    )(proj_rest, proj_rest, proj_rest, proj_rest, mix, conv_w, conv_b, dt_bias, a_log, d_skip, norm_g)
    return (res, landed) if moves else res


def _ssd_bwd(proj_rest, h_states, d_mix, conv_w, conv_b, dt_bias, a_log, d_skip, norm_g, *, name, moves=()):
    t = proj_rest.shape[0]
    nc = t // CHUNK

    def body(x_ref, prev_ref, z_ref, dt_ref, hs_ref, dy_ref, cw_ref, cb_ref, dtb_ref, al_ref, ds_ref, ng_ref,
             dpre_ref, dz_ref, ddt_ref, ddtb_ref, dal_ref, dds_ref, dng_ref, dh_scr):
        i = pl.program_id(0)
        chunk = nc - 1 - i

        @pl.when(i == 0)
        def _():
            dh_scr[...] = jnp.zeros_like(dh_scr)
            ddtb_ref[...] = jnp.zeros_like(ddtb_ref)
            dal_ref[...] = jnp.zeros_like(dal_ref)
            dds_ref[...] = jnp.zeros_like(dds_ref)
            dng_ref[...] = jnp.zeros_like(dng_ref)

        prev8 = jnp.where(chunk == 0, 0.0, prev_ref[...])
        pre = _conv_pre(prev8, x_ref[...], cw_ref[...], cb_ref[...])
        h_prev = tuple(hs_ref[0, j] for j in range(_PAIRS))
        _, vjp = jax.vjp(_ssd_chunk, pre, z_ref[...], dt_ref[...], h_prev, dtb_ref[...], al_ref[...],
                         ds_ref[...], ng_ref[...])
        dpre, dz, ddt, dh_prev, ddtb, dal, dds, dng = vjp((dy_ref[...], tuple(dh_scr[j] for j in range(_PAIRS))))
        dpre_ref[...] = dpre
        dz_ref[...] = dz
        ddt_ref[...] = ddt
        for j in range(_PAIRS):
            dh_scr[j] = dh_prev[j]
        ddtb_ref[...] += ddtb
        dal_ref[...] += dal
        dds_ref[...] += dds
        dng_ref[...] += dng

    rev = lambda i: nc - 1 - i
    small = pl.BlockSpec((1, LANES), lambda i: (0, 0))
    wide = pl.BlockSpec((1, D_MODEL), lambda i: (0, 0))
    res, landed = _pcall(
        body, name=name, grid=(nc,),
        in_specs=[pl.BlockSpec((CHUNK, CONV_DIM), lambda i: (rev(i), 0)),
                  pl.BlockSpec((HALO, CONV_DIM), lambda i: (jnp.maximum(rev(i) * _HALO_PER_CHUNK - 1, 0), 0)),
                  pl.BlockSpec((CHUNK, D_MODEL), lambda i: (rev(i), CONV_DIM // D_MODEL)),
                  pl.BlockSpec((CHUNK, LANES), lambda i: (rev(i), _DT_BLOCK)),
                  pl.BlockSpec((1, _PAIRS, SSM_STATE, LANES), lambda i: (rev(i), 0, 0, 0)),
                  pl.BlockSpec((CHUNK, D_MODEL), lambda i: (rev(i), 1)),
                  pl.BlockSpec((SSM_CONV, CONV_DIM), lambda i: (0, 0)),
                  pl.BlockSpec((1, CONV_DIM), lambda i: (0, 0)),
                  small, small, small, wide],
        out_specs=[pl.BlockSpec((CHUNK, CONV_DIM), lambda i: (rev(i), 0)),
                   pl.BlockSpec((CHUNK, D_MODEL), lambda i: (rev(i), 0)),
                   pl.BlockSpec((CHUNK, LANES), lambda i: (rev(i), 0)),
                   small, small, small, wide],
        out_shape=[jax.ShapeDtypeStruct((t, CONV_DIM), F32), jax.ShapeDtypeStruct((t, D_MODEL), F32),
                   jax.ShapeDtypeStruct((t, LANES), F32),
                   jax.ShapeDtypeStruct((1, LANES), F32), jax.ShapeDtypeStruct((1, LANES), F32),
                   jax.ShapeDtypeStruct((1, LANES), F32), jax.ShapeDtypeStruct((1, D_MODEL), F32)],
        scratch_shapes=[pltpu.VMEM((_PAIRS, SSM_STATE, LANES), F32)],
        semantics=("arbitrary",), moves=moves,
    )(proj_rest, proj_rest, proj_rest, proj_rest, h_states, d_mix, conv_w, conv_b, dt_bias, a_log, d_skip, norm_g)
    return (res, landed) if moves else res


def _conv_bwd(proj_rest, dpre, dz, ddt, conv_w, *, name, tb=256, moves=()):
    t = proj_rest.shape[0]
    nb = t // tb
    per = tb // HALO

    def body(x_ref, prev_ref, dpre_ref, next_ref, dz_ref, ddt_ref, cw_ref, drest_ref, dcw_ref, dcb_ref):
        i = pl.program_id(0)

        @pl.when(i == 0)
        def _():
            dcw_ref[...] = jnp.zeros_like(dcw_ref)
            dcb_ref[...] = jnp.zeros_like(dcb_ref)

        x = x_ref[...]
        dp = dpre_ref[...]
        w = cw_ref[...]
        prev8 = jnp.where(i == 0, 0.0, prev_ref[...])
        next8 = jnp.where(i == nb - 1, 0.0, next_ref[...])
        dx = dp * w[SSM_CONV - 1:SSM_CONV]
        for j in range(SSM_CONV - 1):
            dx = dx + _shift_up(dp, next8, SSM_CONV - 1 - j) * w[j:j + 1]
        drest_ref[:, :CONV_DIM] = dx.astype(drest_ref.dtype)
        drest_ref[:, CONV_DIM:CONV_DIM + D_MODEL] = dz_ref[...].astype(drest_ref.dtype)
        drest_ref[:, CONV_DIM + D_MODEL:] = ddt_ref[...].astype(drest_ref.dtype)
        for j in range(SSM_CONV):
            dcw_ref[j:j + 1, :] += jnp.sum(dp * _shift_down(prev8, x, SSM_CONV - 1 - j), axis=0, keepdims=True)
        dcb_ref[...] += jnp.sum(dp, axis=0, keepdims=True)

    res, landed = _pcall(
        body, name=name, grid=(nb,),
        in_specs=[pl.BlockSpec((tb, CONV_DIM), lambda i: (i, 0)),
                  pl.BlockSpec((HALO, CONV_DIM), lambda i: (jnp.maximum(i * per - 1, 0), 0)),
                  pl.BlockSpec((tb, CONV_DIM), lambda i: (i, 0)),
                  pl.BlockSpec((HALO, CONV_DIM), lambda i: (jnp.minimum((i + 1) * per, nb * per - 1), 0)),
                  pl.BlockSpec((tb, D_MODEL), lambda i: (i, 0)),
                  pl.BlockSpec((tb, LANES), lambda i: (i, 0)),
                  pl.BlockSpec((SSM_CONV, CONV_DIM), lambda i: (0, 0))],
        out_specs=[pl.BlockSpec((tb, REST_W), lambda i: (i, 0)),
                   pl.BlockSpec((SSM_CONV, CONV_DIM), lambda i: (0, 0)),
                   pl.BlockSpec((1, CONV_DIM), lambda i: (0, 0))],
        out_shape=[jax.ShapeDtypeStruct((t, REST_W), BF16), jax.ShapeDtypeStruct((SSM_CONV, CONV_DIM), F32),
                   jax.ShapeDtypeStruct((1, CONV_DIM), F32)],
        semantics=("arbitrary",), moves=moves,
    )(proj_rest, proj_rest, dpre, dpre, dz, ddt, conv_w)
    return (res, landed) if moves else res


_Q_PER_KV = ATTN_HEADS // ATTN_KV


def _attn_fwd(q, k, v, sink_col, *, name):
    t = q.shape[1]
    nb = t // CHUNK

    def body(q_ref, kp_ref, kc_ref, vp_ref, vc_ref, s_ref, o_ref):
        first = pl.program_id(0) == 0
        for j in range(ATTN_KV):
            qj = q_ref[j * _Q_PER_KV:(j + 1) * _Q_PER_KV].reshape(_Q_PER_KV * CHUNK, HEAD_DIM)
            o = _attn_block(qj, kp_ref[j], kc_ref[j], vp_ref[j], vc_ref[j], s_ref[j], first)
            o_ref[j * _Q_PER_KV:(j + 1) * _Q_PER_KV] = o.reshape(_Q_PER_KV, CHUNK, HEAD_DIM).astype(o_ref.dtype)

    cur = lambda i: (0, i, 0)
    prev = lambda i: (0, jnp.maximum(i - 1, 0), 0)
    kv = (ATTN_KV, CHUNK, HEAD_DIM)
    return _pcall(
        body, name=name, grid=(nb,),
        in_specs=[pl.BlockSpec((ATTN_HEADS, CHUNK, HEAD_DIM), cur), pl.BlockSpec(kv, prev), pl.BlockSpec(kv, cur),
                  pl.BlockSpec(kv, prev), pl.BlockSpec(kv, cur),
                  pl.BlockSpec((ATTN_KV, _Q_PER_KV * CHUNK, 1), lambda i: (0, 0, 0))],
        out_specs=[pl.BlockSpec((ATTN_HEADS, CHUNK, HEAD_DIM), cur)],
        out_shape=[jax.ShapeDtypeStruct((ATTN_HEADS, t, HEAD_DIM), BF16)],
        semantics=("parallel",),
    )(q, k, k, v, v, sink_col)[0][0]


def _attn_bwd(q, k, v, sink_col, d_o, *, name, moves=()):
    t = q.shape[1]
    nb = t // CHUNK

    def body(q_ref, kp_ref, kc_ref, vp_ref, vc_ref, s_ref, do_ref, dq_ref, dk_ref, dv_ref, ds_ref, dk_scr, dv_scr):
        i = pl.program_id(0)
        first = i == nb - 1

        @pl.when(i == 0)
        def _():
            dk_scr[...] = jnp.zeros_like(dk_scr)
            dv_scr[...] = jnp.zeros_like(dv_scr)
            ds_ref[...] = jnp.zeros_like(ds_ref)

        for j in range(ATTN_KV):
            heads = slice(j * _Q_PER_KV, (j + 1) * _Q_PER_KV)
            qj = q_ref[heads].reshape(_Q_PER_KV * CHUNK, HEAD_DIM)
            doj = do_ref[heads].reshape(_Q_PER_KV * CHUNK, HEAD_DIM)
            _, vjp = jax.vjp(functools.partial(_attn_block, first=first), qj, kp_ref[j], kc_ref[j], vp_ref[j],
                             vc_ref[j], s_ref[j])
            dq, dkp, dkc, dvp, dvc, dsink = vjp(doj)
            dq_ref[heads] = dq.reshape(_Q_PER_KV, CHUNK, HEAD_DIM)
            dk_ref[j] = dkc + dk_scr[j]
            dv_ref[j] = dvc + dv_scr[j]
            dk_scr[j] = dkp
            dv_scr[j] = dvp
            ds_ref[j] += dsink

    cur = lambda i: (0, nb - 1 - i, 0)
    prev = lambda i: (0, jnp.maximum(nb - 2 - i, 0), 0)
    kv = (ATTN_KV, CHUNK, HEAD_DIM)
    qs = (ATTN_HEADS, CHUNK, HEAD_DIM)
    sk = pl.BlockSpec((ATTN_KV, _Q_PER_KV * CHUNK, 1), lambda i: (0, 0, 0))
    res, landed = _pcall(
        body, name=name, grid=(nb,),
        in_specs=[pl.BlockSpec(qs, cur), pl.BlockSpec(kv, prev), pl.BlockSpec(kv, cur), pl.BlockSpec(kv, prev),
                  pl.BlockSpec(kv, cur), sk, pl.BlockSpec(qs, cur)],
        out_specs=[pl.BlockSpec(qs, cur), pl.BlockSpec(kv, cur), pl.BlockSpec(kv, cur), sk],
        out_shape=[jax.ShapeDtypeStruct((ATTN_HEADS, t, HEAD_DIM), F32), jax.ShapeDtypeStruct((ATTN_KV, t, HEAD_DIM), F32),
                   jax.ShapeDtypeStruct((ATTN_KV, t, HEAD_DIM), F32),
                   jax.ShapeDtypeStruct((ATTN_KV, _Q_PER_KV * CHUNK, 1), F32)],
        scratch_shapes=[pltpu.VMEM(kv, F32), pltpu.VMEM(kv, F32)],
        semantics=("arbitrary",), moves=moves,
    )(q, k, k, v, v, sink_col, d_o)
    return (res, landed) if moves else res


def _adamw(parts, w, m, v, *, name, tb=256, moves=()):
    layers, r, c = w.shape
    n = parts[0].shape[0]
    tb = min(tb, r)
    assert r % tb == 0 and len(parts) == layers, (name, r, tb)
    nb = r // tb

    def body(*refs):
        p_refs = refs[:layers]
        w_ref, m_ref, v_ref, g_ref, d_ref, nm_ref, nv_ref = refs[layers:]
        for layer in range(layers):
            @pl.when(pl.program_id(0) == layer)
            def _(p_ref=p_refs[layer]):
                g = p_ref[0].astype(F32)
                for s in range(1, n):
                    g = g + p_ref[s].astype(F32)
                m_new = ADAM_B1 * m_ref[...] + (1.0 - ADAM_B1) * g
                v_new = ADAM_B2 * v_ref[...] + (1.0 - ADAM_B2) * jnp.square(g)
                m_hat = m_new / (1.0 - ADAM_B1 ** ADAM_STEP)
                v_hat = v_new / (1.0 - ADAM_B2 ** ADAM_STEP)
                g_ref[...] = g
                d_ref[...] = -ADAM_LR * (m_hat / (jnp.sqrt(v_hat) + ADAM_EPS) + ADAM_WD * w_ref[...])
                nm_ref[...] = m_new
                nv_ref[...] = v_new

    part_spec = lambda layer: pl.BlockSpec(
        (n, tb, c), lambda l, i: (0, jnp.clip(i + (l - layer) * nb, 0, nb - 1), 0))
    blk = pl.BlockSpec((None, tb, c), lambda l, i: (l, i, 0))
    res, landed = _pcall(
        body, name=name, grid=(layers, nb),
        in_specs=[part_spec(layer) for layer in range(layers)] + [blk, blk, blk],
        out_specs=[blk] * 4,
        out_shape=[jax.ShapeDtypeStruct((layers, r, c), F32)] * 4,
        semantics=("arbitrary", "arbitrary"), moves=moves,
    )(*parts, w, m, v)
    return (res, landed) if moves else res


def _as_rows(a):
    flat = a.reshape(-1)
    pad = (-flat.shape[0]) % PACK_W
    if pad:
        flat = jnp.pad(flat, (0, pad))
    return flat.reshape(-1, PACK_W)


def _cols_from_shards(g):
    return jnp.transpose(g, (1, 0, 2)).reshape(g.shape[1], -1)


def _cols_to_shards(a):
    return jnp.transpose(a.reshape(a.shape[0], N_DEV, -1), (1, 0, 2))


def _pad_lanes(a):
    return jnp.pad(a, ((0, 0), (0, LANES - a.shape[1])))


def kernel(x, norm_mix_g, norm_mlp_g, final_norm_g, w_in_even, w_out_even, gm_ln_g, gm_ln_b, gm_w_s, gm_b_s, ssm_conv_w, ssm_conv_b, ssm_dt_bias, ssm_a_log, ssm_d, ssm_norm_g, w_qkv, b_qkv, w_o, b_o, attn_sinks, w_up, w_down, loss_target, m_norm_mix_g, m_norm_mlp_g, m_final_norm_g, m_w_in_even, m_w_out_even, m_gm_ln_g, m_gm_ln_b, m_gm_w_s, m_gm_b_s, m_ssm_conv_w, m_ssm_conv_b, m_ssm_dt_bias, m_ssm_a_log, m_ssm_d, m_ssm_norm_g, m_w_qkv, m_b_qkv, m_w_o, m_b_o, m_attn_sinks, m_w_up, m_w_down, v_norm_mix_g, v_norm_mlp_g, v_final_norm_g, v_w_in_even, v_w_out_even, v_gm_ln_g, v_gm_ln_b, v_gm_w_s, v_gm_b_s, v_ssm_conv_w, v_ssm_conv_b, v_ssm_dt_bias, v_ssm_a_log, v_ssm_d, v_ssm_norm_g, v_w_qkv, v_b_qkv, v_w_o, v_b_o, v_attn_sinks, v_w_up, v_w_down):
    names = ["norm_mix_g", "norm_mlp_g", "final_norm_g", "w_in_even", "w_out_even", "gm_ln_g", "gm_ln_b", "gm_w_s",
             "gm_b_s", "ssm_conv_w", "ssm_conv_b", "ssm_dt_bias", "ssm_a_log", "ssm_d", "ssm_norm_g", "w_qkv",
             "b_qkv", "w_o", "b_o", "attn_sinks", "w_up", "w_down"]
    env = locals()
    W = {n: env[n] for n in names}
    M = {n: env["m_" + n] for n in names}
    V = {n: env["v_" + n] for n in names}
    big = ["w_in_even", "w_out_even", "w_qkv", "w_o", "w_up", "w_down"]
    small_sharded = ["ssm_conv_w", "b_qkv", "b_o"]
    replicated = [n for n in names if n not in big and n not in small_sharded]
    me = 4 * lax.axis_index("x") + 2 * lax.axis_index("y") + lax.axis_index("c")
    t = x.shape[1]
    xs = x.reshape(t, D_MODEL)
    target = loss_target.reshape(t, D_MODEL)
    gather = lambda a: _Move("gather", a)
    over_ici = lambda a: _Move("gather_ici", a)
    over_d2d = lambda a: _Move("gather_d2d", a)
    by_core = lambda a: a.reshape((N_CHIP, N_CORE) + a.shape[1:])
    to_sibling = lambda a: [_Move("scatter_d2d", by_core(a))]
    my_core = lax.axis_index("c").astype(jnp.int32).reshape(1)
    pair = lambda a, theirs, name: _pair_add(by_core(a), theirs, my_core, name=name)
    to_chips = lambda a: _Move("scatter_ici", a)
    whole = lambda a: a.reshape((N_DEV,) + a.shape[2:])
    row = lambda a: a.reshape(1, D_MODEL)

    small_flat = jnp.concatenate([W[n].reshape(-1) for n in small_sharded])
    w_in_g, small_g = _exchange([over_ici(w_in_even[0].astype(BF16)), gather(_as_rows(small_flat))],
                                name="gather_w_in", then_d2d=[0])
    w_in = _cols_from_shards(whole(w_in_g))
    w_uv = w_in[:, :2 * D_MODEL]
    w_rest = jnp.concatenate([w_in[:, 3 * D_MODEL:3 * D_MODEL + CONV_DIM], w_in[:, 2 * D_MODEL:3 * D_MODEL],
                              w_in[:, 3 * D_MODEL + CONV_DIM:],
                              jnp.zeros((D_MODEL, LANES - SSM_HEADS), BF16)], axis=1)
    small_all = small_g.reshape(N_DEV, -1)
    n_cw = SSM_CONV * CONV_DIM // N_DEV
    n_bq = QKV_DIM // N_DEV
    conv_w = _cols_from_shards(small_all[:, :n_cw].reshape(N_DEV, SSM_CONV, CONV_DIM // N_DEV))
    bqkv = small_all[:, n_cw:n_cw + n_bq].reshape(1, QKV_DIM)
    bo = small_all[:, n_cw + n_bq:n_cw + n_bq + D_MODEL // N_DEV].reshape(1, D_MODEL)

    conv_b = ssm_conv_b.reshape(1, CONV_DIM)
    dt_bias, a_log, d_skip = _pad_lanes(ssm_dt_bias), _pad_lanes(ssm_a_log), _pad_lanes(ssm_d)
    gm_w = gm_w_s[0]
    gm_b = gm_b_s[0].reshape(GM_GROUPS, CHUNK, 1)
    sink_col = jnp.repeat(attn_sinks.reshape(ATTN_HEADS), CHUNK).reshape(ATTN_KV, _Q_PER_KV * CHUNK, 1)
    w_up_b, w_down_b = w_up.astype(BF16), w_down.astype(BF16)

    (y0, proj_uv), (w_out_g,) = _norm_matmul(xs, row(norm_mix_g[0]), w_uv, name="proj_uv", emit_y=True,
                                             moves=[over_ici(w_out_even[0].astype(BF16))])
    proj_rest, (w_up0_g,) = _norm_matmul(xs, row(norm_mix_g[0]), w_rest, name="proj_rest", emit_y=False,
                                         moves=[over_ici(w_up_b[0])])
    mix, (w_qkv_g, w_o_g, w_out_g, w_up0_g) = _gmlp_fwd(
        proj_uv, gm_ln_g, gm_ln_b, gm_w, gm_b, name="gmlp_fwd",
        moves=[over_ici(w_qkv[0].astype(BF16)), over_ici(w_o[0].astype(BF16)), over_d2d(w_out_g), over_d2d(w_up0_g)])
    (mix, h_states), (w_down0_g,) = _ssd_fwd(
        proj_rest, mix, conv_w, conv_b, dt_bias, a_log, d_skip, ssm_norm_g, name="ssd_fwd",
        moves=[over_ici(w_down_b[0])])
    w_out_f = whole(w_out_g).reshape(2 * D_MODEL, D_MODEL)
    (h1, y1), (w_down0_g, w_qkv_g, w_o_g) = _residual_matmul(
        mix, w_out_f, xs, name="mix_out", norm_g=row(norm_mlp_g[0]),
        moves=[over_d2d(w_down0_g), over_d2d(w_qkv_g), over_d2d(w_o_g)])
    up0, (w_up1_g,) = _mlp_up(y1, whole(w_up0_g), name="mlp_up0", moves=[over_ici(w_up_b[1])])
    (h2, y2), (w_down1_g, w_up1_g) = _mlp_down(up0, whole(w_down0_g), h1, name="mlp_down0", norm_g=row(norm_mix_g[1]),
                                               moves=[over_ici(w_down_b[1]), over_d2d(w_up1_g)])
    wqkv = _cols_from_shards(whole(w_qkv_g))
    wo = whole(w_o_g).reshape(D_MODEL, D_MODEL)
    qkv, (w_down1_g,) = _residual_matmul(y2, wqkv, None, name="qkv", bias=bqkv, moves=[over_d2d(w_down1_g)])
    w_up_g = [whole(w_up0_g), whole(w_up1_g)]
    w_down_g = [whole(w_down0_g), whole(w_down1_g)]
    heads = lambda a, n: jnp.transpose(a.reshape(t, n, HEAD_DIM), (1, 0, 2))
    q = heads(qkv[:, :D_MODEL], ATTN_HEADS)
    k = heads(qkv[:, D_MODEL:D_MODEL + ATTN_KV * HEAD_DIM], ATTN_KV)
    v = heads(qkv[:, D_MODEL + ATTN_KV * HEAD_DIM:], ATTN_KV)
    attn = _attn_fwd(q, k, v, sink_col, name="attn_fwd")
    attn = jnp.transpose(attn, (1, 0, 2)).reshape(t, D_MODEL)
    h3, y3 = _residual_matmul(attn, wo, h2, name="attn_out", bias=bo, norm_g=row(norm_mlp_g[1]))
    up1 = _mlp_up(y3, w_up_g[1], name="mlp_up1")
    h4 = _mlp_down(up1, w_down_g[1], h3, name="mlp_down1")
    loss_part, dh4, dh4_b, d_final_g = _final_loss(h4, row(final_norm_g), target, name="final_loss")

    by_dev_rows = lambda a: a.reshape((N_DEV, a.shape[0] // N_DEV) + a.shape[1:])

    def mlp_bwd(dh, dh_b, h, y, up, layer, first_moves=()):
        res = _mlp_down_dx(dh_b, w_down_g[layer], up, name=f"mlp_down_dx{layer}", moves=first_moves)
        d_up, first_landed = res if first_moves else (res, [])
        g_down = _dw_by_rows(up, dh_b, name=f"mlp_down_dw{layer}", tk=FF_SHARD, square_relu=True)
        g_down = by_dev_rows(g_down)
        g_up, (theirs,) = _dw_by_cols(y, d_up, name=f"mlp_up_dw{layer}", tn=FF_SHARD, by_device=True,
                                      moves=to_sibling(g_down))
        q_down = pair(g_down, theirs, f"mlp_down_pair{layer}")
        (dh_new, dh_new_b, dg, cs), (r_down, theirs) = _dx_norm(
            d_up, w_up_g[layer], h, row(norm_mlp_g[layer]), dh, name=f"mlp_up_dx{layer}", by_device_cols=True,
            moves=[to_chips(q_down)] + to_sibling(g_up))
        q_up = pair(g_up, theirs, f"mlp_up_pair{layer}")
        return dh_new, dh_new_b, cs, dg, q_up, r_down, first_landed

    dh3, dh3_b, cs3, g_nmlp1, q_up1, r_down1, _ = mlp_bwd(dh4, dh4_b, h3, y3, up1, 1)
    g_bo = cs3
    g_wo = by_dev_rows(_dw_by_cols(attn, dh3_b, name="attn_out_dw", tn=FF_SHARD))
    d_attn, (theirs,) = _dx(dh3_b, wo, name="attn_out_dx", moves=to_sibling(g_wo))
    q_wo = pair(g_wo, theirs, "attn_out_pair")
    d_o = jnp.transpose(d_attn.reshape(t, ATTN_HEADS, HEAD_DIM), (1, 0, 2))
    (dq, dk, dv, d_sink), (r_up1, r_wo) = _attn_bwd(q, k, v, sink_col, d_o, name="attn_bwd",
                                                    moves=[to_chips(q_up1), to_chips(q_wo)])
    unheads = lambda a: jnp.transpose(a, (1, 0, 2)).reshape(t, -1)
    dqkv = jnp.concatenate([unheads(dq), unheads(dk), unheads(dv)], axis=1)
    g_bqkv = _colsum(dqkv, name="qkv_db")
    g_wqkv = _cols_to_shards(_dw_by_cols(y2, dqkv, name="qkv_dw", tn=QKV_DIM // 2))
    (dh2, dh2_b, g_nmix1, _), (theirs,) = _dx_norm(dqkv, wqkv, h2, row(norm_mix_g[1]), dh3, name="qkv_dx",
                                                   moves=to_sibling(g_wqkv))
    q_wqkv = pair(g_wqkv, theirs, "qkv_pair")
    dh1, dh1_b, _, g_nmlp0, q_up0, r_down0, (r_wqkv,) = mlp_bwd(dh2, dh2_b, h1, y1, up0, 0,
                                                                first_moves=[to_chips(q_wqkv)])

    d_mix = _dx(dh1_b, w_out_f, name="mix_out_dx")
    g_wout = by_dev_rows(_dw_by_rows(mix, dh1_b, name="mix_out_dw", tk=FF_SHARD))
    (d_uv, g_ln_g, g_ln_b, g_gm_w, g_gm_b), (r_up0, theirs) = _gmlp_bwd(
        proj_uv, d_mix, gm_ln_g, gm_ln_b, gm_w, gm_b, name="gmlp_bwd", moves=[to_chips(q_up0)] + to_sibling(g_wout))
    q_wout = pair(g_wout, theirs, "mix_out_pair")

    early = [("norm_mlp_g", None), ("final_norm_g", None), ("norm_mix_g", 1), ("gm_ln_g", None), ("gm_ln_b", None),
             ("gm_w_s", None), ("gm_b_s", None), ("attn_sinks", None)]
    late = [("norm_mix_g", 0), ("ssm_conv_b", None), ("ssm_dt_bias", None), ("ssm_a_log", None), ("ssm_d", None),
            ("ssm_norm_g", None)]
    early_sharded, late_sharded = ["b_qkv", "b_o"], ["ssm_conv_w"]
    small_grads = {
        ("norm_mlp_g", None): jnp.concatenate([g_nmlp0, g_nmlp1], axis=0),
        ("final_norm_g", None): d_final_g, ("norm_mix_g", 1): g_nmix1,
        ("gm_ln_g", None): g_ln_g, ("gm_ln_b", None): g_ln_b, ("gm_w_s", None): g_gm_w, ("gm_b_s", None): g_gm_b,
        ("attn_sinks", None): jnp.sum(d_sink.reshape(ATTN_HEADS, CHUNK), axis=1),
        "b_qkv": g_bqkv, "b_o": g_bo,
    }
    pack = lambda keys: _as_rows(jnp.concatenate([small_grads[key].reshape(-1) for key in keys]))
    (dpre, dz, ddt, g_dtb, g_alog, g_dskip, g_ssm_ng), (r_wout, early_recv) = _ssd_bwd(
        proj_rest, h_states, d_mix, conv_w, conv_b, dt_bias, a_log, d_skip, ssm_norm_g, name="ssd_bwd",
        moves=[to_chips(q_wout), gather(pack(early + early_sharded))])
    d_rest, g_conv_w, g_conv_b = _conv_bwd(proj_rest, dpre, dz, ddt, conv_w, name="conv_bwd")
    g_w_uv = _dw_by_cols(y0, d_uv, name="proj_uv_dw", tn=FF_SHARD)
    g_w_rest = _dw_by_cols(y0, d_rest, name="proj_rest_dw", tn=REST_W // 5)
    g_w_in = jnp.concatenate([g_w_uv, g_w_rest[:, CONV_DIM:CONV_DIM + D_MODEL], g_w_rest[:, :CONV_DIM],
                              g_w_rest[:, CONV_DIM + D_MODEL:CONV_DIM + D_MODEL + SSM_HEADS]], axis=1)
    g_w_in = _cols_to_shards(g_w_in)
    dy0, (theirs,) = _dx(d_uv, w_uv, name="proj_uv_dx", moves=to_sibling(g_w_in))
    q_w_in = pair(g_w_in, theirs, "proj_pair")
    (dx, _, g_nmix0, _), (r_w_in,) = _dx_norm(d_rest, w_rest, xs, row(norm_mix_g[0]), dh1, name="proj_rest_dx",
                                              partial=dy0, moves=[to_chips(q_w_in)])
    small_grads.update({
        ("norm_mix_g", 0): g_nmix0, ("ssm_conv_b", None): g_conv_b,
        ("ssm_dt_bias", None): g_dtb[:, :SSM_HEADS], ("ssm_a_log", None): g_alog[:, :SSM_HEADS],
        ("ssm_d", None): g_dskip[:, :SSM_HEADS], ("ssm_norm_g", None): g_ssm_ng, "ssm_conv_w": g_conv_w,
    })


    def update(n, parts, moves=()):
        shape = W[n].shape
        as3 = lambda a: a.reshape((len(parts),) + parts[0].shape[1:])
        res = _adamw(parts, as3(W[n]), as3(M[n]), as3(V[n]), name="adamw_" + n, moves=moves)
        res, landed = res if moves else (res, [])
        return [a.reshape(shape) for a in res], landed

    out = {}
    out["w_o"], (late_recv,) = update("w_o", [r_wo], moves=[gather(pack(late + late_sharded))])
    out["w_down"], _ = update("w_down", [r_down0, r_down1])
    out["w_up"], _ = update("w_up", [r_up0, r_up1])
    out["w_out_even"], _ = update("w_out_even", [r_wout])
    out["w_qkv"], _ = update("w_qkv", [r_wqkv])
    out["w_in_even"], _ = update("w_in_even", [r_w_in])

    def unpacked(recv, keys):
        flat, res, o = recv.reshape(N_DEV, -1), {}, 0
        for key in keys:
            res[key] = flat[:, o:o + small_grads[key].size]
            o += small_grads[key].size
        return res

    arrived = {**unpacked(early_recv, early + early_sharded), **unpacked(late_recv, late + late_sharded)}
    piece = lambda tree, key: tree[key[0]] if key[1] is None else tree[key[0]][key[1]]

    def rows_by_device(cat):
        pad = (-cat.shape[1]) % PACK_W
        return jnp.pad(cat, ((0, 0), (0, pad))).reshape(N_DEV, -1, PACK_W)

    rep_keys = early + late
    rep_parts = rows_by_device(jnp.concatenate([arrived[key] for key in rep_keys], axis=1))
    flat_rep = lambda tree: _as_rows(jnp.concatenate([piece(tree, key).reshape(-1) for key in rep_keys]))[None]
    rep_res = _adamw([rep_parts], flat_rep(W), flat_rep(M), flat_rep(V), name="adamw_replicated")
    sh_keys = early_sharded + late_sharded
    shard_parts = []
    for n in sh_keys:
        full = arrived[n].reshape((N_DEV,) + small_grads[n].shape)
        c = full.shape[-1] // N_DEV
        shard_parts.append(lax.dynamic_slice_in_dim(full, me * c, c, axis=full.ndim - 1).reshape(N_DEV, -1))
    sh_rows = rows_by_device(jnp.concatenate(shard_parts, axis=1))
    flat_sh = lambda tree: _as_rows(jnp.concatenate([tree[n].reshape(-1) for n in sh_keys]))[None]
    sh_res = _adamw([sh_rows], flat_sh(W), flat_sh(M), flat_sh(V), name="adamw_small_sharded")

    def unpack_replicated(rows):
        flat, vals, o = rows.reshape(-1), {}, 0
        for key in rep_keys:
            size = piece(W, key).size
            vals[key] = flat[o:o + size]
            o += size
        res = {}
        for n in replicated:
            if (n, None) in vals:
                res[n] = vals[(n, None)].reshape(W[n].shape)
            else:
                res[n] = jnp.stack([vals[(n, r)] for r in range(W[n].shape[0])]).reshape(W[n].shape)
        return res

    def unpack_sharded(rows):
        flat, res, o = rows.reshape(-1), {}, 0
        for n in sh_keys:
            res[n] = flat[o:o + W[n].size].reshape(W[n].shape)
            o += W[n].size
        return res

    results = []
    for idx in range(4):
        d = {n: out[n][idx] for n in big}
        d.update(unpack_replicated(rep_res[idx]))
        d.update(unpack_sharded(sh_res[idx]))
        results.append(d)

    loss = lax.psum(loss_part[0, 0], ("x", "y", "c"))
    grad_x = dx.reshape(x.shape)
    final = [loss, grad_x]
    for d in results:
        final.extend(d[n] for n in names)
    return tuple(final)
```

```python
import dataclasses
import functools

import jax
import jax.numpy as jnp
from jax import lax
from jax.experimental import pallas as pl
from jax.experimental.pallas import tpu as pltpu

F32 = jnp.float32
BF16 = jnp.bfloat16

N_DEV = 8
D_MODEL = 1024
D_FF = 4096
RMS_EPS = 1e-5
LN_EPS = 1e-5
CHUNK = 128
GM_GROUPS = 8
SSM_HEADS = 16
SSM_HEADDIM = 64
SSM_GROUPS = 4
SSM_STATE = 128
SSM_CONV = 4
CONV_DIM = 2048
IN_EVEN = 5136
REST_W = 3200
ATTN_HEADS = 16
ATTN_KV = 2
HEAD_DIM = 64
QKV_DIM = 1280
LANES = 128
HALO = 8
PACK_W = 1024

ADAM_LR = 0.001
ADAM_B1 = 0.9
ADAM_B2 = 0.999
ADAM_EPS = 1e-08
ADAM_WD = 0.01
ADAM_STEP = 10

VMEM_LIMIT_BYTES = 56 * 1024 * 1024


_NN = (((1,), (0,)), ((), ()))
_NT = (((1,), (1,)), ((), ()))
_TN = (((0,), (0,)), ((), ()))


def _dg(a, b, dims):
    return lax.dot_general(a.astype(BF16), b.astype(BF16), dims, preferred_element_type=F32)


@jax.custom_vjp
def _nn(a, b):
    return _dg(a, b, _NN)


@jax.custom_vjp
def _nt(a, b):
    return _dg(a, b, _NT)


@jax.custom_vjp
def _tn(a, b):
    return _dg(a, b, _TN)


_nn.defvjp(lambda a, b: (_dg(a, b, _NN), (a, b)), lambda r, g: (_nt(g, r[1]), _tn(r[0], g)))
_nt.defvjp(lambda a, b: (_dg(a, b, _NT), (a, b)), lambda r, g: (_nn(g, r[1]), _tn(g, r[0])))
_tn.defvjp(lambda a, b: (_dg(a, b, _TN), (a, b)), lambda r, g: (_nt(r[1], g), _nn(r[0], g)))


def _split3_dot(tri, x):
    x1 = x.astype(BF16)
    r1 = x - x1.astype(F32)
    x2 = r1.astype(BF16)
    x3 = (r1 - x2.astype(F32)).astype(BF16)
    t = tri.astype(BF16)
    dot = lambda p: lax.dot_general(t, p, _NN, preferred_element_type=F32)
    return dot(x1) + dot(x2) + dot(x3)


def _tri(lower):
    r = lax.broadcasted_iota(jnp.int32, (CHUNK, CHUNK), 0)
    c = lax.broadcasted_iota(jnp.int32, (CHUNK, CHUNK), 1)
    return jnp.where((r >= c) if lower else (r <= c), 1.0, 0.0).astype(F32)


@jax.custom_vjp
def _cumsum_rows(x):
    return _split3_dot(_tri(True), x)


_cumsum_rows.defvjp(lambda x: (_split3_dot(_tri(True), x), None), lambda _, g: (_split3_dot(_tri(False), g),))


def _sigmoid(x):
    return 1.0 / (1.0 + jnp.exp(-x))


def _silu(x):
    return x * _sigmoid(x)


def _softplus(x):
    return jnp.maximum(x, 0.0) + jnp.log(1.0 + jnp.exp(-jnp.abs(x)))


def _gelu_tanh(x):
    return 0.5 * x * (1.0 + jnp.tanh(0.7978845608028654 * (x + 0.044715 * (x * x * x))))


def _rmsnorm(x, g):
    return x * lax.rsqrt(jnp.mean(x * x, axis=-1, keepdims=True) + RMS_EPS) * g


def _gmlp_chunk(u, v, ln_g, ln_b, w_s, b_s):
    gu = _gelu_tanh(u)
    gv = _gelu_tanh(v)
    mu = jnp.mean(gv, axis=-1, keepdims=True)
    var = jnp.mean(jnp.square(gv - mu), axis=-1, keepdims=True)
    vn = (gv - mu) * lax.rsqrt(var + LN_EPS) * ln_g + ln_b
    r = lax.broadcasted_iota(jnp.int32, (CHUNK, CHUNK), 0)
    c = lax.broadcasted_iota(jnp.int32, (CHUNK, CHUNK), 1)
    causal = r >= c
    outs = []
    for g in range(GM_GROUPS):
        cols = slice(g * LANES, (g + 1) * LANES)
        mixed = _nn(jnp.where(causal, w_s[g], 0.0), vn[:, cols]) + b_s[g]
        outs.append(gu[:, cols] * mixed)
    return jnp.concatenate(outs, axis=1)


def _lane_pick(row, h):
    lane = lax.broadcasted_iota(jnp.int32, row.shape, 1)
    return jnp.sum(jnp.where(lane == h, row, 0.0), axis=1, keepdims=True)


def _col_pick(m, h):
    lane = lax.broadcasted_iota(jnp.int32, m.shape, 1)
    return jnp.sum(jnp.where(lane == h, m, 0.0), axis=1, keepdims=True)


def _row_pick(m, h):
    sub = lax.broadcasted_iota(jnp.int32, m.shape, 0)
    return jnp.sum(jnp.where(sub == h, m, 0.0), axis=0, keepdims=True)


_PAIRS = SSM_HEADS // 2


def _ssd_chunk(pre, z, dt_raw, h_prev, dt_bias, a_log, d_skip, norm_g):
    xbc = _silu(pre)
    dt = _softplus(dt_raw + dt_bias)
    da = dt * (-jnp.exp(a_log))
    a_cum = _cumsum_rows(da)
    a_cum_t = a_cum.T
    dt_t = dt.T
    r = lax.broadcasted_iota(jnp.int32, (CHUNK, CHUNK), 0)
    c = lax.broadcasted_iota(jnp.int32, (CHUNK, CHUNK), 1)
    causal = r >= c
    lane_lo = lax.broadcasted_iota(jnp.int32, (1, LANES), 1) < SSM_HEADDIM
    last_row = lax.broadcasted_iota(jnp.int32, (CHUNK, 1), 0) == CHUNK - 1
    ys, h_next = [], []
    for j in range(_PAIRS):
        g = j // 2
        xs = xbc[:, j * LANES:(j + 1) * LANES]
        bm = xbc[:, 1024 + g * SSM_STATE:1024 + (g + 1) * SSM_STATE]
        cm = xbc[:, 1536 + g * SSM_STATE:1536 + (g + 1) * SSM_STATE]
        cb = _nt(cm, bm)
        y_diag, to_end, e_cum, c_dec, d_row = [], [], [], [], []
        for h in (2 * j, 2 * j + 1):
            col = _col_pick(a_cum, h)
            row = _row_pick(a_cum_t, h)
            dt_col = _col_pick(dt, h)
            dt_row = _row_pick(dt_t, h)
            decay = jnp.exp(jnp.where(causal, col - row, -jnp.inf))
            y_diag.append(_nn(cb * decay * dt_row, xs))
            last = jnp.sum(jnp.where(last_row, col, 0.0), axis=0, keepdims=True)
            to_end.append(jnp.exp(last - col) * dt_col)
            e_cum.append(jnp.exp(col))
            c_dec.append(jnp.exp(last))
            d_row.append(_lane_pick(d_skip, h))
        pair = lambda lo_hi: jnp.where(lane_lo, lo_hi[0], lo_hi[1])
        states = _tn(bm, xs * pair(to_end))
        y_off = _nn(cm, h_prev[j]) * pair(e_cum)
        ys.append(pair(y_diag) + y_off + xs * pair(d_row))
        h_next.append(pair(c_dec) * h_prev[j] + states)
    y = jnp.concatenate(ys, axis=1) * _silu(z)
    width = D_MODEL // SSM_GROUPS
    y = jnp.concatenate(
        [_rmsnorm(y[:, g * width:(g + 1) * width], norm_g[:, g * width:(g + 1) * width]) for g in range(SSM_GROUPS)],
        axis=1)
    return y, tuple(h_next)


def _shift_down(prev8, x, k):
    if k == 0:
        return x
    win = jnp.concatenate([prev8, x], axis=0)
    return pltpu.roll(win, k, 0)[HALO:]


def _shift_up(x, next8, k):
    if k == 0:
        return x
    n = x.shape[0]
    win = jnp.concatenate([x, next8], axis=0)
    return pltpu.roll(win, n + HALO - k, 0)[:n]


def _conv_pre(prev8, x, w, b):
    out = b + x * w[SSM_CONV - 1:SSM_CONV]
    for i in range(SSM_CONV - 1):
        out = out + _shift_down(prev8, x, SSM_CONV - 1 - i) * w[i:i + 1]
    return out


@jax.custom_vjp
def _swap_halves(x):
    return pltpu.roll(x, HEAD_DIM, 1)


_swap_halves.defvjp(lambda x: (pltpu.roll(x, HEAD_DIM, 1), None), lambda _, g: (pltpu.roll(g, HEAD_DIM, 1),))

_PAIRS_PER_KV = ATTN_HEADS // ATTN_KV // 2


def _attn_pairs(q4, kv_prev, kv_cur, sink_lo, sink_hi, first, kv_head):
    kv = jnp.concatenate([kv_prev, kv_cur], axis=0)
    lane = lax.broadcasted_iota(jnp.int32, (1, LANES), 1)
    own = (lane >= HEAD_DIM * kv_head) & (lane < HEAD_DIM * (kv_head + 1))

    def placed(pair):
        mine = jnp.where(own, pair, 0.0)
        lo = mine if kv_head == 0 else _swap_halves(mine)
        return lo, _swap_halves(lo)

    k_lo, k_hi = placed(kv[:, :LANES])
    v_lo, v_hi = placed(kv[:, LANES:])
    out = None
    for k_e, v_e, sink in ((k_lo, v_lo, sink_lo), (k_hi, v_hi, sink_hi)):
        s = _nt(q4, k_e) * (HEAD_DIM ** -0.5)
        rows = lax.broadcasted_iota(jnp.int32, s.shape, 0) & (CHUNK - 1)
        cols = lax.broadcasted_iota(jnp.int32, s.shape, 1)
        valid = (cols <= rows + CHUNK) & (cols > rows) & (cols >= CHUNK * first.astype(jnp.int32))
        s = jnp.where(valid, s, -jnp.inf)
        m = jnp.maximum(jnp.max(s, axis=-1, keepdims=True), sink)
        p = jnp.exp(s - m)
        denom = jnp.sum(p, axis=-1, keepdims=True) + jnp.exp(sink - m)
        o = _nn(p / denom, v_e)
        out = o if out is None else out + o
    return out


N_CHIP = 4
N_CORE = 2
_OTHER_CHIPS = (2, 4, 6)


@dataclasses.dataclass
class _Move:
    kind: str
    src: jax.Array

    def dst_shape(self):
        s = self.src.shape
        shape = {"gather": (N_DEV,) + s, "gather_ici": (N_CHIP, N_CORE) + s, "gather_d2d": s,
                 "scatter_d2d": (N_CHIP,) + s[2:], "scatter_ici": s}[self.kind]
        return jax.ShapeDtypeStruct(tuple(shape), self.src.dtype)


def _peer(x, y, c, k):
    return (1 - x if k & 4 else x, 1 - y if k & 2 else y, 1 - c if k & 1 else c)


def _move_copies(moves, srcs, dsts, send_sems, recv_sems, local_sems):
    x, y, c = lax.axis_index("x"), lax.axis_index("y"), lax.axis_index("c")
    chip = 2 * x + y
    me = 2 * chip + c
    sibling = (x, y, 1 - c)
    all_chips = pl.ds(0, N_CHIP)
    local, remote = [], []

    def push(n, k, src, dst, device):
        remote.append(pltpu.make_async_remote_copy(
            src_ref=src, dst_ref=dst, send_sem=send_sems.at[n, k], recv_sem=recv_sems.at[n, k],
            device_id=device, device_id_type=pl.DeviceIdType.MESH))

    for n, mv in enumerate(moves):
        s, d = srcs[n], dsts[n]
        if mv.kind == "gather":
            local.append(pltpu.make_async_copy(s, d.at[me], local_sems.at[n]))
            for k in range(1, N_DEV):
                push(n, k - 1, s, d.at[me], _peer(x, y, c, k))
        elif mv.kind == "gather_ici":
            local.append(pltpu.make_async_copy(s, d.at[chip, c], local_sems.at[n]))
            for k in _OTHER_CHIPS:
                push(n, k - 1, s, d.at[chip, c], _peer(x, y, c, k))
        elif mv.kind == "gather_d2d":
            push(n, 0, d.at[all_chips, c], d.at[all_chips, c], sibling)
        elif mv.kind == "scatter_d2d":
            push(n, 0, s.at[all_chips, 1 - c], d, sibling)
        else:
            assert mv.kind == "scatter_ici", mv.kind
            local.append(pltpu.make_async_copy(s.at[chip], d.at[chip], local_sems.at[n]))
            for k in _OTHER_CHIPS:
                px, py, _ = _peer(x, y, c, k)
                push(n, k - 1, s.at[2 * px + py], d.at[chip], (px, py, c))
    return local, remote


def _move_aliases(moves, n_in, n_out):
    return {n_in + n: n_out + n for n, mv in enumerate(moves) if mv.kind == "gather_d2d"}


def _pcall(body, *, name, grid, in_specs, out_specs, out_shape, scratch_shapes=(), semantics=(), moves=(),
           aliases=None):
    out_shape, out_specs = list(out_shape), list(out_specs)
    in_specs = list(in_specs)
    if not moves:
        call = pl.pallas_call(
            body, name=name, grid=grid, in_specs=in_specs, out_specs=out_specs, out_shape=out_shape,
            scratch_shapes=list(scratch_shapes), input_output_aliases=aliases or {},
            compiler_params=pltpu.CompilerParams(dimension_semantics=tuple(semantics),
                                                 vmem_limit_bytes=VMEM_LIMIT_BYTES))
        return (lambda *args: (list(call(*args)), []))
    n_in, n_out, n_scr, n_mv = len(in_specs), len(out_shape), len(scratch_shapes), len(moves)
    hbm = pl.BlockSpec(memory_space=pltpu.HBM)

    def carrier(*refs):
        ins, rest = refs[:n_in], refs[n_in:]
        srcs, rest = rest[:n_mv], rest[n_mv:]
        outs, rest = rest[:n_out], rest[n_out:]
        dsts, rest = rest[:n_mv], rest[n_mv:]
        scr, (send_sems, recv_sems, local_sems) = rest[:n_scr], rest[n_scr:]
        first = functools.reduce(jnp.logical_and, [pl.program_id(d) == 0 for d in range(len(grid))])
        last = functools.reduce(jnp.logical_and, [pl.program_id(d) == grid[d] - 1 for d in range(len(grid))])

        @pl.when(first)
        def _():
            local, remote = _move_copies(moves, srcs, dsts, send_sems, recv_sems, local_sems)
            for cp in local + remote:
                cp.start()

        body(*ins, *outs, *scr)

        @pl.when(last)
        def _():
            local, remote = _move_copies(moves, srcs, dsts, send_sems, recv_sems, local_sems)
            for cp in remote + local:
                cp.wait()

    call = pl.pallas_call(
        carrier, name=name, grid=grid,
        in_specs=in_specs + [hbm] * n_mv,
        out_specs=out_specs + [hbm] * n_mv,
        out_shape=out_shape + [mv.dst_shape() for mv in moves],
        scratch_shapes=list(scratch_shapes) + [pltpu.SemaphoreType.DMA((n_mv, N_DEV - 1)),
                                               pltpu.SemaphoreType.DMA((n_mv, N_DEV - 1)),
                                               pltpu.SemaphoreType.DMA((n_mv,))],
        input_output_aliases={**(aliases or {}), **_move_aliases(moves, n_in, n_out)},
        compiler_params=pltpu.CompilerParams(dimension_semantics=("arbitrary",) * len(grid),
                                             vmem_limit_bytes=VMEM_LIMIT_BYTES))

    def run(*args):
        res = list(call(*args, *[mv.src for mv in moves]))
        return res[:n_out], res[n_out:]

    return run


def _exchange(moves, *, name, then_d2d=()):
    n_mv, n_fwd = len(moves), len(then_d2d)
    hbm = pl.BlockSpec(memory_space=pltpu.HBM)
    second = [_Move("gather_d2d", moves[n].src) for n in then_d2d]

    def body(*refs):
        srcs, dsts, sems = refs[:n_mv], refs[n_mv:2 * n_mv], refs[2 * n_mv:]
        local, remote = _move_copies(moves, srcs, dsts, *sems[:3])
        for cp in local + remote:
            cp.start()
        for cp in remote + local:
            cp.wait()
        if second:
            landed = [dsts[n] for n in then_d2d]
            _, remote = _move_copies(second, landed, landed, sems[3], sems[4], None)
            for cp in remote:
                cp.start()
            for cp in remote:
                cp.wait()

    sems = [pltpu.SemaphoreType.DMA((n_mv, N_DEV - 1)), pltpu.SemaphoreType.DMA((n_mv, N_DEV - 1)),
            pltpu.SemaphoreType.DMA((n_mv,))]
    if second:
        sems += [pltpu.SemaphoreType.DMA((n_fwd, N_DEV - 1)), pltpu.SemaphoreType.DMA((n_fwd, N_DEV - 1))]
    return list(pl.pallas_call(
        body, name=name, in_specs=[hbm] * n_mv, out_specs=[hbm] * n_mv,
        out_shape=[mv.dst_shape() for mv in moves], scratch_shapes=sems,
    )(*[mv.src for mv in moves]))


TM = 512
FF_SHARD = D_FF // N_DEV


def _whole(a):
    nd = a.ndim
    return pl.BlockSpec(a.shape, lambda i: (0,) * nd)


def _rows(width, col=0):
    return pl.BlockSpec((TM, width), lambda i: (i, col))


def _acc_row(width):
    return pl.BlockSpec((1, width), lambda i: (0, 0))


def _unpack(res_landed, moves, n_out):
    res, landed = res_landed
    res = res[0] if n_out == 1 else res
    return (res, landed) if moves else res


def _norm_matmul(x, g, w, *, name, emit_y, moves=()):
    t, d = x.shape
    n = w.shape[1]

    def body(x_ref, g_ref, w_ref, *outs):
        y = _rmsnorm(x_ref[...], g_ref[...]).astype(BF16)
        if emit_y:
            outs[0][...] = y
        outs[-1][...] = lax.dot_general(y, w_ref[...], _NN, preferred_element_type=F32)

    shapes = ([jax.ShapeDtypeStruct((t, d), BF16)] if emit_y else []) + [jax.ShapeDtypeStruct((t, n), F32)]
    specs = ([_rows(d)] if emit_y else []) + [_rows(n)]
    return _unpack(_pcall(body, name=name, grid=(t // TM,), in_specs=[_rows(d), _acc_row(d), _whole(w)],
                          out_specs=specs, out_shape=shapes, semantics=("parallel",), moves=moves)(x, g, w),
                   moves, len(shapes))


def _residual_matmul(a, w, res, *, name, bias=None, norm_g=None, moves=()):
    t, k = a.shape
    n = w.shape[1]
    has_res, has_bias, has_norm = res is not None, bias is not None, norm_g is not None

    def body(a_ref, w_ref, *rest):
        rest = list(rest)
        res_ref = rest.pop(0) if has_res else None
        b_ref = rest.pop(0) if has_bias else None
        g_ref = rest.pop(0) if has_norm else None
        h = lax.dot_general(a_ref[...].astype(BF16), w_ref[...], _NN, preferred_element_type=F32)
        if has_res:
            h = h + res_ref[...]
        if has_bias:
            h = h + b_ref[...]
        rest[0][...] = h
        if has_norm:
            rest[1][...] = _rmsnorm(h, g_ref[...]).astype(BF16)

    rows_in = [res] if has_res else []
    extra = ([bias] if has_bias else []) + ([norm_g] if has_norm else [])
    shapes = [jax.ShapeDtypeStruct((t, n), F32)] + ([jax.ShapeDtypeStruct((t, n), BF16)] if has_norm else [])
    return _unpack(_pcall(body, name=name, grid=(t // TM,),
                          in_specs=[_rows(k), _whole(w)] + [_rows(n)] * len(rows_in) + [_acc_row(n)] * len(extra),
                          out_specs=[_rows(n)] * len(shapes), out_shape=shapes, semantics=("parallel",),
                          moves=moves)(a, w, *rows_in, *extra), moves, len(shapes))


def _mlp_up(y, w_cols, *, name, moves=()):
    t, d = y.shape

    def body(y_ref, w_ref, up_ref):
        yv = y_ref[...]
        for j in range(N_DEV):
            up_ref[:, j * FF_SHARD:(j + 1) * FF_SHARD] = lax.dot_general(
                yv, w_ref[j], _NN, preferred_element_type=F32).astype(up_ref.dtype)

    return _unpack(_pcall(body, name=name, grid=(t // TM,), in_specs=[_rows(d), _whole(w_cols)],
                          out_specs=[_rows(D_FF)], out_shape=[jax.ShapeDtypeStruct((t, D_FF), BF16)],
                          semantics=("parallel",), moves=moves)(y, w_cols), moves, 1)


def _sq_relu(u):
    return jnp.square(jnp.maximum(u.astype(F32), 0.0))


def _mlp_down(up, w_rows, res, *, name, norm_g=None, moves=()):
    t = up.shape[0]
    has_norm = norm_g is not None

    def body(up_ref, w_ref, res_ref, *rest):
        h = res_ref[...]
        for j in range(N_DEV):
            act = _sq_relu(up_ref[:, j * FF_SHARD:(j + 1) * FF_SHARD]).astype(BF16)
            h = h + lax.dot_general(act, w_ref[j], _NN, preferred_element_type=F32)
        if has_norm:
            g_ref, h_ref, y_ref = rest
            y_ref[...] = _rmsnorm(h, g_ref[...]).astype(BF16)
        else:
            (h_ref,) = rest
        h_ref[...] = h

    shapes = [jax.ShapeDtypeStruct((t, D_MODEL), F32)] + ([jax.ShapeDtypeStruct((t, D_MODEL), BF16)] if has_norm else [])
    return _unpack(_pcall(body, name=name, grid=(t // TM,),
                          in_specs=[_rows(D_FF), _whole(w_rows), _rows(D_MODEL)] + ([_acc_row(D_MODEL)] if has_norm else []),
                          out_specs=[_rows(D_MODEL)] * len(shapes), out_shape=shapes, semantics=("parallel",),
                          moves=moves)(up, w_rows, res, *([norm_g] if has_norm else [])), moves, len(shapes))


def _mlp_down_dx(dh, w_rows, up, *, name, moves=()):
    t = up.shape[0]

    def body(dh_ref, w_ref, up_ref, o_ref):
        dhv = dh_ref[...]
        for j in range(N_DEV):
            cols = slice(j * FF_SHARD, (j + 1) * FF_SHARD)
            d_act = lax.dot_general(dhv, w_ref[j], _NT, preferred_element_type=F32)
            o_ref[:, cols] = (d_act * (2.0 * jnp.maximum(up_ref[:, cols].astype(F32), 0.0))).astype(o_ref.dtype)

    return _unpack(_pcall(body, name=name, grid=(t // TM,),
                          in_specs=[_rows(D_MODEL), _whole(w_rows), _rows(D_FF)],
                          out_specs=[_rows(D_FF)], out_shape=[jax.ShapeDtypeStruct((t, D_FF), BF16)],
                          semantics=("parallel",), moves=moves)(dh, w_rows, up), moves, 1)


def _dw_by_cols(x, dy, *, name, tn, by_device=False, moves=()):
    t, k = x.shape
    n = dy.shape[1]
    assert n % tn == 0, (name, n, tn)

    def body(x_ref, dy_ref, o_ref):
        o_ref[...] = lax.dot_general(x_ref[...].astype(BF16), dy_ref[...].astype(BF16), _TN,
                                     preferred_element_type=F32).astype(o_ref.dtype)

    if by_device:
        out_spec, out_shape = pl.BlockSpec((None, k, tn), lambda j: (j, 0, 0)), (n // tn, k, tn)
    else:
        out_spec, out_shape = pl.BlockSpec((k, tn), lambda j: (0, j)), (k, n)
    return _unpack(_pcall(body, name=name, grid=(n // tn,),
                          in_specs=[_whole(x), pl.BlockSpec((t, tn), lambda j: (0, j))],
                          out_specs=[out_spec], out_shape=[jax.ShapeDtypeStruct(out_shape, BF16)],
                          semantics=("parallel",), moves=moves)(x, dy), moves, 1)


def _dw_by_rows(x, dy, *, name, tk, square_relu=False, moves=()):
    t, k = x.shape
    n = dy.shape[1]
    assert k % tk == 0, (name, k, tk)

    def body(x_ref, dy_ref, o_ref):
        xv = _sq_relu(x_ref[...]) if square_relu else x_ref[...]
        o_ref[...] = lax.dot_general(xv.astype(BF16), dy_ref[...].astype(BF16), _TN,
                                     preferred_element_type=F32).astype(o_ref.dtype)

    return _unpack(_pcall(body, name=name, grid=(k // tk,),
                          in_specs=[pl.BlockSpec((t, tk), lambda j: (0, j)), _whole(dy)],
                          out_specs=[pl.BlockSpec((tk, n), lambda j: (j, 0))],
                          out_shape=[jax.ShapeDtypeStruct((k, n), BF16)],
                          semantics=("parallel",), moves=moves)(x, dy), moves, 1)


def _dx(dy, w, *, name, partial=None, moves=()):
    t, k = dy.shape
    n = w.shape[0]
    has_partial = partial is not None

    def body(dy_ref, w_ref, *rest):
        out = lax.dot_general(dy_ref[...].astype(BF16), w_ref[...], _NT, preferred_element_type=F32)
        if has_partial:
            out = out + rest[0][...]
        rest[-1][...] = out

    return _unpack(_pcall(body, name=name, grid=(t // TM,),
                          in_specs=[_rows(k), _whole(w)] + ([_rows(n)] if has_partial else []),
                          out_specs=[_rows(n)], out_shape=[jax.ShapeDtypeStruct((t, n), F32)],
                          semantics=("parallel",), moves=moves)(dy, w, *([partial] if has_partial else [])),
                   moves, 1)


def _dx_norm(dy, w, h, g, dres, *, name, partial=None, by_device_cols=False, moves=()):
    t, k = dy.shape
    d = h.shape[1]
    has_partial = partial is not None

    def body(dy_ref, w_ref, h_ref, g_ref, dres_ref, *rest):
        if by_device_cols:
            kc = k // N_DEV
            d_y = jnp.zeros((TM, d), F32)
            for j in range(N_DEV):
                d_y = d_y + lax.dot_general(dy_ref[:, j * kc:(j + 1) * kc].astype(BF16), w_ref[j], _NT,
                                            preferred_element_type=F32)
        else:
            d_y = lax.dot_general(dy_ref[...].astype(BF16), w_ref[...], _NT, preferred_element_type=F32)
        if has_partial:
            d_y = d_y + rest[0][...]
        dh_ref, dhb_ref, dg_ref, cs_ref = rest[-4:]
        _, vjp = jax.vjp(_rmsnorm, h_ref[...], g_ref[...])
        dh, dg = vjp(d_y)
        dh = dh + dres_ref[...]
        dh_ref[...] = dh
        dhb_ref[...] = dh.astype(BF16)

        @pl.when(pl.program_id(0) == 0)
        def _():
            dg_ref[...] = jnp.zeros_like(dg_ref)
            cs_ref[...] = jnp.zeros_like(cs_ref)

        dg_ref[...] += dg
        cs_ref[...] += jnp.sum(dh, axis=0, keepdims=True)

    shapes = [jax.ShapeDtypeStruct((t, d), F32), jax.ShapeDtypeStruct((t, d), BF16),
              jax.ShapeDtypeStruct((1, d), F32), jax.ShapeDtypeStruct((1, d), F32)]
    return _unpack(_pcall(body, name=name, grid=(t // TM,),
                          in_specs=[_rows(k), _whole(w), _rows(d), _acc_row(d), _rows(d)]
                          + ([_rows(d)] if has_partial else []),
                          out_specs=[_rows(d), _rows(d), _acc_row(d), _acc_row(d)], out_shape=shapes,
                          semantics=("arbitrary",), moves=moves)(dy, w, h, g, dres, *([partial] if has_partial else [])),
                   moves, 4)


def _pair_add(by_core, theirs, core, *, name, tb=512):
    n_chip, _, r, c = by_core.shape
    tb = min(tb, r)
    assert r % tb == 0, (name, r, tb)

    def body(core_ref, a_ref, b_ref, o_ref):
        del core_ref
        o_ref[...] = (a_ref[...].astype(F32) + b_ref[...].astype(F32)).astype(o_ref.dtype)

    blk = pl.BlockSpec((None, tb, c), lambda ch, i, core_ref: (ch, i, 0))
    return pl.pallas_call(
        body, name=name,
        grid_spec=pltpu.PrefetchScalarGridSpec(
            num_scalar_prefetch=1, grid=(n_chip, r // tb),
            in_specs=[pl.BlockSpec((None, None, tb, c), lambda ch, i, core_ref: (ch, core_ref[0], i, 0)), blk],
            out_specs=blk),
        out_shape=jax.ShapeDtypeStruct((n_chip, r, c), by_core.dtype),
        compiler_params=pltpu.CompilerParams(dimension_semantics=("parallel", "parallel"),
                                             vmem_limit_bytes=VMEM_LIMIT_BYTES),
    )(core, by_core, theirs)


def _colsum(a, *, name, tb=512):
    t, d = a.shape

    def body(a_ref, o_ref):
        @pl.when(pl.program_id(0) == 0)
        def _():
            o_ref[...] = jnp.zeros_like(o_ref)

        o_ref[...] += jnp.sum(a_ref[...].astype(F32), axis=0, keepdims=True)

    return _pcall(
        body, name=name, grid=(t // tb,),
        in_specs=[pl.BlockSpec((tb, d), lambda i: (i, 0))],
        out_specs=[pl.BlockSpec((1, d), lambda i: (0, 0))],
        out_shape=[jax.ShapeDtypeStruct((1, d), F32)],
        semantics=("arbitrary",),
    )(a)[0][0]


def _final_loss(h, g, target, *, name, tb=512):
    t, d = h.shape

    def body(h_ref, g_ref, tgt_ref, loss_ref, dh_ref, dhb_ref, dg_ref):
        def f(hh, gg):
            err = jnp.square(_rmsnorm(hh, gg) - tgt_ref[...])
            return 0.5 * jnp.sum(jnp.mean(err, axis=-1, keepdims=True), axis=0, keepdims=True)

        val, vjp = jax.vjp(f, h_ref[...], g_ref[...])
        dh, dg = vjp(jnp.ones((1, 1), F32))
        dh_ref[...] = dh
        dhb_ref[...] = dh.astype(BF16)

        @pl.when(pl.program_id(0) == 0)
        def _():
            loss_ref[...] = jnp.zeros_like(loss_ref)
            dg_ref[...] = jnp.zeros_like(dg_ref)

        loss_ref[...] += val
        dg_ref[...] += dg

    blk = pl.BlockSpec((tb, d), lambda i: (i, 0))
    row = pl.BlockSpec((1, d), lambda i: (0, 0))
    return _pcall(
        body, name=name, grid=(t // tb,),
        in_specs=[blk, row, blk],
        out_specs=[pl.BlockSpec((8, LANES), lambda i: (0, 0)), blk, blk, row],
        out_shape=[jax.ShapeDtypeStruct((8, LANES), F32), jax.ShapeDtypeStruct((t, d), F32),
                   jax.ShapeDtypeStruct((t, d), BF16), jax.ShapeDtypeStruct((1, d), F32)],
        semantics=("arbitrary",),
    )(h, g, target)[0]


def _gmlp_fwd(proj_uv, ln_g, ln_b, w_s, b_s, *, name, moves=()):
    t = proj_uv.shape[0]
    w = D_MODEL

    def body(u_ref, v_ref, g_ref, b_ref, w_ref, bs_ref, o_ref):
        o_ref[...] = _gmlp_chunk(u_ref[...], v_ref[...], g_ref[...], b_ref[...], w_ref[...],
                                 bs_ref[...]).astype(o_ref.dtype)

    row = pl.BlockSpec((1, w), lambda i: (0, 0))
    res, landed = _pcall(
        body, name=name, grid=(t // CHUNK,),
        in_specs=[pl.BlockSpec((CHUNK, w), lambda i: (i, 0)), pl.BlockSpec((CHUNK, w), lambda i: (i, 1)), row, row,
                  pl.BlockSpec((GM_GROUPS, CHUNK, CHUNK), lambda i: (0, 0, 0)),
                  pl.BlockSpec((GM_GROUPS, CHUNK, 1), lambda i: (0, 0, 0))],
        out_specs=[pl.BlockSpec((CHUNK, w), lambda i: (i, 0))],
        out_shape=[jax.ShapeDtypeStruct((t, 2 * w), BF16)],
        semantics=("parallel",), moves=moves,
    )(proj_uv, proj_uv, ln_g, ln_b, w_s, b_s)
    return (res[0], landed) if moves else res[0]


def _gmlp_bwd(proj_uv, d_mix, ln_g, ln_b, w_s, b_s, *, name, moves=()):
    t = proj_uv.shape[0]
    w = D_MODEL

    def body(u_ref, v_ref, da_ref, g_ref, b_ref, w_ref, bs_ref, duv_ref, dg_ref, db_ref, dw_ref, dbs_ref):
        _, vjp = jax.vjp(_gmlp_chunk, u_ref[...], v_ref[...], g_ref[...], b_ref[...], w_ref[...], bs_ref[...])
        du, dv, dg, db, dw, dbs = vjp(da_ref[...])
        duv_ref[:, :w] = du.astype(duv_ref.dtype)
        duv_ref[:, w:] = dv.astype(duv_ref.dtype)

        @pl.when(pl.program_id(0) == 0)
        def _():
            dg_ref[...] = jnp.zeros_like(dg_ref)
            db_ref[...] = jnp.zeros_like(db_ref)
            dw_ref[...] = jnp.zeros_like(dw_ref)
            dbs_ref[...] = jnp.zeros_like(dbs_ref)

        dg_ref[...] += dg
        db_ref[...] += db
        dw_ref[...] += dw
        dbs_ref[...] += dbs

    row = pl.BlockSpec((1, w), lambda i: (0, 0))
    ws = pl.BlockSpec((GM_GROUPS, CHUNK, CHUNK), lambda i: (0, 0, 0))
    bs = pl.BlockSpec((GM_GROUPS, CHUNK, 1), lambda i: (0, 0, 0))
    res, landed = _pcall(
        body, name=name, grid=(t // CHUNK,),
        in_specs=[pl.BlockSpec((CHUNK, w), lambda i: (i, 0)), pl.BlockSpec((CHUNK, w), lambda i: (i, 1)),
                  pl.BlockSpec((CHUNK, w), lambda i: (i, 0)), row, row, ws, bs],
        out_specs=[pl.BlockSpec((CHUNK, 2 * w), lambda i: (i, 0)), row, row, ws, bs],
        out_shape=[jax.ShapeDtypeStruct((t, 2 * w), BF16), jax.ShapeDtypeStruct((1, w), F32),
                   jax.ShapeDtypeStruct((1, w), F32), jax.ShapeDtypeStruct((GM_GROUPS, CHUNK, CHUNK), F32),
                   jax.ShapeDtypeStruct((GM_GROUPS, CHUNK, 1), F32)],
        semantics=("arbitrary",), moves=moves,
    )(proj_uv, proj_uv, d_mix, ln_g, ln_b, w_s, b_s)
    return (res, landed) if moves else res


_HALO_PER_CHUNK = CHUNK // HALO
_DT_BLOCK = (CONV_DIM + D_MODEL) // LANES


def _ssd_fwd(proj_rest, mix, conv_w, conv_b, dt_bias, a_log, d_skip, norm_g, *, name, moves=()):
    t = proj_rest.shape[0]
    nc = t // CHUNK

    def body(x_ref, prev_ref, z_ref, dt_ref, mix_ref, cw_ref, cb_ref, dtb_ref, al_ref, ds_ref, ng_ref, y_ref, hs_ref,
             h_scr):
        del mix_ref
        i = pl.program_id(0)

        @pl.when(i == 0)
        def _():
            h_scr[...] = jnp.zeros_like(h_scr)

        prev8 = jnp.where(i == 0, 0.0, prev_ref[...])
        pre = _conv_pre(prev8, x_ref[...], cw_ref[...], cb_ref[...])
        hs_ref[0] = h_scr[...]
        h_prev = tuple(h_scr[j] for j in range(_PAIRS))
        y, h_next = _ssd_chunk(pre, z_ref[...], dt_ref[...], h_prev, dtb_ref[...], al_ref[...], ds_ref[...],
                               ng_ref[...])
        y_ref[...] = y.astype(y_ref.dtype)
        for j in range(_PAIRS):
            h_scr[j] = h_next[j]

    small = pl.BlockSpec((1, LANES), lambda i: (0, 0))
    res, landed = _pcall(
        body, name=name, grid=(nc,),
        in_specs=[pl.BlockSpec((CHUNK, CONV_DIM), lambda i: (i, 0)),
                  pl.BlockSpec((HALO, CONV_DIM), lambda i: (jnp.maximum(i * _HALO_PER_CHUNK - 1, 0), 0)),
                  pl.BlockSpec((CHUNK, D_MODEL), lambda i: (i, CONV_DIM // D_MODEL)),
                  pl.BlockSpec((CHUNK, LANES), lambda i: (i, _DT_BLOCK)),
                  pl.BlockSpec(memory_space=pl.ANY),
                  pl.BlockSpec((SSM_CONV, CONV_DIM), lambda i: (0, 0)),
                  pl.BlockSpec((1, CONV_DIM), lambda i: (0, 0)),
                  small, small, small, pl.BlockSpec((1, D_MODEL), lambda i: (0, 0))],
        out_specs=[pl.BlockSpec((CHUNK, D_MODEL), lambda i: (i, 1)),
                   pl.BlockSpec((1, _PAIRS, SSM_STATE, LANES), lambda i: (i, 0, 0, 0))],
        out_shape=[jax.ShapeDtypeStruct((t, 2 * D_MODEL), BF16),
                   jax.ShapeDtypeStruct((nc, _PAIRS, SSM_STATE, LANES), F32)],
        scratch_shapes=[pltpu.VMEM((_PAIRS, SSM_STATE, LANES), F32)],
        semantics=("arbitrary",), moves=moves, aliases={4: 0},
    )(proj_rest, proj_rest, proj_rest, proj_rest, mix, conv_w, conv_b, dt_bias, a_log, d_skip, norm_g)
    return (res, landed) if moves else res


def _ssd_bwd(proj_rest, h_states, d_mix, conv_w, conv_b, dt_bias, a_log, d_skip, norm_g, *, name, moves=()):
    t = proj_rest.shape[0]
    nc = t // CHUNK

    def body(x_ref, prev_ref, z_ref, dt_ref, hs_ref, dy_ref, cw_ref, cb_ref, dtb_ref, al_ref, ds_ref, ng_ref,
             dpre_ref, dz_ref, ddt_ref, ddtb_ref, dal_ref, dds_ref, dng_ref, dh_scr):
        i = pl.program_id(0)
        chunk = nc - 1 - i

        @pl.when(i == 0)
        def _():
            dh_scr[...] = jnp.zeros_like(dh_scr)
            ddtb_ref[...] = jnp.zeros_like(ddtb_ref)
            dal_ref[...] = jnp.zeros_like(dal_ref)
            dds_ref[...] = jnp.zeros_like(dds_ref)
            dng_ref[...] = jnp.zeros_like(dng_ref)

        prev8 = jnp.where(chunk == 0, 0.0, prev_ref[...])
        pre = _conv_pre(prev8, x_ref[...], cw_ref[...], cb_ref[...])
        h_prev = tuple(hs_ref[0, j] for j in range(_PAIRS))
        _, vjp = jax.vjp(_ssd_chunk, pre, z_ref[...], dt_ref[...], h_prev, dtb_ref[...], al_ref[...],
                         ds_ref[...], ng_ref[...])
        dpre, dz, ddt, dh_prev, ddtb, dal, dds, dng = vjp((dy_ref[...], tuple(dh_scr[j] for j in range(_PAIRS))))
        dpre_ref[...] = dpre
        dz_ref[...] = dz
        ddt_ref[...] = ddt
        for j in range(_PAIRS):
            dh_scr[j] = dh_prev[j]
        ddtb_ref[...] += ddtb
        dal_ref[...] += dal
        dds_ref[...] += dds
        dng_ref[...] += dng

    rev = lambda i: nc - 1 - i
    small = pl.BlockSpec((1, LANES), lambda i: (0, 0))
    wide = pl.BlockSpec((1, D_MODEL), lambda i: (0, 0))
    res, landed = _pcall(
        body, name=name, grid=(nc,),
        in_specs=[pl.BlockSpec((CHUNK, CONV_DIM), lambda i: (rev(i), 0)),
                  pl.BlockSpec((HALO, CONV_DIM), lambda i: (jnp.maximum(rev(i) * _HALO_PER_CHUNK - 1, 0), 0)),
                  pl.BlockSpec((CHUNK, D_MODEL), lambda i: (rev(i), CONV_DIM // D_MODEL)),
                  pl.BlockSpec((CHUNK, LANES), lambda i: (rev(i), _DT_BLOCK)),
                  pl.BlockSpec((1, _PAIRS, SSM_STATE, LANES), lambda i: (rev(i), 0, 0, 0)),
                  pl.BlockSpec((CHUNK, D_MODEL), lambda i: (rev(i), 1)),
                  pl.BlockSpec((SSM_CONV, CONV_DIM), lambda i: (0, 0)),
                  pl.BlockSpec((1, CONV_DIM), lambda i: (0, 0)),
                  small, small, small, wide],
        out_specs=[pl.BlockSpec((CHUNK, CONV_DIM), lambda i: (rev(i), 0)),
                   pl.BlockSpec((CHUNK, D_MODEL), lambda i: (rev(i), 0)),
                   pl.BlockSpec((CHUNK, LANES), lambda i: (rev(i), 0)),
                   small, small, small, wide],
        out_shape=[jax.ShapeDtypeStruct((t, CONV_DIM), F32), jax.ShapeDtypeStruct((t, D_MODEL), F32),
                   jax.ShapeDtypeStruct((t, LANES), F32),
                   jax.ShapeDtypeStruct((1, LANES), F32), jax.ShapeDtypeStruct((1, LANES), F32),
                   jax.ShapeDtypeStruct((1, LANES), F32), jax.ShapeDtypeStruct((1, D_MODEL), F32)],
        scratch_shapes=[pltpu.VMEM((_PAIRS, SSM_STATE, LANES), F32)],
        semantics=("arbitrary",), moves=moves,
    )(proj_rest, proj_rest, proj_rest, proj_rest, h_states, d_mix, conv_w, conv_b, dt_bias, a_log, d_skip, norm_g)
    return (res, landed) if moves else res


def _conv_bwd(proj_rest, dpre, dz, ddt, conv_w, *, name, tb=256, moves=()):
    t = proj_rest.shape[0]
    nb = t // tb
    per = tb // HALO

    def body(x_ref, prev_ref, dpre_ref, next_ref, dz_ref, ddt_ref, cw_ref, drest_ref, dcw_ref, dcb_ref):
        i = pl.program_id(0)

        @pl.when(i == 0)
        def _():
            dcw_ref[...] = jnp.zeros_like(dcw_ref)
            dcb_ref[...] = jnp.zeros_like(dcb_ref)

        x = x_ref[...]
        dp = dpre_ref[...]
        w = cw_ref[...]
        prev8 = jnp.where(i == 0, 0.0, prev_ref[...])
        next8 = jnp.where(i == nb - 1, 0.0, next_ref[...])
        dx = dp * w[SSM_CONV - 1:SSM_CONV]
        for j in range(SSM_CONV - 1):
            dx = dx + _shift_up(dp, next8, SSM_CONV - 1 - j) * w[j:j + 1]
        drest_ref[:, :CONV_DIM] = dx.astype(drest_ref.dtype)
        drest_ref[:, CONV_DIM:CONV_DIM + D_MODEL] = dz_ref[...].astype(drest_ref.dtype)
        drest_ref[:, CONV_DIM + D_MODEL:] = ddt_ref[...].astype(drest_ref.dtype)
        for j in range(SSM_CONV):
            dcw_ref[j:j + 1, :] += jnp.sum(dp * _shift_down(prev8, x, SSM_CONV - 1 - j), axis=0, keepdims=True)
        dcb_ref[...] += jnp.sum(dp, axis=0, keepdims=True)

    res, landed = _pcall(
        body, name=name, grid=(nb,),
        in_specs=[pl.BlockSpec((tb, CONV_DIM), lambda i: (i, 0)),
                  pl.BlockSpec((HALO, CONV_DIM), lambda i: (jnp.maximum(i * per - 1, 0), 0)),
                  pl.BlockSpec((tb, CONV_DIM), lambda i: (i, 0)),
                  pl.BlockSpec((HALO, CONV_DIM), lambda i: (jnp.minimum((i + 1) * per, nb * per - 1), 0)),
                  pl.BlockSpec((tb, D_MODEL), lambda i: (i, 0)),
                  pl.BlockSpec((tb, LANES), lambda i: (i, 0)),
                  pl.BlockSpec((SSM_CONV, CONV_DIM), lambda i: (0, 0))],
        out_specs=[pl.BlockSpec((tb, REST_W), lambda i: (i, 0)),
                   pl.BlockSpec((SSM_CONV, CONV_DIM), lambda i: (0, 0)),
                   pl.BlockSpec((1, CONV_DIM), lambda i: (0, 0))],
        out_shape=[jax.ShapeDtypeStruct((t, REST_W), BF16), jax.ShapeDtypeStruct((SSM_CONV, CONV_DIM), F32),
                   jax.ShapeDtypeStruct((1, CONV_DIM), F32)],
        semantics=("arbitrary",), moves=moves,
    )(proj_rest, proj_rest, dpre, dpre, dz, ddt, conv_w)
    return (res, landed) if moves else res


_KV_BLOCK = D_MODEL // (2 * LANES)
_SINK_ROWS = _PAIRS_PER_KV * CHUNK


def _stack_pairs(ref, kv_head):
    base = kv_head * _PAIRS_PER_KV
    return jnp.concatenate([ref[:, (base + p) * LANES:(base + p + 1) * LANES] for p in range(_PAIRS_PER_KV)], axis=0)


def _attn_fwd(qkv, sinks, *, name, moves=()):
    t = qkv.shape[0]
    nb = t // CHUNK

    def body(q_ref, kvp_ref, kvc_ref, s_ref, o_ref):
        first = pl.program_id(0) == 0
        for j in range(ATTN_KV):
            o = _attn_pairs(_stack_pairs(q_ref, j), kvp_ref[...], kvc_ref[...], s_ref[j, 0], s_ref[j, 1], first, j)
            for p in range(_PAIRS_PER_KV):
                col = (j * _PAIRS_PER_KV + p) * LANES
                o_ref[:, col:col + LANES] = o[p * CHUNK:(p + 1) * CHUNK].astype(o_ref.dtype)

    return _unpack(_pcall(
        body, name=name, grid=(nb,),
        in_specs=[pl.BlockSpec((CHUNK, D_MODEL), lambda i: (i, 0)),
                  pl.BlockSpec((CHUNK, 2 * LANES), lambda i: (jnp.maximum(i - 1, 0), _KV_BLOCK)),
                  pl.BlockSpec((CHUNK, 2 * LANES), lambda i: (i, _KV_BLOCK)),
                  pl.BlockSpec((ATTN_KV, 2, _SINK_ROWS, 1), lambda i: (0, 0, 0, 0))],
        out_specs=[pl.BlockSpec((CHUNK, D_MODEL), lambda i: (i, 0))],
        out_shape=[jax.ShapeDtypeStruct((t, D_MODEL), BF16)],
        semantics=("parallel",), moves=moves,
    )(qkv, qkv, qkv, sinks), moves, 1)


def _attn_bwd(qkv, sinks, d_o, *, name, moves=()):
    t = qkv.shape[0]
    nb = t // CHUNK

    def body(q_ref, kvp_ref, kvc_ref, s_ref, do_ref, dqkv_ref, ds_ref, dkv_scr):
        i = pl.program_id(0)
        first = i == nb - 1

        @pl.when(i == 0)
        def _():
            dkv_scr[...] = jnp.zeros_like(dkv_scr)
            ds_ref[...] = jnp.zeros_like(ds_ref)

        dkv_cur = dkv_scr[...]
        dkv_prev = jnp.zeros_like(dkv_cur)
        for j in range(ATTN_KV):
            _, vjp = jax.vjp(functools.partial(_attn_pairs, first=first, kv_head=j), _stack_pairs(q_ref, j),
                             kvp_ref[...], kvc_ref[...], s_ref[j, 0], s_ref[j, 1])
            dq4, dkvp, dkvc, ds_lo, ds_hi = vjp(_stack_pairs(do_ref, j))
            for p in range(_PAIRS_PER_KV):
                col = (j * _PAIRS_PER_KV + p) * LANES
                dqkv_ref[:, col:col + LANES] = dq4[p * CHUNK:(p + 1) * CHUNK]
            dkv_cur = dkv_cur + dkvc
            dkv_prev = dkv_prev + dkvp
            ds_ref[j, 0] += ds_lo
            ds_ref[j, 1] += ds_hi
        dqkv_ref[:, D_MODEL:] = dkv_cur
        dkv_scr[...] = dkv_prev

    cur = lambda i: (nb - 1 - i, 0)
    sk = pl.BlockSpec((ATTN_KV, 2, _SINK_ROWS, 1), lambda i: (0, 0, 0, 0))
    res, landed = _pcall(
        body, name=name, grid=(nb,),
        in_specs=[pl.BlockSpec((CHUNK, D_MODEL), cur),
                  pl.BlockSpec((CHUNK, 2 * LANES), lambda i: (jnp.maximum(nb - 2 - i, 0), _KV_BLOCK)),
                  pl.BlockSpec((CHUNK, 2 * LANES), lambda i: (nb - 1 - i, _KV_BLOCK)),
                  sk, pl.BlockSpec((CHUNK, D_MODEL), cur)],
        out_specs=[pl.BlockSpec((CHUNK, QKV_DIM), cur), sk],
        out_shape=[jax.ShapeDtypeStruct((t, QKV_DIM), F32), jax.ShapeDtypeStruct((ATTN_KV, 2, _SINK_ROWS, 1), F32)],
        scratch_shapes=[pltpu.VMEM((CHUNK, 2 * LANES), F32)],
        semantics=("arbitrary",), moves=moves,
    )(qkv, qkv, qkv, sinks, d_o)
    return (res, landed) if moves else res


def _adamw(parts, w, m, v, *, name, tb=256, moves=()):
    layers, r, c = w.shape
    n = parts[0].shape[0]
    tb = min(tb, r)
    assert r % tb == 0 and len(parts) == layers, (name, r, tb)
    nb = r // tb

    def body(*refs):
        p_refs = refs[:layers]
        w_ref, m_ref, v_ref, g_ref, d_ref, nm_ref, nv_ref = refs[layers:]
        for layer in range(layers):
            @pl.when(pl.program_id(0) == layer)
            def _(p_ref=p_refs[layer]):
                g = p_ref[0].astype(F32)
                for s in range(1, n):
                    g = g + p_ref[s].astype(F32)
                m_new = ADAM_B1 * m_ref[...] + (1.0 - ADAM_B1) * g
                v_new = ADAM_B2 * v_ref[...] + (1.0 - ADAM_B2) * jnp.square(g)
                m_hat = m_new / (1.0 - ADAM_B1 ** ADAM_STEP)
                v_hat = v_new / (1.0 - ADAM_B2 ** ADAM_STEP)
                g_ref[...] = g
                d_ref[...] = -ADAM_LR * (m_hat / (jnp.sqrt(v_hat) + ADAM_EPS) + ADAM_WD * w_ref[...])
                nm_ref[...] = m_new
                nv_ref[...] = v_new

    part_spec = lambda layer: pl.BlockSpec(
        (n, tb, c), lambda l, i: (0, jnp.clip(i + (l - layer) * nb, 0, nb - 1), 0))
    blk = pl.BlockSpec((None, tb, c), lambda l, i: (l, i, 0))
    res, landed = _pcall(
        body, name=name, grid=(layers, nb),
        in_specs=[part_spec(layer) for layer in range(layers)] + [blk, blk, blk],
        out_specs=[blk] * 4,
        out_shape=[jax.ShapeDtypeStruct((layers, r, c), F32)] * 4,
        semantics=("arbitrary", "arbitrary"), moves=moves,
    )(*parts, w, m, v)
    return (res, landed) if moves else res


def _as_rows(a):
    flat = a.reshape(-1)
    pad = (-flat.shape[0]) % PACK_W
    if pad:
        flat = jnp.pad(flat, (0, pad))
    return flat.reshape(-1, PACK_W)


def _cols_from_shards(g):
    return jnp.transpose(g, (1, 0, 2)).reshape(g.shape[1], -1)


def _cols_to_shards(a):
    return jnp.transpose(a.reshape(a.shape[0], N_DEV, -1), (1, 0, 2))


def _pad_lanes(a):
    return jnp.pad(a, ((0, 0), (0, LANES - a.shape[1])))


def kernel(x, norm_mix_g, norm_mlp_g, final_norm_g, w_in_even, w_out_even, gm_ln_g, gm_ln_b, gm_w_s, gm_b_s, ssm_conv_w, ssm_conv_b, ssm_dt_bias, ssm_a_log, ssm_d, ssm_norm_g, w_qkv, b_qkv, w_o, b_o, attn_sinks, w_up, w_down, loss_target, m_norm_mix_g, m_norm_mlp_g, m_final_norm_g, m_w_in_even, m_w_out_even, m_gm_ln_g, m_gm_ln_b, m_gm_w_s, m_gm_b_s, m_ssm_conv_w, m_ssm_conv_b, m_ssm_dt_bias, m_ssm_a_log, m_ssm_d, m_ssm_norm_g, m_w_qkv, m_b_qkv, m_w_o, m_b_o, m_attn_sinks, m_w_up, m_w_down, v_norm_mix_g, v_norm_mlp_g, v_final_norm_g, v_w_in_even, v_w_out_even, v_gm_ln_g, v_gm_ln_b, v_gm_w_s, v_gm_b_s, v_ssm_conv_w, v_ssm_conv_b, v_ssm_dt_bias, v_ssm_a_log, v_ssm_d, v_ssm_norm_g, v_w_qkv, v_b_qkv, v_w_o, v_b_o, v_attn_sinks, v_w_up, v_w_down):
    names = ["norm_mix_g", "norm_mlp_g", "final_norm_g", "w_in_even", "w_out_even", "gm_ln_g", "gm_ln_b", "gm_w_s",
             "gm_b_s", "ssm_conv_w", "ssm_conv_b", "ssm_dt_bias", "ssm_a_log", "ssm_d", "ssm_norm_g", "w_qkv",
             "b_qkv", "w_o", "b_o", "attn_sinks", "w_up", "w_down"]
    env = locals()
    W = {n: env[n] for n in names}
    M = {n: env["m_" + n] for n in names}
    V = {n: env["v_" + n] for n in names}
    big = ["w_in_even", "w_out_even", "w_qkv", "w_o", "w_up", "w_down"]
    small_sharded = ["ssm_conv_w", "b_qkv", "b_o"]
    replicated = [n for n in names if n not in big and n not in small_sharded]
    me = 4 * lax.axis_index("x") + 2 * lax.axis_index("y") + lax.axis_index("c")
    t = x.shape[1]
    xs = x.reshape(t, D_MODEL)
    target = loss_target.reshape(t, D_MODEL)
    gather = lambda a: _Move("gather", a)
    over_ici = lambda a: _Move("gather_ici", a)
    over_d2d = lambda a: _Move("gather_d2d", a)
    by_core = lambda a: a.reshape((N_CHIP, N_CORE) + a.shape[1:])
    to_sibling = lambda a: [_Move("scatter_d2d", by_core(a))]
    my_core = lax.axis_index("c").astype(jnp.int32).reshape(1)
    pair = lambda a, theirs, name: _pair_add(by_core(a), theirs, my_core, name=name)
    to_chips = lambda a: _Move("scatter_ici", a)
    whole = lambda a: a.reshape((N_DEV,) + a.shape[2:])
    row = lambda a: a.reshape(1, D_MODEL)

    small_flat = jnp.concatenate([W[n].reshape(-1) for n in small_sharded])
    w_in_g, small_g = _exchange([over_ici(w_in_even[0].astype(BF16)), gather(_as_rows(small_flat))],
                                name="gather_w_in", then_d2d=[0])
    w_in = _cols_from_shards(whole(w_in_g))
    w_uv = w_in[:, :2 * D_MODEL]
    w_rest = jnp.concatenate([w_in[:, 3 * D_MODEL:3 * D_MODEL + CONV_DIM], w_in[:, 2 * D_MODEL:3 * D_MODEL],
                              w_in[:, 3 * D_MODEL + CONV_DIM:],
                              jnp.zeros((D_MODEL, LANES - SSM_HEADS), BF16)], axis=1)
    small_all = small_g.reshape(N_DEV, -1)
    n_cw = SSM_CONV * CONV_DIM // N_DEV
    n_bq = QKV_DIM // N_DEV
    conv_w = _cols_from_shards(small_all[:, :n_cw].reshape(N_DEV, SSM_CONV, CONV_DIM // N_DEV))
    bqkv = small_all[:, n_cw:n_cw + n_bq].reshape(1, QKV_DIM)
    bo = small_all[:, n_cw + n_bq:n_cw + n_bq + D_MODEL // N_DEV].reshape(1, D_MODEL)

    conv_b = ssm_conv_b.reshape(1, CONV_DIM)
    dt_bias, a_log, d_skip = _pad_lanes(ssm_dt_bias), _pad_lanes(ssm_a_log), _pad_lanes(ssm_d)
    gm_w = gm_w_s[0]
    gm_b = gm_b_s[0].reshape(GM_GROUPS, CHUNK, 1)
    sink_rows = jnp.repeat(jnp.transpose(attn_sinks.reshape(ATTN_KV, _PAIRS_PER_KV, 2), (0, 2, 1)), CHUNK,
                           axis=2).reshape(ATTN_KV, 2, _SINK_ROWS, 1)
    w_up_b, w_down_b = w_up.astype(BF16), w_down.astype(BF16)

    (y0, proj_uv), (w_out_g,) = _norm_matmul(xs, row(norm_mix_g[0]), w_uv, name="proj_uv", emit_y=True,
                                             moves=[over_ici(w_out_even[0].astype(BF16))])
    proj_rest, (w_qkv_g, w_o_g) = _norm_matmul(xs, row(norm_mix_g[0]), w_rest, name="proj_rest", emit_y=False,
                                               moves=[over_ici(w_qkv[0].astype(BF16)), over_ici(w_o[0].astype(BF16))])
    mix, (w_out_g, w_qkv_g, w_o_g) = _gmlp_fwd(
        proj_uv, gm_ln_g, gm_ln_b, gm_w, gm_b, name="gmlp_fwd",
        moves=[over_d2d(w_out_g), over_d2d(w_qkv_g), over_d2d(w_o_g)])
    (mix, h_states), (w_up0_g,) = _ssd_fwd(
        proj_rest, mix, conv_w, conv_b, dt_bias, a_log, d_skip, ssm_norm_g, name="ssd_fwd",
        moves=[over_ici(w_up_b[0])])
    w_out_f = whole(w_out_g).reshape(2 * D_MODEL, D_MODEL)
    (h1, y1), (w_down0_g, w_up0_g) = _residual_matmul(
        mix, w_out_f, xs, name="mix_out", norm_g=row(norm_mlp_g[0]),
        moves=[over_ici(w_down_b[0]), over_d2d(w_up0_g)])
    up0, (w_down0_g,) = _mlp_up(y1, whole(w_up0_g), name="mlp_up0", moves=[over_d2d(w_down0_g)])
    h2, y2 = _mlp_down(up0, whole(w_down0_g), h1, name="mlp_down0", norm_g=row(norm_mix_g[1]))
    wqkv = _cols_from_shards(whole(w_qkv_g))
    wo = whole(w_o_g).reshape(D_MODEL, D_MODEL)
    qkv = _residual_matmul(y2, wqkv, None, name="qkv", bias=bqkv)
    attn, (w_up1_g, w_down1_g) = _attn_fwd(qkv, sink_rows, name="attn_fwd",
                                           moves=[over_ici(w_up_b[1]), over_ici(w_down_b[1])])
    (h3, y3), (w_up1_g, w_down1_g) = _residual_matmul(attn, wo, h2, name="attn_out", bias=bo, norm_g=row(norm_mlp_g[1]),
                                                      moves=[over_d2d(w_up1_g), over_d2d(w_down1_g)])
    w_up_g = [whole(w_up0_g), whole(w_up1_g)]
    w_down_g = [whole(w_down0_g), whole(w_down1_g)]
    up1 = _mlp_up(y3, w_up_g[1], name="mlp_up1")
    h4 = _mlp_down(up1, w_down_g[1], h3, name="mlp_down1")
    loss_part, dh4, dh4_b, d_final_g = _final_loss(h4, row(final_norm_g), target, name="final_loss")

    by_dev_rows = lambda a: a.reshape((N_DEV, a.shape[0] // N_DEV) + a.shape[1:])

    def mlp_bwd(dh, dh_b, h, y, up, layer, first_moves=()):
        res = _mlp_down_dx(dh_b, w_down_g[layer], up, name=f"mlp_down_dx{layer}", moves=first_moves)
        d_up, first_landed = res if first_moves else (res, [])
        g_down = _dw_by_rows(up, dh_b, name=f"mlp_down_dw{layer}", tk=FF_SHARD, square_relu=True)
        g_down = by_dev_rows(g_down)
        g_up, (theirs,) = _dw_by_cols(y, d_up, name=f"mlp_up_dw{layer}", tn=FF_SHARD, by_device=True,
                                      moves=to_sibling(g_down))
        q_down = pair(g_down, theirs, f"mlp_down_pair{layer}")
        (dh_new, dh_new_b, dg, cs), (r_down, theirs) = _dx_norm(
            d_up, w_up_g[layer], h, row(norm_mlp_g[layer]), dh, name=f"mlp_up_dx{layer}", by_device_cols=True,
            moves=[to_chips(q_down)] + to_sibling(g_up))
        q_up = pair(g_up, theirs, f"mlp_up_pair{layer}")
        return dh_new, dh_new_b, cs, dg, q_up, r_down, first_landed

    dh3, dh3_b, cs3, g_nmlp1, q_up1, r_down1, _ = mlp_bwd(dh4, dh4_b, h3, y3, up1, 1)
    g_bo = cs3
    g_wo = by_dev_rows(_dw_by_cols(attn, dh3_b, name="attn_out_dw", tn=FF_SHARD))
    d_attn, (theirs,) = _dx(dh3_b, wo, name="attn_out_dx", moves=to_sibling(g_wo))
    q_wo = pair(g_wo, theirs, "attn_out_pair")
    (dqkv, d_sink), (r_up1, r_wo) = _attn_bwd(qkv, sink_rows, d_attn, name="attn_bwd",
                                              moves=[to_chips(q_up1), to_chips(q_wo)])
    g_bqkv = _colsum(dqkv, name="qkv_db")
    g_wqkv = _cols_to_shards(_dw_by_cols(y2, dqkv, name="qkv_dw", tn=QKV_DIM // 2))
    (dh2, dh2_b, g_nmix1, _), (theirs,) = _dx_norm(dqkv, wqkv, h2, row(norm_mix_g[1]), dh3, name="qkv_dx",
                                                   moves=to_sibling(g_wqkv))
    q_wqkv = pair(g_wqkv, theirs, "qkv_pair")
    dh1, dh1_b, _, g_nmlp0, q_up0, r_down0, (r_wqkv,) = mlp_bwd(dh2, dh2_b, h1, y1, up0, 0,
                                                                first_moves=[to_chips(q_wqkv)])

    d_mix = _dx(dh1_b, w_out_f, name="mix_out_dx")
    g_wout = by_dev_rows(_dw_by_rows(mix, dh1_b, name="mix_out_dw", tk=FF_SHARD))
    (d_uv, g_ln_g, g_ln_b, g_gm_w, g_gm_b), (r_up0, theirs) = _gmlp_bwd(
        proj_uv, d_mix, gm_ln_g, gm_ln_b, gm_w, gm_b, name="gmlp_bwd", moves=[to_chips(q_up0)] + to_sibling(g_wout))
    q_wout = pair(g_wout, theirs, "mix_out_pair")

    early = [("norm_mlp_g", None), ("final_norm_g", None), ("norm_mix_g", 1), ("gm_ln_g", None), ("gm_ln_b", None),
             ("gm_w_s", None), ("gm_b_s", None), ("attn_sinks", None)]
    late = [("norm_mix_g", 0), ("ssm_conv_b", None), ("ssm_dt_bias", None), ("ssm_a_log", None), ("ssm_d", None),
            ("ssm_norm_g", None)]
    early_sharded, late_sharded = ["b_qkv", "b_o"], ["ssm_conv_w"]
    small_grads = {
        ("norm_mlp_g", None): jnp.concatenate([g_nmlp0, g_nmlp1], axis=0),
        ("final_norm_g", None): d_final_g, ("norm_mix_g", 1): g_nmix1,
        ("gm_ln_g", None): g_ln_g, ("gm_ln_b", None): g_ln_b, ("gm_w_s", None): g_gm_w, ("gm_b_s", None): g_gm_b,
        ("attn_sinks", None): jnp.transpose(
            jnp.sum(d_sink.reshape(ATTN_KV, 2, _PAIRS_PER_KV, CHUNK), axis=3), (0, 2, 1)),
        "b_qkv": g_bqkv, "b_o": g_bo,
    }
    pack = lambda keys: _as_rows(jnp.concatenate([small_grads[key].reshape(-1) for key in keys]))
    (dpre, dz, ddt, g_dtb, g_alog, g_dskip, g_ssm_ng), (r_wout, early_recv) = _ssd_bwd(
        proj_rest, h_states, d_mix, conv_w, conv_b, dt_bias, a_log, d_skip, ssm_norm_g, name="ssd_bwd",
        moves=[to_chips(q_wout), gather(pack(early + early_sharded))])
    d_rest, g_conv_w, g_conv_b = _conv_bwd(proj_rest, dpre, dz, ddt, conv_w, name="conv_bwd")
    g_w_uv = _dw_by_cols(y0, d_uv, name="proj_uv_dw", tn=FF_SHARD)
    g_w_rest = _dw_by_cols(y0, d_rest, name="proj_rest_dw", tn=REST_W // 5)
    g_w_in = jnp.concatenate([g_w_uv, g_w_rest[:, CONV_DIM:CONV_DIM + D_MODEL], g_w_rest[:, :CONV_DIM],
                              g_w_rest[:, CONV_DIM + D_MODEL:CONV_DIM + D_MODEL + SSM_HEADS]], axis=1)
    g_w_in = _cols_to_shards(g_w_in)
    dy0, (theirs,) = _dx(d_uv, w_uv, name="proj_uv_dx", moves=to_sibling(g_w_in))
    q_w_in = pair(g_w_in, theirs, "proj_pair")
    (dx, _, g_nmix0, _), (r_w_in,) = _dx_norm(d_rest, w_rest, xs, row(norm_mix_g[0]), dh1, name="proj_rest_dx",
                                              partial=dy0, moves=[to_chips(q_w_in)])
    small_grads.update({
        ("norm_mix_g", 0): g_nmix0, ("ssm_conv_b", None): g_conv_b,
        ("ssm_dt_bias", None): g_dtb[:, :SSM_HEADS], ("ssm_a_log", None): g_alog[:, :SSM_HEADS],
        ("ssm_d", None): g_dskip[:, :SSM_HEADS], ("ssm_norm_g", None): g_ssm_ng, "ssm_conv_w": g_conv_w,
    })


    def update(n, parts, moves=()):
        shape = W[n].shape
        as3 = lambda a: a.reshape((len(parts),) + parts[0].shape[1:])
        res = _adamw(parts, as3(W[n]), as3(M[n]), as3(V[n]), name="adamw_" + n, moves=moves)
        res, landed = res if moves else (res, [])
        return [a.reshape(shape) for a in res], landed

    out = {}
    out["w_o"], (late_recv,) = update("w_o", [r_wo], moves=[gather(pack(late + late_sharded))])
    out["w_down"], _ = update("w_down", [r_down0, r_down1])
    out["w_up"], _ = update("w_up", [r_up0, r_up1])
    out["w_out_even"], _ = update("w_out_even", [r_wout])
    out["w_qkv"], _ = update("w_qkv", [r_wqkv])
    out["w_in_even"], _ = update("w_in_even", [r_w_in])

    def unpacked(recv, keys):
        flat, res, o = recv.reshape(N_DEV, -1), {}, 0
        for key in keys:
            res[key] = flat[:, o:o + small_grads[key].size]
            o += small_grads[key].size
        return res

    arrived = {**unpacked(early_recv, early + early_sharded), **unpacked(late_recv, late + late_sharded)}
    piece = lambda tree, key: tree[key[0]] if key[1] is None else tree[key[0]][key[1]]

    def rows_by_device(cat):
        pad = (-cat.shape[1]) % PACK_W
        return jnp.pad(cat, ((0, 0), (0, pad))).reshape(N_DEV, -1, PACK_W)

    rep_keys = early + late
    rep_parts = rows_by_device(jnp.concatenate([arrived[key] for key in rep_keys], axis=1))
    flat_rep = lambda tree: _as_rows(jnp.concatenate([piece(tree, key).reshape(-1) for key in rep_keys]))[None]
    rep_res = _adamw([rep_parts], flat_rep(W), flat_rep(M), flat_rep(V), name="adamw_replicated")
    sh_keys = early_sharded + late_sharded
    shard_parts = []
    for n in sh_keys:
        full = arrived[n].reshape((N_DEV,) + small_grads[n].shape)
        c = full.shape[-1] // N_DEV
        shard_parts.append(lax.dynamic_slice_in_dim(full, me * c, c, axis=full.ndim - 1).reshape(N_DEV, -1))
    sh_rows = rows_by_device(jnp.concatenate(shard_parts, axis=1))
    flat_sh = lambda tree: _as_rows(jnp.concatenate([tree[n].reshape(-1) for n in sh_keys]))[None]
    sh_res = _adamw([sh_rows], flat_sh(W), flat_sh(M), flat_sh(V), name="adamw_small_sharded")

    def unpack_replicated(rows):
        flat, vals, o = rows.reshape(-1), {}, 0
        for key in rep_keys:
            size = piece(W, key).size
            vals[key] = flat[o:o + size]
            o += size
        res = {}
        for n in replicated:
            if (n, None) in vals:
                res[n] = vals[(n, None)].reshape(W[n].shape)
            else:
                res[n] = jnp.stack([vals[(n, r)] for r in range(W[n].shape[0])]).reshape(W[n].shape)
        return res

    def unpack_sharded(rows):
        flat, res, o = rows.reshape(-1), {}, 0
        for n in sh_keys:
            res[n] = flat[o:o + W[n].size].reshape(W[n].shape)
            o += W[n].size
        return res

    results = []
    for idx in range(4):
        d = {n: out[n][idx] for n in big}
        d.update(unpack_replicated(rep_res[idx]))
        d.update(unpack_sharded(sh_res[idx]))
        results.append(d)

    loss = lax.psum(loss_part[0, 0], ("x", "y", "c"))
    grad_x = dx.reshape(x.shape)
    final = [loss, grad_x]
    for d in results:
        final.extend(d[n] for n in names)
    return tuple(final)
```

```python
import dataclasses
import functools

import jax
import jax.numpy as jnp
from jax import lax
from jax.experimental import pallas as pl
from jax.experimental.pallas import tpu as pltpu

F32 = jnp.float32
BF16 = jnp.bfloat16

N_DEV = 8
D_MODEL = 1024
D_FF = 4096
RMS_EPS = 1e-5
LN_EPS = 1e-5
CHUNK = 128
GM_GROUPS = 8
SSM_HEADS = 16
SSM_HEADDIM = 64
SSM_GROUPS = 4
SSM_STATE = 128
SSM_CONV = 4
CONV_DIM = 2048
IN_EVEN = 5136
REST_W = 3200
ATTN_HEADS = 16
ATTN_KV = 2
HEAD_DIM = 64
QKV_DIM = 1280
LANES = 128
HALO = 8
PACK_W = 1024

ADAM_LR = 0.001
ADAM_B1 = 0.9
ADAM_B2 = 0.999
ADAM_EPS = 1e-08
ADAM_WD = 0.01
ADAM_STEP = 10

VMEM_LIMIT_BYTES = 56 * 1024 * 1024


_NN = (((1,), (0,)), ((), ()))
_NT = (((1,), (1,)), ((), ()))
_TN = (((0,), (0,)), ((), ()))


def _dg(a, b, dims):
    return lax.dot_general(a.astype(BF16), b.astype(BF16), dims, preferred_element_type=F32)


@jax.custom_vjp
def _nn(a, b):
    return _dg(a, b, _NN)


@jax.custom_vjp
def _nt(a, b):
    return _dg(a, b, _NT)


@jax.custom_vjp
def _tn(a, b):
    return _dg(a, b, _TN)


_nn.defvjp(lambda a, b: (_dg(a, b, _NN), (a, b)), lambda r, g: (_nt(g, r[1]), _tn(r[0], g)))
_nt.defvjp(lambda a, b: (_dg(a, b, _NT), (a, b)), lambda r, g: (_nn(g, r[1]), _tn(g, r[0])))
_tn.defvjp(lambda a, b: (_dg(a, b, _TN), (a, b)), lambda r, g: (_nt(r[1], g), _nn(r[0], g)))


def _split3_dot(tri, x):
    x1 = x.astype(BF16)
    r1 = x - x1.astype(F32)
    x2 = r1.astype(BF16)
    x3 = (r1 - x2.astype(F32)).astype(BF16)
    t = tri.astype(BF16)
    dot = lambda p: lax.dot_general(t, p, _NN, preferred_element_type=F32)
    return dot(x1) + dot(x2) + dot(x3)


def _tri(lower):
    r = lax.broadcasted_iota(jnp.int32, (CHUNK, CHUNK), 0)
    c = lax.broadcasted_iota(jnp.int32, (CHUNK, CHUNK), 1)
    return jnp.where((r >= c) if lower else (r <= c), 1.0, 0.0).astype(F32)


@jax.custom_vjp
def _cumsum_rows(x):
    return _split3_dot(_tri(True), x)


_cumsum_rows.defvjp(lambda x: (_split3_dot(_tri(True), x), None), lambda _, g: (_split3_dot(_tri(False), g),))


def _sigmoid(x):
    return 1.0 / (1.0 + jnp.exp(-x))


def _silu(x):
    return x * _sigmoid(x)


def _softplus(x):
    return jnp.maximum(x, 0.0) + jnp.log(1.0 + jnp.exp(-jnp.abs(x)))


def _gelu_tanh(x):
    return 0.5 * x * (1.0 + jnp.tanh(0.7978845608028654 * (x + 0.044715 * (x * x * x))))


def _rmsnorm(x, g):
    return x * lax.rsqrt(jnp.mean(x * x, axis=-1, keepdims=True) + RMS_EPS) * g


def _gmlp_chunk(u, v, ln_g, ln_b, w_s, b_s):
    gu = _gelu_tanh(u)
    gv = _gelu_tanh(v)
    mu = jnp.mean(gv, axis=-1, keepdims=True)
    var = jnp.mean(jnp.square(gv - mu), axis=-1, keepdims=True)
    vn = (gv - mu) * lax.rsqrt(var + LN_EPS) * ln_g + ln_b
    r = lax.broadcasted_iota(jnp.int32, (CHUNK, CHUNK), 0)
    c = lax.broadcasted_iota(jnp.int32, (CHUNK, CHUNK), 1)
    causal = r >= c
    outs = []
    for g in range(GM_GROUPS):
        cols = slice(g * LANES, (g + 1) * LANES)
        mixed = _nn(jnp.where(causal, w_s[g], 0.0), vn[:, cols]) + b_s[g]
        outs.append(gu[:, cols] * mixed)
    return jnp.concatenate(outs, axis=1)


def _lane_pick(row, h):
    lane = lax.broadcasted_iota(jnp.int32, row.shape, 1)
    return jnp.sum(jnp.where(lane == h, row, 0.0), axis=1, keepdims=True)


def _col_pick(m, h):
    lane = lax.broadcasted_iota(jnp.int32, m.shape, 1)
    return jnp.sum(jnp.where(lane == h, m, 0.0), axis=1, keepdims=True)


def _row_pick(m, h):
    sub = lax.broadcasted_iota(jnp.int32, m.shape, 0)
    return jnp.sum(jnp.where(sub == h, m, 0.0), axis=0, keepdims=True)


_PAIRS = SSM_HEADS // 2


def _ssd_chunk(pre, z, dt_raw, h_prev, dt_bias, a_log, d_skip, norm_g):
    xbc = _silu(pre)
    dt = _softplus(dt_raw + dt_bias)
    da = dt * (-jnp.exp(a_log))
    a_cum = _cumsum_rows(da)
    a_cum_t = a_cum.T
    dt_t = dt.T
    r = lax.broadcasted_iota(jnp.int32, (CHUNK, CHUNK), 0)
    c = lax.broadcasted_iota(jnp.int32, (CHUNK, CHUNK), 1)
    causal = r >= c
    lane_lo = lax.broadcasted_iota(jnp.int32, (1, LANES), 1) < SSM_HEADDIM
    last_row = lax.broadcasted_iota(jnp.int32, (CHUNK, 1), 0) == CHUNK - 1
    ys, h_next = [], []
    for j in range(_PAIRS):
        g = j // 2
        xs = xbc[:, j * LANES:(j + 1) * LANES]
        bm = xbc[:, 1024 + g * SSM_STATE:1024 + (g + 1) * SSM_STATE]
        cm = xbc[:, 1536 + g * SSM_STATE:1536 + (g + 1) * SSM_STATE]
        cb = _nt(cm, bm)
        y_diag, to_end, e_cum, c_dec, d_row = [], [], [], [], []
        for h in (2 * j, 2 * j + 1):
            col = _col_pick(a_cum, h)
            row = _row_pick(a_cum_t, h)
            dt_col = _col_pick(dt, h)
            dt_row = _row_pick(dt_t, h)
            decay = jnp.exp(jnp.where(causal, col - row, -jnp.inf))
            y_diag.append(_nn(cb * decay * dt_row, xs))
            last = jnp.sum(jnp.where(last_row, col, 0.0), axis=0, keepdims=True)
            to_end.append(jnp.exp(last - col) * dt_col)
            e_cum.append(jnp.exp(col))
            c_dec.append(jnp.exp(last))
            d_row.append(_lane_pick(d_skip, h))
        pair = lambda lo_hi: jnp.where(lane_lo, lo_hi[0], lo_hi[1])
        states = _tn(bm, xs * pair(to_end))
        y_off = _nn(cm, h_prev[j]) * pair(e_cum)
        ys.append(pair(y_diag) + y_off + xs * pair(d_row))
        h_next.append(pair(c_dec) * h_prev[j] + states)
    y = jnp.concatenate(ys, axis=1) * _silu(z)
    width = D_MODEL // SSM_GROUPS
    y = jnp.concatenate(
        [_rmsnorm(y[:, g * width:(g + 1) * width], norm_g[:, g * width:(g + 1) * width]) for g in range(SSM_GROUPS)],
        axis=1)
    return y, tuple(h_next)


def _shift_down(prev8, x, k):
    if k == 0:
        return x
    win = jnp.concatenate([prev8, x], axis=0)
    return pltpu.roll(win, k, 0)[HALO:]


def _shift_up(x, next8, k):
    if k == 0:
        return x
    n = x.shape[0]
    win = jnp.concatenate([x, next8], axis=0)
    return pltpu.roll(win, n + HALO - k, 0)[:n]


def _conv_pre(prev8, x, w, b):
    out = b + x * w[SSM_CONV - 1:SSM_CONV]
    for i in range(SSM_CONV - 1):
        out = out + _shift_down(prev8, x, SSM_CONV - 1 - i) * w[i:i + 1]
    return out


@jax.custom_vjp
def _swap_halves(x):
    return pltpu.roll(x, HEAD_DIM, 1)


_swap_halves.defvjp(lambda x: (pltpu.roll(x, HEAD_DIM, 1), None), lambda _, g: (pltpu.roll(g, HEAD_DIM, 1),))

_PAIRS_PER_KV = ATTN_HEADS // ATTN_KV // 2


def _attn_pairs(q4, kv_prev, kv_cur, sink_lo, sink_hi, first, kv_head):
    kv = jnp.concatenate([kv_prev, kv_cur], axis=0)
    lane = lax.broadcasted_iota(jnp.int32, (1, LANES), 1)
    own = (lane >= HEAD_DIM * kv_head) & (lane < HEAD_DIM * (kv_head + 1))

    def placed(pair):
        mine = jnp.where(own, pair, 0.0)
        lo = mine if kv_head == 0 else _swap_halves(mine)
        return lo, _swap_halves(lo)

    k_lo, k_hi = placed(kv[:, :LANES])
    v_lo, v_hi = placed(kv[:, LANES:])
    out = None
    for k_e, v_e, sink in ((k_lo, v_lo, sink_lo), (k_hi, v_hi, sink_hi)):
        s = _nt(q4, k_e) * (HEAD_DIM ** -0.5)
        rows = lax.broadcasted_iota(jnp.int32, s.shape, 0) & (CHUNK - 1)
        cols = lax.broadcasted_iota(jnp.int32, s.shape, 1)
        valid = (cols <= rows + CHUNK) & (cols > rows) & (cols >= CHUNK * first.astype(jnp.int32))
        s = jnp.where(valid, s, -jnp.inf)
        m = lax.stop_gradient(jnp.maximum(jnp.max(s, axis=-1, keepdims=True), sink))
        p = jnp.exp(s - m)
        denom = jnp.sum(p, axis=-1, keepdims=True) + jnp.exp(sink - m)
        o = _nn(p, v_e) / denom
        out = o if out is None else out + o
    return out


N_CHIP = 4
N_CORE = 2
_OTHER_CHIPS = (2, 4, 6)


@dataclasses.dataclass
class _Move:
    kind: str
    src: jax.Array
    rows: tuple = None

    def dst_shape(self):
        s = self.src.shape
        shape = {"gather": (N_DEV,) + s, "gather_ici": (N_CHIP, N_CORE) + s, "gather_d2d": s,
                 "scatter_d2d": (N_CHIP,) + s[2:], "scatter_ici": s}[self.kind]
        if self.rows is not None:
            shape = (shape[0], self.rows[1]) + tuple(shape[2:])
        return jax.ShapeDtypeStruct(tuple(shape), self.src.dtype)


def _peer(x, y, c, k):
    return (1 - x if k & 4 else x, 1 - y if k & 2 else y, 1 - c if k & 1 else c)


def _move_copies(moves, srcs, dsts, send_sems, recv_sems, local_sems):
    x, y, c = lax.axis_index("x"), lax.axis_index("y"), lax.axis_index("c")
    chip = 2 * x + y
    me = 2 * chip + c
    sibling = (x, y, 1 - c)
    all_chips = pl.ds(0, N_CHIP)
    local, remote = [], []

    def push(n, k, src, dst, device):
        remote.append(pltpu.make_async_remote_copy(
            src_ref=src, dst_ref=dst, send_sem=send_sems.at[n, k], recv_sem=recv_sems.at[n, k],
            device_id=device, device_id_type=pl.DeviceIdType.MESH))

    for n, mv in enumerate(moves):
        s, d = srcs[n], dsts[n]
        if mv.kind == "gather":
            local.append(pltpu.make_async_copy(s, d.at[me], local_sems.at[n]))
            for k in range(1, N_DEV):
                push(n, k - 1, s, d.at[me], _peer(x, y, c, k))
        elif mv.kind == "gather_ici":
            local.append(pltpu.make_async_copy(s, d.at[chip, c], local_sems.at[n]))
            for k in _OTHER_CHIPS:
                push(n, k - 1, s, d.at[chip, c], _peer(x, y, c, k))
        elif mv.kind == "gather_d2d":
            push(n, 0, d.at[all_chips, c], d.at[all_chips, c], sibling)
        elif mv.kind == "scatter_d2d":
            push(n, 0, s.at[all_chips, 1 - c], d, sibling)
        else:
            assert mv.kind == "scatter_ici", mv.kind
            part = (lambda r: r) if mv.rows is None else (lambda r: r.at[pl.ds(mv.rows[0], mv.rows[1])])
            local.append(pltpu.make_async_copy(part(s.at[chip]), d.at[chip], local_sems.at[n]))
            for k in _OTHER_CHIPS:
                px, py, _ = _peer(x, y, c, k)
                push(n, k - 1, part(s.at[2 * px + py]), d.at[chip], (px, py, c))
    return local, remote


def _move_aliases(moves, n_in, n_out):
    return {n_in + n: n_out + n for n, mv in enumerate(moves) if mv.kind == "gather_d2d"}


def _pcall(body, *, name, grid, in_specs, out_specs, out_shape, scratch_shapes=(), semantics=(), moves=(),
           aliases=None):
    out_shape, out_specs = list(out_shape), list(out_specs)
    in_specs = list(in_specs)
    if not moves:
        call = pl.pallas_call(
            body, name=name, grid=grid, in_specs=in_specs, out_specs=out_specs, out_shape=out_shape,
            scratch_shapes=list(scratch_shapes), input_output_aliases=aliases or {},
            compiler_params=pltpu.CompilerParams(dimension_semantics=tuple(semantics),
                                                 vmem_limit_bytes=VMEM_LIMIT_BYTES))
        return (lambda *args: (list(call(*args)), []))
    n_in, n_out, n_scr, n_mv = len(in_specs), len(out_shape), len(scratch_shapes), len(moves)
    hbm = pl.BlockSpec(memory_space=pltpu.HBM)

    def carrier(*refs):
        ins, rest = refs[:n_in], refs[n_in:]
        srcs, rest = rest[:n_mv], rest[n_mv:]
        outs, rest = rest[:n_out], rest[n_out:]
        dsts, rest = rest[:n_mv], rest[n_mv:]
        scr, (send_sems, recv_sems, local_sems) = rest[:n_scr], rest[n_scr:]
        first = functools.reduce(jnp.logical_and, [pl.program_id(d) == 0 for d in range(len(grid))])
        last = functools.reduce(jnp.logical_and, [pl.program_id(d) == grid[d] - 1 for d in range(len(grid))])

        @pl.when(first)
        def _():
            local, remote = _move_copies(moves, srcs, dsts, send_sems, recv_sems, local_sems)
            for cp in local + remote:
                cp.start()

        body(*ins, *outs, *scr)

        @pl.when(last)
        def _():
            local, remote = _move_copies(moves, srcs, dsts, send_sems, recv_sems, local_sems)
            for cp in remote + local:
                cp.wait()

    call = pl.pallas_call(
        carrier, name=name, grid=grid,
        in_specs=in_specs + [hbm] * n_mv,
        out_specs=out_specs + [hbm] * n_mv,
        out_shape=out_shape + [mv.dst_shape() for mv in moves],
        scratch_shapes=list(scratch_shapes) + [pltpu.SemaphoreType.DMA((n_mv, N_DEV - 1)),
                                               pltpu.SemaphoreType.DMA((n_mv, N_DEV - 1)),
                                               pltpu.SemaphoreType.DMA((n_mv,))],
        input_output_aliases={**(aliases or {}), **_move_aliases(moves, n_in, n_out)},
        compiler_params=pltpu.CompilerParams(dimension_semantics=("arbitrary",) * len(grid),
                                             vmem_limit_bytes=VMEM_LIMIT_BYTES))

    def run(*args):
        res = list(call(*args, *[mv.src for mv in moves]))
        return res[:n_out], res[n_out:]

    return run


def _exchange(moves, *, name, then_d2d=()):
    n_mv, n_fwd = len(moves), len(then_d2d)
    hbm = pl.BlockSpec(memory_space=pltpu.HBM)
    second = [_Move("gather_d2d", moves[n].src) for n in then_d2d]

    def body(*refs):
        srcs, dsts, sems = refs[:n_mv], refs[n_mv:2 * n_mv], refs[2 * n_mv:]
        local, remote = _move_copies(moves, srcs, dsts, *sems[:3])
        for cp in local + remote:
            cp.start()
        for cp in remote + local:
            cp.wait()
        if second:
            landed = [dsts[n] for n in then_d2d]
            _, remote = _move_copies(second, landed, landed, sems[3], sems[4], None)
            for cp in remote:
                cp.start()
            for cp in remote:
                cp.wait()

    sems = [pltpu.SemaphoreType.DMA((n_mv, N_DEV - 1)), pltpu.SemaphoreType.DMA((n_mv, N_DEV - 1)),
            pltpu.SemaphoreType.DMA((n_mv,))]
    if second:
        sems += [pltpu.SemaphoreType.DMA((n_fwd, N_DEV - 1)), pltpu.SemaphoreType.DMA((n_fwd, N_DEV - 1))]
    return list(pl.pallas_call(
        body, name=name, in_specs=[hbm] * n_mv, out_specs=[hbm] * n_mv,
        out_shape=[mv.dst_shape() for mv in moves], scratch_shapes=sems,
    )(*[mv.src for mv in moves]))


TM = 512
FF_SHARD = D_FF // N_DEV


def _whole(a):
    nd = a.ndim
    return pl.BlockSpec(a.shape, lambda i: (0,) * nd)


def _rows(width, col=0):
    return pl.BlockSpec((TM, width), lambda i: (i, col))


def _acc_row(width):
    return pl.BlockSpec((1, width), lambda i: (0, 0))


def _unpack(res_landed, moves, n_out):
    res, landed = res_landed
    res = res[0] if n_out == 1 else res
    return (res, landed) if moves else res


def _norm_matmul(x, g, w, *, name, emit_y, moves=()):
    t, d = x.shape
    n = w.shape[1]

    def body(x_ref, g_ref, w_ref, *outs):
        y = _rmsnorm(x_ref[...], g_ref[...]).astype(BF16)
        if emit_y:
            outs[0][...] = y
        outs[-1][...] = lax.dot_general(y, w_ref[...], _NN, preferred_element_type=F32)

    shapes = ([jax.ShapeDtypeStruct((t, d), BF16)] if emit_y else []) + [jax.ShapeDtypeStruct((t, n), F32)]
    specs = ([_rows(d)] if emit_y else []) + [_rows(n)]
    return _unpack(_pcall(body, name=name, grid=(t // TM,), in_specs=[_rows(d), _acc_row(d), _whole(w)],
                          out_specs=specs, out_shape=shapes, semantics=("parallel",), moves=moves)(x, g, w),
                   moves, len(shapes))


def _residual_matmul(a, w, res, *, name, bias=None, norm_g=None, moves=()):
    t, k = a.shape
    n = w.shape[1]
    has_res, has_bias, has_norm = res is not None, bias is not None, norm_g is not None

    def body(a_ref, w_ref, *rest):
        rest = list(rest)
        res_ref = rest.pop(0) if has_res else None
        b_ref = rest.pop(0) if has_bias else None
        g_ref = rest.pop(0) if has_norm else None
        h = lax.dot_general(a_ref[...].astype(BF16), w_ref[...], _NN, preferred_element_type=F32)
        if has_res:
            h = h + res_ref[...]
        if has_bias:
            h = h + b_ref[...]
        rest[0][...] = h
        if has_norm:
            rest[1][...] = _rmsnorm(h, g_ref[...]).astype(BF16)

    rows_in = [res] if has_res else []
    extra = ([bias] if has_bias else []) + ([norm_g] if has_norm else [])
    shapes = [jax.ShapeDtypeStruct((t, n), F32)] + ([jax.ShapeDtypeStruct((t, n), BF16)] if has_norm else [])
    return _unpack(_pcall(body, name=name, grid=(t // TM,),
                          in_specs=[_rows(k), _whole(w)] + [_rows(n)] * len(rows_in) + [_acc_row(n)] * len(extra),
                          out_specs=[_rows(n)] * len(shapes), out_shape=shapes, semantics=("parallel",),
                          moves=moves)(a, w, *rows_in, *extra), moves, len(shapes))


def _mlp_up(y, w_cols, *, name, moves=()):
    t, d = y.shape

    def body(y_ref, w_ref, up_ref):
        yv = y_ref[...]
        for j in range(N_DEV):
            up_ref[:, j * FF_SHARD:(j + 1) * FF_SHARD] = lax.dot_general(
                yv, w_ref[j], _NN, preferred_element_type=F32).astype(up_ref.dtype)

    return _unpack(_pcall(body, name=name, grid=(t // TM,), in_specs=[_rows(d), _whole(w_cols)],
                          out_specs=[_rows(D_FF)], out_shape=[jax.ShapeDtypeStruct((t, D_FF), BF16)],
                          semantics=("parallel",), moves=moves)(y, w_cols), moves, 1)


def _sq_relu(u):
    return jnp.square(jnp.maximum(u.astype(F32), 0.0))


def _mlp_down(up, w_rows, res, *, name, norm_g=None, moves=()):
    t = up.shape[0]
    has_norm = norm_g is not None

    def body(up_ref, w_ref, res_ref, *rest):
        h = res_ref[...]
        for j in range(N_DEV):
            act = _sq_relu(up_ref[:, j * FF_SHARD:(j + 1) * FF_SHARD]).astype(BF16)
            h = h + lax.dot_general(act, w_ref[j], _NN, preferred_element_type=F32)
        if has_norm:
            g_ref, h_ref, y_ref = rest
            y_ref[...] = _rmsnorm(h, g_ref[...]).astype(BF16)
        else:
            (h_ref,) = rest
        h_ref[...] = h

    shapes = [jax.ShapeDtypeStruct((t, D_MODEL), F32)] + ([jax.ShapeDtypeStruct((t, D_MODEL), BF16)] if has_norm else [])
    return _unpack(_pcall(body, name=name, grid=(t // TM,),
                          in_specs=[_rows(D_FF), _whole(w_rows), _rows(D_MODEL)] + ([_acc_row(D_MODEL)] if has_norm else []),
                          out_specs=[_rows(D_MODEL)] * len(shapes), out_shape=shapes, semantics=("parallel",),
                          moves=moves)(up, w_rows, res, *([norm_g] if has_norm else [])), moves, len(shapes))


def _mlp_down_dx(dh, w_rows, up, *, name, moves=()):
    t = up.shape[0]

    def body(dh_ref, w_ref, up_ref, o_ref):
        dhv = dh_ref[...]
        for j in range(N_DEV):
            cols = slice(j * FF_SHARD, (j + 1) * FF_SHARD)
            d_act = lax.dot_general(dhv, w_ref[j], _NT, preferred_element_type=F32)
            o_ref[:, cols] = (d_act * (2.0 * jnp.maximum(up_ref[:, cols].astype(F32), 0.0))).astype(o_ref.dtype)

    return _unpack(_pcall(body, name=name, grid=(t // TM,),
                          in_specs=[_rows(D_MODEL), _whole(w_rows), _rows(D_FF)],
                          out_specs=[_rows(D_FF)], out_shape=[jax.ShapeDtypeStruct((t, D_FF), BF16)],
                          semantics=("parallel",), moves=moves)(dh, w_rows, up), moves, 1)


def _dw_by_cols(x, dy, *, name, tn, by_device=False, moves=()):
    t, k = x.shape
    n = dy.shape[1]
    assert n % tn == 0, (name, n, tn)

    def body(x_ref, dy_ref, o_ref):
        o_ref[...] = lax.dot_general(x_ref[...].astype(BF16), dy_ref[...].astype(BF16), _TN,
                                     preferred_element_type=F32).astype(o_ref.dtype)

    if by_device:
        out_spec, out_shape = pl.BlockSpec((None, k, tn), lambda j: (j, 0, 0)), (n // tn, k, tn)
    else:
        out_spec, out_shape = pl.BlockSpec((k, tn), lambda j: (0, j)), (k, n)
    return _unpack(_pcall(body, name=name, grid=(n // tn,),
                          in_specs=[_whole(x), pl.BlockSpec((t, tn), lambda j: (0, j))],
                          out_specs=[out_spec], out_shape=[jax.ShapeDtypeStruct(out_shape, BF16)],
                          semantics=("parallel",), moves=moves)(x, dy), moves, 1)


def _dw_by_rows(x, dy, *, name, tk, square_relu=False, moves=()):
    t, k = x.shape
    n = dy.shape[1]
    assert k % tk == 0, (name, k, tk)

    def body(x_ref, dy_ref, o_ref):
        xv = _sq_relu(x_ref[...]) if square_relu else x_ref[...]
        o_ref[...] = lax.dot_general(xv.astype(BF16), dy_ref[...].astype(BF16), _TN,
                                     preferred_element_type=F32).astype(o_ref.dtype)

    return _unpack(_pcall(body, name=name, grid=(k // tk,),
                          in_specs=[pl.BlockSpec((t, tk), lambda j: (0, j)), _whole(dy)],
                          out_specs=[pl.BlockSpec((tk, n), lambda j: (j, 0))],
                          out_shape=[jax.ShapeDtypeStruct((k, n), BF16)],
                          semantics=("parallel",), moves=moves)(x, dy), moves, 1)


def _dx(dy, w, *, name, partial=None, moves=()):
    t, k = dy.shape
    n = w.shape[0]
    has_partial = partial is not None

    def body(dy_ref, w_ref, *rest):
        out = lax.dot_general(dy_ref[...].astype(BF16), w_ref[...], _NT, preferred_element_type=F32)
        if has_partial:
            out = out + rest[0][...]
        rest[-1][...] = out

    return _unpack(_pcall(body, name=name, grid=(t // TM,),
                          in_specs=[_rows(k), _whole(w)] + ([_rows(n)] if has_partial else []),
                          out_specs=[_rows(n)], out_shape=[jax.ShapeDtypeStruct((t, n), F32)],
                          semantics=("parallel",), moves=moves)(dy, w, *([partial] if has_partial else [])),
                   moves, 1)


def _dx_norm(dy, w, h, g, dres, *, name, partial=None, by_device_cols=False, moves=()):
    t, k = dy.shape
    d = h.shape[1]
    has_partial = partial is not None

    def body(dy_ref, w_ref, h_ref, g_ref, dres_ref, *rest):
        if by_device_cols:
            kc = k // N_DEV
            d_y = jnp.zeros((TM, d), F32)
            for j in range(N_DEV):
                d_y = d_y + lax.dot_general(dy_ref[:, j * kc:(j + 1) * kc].astype(BF16), w_ref[j], _NT,
                                            preferred_element_type=F32)
        else:
            d_y = lax.dot_general(dy_ref[...].astype(BF16), w_ref[...], _NT, preferred_element_type=F32)
        if has_partial:
            d_y = d_y + rest[0][...]
        dh_ref, dhb_ref, dg_ref, cs_ref = rest[-4:]
        _, vjp = jax.vjp(_rmsnorm, h_ref[...], g_ref[...])
        dh, dg = vjp(d_y)
        dh = dh + dres_ref[...]
        dh_ref[...] = dh
        dhb_ref[...] = dh.astype(BF16)

        @pl.when(pl.program_id(0) == 0)
        def _():
            dg_ref[...] = jnp.zeros_like(dg_ref)
            cs_ref[...] = jnp.zeros_like(cs_ref)

        dg_ref[...] += dg
        cs_ref[...] += jnp.sum(dh, axis=0, keepdims=True)

    shapes = [jax.ShapeDtypeStruct((t, d), F32), jax.ShapeDtypeStruct((t, d), BF16),
              jax.ShapeDtypeStruct((1, d), F32), jax.ShapeDtypeStruct((1, d), F32)]
    return _unpack(_pcall(body, name=name, grid=(t // TM,),
                          in_specs=[_rows(k), _whole(w), _rows(d), _acc_row(d), _rows(d)]
                          + ([_rows(d)] if has_partial else []),
                          out_specs=[_rows(d), _rows(d), _acc_row(d), _acc_row(d)], out_shape=shapes,
                          semantics=("arbitrary",), moves=moves)(dy, w, h, g, dres, *([partial] if has_partial else [])),
                   moves, 4)


def _pair_add(by_core, theirs, core, *, name, tb=512):
    n_chip, _, r, c = by_core.shape
    tb = min(tb, r)
    assert r % tb == 0, (name, r, tb)

    def body(core_ref, a_ref, b_ref, o_ref):
        del core_ref
        o_ref[...] = (a_ref[...].astype(F32) + b_ref[...].astype(F32)).astype(o_ref.dtype)

    blk = pl.BlockSpec((None, tb, c), lambda ch, i, core_ref: (ch, i, 0))
    return pl.pallas_call(
        body, name=name,
        grid_spec=pltpu.PrefetchScalarGridSpec(
            num_scalar_prefetch=1, grid=(n_chip, r // tb),
            in_specs=[pl.BlockSpec((None, None, tb, c), lambda ch, i, core_ref: (ch, core_ref[0], i, 0)), blk],
            out_specs=blk),
        out_shape=jax.ShapeDtypeStruct((n_chip, r, c), by_core.dtype),
        compiler_params=pltpu.CompilerParams(dimension_semantics=("parallel", "parallel"),
                                             vmem_limit_bytes=VMEM_LIMIT_BYTES),
    )(core, by_core, theirs)


def _colsum(a, *, name, tb=512):
    t, d = a.shape

    def body(a_ref, o_ref):
        @pl.when(pl.program_id(0) == 0)
        def _():
            o_ref[...] = jnp.zeros_like(o_ref)

        o_ref[...] += jnp.sum(a_ref[...].astype(F32), axis=0, keepdims=True)

    return _pcall(
        body, name=name, grid=(t // tb,),
        in_specs=[pl.BlockSpec((tb, d), lambda i: (i, 0))],
        out_specs=[pl.BlockSpec((1, d), lambda i: (0, 0))],
        out_shape=[jax.ShapeDtypeStruct((1, d), F32)],
        semantics=("arbitrary",),
    )(a)[0][0]


def _mlp_down_loss(up, w_rows, res, g, target, *, name):
    t, d = res.shape

    def body(up_ref, w_ref, res_ref, g_ref, tgt_ref, loss_ref, dh_ref, dhb_ref, dg_ref):
        h = res_ref[...]
        for j in range(N_DEV):
            act = _sq_relu(up_ref[:, j * FF_SHARD:(j + 1) * FF_SHARD]).astype(BF16)
            h = h + lax.dot_general(act, w_ref[j], _NN, preferred_element_type=F32)

        def f(hh, gg):
            err = jnp.square(_rmsnorm(hh, gg) - tgt_ref[...])
            return 0.5 * jnp.sum(jnp.mean(err, axis=-1, keepdims=True), axis=0, keepdims=True)

        val, vjp = jax.vjp(f, h, g_ref[...])
        dh, dg = vjp(jnp.ones((1, 1), F32))
        dh_ref[...] = dh
        dhb_ref[...] = dh.astype(BF16)

        @pl.when(pl.program_id(0) == 0)
        def _():
            loss_ref[...] = jnp.zeros_like(loss_ref)
            dg_ref[...] = jnp.zeros_like(dg_ref)

        loss_ref[...] += val
        dg_ref[...] += dg

    return _pcall(
        body, name=name, grid=(t // TM,),
        in_specs=[_rows(D_FF), _whole(w_rows), _rows(d), _acc_row(d), _rows(d)],
        out_specs=[pl.BlockSpec((8, LANES), lambda i: (0, 0)), _rows(d), _rows(d), _acc_row(d)],
        out_shape=[jax.ShapeDtypeStruct((8, LANES), F32), jax.ShapeDtypeStruct((t, d), F32),
                   jax.ShapeDtypeStruct((t, d), BF16), jax.ShapeDtypeStruct((1, d), F32)],
        semantics=("arbitrary",),
    )(up, w_rows, res, g, target)[0]


def _gmlp_fwd(proj_uv, ln_g, ln_b, w_s, b_s, *, name, moves=()):
    t = proj_uv.shape[0]
    w = D_MODEL

    def body(u_ref, v_ref, g_ref, b_ref, w_ref, bs_ref, o_ref):
        o_ref[...] = _gmlp_chunk(u_ref[...], v_ref[...], g_ref[...], b_ref[...], w_ref[...],
                                 bs_ref[...]).astype(o_ref.dtype)

    row = pl.BlockSpec((1, w), lambda i: (0, 0))
    res, landed = _pcall(
        body, name=name, grid=(t // CHUNK,),
        in_specs=[pl.BlockSpec((CHUNK, w), lambda i: (i, 0)), pl.BlockSpec((CHUNK, w), lambda i: (i, 1)), row, row,
                  pl.BlockSpec((GM_GROUPS, CHUNK, CHUNK), lambda i: (0, 0, 0)),
                  pl.BlockSpec((GM_GROUPS, CHUNK, 1), lambda i: (0, 0, 0))],
        out_specs=[pl.BlockSpec((CHUNK, w), lambda i: (i, 0))],
        out_shape=[jax.ShapeDtypeStruct((t, 2 * w), BF16)],
        semantics=("parallel",), moves=moves,
    )(proj_uv, proj_uv, ln_g, ln_b, w_s, b_s)
    return (res[0], landed) if moves else res[0]


def _gmlp_bwd(proj_uv, d_mix, ln_g, ln_b, w_s, b_s, *, name, moves=()):
    t = proj_uv.shape[0]
    w = D_MODEL

    def body(u_ref, v_ref, da_ref, g_ref, b_ref, w_ref, bs_ref, duv_ref, dg_ref, db_ref, dw_ref, dbs_ref):
        _, vjp = jax.vjp(_gmlp_chunk, u_ref[...], v_ref[...], g_ref[...], b_ref[...], w_ref[...], bs_ref[...])
        du, dv, dg, db, dw, dbs = vjp(da_ref[...])
        duv_ref[:, :w] = du.astype(duv_ref.dtype)
        duv_ref[:, w:] = dv.astype(duv_ref.dtype)

        @pl.when(pl.program_id(0) == 0)
        def _():
            dg_ref[...] = jnp.zeros_like(dg_ref)
            db_ref[...] = jnp.zeros_like(db_ref)
            dw_ref[...] = jnp.zeros_like(dw_ref)
            dbs_ref[...] = jnp.zeros_like(dbs_ref)

        dg_ref[...] += dg
        db_ref[...] += db
        dw_ref[...] += dw
        dbs_ref[...] += dbs

    row = pl.BlockSpec((1, w), lambda i: (0, 0))
    ws = pl.BlockSpec((GM_GROUPS, CHUNK, CHUNK), lambda i: (0, 0, 0))
    bs = pl.BlockSpec((GM_GROUPS, CHUNK, 1), lambda i: (0, 0, 0))
    res, landed = _pcall(
        body, name=name, grid=(t // CHUNK,),
        in_specs=[pl.BlockSpec((CHUNK, w), lambda i: (i, 0)), pl.BlockSpec((CHUNK, w), lambda i: (i, 1)),
                  pl.BlockSpec((CHUNK, w), lambda i: (i, 0)), row, row, ws, bs],
        out_specs=[pl.BlockSpec((CHUNK, 2 * w), lambda i: (i, 0)), row, row, ws, bs],
        out_shape=[jax.ShapeDtypeStruct((t, 2 * w), BF16), jax.ShapeDtypeStruct((1, w), F32),
                   jax.ShapeDtypeStruct((1, w), F32), jax.ShapeDtypeStruct((GM_GROUPS, CHUNK, CHUNK), F32),
                   jax.ShapeDtypeStruct((GM_GROUPS, CHUNK, 1), F32)],
        semantics=("arbitrary",), moves=moves,
    )(proj_uv, proj_uv, d_mix, ln_g, ln_b, w_s, b_s)
    return (res, landed) if moves else res


_HALO_PER_CHUNK = CHUNK // HALO
_DT_BLOCK = (CONV_DIM + D_MODEL) // LANES


def _ssd_fwd(proj_rest, mix, conv_w, conv_b, dt_bias, a_log, d_skip, norm_g, *, name, moves=()):
    t = proj_rest.shape[0]
    nc = t // CHUNK

    def body(x_ref, prev_ref, z_ref, dt_ref, mix_ref, cw_ref, cb_ref, dtb_ref, al_ref, ds_ref, ng_ref, y_ref, hs_ref,
             pre_ref, h_scr):
        del mix_ref
        i = pl.program_id(0)

        @pl.when(i == 0)
        def _():
            h_scr[...] = jnp.zeros_like(h_scr)

        prev8 = jnp.where(i == 0, 0.0, prev_ref[...])
        pre = _conv_pre(prev8, x_ref[...], cw_ref[...], cb_ref[...])
        pre_ref[...] = pre
        hs_ref[0] = h_scr[...]
        h_prev = tuple(h_scr[j] for j in range(_PAIRS))
        y, h_next = _ssd_chunk(pre, z_ref[...], dt_ref[...], h_prev, dtb_ref[...], al_ref[...], ds_ref[...],
                               ng_ref[...])
        y_ref[...] = y.astype(y_ref.dtype)
        for j in range(_PAIRS):
            h_scr[j] = h_next[j]

    small = pl.BlockSpec((1, LANES), lambda i: (0, 0))
    res, landed = _pcall(
        body, name=name, grid=(nc,),
        in_specs=[pl.BlockSpec((CHUNK, CONV_DIM), lambda i: (i, 0)),
                  pl.BlockSpec((HALO, CONV_DIM), lambda i: (jnp.maximum(i * _HALO_PER_CHUNK - 1, 0), 0)),
                  pl.BlockSpec((CHUNK, D_MODEL), lambda i: (i, CONV_DIM // D_MODEL)),
                  pl.BlockSpec((CHUNK, LANES), lambda i: (i, _DT_BLOCK)),
                  pl.BlockSpec(memory_space=pl.ANY),
                  pl.BlockSpec((SSM_CONV, CONV_DIM), lambda i: (0, 0)),
                  pl.BlockSpec((1, CONV_DIM), lambda i: (0, 0)),
                  small, small, small, pl.BlockSpec((1, D_MODEL), lambda i: (0, 0))],
        out_specs=[pl.BlockSpec((CHUNK, D_MODEL), lambda i: (i, 1)),
                   pl.BlockSpec((1, _PAIRS, SSM_STATE, LANES), lambda i: (i, 0, 0, 0)),
                   pl.BlockSpec((CHUNK, CONV_DIM), lambda i: (i, 0))],
        out_shape=[jax.ShapeDtypeStruct((t, 2 * D_MODEL), BF16),
                   jax.ShapeDtypeStruct((nc, _PAIRS, SSM_STATE, LANES), F32),
                   jax.ShapeDtypeStruct((t, CONV_DIM), F32)],
        scratch_shapes=[pltpu.VMEM((_PAIRS, SSM_STATE, LANES), F32)],
        semantics=("arbitrary",), moves=moves, aliases={4: 0},
    )(proj_rest, proj_rest, proj_rest, proj_rest, mix, conv_w, conv_b, dt_bias, a_log, d_skip, norm_g)
    return (res, landed) if moves else res


def _ssd_bwd(proj_rest, pre, h_states, d_mix, dt_bias, a_log, d_skip, norm_g, *, name, moves=()):
    t = proj_rest.shape[0]
    nc = t // CHUNK

    def body(pre_ref, z_ref, dt_ref, hs_ref, dy_ref, dtb_ref, al_ref, ds_ref, ng_ref,
             dpre_ref, dz_ref, ddt_ref, ddtb_ref, dal_ref, dds_ref, dng_ref, dh_scr):
        i = pl.program_id(0)

        @pl.when(i == 0)
        def _():
            dh_scr[...] = jnp.zeros_like(dh_scr)
            ddtb_ref[...] = jnp.zeros_like(ddtb_ref)
            dal_ref[...] = jnp.zeros_like(dal_ref)
            dds_ref[...] = jnp.zeros_like(dds_ref)
            dng_ref[...] = jnp.zeros_like(dng_ref)

        h_prev = tuple(hs_ref[0, j] for j in range(_PAIRS))
        _, vjp = jax.vjp(_ssd_chunk, pre_ref[...], z_ref[...], dt_ref[...], h_prev, dtb_ref[...], al_ref[...],
                         ds_ref[...], ng_ref[...])
        dpre, dz, ddt, dh_prev, ddtb, dal, dds, dng = vjp((dy_ref[...], tuple(dh_scr[j] for j in range(_PAIRS))))
        dpre_ref[...] = dpre
        dz_ref[...] = dz
        ddt_ref[...] = ddt
        for j in range(_PAIRS):
            dh_scr[j] = dh_prev[j]
        ddtb_ref[...] += ddtb
        dal_ref[...] += dal
        dds_ref[...] += dds
        dng_ref[...] += dng

    rev = lambda i: nc - 1 - i
    small = pl.BlockSpec((1, LANES), lambda i: (0, 0))
    wide = pl.BlockSpec((1, D_MODEL), lambda i: (0, 0))
    res, landed = _pcall(
        body, name=name, grid=(nc,),
        in_specs=[pl.BlockSpec((CHUNK, CONV_DIM), lambda i: (rev(i), 0)),
                  pl.BlockSpec((CHUNK, D_MODEL), lambda i: (rev(i), CONV_DIM // D_MODEL)),
                  pl.BlockSpec((CHUNK, LANES), lambda i: (rev(i), _DT_BLOCK)),
                  pl.BlockSpec((1, _PAIRS, SSM_STATE, LANES), lambda i: (rev(i), 0, 0, 0)),
                  pl.BlockSpec((CHUNK, D_MODEL), lambda i: (rev(i), 1)),
                  small, small, small, wide],
        out_specs=[pl.BlockSpec((CHUNK, CONV_DIM), lambda i: (rev(i), 0)),
                   pl.BlockSpec((CHUNK, D_MODEL), lambda i: (rev(i), 0)),
                   pl.BlockSpec((CHUNK, LANES), lambda i: (rev(i), 0)),
                   small, small, small, wide],
        out_shape=[jax.ShapeDtypeStruct((t, CONV_DIM), F32), jax.ShapeDtypeStruct((t, D_MODEL), F32),
                   jax.ShapeDtypeStruct((t, LANES), F32),
                   jax.ShapeDtypeStruct((1, LANES), F32), jax.ShapeDtypeStruct((1, LANES), F32),
                   jax.ShapeDtypeStruct((1, LANES), F32), jax.ShapeDtypeStruct((1, D_MODEL), F32)],
        scratch_shapes=[pltpu.VMEM((_PAIRS, SSM_STATE, LANES), F32)],
        semantics=("arbitrary",), moves=moves,
    )(pre, proj_rest, proj_rest, h_states, d_mix, dt_bias, a_log, d_skip, norm_g)
    return (res, landed) if moves else res


def _conv_bwd(proj_rest, dpre, dz, ddt, conv_w, *, name, tb=256, moves=()):
    t = proj_rest.shape[0]
    nb = t // tb
    per = tb // HALO

    def body(x_ref, prev_ref, dpre_ref, next_ref, dz_ref, ddt_ref, cw_ref, drest_ref, dcw_ref, dcb_ref):
        i = pl.program_id(0)

        @pl.when(i == 0)
        def _():
            dcw_ref[...] = jnp.zeros_like(dcw_ref)
            dcb_ref[...] = jnp.zeros_like(dcb_ref)

        x = x_ref[...]
        dp = dpre_ref[...]
        w = cw_ref[...]
        prev8 = jnp.where(i == 0, 0.0, prev_ref[...])
        next8 = jnp.where(i == nb - 1, 0.0, next_ref[...])
        dx = dp * w[SSM_CONV - 1:SSM_CONV]
        for j in range(SSM_CONV - 1):
            dx = dx + _shift_up(dp, next8, SSM_CONV - 1 - j) * w[j:j + 1]
        drest_ref[:, :CONV_DIM] = dx.astype(drest_ref.dtype)
        drest_ref[:, CONV_DIM:CONV_DIM + D_MODEL] = dz_ref[...].astype(drest_ref.dtype)
        drest_ref[:, CONV_DIM + D_MODEL:] = ddt_ref[...].astype(drest_ref.dtype)
        for j in range(SSM_CONV):
            dcw_ref[j:j + 1, :] += jnp.sum(dp * _shift_down(prev8, x, SSM_CONV - 1 - j), axis=0, keepdims=True)
        dcb_ref[...] += jnp.sum(dp, axis=0, keepdims=True)

    res, landed = _pcall(
        body, name=name, grid=(nb,),
        in_specs=[pl.BlockSpec((tb, CONV_DIM), lambda i: (i, 0)),
                  pl.BlockSpec((HALO, CONV_DIM), lambda i: (jnp.maximum(i * per - 1, 0), 0)),
                  pl.BlockSpec((tb, CONV_DIM), lambda i: (i, 0)),
                  pl.BlockSpec((HALO, CONV_DIM), lambda i: (jnp.minimum((i + 1) * per, nb * per - 1), 0)),
                  pl.BlockSpec((tb, D_MODEL), lambda i: (i, 0)),
                  pl.BlockSpec((tb, LANES), lambda i: (i, 0)),
                  pl.BlockSpec((SSM_CONV, CONV_DIM), lambda i: (0, 0))],
        out_specs=[pl.BlockSpec((tb, REST_W), lambda i: (i, 0)),
                   pl.BlockSpec((SSM_CONV, CONV_DIM), lambda i: (0, 0)),
                   pl.BlockSpec((1, CONV_DIM), lambda i: (0, 0))],
        out_shape=[jax.ShapeDtypeStruct((t, REST_W), BF16), jax.ShapeDtypeStruct((SSM_CONV, CONV_DIM), F32),
                   jax.ShapeDtypeStruct((1, CONV_DIM), F32)],
        semantics=("arbitrary",), moves=moves,
    )(proj_rest, proj_rest, dpre, dpre, dz, ddt, conv_w)
    return (res, landed) if moves else res


_KV_BLOCK = D_MODEL // (2 * LANES)
_SINK_ROWS = _PAIRS_PER_KV * CHUNK


def _stack_pairs(ref, kv_head):
    base = kv_head * _PAIRS_PER_KV
    return jnp.concatenate([ref[:, (base + p) * LANES:(base + p + 1) * LANES] for p in range(_PAIRS_PER_KV)], axis=0)


def _attn_fwd(qkv, sinks, *, name, moves=()):
    t = qkv.shape[0]
    nb = t // CHUNK

    def body(q_ref, kvp_ref, kvc_ref, s_ref, o_ref):
        first = pl.program_id(0) == 0
        for j in range(ATTN_KV):
            o = _attn_pairs(_stack_pairs(q_ref, j), kvp_ref[...], kvc_ref[...], s_ref[j, 0], s_ref[j, 1], first, j)
            for p in range(_PAIRS_PER_KV):
                col = (j * _PAIRS_PER_KV + p) * LANES
                o_ref[:, col:col + LANES] = o[p * CHUNK:(p + 1) * CHUNK].astype(o_ref.dtype)

    return _unpack(_pcall(
        body, name=name, grid=(nb,),
        in_specs=[pl.BlockSpec((CHUNK, D_MODEL), lambda i: (i, 0)),
                  pl.BlockSpec((CHUNK, 2 * LANES), lambda i: (jnp.maximum(i - 1, 0), _KV_BLOCK)),
                  pl.BlockSpec((CHUNK, 2 * LANES), lambda i: (i, _KV_BLOCK)),
                  pl.BlockSpec((ATTN_KV, 2, _SINK_ROWS, 1), lambda i: (0, 0, 0, 0))],
        out_specs=[pl.BlockSpec((CHUNK, D_MODEL), lambda i: (i, 0))],
        out_shape=[jax.ShapeDtypeStruct((t, D_MODEL), BF16)],
        semantics=("parallel",), moves=moves,
    )(qkv, qkv, qkv, sinks), moves, 1)


def _attn_bwd(qkv, sinks, d_o, *, name, moves=()):
    t = qkv.shape[0]
    nb = t // CHUNK

    def body(q_ref, kvp_ref, kvc_ref, s_ref, do_ref, dqkv_ref, ds_ref, dkv_scr):
        i = pl.program_id(0)
        first = i == nb - 1

        @pl.when(i == 0)
        def _():
            dkv_scr[...] = jnp.zeros_like(dkv_scr)
            ds_ref[...] = jnp.zeros_like(ds_ref)

        dkv_cur = dkv_scr[...]
        dkv_prev = jnp.zeros_like(dkv_cur)
        for j in range(ATTN_KV):
            _, vjp = jax.vjp(functools.partial(_attn_pairs, first=first, kv_head=j), _stack_pairs(q_ref, j),
                             kvp_ref[...], kvc_ref[...], s_ref[j, 0], s_ref[j, 1])
            dq4, dkvp, dkvc, ds_lo, ds_hi = vjp(_stack_pairs(do_ref, j))
            for p in range(_PAIRS_PER_KV):
                col = (j * _PAIRS_PER_KV + p) * LANES
                dqkv_ref[:, col:col + LANES] = dq4[p * CHUNK:(p + 1) * CHUNK]
            dkv_cur = dkv_cur + dkvc
            dkv_prev = dkv_prev + dkvp
            ds_ref[j, 0] += ds_lo
            ds_ref[j, 1] += ds_hi
        dqkv_ref[:, D_MODEL:] = dkv_cur
        dkv_scr[...] = dkv_prev

    cur = lambda i: (nb - 1 - i, 0)
    sk = pl.BlockSpec((ATTN_KV, 2, _SINK_ROWS, 1), lambda i: (0, 0, 0, 0))
    res, landed = _pcall(
        body, name=name, grid=(nb,),
        in_specs=[pl.BlockSpec((CHUNK, D_MODEL), cur),
                  pl.BlockSpec((CHUNK, 2 * LANES), lambda i: (jnp.maximum(nb - 2 - i, 0), _KV_BLOCK)),
                  pl.BlockSpec((CHUNK, 2 * LANES), lambda i: (nb - 1 - i, _KV_BLOCK)),
                  sk, pl.BlockSpec((CHUNK, D_MODEL), cur)],
        out_specs=[pl.BlockSpec((CHUNK, QKV_DIM), cur), sk],
        out_shape=[jax.ShapeDtypeStruct((t, QKV_DIM), F32), jax.ShapeDtypeStruct((ATTN_KV, 2, _SINK_ROWS, 1), F32)],
        scratch_shapes=[pltpu.VMEM((CHUNK, 2 * LANES), F32)],
        semantics=("arbitrary",), moves=moves,
    )(qkv, qkv, qkv, sinks, d_o)
    return (res, landed) if moves else res


def _adamw(parts, w, m, v, *, name, tb=512, moves=()):
    layers, r, c = w.shape
    n = parts[0].shape[0]
    tb = min(tb, r)
    assert r % tb == 0 and len(parts) == layers, (name, r, tb)
    nb = r // tb

    def body(*refs):
        p_refs = refs[:layers]
        w_ref, m_ref, v_ref, g_ref, d_ref, nm_ref, nv_ref = refs[layers:]
        for layer in range(layers):
            @pl.when(pl.program_id(0) == layer)
            def _(p_ref=p_refs[layer]):
                g = p_ref[0].astype(F32)
                for s in range(1, n):
                    g = g + p_ref[s].astype(F32)
                m_new = ADAM_B1 * m_ref[...] + (1.0 - ADAM_B1) * g
                v_new = ADAM_B2 * v_ref[...] + (1.0 - ADAM_B2) * jnp.square(g)
                m_hat = m_new / (1.0 - ADAM_B1 ** ADAM_STEP)
                v_hat = v_new / (1.0 - ADAM_B2 ** ADAM_STEP)
                g_ref[...] = g
                d_ref[...] = -ADAM_LR * (m_hat / (jnp.sqrt(v_hat) + ADAM_EPS) + ADAM_WD * w_ref[...])
                nm_ref[...] = m_new
                nv_ref[...] = v_new

    part_spec = lambda layer: pl.BlockSpec(
        (n, tb, c), lambda l, i: (0, jnp.clip(i + (l - layer) * nb, 0, nb - 1), 0))
    blk = pl.BlockSpec((None, tb, c), lambda l, i: (l, i, 0))
    res, landed = _pcall(
        body, name=name, grid=(layers, nb),
        in_specs=[part_spec(layer) for layer in range(layers)] + [blk, blk, blk],
        out_specs=[blk] * 4,
        out_shape=[jax.ShapeDtypeStruct((layers, r, c), F32)] * 4,
        semantics=("arbitrary", "arbitrary"), moves=moves,
    )(*parts, w, m, v)
    return (res, landed) if moves else res


def _as_rows(a):
    flat = a.reshape(-1)
    pad = (-flat.shape[0]) % PACK_W
    if pad:
        flat = jnp.pad(flat, (0, pad))
    return flat.reshape(-1, PACK_W)


def _cols_from_shards(g):
    return jnp.transpose(g, (1, 0, 2)).reshape(g.shape[1], -1)


def _cols_to_shards(a):
    return jnp.transpose(a.reshape(a.shape[0], N_DEV, -1), (1, 0, 2))


def _shard_cols(shards, lo, hi):
    c = shards.shape[2]
    pieces = []
    for j in range(shards.shape[0]):
        a, b = max(lo, j * c), min(hi, (j + 1) * c)
        if a < b:
            pieces.append(shards[j, :, a - j * c:b - j * c])
    return pieces


def _cols_of(sources, lo, hi):
    pieces = []
    for arr, col0, first, last in sources:
        a, b = max(lo, first), min(hi, last)
        if a < b:
            pieces.append(arr[:, col0 + a - first:col0 + b - first])
    return pieces


def _pad_lanes(a):
    return jnp.pad(a, ((0, 0), (0, LANES - a.shape[1])))


def kernel(x, norm_mix_g, norm_mlp_g, final_norm_g, w_in_even, w_out_even, gm_ln_g, gm_ln_b, gm_w_s, gm_b_s, ssm_conv_w, ssm_conv_b, ssm_dt_bias, ssm_a_log, ssm_d, ssm_norm_g, w_qkv, b_qkv, w_o, b_o, attn_sinks, w_up, w_down, loss_target, m_norm_mix_g, m_norm_mlp_g, m_final_norm_g, m_w_in_even, m_w_out_even, m_gm_ln_g, m_gm_ln_b, m_gm_w_s, m_gm_b_s, m_ssm_conv_w, m_ssm_conv_b, m_ssm_dt_bias, m_ssm_a_log, m_ssm_d, m_ssm_norm_g, m_w_qkv, m_b_qkv, m_w_o, m_b_o, m_attn_sinks, m_w_up, m_w_down, v_norm_mix_g, v_norm_mlp_g, v_final_norm_g, v_w_in_even, v_w_out_even, v_gm_ln_g, v_gm_ln_b, v_gm_w_s, v_gm_b_s, v_ssm_conv_w, v_ssm_conv_b, v_ssm_dt_bias, v_ssm_a_log, v_ssm_d, v_ssm_norm_g, v_w_qkv, v_b_qkv, v_w_o, v_b_o, v_attn_sinks, v_w_up, v_w_down):
    names = ["norm_mix_g", "norm_mlp_g", "final_norm_g", "w_in_even", "w_out_even", "gm_ln_g", "gm_ln_b", "gm_w_s",
             "gm_b_s", "ssm_conv_w", "ssm_conv_b", "ssm_dt_bias", "ssm_a_log", "ssm_d", "ssm_norm_g", "w_qkv",
             "b_qkv", "w_o", "b_o", "attn_sinks", "w_up", "w_down"]
    env = locals()
    W = {n: env[n] for n in names}
    M = {n: env["m_" + n] for n in names}
    V = {n: env["v_" + n] for n in names}
    big = ["w_in_even", "w_out_even", "w_qkv", "w_o", "w_up", "w_down"]
    small_sharded = ["ssm_conv_w", "b_qkv", "b_o"]
    replicated = [n for n in names if n not in big and n not in small_sharded]
    me = 4 * lax.axis_index("x") + 2 * lax.axis_index("y") + lax.axis_index("c")
    t = x.shape[1]
    xs = x.reshape(t, D_MODEL)
    target = loss_target.reshape(t, D_MODEL)
    gather = lambda a: _Move("gather", a)
    over_ici = lambda a: _Move("gather_ici", a)
    over_d2d = lambda a: _Move("gather_d2d", a)
    by_core = lambda a: a.reshape((N_CHIP, N_CORE) + a.shape[1:])
    to_sibling = lambda a: [_Move("scatter_d2d", by_core(a))]
    my_core = lax.axis_index("c").astype(jnp.int32).reshape(1)
    pair = lambda a, theirs, name: _pair_add(by_core(a), theirs, my_core, name=name)
    to_chips = lambda a: _Move("scatter_ici", a)
    whole = lambda a: a.reshape((N_DEV,) + a.shape[2:])
    row = lambda a: a.reshape(1, D_MODEL)

    small_flat = jnp.concatenate([W[n].reshape(-1) for n in small_sharded])
    w_in_g, small_g = _exchange([over_ici(w_in_even[0].astype(BF16)), gather(_as_rows(small_flat))],
                                name="gather_w_in", then_d2d=[0])
    w_in_s = whole(w_in_g)
    z_lo, xbc_lo, dt_lo = 2 * D_MODEL, 3 * D_MODEL, 3 * D_MODEL + CONV_DIM
    w_uv = jnp.concatenate(_shard_cols(w_in_s, 0, z_lo), axis=1)
    w_rest = jnp.concatenate(_shard_cols(w_in_s, xbc_lo, dt_lo) + _shard_cols(w_in_s, z_lo, xbc_lo)
                             + _shard_cols(w_in_s, dt_lo, IN_EVEN)
                             + [jnp.zeros((D_MODEL, LANES - SSM_HEADS), BF16)], axis=1)
    small_all = small_g.reshape(N_DEV, -1)
    n_cw = SSM_CONV * CONV_DIM // N_DEV
    n_bq = QKV_DIM // N_DEV
    conv_w = _cols_from_shards(small_all[:, :n_cw].reshape(N_DEV, SSM_CONV, CONV_DIM // N_DEV))
    bqkv = small_all[:, n_cw:n_cw + n_bq].reshape(1, QKV_DIM)
    bo = small_all[:, n_cw + n_bq:n_cw + n_bq + D_MODEL // N_DEV].reshape(1, D_MODEL)

    conv_b = ssm_conv_b.reshape(1, CONV_DIM)
    dt_bias, a_log, d_skip = _pad_lanes(ssm_dt_bias), _pad_lanes(ssm_a_log), _pad_lanes(ssm_d)
    gm_w = gm_w_s[0]
    gm_b = gm_b_s[0].reshape(GM_GROUPS, CHUNK, 1)
    sink_rows = jnp.repeat(jnp.transpose(attn_sinks.reshape(ATTN_KV, _PAIRS_PER_KV, 2), (0, 2, 1)), CHUNK,
                           axis=2).reshape(ATTN_KV, 2, _SINK_ROWS, 1)
    w_up_b, w_down_b = w_up.astype(BF16), w_down.astype(BF16)

    (y0, proj_uv), (w_out_g,) = _norm_matmul(xs, row(norm_mix_g[0]), w_uv, name="proj_uv", emit_y=True,
                                             moves=[over_ici(w_out_even[0].astype(BF16))])
    proj_rest, (w_qkv_g, w_o_g) = _norm_matmul(xs, row(norm_mix_g[0]), w_rest, name="proj_rest", emit_y=False,
                                               moves=[over_ici(w_qkv[0].astype(BF16)), over_ici(w_o[0].astype(BF16))])
    mix, (w_out_g, w_qkv_g, w_o_g) = _gmlp_fwd(
        proj_uv, gm_ln_g, gm_ln_b, gm_w, gm_b, name="gmlp_fwd",
        moves=[over_d2d(w_out_g), over_d2d(w_qkv_g), over_d2d(w_o_g)])
    (mix, h_states, conv_pre), (w_up0_g,) = _ssd_fwd(
        proj_rest, mix, conv_w, conv_b, dt_bias, a_log, d_skip, ssm_norm_g, name="ssd_fwd",
        moves=[over_ici(w_up_b[0])])
    w_out_f = whole(w_out_g).reshape(2 * D_MODEL, D_MODEL)
    (h1, y1), (w_down0_g, w_up0_g) = _residual_matmul(
        mix, w_out_f, xs, name="mix_out", norm_g=row(norm_mlp_g[0]),
        moves=[over_ici(w_down_b[0]), over_d2d(w_up0_g)])
    up0, (w_down0_g,) = _mlp_up(y1, whole(w_up0_g), name="mlp_up0", moves=[over_d2d(w_down0_g)])
    h2, y2 = _mlp_down(up0, whole(w_down0_g), h1, name="mlp_down0", norm_g=row(norm_mix_g[1]))
    wqkv = _cols_from_shards(whole(w_qkv_g))
    wo = whole(w_o_g).reshape(D_MODEL, D_MODEL)
    qkv = _residual_matmul(y2, wqkv, None, name="qkv", bias=bqkv)
    attn, (w_up1_g, w_down1_g) = _attn_fwd(qkv, sink_rows, name="attn_fwd",
                                           moves=[over_ici(w_up_b[1]), over_ici(w_down_b[1])])
    (h3, y3), (w_up1_g, w_down1_g) = _residual_matmul(attn, wo, h2, name="attn_out", bias=bo, norm_g=row(norm_mlp_g[1]),
                                                      moves=[over_d2d(w_up1_g), over_d2d(w_down1_g)])
    w_up_g = [whole(w_up0_g), whole(w_up1_g)]
    w_down_g = [whole(w_down0_g), whole(w_down1_g)]
    up1 = _mlp_up(y3, w_up_g[1], name="mlp_up1")
    loss_part, dh4, dh4_b, d_final_g = _mlp_down_loss(up1, w_down_g[1], h3, row(final_norm_g), target,
                                                      name="mlp_down1_loss")

    by_dev_rows = lambda a: a.reshape((N_DEV, a.shape[0] // N_DEV) + a.shape[1:])

    def mlp_bwd(dh, dh_b, h, y, up, layer, first_moves=()):
        res = _mlp_down_dx(dh_b, w_down_g[layer], up, name=f"mlp_down_dx{layer}", moves=first_moves)
        d_up, first_landed = res if first_moves else (res, [])
        g_down = _dw_by_rows(up, dh_b, name=f"mlp_down_dw{layer}", tk=FF_SHARD, square_relu=True)
        g_down = by_dev_rows(g_down)
        g_up, (theirs,) = _dw_by_cols(y, d_up, name=f"mlp_up_dw{layer}", tn=FF_SHARD, by_device=True,
                                      moves=to_sibling(g_down))
        q_down = pair(g_down, theirs, f"mlp_down_pair{layer}")
        (dh_new, dh_new_b, dg, cs), (r_down, theirs) = _dx_norm(
            d_up, w_up_g[layer], h, row(norm_mlp_g[layer]), dh, name=f"mlp_up_dx{layer}", by_device_cols=True,
            moves=[to_chips(q_down)] + to_sibling(g_up))
        q_up = pair(g_up, theirs, f"mlp_up_pair{layer}")
        return dh_new, dh_new_b, cs, dg, q_up, r_down, first_landed

    dh3, dh3_b, cs3, g_nmlp1, q_up1, r_down1, _ = mlp_bwd(dh4, dh4_b, h3, y3, up1, 1)
    g_bo = cs3
    g_wo = by_dev_rows(_dw_by_cols(attn, dh3_b, name="attn_out_dw", tn=FF_SHARD))
    d_attn, (theirs,) = _dx(dh3_b, wo, name="attn_out_dx", moves=to_sibling(g_wo))
    q_wo = pair(g_wo, theirs, "attn_out_pair")
    (dqkv, d_sink), (r_up1, r_wo) = _attn_bwd(qkv, sink_rows, d_attn, name="attn_bwd",
                                              moves=[to_chips(q_up1), to_chips(q_wo)])
    g_bqkv = _colsum(dqkv, name="qkv_db")
    g_wqkv = _cols_to_shards(_dw_by_cols(y2, dqkv, name="qkv_dw", tn=QKV_DIM // 2))
    (dh2, dh2_b, g_nmix1, _), (theirs,) = _dx_norm(dqkv, wqkv, h2, row(norm_mix_g[1]), dh3, name="qkv_dx",
                                                   moves=to_sibling(g_wqkv))
    q_wqkv = pair(g_wqkv, theirs, "qkv_pair")
    dh1, dh1_b, _, g_nmlp0, q_up0, r_down0, (r_wqkv,) = mlp_bwd(dh2, dh2_b, h1, y1, up0, 0,
                                                                first_moves=[to_chips(q_wqkv)])

    d_mix = _dx(dh1_b, w_out_f, name="mix_out_dx")
    g_wout = by_dev_rows(_dw_by_rows(mix, dh1_b, name="mix_out_dw", tk=FF_SHARD))
    (d_uv, g_ln_g, g_ln_b, g_gm_w, g_gm_b), (r_up0, theirs) = _gmlp_bwd(
        proj_uv, d_mix, gm_ln_g, gm_ln_b, gm_w, gm_b, name="gmlp_bwd", moves=[to_chips(q_up0)] + to_sibling(g_wout))
    q_wout = pair(g_wout, theirs, "mix_out_pair")

    early = [("norm_mlp_g", None), ("final_norm_g", None), ("norm_mix_g", 1), ("gm_ln_g", None), ("gm_ln_b", None),
             ("gm_w_s", None), ("gm_b_s", None), ("attn_sinks", None)]
    late = [("norm_mix_g", 0), ("ssm_conv_b", None), ("ssm_dt_bias", None), ("ssm_a_log", None), ("ssm_d", None),
            ("ssm_norm_g", None)]
    early_sharded, late_sharded = ["b_qkv", "b_o"], ["ssm_conv_w"]
    small_grads = {
        ("norm_mlp_g", None): jnp.concatenate([g_nmlp0, g_nmlp1], axis=0),
        ("final_norm_g", None): d_final_g, ("norm_mix_g", 1): g_nmix1,
        ("gm_ln_g", None): g_ln_g, ("gm_ln_b", None): g_ln_b, ("gm_w_s", None): g_gm_w, ("gm_b_s", None): g_gm_b,
        ("attn_sinks", None): jnp.transpose(
            jnp.sum(d_sink.reshape(ATTN_KV, 2, _PAIRS_PER_KV, CHUNK), axis=3), (0, 2, 1)),
        "b_qkv": g_bqkv, "b_o": g_bo,
    }
    pack = lambda keys: _as_rows(jnp.concatenate([small_grads[key].reshape(-1) for key in keys]))
    (dpre, dz, ddt, g_dtb, g_alog, g_dskip, g_ssm_ng), (r_wout, early_recv) = _ssd_bwd(
        proj_rest, conv_pre, h_states, d_mix, dt_bias, a_log, d_skip, ssm_norm_g, name="ssd_bwd",
        moves=[to_chips(q_wout), gather(pack(early + early_sharded))])
    d_rest, g_conv_w, g_conv_b = _conv_bwd(proj_rest, dpre, dz, ddt, conv_w, name="conv_bwd")
    g_w_uv = _dw_by_cols(y0, d_uv, name="proj_uv_dw", tn=FF_SHARD)
    g_w_rest = _dw_by_cols(y0, d_rest, name="proj_rest_dw", tn=REST_W // 5)
    in_cols = [(g_w_uv, 0, 0, z_lo), (g_w_rest, CONV_DIM, z_lo, xbc_lo), (g_w_rest, 0, xbc_lo, dt_lo),
               (g_w_rest, CONV_DIM + D_MODEL, dt_lo, IN_EVEN)]
    in_shard = IN_EVEN // N_DEV
    g_w_in = jnp.stack([jnp.concatenate(_cols_of(in_cols, j * in_shard, (j + 1) * in_shard), axis=1)
                        for j in range(N_DEV)])
    dy0, (theirs,) = _dx(d_uv, w_uv, name="proj_uv_dx", moves=to_sibling(g_w_in))
    q_w_in = pair(g_w_in, theirs, "proj_pair")
    half = D_MODEL // 2
    (dx, _, g_nmix0, _), (r_w_in_a,) = _dx_norm(d_rest, w_rest, xs, row(norm_mix_g[0]), dh1, name="proj_rest_dx",
                                                partial=dy0, moves=[_Move("scatter_ici", q_w_in, rows=(0, half))])
    small_grads.update({
        ("norm_mix_g", 0): g_nmix0, ("ssm_conv_b", None): g_conv_b,
        ("ssm_dt_bias", None): g_dtb[:, :SSM_HEADS], ("ssm_a_log", None): g_alog[:, :SSM_HEADS],
        ("ssm_d", None): g_dskip[:, :SSM_HEADS], ("ssm_norm_g", None): g_ssm_ng, "ssm_conv_w": g_conv_w,
    })


    def update(n, parts, moves=()):
        shape = W[n].shape
        as3 = lambda a: a.reshape((len(parts),) + parts[0].shape[1:])
        res = _adamw(parts, as3(W[n]), as3(M[n]), as3(V[n]), name="adamw_" + n, moves=moves)
        res, landed = res if moves else (res, [])
        return [a.reshape(shape) for a in res], landed

    out = {}
    out["w_o"], (late_recv,) = update("w_o", [r_wo], moves=[gather(pack(late + late_sharded))])
    out["w_down"], _ = update("w_down", [r_down0, r_down1])
    out["w_up"], _ = update("w_up", [r_up0, r_up1])
    out["w_out_even"], _ = update("w_out_even", [r_wout])
    out["w_qkv"], (r_w_in_b,) = update("w_qkv", [r_wqkv], moves=[_Move("scatter_ici", q_w_in, rows=(half, half))])
    out["w_in_even"], _ = update("w_in_even", [r_w_in_a, r_w_in_b])

    def unpacked(recv, keys):
        flat, res, o = recv.reshape(N_DEV, -1), {}, 0
        for key in keys:
            res[key] = flat[:, o:o + small_grads[key].size]
            o += small_grads[key].size
        return res

    arrived = {**unpacked(early_recv, early + early_sharded), **unpacked(late_recv, late + late_sharded)}
    piece = lambda tree, key: tree[key[0]] if key[1] is None else tree[key[0]][key[1]]

    def rows_by_device(cat):
        pad = (-cat.shape[1]) % PACK_W
        return jnp.pad(cat, ((0, 0), (0, pad))).reshape(N_DEV, -1, PACK_W)

    rep_keys = early + late
    rep_parts = rows_by_device(jnp.concatenate([arrived[key] for key in rep_keys], axis=1))
    flat_rep = lambda tree: _as_rows(jnp.concatenate([piece(tree, key).reshape(-1) for key in rep_keys]))[None]
    rep_res = _adamw([rep_parts], flat_rep(W), flat_rep(M), flat_rep(V), name="adamw_replicated")
    sh_keys = early_sharded + late_sharded
    shard_parts = []
    for n in sh_keys:
        full = arrived[n].reshape((N_DEV,) + small_grads[n].shape)
        c = full.shape[-1] // N_DEV
        shard_parts.append(lax.dynamic_slice_in_dim(full, me * c, c, axis=full.ndim - 1).reshape(N_DEV, -1))
    sh_rows = rows_by_device(jnp.concatenate(shard_parts, axis=1))
    flat_sh = lambda tree: _as_rows(jnp.concatenate([tree[n].reshape(-1) for n in sh_keys]))[None]
    sh_res = _adamw([sh_rows], flat_sh(W), flat_sh(M), flat_sh(V), name="adamw_small_sharded")

    def unpack_replicated(rows):
        flat, vals, o = rows.reshape(-1), {}, 0
        for key in rep_keys:
            size = piece(W, key).size
            vals[key] = flat[o:o + size]
            o += size
        res = {}
        for n in replicated:
            if (n, None) in vals:
                res[n] = vals[(n, None)].reshape(W[n].shape)
            else:
                res[n] = jnp.stack([vals[(n, r)] for r in range(W[n].shape[0])]).reshape(W[n].shape)
        return res

    def unpack_sharded(rows):
        flat, res, o = rows.reshape(-1), {}, 0
        for n in sh_keys:
            res[n] = flat[o:o + W[n].size].reshape(W[n].shape)
            o += W[n].size
        return res

    results = []
    for idx in range(4):
        d = {n: out[n][idx] for n in big}
        d.update(unpack_replicated(rep_res[idx]))
        d.update(unpack_sharded(sh_res[idx]))
        results.append(d)

    loss = lax.psum(loss_part[0, 0], ("x", "y", "c"))
    grad_x = dx.reshape(x.shape)
    final = [loss, grad_x]
    for d in results:
        final.extend(d[n] for n in names)
    return tuple(final)
```

```python
import dataclasses
import functools

import jax
import jax.numpy as jnp
from jax import lax
from jax.experimental import pallas as pl
from jax.experimental.pallas import tpu as pltpu

F32 = jnp.float32
BF16 = jnp.bfloat16

N_DEV = 8
D_MODEL = 1024
D_FF = 4096
RMS_EPS = 1e-5
LN_EPS = 1e-5
CHUNK = 128
GM_GROUPS = 8
SSM_HEADS = 16
SSM_HEADDIM = 64
SSM_GROUPS = 4
SSM_STATE = 128
SSM_CONV = 4
CONV_DIM = 2048
IN_EVEN = 5136
REST_W = 3200
ATTN_HEADS = 16
ATTN_KV = 2
HEAD_DIM = 64
QKV_DIM = 1280
LANES = 128
HALO = 8
PACK_W = 1024

ADAM_LR = 0.001
ADAM_B1 = 0.9
ADAM_B2 = 0.999
ADAM_EPS = 1e-08
ADAM_WD = 0.01
ADAM_STEP = 10

VMEM_LIMIT_BYTES = 56 * 1024 * 1024


_NN = (((1,), (0,)), ((), ()))
_NT = (((1,), (1,)), ((), ()))
_TN = (((0,), (0,)), ((), ()))


def _dg(a, b, dims):
    return lax.dot_general(a.astype(BF16), b.astype(BF16), dims, preferred_element_type=F32)


@jax.custom_vjp
def _nn(a, b):
    return _dg(a, b, _NN)


@jax.custom_vjp
def _nt(a, b):
    return _dg(a, b, _NT)


@jax.custom_vjp
def _tn(a, b):
    return _dg(a, b, _TN)


_nn.defvjp(lambda a, b: (_dg(a, b, _NN), (a, b)), lambda r, g: (_nt(g, r[1]), _tn(r[0], g)))
_nt.defvjp(lambda a, b: (_dg(a, b, _NT), (a, b)), lambda r, g: (_nn(g, r[1]), _tn(g, r[0])))
_tn.defvjp(lambda a, b: (_dg(a, b, _TN), (a, b)), lambda r, g: (_nt(r[1], g), _nn(r[0], g)))


def _split3_dot(tri, x):
    x1 = x.astype(BF16)
    r1 = x - x1.astype(F32)
    x2 = r1.astype(BF16)
    x3 = (r1 - x2.astype(F32)).astype(BF16)
    t = tri.astype(BF16)
    dot = lambda p: lax.dot_general(t, p, _NN, preferred_element_type=F32)
    return dot(x1) + dot(x2) + dot(x3)


def _tri(lower):
    r = lax.broadcasted_iota(jnp.int32, (CHUNK, CHUNK), 0)
    c = lax.broadcasted_iota(jnp.int32, (CHUNK, CHUNK), 1)
    return jnp.where((r >= c) if lower else (r <= c), 1.0, 0.0).astype(F32)


@jax.custom_vjp
def _cumsum_rows(x):
    return _split3_dot(_tri(True), x)


_cumsum_rows.defvjp(lambda x: (_split3_dot(_tri(True), x), None), lambda _, g: (_split3_dot(_tri(False), g),))


def _sigmoid(x):
    return 1.0 / (1.0 + jnp.exp(-x))


def _silu(x):
    return x * _sigmoid(x)


def _softplus(x):
    return jnp.maximum(x, 0.0) + jnp.log(1.0 + jnp.exp(-jnp.abs(x)))


def _gelu_tanh(x):
    return 0.5 * x * (1.0 + jnp.tanh(0.7978845608028654 * (x + 0.044715 * (x * x * x))))


def _rmsnorm(x, g):
    return x * lax.rsqrt(jnp.mean(x * x, axis=-1, keepdims=True) + RMS_EPS) * g


def _gmlp_chunk(u, v, ln_g, ln_b, w_s, b_s):
    gu = _gelu_tanh(u)
    gv = _gelu_tanh(v)
    mu = jnp.mean(gv, axis=-1, keepdims=True)
    var = jnp.mean(jnp.square(gv - mu), axis=-1, keepdims=True)
    vn = (gv - mu) * lax.rsqrt(var + LN_EPS) * ln_g + ln_b
    r = lax.broadcasted_iota(jnp.int32, (CHUNK, CHUNK), 0)
    c = lax.broadcasted_iota(jnp.int32, (CHUNK, CHUNK), 1)
    causal = r >= c
    outs = []
    for g in range(GM_GROUPS):
        cols = slice(g * LANES, (g + 1) * LANES)
        mixed = _nn(jnp.where(causal, w_s[g], 0.0), vn[:, cols]) + b_s[g]
        outs.append(gu[:, cols] * mixed)
    return jnp.concatenate(outs, axis=1)


def _lane_pick(row, h):
    lane = lax.broadcasted_iota(jnp.int32, row.shape, 1)
    return jnp.sum(jnp.where(lane == h, row, 0.0), axis=1, keepdims=True)


def _col_pick(m, h):
    lane = lax.broadcasted_iota(jnp.int32, m.shape, 1)
    return jnp.sum(jnp.where(lane == h, m, 0.0), axis=1, keepdims=True)


def _row_pick(m, h):
    sub = lax.broadcasted_iota(jnp.int32, m.shape, 0)
    return jnp.sum(jnp.where(sub == h, m, 0.0), axis=0, keepdims=True)


_PAIRS = SSM_HEADS // 2


def _ssd_chunk(pre, z, dt_raw, h_prev, dt_bias, a_log, d_skip, norm_g):
    xbc = _silu(pre)
    dt = _softplus(dt_raw + dt_bias)
    da = dt * (-jnp.exp(a_log))
    a_cum = _cumsum_rows(da)
    a_cum_t = a_cum.T
    dt_t = dt.T
    r = lax.broadcasted_iota(jnp.int32, (CHUNK, CHUNK), 0)
    c = lax.broadcasted_iota(jnp.int32, (CHUNK, CHUNK), 1)
    causal = r >= c
    lane_lo = lax.broadcasted_iota(jnp.int32, (1, LANES), 1) < SSM_HEADDIM
    last_row = lax.broadcasted_iota(jnp.int32, (CHUNK, 1), 0) == CHUNK - 1
    ys, h_next = [], []
    for j in range(_PAIRS):
        g = j // 2
        xs = xbc[:, j * LANES:(j + 1) * LANES]
        bm = xbc[:, 1024 + g * SSM_STATE:1024 + (g + 1) * SSM_STATE]
        cm = xbc[:, 1536 + g * SSM_STATE:1536 + (g + 1) * SSM_STATE]
        cb = _nt(cm, bm)
        y_diag, to_end, e_cum, c_dec, d_row = [], [], [], [], []
        for h in (2 * j, 2 * j + 1):
            col = _col_pick(a_cum, h)
            row = _row_pick(a_cum_t, h)
            dt_col = _col_pick(dt, h)
            dt_row = _row_pick(dt_t, h)
            decay = jnp.exp(jnp.where(causal, col - row, -jnp.inf))
            y_diag.append(_nn(cb * decay * dt_row, xs))
            last = jnp.sum(jnp.where(last_row, col, 0.0), axis=0, keepdims=True)
            to_end.append(jnp.exp(last - col) * dt_col)
            e_cum.append(jnp.exp(col))
            c_dec.append(jnp.exp(last))
            d_row.append(_lane_pick(d_skip, h))
        pair = lambda lo_hi: jnp.where(lane_lo, lo_hi[0], lo_hi[1])
        states = _tn(bm, xs * pair(to_end))
        y_off = _nn(cm, h_prev[j]) * pair(e_cum)
        ys.append(pair(y_diag) + y_off + xs * pair(d_row))
        h_next.append(pair(c_dec) * h_prev[j] + states)
    y = jnp.concatenate(ys, axis=1) * _silu(z)
    width = D_MODEL // SSM_GROUPS
    y = jnp.concatenate(
        [_rmsnorm(y[:, g * width:(g + 1) * width], norm_g[:, g * width:(g + 1) * width]) for g in range(SSM_GROUPS)],
        axis=1)
    return y, tuple(h_next)


def _shift_down(prev8, x, k):
    if k == 0:
        return x
    win = jnp.concatenate([prev8, x], axis=0)
    return pltpu.roll(win, k, 0)[HALO:]


def _shift_up(x, next8, k):
    if k == 0:
        return x
    n = x.shape[0]
    win = jnp.concatenate([x, next8], axis=0)
    return pltpu.roll(win, n + HALO - k, 0)[:n]


def _conv_pre(prev8, x, w, b):
    out = b + x * w[SSM_CONV - 1:SSM_CONV]
    for i in range(SSM_CONV - 1):
        out = out + _shift_down(prev8, x, SSM_CONV - 1 - i) * w[i:i + 1]
    return out


@jax.custom_vjp
def _swap_halves(x):
    return pltpu.roll(x, HEAD_DIM, 1)


_swap_halves.defvjp(lambda x: (pltpu.roll(x, HEAD_DIM, 1), None), lambda _, g: (pltpu.roll(g, HEAD_DIM, 1),))

_PAIRS_PER_KV = ATTN_HEADS // ATTN_KV // 2


def _attn_pairs(q4, kv_prev, kv_cur, sink_lo, sink_hi, first, kv_head):
    kv = jnp.concatenate([kv_prev, kv_cur], axis=0)
    lane = lax.broadcasted_iota(jnp.int32, (1, LANES), 1)
    own = (lane >= HEAD_DIM * kv_head) & (lane < HEAD_DIM * (kv_head + 1))

    def placed(pair):
        mine = jnp.where(own, pair, 0.0)
        lo = mine if kv_head == 0 else _swap_halves(mine)
        return lo, _swap_halves(lo)

    k_lo, k_hi = placed(kv[:, :LANES])
    v_lo, v_hi = placed(kv[:, LANES:])
    out = None
    for k_e, v_e, sink in ((k_lo, v_lo, sink_lo), (k_hi, v_hi, sink_hi)):
        s = _nt(q4, k_e) * (HEAD_DIM ** -0.5)
        rows = lax.broadcasted_iota(jnp.int32, s.shape, 0) & (CHUNK - 1)
        cols = lax.broadcasted_iota(jnp.int32, s.shape, 1)
        valid = (cols <= rows + CHUNK) & (cols > rows) & (cols >= CHUNK * first.astype(jnp.int32))
        s = jnp.where(valid, s, -jnp.inf)
        m = lax.stop_gradient(jnp.maximum(jnp.max(s, axis=-1, keepdims=True), sink))
        p = jnp.exp(s - m)
        denom = jnp.sum(p, axis=-1, keepdims=True) + jnp.exp(sink - m)
        o = _nn(p, v_e) / denom
        out = o if out is None else out + o
    return out


N_CHIP = 4
N_CORE = 2
_OTHER_CHIPS = (2, 4, 6)


@dataclasses.dataclass
class _Move:
    kind: str
    src: jax.Array
    rows: tuple = None

    def dst_shape(self):
        s = self.src.shape
        shape = {"gather": (N_DEV,) + s, "gather_ici": (N_CHIP, N_CORE) + s, "gather_d2d": s,
                 "scatter_d2d": (N_CHIP,) + s[2:], "scatter_ici": s}[self.kind]
        if self.rows is not None:
            shape = (shape[0], self.rows[1]) + tuple(shape[2:])
        return jax.ShapeDtypeStruct(tuple(shape), self.src.dtype)


def _peer(x, y, c, k):
    return (1 - x if k & 4 else x, 1 - y if k & 2 else y, 1 - c if k & 1 else c)


def _move_copies(moves, srcs, dsts, send_sems, recv_sems, local_sems):
    x, y, c = lax.axis_index("x"), lax.axis_index("y"), lax.axis_index("c")
    chip = 2 * x + y
    me = 2 * chip + c
    sibling = (x, y, 1 - c)
    all_chips = pl.ds(0, N_CHIP)
    local, remote = [], []

    def push(n, k, src, dst, device):
        remote.append(pltpu.make_async_remote_copy(
            src_ref=src, dst_ref=dst, send_sem=send_sems.at[n, k], recv_sem=recv_sems.at[n, k],
            device_id=device, device_id_type=pl.DeviceIdType.MESH))

    for n, mv in enumerate(moves):
        s, d = srcs[n], dsts[n]
        if mv.kind == "gather":
            local.append(pltpu.make_async_copy(s, d.at[me], local_sems.at[n]))
            for k in range(1, N_DEV):
                push(n, k - 1, s, d.at[me], _peer(x, y, c, k))
        elif mv.kind == "gather_ici":
            local.append(pltpu.make_async_copy(s, d.at[chip, c], local_sems.at[n]))
            for k in _OTHER_CHIPS:
                push(n, k - 1, s, d.at[chip, c], _peer(x, y, c, k))
        elif mv.kind == "gather_d2d":
            push(n, 0, d.at[all_chips, c], d.at[all_chips, c], sibling)
        elif mv.kind == "scatter_d2d":
            push(n, 0, s.at[all_chips, 1 - c], d, sibling)
        else:
            assert mv.kind == "scatter_ici", mv.kind
            part = (lambda r: r) if mv.rows is None else (lambda r: r.at[pl.ds(mv.rows[0], mv.rows[1])])
            local.append(pltpu.make_async_copy(part(s.at[chip]), d.at[chip], local_sems.at[n]))
            for k in _OTHER_CHIPS:
                px, py, _ = _peer(x, y, c, k)
                push(n, k - 1, part(s.at[2 * px + py]), d.at[chip], (px, py, c))
    return local, remote


def _move_aliases(moves, n_in, n_out):
    return {n_in + n: n_out + n for n, mv in enumerate(moves) if mv.kind == "gather_d2d"}


def _pcall(body, *, name, grid, in_specs, out_specs, out_shape, scratch_shapes=(), semantics=(), moves=(),
           aliases=None):
    out_shape, out_specs = list(out_shape), list(out_specs)
    in_specs = list(in_specs)
    if not moves:
        call = pl.pallas_call(
            body, name=name, grid=grid, in_specs=in_specs, out_specs=out_specs, out_shape=out_shape,
            scratch_shapes=list(scratch_shapes), input_output_aliases=aliases or {},
            compiler_params=pltpu.CompilerParams(dimension_semantics=tuple(semantics),
                                                 vmem_limit_bytes=VMEM_LIMIT_BYTES))
        return (lambda *args: (list(call(*args)), []))
    n_in, n_out, n_scr, n_mv = len(in_specs), len(out_shape), len(scratch_shapes), len(moves)
    hbm = pl.BlockSpec(memory_space=pltpu.HBM)

    def carrier(*refs):
        ins, rest = refs[:n_in], refs[n_in:]
        srcs, rest = rest[:n_mv], rest[n_mv:]
        outs, rest = rest[:n_out], rest[n_out:]
        dsts, rest = rest[:n_mv], rest[n_mv:]
        scr, (send_sems, recv_sems, local_sems) = rest[:n_scr], rest[n_scr:]
        first = functools.reduce(jnp.logical_and, [pl.program_id(d) == 0 for d in range(len(grid))])
        last = functools.reduce(jnp.logical_and, [pl.program_id(d) == grid[d] - 1 for d in range(len(grid))])

        @pl.when(first)
        def _():
            local, remote = _move_copies(moves, srcs, dsts, send_sems, recv_sems, local_sems)
            for cp in local + remote:
                cp.start()

        body(*ins, *outs, *scr)

        @pl.when(last)
        def _():
            local, remote = _move_copies(moves, srcs, dsts, send_sems, recv_sems, local_sems)
            for cp in remote + local:
                cp.wait()

    call = pl.pallas_call(
        carrier, name=name, grid=grid,
        in_specs=in_specs + [hbm] * n_mv,
        out_specs=out_specs + [hbm] * n_mv,
        out_shape=out_shape + [mv.dst_shape() for mv in moves],
        scratch_shapes=list(scratch_shapes) + [pltpu.SemaphoreType.DMA((n_mv, N_DEV - 1)),
                                               pltpu.SemaphoreType.DMA((n_mv, N_DEV - 1)),
                                               pltpu.SemaphoreType.DMA((n_mv,))],
        input_output_aliases={**(aliases or {}), **_move_aliases(moves, n_in, n_out)},
        compiler_params=pltpu.CompilerParams(dimension_semantics=("arbitrary",) * len(grid),
                                             vmem_limit_bytes=VMEM_LIMIT_BYTES))

    def run(*args):
        res = list(call(*args, *[mv.src for mv in moves]))
        return res[:n_out], res[n_out:]

    return run


def _exchange(moves, *, name, then_d2d=()):
    n_mv, n_fwd = len(moves), len(then_d2d)
    hbm = pl.BlockSpec(memory_space=pltpu.HBM)
    second = [_Move("gather_d2d", moves[n].src) for n in then_d2d]

    def body(*refs):
        srcs, dsts, sems = refs[:n_mv], refs[n_mv:2 * n_mv], refs[2 * n_mv:]
        local, remote = _move_copies(moves, srcs, dsts, *sems[:3])
        for cp in local + remote:
            cp.start()
        for cp in remote + local:
            cp.wait()
        if second:
            landed = [dsts[n] for n in then_d2d]
            _, remote = _move_copies(second, landed, landed, sems[3], sems[4], None)
            for cp in remote:
                cp.start()
            for cp in remote:
                cp.wait()

    sems = [pltpu.SemaphoreType.DMA((n_mv, N_DEV - 1)), pltpu.SemaphoreType.DMA((n_mv, N_DEV - 1)),
            pltpu.SemaphoreType.DMA((n_mv,))]
    if second:
        sems += [pltpu.SemaphoreType.DMA((n_fwd, N_DEV - 1)), pltpu.SemaphoreType.DMA((n_fwd, N_DEV - 1))]
    return list(pl.pallas_call(
        body, name=name, in_specs=[hbm] * n_mv, out_specs=[hbm] * n_mv,
        out_shape=[mv.dst_shape() for mv in moves], scratch_shapes=sems,
    )(*[mv.src for mv in moves]))


TM = 512
FF_SHARD = D_FF // N_DEV


def _whole(a):
    nd = a.ndim
    return pl.BlockSpec(a.shape, lambda i: (0,) * nd)


def _rows(width, col=0):
    return pl.BlockSpec((TM, width), lambda i: (i, col))


def _acc_row(width):
    return pl.BlockSpec((1, width), lambda i: (0, 0))


def _unpack(res_landed, moves, n_out):
    res, landed = res_landed
    res = res[0] if n_out == 1 else res
    return (res, landed) if moves else res


def _norm_matmul(x, g, w, *, name, emit_y, moves=()):
    t, d = x.shape
    n = w.shape[1]

    def body(x_ref, g_ref, w_ref, *outs):
        y = _rmsnorm(x_ref[...], g_ref[...]).astype(BF16)
        if emit_y:
            outs[0][...] = y
        outs[-1][...] = lax.dot_general(y, w_ref[...], _NN, preferred_element_type=F32)

    shapes = ([jax.ShapeDtypeStruct((t, d), BF16)] if emit_y else []) + [jax.ShapeDtypeStruct((t, n), F32)]
    specs = ([_rows(d)] if emit_y else []) + [_rows(n)]
    return _unpack(_pcall(body, name=name, grid=(t // TM,), in_specs=[_rows(d), _acc_row(d), _whole(w)],
                          out_specs=specs, out_shape=shapes, semantics=("parallel",), moves=moves)(x, g, w),
                   moves, len(shapes))


def _residual_matmul(a, w, res, *, name, bias=None, norm_g=None, moves=()):
    t, k = a.shape
    n = w.shape[1]
    has_res, has_bias, has_norm = res is not None, bias is not None, norm_g is not None

    def body(a_ref, w_ref, *rest):
        rest = list(rest)
        res_ref = rest.pop(0) if has_res else None
        b_ref = rest.pop(0) if has_bias else None
        g_ref = rest.pop(0) if has_norm else None
        h = lax.dot_general(a_ref[...].astype(BF16), w_ref[...], _NN, preferred_element_type=F32)
        if has_res:
            h = h + res_ref[...]
        if has_bias:
            h = h + b_ref[...]
        rest[0][...] = h
        if has_norm:
            rest[1][...] = _rmsnorm(h, g_ref[...]).astype(BF16)

    rows_in = [res] if has_res else []
    extra = ([bias] if has_bias else []) + ([norm_g] if has_norm else [])
    shapes = [jax.ShapeDtypeStruct((t, n), F32)] + ([jax.ShapeDtypeStruct((t, n), BF16)] if has_norm else [])
    return _unpack(_pcall(body, name=name, grid=(t // TM,),
                          in_specs=[_rows(k), _whole(w)] + [_rows(n)] * len(rows_in) + [_acc_row(n)] * len(extra),
                          out_specs=[_rows(n)] * len(shapes), out_shape=shapes, semantics=("parallel",),
                          moves=moves)(a, w, *rows_in, *extra), moves, len(shapes))


def _mlp_up(y, w_cols, *, name, moves=()):
    t, d = y.shape

    def body(y_ref, w_ref, up_ref):
        yv = y_ref[...]
        for j in range(N_DEV):
            up_ref[:, j * FF_SHARD:(j + 1) * FF_SHARD] = lax.dot_general(
                yv, w_ref[j], _NN, preferred_element_type=F32).astype(up_ref.dtype)

    return _unpack(_pcall(body, name=name, grid=(t // TM,), in_specs=[_rows(d), _whole(w_cols)],
                          out_specs=[_rows(D_FF)], out_shape=[jax.ShapeDtypeStruct((t, D_FF), BF16)],
                          semantics=("parallel",), moves=moves)(y, w_cols), moves, 1)


def _sq_relu(u):
    return jnp.square(jnp.maximum(u.astype(F32), 0.0))


def _down_blocks(w_refs):
    for j in range(N_DEV):
        off = j * FF_SHARD
        for w_ref in w_refs:
            yield off, w_ref.shape[1], w_ref[j]
            off += w_ref.shape[1]


def _mlp_down(up, w_rows, res, *, name, norm_g=None, moves=()):
    t = up.shape[0]
    has_norm = norm_g is not None
    n_w = len(w_rows)

    def body(up_ref, *rest):
        w_refs, res_ref, rest = rest[:n_w], rest[n_w], rest[n_w + 1:]
        h = res_ref[...]
        for off, rows, w_blk in _down_blocks(w_refs):
            act = _sq_relu(up_ref[:, off:off + rows]).astype(BF16)
            h = h + lax.dot_general(act, w_blk, _NN, preferred_element_type=F32)
        if has_norm:
            g_ref, h_ref, y_ref = rest
            y_ref[...] = _rmsnorm(h, g_ref[...]).astype(BF16)
        else:
            (h_ref,) = rest
        h_ref[...] = h

    shapes = [jax.ShapeDtypeStruct((t, D_MODEL), F32)] + ([jax.ShapeDtypeStruct((t, D_MODEL), BF16)] if has_norm else [])
    return _unpack(_pcall(body, name=name, grid=(t // TM,),
                          in_specs=[_rows(D_FF)] + [_whole(w) for w in w_rows] + [_rows(D_MODEL)]
                          + ([_acc_row(D_MODEL)] if has_norm else []),
                          out_specs=[_rows(D_MODEL)] * len(shapes), out_shape=shapes, semantics=("parallel",),
                          moves=moves)(up, *w_rows, res, *([norm_g] if has_norm else [])), moves, len(shapes))


def _mlp_down_dx(dh, w_rows, up, *, name, moves=()):
    t = up.shape[0]
    n_w = len(w_rows)

    def body(dh_ref, *rest):
        w_refs, (up_ref, o_ref) = rest[:n_w], rest[n_w:]
        dhv = dh_ref[...]
        for off, rows, w_blk in _down_blocks(w_refs):
            cols = slice(off, off + rows)
            d_act = lax.dot_general(dhv, w_blk, _NT, preferred_element_type=F32)
            o_ref[:, cols] = (d_act * (2.0 * jnp.maximum(up_ref[:, cols].astype(F32), 0.0))).astype(o_ref.dtype)

    return _unpack(_pcall(body, name=name, grid=(t // TM,),
                          in_specs=[_rows(D_MODEL)] + [_whole(w) for w in w_rows] + [_rows(D_FF)],
                          out_specs=[_rows(D_FF)], out_shape=[jax.ShapeDtypeStruct((t, D_FF), BF16)],
                          semantics=("parallel",), moves=moves)(dh, *w_rows, up), moves, 1)


def _dw_by_cols(x, dy, *, name, tn, by_device=False, moves=()):
    t, k = x.shape
    n = dy.shape[1]
    assert n % tn == 0, (name, n, tn)

    def body(x_ref, dy_ref, o_ref):
        o_ref[...] = lax.dot_general(x_ref[...].astype(BF16), dy_ref[...].astype(BF16), _TN,
                                     preferred_element_type=F32).astype(o_ref.dtype)

    if by_device:
        out_spec, out_shape = pl.BlockSpec((None, k, tn), lambda j: (j, 0, 0)), (n // tn, k, tn)
    else:
        out_spec, out_shape = pl.BlockSpec((k, tn), lambda j: (0, j)), (k, n)
    return _unpack(_pcall(body, name=name, grid=(n // tn,),
                          in_specs=[_whole(x), pl.BlockSpec((t, tn), lambda j: (0, j))],
                          out_specs=[out_spec], out_shape=[jax.ShapeDtypeStruct(out_shape, BF16)],
                          semantics=("parallel",), moves=moves)(x, dy), moves, 1)


def _dw_by_rows(x, dy, *, name, tk, square_relu=False, moves=()):
    t, k = x.shape
    n = dy.shape[1]
    assert k % tk == 0, (name, k, tk)

    def body(x_ref, dy_ref, o_ref):
        xv = _sq_relu(x_ref[...]) if square_relu else x_ref[...]
        o_ref[...] = lax.dot_general(xv.astype(BF16), dy_ref[...].astype(BF16), _TN,
                                     preferred_element_type=F32).astype(o_ref.dtype)

    return _unpack(_pcall(body, name=name, grid=(k // tk,),
                          in_specs=[pl.BlockSpec((t, tk), lambda j: (0, j)), _whole(dy)],
                          out_specs=[pl.BlockSpec((tk, n), lambda j: (j, 0))],
                          out_shape=[jax.ShapeDtypeStruct((k, n), BF16)],
                          semantics=("parallel",), moves=moves)(x, dy), moves, 1)


def _dx(dy, w, *, name, partial=None, moves=()):
    t, k = dy.shape
    n = w.shape[0]
    has_partial = partial is not None

    def body(dy_ref, w_ref, *rest):
        out = lax.dot_general(dy_ref[...].astype(BF16), w_ref[...], _NT, preferred_element_type=F32)
        if has_partial:
            out = out + rest[0][...]
        rest[-1][...] = out

    return _unpack(_pcall(body, name=name, grid=(t // TM,),
                          in_specs=[_rows(k), _whole(w)] + ([_rows(n)] if has_partial else []),
                          out_specs=[_rows(n)], out_shape=[jax.ShapeDtypeStruct((t, n), F32)],
                          semantics=("parallel",), moves=moves)(dy, w, *([partial] if has_partial else [])),
                   moves, 1)


def _dx_norm(dy, w, h, g, dres, *, name, partial=None, by_device_cols=False, moves=()):
    t, k = dy.shape
    d = h.shape[1]
    has_partial = partial is not None

    def body(dy_ref, w_ref, h_ref, g_ref, dres_ref, *rest):
        if by_device_cols:
            kc = k // N_DEV
            d_y = jnp.zeros((TM, d), F32)
            for j in range(N_DEV):
                d_y = d_y + lax.dot_general(dy_ref[:, j * kc:(j + 1) * kc].astype(BF16), w_ref[j], _NT,
                                            preferred_element_type=F32)
        else:
            d_y = lax.dot_general(dy_ref[...].astype(BF16), w_ref[...], _NT, preferred_element_type=F32)
        if has_partial:
            d_y = d_y + rest[0][...]
        dh_ref, dhb_ref, dg_ref, cs_ref = rest[-4:]
        _, vjp = jax.vjp(_rmsnorm, h_ref[...], g_ref[...])
        dh, dg = vjp(d_y)
        dh = dh + dres_ref[...]
        dh_ref[...] = dh
        dhb_ref[...] = dh.astype(BF16)

        @pl.when(pl.program_id(0) == 0)
        def _():
            dg_ref[...] = jnp.zeros_like(dg_ref)
            cs_ref[...] = jnp.zeros_like(cs_ref)

        dg_ref[...] += dg
        cs_ref[...] += jnp.sum(dh, axis=0, keepdims=True)

    shapes = [jax.ShapeDtypeStruct((t, d), F32), jax.ShapeDtypeStruct((t, d), BF16),
              jax.ShapeDtypeStruct((1, d), F32), jax.ShapeDtypeStruct((1, d), F32)]
    return _unpack(_pcall(body, name=name, grid=(t // TM,),
                          in_specs=[_rows(k), _whole(w), _rows(d), _acc_row(d), _rows(d)]
                          + ([_rows(d)] if has_partial else []),
                          out_specs=[_rows(d), _rows(d), _acc_row(d), _acc_row(d)], out_shape=shapes,
                          semantics=("arbitrary",), moves=moves)(dy, w, h, g, dres, *([partial] if has_partial else [])),
                   moves, 4)


def _pair_add(by_core, theirs, core, *, name, tb=512):
    n_chip, _, r, c = by_core.shape
    tb = min(tb, r)
    assert r % tb == 0, (name, r, tb)

    def body(core_ref, a_ref, b_ref, o_ref):
        del core_ref
        o_ref[...] = (a_ref[...].astype(F32) + b_ref[...].astype(F32)).astype(o_ref.dtype)

    blk = pl.BlockSpec((None, tb, c), lambda ch, i, core_ref: (ch, i, 0))
    return pl.pallas_call(
        body, name=name,
        grid_spec=pltpu.PrefetchScalarGridSpec(
            num_scalar_prefetch=1, grid=(n_chip, r // tb),
            in_specs=[pl.BlockSpec((None, None, tb, c), lambda ch, i, core_ref: (ch, core_ref[0], i, 0)), blk],
            out_specs=blk),
        out_shape=jax.ShapeDtypeStruct((n_chip, r, c), by_core.dtype),
        compiler_params=pltpu.CompilerParams(dimension_semantics=("parallel", "parallel"),
                                             vmem_limit_bytes=VMEM_LIMIT_BYTES),
    )(core, by_core, theirs)


def _colsum(a, *, name, tb=512):
    t, d = a.shape

    def body(a_ref, o_ref):
        @pl.when(pl.program_id(0) == 0)
        def _():
            o_ref[...] = jnp.zeros_like(o_ref)

        o_ref[...] += jnp.sum(a_ref[...].astype(F32), axis=0, keepdims=True)

    return _pcall(
        body, name=name, grid=(t // tb,),
        in_specs=[pl.BlockSpec((tb, d), lambda i: (i, 0))],
        out_specs=[pl.BlockSpec((1, d), lambda i: (0, 0))],
        out_shape=[jax.ShapeDtypeStruct((1, d), F32)],
        semantics=("arbitrary",),
    )(a)[0][0]


def _mlp_down_loss(up, w_rows, res, g, target, *, name):
    t, d = res.shape
    n_w = len(w_rows)

    def body(up_ref, *rest):
        w_refs, (res_ref, g_ref, tgt_ref, loss_ref, dh_ref, dhb_ref, dg_ref) = rest[:n_w], rest[n_w:]
        h = res_ref[...]
        for off, rows, w_blk in _down_blocks(w_refs):
            act = _sq_relu(up_ref[:, off:off + rows]).astype(BF16)
            h = h + lax.dot_general(act, w_blk, _NN, preferred_element_type=F32)

        def f(hh, gg):
            err = jnp.square(_rmsnorm(hh, gg) - tgt_ref[...])
            return 0.5 * jnp.sum(jnp.mean(err, axis=-1, keepdims=True), axis=0, keepdims=True)

        val, vjp = jax.vjp(f, h, g_ref[...])
        dh, dg = vjp(jnp.ones((1, 1), F32))
        dh_ref[...] = dh
        dhb_ref[...] = dh.astype(BF16)

        @pl.when(pl.program_id(0) == 0)
        def _():
            loss_ref[...] = jnp.zeros_like(loss_ref)
            dg_ref[...] = jnp.zeros_like(dg_ref)

        loss_ref[...] += val
        dg_ref[...] += dg

    return _pcall(
        body, name=name, grid=(t // TM,),
        in_specs=[_rows(D_FF)] + [_whole(w) for w in w_rows] + [_rows(d), _acc_row(d), _rows(d)],
        out_specs=[pl.BlockSpec((8, LANES), lambda i: (0, 0)), _rows(d), _rows(d), _acc_row(d)],
        out_shape=[jax.ShapeDtypeStruct((8, LANES), F32), jax.ShapeDtypeStruct((t, d), F32),
                   jax.ShapeDtypeStruct((t, d), BF16), jax.ShapeDtypeStruct((1, d), F32)],
        semantics=("arbitrary",),
    )(up, *w_rows, res, g, target)[0]


def _gmlp_fwd(proj_uv, ln_g, ln_b, w_s, b_s, *, name, moves=()):
    t = proj_uv.shape[0]
    w = D_MODEL

    def body(u_ref, v_ref, g_ref, b_ref, w_ref, bs_ref, o_ref):
        o_ref[...] = _gmlp_chunk(u_ref[...], v_ref[...], g_ref[...], b_ref[...], w_ref[...],
                                 bs_ref[...]).astype(o_ref.dtype)

    row = pl.BlockSpec((1, w), lambda i: (0, 0))
    res, landed = _pcall(
        body, name=name, grid=(t // CHUNK,),
        in_specs=[pl.BlockSpec((CHUNK, w), lambda i: (i, 0)), pl.BlockSpec((CHUNK, w), lambda i: (i, 1)), row, row,
                  pl.BlockSpec((GM_GROUPS, CHUNK, CHUNK), lambda i: (0, 0, 0)),
                  pl.BlockSpec((GM_GROUPS, CHUNK, 1), lambda i: (0, 0, 0))],
        out_specs=[pl.BlockSpec((CHUNK, w), lambda i: (i, 0))],
        out_shape=[jax.ShapeDtypeStruct((t, 2 * w), BF16)],
        semantics=("parallel",), moves=moves,
    )(proj_uv, proj_uv, ln_g, ln_b, w_s, b_s)
    return (res[0], landed) if moves else res[0]


def _gmlp_bwd(proj_uv, d_mix, ln_g, ln_b, w_s, b_s, *, name, moves=()):
    t = proj_uv.shape[0]
    w = D_MODEL

    def body(u_ref, v_ref, da_ref, g_ref, b_ref, w_ref, bs_ref, duv_ref, dg_ref, db_ref, dw_ref, dbs_ref):
        _, vjp = jax.vjp(_gmlp_chunk, u_ref[...], v_ref[...], g_ref[...], b_ref[...], w_ref[...], bs_ref[...])
        du, dv, dg, db, dw, dbs = vjp(da_ref[...])
        duv_ref[:, :w] = du.astype(duv_ref.dtype)
        duv_ref[:, w:] = dv.astype(duv_ref.dtype)

        @pl.when(pl.program_id(0) == 0)
        def _():
            dg_ref[...] = jnp.zeros_like(dg_ref)
            db_ref[...] = jnp.zeros_like(db_ref)
            dw_ref[...] = jnp.zeros_like(dw_ref)
            dbs_ref[...] = jnp.zeros_like(dbs_ref)

        dg_ref[...] += dg
        db_ref[...] += db
        dw_ref[...] += dw
        dbs_ref[...] += dbs

    row = pl.BlockSpec((1, w), lambda i: (0, 0))
    ws = pl.BlockSpec((GM_GROUPS, CHUNK, CHUNK), lambda i: (0, 0, 0))
    bs = pl.BlockSpec((GM_GROUPS, CHUNK, 1), lambda i: (0, 0, 0))
    res, landed = _pcall(
        body, name=name, grid=(t // CHUNK,),
        in_specs=[pl.BlockSpec((CHUNK, w), lambda i: (i, 0)), pl.BlockSpec((CHUNK, w), lambda i: (i, 1)),
                  pl.BlockSpec((CHUNK, w), lambda i: (i, 0)), row, row, ws, bs],
        out_specs=[pl.BlockSpec((CHUNK, 2 * w), lambda i: (i, 0)), row, row, ws, bs],
        out_shape=[jax.ShapeDtypeStruct((t, 2 * w), BF16), jax.ShapeDtypeStruct((1, w), F32),
                   jax.ShapeDtypeStruct((1, w), F32), jax.ShapeDtypeStruct((GM_GROUPS, CHUNK, CHUNK), F32),
                   jax.ShapeDtypeStruct((GM_GROUPS, CHUNK, 1), F32)],
        semantics=("arbitrary",), moves=moves,
    )(proj_uv, proj_uv, d_mix, ln_g, ln_b, w_s, b_s)
    return (res, landed) if moves else res


_HALO_PER_CHUNK = CHUNK // HALO
_DT_BLOCK = (CONV_DIM + D_MODEL) // LANES


def _ssd_fwd(proj_rest, mix, conv_w, conv_b, dt_bias, a_log, d_skip, norm_g, *, name, moves=()):
    t = proj_rest.shape[0]
    nc = t // CHUNK

    def body(x_ref, prev_ref, z_ref, dt_ref, mix_ref, cw_ref, cb_ref, dtb_ref, al_ref, ds_ref, ng_ref, y_ref, hs_ref,
             pre_ref, h_scr):
        del mix_ref
        i = pl.program_id(0)

        @pl.when(i == 0)
        def _():
            h_scr[...] = jnp.zeros_like(h_scr)

        prev8 = jnp.where(i == 0, 0.0, prev_ref[...])
        pre = _conv_pre(prev8, x_ref[...], cw_ref[...], cb_ref[...])
        pre_ref[...] = pre
        hs_ref[0] = h_scr[...]
        h_prev = tuple(h_scr[j] for j in range(_PAIRS))
        y, h_next = _ssd_chunk(pre, z_ref[...], dt_ref[...], h_prev, dtb_ref[...], al_ref[...], ds_ref[...],
                               ng_ref[...])
        y_ref[...] = y.astype(y_ref.dtype)
        for j in range(_PAIRS):
            h_scr[j] = h_next[j]

    small = pl.BlockSpec((1, LANES), lambda i: (0, 0))
    res, landed = _pcall(
        body, name=name, grid=(nc,),
        in_specs=[pl.BlockSpec((CHUNK, CONV_DIM), lambda i: (i, 0)),
                  pl.BlockSpec((HALO, CONV_DIM), lambda i: (jnp.maximum(i * _HALO_PER_CHUNK - 1, 0), 0)),
                  pl.BlockSpec((CHUNK, D_MODEL), lambda i: (i, CONV_DIM // D_MODEL)),
                  pl.BlockSpec((CHUNK, LANES), lambda i: (i, _DT_BLOCK)),
                  pl.BlockSpec(memory_space=pl.ANY),
                  pl.BlockSpec((SSM_CONV, CONV_DIM), lambda i: (0, 0)),
                  pl.BlockSpec((1, CONV_DIM), lambda i: (0, 0)),
                  small, small, small, pl.BlockSpec((1, D_MODEL), lambda i: (0, 0))],
        out_specs=[pl.BlockSpec((CHUNK, D_MODEL), lambda i: (i, 1)),
                   pl.BlockSpec((1, _PAIRS, SSM_STATE, LANES), lambda i: (i, 0, 0, 0)),
                   pl.BlockSpec((CHUNK, CONV_DIM), lambda i: (i, 0))],
        out_shape=[jax.ShapeDtypeStruct((t, 2 * D_MODEL), BF16),
                   jax.ShapeDtypeStruct((nc, _PAIRS, SSM_STATE, LANES), F32),
                   jax.ShapeDtypeStruct((t, CONV_DIM), F32)],
        scratch_shapes=[pltpu.VMEM((_PAIRS, SSM_STATE, LANES), F32)],
        semantics=("arbitrary",), moves=moves, aliases={4: 0},
    )(proj_rest, proj_rest, proj_rest, proj_rest, mix, conv_w, conv_b, dt_bias, a_log, d_skip, norm_g)
    return (res, landed) if moves else res


def _ssd_bwd(proj_rest, pre, h_states, d_mix, dt_bias, a_log, d_skip, norm_g, *, name, moves=()):
    t = proj_rest.shape[0]
    nc = t // CHUNK

    def body(pre_ref, z_ref, dt_ref, hs_ref, dy_ref, dtb_ref, al_ref, ds_ref, ng_ref,
             dpre_ref, dz_ref, ddt_ref, ddtb_ref, dal_ref, dds_ref, dng_ref, dh_scr):
        i = pl.program_id(0)

        @pl.when(i == 0)
        def _():
            dh_scr[...] = jnp.zeros_like(dh_scr)
            ddtb_ref[...] = jnp.zeros_like(ddtb_ref)
            dal_ref[...] = jnp.zeros_like(dal_ref)
            dds_ref[...] = jnp.zeros_like(dds_ref)
            dng_ref[...] = jnp.zeros_like(dng_ref)

        h_prev = tuple(hs_ref[0, j] for j in range(_PAIRS))
        _, vjp = jax.vjp(_ssd_chunk, pre_ref[...], z_ref[...], dt_ref[...], h_prev, dtb_ref[...], al_ref[...],
                         ds_ref[...], ng_ref[...])
        dpre, dz, ddt, dh_prev, ddtb, dal, dds, dng = vjp((dy_ref[...], tuple(dh_scr[j] for j in range(_PAIRS))))
        dpre_ref[...] = dpre
        dz_ref[...] = dz.astype(dz_ref.dtype)
        ddt_ref[...] = ddt.astype(ddt_ref.dtype)
        for j in range(_PAIRS):
            dh_scr[j] = dh_prev[j]
        ddtb_ref[...] += ddtb
        dal_ref[...] += dal
        dds_ref[...] += dds
        dng_ref[...] += dng

    rev = lambda i: nc - 1 - i
    small = pl.BlockSpec((1, LANES), lambda i: (0, 0))
    wide = pl.BlockSpec((1, D_MODEL), lambda i: (0, 0))
    res, landed = _pcall(
        body, name=name, grid=(nc,),
        in_specs=[pl.BlockSpec((CHUNK, CONV_DIM), lambda i: (rev(i), 0)),
                  pl.BlockSpec((CHUNK, D_MODEL), lambda i: (rev(i), CONV_DIM // D_MODEL)),
                  pl.BlockSpec((CHUNK, LANES), lambda i: (rev(i), _DT_BLOCK)),
                  pl.BlockSpec((1, _PAIRS, SSM_STATE, LANES), lambda i: (rev(i), 0, 0, 0)),
                  pl.BlockSpec((CHUNK, D_MODEL), lambda i: (rev(i), 1)),
                  small, small, small, wide],
        out_specs=[pl.BlockSpec((CHUNK, CONV_DIM), lambda i: (rev(i), 0)),
                   pl.BlockSpec((CHUNK, D_MODEL), lambda i: (rev(i), 0)),
                   pl.BlockSpec((CHUNK, LANES), lambda i: (rev(i), 0)),
                   small, small, small, wide],
        out_shape=[jax.ShapeDtypeStruct((t, CONV_DIM), F32), jax.ShapeDtypeStruct((t, D_MODEL), BF16),
                   jax.ShapeDtypeStruct((t, LANES), BF16),
                   jax.ShapeDtypeStruct((1, LANES), F32), jax.ShapeDtypeStruct((1, LANES), F32),
                   jax.ShapeDtypeStruct((1, LANES), F32), jax.ShapeDtypeStruct((1, D_MODEL), F32)],
        scratch_shapes=[pltpu.VMEM((_PAIRS, SSM_STATE, LANES), F32)],
        semantics=("arbitrary",), moves=moves,
    )(pre, proj_rest, proj_rest, h_states, d_mix, dt_bias, a_log, d_skip, norm_g)
    return (res, landed) if moves else res


def _conv_bwd(proj_rest, dpre, dz, ddt, conv_w, *, name, tb=256, moves=()):
    t = proj_rest.shape[0]
    nb = t // tb
    per = tb // HALO

    def body(x_ref, prev_ref, dpre_ref, next_ref, dz_ref, ddt_ref, cw_ref, drest_ref, dcw_ref, dcb_ref):
        i = pl.program_id(0)

        @pl.when(i == 0)
        def _():
            dcw_ref[...] = jnp.zeros_like(dcw_ref)
            dcb_ref[...] = jnp.zeros_like(dcb_ref)

        x = x_ref[...]
        dp = dpre_ref[...]
        w = cw_ref[...]
        prev8 = jnp.where(i == 0, 0.0, prev_ref[...])
        next8 = jnp.where(i == nb - 1, 0.0, next_ref[...])
        dx = dp * w[SSM_CONV - 1:SSM_CONV]
        for j in range(SSM_CONV - 1):
            dx = dx + _shift_up(dp, next8, SSM_CONV - 1 - j) * w[j:j + 1]
        drest_ref[:, :CONV_DIM] = dx.astype(drest_ref.dtype)
        drest_ref[:, CONV_DIM:CONV_DIM + D_MODEL] = dz_ref[...].astype(drest_ref.dtype)
        drest_ref[:, CONV_DIM + D_MODEL:] = ddt_ref[...].astype(drest_ref.dtype)
        for j in range(SSM_CONV):
            dcw_ref[j:j + 1, :] += jnp.sum(dp * _shift_down(prev8, x, SSM_CONV - 1 - j), axis=0, keepdims=True)
        dcb_ref[...] += jnp.sum(dp, axis=0, keepdims=True)

    res, landed = _pcall(
        body, name=name, grid=(nb,),
        in_specs=[pl.BlockSpec((tb, CONV_DIM), lambda i: (i, 0)),
                  pl.BlockSpec((HALO, CONV_DIM), lambda i: (jnp.maximum(i * per - 1, 0), 0)),
                  pl.BlockSpec((tb, CONV_DIM), lambda i: (i, 0)),
                  pl.BlockSpec((HALO, CONV_DIM), lambda i: (jnp.minimum((i + 1) * per, nb * per - 1), 0)),
                  pl.BlockSpec((tb, D_MODEL), lambda i: (i, 0)),
                  pl.BlockSpec((tb, LANES), lambda i: (i, 0)),
                  pl.BlockSpec((SSM_CONV, CONV_DIM), lambda i: (0, 0))],
        out_specs=[pl.BlockSpec((tb, REST_W), lambda i: (i, 0)),
                   pl.BlockSpec((SSM_CONV, CONV_DIM), lambda i: (0, 0)),
                   pl.BlockSpec((1, CONV_DIM), lambda i: (0, 0))],
        out_shape=[jax.ShapeDtypeStruct((t, REST_W), BF16), jax.ShapeDtypeStruct((SSM_CONV, CONV_DIM), F32),
                   jax.ShapeDtypeStruct((1, CONV_DIM), F32)],
        semantics=("arbitrary",), moves=moves,
    )(proj_rest, proj_rest, dpre, dpre, dz, ddt, conv_w)
    return (res, landed) if moves else res


_KV_BLOCK = D_MODEL // (2 * LANES)
_SINK_ROWS = _PAIRS_PER_KV * CHUNK


def _stack_pairs(ref, kv_head):
    base = kv_head * _PAIRS_PER_KV
    return jnp.concatenate([ref[:, (base + p) * LANES:(base + p + 1) * LANES] for p in range(_PAIRS_PER_KV)], axis=0)


def _attn_fwd(qkv, sinks, *, name, moves=()):
    t = qkv.shape[0]
    nb = t // CHUNK

    def body(q_ref, kvp_ref, kvc_ref, s_ref, o_ref):
        first = pl.program_id(0) == 0
        for j in range(ATTN_KV):
            o = _attn_pairs(_stack_pairs(q_ref, j), kvp_ref[...], kvc_ref[...], s_ref[j, 0], s_ref[j, 1], first, j)
            for p in range(_PAIRS_PER_KV):
                col = (j * _PAIRS_PER_KV + p) * LANES
                o_ref[:, col:col + LANES] = o[p * CHUNK:(p + 1) * CHUNK].astype(o_ref.dtype)

    return _unpack(_pcall(
        body, name=name, grid=(nb,),
        in_specs=[pl.BlockSpec((CHUNK, D_MODEL), lambda i: (i, 0)),
                  pl.BlockSpec((CHUNK, 2 * LANES), lambda i: (jnp.maximum(i - 1, 0), _KV_BLOCK)),
                  pl.BlockSpec((CHUNK, 2 * LANES), lambda i: (i, _KV_BLOCK)),
                  pl.BlockSpec((ATTN_KV, 2, _SINK_ROWS, 1), lambda i: (0, 0, 0, 0))],
        out_specs=[pl.BlockSpec((CHUNK, D_MODEL), lambda i: (i, 0))],
        out_shape=[jax.ShapeDtypeStruct((t, D_MODEL), BF16)],
        semantics=("parallel",), moves=moves,
    )(qkv, qkv, qkv, sinks), moves, 1)


def _attn_bwd(qkv, sinks, d_o, *, name, moves=()):
    t = qkv.shape[0]
    nb = t // CHUNK

    def body(q_ref, kvp_ref, kvc_ref, s_ref, do_ref, dqkv_ref, ds_ref, dkv_scr):
        i = pl.program_id(0)
        first = i == nb - 1

        @pl.when(i == 0)
        def _():
            dkv_scr[...] = jnp.zeros_like(dkv_scr)
            ds_ref[...] = jnp.zeros_like(ds_ref)

        dkv_cur = dkv_scr[...]
        dkv_prev = jnp.zeros_like(dkv_cur)
        for j in range(ATTN_KV):
            _, vjp = jax.vjp(functools.partial(_attn_pairs, first=first, kv_head=j), _stack_pairs(q_ref, j),
                             kvp_ref[...], kvc_ref[...], s_ref[j, 0], s_ref[j, 1])
            dq4, dkvp, dkvc, ds_lo, ds_hi = vjp(_stack_pairs(do_ref, j))
            for p in range(_PAIRS_PER_KV):
                col = (j * _PAIRS_PER_KV + p) * LANES
                dqkv_ref[:, col:col + LANES] = dq4[p * CHUNK:(p + 1) * CHUNK]
            dkv_cur = dkv_cur + dkvc
            dkv_prev = dkv_prev + dkvp
            ds_ref[j, 0] += ds_lo
            ds_ref[j, 1] += ds_hi
        dqkv_ref[:, D_MODEL:] = dkv_cur
        dkv_scr[...] = dkv_prev

    cur = lambda i: (nb - 1 - i, 0)
    sk = pl.BlockSpec((ATTN_KV, 2, _SINK_ROWS, 1), lambda i: (0, 0, 0, 0))
    res, landed = _pcall(
        body, name=name, grid=(nb,),
        in_specs=[pl.BlockSpec((CHUNK, D_MODEL), cur),
                  pl.BlockSpec((CHUNK, 2 * LANES), lambda i: (jnp.maximum(nb - 2 - i, 0), _KV_BLOCK)),
                  pl.BlockSpec((CHUNK, 2 * LANES), lambda i: (nb - 1 - i, _KV_BLOCK)),
                  sk, pl.BlockSpec((CHUNK, D_MODEL), cur)],
        out_specs=[pl.BlockSpec((CHUNK, QKV_DIM), cur), sk],
        out_shape=[jax.ShapeDtypeStruct((t, QKV_DIM), F32), jax.ShapeDtypeStruct((ATTN_KV, 2, _SINK_ROWS, 1), F32)],
        scratch_shapes=[pltpu.VMEM((CHUNK, 2 * LANES), F32)],
        semantics=("arbitrary",), moves=moves,
    )(qkv, qkv, qkv, sinks, d_o)
    return (res, landed) if moves else res


def _adamw(parts, w, m, v, *, name, tb=512, moves=()):
    layers, r, c = w.shape
    n = parts[0].shape[0]
    tb = min(tb, r)
    assert r % tb == 0 and len(parts) == layers, (name, r, tb)
    nb = r // tb

    def body(*refs):
        p_refs = refs[:layers]
        w_ref, m_ref, v_ref, g_ref, d_ref, nm_ref, nv_ref = refs[layers:]
        for layer in range(layers):
            @pl.when(pl.program_id(0) == layer)
            def _(p_ref=p_refs[layer]):
                g = p_ref[0].astype(F32)
                for s in range(1, n):
                    g = g + p_ref[s].astype(F32)
                m_new = ADAM_B1 * m_ref[...] + (1.0 - ADAM_B1) * g
                v_new = ADAM_B2 * v_ref[...] + (1.0 - ADAM_B2) * jnp.square(g)
                m_hat = m_new / (1.0 - ADAM_B1 ** ADAM_STEP)
                v_hat = v_new / (1.0 - ADAM_B2 ** ADAM_STEP)
                g_ref[...] = g
                d_ref[...] = -ADAM_LR * (m_hat / (jnp.sqrt(v_hat) + ADAM_EPS) + ADAM_WD * w_ref[...])
                nm_ref[...] = m_new
                nv_ref[...] = v_new

    part_spec = lambda layer: pl.BlockSpec(
        (n, tb, c), lambda l, i: (0, jnp.clip(i + (l - layer) * nb, 0, nb - 1), 0))
    blk = pl.BlockSpec((None, tb, c), lambda l, i: (l, i, 0))
    res, landed = _pcall(
        body, name=name, grid=(layers, nb),
        in_specs=[part_spec(layer) for layer in range(layers)] + [blk, blk, blk],
        out_specs=[blk] * 4,
        out_shape=[jax.ShapeDtypeStruct((layers, r, c), F32)] * 4,
        semantics=("arbitrary", "arbitrary"), moves=moves,
    )(*parts, w, m, v)
    return (res, landed) if moves else res


def _as_rows(a):
    flat = a.reshape(-1)
    pad = (-flat.shape[0]) % PACK_W
    if pad:
        flat = jnp.pad(flat, (0, pad))
    return flat.reshape(-1, PACK_W)


def _cols_from_shards(g):
    return jnp.transpose(g, (1, 0, 2)).reshape(g.shape[1], -1)


def _cols_to_shards(a):
    return jnp.transpose(a.reshape(a.shape[0], N_DEV, -1), (1, 0, 2))


def _shard_cols(shards, lo, hi):
    c = shards.shape[2]
    pieces = []
    for j in range(shards.shape[0]):
        a, b = max(lo, j * c), min(hi, (j + 1) * c)
        if a < b:
            pieces.append(shards[j, :, a - j * c:b - j * c])
    return pieces


def _cols_of(sources, lo, hi):
    pieces = []
    for arr, col0, first, last in sources:
        a, b = max(lo, first), min(hi, last)
        if a < b:
            pieces.append(arr[:, col0 + a - first:col0 + b - first])
    return pieces


def _pad_lanes(a):
    return jnp.pad(a, ((0, 0), (0, LANES - a.shape[1])))


def kernel(x, norm_mix_g, norm_mlp_g, final_norm_g, w_in_even, w_out_even, gm_ln_g, gm_ln_b, gm_w_s, gm_b_s, ssm_conv_w, ssm_conv_b, ssm_dt_bias, ssm_a_log, ssm_d, ssm_norm_g, w_qkv, b_qkv, w_o, b_o, attn_sinks, w_up, w_down, loss_target, m_norm_mix_g, m_norm_mlp_g, m_final_norm_g, m_w_in_even, m_w_out_even, m_gm_ln_g, m_gm_ln_b, m_gm_w_s, m_gm_b_s, m_ssm_conv_w, m_ssm_conv_b, m_ssm_dt_bias, m_ssm_a_log, m_ssm_d, m_ssm_norm_g, m_w_qkv, m_b_qkv, m_w_o, m_b_o, m_attn_sinks, m_w_up, m_w_down, v_norm_mix_g, v_norm_mlp_g, v_final_norm_g, v_w_in_even, v_w_out_even, v_gm_ln_g, v_gm_ln_b, v_gm_w_s, v_gm_b_s, v_ssm_conv_w, v_ssm_conv_b, v_ssm_dt_bias, v_ssm_a_log, v_ssm_d, v_ssm_norm_g, v_w_qkv, v_b_qkv, v_w_o, v_b_o, v_attn_sinks, v_w_up, v_w_down):
    names = ["norm_mix_g", "norm_mlp_g", "final_norm_g", "w_in_even", "w_out_even", "gm_ln_g", "gm_ln_b", "gm_w_s",
             "gm_b_s", "ssm_conv_w", "ssm_conv_b", "ssm_dt_bias", "ssm_a_log", "ssm_d", "ssm_norm_g", "w_qkv",
             "b_qkv", "w_o", "b_o", "attn_sinks", "w_up", "w_down"]
    env = locals()
    W = {n: env[n] for n in names}
    M = {n: env["m_" + n] for n in names}
    V = {n: env["v_" + n] for n in names}
    big = ["w_in_even", "w_out_even", "w_qkv", "w_o", "w_up", "w_down"]
    small_sharded = ["ssm_conv_w", "b_qkv", "b_o"]
    replicated = [n for n in names if n not in big and n not in small_sharded]
    me = 4 * lax.axis_index("x") + 2 * lax.axis_index("y") + lax.axis_index("c")
    t = x.shape[1]
    xs = x.reshape(t, D_MODEL)
    target = loss_target.reshape(t, D_MODEL)
    gather = lambda a: _Move("gather", a)
    over_ici = lambda a: _Move("gather_ici", a)
    over_d2d = lambda a: _Move("gather_d2d", a)
    by_core = lambda a: a.reshape((N_CHIP, N_CORE) + a.shape[1:])
    to_sibling = lambda a: [_Move("scatter_d2d", by_core(a))]
    my_core = lax.axis_index("c").astype(jnp.int32).reshape(1)
    pair = lambda a, theirs, name: _pair_add(by_core(a), theirs, my_core, name=name)
    to_chips = lambda a: _Move("scatter_ici", a)
    whole = lambda a: a.reshape((N_DEV,) + a.shape[2:])
    row = lambda a: a.reshape(1, D_MODEL)

    small_flat = jnp.concatenate([W[n].reshape(-1) for n in small_sharded])
    w_in_g, small_g = _exchange([over_ici(w_in_even[0].astype(BF16)), gather(_as_rows(small_flat))],
                                name="gather_w_in", then_d2d=[0])
    w_in_s = whole(w_in_g)
    z_lo, xbc_lo, dt_lo = 2 * D_MODEL, 3 * D_MODEL, 3 * D_MODEL + CONV_DIM
    w_uv = jnp.concatenate(_shard_cols(w_in_s, 0, z_lo), axis=1)
    w_rest = jnp.concatenate(_shard_cols(w_in_s, xbc_lo, dt_lo) + _shard_cols(w_in_s, z_lo, xbc_lo)
                             + _shard_cols(w_in_s, dt_lo, IN_EVEN)
                             + [jnp.zeros((D_MODEL, LANES - SSM_HEADS), BF16)], axis=1)
    small_all = small_g.reshape(N_DEV, -1)
    n_cw = SSM_CONV * CONV_DIM // N_DEV
    n_bq = QKV_DIM // N_DEV
    conv_w = _cols_from_shards(small_all[:, :n_cw].reshape(N_DEV, SSM_CONV, CONV_DIM // N_DEV))
    bqkv = small_all[:, n_cw:n_cw + n_bq].reshape(1, QKV_DIM)
    bo = small_all[:, n_cw + n_bq:n_cw + n_bq + D_MODEL // N_DEV].reshape(1, D_MODEL)

    conv_b = ssm_conv_b.reshape(1, CONV_DIM)
    dt_bias, a_log, d_skip = _pad_lanes(ssm_dt_bias), _pad_lanes(ssm_a_log), _pad_lanes(ssm_d)
    gm_w = gm_w_s[0]
    gm_b = gm_b_s[0].reshape(GM_GROUPS, CHUNK, 1)
    sink_rows = jnp.repeat(jnp.transpose(attn_sinks.reshape(ATTN_KV, _PAIRS_PER_KV, 2), (0, 2, 1)), CHUNK,
                           axis=2).reshape(ATTN_KV, 2, _SINK_ROWS, 1)
    w_up_b, w_down_b = w_up.astype(BF16), w_down.astype(BF16)

    w_down0_a, w_down0_b = w_down_b[0, :FF_SHARD // 2], w_down_b[0, FF_SHARD // 2:]
    (y0, proj_uv), (w_qkv_g,) = _norm_matmul(xs, row(norm_mix_g[0]), w_uv, name="proj_uv", emit_y=True,
                                             moves=[over_ici(w_qkv[0].astype(BF16))])
    proj_rest, (w_out_g,) = _norm_matmul(xs, row(norm_mix_g[0]), w_rest, name="proj_rest", emit_y=False,
                                         moves=[over_ici(w_out_even[0].astype(BF16))])
    mix, (w_o_g, w_down0_a, w_out_g, w_qkv_g) = _gmlp_fwd(
        proj_uv, gm_ln_g, gm_ln_b, gm_w, gm_b, name="gmlp_fwd",
        moves=[over_ici(w_o[0].astype(BF16)), over_ici(w_down0_a), over_d2d(w_out_g), over_d2d(w_qkv_g)])
    (mix, h_states, conv_pre), (w_up0_g, w_o_g, w_down0_a) = _ssd_fwd(
        proj_rest, mix, conv_w, conv_b, dt_bias, a_log, d_skip, ssm_norm_g, name="ssd_fwd",
        moves=[over_ici(w_up_b[0]), over_d2d(w_o_g), over_d2d(w_down0_a)])
    w_out_f = whole(w_out_g).reshape(2 * D_MODEL, D_MODEL)
    (h1, y1), (w_down0_b, w_up0_g) = _residual_matmul(
        mix, w_out_f, xs, name="mix_out", norm_g=row(norm_mlp_g[0]),
        moves=[over_ici(w_down0_b), over_d2d(w_up0_g)])
    up0, (w_down0_b,) = _mlp_up(y1, whole(w_up0_g), name="mlp_up0", moves=[over_d2d(w_down0_b)])
    w_down_g = [[whole(w_down0_a), whole(w_down0_b)]]
    h2, y2 = _mlp_down(up0, w_down_g[0], h1, name="mlp_down0", norm_g=row(norm_mix_g[1]))
    wqkv = _cols_from_shards(whole(w_qkv_g))
    wo = whole(w_o_g).reshape(D_MODEL, D_MODEL)
    qkv = _residual_matmul(y2, wqkv, None, name="qkv", bias=bqkv)
    attn, (w_up1_g, w_down1_g) = _attn_fwd(qkv, sink_rows, name="attn_fwd",
                                           moves=[over_ici(w_up_b[1]), over_ici(w_down_b[1])])
    (h3, y3), (w_up1_g,) = _residual_matmul(attn, wo, h2, name="attn_out", bias=bo, norm_g=row(norm_mlp_g[1]),
                                            moves=[over_d2d(w_up1_g)])
    w_up_g = [whole(w_up0_g), whole(w_up1_g)]
    up1, (w_down1_g,) = _mlp_up(y3, w_up_g[1], name="mlp_up1", moves=[over_d2d(w_down1_g)])
    w_down_g.append([whole(w_down1_g)])
    loss_part, dh4, dh4_b, d_final_g = _mlp_down_loss(up1, w_down_g[1], h3, row(final_norm_g), target,
                                                      name="mlp_down1_loss")

    by_dev_rows = lambda a: a.reshape((N_DEV, a.shape[0] // N_DEV) + a.shape[1:])

    def mlp_bwd(dh, dh_b, h, y, up, layer, first_moves=()):
        res = _mlp_down_dx(dh_b, w_down_g[layer], up, name=f"mlp_down_dx{layer}", moves=first_moves)
        d_up, first_landed = res if first_moves else (res, [])
        g_down = _dw_by_rows(up, dh_b, name=f"mlp_down_dw{layer}", tk=FF_SHARD, square_relu=True)
        g_down = by_dev_rows(g_down)
        g_up, (theirs,) = _dw_by_cols(y, d_up, name=f"mlp_up_dw{layer}", tn=FF_SHARD, by_device=True,
                                      moves=to_sibling(g_down))
        q_down = pair(g_down, theirs, f"mlp_down_pair{layer}")
        (dh_new, dh_new_b, dg, cs), (r_down, theirs) = _dx_norm(
            d_up, w_up_g[layer], h, row(norm_mlp_g[layer]), dh, name=f"mlp_up_dx{layer}", by_device_cols=True,
            moves=[to_chips(q_down)] + to_sibling(g_up))
        q_up = pair(g_up, theirs, f"mlp_up_pair{layer}")
        return dh_new, dh_new_b, cs, dg, q_up, r_down, first_landed

    dh3, dh3_b, cs3, g_nmlp1, q_up1, r_down1, _ = mlp_bwd(dh4, dh4_b, h3, y3, up1, 1)
    g_bo = cs3
    g_wo = by_dev_rows(_dw_by_cols(attn, dh3_b, name="attn_out_dw", tn=FF_SHARD))
    d_attn, (theirs,) = _dx(dh3_b, wo, name="attn_out_dx", moves=to_sibling(g_wo))
    q_wo = pair(g_wo, theirs, "attn_out_pair")
    (dqkv, d_sink), (r_up1, r_wo) = _attn_bwd(qkv, sink_rows, d_attn, name="attn_bwd",
                                              moves=[to_chips(q_up1), to_chips(q_wo)])
    g_bqkv = _colsum(dqkv, name="qkv_db")
    g_wqkv = _cols_to_shards(_dw_by_cols(y2, dqkv, name="qkv_dw", tn=QKV_DIM // 2))
    (dh2, dh2_b, g_nmix1, _), (theirs,) = _dx_norm(dqkv, wqkv, h2, row(norm_mix_g[1]), dh3, name="qkv_dx",
                                                   moves=to_sibling(g_wqkv))
    q_wqkv = pair(g_wqkv, theirs, "qkv_pair")
    dh1, dh1_b, _, g_nmlp0, q_up0, r_down0, (r_wqkv,) = mlp_bwd(dh2, dh2_b, h1, y1, up0, 0,
                                                                first_moves=[to_chips(q_wqkv)])

    d_mix = _dx(dh1_b, w_out_f, name="mix_out_dx")
    g_wout = by_dev_rows(_dw_by_rows(mix, dh1_b, name="mix_out_dw", tk=FF_SHARD))
    (d_uv, g_ln_g, g_ln_b, g_gm_w, g_gm_b), (r_up0, theirs) = _gmlp_bwd(
        proj_uv, d_mix, gm_ln_g, gm_ln_b, gm_w, gm_b, name="gmlp_bwd", moves=[to_chips(q_up0)] + to_sibling(g_wout))
    q_wout = pair(g_wout, theirs, "mix_out_pair")

    early = [("norm_mlp_g", None), ("final_norm_g", None), ("norm_mix_g", 1), ("gm_ln_g", None), ("gm_ln_b", None),
             ("gm_w_s", None), ("gm_b_s", None), ("attn_sinks", None)]
    late = [("norm_mix_g", 0), ("ssm_conv_b", None), ("ssm_dt_bias", None), ("ssm_a_log", None), ("ssm_d", None),
            ("ssm_norm_g", None)]
    early_sharded, late_sharded = ["b_qkv", "b_o"], ["ssm_conv_w"]
    small_grads = {
        ("norm_mlp_g", None): jnp.concatenate([g_nmlp0, g_nmlp1], axis=0),
        ("final_norm_g", None): d_final_g, ("norm_mix_g", 1): g_nmix1,
        ("gm_ln_g", None): g_ln_g, ("gm_ln_b", None): g_ln_b, ("gm_w_s", None): g_gm_w, ("gm_b_s", None): g_gm_b,
        ("attn_sinks", None): jnp.transpose(
            jnp.sum(d_sink.reshape(ATTN_KV, 2, _PAIRS_PER_KV, CHUNK), axis=3), (0, 2, 1)),
        "b_qkv": g_bqkv, "b_o": g_bo,
    }
    pack = lambda keys: _as_rows(jnp.concatenate([small_grads[key].reshape(-1) for key in keys]))
    (dpre, dz, ddt, g_dtb, g_alog, g_dskip, g_ssm_ng), (r_wout, early_recv) = _ssd_bwd(
        proj_rest, conv_pre, h_states, d_mix, dt_bias, a_log, d_skip, ssm_norm_g, name="ssd_bwd",
        moves=[to_chips(q_wout), gather(pack(early + early_sharded))])
    d_rest, g_conv_w, g_conv_b = _conv_bwd(proj_rest, dpre, dz, ddt, conv_w, name="conv_bwd")
    g_w_uv = _dw_by_cols(y0, d_uv, name="proj_uv_dw", tn=FF_SHARD)
    g_w_rest = _dw_by_cols(y0, d_rest, name="proj_rest_dw", tn=REST_W // 5)
    in_cols = [(g_w_uv, 0, 0, z_lo), (g_w_rest, CONV_DIM, z_lo, xbc_lo), (g_w_rest, 0, xbc_lo, dt_lo),
               (g_w_rest, CONV_DIM + D_MODEL, dt_lo, IN_EVEN)]
    in_shard = IN_EVEN // N_DEV
    g_w_in = jnp.stack([jnp.concatenate(_cols_of(in_cols, j * in_shard, (j + 1) * in_shard), axis=1)
                        for j in range(N_DEV)])
    dy0, (theirs,) = _dx(d_uv, w_uv, name="proj_uv_dx", moves=to_sibling(g_w_in))
    q_w_in = pair(g_w_in, theirs, "proj_pair")
    quarter = D_MODEL // 4
    in_part = lambda k: _Move("scatter_ici", q_w_in, rows=(k * quarter, quarter))
    (dx, _, g_nmix0, _), r_w_in = _dx_norm(d_rest, w_rest, xs, row(norm_mix_g[0]), dh1, name="proj_rest_dx",
                                           partial=dy0, moves=[in_part(0), in_part(1)])
    small_grads.update({
        ("loss", None): loss_part[:1, :1],
        ("norm_mix_g", 0): g_nmix0, ("ssm_conv_b", None): g_conv_b,
        ("ssm_dt_bias", None): g_dtb[:, :SSM_HEADS], ("ssm_a_log", None): g_alog[:, :SSM_HEADS],
        ("ssm_d", None): g_dskip[:, :SSM_HEADS], ("ssm_norm_g", None): g_ssm_ng, "ssm_conv_w": g_conv_w,
    })


    def update(n, parts, moves=()):
        shape = W[n].shape
        as3 = lambda a: a.reshape((len(parts),) + parts[0].shape[1:])
        res = _adamw(parts, as3(W[n]), as3(M[n]), as3(V[n]), name="adamw_" + n, moves=moves)
        res, landed = res if moves else (res, [])
        return [a.reshape(shape) for a in res], landed

    out = {}
    late_keys = late + late_sharded + [("loss", None)]
    out["w_o"], (late_recv,) = update("w_o", [r_wo], moves=[gather(pack(late_keys))])
    out["w_down"], _ = update("w_down", [r_down0, r_down1])
    out["w_up"], _ = update("w_up", [r_up0, r_up1])
    out["w_out_even"], (r_w_in_2,) = update("w_out_even", [r_wout], moves=[in_part(2)])
    out["w_qkv"], (r_w_in_3,) = update("w_qkv", [r_wqkv], moves=[in_part(3)])
    out["w_in_even"], _ = update("w_in_even", list(r_w_in) + [r_w_in_2, r_w_in_3])

    def unpacked(recv, keys):
        flat, res, o = recv.reshape(N_DEV, -1), {}, 0
        for key in keys:
            res[key] = flat[:, o:o + small_grads[key].size]
            o += small_grads[key].size
        return res

    arrived = {**unpacked(early_recv, early + early_sharded), **unpacked(late_recv, late_keys)}
    piece = lambda tree, key: tree[key[0]] if key[1] is None else tree[key[0]][key[1]]

    def rows_by_device(cat):
        pad = (-cat.shape[1]) % PACK_W
        return jnp.pad(cat, ((0, 0), (0, pad))).reshape(N_DEV, -1, PACK_W)

    rep_keys = early + late
    rep_parts = rows_by_device(jnp.concatenate([arrived[key] for key in rep_keys], axis=1))
    flat_rep = lambda tree: _as_rows(jnp.concatenate([piece(tree, key).reshape(-1) for key in rep_keys]))[None]
    rep_res = _adamw([rep_parts], flat_rep(W), flat_rep(M), flat_rep(V), name="adamw_replicated")
    sh_keys = early_sharded + late_sharded
    shard_parts = []
    for n in sh_keys:
        full = arrived[n].reshape((N_DEV,) + small_grads[n].shape)
        c = full.shape[-1] // N_DEV
        shard_parts.append(lax.dynamic_slice_in_dim(full, me * c, c, axis=full.ndim - 1).reshape(N_DEV, -1))
    sh_rows = rows_by_device(jnp.concatenate(shard_parts, axis=1))
    flat_sh = lambda tree: _as_rows(jnp.concatenate([tree[n].reshape(-1) for n in sh_keys]))[None]
    sh_res = _adamw([sh_rows], flat_sh(W), flat_sh(M), flat_sh(V), name="adamw_small_sharded")

    def unpack_replicated(rows):
        flat, vals, o = rows.reshape(-1), {}, 0
        for key in rep_keys:
            size = piece(W, key).size
            vals[key] = flat[o:o + size]
            o += size
        res = {}
        for n in replicated:
            if (n, None) in vals:
                res[n] = vals[(n, None)].reshape(W[n].shape)
            else:
                res[n] = jnp.stack([vals[(n, r)] for r in range(W[n].shape[0])]).reshape(W[n].shape)
        return res

    def unpack_sharded(rows):
        flat, res, o = rows.reshape(-1), {}, 0
        for n in sh_keys:
            res[n] = flat[o:o + W[n].size].reshape(W[n].shape)
            o += W[n].size
        return res

    results = []
    for idx in range(4):
        d = {n: out[n][idx] for n in big}
        d.update(unpack_replicated(rep_res[idx]))
        d.update(unpack_sharded(sh_res[idx]))
        results.append(d)

    loss = jnp.sum(arrived[("loss", None)])
    grad_x = dx.reshape(x.shape)
    final = [loss, grad_x]
    for d in results:
        final.extend(d[n] for n in names)
    return tuple(final)
```

```python
import dataclasses
import functools

import jax
import jax.numpy as jnp
from jax import lax
from jax.experimental import pallas as pl
from jax.experimental.pallas import tpu as pltpu

F32 = jnp.float32
BF16 = jnp.bfloat16

N_DEV = 8
D_MODEL = 1024
D_FF = 4096
RMS_EPS = 1e-5
LN_EPS = 1e-5
CHUNK = 128
GM_GROUPS = 8
SSM_HEADS = 16
SSM_HEADDIM = 64
SSM_GROUPS = 4
SSM_STATE = 128
SSM_CONV = 4
CONV_DIM = 2048
IN_EVEN = 5136
REST_W = 3200
ATTN_HEADS = 16
ATTN_KV = 2
HEAD_DIM = 64
QKV_DIM = 1280
LANES = 128
HALO = 8
PACK_W = 1024

ADAM_LR = 0.001
ADAM_B1 = 0.9
ADAM_B2 = 0.999
ADAM_EPS = 1e-08
ADAM_WD = 0.01
ADAM_STEP = 10

VMEM_LIMIT_BYTES = 56 * 1024 * 1024


_NN = (((1,), (0,)), ((), ()))
_NT = (((1,), (1,)), ((), ()))
_TN = (((0,), (0,)), ((), ()))


def _dg(a, b, dims):
    return lax.dot_general(a.astype(BF16), b.astype(BF16), dims, preferred_element_type=F32)


@jax.custom_vjp
def _nn(a, b):
    return _dg(a, b, _NN)


@jax.custom_vjp
def _nt(a, b):
    return _dg(a, b, _NT)


@jax.custom_vjp
def _tn(a, b):
    return _dg(a, b, _TN)


_nn.defvjp(lambda a, b: (_dg(a, b, _NN), (a, b)), lambda r, g: (_nt(g, r[1]), _tn(r[0], g)))
_nt.defvjp(lambda a, b: (_dg(a, b, _NT), (a, b)), lambda r, g: (_nn(g, r[1]), _tn(g, r[0])))
_tn.defvjp(lambda a, b: (_dg(a, b, _TN), (a, b)), lambda r, g: (_nt(r[1], g), _nn(r[0], g)))


def _split3_dot(tri, x):
    x1 = x.astype(BF16)
    r1 = x - x1.astype(F32)
    x2 = r1.astype(BF16)
    x3 = (r1 - x2.astype(F32)).astype(BF16)
    t = tri.astype(BF16)
    dot = lambda p: lax.dot_general(t, p, _NN, preferred_element_type=F32)
    return dot(x1) + dot(x2) + dot(x3)


def _tri(lower):
    r = lax.broadcasted_iota(jnp.int32, (CHUNK, CHUNK), 0)
    c = lax.broadcasted_iota(jnp.int32, (CHUNK, CHUNK), 1)
    return jnp.where((r >= c) if lower else (r <= c), 1.0, 0.0).astype(F32)


@jax.custom_vjp
def _cumsum_rows(x):
    return _split3_dot(_tri(True), x)


_cumsum_rows.defvjp(lambda x: (_split3_dot(_tri(True), x), None), lambda _, g: (_split3_dot(_tri(False), g),))


def _sigmoid(x):
    return 1.0 / (1.0 + jnp.exp(-x))


def _silu(x):
    return x * _sigmoid(x)


def _softplus(x):
    return jnp.maximum(x, 0.0) + jnp.log(1.0 + jnp.exp(-jnp.abs(x)))


def _gelu_tanh(x):
    return 0.5 * x * (1.0 + jnp.tanh(0.7978845608028654 * (x + 0.044715 * (x * x * x))))


def _rmsnorm(x, g):
    return x * lax.rsqrt(jnp.mean(x * x, axis=-1, keepdims=True) + RMS_EPS) * g


def _gmlp_chunk(u, v, ln_g, ln_b, w_s, b_s):
    gu = _gelu_tanh(u)
    gv = _gelu_tanh(v)
    mu = jnp.mean(gv, axis=-1, keepdims=True)
    var = jnp.mean(jnp.square(gv - mu), axis=-1, keepdims=True)
    vn = (gv - mu) * lax.rsqrt(var + LN_EPS) * ln_g + ln_b
    r = lax.broadcasted_iota(jnp.int32, (CHUNK, CHUNK), 0)
    c = lax.broadcasted_iota(jnp.int32, (CHUNK, CHUNK), 1)
    causal = r >= c
    outs = []
    for g in range(GM_GROUPS):
        cols = slice(g * LANES, (g + 1) * LANES)
        mixed = _nn(jnp.where(causal, w_s[g], 0.0), vn[:, cols]) + b_s[g]
        outs.append(gu[:, cols] * mixed)
    return jnp.concatenate(outs, axis=1)


def _lane_pick(row, h):
    lane = lax.broadcasted_iota(jnp.int32, row.shape, 1)
    return jnp.sum(jnp.where(lane == h, row, 0.0), axis=1, keepdims=True)


def _col_pick(m, h):
    lane = lax.broadcasted_iota(jnp.int32, m.shape, 1)
    return jnp.sum(jnp.where(lane == h, m, 0.0), axis=1, keepdims=True)


def _row_pick(m, h):
    sub = lax.broadcasted_iota(jnp.int32, m.shape, 0)
    return jnp.sum(jnp.where(sub == h, m, 0.0), axis=0, keepdims=True)


_PAIRS = SSM_HEADS // 2


def _ssd_chunk(pre, z, dt_raw, h_prev, dt_bias, a_log, d_skip, norm_g):
    xbc = _silu(pre)
    dt = _softplus(dt_raw + dt_bias)
    da = dt * (-jnp.exp(a_log))
    a_cum = _cumsum_rows(da)
    a_cum_t = a_cum.T
    dt_t = dt.T
    r = lax.broadcasted_iota(jnp.int32, (CHUNK, CHUNK), 0)
    c = lax.broadcasted_iota(jnp.int32, (CHUNK, CHUNK), 1)
    causal = r >= c
    lane_lo = lax.broadcasted_iota(jnp.int32, (1, LANES), 1) < SSM_HEADDIM
    last_row = lax.broadcasted_iota(jnp.int32, (CHUNK, 1), 0) == CHUNK - 1
    ys, h_next = [], []
    for j in range(_PAIRS):
        g = j // 2
        xs = xbc[:, j * LANES:(j + 1) * LANES]
        bm = xbc[:, 1024 + g * SSM_STATE:1024 + (g + 1) * SSM_STATE]
        cm = xbc[:, 1536 + g * SSM_STATE:1536 + (g + 1) * SSM_STATE]
        cb = _nt(cm, bm)
        y_diag, to_end, e_cum, c_dec, d_row = [], [], [], [], []
        for h in (2 * j, 2 * j + 1):
            col = _col_pick(a_cum, h)
            row = _row_pick(a_cum_t, h)
            dt_col = _col_pick(dt, h)
            dt_row = _row_pick(dt_t, h)
            decay = jnp.exp(jnp.where(causal, col - row, -jnp.inf))
            y_diag.append(_nn(cb * decay * dt_row, xs))
            last = jnp.sum(jnp.where(last_row, col, 0.0), axis=0, keepdims=True)
            to_end.append(jnp.exp(last - col) * dt_col)
            e_cum.append(jnp.exp(col))
            c_dec.append(jnp.exp(last))
            d_row.append(_lane_pick(d_skip, h))
        pair = lambda lo_hi: jnp.where(lane_lo, lo_hi[0], lo_hi[1])
        states = _tn(bm, xs * pair(to_end))
        y_off = _nn(cm, h_prev[j]) * pair(e_cum)
        ys.append(pair(y_diag) + y_off + xs * pair(d_row))
        h_next.append(pair(c_dec) * h_prev[j] + states)
    y = jnp.concatenate(ys, axis=1) * _silu(z)
    width = D_MODEL // SSM_GROUPS
    y = jnp.concatenate(
        [_rmsnorm(y[:, g * width:(g + 1) * width], norm_g[:, g * width:(g + 1) * width]) for g in range(SSM_GROUPS)],
        axis=1)
    return y, tuple(h_next)


def _shift_down(prev8, x, k):
    if k == 0:
        return x
    win = jnp.concatenate([prev8, x], axis=0)
    return pltpu.roll(win, k, 0)[HALO:]


def _shift_up(x, next8, k):
    if k == 0:
        return x
    n = x.shape[0]
    win = jnp.concatenate([x, next8], axis=0)
    return pltpu.roll(win, n + HALO - k, 0)[:n]


def _conv_pre(prev8, x, w, b):
    out = b + x * w[SSM_CONV - 1:SSM_CONV]
    for i in range(SSM_CONV - 1):
        out = out + _shift_down(prev8, x, SSM_CONV - 1 - i) * w[i:i + 1]
    return out


@jax.custom_vjp
def _swap_halves(x):
    return pltpu.roll(x, HEAD_DIM, 1)


_swap_halves.defvjp(lambda x: (pltpu.roll(x, HEAD_DIM, 1), None), lambda _, g: (pltpu.roll(g, HEAD_DIM, 1),))

_PAIRS_PER_KV = ATTN_HEADS // ATTN_KV // 2


def _attn_pairs(q4, kv_prev, kv_cur, sink_lo, sink_hi, first, kv_head):
    kv = jnp.concatenate([kv_prev, kv_cur], axis=0)
    lane = lax.broadcasted_iota(jnp.int32, (1, LANES), 1)
    own = (lane >= HEAD_DIM * kv_head) & (lane < HEAD_DIM * (kv_head + 1))

    def placed(pair):
        mine = jnp.where(own, pair, 0.0)
        lo = mine if kv_head == 0 else _swap_halves(mine)
        return lo, _swap_halves(lo)

    k_lo, k_hi = placed(kv[:, :LANES])
    v_lo, v_hi = placed(kv[:, LANES:])
    out = None
    for k_e, v_e, sink in ((k_lo, v_lo, sink_lo), (k_hi, v_hi, sink_hi)):
        s = _nt(q4, k_e) * (HEAD_DIM ** -0.5)
        rows = lax.broadcasted_iota(jnp.int32, s.shape, 0) & (CHUNK - 1)
        cols = lax.broadcasted_iota(jnp.int32, s.shape, 1)
        valid = (cols <= rows + CHUNK) & (cols > rows) & (cols >= CHUNK * first.astype(jnp.int32))
        s = jnp.where(valid, s, -jnp.inf)
        m = lax.stop_gradient(jnp.maximum(jnp.max(s, axis=-1, keepdims=True), sink))
        p = jnp.exp(s - m)
        denom = jnp.sum(p, axis=-1, keepdims=True) + jnp.exp(sink - m)
        o = _nn(p, v_e) / denom
        out = o if out is None else out + o
    return out


N_CHIP = 4
N_CORE = 2
_OTHER_CHIPS = (2, 4, 6)


@dataclasses.dataclass
class _Move:
    kind: str
    src: jax.Array
    rows: tuple = None

    def dst_shape(self):
        s = self.src.shape
        shape = {"gather": (N_DEV,) + s, "gather_ici": (N_CHIP, N_CORE) + s, "gather_d2d": s,
                 "scatter_d2d": (N_CHIP,) + s[2:], "scatter_ici": s}[self.kind]
        if self.rows is not None:
            shape = (shape[0], self.rows[1]) + tuple(shape[2:])
        return jax.ShapeDtypeStruct(tuple(shape), self.src.dtype)


def _peer(x, y, c, k):
    return (1 - x if k & 4 else x, 1 - y if k & 2 else y, 1 - c if k & 1 else c)


def _move_copies(moves, srcs, dsts, send_sems, recv_sems, local_sems):
    x, y, c = lax.axis_index("x"), lax.axis_index("y"), lax.axis_index("c")
    chip = 2 * x + y
    me = 2 * chip + c
    sibling = (x, y, 1 - c)
    all_chips = pl.ds(0, N_CHIP)
    local, remote = [], []

    def push(n, k, src, dst, device):
        remote.append(pltpu.make_async_remote_copy(
            src_ref=src, dst_ref=dst, send_sem=send_sems.at[n, k], recv_sem=recv_sems.at[n, k],
            device_id=device, device_id_type=pl.DeviceIdType.MESH))

    for n, mv in enumerate(moves):
        s, d = srcs[n], dsts[n]
        if mv.kind == "gather":
            local.append(pltpu.make_async_copy(s, d.at[me], local_sems.at[n]))
            for k in range(1, N_DEV):
                push(n, k - 1, s, d.at[me], _peer(x, y, c, k))
        elif mv.kind == "gather_ici":
            local.append(pltpu.make_async_copy(s, d.at[chip, c], local_sems.at[n]))
            for k in _OTHER_CHIPS:
                push(n, k - 1, s, d.at[chip, c], _peer(x, y, c, k))
        elif mv.kind == "gather_d2d":
            push(n, 0, d.at[all_chips, c], d.at[all_chips, c], sibling)
        elif mv.kind == "scatter_d2d":
            push(n, 0, s.at[all_chips, 1 - c], d, sibling)
        else:
            assert mv.kind == "scatter_ici", mv.kind
            part = (lambda r: r) if mv.rows is None else (lambda r: r.at[pl.ds(mv.rows[0], mv.rows[1])])
            local.append(pltpu.make_async_copy(part(s.at[chip]), d.at[chip], local_sems.at[n]))
            for k in _OTHER_CHIPS:
                px, py, _ = _peer(x, y, c, k)
                push(n, k - 1, part(s.at[2 * px + py]), d.at[chip], (px, py, c))
    return local, remote


def _move_aliases(moves, n_in, n_out):
    return {n_in + n: n_out + n for n, mv in enumerate(moves) if mv.kind == "gather_d2d"}


def _pcall(body, *, name, grid, in_specs, out_specs, out_shape, scratch_shapes=(), semantics=(), moves=(),
           aliases=None):
    out_shape, out_specs = list(out_shape), list(out_specs)
    in_specs = list(in_specs)
    if not moves:
        call = pl.pallas_call(
            body, name=name, grid=grid, in_specs=in_specs, out_specs=out_specs, out_shape=out_shape,
            scratch_shapes=list(scratch_shapes), input_output_aliases=aliases or {},
            compiler_params=pltpu.CompilerParams(dimension_semantics=tuple(semantics),
                                                 vmem_limit_bytes=VMEM_LIMIT_BYTES))
        return (lambda *args: (list(call(*args)), []))
    n_in, n_out, n_scr, n_mv = len(in_specs), len(out_shape), len(scratch_shapes), len(moves)
    hbm = pl.BlockSpec(memory_space=pltpu.HBM)

    def carrier(*refs):
        ins, rest = refs[:n_in], refs[n_in:]
        srcs, rest = rest[:n_mv], rest[n_mv:]
        outs, rest = rest[:n_out], rest[n_out:]
        dsts, rest = rest[:n_mv], rest[n_mv:]
        scr, (send_sems, recv_sems, local_sems) = rest[:n_scr], rest[n_scr:]
        first = functools.reduce(jnp.logical_and, [pl.program_id(d) == 0 for d in range(len(grid))])
        last = functools.reduce(jnp.logical_and, [pl.program_id(d) == grid[d] - 1 for d in range(len(grid))])

        @pl.when(first)
        def _():
            local, remote = _move_copies(moves, srcs, dsts, send_sems, recv_sems, local_sems)
            for cp in local + remote:
                cp.start()

        body(*ins, *outs, *scr)

        @pl.when(last)
        def _():
            local, remote = _move_copies(moves, srcs, dsts, send_sems, recv_sems, local_sems)
            for cp in remote + local:
                cp.wait()

    call = pl.pallas_call(
        carrier, name=name, grid=grid,
        in_specs=in_specs + [hbm] * n_mv,
        out_specs=out_specs + [hbm] * n_mv,
        out_shape=out_shape + [mv.dst_shape() for mv in moves],
        scratch_shapes=list(scratch_shapes) + [pltpu.SemaphoreType.DMA((n_mv, N_DEV - 1)),
                                               pltpu.SemaphoreType.DMA((n_mv, N_DEV - 1)),
                                               pltpu.SemaphoreType.DMA((n_mv,))],
        input_output_aliases={**(aliases or {}), **_move_aliases(moves, n_in, n_out)},
        compiler_params=pltpu.CompilerParams(dimension_semantics=("arbitrary",) * len(grid),
                                             vmem_limit_bytes=VMEM_LIMIT_BYTES))

    def run(*args):
        res = list(call(*args, *[mv.src for mv in moves]))
        return res[:n_out], res[n_out:]

    return run


def _exchange(moves, *, name, then_d2d=()):
    n_mv, n_fwd = len(moves), len(then_d2d)
    hbm = pl.BlockSpec(memory_space=pltpu.HBM)
    copies_of = {"gather": N_DEV - 1, "gather_ici": len(_OTHER_CHIPS), "gather_d2d": 1, "scatter_d2d": 1,
                 "scatter_ici": len(_OTHER_CHIPS)}
    first_copy = [sum(copies_of[mv.kind] for mv in moves[:n]) for n in range(n_mv)]

    def body(*refs):
        srcs, dsts, sems = refs[:n_mv], refs[n_mv:2 * n_mv], refs[2 * n_mv:]
        local, remote = _move_copies(moves, srcs, dsts, *sems[:3])
        for cp in local + remote:
            cp.start()
        x, y, c = lax.axis_index("x"), lax.axis_index("y"), lax.axis_index("c")
        chip, sibling = 2 * x + y, (x, y, 1 - c)
        passed, passed_on = [], set()

        def to_sibling(f, k, src, slot):
            cp = pltpu.make_async_remote_copy(src_ref=src, dst_ref=slot, send_sem=sems[3].at[f, k],
                                              recv_sem=sems[4].at[f, k], device_id=sibling,
                                              device_id_type=pl.DeviceIdType.MESH)
            cp.start()
            passed.append(cp)

        for f, n in enumerate(then_d2d):
            assert moves[n].kind == "gather_ici"
            d = dsts[n]
            to_sibling(f, 0, srcs[n], d.at[chip, c])
            for i, k in enumerate(_OTHER_CHIPS):
                remote[first_copy[n] + i].wait_recv()
                passed_on.add(first_copy[n] + i)
                px, py, _ = _peer(x, y, c, k)
                to_sibling(f, k - 1, d.at[2 * px + py, c], d.at[2 * px + py, c])
        for i, cp in enumerate(remote):
            if i in passed_on:
                cp.wait_send()
            else:
                cp.wait()
        for cp in local + passed:
            cp.wait()

    sems = [pltpu.SemaphoreType.DMA((n_mv, N_DEV - 1)), pltpu.SemaphoreType.DMA((n_mv, N_DEV - 1)),
            pltpu.SemaphoreType.DMA((n_mv,))]
    if then_d2d:
        sems += [pltpu.SemaphoreType.DMA((n_fwd, N_DEV - 1)), pltpu.SemaphoreType.DMA((n_fwd, N_DEV - 1))]
    return list(pl.pallas_call(
        body, name=name, in_specs=[hbm] * n_mv, out_specs=[hbm] * n_mv,
        out_shape=[mv.dst_shape() for mv in moves], scratch_shapes=sems,
    )(*[mv.src for mv in moves]))


TM = 512
FF_SHARD = D_FF // N_DEV


def _whole(a):
    nd = a.ndim
    return pl.BlockSpec(a.shape, lambda i: (0,) * nd)


def _rows(width, col=0):
    return pl.BlockSpec((TM, width), lambda i: (i, col))


def _acc_row(width):
    return pl.BlockSpec((1, width), lambda i: (0, 0))


def _unpack(res_landed, moves, n_out):
    res, landed = res_landed
    res = res[0] if n_out == 1 else res
    return (res, landed) if moves else res


def _norm_matmul(x, g, w, *, name, emit_y, moves=()):
    t, d = x.shape
    n = w.shape[1]

    def body(x_ref, g_ref, w_ref, *outs):
        y = _rmsnorm(x_ref[...], g_ref[...]).astype(BF16)
        if emit_y:
            outs[0][...] = y
        outs[-1][...] = lax.dot_general(y, w_ref[...], _NN, preferred_element_type=F32)

    shapes = ([jax.ShapeDtypeStruct((t, d), BF16)] if emit_y else []) + [jax.ShapeDtypeStruct((t, n), F32)]
    specs = ([_rows(d)] if emit_y else []) + [_rows(n)]
    return _unpack(_pcall(body, name=name, grid=(t // TM,), in_specs=[_rows(d), _acc_row(d), _whole(w)],
                          out_specs=specs, out_shape=shapes, semantics=("parallel",), moves=moves)(x, g, w),
                   moves, len(shapes))


def _residual_matmul(a, w, res, *, name, bias=None, norm_g=None, w_transposed=False, moves=()):
    t, k = a.shape
    n = w.shape[0 if w_transposed else 1]
    contract = _NT if w_transposed else _NN
    has_res, has_bias, has_norm = res is not None, bias is not None, norm_g is not None

    def body(a_ref, w_ref, *rest):
        rest = list(rest)
        res_ref = rest.pop(0) if has_res else None
        b_ref = rest.pop(0) if has_bias else None
        g_ref = rest.pop(0) if has_norm else None
        h = lax.dot_general(a_ref[...].astype(BF16), w_ref[...], contract, preferred_element_type=F32)
        if has_res:
            h = h + res_ref[...]
        if has_bias:
            h = h + b_ref[...]
        rest[0][...] = h
        if has_norm:
            rest[1][...] = _rmsnorm(h, g_ref[...]).astype(BF16)

    rows_in = [res] if has_res else []
    extra = ([bias] if has_bias else []) + ([norm_g] if has_norm else [])
    shapes = [jax.ShapeDtypeStruct((t, n), F32)] + ([jax.ShapeDtypeStruct((t, n), BF16)] if has_norm else [])
    return _unpack(_pcall(body, name=name, grid=(t // TM,),
                          in_specs=[_rows(k), _whole(w)] + [_rows(n)] * len(rows_in) + [_acc_row(n)] * len(extra),
                          out_specs=[_rows(n)] * len(shapes), out_shape=shapes, semantics=("parallel",),
                          moves=moves)(a, w, *rows_in, *extra), moves, len(shapes))


def _mlp_up(y, w_cols, *, name, moves=()):
    t, d = y.shape

    def body(y_ref, w_ref, up_ref):
        yv = y_ref[...]
        for j in range(N_DEV):
            up_ref[:, j * FF_SHARD:(j + 1) * FF_SHARD] = lax.dot_general(
                yv, w_ref[j], _NN, preferred_element_type=F32).astype(up_ref.dtype)

    return _unpack(_pcall(body, name=name, grid=(t // TM,), in_specs=[_rows(d), _whole(w_cols)],
                          out_specs=[_rows(D_FF)], out_shape=[jax.ShapeDtypeStruct((t, D_FF), BF16)],
                          semantics=("parallel",), moves=moves)(y, w_cols), moves, 1)


def _sq_relu(u):
    return jnp.square(jnp.maximum(u.astype(F32), 0.0))


def _down_blocks(w_refs):
    for j in range(N_DEV):
        off = j * FF_SHARD
        for w_ref in w_refs:
            yield off, w_ref.shape[1], w_ref[j]
            off += w_ref.shape[1]


def _mlp_down(up, w_rows, res, *, name, norm_g=None, moves=()):
    t = up.shape[0]
    has_norm = norm_g is not None
    n_w = len(w_rows)

    def body(up_ref, *rest):
        w_refs, res_ref, rest = rest[:n_w], rest[n_w], rest[n_w + 1:]
        h = res_ref[...]
        for off, rows, w_blk in _down_blocks(w_refs):
            act = _sq_relu(up_ref[:, off:off + rows]).astype(BF16)
            h = h + lax.dot_general(act, w_blk, _NN, preferred_element_type=F32)
        if has_norm:
            g_ref, h_ref, y_ref = rest
            y_ref[...] = _rmsnorm(h, g_ref[...]).astype(BF16)
        else:
            (h_ref,) = rest
        h_ref[...] = h

    shapes = [jax.ShapeDtypeStruct((t, D_MODEL), F32)] + ([jax.ShapeDtypeStruct((t, D_MODEL), BF16)] if has_norm else [])
    return _unpack(_pcall(body, name=name, grid=(t // TM,),
                          in_specs=[_rows(D_FF)] + [_whole(w) for w in w_rows] + [_rows(D_MODEL)]
                          + ([_acc_row(D_MODEL)] if has_norm else []),
                          out_specs=[_rows(D_MODEL)] * len(shapes), out_shape=shapes, semantics=("parallel",),
                          moves=moves)(up, *w_rows, res, *([norm_g] if has_norm else [])), moves, len(shapes))


def _mlp_down_dx(dh, w_rows, up, *, name, moves=()):
    t = up.shape[0]
    n_w = len(w_rows)

    def body(dh_ref, *rest):
        w_refs, (up_ref, o_ref) = rest[:n_w], rest[n_w:]
        dhv = dh_ref[...]
        for off, rows, w_blk in _down_blocks(w_refs):
            cols = slice(off, off + rows)
            d_act = lax.dot_general(dhv, w_blk, _NT, preferred_element_type=F32)
            o_ref[:, cols] = (d_act * (2.0 * jnp.maximum(up_ref[:, cols].astype(F32), 0.0))).astype(o_ref.dtype)

    return _unpack(_pcall(body, name=name, grid=(t // TM,),
                          in_specs=[_rows(D_MODEL)] + [_whole(w) for w in w_rows] + [_rows(D_FF)],
                          out_specs=[_rows(D_FF)], out_shape=[jax.ShapeDtypeStruct((t, D_FF), BF16)],
                          semantics=("parallel",), moves=moves)(dh, *w_rows, up), moves, 1)


def _dw_by_cols(x, dy, *, name, tn, by_device=False, moves=()):
    t, k = x.shape
    n = dy.shape[1]
    assert n % tn == 0, (name, n, tn)

    def body(x_ref, dy_ref, o_ref):
        o_ref[...] = lax.dot_general(x_ref[...].astype(BF16), dy_ref[...].astype(BF16), _TN,
                                     preferred_element_type=F32).astype(o_ref.dtype)

    if by_device:
        out_spec, out_shape = pl.BlockSpec((None, k, tn), lambda j: (j, 0, 0)), (n // tn, k, tn)
    else:
        out_spec, out_shape = pl.BlockSpec((k, tn), lambda j: (0, j)), (k, n)
    return _unpack(_pcall(body, name=name, grid=(n // tn,),
                          in_specs=[_whole(x), pl.BlockSpec((t, tn), lambda j: (0, j))],
                          out_specs=[out_spec], out_shape=[jax.ShapeDtypeStruct(out_shape, BF16)],
                          semantics=("parallel",), moves=moves)(x, dy), moves, 1)


def _dw_by_rows(x, dy, *, name, tk, square_relu=False, moves=()):
    t, k = x.shape
    n = dy.shape[1]
    assert k % tk == 0, (name, k, tk)

    def body(x_ref, dy_ref, o_ref):
        xv = _sq_relu(x_ref[...]) if square_relu else x_ref[...]
        o_ref[...] = lax.dot_general(xv.astype(BF16), dy_ref[...].astype(BF16), _TN,
                                     preferred_element_type=F32).astype(o_ref.dtype)

    return _unpack(_pcall(body, name=name, grid=(k // tk,),
                          in_specs=[pl.BlockSpec((t, tk), lambda j: (0, j)), _whole(dy)],
                          out_specs=[pl.BlockSpec((tk, n), lambda j: (j, 0))],
                          out_shape=[jax.ShapeDtypeStruct((k, n), BF16)],
                          semantics=("parallel",), moves=moves)(x, dy), moves, 1)


def _dx(dy, w, *, name, partial=None, moves=()):
    t, k = dy.shape
    n = w.shape[0]
    has_partial = partial is not None

    def body(dy_ref, w_ref, *rest):
        out = lax.dot_general(dy_ref[...].astype(BF16), w_ref[...], _NT, preferred_element_type=F32)
        if has_partial:
            out = out + rest[0][...]
        rest[-1][...] = out

    return _unpack(_pcall(body, name=name, grid=(t // TM,),
                          in_specs=[_rows(k), _whole(w)] + ([_rows(n)] if has_partial else []),
                          out_specs=[_rows(n)], out_shape=[jax.ShapeDtypeStruct((t, n), F32)],
                          semantics=("parallel",), moves=moves)(dy, w, *([partial] if has_partial else [])),
                   moves, 1)


def _dx_norm(dy, w, h, g, dres, *, name, partial=None, by_device_cols=False, w_transposed=False, moves=()):
    t, k = dy.shape
    d = h.shape[1]
    has_partial = partial is not None

    def body(dy_ref, w_ref, h_ref, g_ref, dres_ref, *rest):
        if by_device_cols:
            kc = k // N_DEV
            d_y = jnp.zeros((TM, d), F32)
            for j in range(N_DEV):
                d_y = d_y + lax.dot_general(dy_ref[:, j * kc:(j + 1) * kc].astype(BF16), w_ref[j], _NT,
                                            preferred_element_type=F32)
        else:
            d_y = lax.dot_general(dy_ref[...].astype(BF16), w_ref[...], _NN if w_transposed else _NT,
                                  preferred_element_type=F32)
        if has_partial:
            d_y = d_y + rest[0][...]
        dh_ref, dhb_ref, dg_ref, cs_ref = rest[-4:]
        _, vjp = jax.vjp(_rmsnorm, h_ref[...], g_ref[...])
        dh, dg = vjp(d_y)
        dh = dh + dres_ref[...]
        dh_ref[...] = dh
        dhb_ref[...] = dh.astype(BF16)

        @pl.when(pl.program_id(0) == 0)
        def _():
            dg_ref[...] = jnp.zeros_like(dg_ref)
            cs_ref[...] = jnp.zeros_like(cs_ref)

        dg_ref[...] += dg
        cs_ref[...] += jnp.sum(dh, axis=0, keepdims=True)

    shapes = [jax.ShapeDtypeStruct((t, d), F32), jax.ShapeDtypeStruct((t, d), BF16),
              jax.ShapeDtypeStruct((1, d), F32), jax.ShapeDtypeStruct((1, d), F32)]
    return _unpack(_pcall(body, name=name, grid=(t // TM,),
                          in_specs=[_rows(k), _whole(w), _rows(d), _acc_row(d), _rows(d)]
                          + ([_rows(d)] if has_partial else []),
                          out_specs=[_rows(d), _rows(d), _acc_row(d), _acc_row(d)], out_shape=shapes,
                          semantics=("arbitrary",), moves=moves)(dy, w, h, g, dres, *([partial] if has_partial else [])),
                   moves, 4)


def _pair_add(by_core, theirs, core, *, name, tb=512):
    n_chip, _, r, c = by_core.shape
    tb = min(tb, r)
    assert r % tb == 0, (name, r, tb)

    def body(core_ref, a_ref, b_ref, o_ref):
        del core_ref
        o_ref[...] = (a_ref[...].astype(F32) + b_ref[...].astype(F32)).astype(o_ref.dtype)

    blk = pl.BlockSpec((None, tb, c), lambda ch, i, core_ref: (ch, i, 0))
    return pl.pallas_call(
        body, name=name,
        grid_spec=pltpu.PrefetchScalarGridSpec(
            num_scalar_prefetch=1, grid=(n_chip, r // tb),
            in_specs=[pl.BlockSpec((None, None, tb, c), lambda ch, i, core_ref: (ch, core_ref[0], i, 0)), blk],
            out_specs=blk),
        out_shape=jax.ShapeDtypeStruct((n_chip, r, c), by_core.dtype),
        compiler_params=pltpu.CompilerParams(dimension_semantics=("parallel", "parallel"),
                                             vmem_limit_bytes=VMEM_LIMIT_BYTES),
    )(core, by_core, theirs)


def _colsum(a, *, name, tb=512):
    t, d = a.shape

    def body(a_ref, o_ref):
        @pl.when(pl.program_id(0) == 0)
        def _():
            o_ref[...] = jnp.zeros_like(o_ref)

        o_ref[...] += jnp.sum(a_ref[...].astype(F32), axis=0, keepdims=True)

    return _pcall(
        body, name=name, grid=(t // tb,),
        in_specs=[pl.BlockSpec((tb, d), lambda i: (i, 0))],
        out_specs=[pl.BlockSpec((1, d), lambda i: (0, 0))],
        out_shape=[jax.ShapeDtypeStruct((1, d), F32)],
        semantics=("arbitrary",),
    )(a)[0][0]


def _mlp_down_loss(up, w_rows, res, g, target, *, name):
    t, d = res.shape
    n_w = len(w_rows)

    def body(up_ref, *rest):
        w_refs, (res_ref, g_ref, tgt_ref, loss_ref, dh_ref, dhb_ref, dg_ref) = rest[:n_w], rest[n_w:]
        h = res_ref[...]
        for off, rows, w_blk in _down_blocks(w_refs):
            act = _sq_relu(up_ref[:, off:off + rows]).astype(BF16)
            h = h + lax.dot_general(act, w_blk, _NN, preferred_element_type=F32)

        def f(hh, gg):
            err = jnp.square(_rmsnorm(hh, gg) - tgt_ref[...])
            return 0.5 * jnp.sum(jnp.mean(err, axis=-1, keepdims=True), axis=0, keepdims=True)

        val, vjp = jax.vjp(f, h, g_ref[...])
        dh, dg = vjp(jnp.ones((1, 1), F32))
        dh_ref[...] = dh
        dhb_ref[...] = dh.astype(BF16)

        @pl.when(pl.program_id(0) == 0)
        def _():
            loss_ref[...] = jnp.zeros_like(loss_ref)
            dg_ref[...] = jnp.zeros_like(dg_ref)

        loss_ref[...] += val
        dg_ref[...] += dg

    return _pcall(
        body, name=name, grid=(t // TM,),
        in_specs=[_rows(D_FF)] + [_whole(w) for w in w_rows] + [_rows(d), _acc_row(d), _rows(d)],
        out_specs=[pl.BlockSpec((8, LANES), lambda i: (0, 0)), _rows(d), _rows(d), _acc_row(d)],
        out_shape=[jax.ShapeDtypeStruct((8, LANES), F32), jax.ShapeDtypeStruct((t, d), F32),
                   jax.ShapeDtypeStruct((t, d), BF16), jax.ShapeDtypeStruct((1, d), F32)],
        semantics=("arbitrary",),
    )(up, *w_rows, res, g, target)[0]


def _gmlp_fwd(proj_uv, ln_g, ln_b, w_s, b_s, *, name, moves=()):
    t = proj_uv.shape[0]
    w = D_MODEL

    def body(u_ref, v_ref, g_ref, b_ref, w_ref, bs_ref, o_ref):
        o_ref[...] = _gmlp_chunk(u_ref[...], v_ref[...], g_ref[...], b_ref[...], w_ref[...],
                                 bs_ref[...]).astype(o_ref.dtype)

    row = pl.BlockSpec((1, w), lambda i: (0, 0))
    res, landed = _pcall(
        body, name=name, grid=(t // CHUNK,),
        in_specs=[pl.BlockSpec((CHUNK, w), lambda i: (i, 0)), pl.BlockSpec((CHUNK, w), lambda i: (i, 1)), row, row,
                  pl.BlockSpec((GM_GROUPS, CHUNK, CHUNK), lambda i: (0, 0, 0)),
                  pl.BlockSpec((GM_GROUPS, CHUNK, 1), lambda i: (0, 0, 0))],
        out_specs=[pl.BlockSpec((CHUNK, w), lambda i: (i, 0))],
        out_shape=[jax.ShapeDtypeStruct((t, 2 * w), BF16)],
        semantics=("parallel",), moves=moves,
    )(proj_uv, proj_uv, ln_g, ln_b, w_s, b_s)
    return (res[0], landed) if moves else res[0]


def _gmlp_bwd(proj_uv, d_mix, ln_g, ln_b, w_s, b_s, *, name, moves=()):
    t = proj_uv.shape[0]
    w = D_MODEL

    def body(u_ref, v_ref, da_ref, g_ref, b_ref, w_ref, bs_ref, duv_ref, dg_ref, db_ref, dw_ref, dbs_ref):
        _, vjp = jax.vjp(_gmlp_chunk, u_ref[...], v_ref[...], g_ref[...], b_ref[...], w_ref[...], bs_ref[...])
        du, dv, dg, db, dw, dbs = vjp(da_ref[...])
        duv_ref[:, :w] = du.astype(duv_ref.dtype)
        duv_ref[:, w:] = dv.astype(duv_ref.dtype)

        @pl.when(pl.program_id(0) == 0)
        def _():
            dg_ref[...] = jnp.zeros_like(dg_ref)
            db_ref[...] = jnp.zeros_like(db_ref)
            dw_ref[...] = jnp.zeros_like(dw_ref)
            dbs_ref[...] = jnp.zeros_like(dbs_ref)

        dg_ref[...] += dg
        db_ref[...] += db
        dw_ref[...] += dw
        dbs_ref[...] += dbs

    row = pl.BlockSpec((1, w), lambda i: (0, 0))
    ws = pl.BlockSpec((GM_GROUPS, CHUNK, CHUNK), lambda i: (0, 0, 0))
    bs = pl.BlockSpec((GM_GROUPS, CHUNK, 1), lambda i: (0, 0, 0))
    res, landed = _pcall(
        body, name=name, grid=(t // CHUNK,),
        in_specs=[pl.BlockSpec((CHUNK, w), lambda i: (i, 0)), pl.BlockSpec((CHUNK, w), lambda i: (i, 1)),
                  pl.BlockSpec((CHUNK, w), lambda i: (i, 0)), row, row, ws, bs],
        out_specs=[pl.BlockSpec((CHUNK, 2 * w), lambda i: (i, 0)), row, row, ws, bs],
        out_shape=[jax.ShapeDtypeStruct((t, 2 * w), BF16), jax.ShapeDtypeStruct((1, w), F32),
                   jax.ShapeDtypeStruct((1, w), F32), jax.ShapeDtypeStruct((GM_GROUPS, CHUNK, CHUNK), F32),
                   jax.ShapeDtypeStruct((GM_GROUPS, CHUNK, 1), F32)],
        semantics=("arbitrary",), moves=moves,
    )(proj_uv, proj_uv, d_mix, ln_g, ln_b, w_s, b_s)
    return (res, landed) if moves else res


_HALO_PER_CHUNK = CHUNK // HALO
_DT_BLOCK = (CONV_DIM + D_MODEL) // LANES


def _ssd_fwd(proj_rest, mix, conv_w, conv_b, dt_bias, a_log, d_skip, norm_g, *, name, moves=()):
    t = proj_rest.shape[0]
    nc = t // CHUNK

    def body(x_ref, prev_ref, z_ref, dt_ref, mix_ref, cw_ref, cb_ref, dtb_ref, al_ref, ds_ref, ng_ref, y_ref, hs_ref,
             pre_ref, h_scr):
        del mix_ref
        i = pl.program_id(0)

        @pl.when(i == 0)
        def _():
            h_scr[...] = jnp.zeros_like(h_scr)

        prev8 = jnp.where(i == 0, 0.0, prev_ref[...])
        pre = _conv_pre(prev8, x_ref[...], cw_ref[...], cb_ref[...])
        pre_ref[...] = pre
        hs_ref[0] = h_scr[...]
        h_prev = tuple(h_scr[j] for j in range(_PAIRS))
        y, h_next = _ssd_chunk(pre, z_ref[...], dt_ref[...], h_prev, dtb_ref[...], al_ref[...], ds_ref[...],
                               ng_ref[...])
        y_ref[...] = y.astype(y_ref.dtype)
        for j in range(_PAIRS):
            h_scr[j] = h_next[j]

    small = pl.BlockSpec((1, LANES), lambda i: (0, 0))
    res, landed = _pcall(
        body, name=name, grid=(nc,),
        in_specs=[pl.BlockSpec((CHUNK, CONV_DIM), lambda i: (i, 0)),
                  pl.BlockSpec((HALO, CONV_DIM), lambda i: (jnp.maximum(i * _HALO_PER_CHUNK - 1, 0), 0)),
                  pl.BlockSpec((CHUNK, D_MODEL), lambda i: (i, CONV_DIM // D_MODEL)),
                  pl.BlockSpec((CHUNK, LANES), lambda i: (i, _DT_BLOCK)),
                  pl.BlockSpec(memory_space=pl.ANY),
                  pl.BlockSpec((SSM_CONV, CONV_DIM), lambda i: (0, 0)),
                  pl.BlockSpec((1, CONV_DIM), lambda i: (0, 0)),
                  small, small, small, pl.BlockSpec((1, D_MODEL), lambda i: (0, 0))],
        out_specs=[pl.BlockSpec((CHUNK, D_MODEL), lambda i: (i, 1)),
                   pl.BlockSpec((1, _PAIRS, SSM_STATE, LANES), lambda i: (i, 0, 0, 0)),
                   pl.BlockSpec((CHUNK, CONV_DIM), lambda i: (i, 0))],
        out_shape=[jax.ShapeDtypeStruct((t, 2 * D_MODEL), BF16),
                   jax.ShapeDtypeStruct((nc, _PAIRS, SSM_STATE, LANES), F32),
                   jax.ShapeDtypeStruct((t, CONV_DIM), F32)],
        scratch_shapes=[pltpu.VMEM((_PAIRS, SSM_STATE, LANES), F32)],
        semantics=("arbitrary",), moves=moves, aliases={4: 0},
    )(proj_rest, proj_rest, proj_rest, proj_rest, mix, conv_w, conv_b, dt_bias, a_log, d_skip, norm_g)
    return (res, landed) if moves else res


def _ssd_bwd(proj_rest, pre, h_states, d_mix, dt_bias, a_log, d_skip, norm_g, *, name, moves=()):
    t = proj_rest.shape[0]
    nc = t // CHUNK

    def body(pre_ref, z_ref, dt_ref, hs_ref, dy_ref, dtb_ref, al_ref, ds_ref, ng_ref,
             dpre_ref, dz_ref, ddt_ref, ddtb_ref, dal_ref, dds_ref, dng_ref, dh_scr):
        i = pl.program_id(0)

        @pl.when(i == 0)
        def _():
            dh_scr[...] = jnp.zeros_like(dh_scr)
            ddtb_ref[...] = jnp.zeros_like(ddtb_ref)
            dal_ref[...] = jnp.zeros_like(dal_ref)
            dds_ref[...] = jnp.zeros_like(dds_ref)
            dng_ref[...] = jnp.zeros_like(dng_ref)

        h_prev = tuple(hs_ref[0, j] for j in range(_PAIRS))
        _, vjp = jax.vjp(_ssd_chunk, pre_ref[...], z_ref[...], dt_ref[...], h_prev, dtb_ref[...], al_ref[...],
                         ds_ref[...], ng_ref[...])
        dpre, dz, ddt, dh_prev, ddtb, dal, dds, dng = vjp((dy_ref[...], tuple(dh_scr[j] for j in range(_PAIRS))))
        dpre_ref[...] = dpre
        dz_ref[...] = dz.astype(dz_ref.dtype)
        ddt_ref[...] = ddt.astype(ddt_ref.dtype)
        for j in range(_PAIRS):
            dh_scr[j] = dh_prev[j]
        ddtb_ref[...] += ddtb
        dal_ref[...] += dal
        dds_ref[...] += dds
        dng_ref[...] += dng

    rev = lambda i: nc - 1 - i
    small = pl.BlockSpec((1, LANES), lambda i: (0, 0))
    wide = pl.BlockSpec((1, D_MODEL), lambda i: (0, 0))
    res, landed = _pcall(
        body, name=name, grid=(nc,),
        in_specs=[pl.BlockSpec((CHUNK, CONV_DIM), lambda i: (rev(i), 0)),
                  pl.BlockSpec((CHUNK, D_MODEL), lambda i: (rev(i), CONV_DIM // D_MODEL)),
                  pl.BlockSpec((CHUNK, LANES), lambda i: (rev(i), _DT_BLOCK)),
                  pl.BlockSpec((1, _PAIRS, SSM_STATE, LANES), lambda i: (rev(i), 0, 0, 0)),
                  pl.BlockSpec((CHUNK, D_MODEL), lambda i: (rev(i), 1)),
                  small, small, small, wide],
        out_specs=[pl.BlockSpec((CHUNK, CONV_DIM), lambda i: (rev(i), 0)),
                   pl.BlockSpec((CHUNK, D_MODEL), lambda i: (rev(i), 0)),
                   pl.BlockSpec((CHUNK, LANES), lambda i: (rev(i), 0)),
                   small, small, small, wide],
        out_shape=[jax.ShapeDtypeStruct((t, CONV_DIM), F32), jax.ShapeDtypeStruct((t, D_MODEL), BF16),
                   jax.ShapeDtypeStruct((t, LANES), BF16),
                   jax.ShapeDtypeStruct((1, LANES), F32), jax.ShapeDtypeStruct((1, LANES), F32),
                   jax.ShapeDtypeStruct((1, LANES), F32), jax.ShapeDtypeStruct((1, D_MODEL), F32)],
        scratch_shapes=[pltpu.VMEM((_PAIRS, SSM_STATE, LANES), F32)],
        semantics=("arbitrary",), moves=moves,
    )(pre, proj_rest, proj_rest, h_states, d_mix, dt_bias, a_log, d_skip, norm_g)
    return (res, landed) if moves else res


def _conv_bwd(proj_rest, dpre, dz, ddt, conv_w, *, name, tb=256, moves=()):
    t = proj_rest.shape[0]
    nb = t // tb
    per = tb // HALO

    def body(x_ref, prev_ref, dpre_ref, next_ref, dz_ref, ddt_ref, cw_ref, drest_ref, dcw_ref, dcb_ref):
        i = pl.program_id(0)

        @pl.when(i == 0)
        def _():
            dcw_ref[...] = jnp.zeros_like(dcw_ref)
            dcb_ref[...] = jnp.zeros_like(dcb_ref)

        x = x_ref[...]
        dp = dpre_ref[...]
        w = cw_ref[...]
        prev8 = jnp.where(i == 0, 0.0, prev_ref[...])
        next8 = jnp.where(i == nb - 1, 0.0, next_ref[...])
        dx = dp * w[SSM_CONV - 1:SSM_CONV]
        for j in range(SSM_CONV - 1):
            dx = dx + _shift_up(dp, next8, SSM_CONV - 1 - j) * w[j:j + 1]
        drest_ref[:, :CONV_DIM] = dx.astype(drest_ref.dtype)
        drest_ref[:, CONV_DIM:CONV_DIM + D_MODEL] = dz_ref[...].astype(drest_ref.dtype)
        drest_ref[:, CONV_DIM + D_MODEL:] = ddt_ref[...].astype(drest_ref.dtype)
        for j in range(SSM_CONV):
            dcw_ref[j:j + 1, :] += jnp.sum(dp * _shift_down(prev8, x, SSM_CONV - 1 - j), axis=0, keepdims=True)
        dcb_ref[...] += jnp.sum(dp, axis=0, keepdims=True)

    res, landed = _pcall(
        body, name=name, grid=(nb,),
        in_specs=[pl.BlockSpec((tb, CONV_DIM), lambda i: (i, 0)),
                  pl.BlockSpec((HALO, CONV_DIM), lambda i: (jnp.maximum(i * per - 1, 0), 0)),
                  pl.BlockSpec((tb, CONV_DIM), lambda i: (i, 0)),
                  pl.BlockSpec((HALO, CONV_DIM), lambda i: (jnp.minimum((i + 1) * per, nb * per - 1), 0)),
                  pl.BlockSpec((tb, D_MODEL), lambda i: (i, 0)),
                  pl.BlockSpec((tb, LANES), lambda i: (i, 0)),
                  pl.BlockSpec((SSM_CONV, CONV_DIM), lambda i: (0, 0))],
        out_specs=[pl.BlockSpec((tb, REST_W), lambda i: (i, 0)),
                   pl.BlockSpec((SSM_CONV, CONV_DIM), lambda i: (0, 0)),
                   pl.BlockSpec((1, CONV_DIM), lambda i: (0, 0))],
        out_shape=[jax.ShapeDtypeStruct((t, REST_W), BF16), jax.ShapeDtypeStruct((SSM_CONV, CONV_DIM), F32),
                   jax.ShapeDtypeStruct((1, CONV_DIM), F32)],
        semantics=("arbitrary",), moves=moves,
    )(proj_rest, proj_rest, dpre, dpre, dz, ddt, conv_w)
    return (res, landed) if moves else res


_KV_BLOCK = D_MODEL // (2 * LANES)
_SINK_ROWS = _PAIRS_PER_KV * CHUNK


def _stack_pairs(ref, kv_head):
    base = kv_head * _PAIRS_PER_KV
    return jnp.concatenate([ref[:, (base + p) * LANES:(base + p + 1) * LANES] for p in range(_PAIRS_PER_KV)], axis=0)


def _attn_fwd(qkv, sinks, *, name, moves=()):
    t = qkv.shape[0]
    nb = t // CHUNK

    def body(q_ref, kvp_ref, kvc_ref, s_ref, o_ref):
        first = pl.program_id(0) == 0
        for j in range(ATTN_KV):
            o = _attn_pairs(_stack_pairs(q_ref, j), kvp_ref[...], kvc_ref[...], s_ref[j, 0], s_ref[j, 1], first, j)
            for p in range(_PAIRS_PER_KV):
                col = (j * _PAIRS_PER_KV + p) * LANES
                o_ref[:, col:col + LANES] = o[p * CHUNK:(p + 1) * CHUNK].astype(o_ref.dtype)

    return _unpack(_pcall(
        body, name=name, grid=(nb,),
        in_specs=[pl.BlockSpec((CHUNK, D_MODEL), lambda i: (i, 0)),
                  pl.BlockSpec((CHUNK, 2 * LANES), lambda i: (jnp.maximum(i - 1, 0), _KV_BLOCK)),
                  pl.BlockSpec((CHUNK, 2 * LANES), lambda i: (i, _KV_BLOCK)),
                  pl.BlockSpec((ATTN_KV, 2, _SINK_ROWS, 1), lambda i: (0, 0, 0, 0))],
        out_specs=[pl.BlockSpec((CHUNK, D_MODEL), lambda i: (i, 0))],
        out_shape=[jax.ShapeDtypeStruct((t, D_MODEL), BF16)],
        semantics=("parallel",), moves=moves,
    )(qkv, qkv, qkv, sinks), moves, 1)


def _attn_bwd(qkv, sinks, d_o, *, name, moves=()):
    t = qkv.shape[0]
    nb = t // CHUNK

    def body(q_ref, kvp_ref, kvc_ref, s_ref, do_ref, dqkv_ref, ds_ref, dkv_scr):
        i = pl.program_id(0)
        first = i == nb - 1

        @pl.when(i == 0)
        def _():
            dkv_scr[...] = jnp.zeros_like(dkv_scr)
            ds_ref[...] = jnp.zeros_like(ds_ref)

        dkv_cur = dkv_scr[...]
        dkv_prev = jnp.zeros_like(dkv_cur)
        for j in range(ATTN_KV):
            _, vjp = jax.vjp(functools.partial(_attn_pairs, first=first, kv_head=j), _stack_pairs(q_ref, j),
                             kvp_ref[...], kvc_ref[...], s_ref[j, 0], s_ref[j, 1])
            dq4, dkvp, dkvc, ds_lo, ds_hi = vjp(_stack_pairs(do_ref, j))
            for p in range(_PAIRS_PER_KV):
                col = (j * _PAIRS_PER_KV + p) * LANES
                dqkv_ref[:, col:col + LANES] = dq4[p * CHUNK:(p + 1) * CHUNK]
            dkv_cur = dkv_cur + dkvc
            dkv_prev = dkv_prev + dkvp
            ds_ref[j, 0] += ds_lo
            ds_ref[j, 1] += ds_hi
        dqkv_ref[:, D_MODEL:] = dkv_cur
        dkv_scr[...] = dkv_prev

    cur = lambda i: (nb - 1 - i, 0)
    sk = pl.BlockSpec((ATTN_KV, 2, _SINK_ROWS, 1), lambda i: (0, 0, 0, 0))
    res, landed = _pcall(
        body, name=name, grid=(nb,),
        in_specs=[pl.BlockSpec((CHUNK, D_MODEL), cur),
                  pl.BlockSpec((CHUNK, 2 * LANES), lambda i: (jnp.maximum(nb - 2 - i, 0), _KV_BLOCK)),
                  pl.BlockSpec((CHUNK, 2 * LANES), lambda i: (nb - 1 - i, _KV_BLOCK)),
                  sk, pl.BlockSpec((CHUNK, D_MODEL), cur)],
        out_specs=[pl.BlockSpec((CHUNK, QKV_DIM), cur), sk],
        out_shape=[jax.ShapeDtypeStruct((t, QKV_DIM), F32), jax.ShapeDtypeStruct((ATTN_KV, 2, _SINK_ROWS, 1), F32)],
        scratch_shapes=[pltpu.VMEM((CHUNK, 2 * LANES), F32)],
        semantics=("arbitrary",), moves=moves,
    )(qkv, qkv, qkv, sinks, d_o)
    return (res, landed) if moves else res


def _adamw(parts, w, m, v, *, name, tb=512, moves=()):
    layers, r, c = w.shape
    n = parts[0].shape[0]
    tb = min(tb, r)
    assert r % tb == 0 and len(parts) == layers, (name, r, tb)
    nb = r // tb

    def body(*refs):
        p_refs = refs[:layers]
        w_ref, m_ref, v_ref, g_ref, d_ref, nm_ref, nv_ref = refs[layers:]
        for layer in range(layers):
            @pl.when(pl.program_id(0) == layer)
            def _(p_ref=p_refs[layer]):
                g = p_ref[0].astype(F32)
                for s in range(1, n):
                    g = g + p_ref[s].astype(F32)
                m_new = ADAM_B1 * m_ref[...] + (1.0 - ADAM_B1) * g
                v_new = ADAM_B2 * v_ref[...] + (1.0 - ADAM_B2) * jnp.square(g)
                m_hat = m_new / (1.0 - ADAM_B1 ** ADAM_STEP)
                v_hat = v_new / (1.0 - ADAM_B2 ** ADAM_STEP)
                g_ref[...] = g
                d_ref[...] = -ADAM_LR * (m_hat / (jnp.sqrt(v_hat) + ADAM_EPS) + ADAM_WD * w_ref[...])
                nm_ref[...] = m_new
                nv_ref[...] = v_new

    part_spec = lambda layer: pl.BlockSpec(
        (n, tb, c), lambda l, i: (0, jnp.clip(i + (l - layer) * nb, 0, nb - 1), 0))
    blk = pl.BlockSpec((None, tb, c), lambda l, i: (l, i, 0))
    res, landed = _pcall(
        body, name=name, grid=(layers, nb),
        in_specs=[part_spec(layer) for layer in range(layers)] + [blk, blk, blk],
        out_specs=[blk] * 4,
        out_shape=[jax.ShapeDtypeStruct((layers, r, c), F32)] * 4,
        semantics=("arbitrary", "arbitrary"), moves=moves,
    )(*parts, w, m, v)
    return (res, landed) if moves else res


def _as_rows(a):
    flat = a.reshape(-1)
    pad = (-flat.shape[0]) % PACK_W
    if pad:
        flat = jnp.pad(flat, (0, pad))
    return flat.reshape(-1, PACK_W)


def _cols_from_shards(g):
    return jnp.transpose(g, (1, 0, 2)).reshape(g.shape[1], -1)


def _shard_cols(shards, lo, hi):
    c = shards.shape[2]
    pieces = []
    for j in range(shards.shape[0]):
        a, b = max(lo, j * c), min(hi, (j + 1) * c)
        if a < b:
            pieces.append(shards[j, :, a - j * c:b - j * c])
    return pieces


def _cols_of(sources, lo, hi):
    pieces = []
    for arr, col0, first, last in sources:
        a, b = max(lo, first), min(hi, last)
        if a < b:
            pieces.append(arr[:, col0 + a - first:col0 + b - first])
    return pieces


def _pad_lanes(a):
    return jnp.pad(a, ((0, 0), (0, LANES - a.shape[1])))


def kernel(x, norm_mix_g, norm_mlp_g, final_norm_g, w_in_even, w_out_even, gm_ln_g, gm_ln_b, gm_w_s, gm_b_s, ssm_conv_w, ssm_conv_b, ssm_dt_bias, ssm_a_log, ssm_d, ssm_norm_g, w_qkv, b_qkv, w_o, b_o, attn_sinks, w_up, w_down, loss_target, m_norm_mix_g, m_norm_mlp_g, m_final_norm_g, m_w_in_even, m_w_out_even, m_gm_ln_g, m_gm_ln_b, m_gm_w_s, m_gm_b_s, m_ssm_conv_w, m_ssm_conv_b, m_ssm_dt_bias, m_ssm_a_log, m_ssm_d, m_ssm_norm_g, m_w_qkv, m_b_qkv, m_w_o, m_b_o, m_attn_sinks, m_w_up, m_w_down, v_norm_mix_g, v_norm_mlp_g, v_final_norm_g, v_w_in_even, v_w_out_even, v_gm_ln_g, v_gm_ln_b, v_gm_w_s, v_gm_b_s, v_ssm_conv_w, v_ssm_conv_b, v_ssm_dt_bias, v_ssm_a_log, v_ssm_d, v_ssm_norm_g, v_w_qkv, v_b_qkv, v_w_o, v_b_o, v_attn_sinks, v_w_up, v_w_down):
    names = ["norm_mix_g", "norm_mlp_g", "final_norm_g", "w_in_even", "w_out_even", "gm_ln_g", "gm_ln_b", "gm_w_s",
             "gm_b_s", "ssm_conv_w", "ssm_conv_b", "ssm_dt_bias", "ssm_a_log", "ssm_d", "ssm_norm_g", "w_qkv",
             "b_qkv", "w_o", "b_o", "attn_sinks", "w_up", "w_down"]
    env = locals()
    W = {n: env[n] for n in names}
    M = {n: env["m_" + n] for n in names}
    V = {n: env["v_" + n] for n in names}
    big = ["w_in_even", "w_out_even", "w_qkv", "w_o", "w_up", "w_down"]
    small_sharded = ["ssm_conv_w", "b_qkv", "b_o"]
    replicated = [n for n in names if n not in big and n not in small_sharded]
    me = 4 * lax.axis_index("x") + 2 * lax.axis_index("y") + lax.axis_index("c")
    t = x.shape[1]
    xs = x.reshape(t, D_MODEL)
    target = loss_target.reshape(t, D_MODEL)
    gather = lambda a: _Move("gather", a)
    over_ici = lambda a: _Move("gather_ici", a)
    over_d2d = lambda a: _Move("gather_d2d", a)
    by_core = lambda a: a.reshape((N_CHIP, N_CORE) + a.shape[1:])
    to_sibling = lambda a: [_Move("scatter_d2d", by_core(a))]
    my_core = lax.axis_index("c").astype(jnp.int32).reshape(1)
    pair = lambda a, theirs, name: _pair_add(by_core(a), theirs, my_core, name=name)
    to_chips = lambda a: _Move("scatter_ici", a)
    whole = lambda a: a.reshape((N_DEV,) + a.shape[2:])
    row = lambda a: a.reshape(1, D_MODEL)

    small_flat = jnp.concatenate([W[n].reshape(-1) for n in small_sharded])
    w_in_g, small_g = _exchange([over_ici(w_in_even[0].astype(BF16)), gather(_as_rows(small_flat))],
                                name="gather_w_in", then_d2d=[0])
    w_in_s = whole(w_in_g)
    z_lo, xbc_lo, dt_lo = 2 * D_MODEL, 3 * D_MODEL, 3 * D_MODEL + CONV_DIM
    w_uv = jnp.concatenate(_shard_cols(w_in_s, 0, z_lo), axis=1)
    w_rest = jnp.concatenate(_shard_cols(w_in_s, xbc_lo, dt_lo) + _shard_cols(w_in_s, z_lo, xbc_lo)
                             + _shard_cols(w_in_s, dt_lo, IN_EVEN)
                             + [jnp.zeros((D_MODEL, LANES - SSM_HEADS), BF16)], axis=1)
    small_all = small_g.reshape(N_DEV, -1)
    n_cw = SSM_CONV * CONV_DIM // N_DEV
    n_bq = QKV_DIM // N_DEV
    conv_w = _cols_from_shards(small_all[:, :n_cw].reshape(N_DEV, SSM_CONV, CONV_DIM // N_DEV))
    bqkv = small_all[:, n_cw:n_cw + n_bq].reshape(1, QKV_DIM)
    bo = small_all[:, n_cw + n_bq:n_cw + n_bq + D_MODEL // N_DEV].reshape(1, D_MODEL)

    conv_b = ssm_conv_b.reshape(1, CONV_DIM)
    dt_bias, a_log, d_skip = _pad_lanes(ssm_dt_bias), _pad_lanes(ssm_a_log), _pad_lanes(ssm_d)
    gm_w = gm_w_s[0]
    gm_b = gm_b_s[0].reshape(GM_GROUPS, CHUNK, 1)
    sink_rows = jnp.repeat(jnp.transpose(attn_sinks.reshape(ATTN_KV, _PAIRS_PER_KV, 2), (0, 2, 1)), CHUNK,
                           axis=2).reshape(ATTN_KV, 2, _SINK_ROWS, 1)
    w_up_b, w_down_b = w_up.astype(BF16), w_down.astype(BF16)

    w_down0_a, w_down0_b = w_down_b[0, :FF_SHARD // 2], w_down_b[0, FF_SHARD // 2:]
    (y0, proj_uv), (w_qkv_g,) = _norm_matmul(xs, row(norm_mix_g[0]), w_uv, name="proj_uv", emit_y=True,
                                             moves=[over_ici(jnp.transpose(w_qkv[0]).astype(BF16))])
    proj_rest, (w_out_g,) = _norm_matmul(xs, row(norm_mix_g[0]), w_rest, name="proj_rest", emit_y=False,
                                         moves=[over_ici(w_out_even[0].astype(BF16))])
    mix, (w_o_g, w_down0_a, w_out_g, w_qkv_g) = _gmlp_fwd(
        proj_uv, gm_ln_g, gm_ln_b, gm_w, gm_b, name="gmlp_fwd",
        moves=[over_ici(w_o[0].astype(BF16)), over_ici(w_down0_a), over_d2d(w_out_g), over_d2d(w_qkv_g)])
    (mix, h_states, conv_pre), (w_up0_g, w_o_g, w_down0_a) = _ssd_fwd(
        proj_rest, mix, conv_w, conv_b, dt_bias, a_log, d_skip, ssm_norm_g, name="ssd_fwd",
        moves=[over_ici(w_up_b[0]), over_d2d(w_o_g), over_d2d(w_down0_a)])
    w_out_f = whole(w_out_g).reshape(2 * D_MODEL, D_MODEL)
    (h1, y1), (w_down0_b, w_up0_g) = _residual_matmul(
        mix, w_out_f, xs, name="mix_out", norm_g=row(norm_mlp_g[0]),
        moves=[over_ici(w_down0_b), over_d2d(w_up0_g)])
    up0, (w_down0_b,) = _mlp_up(y1, whole(w_up0_g), name="mlp_up0", moves=[over_d2d(w_down0_b)])
    w_down_g = [[whole(w_down0_a), whole(w_down0_b)]]
    h2, y2 = _mlp_down(up0, w_down_g[0], h1, name="mlp_down0", norm_g=row(norm_mix_g[1]))
    wqkv = whole(w_qkv_g).reshape(QKV_DIM, D_MODEL)
    wo = whole(w_o_g).reshape(D_MODEL, D_MODEL)
    qkv = _residual_matmul(y2, wqkv, None, name="qkv", bias=bqkv, w_transposed=True)
    attn, (w_up1_g, w_down1_g) = _attn_fwd(qkv, sink_rows, name="attn_fwd",
                                           moves=[over_ici(w_up_b[1]), over_ici(w_down_b[1])])
    (h3, y3), (w_up1_g,) = _residual_matmul(attn, wo, h2, name="attn_out", bias=bo, norm_g=row(norm_mlp_g[1]),
                                            moves=[over_d2d(w_up1_g)])
    w_up_g = [whole(w_up0_g), whole(w_up1_g)]
    up1, (w_down1_g,) = _mlp_up(y3, w_up_g[1], name="mlp_up1", moves=[over_d2d(w_down1_g)])
    w_down_g.append([whole(w_down1_g)])
    loss_part, dh4, dh4_b, d_final_g = _mlp_down_loss(up1, w_down_g[1], h3, row(final_norm_g), target,
                                                      name="mlp_down1_loss")

    by_dev_rows = lambda a: a.reshape((N_DEV, a.shape[0] // N_DEV) + a.shape[1:])

    def mlp_bwd(dh, dh_b, h, y, up, layer, first_moves=()):
        res = _mlp_down_dx(dh_b, w_down_g[layer], up, name=f"mlp_down_dx{layer}", moves=first_moves)
        d_up, first_landed = res if first_moves else (res, [])
        g_down = _dw_by_rows(up, dh_b, name=f"mlp_down_dw{layer}", tk=FF_SHARD, square_relu=True)
        g_down = by_dev_rows(g_down)
        g_up, (theirs,) = _dw_by_cols(y, d_up, name=f"mlp_up_dw{layer}", tn=FF_SHARD, by_device=True,
                                      moves=to_sibling(g_down))
        q_down = pair(g_down, theirs, f"mlp_down_pair{layer}")
        (dh_new, dh_new_b, dg, cs), (r_down, theirs) = _dx_norm(
            d_up, w_up_g[layer], h, row(norm_mlp_g[layer]), dh, name=f"mlp_up_dx{layer}", by_device_cols=True,
            moves=[to_chips(q_down)] + to_sibling(g_up))
        q_up = pair(g_up, theirs, f"mlp_up_pair{layer}")
        return dh_new, dh_new_b, cs, dg, q_up, r_down, first_landed

    dh3, dh3_b, cs3, g_nmlp1, q_up1, r_down1, _ = mlp_bwd(dh4, dh4_b, h3, y3, up1, 1)
    g_bo = cs3
    g_wo = by_dev_rows(_dw_by_cols(attn, dh3_b, name="attn_out_dw", tn=FF_SHARD))
    d_attn, (theirs,) = _dx(dh3_b, wo, name="attn_out_dx", moves=to_sibling(g_wo))
    q_wo = pair(g_wo, theirs, "attn_out_pair")
    (dqkv, d_sink), (r_up1, r_wo) = _attn_bwd(qkv, sink_rows, d_attn, name="attn_bwd",
                                              moves=[to_chips(q_up1), to_chips(q_wo)])
    g_bqkv = _colsum(dqkv, name="qkv_db")
    g_wqkv = by_dev_rows(_dw_by_rows(dqkv, y2, name="qkv_dw", tk=QKV_DIM // 2))
    (dh2, dh2_b, g_nmix1, _), (theirs,) = _dx_norm(dqkv, wqkv, h2, row(norm_mix_g[1]), dh3, name="qkv_dx",
                                                   w_transposed=True, moves=to_sibling(g_wqkv))
    q_wqkv = pair(g_wqkv, theirs, "qkv_pair")
    dh1, dh1_b, _, g_nmlp0, q_up0, r_down0, (r_wqkv,) = mlp_bwd(dh2, dh2_b, h1, y1, up0, 0,
                                                                first_moves=[to_chips(q_wqkv)])

    d_mix = _dx(dh1_b, w_out_f, name="mix_out_dx")
    g_wout = by_dev_rows(_dw_by_rows(mix, dh1_b, name="mix_out_dw", tk=FF_SHARD))
    (d_uv, g_ln_g, g_ln_b, g_gm_w, g_gm_b), (r_up0, theirs) = _gmlp_bwd(
        proj_uv, d_mix, gm_ln_g, gm_ln_b, gm_w, gm_b, name="gmlp_bwd", moves=[to_chips(q_up0)] + to_sibling(g_wout))
    q_wout = pair(g_wout, theirs, "mix_out_pair")

    early = [("norm_mlp_g", None), ("final_norm_g", None), ("norm_mix_g", 1), ("gm_ln_g", None), ("gm_ln_b", None),
             ("gm_w_s", None), ("gm_b_s", None), ("attn_sinks", None)]
    late = [("norm_mix_g", 0), ("ssm_conv_b", None), ("ssm_dt_bias", None), ("ssm_a_log", None), ("ssm_d", None),
            ("ssm_norm_g", None)]
    early_sharded, late_sharded = ["b_qkv", "b_o"], ["ssm_conv_w"]
    small_grads = {
        ("norm_mlp_g", None): jnp.concatenate([g_nmlp0, g_nmlp1], axis=0),
        ("final_norm_g", None): d_final_g, ("norm_mix_g", 1): g_nmix1,
        ("gm_ln_g", None): g_ln_g, ("gm_ln_b", None): g_ln_b, ("gm_w_s", None): g_gm_w, ("gm_b_s", None): g_gm_b,
        ("attn_sinks", None): jnp.transpose(
            jnp.sum(d_sink.reshape(ATTN_KV, 2, _PAIRS_PER_KV, CHUNK), axis=3), (0, 2, 1)),
        "b_qkv": g_bqkv, "b_o": g_bo,
    }
    pack = lambda keys: _as_rows(jnp.concatenate([small_grads[key].reshape(-1) for key in keys]))
    (dpre, dz, ddt, g_dtb, g_alog, g_dskip, g_ssm_ng), (r_wout, early_recv) = _ssd_bwd(
        proj_rest, conv_pre, h_states, d_mix, dt_bias, a_log, d_skip, ssm_norm_g, name="ssd_bwd",
        moves=[to_chips(q_wout), gather(pack(early + early_sharded))])
    d_rest, g_conv_w, g_conv_b = _conv_bwd(proj_rest, dpre, dz, ddt, conv_w, name="conv_bwd")
    g_w_uv = _dw_by_cols(y0, d_uv, name="proj_uv_dw", tn=FF_SHARD)
    g_w_rest = _dw_by_cols(y0, d_rest, name="proj_rest_dw", tn=REST_W // 5)
    in_cols = [(g_w_uv, 0, 0, z_lo), (g_w_rest, CONV_DIM, z_lo, xbc_lo), (g_w_rest, 0, xbc_lo, dt_lo),
               (g_w_rest, CONV_DIM + D_MODEL, dt_lo, IN_EVEN)]
    in_shard = IN_EVEN // N_DEV
    g_w_in = jnp.stack([jnp.concatenate(_cols_of(in_cols, j * in_shard, (j + 1) * in_shard), axis=1)
                        for j in range(N_DEV)])
    dy0, (theirs,) = _dx(d_uv, w_uv, name="proj_uv_dx", moves=to_sibling(g_w_in))
    q_w_in = pair(g_w_in, theirs, "proj_pair")
    quarter = D_MODEL // 4
    in_part = lambda k: _Move("scatter_ici", q_w_in, rows=(k * quarter, quarter))
    (dx, _, g_nmix0, _), r_w_in = _dx_norm(d_rest, w_rest, xs, row(norm_mix_g[0]), dh1, name="proj_rest_dx",
                                           partial=dy0, moves=[in_part(0), in_part(1)])
    small_grads.update({
        ("loss", None): loss_part[:1, :1],
        ("norm_mix_g", 0): g_nmix0, ("ssm_conv_b", None): g_conv_b,
        ("ssm_dt_bias", None): g_dtb[:, :SSM_HEADS], ("ssm_a_log", None): g_alog[:, :SSM_HEADS],
        ("ssm_d", None): g_dskip[:, :SSM_HEADS], ("ssm_norm_g", None): g_ssm_ng, "ssm_conv_w": g_conv_w,
    })


    def update(n, parts, moves=(), transposed=False):
        shape = W[n].shape
        if transposed:
            as3 = lambda a: jnp.transpose(a[0])[None]
            back = lambda a: jnp.transpose(a[0])[None]
        else:
            as3 = lambda a: a.reshape((len(parts),) + parts[0].shape[1:])
            back = lambda a: a.reshape(shape)
        res = _adamw(parts, as3(W[n]), as3(M[n]), as3(V[n]), name="adamw_" + n, moves=moves)
        res, landed = res if moves else (res, [])
        return [back(a) for a in res], landed

    out = {}
    late_keys = late + late_sharded + [("loss", None)]
    out["w_o"], (late_recv,) = update("w_o", [r_wo], moves=[gather(pack(late_keys))])
    out["w_down"], _ = update("w_down", [r_down0, r_down1])
    out["w_up"], _ = update("w_up", [r_up0, r_up1])
    out["w_out_even"], _ = update("w_out_even", [r_wout])
    out["w_qkv"], r_w_in_late = update("w_qkv", [r_wqkv], moves=[in_part(2), in_part(3)], transposed=True)
    out["w_in_even"], _ = update("w_in_even", list(r_w_in) + list(r_w_in_late))

    def unpacked(recv, keys):
        flat, res, o = recv.reshape(N_DEV, -1), {}, 0
        for key in keys:
            res[key] = flat[:, o:o + small_grads[key].size]
            o += small_grads[key].size
        return res

    arrived = {**unpacked(early_recv, early + early_sharded), **unpacked(late_recv, late_keys)}
    piece = lambda tree, key: tree[key[0]] if key[1] is None else tree[key[0]][key[1]]

    def rows_by_device(cat):
        pad = (-cat.shape[1]) % PACK_W
        return jnp.pad(cat, ((0, 0), (0, pad))).reshape(N_DEV, -1, PACK_W)

    rep_keys = early + late
    rep_parts = rows_by_device(jnp.concatenate([arrived[key] for key in rep_keys], axis=1))
    flat_rep = lambda tree: _as_rows(jnp.concatenate([piece(tree, key).reshape(-1) for key in rep_keys]))[None]
    rep_res = _adamw([rep_parts], flat_rep(W), flat_rep(M), flat_rep(V), name="adamw_replicated")
    sh_keys = early_sharded + late_sharded
    shard_parts = []
    for n in sh_keys:
        full = arrived[n].reshape((N_DEV,) + small_grads[n].shape)
        c = full.shape[-1] // N_DEV
        shard_parts.append(lax.dynamic_slice_in_dim(full, me * c, c, axis=full.ndim - 1).reshape(N_DEV, -1))
    sh_rows = rows_by_device(jnp.concatenate(shard_parts, axis=1))
    flat_sh = lambda tree: _as_rows(jnp.concatenate([tree[n].reshape(-1) for n in sh_keys]))[None]
    sh_res = _adamw([sh_rows], flat_sh(W), flat_sh(M), flat_sh(V), name="adamw_small_sharded")

    def unpack_replicated(rows):
        flat, vals, o = rows.reshape(-1), {}, 0
        for key in rep_keys:
            size = piece(W, key).size
            vals[key] = flat[o:o + size]
            o += size
        res = {}
        for n in replicated:
            if (n, None) in vals:
                res[n] = vals[(n, None)].reshape(W[n].shape)
            else:
                res[n] = jnp.stack([vals[(n, r)] for r in range(W[n].shape[0])]).reshape(W[n].shape)
        return res

    def unpack_sharded(rows):
        flat, res, o = rows.reshape(-1), {}, 0
        for n in sh_keys:
            res[n] = flat[o:o + W[n].size].reshape(W[n].shape)
            o += W[n].size
        return res

    results = []
    for idx in range(4):
        d = {n: out[n][idx] for n in big}
        d.update(unpack_replicated(rep_res[idx]))
        d.update(unpack_sharded(sh_res[idx]))
        results.append(d)

    loss = jnp.sum(arrived[("loss", None)])
    grad_x = dx.reshape(x.shape)
    final = [loss, grad_x]
    for d in results:
        final.extend(d[n] for n in names)
    return tuple(final)
```

```python
import dataclasses
import functools

import jax
import jax.numpy as jnp
from jax import lax
from jax.experimental import pallas as pl
from jax.experimental.pallas import tpu as pltpu

F32 = jnp.float32
BF16 = jnp.bfloat16

N_DEV = 8
D_MODEL = 1024
D_FF = 4096
RMS_EPS = 1e-5
LN_EPS = 1e-5
CHUNK = 128
GM_GROUPS = 8
SSM_HEADS = 16
SSM_HEADDIM = 64
SSM_GROUPS = 4
SSM_STATE = 128
SSM_CONV = 4
CONV_DIM = 2048
IN_EVEN = 5136
REST_W = 3200
ATTN_HEADS = 16
ATTN_KV = 2
HEAD_DIM = 64
QKV_DIM = 1280
LANES = 128
HALO = 8
PACK_W = 1024

ADAM_LR = 0.001
ADAM_B1 = 0.9
ADAM_B2 = 0.999
ADAM_EPS = 1e-08
ADAM_WD = 0.01
ADAM_STEP = 10

VMEM_LIMIT_BYTES = 56 * 1024 * 1024


_NN = (((1,), (0,)), ((), ()))
_NT = (((1,), (1,)), ((), ()))
_TN = (((0,), (0,)), ((), ()))


def _dg(a, b, dims):
    return lax.dot_general(a.astype(BF16), b.astype(BF16), dims, preferred_element_type=F32)


@jax.custom_vjp
def _nn(a, b):
    return _dg(a, b, _NN)


@jax.custom_vjp
def _nt(a, b):
    return _dg(a, b, _NT)


@jax.custom_vjp
def _tn(a, b):
    return _dg(a, b, _TN)


_nn.defvjp(lambda a, b: (_dg(a, b, _NN), (a, b)), lambda r, g: (_nt(g, r[1]), _tn(r[0], g)))
_nt.defvjp(lambda a, b: (_dg(a, b, _NT), (a, b)), lambda r, g: (_nn(g, r[1]), _tn(g, r[0])))
_tn.defvjp(lambda a, b: (_dg(a, b, _TN), (a, b)), lambda r, g: (_nt(r[1], g), _nn(r[0], g)))


def _split3_dot(tri, x):
    x1 = x.astype(BF16)
    r1 = x - x1.astype(F32)
    x2 = r1.astype(BF16)
    x3 = (r1 - x2.astype(F32)).astype(BF16)
    t = tri.astype(BF16)
    dot = lambda p: lax.dot_general(t, p, _NN, preferred_element_type=F32)
    return dot(x1) + dot(x2) + dot(x3)


def _tri(lower):
    r = lax.broadcasted_iota(jnp.int32, (CHUNK, CHUNK), 0)
    c = lax.broadcasted_iota(jnp.int32, (CHUNK, CHUNK), 1)
    return jnp.where((r >= c) if lower else (r <= c), 1.0, 0.0).astype(F32)


@jax.custom_vjp
def _cumsum_rows(x):
    return _split3_dot(_tri(True), x)


_cumsum_rows.defvjp(lambda x: (_split3_dot(_tri(True), x), None), lambda _, g: (_split3_dot(_tri(False), g),))


def _sigmoid(x):
    return 1.0 / (1.0 + jnp.exp(-x))


def _silu(x):
    return x * _sigmoid(x)


def _softplus(x):
    return jnp.maximum(x, 0.0) + jnp.log(1.0 + jnp.exp(-jnp.abs(x)))


def _gelu_tanh(x):
    return 0.5 * x * (1.0 + jnp.tanh(0.7978845608028654 * (x + 0.044715 * (x * x * x))))


def _rmsnorm(x, g):
    return x * lax.rsqrt(jnp.mean(x * x, axis=-1, keepdims=True) + RMS_EPS) * g


def _gmlp_chunk(u, v, ln_g, ln_b, w_s, b_s):
    gu = _gelu_tanh(u)
    gv = _gelu_tanh(v)
    mu = jnp.mean(gv, axis=-1, keepdims=True)
    var = jnp.mean(jnp.square(gv - mu), axis=-1, keepdims=True)
    vn = (gv - mu) * lax.rsqrt(var + LN_EPS) * ln_g + ln_b
    r = lax.broadcasted_iota(jnp.int32, (CHUNK, CHUNK), 0)
    c = lax.broadcasted_iota(jnp.int32, (CHUNK, CHUNK), 1)
    causal = r >= c
    outs = []
    for g in range(GM_GROUPS):
        cols = slice(g * LANES, (g + 1) * LANES)
        mixed = _nn(jnp.where(causal, w_s[g], 0.0), vn[:, cols]) + b_s[g]
        outs.append(gu[:, cols] * mixed)
    return jnp.concatenate(outs, axis=1)


def _lane_pick(row, h):
    lane = lax.broadcasted_iota(jnp.int32, row.shape, 1)
    return jnp.sum(jnp.where(lane == h, row, 0.0), axis=1, keepdims=True)


def _col_pick(m, h):
    lane = lax.broadcasted_iota(jnp.int32, m.shape, 1)
    return jnp.sum(jnp.where(lane == h, m, 0.0), axis=1, keepdims=True)


def _row_pick(m, h):
    sub = lax.broadcasted_iota(jnp.int32, m.shape, 0)
    return jnp.sum(jnp.where(sub == h, m, 0.0), axis=0, keepdims=True)


_PAIRS = SSM_HEADS // 2


def _ssd_chunk(pre, z, dt_raw, h_prev, dt_bias, a_log, d_skip, norm_g):
    xbc = _silu(pre)
    dt = _softplus(dt_raw + dt_bias)
    da = dt * (-jnp.exp(a_log))
    a_cum = _cumsum_rows(da)
    a_cum_t = a_cum.T
    dt_t = dt.T
    r = lax.broadcasted_iota(jnp.int32, (CHUNK, CHUNK), 0)
    c = lax.broadcasted_iota(jnp.int32, (CHUNK, CHUNK), 1)
    causal = r >= c
    lane_lo = lax.broadcasted_iota(jnp.int32, (1, LANES), 1) < SSM_HEADDIM
    last_row = lax.broadcasted_iota(jnp.int32, (CHUNK, 1), 0) == CHUNK - 1
    ys, h_next = [], []
    for j in range(_PAIRS):
        g = j // 2
        xs = xbc[:, j * LANES:(j + 1) * LANES]
        bm = xbc[:, 1024 + g * SSM_STATE:1024 + (g + 1) * SSM_STATE]
        cm = xbc[:, 1536 + g * SSM_STATE:1536 + (g + 1) * SSM_STATE]
        cb = _nt(cm, bm)
        y_diag, to_end, e_cum, c_dec, d_row = [], [], [], [], []
        for h in (2 * j, 2 * j + 1):
            col = _col_pick(a_cum, h)
            row = _row_pick(a_cum_t, h)
            dt_col = _col_pick(dt, h)
            dt_row = _row_pick(dt_t, h)
            decay = jnp.exp(jnp.where(causal, col - row, -jnp.inf))
            y_diag.append(_nn(cb * decay * dt_row, xs))
            last = jnp.sum(jnp.where(last_row, col, 0.0), axis=0, keepdims=True)
            to_end.append(jnp.exp(last - col) * dt_col)
            e_cum.append(jnp.exp(col))
            c_dec.append(jnp.exp(last))
            d_row.append(_lane_pick(d_skip, h))
        pair = lambda lo_hi: jnp.where(lane_lo, lo_hi[0], lo_hi[1])
        states = _tn(bm, xs * pair(to_end))
        y_off = _nn(cm, h_prev[j]) * pair(e_cum)
        ys.append(pair(y_diag) + y_off + xs * pair(d_row))
        h_next.append(pair(c_dec) * h_prev[j] + states)
    y = jnp.concatenate(ys, axis=1) * _silu(z)
    width = D_MODEL // SSM_GROUPS
    y = jnp.concatenate(
        [_rmsnorm(y[:, g * width:(g + 1) * width], norm_g[:, g * width:(g + 1) * width]) for g in range(SSM_GROUPS)],
        axis=1)
    return y, tuple(h_next)


def _shift_down(prev8, x, k):
    if k == 0:
        return x
    win = jnp.concatenate([prev8, x], axis=0)
    return pltpu.roll(win, k, 0)[HALO:]


def _shift_up(x, next8, k):
    if k == 0:
        return x
    n = x.shape[0]
    win = jnp.concatenate([x, next8], axis=0)
    return pltpu.roll(win, n + HALO - k, 0)[:n]


def _conv_pre(prev8, x, w, b):
    out = b + x * w[SSM_CONV - 1:SSM_CONV]
    for i in range(SSM_CONV - 1):
        out = out + _shift_down(prev8, x, SSM_CONV - 1 - i) * w[i:i + 1]
    return out


def _swap_halves(x):
    return pltpu.roll(x, HEAD_DIM, 1)


_PAIRS_PER_KV = ATTN_HEADS // ATTN_KV // 2
_ATTN_SCALE = HEAD_DIM ** -0.5


def _parity_lanes(parity):
    lane = lax.broadcasted_iota(jnp.int32, (1, LANES), 1)
    return (lane >= HEAD_DIM * parity) & (lane < HEAD_DIM * (parity + 1))


def _kv_placed(pair, kv_head):
    mine = jnp.where(_parity_lanes(kv_head), pair, 0.0)
    lo = mine if kv_head == 0 else _swap_halves(mine)
    return lo, _swap_halves(lo)


def _kv_unplaced(d_lo, d_hi, kv_head):
    d = jnp.where(_parity_lanes(0), d_lo, 0.0) + _swap_halves(jnp.where(_parity_lanes(1), d_hi, 0.0))
    return d if kv_head == 0 else _swap_halves(d)


def _attn_probs(q4, k_e, sink, first):
    s = _dg(q4, k_e, _NT) * _ATTN_SCALE
    rows = lax.broadcasted_iota(jnp.int32, s.shape, 0) & (CHUNK - 1)
    cols = lax.broadcasted_iota(jnp.int32, s.shape, 1)
    valid = (cols <= rows + CHUNK) & (cols > rows) & (cols >= CHUNK * first.astype(jnp.int32))
    s = jnp.where(valid, s, -jnp.inf)
    m = jnp.maximum(jnp.max(s, axis=-1, keepdims=True), sink)
    p = jnp.exp(s - m)
    e_sink = jnp.exp(sink - m)
    return p, e_sink, jnp.sum(p, axis=-1, keepdims=True) + e_sink


def _lane_column(col, idx):
    lane = lax.broadcasted_iota(jnp.int32, (1, LANES), 1)
    return jnp.where(lane == idx, col, 0.0)


N_CHIP = 4
N_CORE = 2
_OTHER_CHIPS = (2, 4, 6)


@dataclasses.dataclass
class _Move:
    kind: str
    src: jax.Array
    rows: tuple = None

    def dst_shape(self):
        s = self.src.shape
        shape = {"gather": (N_DEV,) + s, "gather_ici": (N_CHIP, N_CORE) + s, "gather_d2d": s,
                 "scatter_d2d": (N_CHIP,) + s[2:], "scatter_ici": s}[self.kind]
        if self.rows is not None:
            shape = (shape[0], self.rows[1]) + tuple(shape[2:])
        return jax.ShapeDtypeStruct(tuple(shape), self.src.dtype)


def _peer(x, y, c, k):
    return (1 - x if k & 4 else x, 1 - y if k & 2 else y, 1 - c if k & 1 else c)


def _move_copies(moves, srcs, dsts, send_sems, recv_sems, local_sems):
    x, y, c = lax.axis_index("x"), lax.axis_index("y"), lax.axis_index("c")
    chip = 2 * x + y
    me = 2 * chip + c
    sibling = (x, y, 1 - c)
    all_chips = pl.ds(0, N_CHIP)
    local, remote = [], []

    def push(n, k, src, dst, device):
        remote.append(pltpu.make_async_remote_copy(
            src_ref=src, dst_ref=dst, send_sem=send_sems.at[n, k], recv_sem=recv_sems.at[n, k],
            device_id=device, device_id_type=pl.DeviceIdType.MESH))

    for n, mv in enumerate(moves):
        s, d = srcs[n], dsts[n]
        if mv.kind == "gather":
            local.append(pltpu.make_async_copy(s, d.at[me], local_sems.at[n]))
            for k in range(1, N_DEV):
                push(n, k - 1, s, d.at[me], _peer(x, y, c, k))
        elif mv.kind == "gather_ici":
            local.append(pltpu.make_async_copy(s, d.at[chip, c], local_sems.at[n]))
            for k in _OTHER_CHIPS:
                push(n, k - 1, s, d.at[chip, c], _peer(x, y, c, k))
        elif mv.kind == "gather_d2d":
            push(n, 0, d.at[all_chips, c], d.at[all_chips, c], sibling)
        elif mv.kind == "scatter_d2d":
            push(n, 0, s.at[all_chips, 1 - c], d, sibling)
        else:
            assert mv.kind == "scatter_ici", mv.kind
            part = (lambda r: r) if mv.rows is None else (lambda r: r.at[pl.ds(mv.rows[0], mv.rows[1])])
            local.append(pltpu.make_async_copy(part(s.at[chip]), d.at[chip], local_sems.at[n]))
            for k in _OTHER_CHIPS:
                px, py, _ = _peer(x, y, c, k)
                push(n, k - 1, part(s.at[2 * px + py]), d.at[chip], (px, py, c))
    return local, remote


def _move_aliases(moves, n_in, n_out):
    return {n_in + n: n_out + n for n, mv in enumerate(moves) if mv.kind == "gather_d2d"}


def _pcall(body, *, name, grid, in_specs, out_specs, out_shape, scratch_shapes=(), semantics=(), moves=(),
           aliases=None):
    out_shape, out_specs = list(out_shape), list(out_specs)
    in_specs = list(in_specs)
    if not moves:
        call = pl.pallas_call(
            body, name=name, grid=grid, in_specs=in_specs, out_specs=out_specs, out_shape=out_shape,
            scratch_shapes=list(scratch_shapes), input_output_aliases=aliases or {},
            compiler_params=pltpu.CompilerParams(dimension_semantics=tuple(semantics),
                                                 vmem_limit_bytes=VMEM_LIMIT_BYTES))
        return (lambda *args: (list(call(*args)), []))
    n_in, n_out, n_scr, n_mv = len(in_specs), len(out_shape), len(scratch_shapes), len(moves)
    hbm = pl.BlockSpec(memory_space=pltpu.HBM)

    def carrier(*refs):
        ins, rest = refs[:n_in], refs[n_in:]
        srcs, rest = rest[:n_mv], rest[n_mv:]
        outs, rest = rest[:n_out], rest[n_out:]
        dsts, rest = rest[:n_mv], rest[n_mv:]
        scr, (send_sems, recv_sems, local_sems) = rest[:n_scr], rest[n_scr:]
        first = functools.reduce(jnp.logical_and, [pl.program_id(d) == 0 for d in range(len(grid))])
        last = functools.reduce(jnp.logical_and, [pl.program_id(d) == grid[d] - 1 for d in range(len(grid))])

        @pl.when(first)
        def _():
            local, remote = _move_copies(moves, srcs, dsts, send_sems, recv_sems, local_sems)
            for cp in local + remote:
                cp.start()

        body(*ins, *outs, *scr)

        @pl.when(last)
        def _():
            local, remote = _move_copies(moves, srcs, dsts, send_sems, recv_sems, local_sems)
            for cp in remote + local:
                cp.wait()

    call = pl.pallas_call(
        carrier, name=name, grid=grid,
        in_specs=in_specs + [hbm] * n_mv,
        out_specs=out_specs + [hbm] * n_mv,
        out_shape=out_shape + [mv.dst_shape() for mv in moves],
        scratch_shapes=list(scratch_shapes) + [pltpu.SemaphoreType.DMA((n_mv, N_DEV - 1)),
                                               pltpu.SemaphoreType.DMA((n_mv, N_DEV - 1)),
                                               pltpu.SemaphoreType.DMA((n_mv,))],
        input_output_aliases={**(aliases or {}), **_move_aliases(moves, n_in, n_out)},
        compiler_params=pltpu.CompilerParams(dimension_semantics=("arbitrary",) * len(grid),
                                             vmem_limit_bytes=VMEM_LIMIT_BYTES))

    def run(*args):
        res = list(call(*args, *[mv.src for mv in moves]))
        return res[:n_out], res[n_out:]

    return run


def _exchange(moves, *, name, then_d2d=()):
    n_mv, n_fwd = len(moves), len(then_d2d)
    hbm = pl.BlockSpec(memory_space=pltpu.HBM)
    copies_of = {"gather": N_DEV - 1, "gather_ici": len(_OTHER_CHIPS), "gather_d2d": 1, "scatter_d2d": 1,
                 "scatter_ici": len(_OTHER_CHIPS)}
    first_copy = [sum(copies_of[mv.kind] for mv in moves[:n]) for n in range(n_mv)]

    def body(*refs):
        srcs, dsts, sems = refs[:n_mv], refs[n_mv:2 * n_mv], refs[2 * n_mv:]
        local, remote = _move_copies(moves, srcs, dsts, *sems[:3])
        for cp in local + remote:
            cp.start()
        x, y, c = lax.axis_index("x"), lax.axis_index("y"), lax.axis_index("c")
        chip, sibling = 2 * x + y, (x, y, 1 - c)
        passed, passed_on = [], set()

        def to_sibling(f, k, src, slot):
            cp = pltpu.make_async_remote_copy(src_ref=src, dst_ref=slot, send_sem=sems[3].at[f, k],
                                              recv_sem=sems[4].at[f, k], device_id=sibling,
                                              device_id_type=pl.DeviceIdType.MESH)
            cp.start()
            passed.append(cp)

        for f, n in enumerate(then_d2d):
            assert moves[n].kind == "gather_ici"
            d = dsts[n]
            to_sibling(f, 0, srcs[n], d.at[chip, c])
            for i, k in enumerate(_OTHER_CHIPS):
                remote[first_copy[n] + i].wait_recv()
                passed_on.add(first_copy[n] + i)
                px, py, _ = _peer(x, y, c, k)
                to_sibling(f, k - 1, d.at[2 * px + py, c], d.at[2 * px + py, c])
        for i, cp in enumerate(remote):
            if i in passed_on:
                cp.wait_send()
            else:
                cp.wait()
        for cp in local + passed:
            cp.wait()

    sems = [pltpu.SemaphoreType.DMA((n_mv, N_DEV - 1)), pltpu.SemaphoreType.DMA((n_mv, N_DEV - 1)),
            pltpu.SemaphoreType.DMA((n_mv,))]
    if then_d2d:
        sems += [pltpu.SemaphoreType.DMA((n_fwd, N_DEV - 1)), pltpu.SemaphoreType.DMA((n_fwd, N_DEV - 1))]
    return list(pl.pallas_call(
        body, name=name, in_specs=[hbm] * n_mv, out_specs=[hbm] * n_mv,
        out_shape=[mv.dst_shape() for mv in moves], scratch_shapes=sems,
    )(*[mv.src for mv in moves]))


TM = 512
FF_SHARD = D_FF // N_DEV


def _whole(a):
    nd = a.ndim
    return pl.BlockSpec(a.shape, lambda i: (0,) * nd)


def _rows(width, col=0):
    return pl.BlockSpec((TM, width), lambda i: (i, col))


def _acc_row(width):
    return pl.BlockSpec((1, width), lambda i: (0, 0))


def _unpack(res_landed, moves, n_out):
    res, landed = res_landed
    res = res[0] if n_out == 1 else res
    return (res, landed) if moves else res


def _norm_matmul(x, g, w, *, name, emit_y, moves=()):
    t, d = x.shape
    n = w.shape[1]

    def body(x_ref, g_ref, w_ref, *outs):
        y = _rmsnorm(x_ref[...], g_ref[...]).astype(BF16)
        if emit_y:
            outs[0][...] = y
        outs[-1][...] = lax.dot_general(y, w_ref[...], _NN, preferred_element_type=F32)

    shapes = ([jax.ShapeDtypeStruct((t, d), BF16)] if emit_y else []) + [jax.ShapeDtypeStruct((t, n), F32)]
    specs = ([_rows(d)] if emit_y else []) + [_rows(n)]
    return _unpack(_pcall(body, name=name, grid=(t // TM,), in_specs=[_rows(d), _acc_row(d), _whole(w)],
                          out_specs=specs, out_shape=shapes, semantics=("parallel",), moves=moves)(x, g, w),
                   moves, len(shapes))


def _residual_matmul(a, w, res, *, name, bias=None, norm_g=None, w_transposed=False, moves=()):
    t, k = a.shape
    n = w.shape[0 if w_transposed else 1]
    contract = _NT if w_transposed else _NN
    has_res, has_bias, has_norm = res is not None, bias is not None, norm_g is not None

    def body(a_ref, w_ref, *rest):
        rest = list(rest)
        res_ref = rest.pop(0) if has_res else None
        b_ref = rest.pop(0) if has_bias else None
        g_ref = rest.pop(0) if has_norm else None
        h = lax.dot_general(a_ref[...].astype(BF16), w_ref[...], contract, preferred_element_type=F32)
        if has_res:
            h = h + res_ref[...]
        if has_bias:
            h = h + b_ref[...]
        rest[0][...] = h
        if has_norm:
            rest[1][...] = _rmsnorm(h, g_ref[...]).astype(BF16)

    rows_in = [res] if has_res else []
    extra = ([bias] if has_bias else []) + ([norm_g] if has_norm else [])
    shapes = [jax.ShapeDtypeStruct((t, n), F32)] + ([jax.ShapeDtypeStruct((t, n), BF16)] if has_norm else [])
    return _unpack(_pcall(body, name=name, grid=(t // TM,),
                          in_specs=[_rows(k), _whole(w)] + [_rows(n)] * len(rows_in) + [_acc_row(n)] * len(extra),
                          out_specs=[_rows(n)] * len(shapes), out_shape=shapes, semantics=("parallel",),
                          moves=moves)(a, w, *rows_in, *extra), moves, len(shapes))


def _mlp_up(y, w_cols, *, name, moves=()):
    t, d = y.shape

    def body(y_ref, w_ref, up_ref):
        yv = y_ref[...]
        for j in range(N_DEV):
            up_ref[:, j * FF_SHARD:(j + 1) * FF_SHARD] = lax.dot_general(
                yv, w_ref[j], _NN, preferred_element_type=F32).astype(up_ref.dtype)

    return _unpack(_pcall(body, name=name, grid=(t // TM,), in_specs=[_rows(d), _whole(w_cols)],
                          out_specs=[_rows(D_FF)], out_shape=[jax.ShapeDtypeStruct((t, D_FF), BF16)],
                          semantics=("parallel",), moves=moves)(y, w_cols), moves, 1)


def _sq_relu(u):
    return jnp.square(jnp.maximum(u.astype(F32), 0.0))


def _down_blocks(w_refs):
    for j in range(N_DEV):
        off = j * FF_SHARD
        for w_ref in w_refs:
            yield off, w_ref.shape[1], w_ref[j]
            off += w_ref.shape[1]


def _mlp_down(up, w_rows, res, *, name, norm_g=None, moves=()):
    t = up.shape[0]
    has_norm = norm_g is not None
    n_w = len(w_rows)

    def body(up_ref, *rest):
        w_refs, res_ref, rest = rest[:n_w], rest[n_w], rest[n_w + 1:]
        h = res_ref[...]
        for off, rows, w_blk in _down_blocks(w_refs):
            act = _sq_relu(up_ref[:, off:off + rows]).astype(BF16)
            h = h + lax.dot_general(act, w_blk, _NN, preferred_element_type=F32)
        if has_norm:
            g_ref, h_ref, y_ref = rest
            y_ref[...] = _rmsnorm(h, g_ref[...]).astype(BF16)
        else:
            (h_ref,) = rest
        h_ref[...] = h

    shapes = [jax.ShapeDtypeStruct((t, D_MODEL), F32)] + ([jax.ShapeDtypeStruct((t, D_MODEL), BF16)] if has_norm else [])
    return _unpack(_pcall(body, name=name, grid=(t // TM,),
                          in_specs=[_rows(D_FF)] + [_whole(w) for w in w_rows] + [_rows(D_MODEL)]
                          + ([_acc_row(D_MODEL)] if has_norm else []),
                          out_specs=[_rows(D_MODEL)] * len(shapes), out_shape=shapes, semantics=("parallel",),
                          moves=moves)(up, *w_rows, res, *([norm_g] if has_norm else [])), moves, len(shapes))


def _mlp_down_dx(dh, w_rows, up, *, name, moves=()):
    t = up.shape[0]
    n_w = len(w_rows)

    def body(dh_ref, *rest):
        w_refs, (up_ref, o_ref) = rest[:n_w], rest[n_w:]
        dhv = dh_ref[...]
        for off, rows, w_blk in _down_blocks(w_refs):
            cols = slice(off, off + rows)
            d_act = lax.dot_general(dhv, w_blk, _NT, preferred_element_type=F32)
            o_ref[:, cols] = (d_act * (2.0 * jnp.maximum(up_ref[:, cols].astype(F32), 0.0))).astype(o_ref.dtype)

    return _unpack(_pcall(body, name=name, grid=(t // TM,),
                          in_specs=[_rows(D_MODEL)] + [_whole(w) for w in w_rows] + [_rows(D_FF)],
                          out_specs=[_rows(D_FF)], out_shape=[jax.ShapeDtypeStruct((t, D_FF), BF16)],
                          semantics=("parallel",), moves=moves)(dh, *w_rows, up), moves, 1)


def _dw_by_cols(x, dy, *, name, tn, by_device=False, moves=()):
    t, k = x.shape
    n = dy.shape[1]
    assert n % tn == 0, (name, n, tn)

    def body(x_ref, dy_ref, o_ref):
        o_ref[...] = lax.dot_general(x_ref[...].astype(BF16), dy_ref[...].astype(BF16), _TN,
                                     preferred_element_type=F32).astype(o_ref.dtype)

    if by_device:
        out_spec, out_shape = pl.BlockSpec((None, k, tn), lambda j: (j, 0, 0)), (n // tn, k, tn)
    else:
        out_spec, out_shape = pl.BlockSpec((k, tn), lambda j: (0, j)), (k, n)
    return _unpack(_pcall(body, name=name, grid=(n // tn,),
                          in_specs=[_whole(x), pl.BlockSpec((t, tn), lambda j: (0, j))],
                          out_specs=[out_spec], out_shape=[jax.ShapeDtypeStruct(out_shape, BF16)],
                          semantics=("parallel",), moves=moves)(x, dy), moves, 1)


def _dw_by_rows(x, dy, *, name, tk, square_relu=False, moves=()):
    t, k = x.shape
    n = dy.shape[1]
    assert k % tk == 0, (name, k, tk)

    def body(x_ref, dy_ref, o_ref):
        xv = _sq_relu(x_ref[...]) if square_relu else x_ref[...]
        o_ref[...] = lax.dot_general(xv.astype(BF16), dy_ref[...].astype(BF16), _TN,
                                     preferred_element_type=F32).astype(o_ref.dtype)

    return _unpack(_pcall(body, name=name, grid=(k // tk,),
                          in_specs=[pl.BlockSpec((t, tk), lambda j: (0, j)), _whole(dy)],
                          out_specs=[pl.BlockSpec((tk, n), lambda j: (j, 0))],
                          out_shape=[jax.ShapeDtypeStruct((k, n), BF16)],
                          semantics=("parallel",), moves=moves)(x, dy), moves, 1)


def _dx(dy, w, *, name, partial=None, moves=()):
    t, k = dy.shape
    n = w.shape[0]
    has_partial = partial is not None

    def body(dy_ref, w_ref, *rest):
        out = lax.dot_general(dy_ref[...].astype(BF16), w_ref[...], _NT, preferred_element_type=F32)
        if has_partial:
            out = out + rest[0][...]
        rest[-1][...] = out

    return _unpack(_pcall(body, name=name, grid=(t // TM,),
                          in_specs=[_rows(k), _whole(w)] + ([_rows(n)] if has_partial else []),
                          out_specs=[_rows(n)], out_shape=[jax.ShapeDtypeStruct((t, n), F32)],
                          semantics=("parallel",), moves=moves)(dy, w, *([partial] if has_partial else [])),
                   moves, 1)


def _dx_norm(dy, w, h, g, dres, *, name, partial=None, by_device_cols=False, w_transposed=False, moves=()):
    t, k = dy.shape
    d = h.shape[1]
    has_partial = partial is not None

    def body(dy_ref, w_ref, h_ref, g_ref, dres_ref, *rest):
        if by_device_cols:
            kc = k // N_DEV
            d_y = jnp.zeros((TM, d), F32)
            for j in range(N_DEV):
                d_y = d_y + lax.dot_general(dy_ref[:, j * kc:(j + 1) * kc].astype(BF16), w_ref[j], _NT,
                                            preferred_element_type=F32)
        else:
            d_y = lax.dot_general(dy_ref[...].astype(BF16), w_ref[...], _NN if w_transposed else _NT,
                                  preferred_element_type=F32)
        if has_partial:
            d_y = d_y + rest[0][...]
        dh_ref, dhb_ref, dg_ref, cs_ref = rest[-4:]
        _, vjp = jax.vjp(_rmsnorm, h_ref[...], g_ref[...])
        dh, dg = vjp(d_y)
        dh = dh + dres_ref[...]
        dh_ref[...] = dh
        dhb_ref[...] = dh.astype(BF16)

        @pl.when(pl.program_id(0) == 0)
        def _():
            dg_ref[...] = jnp.zeros_like(dg_ref)
            cs_ref[...] = jnp.zeros_like(cs_ref)

        dg_ref[...] += dg
        cs_ref[...] += jnp.sum(dh, axis=0, keepdims=True)

    shapes = [jax.ShapeDtypeStruct((t, d), F32), jax.ShapeDtypeStruct((t, d), BF16),
              jax.ShapeDtypeStruct((1, d), F32), jax.ShapeDtypeStruct((1, d), F32)]
    return _unpack(_pcall(body, name=name, grid=(t // TM,),
                          in_specs=[_rows(k), _whole(w), _rows(d), _acc_row(d), _rows(d)]
                          + ([_rows(d)] if has_partial else []),
                          out_specs=[_rows(d), _rows(d), _acc_row(d), _acc_row(d)], out_shape=shapes,
                          semantics=("arbitrary",), moves=moves)(dy, w, h, g, dres, *([partial] if has_partial else [])),
                   moves, 4)


def _pair_add(by_core, theirs, core, *, name, tb=512):
    n_chip, _, r, c = by_core.shape
    tb = min(tb, r)
    assert r % tb == 0, (name, r, tb)

    def body(core_ref, a_ref, b_ref, o_ref):
        del core_ref
        o_ref[...] = (a_ref[...].astype(F32) + b_ref[...].astype(F32)).astype(o_ref.dtype)

    blk = pl.BlockSpec((None, tb, c), lambda ch, i, core_ref: (ch, i, 0))
    return pl.pallas_call(
        body, name=name,
        grid_spec=pltpu.PrefetchScalarGridSpec(
            num_scalar_prefetch=1, grid=(n_chip, r // tb),
            in_specs=[pl.BlockSpec((None, None, tb, c), lambda ch, i, core_ref: (ch, core_ref[0], i, 0)), blk],
            out_specs=blk),
        out_shape=jax.ShapeDtypeStruct((n_chip, r, c), by_core.dtype),
        compiler_params=pltpu.CompilerParams(dimension_semantics=("parallel", "parallel"),
                                             vmem_limit_bytes=VMEM_LIMIT_BYTES),
    )(core, by_core, theirs)


def _colsum(a, *, name, tb=512):
    t, d = a.shape

    def body(a_ref, o_ref):
        @pl.when(pl.program_id(0) == 0)
        def _():
            o_ref[...] = jnp.zeros_like(o_ref)

        o_ref[...] += jnp.sum(a_ref[...].astype(F32), axis=0, keepdims=True)

    return _pcall(
        body, name=name, grid=(t // tb,),
        in_specs=[pl.BlockSpec((tb, d), lambda i: (i, 0))],
        out_specs=[pl.BlockSpec((1, d), lambda i: (0, 0))],
        out_shape=[jax.ShapeDtypeStruct((1, d), F32)],
        semantics=("arbitrary",),
    )(a)[0][0]


def _mlp_down_loss(up, w_rows, res, g, target, *, name):
    t, d = res.shape
    n_w = len(w_rows)

    def body(up_ref, *rest):
        w_refs, (res_ref, g_ref, tgt_ref, loss_ref, dh_ref, dhb_ref, dg_ref) = rest[:n_w], rest[n_w:]
        h = res_ref[...]
        for off, rows, w_blk in _down_blocks(w_refs):
            act = _sq_relu(up_ref[:, off:off + rows]).astype(BF16)
            h = h + lax.dot_general(act, w_blk, _NN, preferred_element_type=F32)

        def f(hh, gg):
            err = jnp.square(_rmsnorm(hh, gg) - tgt_ref[...])
            return 0.5 * jnp.sum(jnp.mean(err, axis=-1, keepdims=True), axis=0, keepdims=True)

        val, vjp = jax.vjp(f, h, g_ref[...])
        dh, dg = vjp(jnp.ones((1, 1), F32))
        dh_ref[...] = dh
        dhb_ref[...] = dh.astype(BF16)

        @pl.when(pl.program_id(0) == 0)
        def _():
            loss_ref[...] = jnp.zeros_like(loss_ref)
            dg_ref[...] = jnp.zeros_like(dg_ref)

        loss_ref[...] += val
        dg_ref[...] += dg

    return _pcall(
        body, name=name, grid=(t // TM,),
        in_specs=[_rows(D_FF)] + [_whole(w) for w in w_rows] + [_rows(d), _acc_row(d), _rows(d)],
        out_specs=[pl.BlockSpec((8, LANES), lambda i: (0, 0)), _rows(d), _rows(d), _acc_row(d)],
        out_shape=[jax.ShapeDtypeStruct((8, LANES), F32), jax.ShapeDtypeStruct((t, d), F32),
                   jax.ShapeDtypeStruct((t, d), BF16), jax.ShapeDtypeStruct((1, d), F32)],
        semantics=("arbitrary",),
    )(up, *w_rows, res, g, target)[0]


def _gmlp_fwd(proj_uv, ln_g, ln_b, w_s, b_s, *, name, moves=()):
    t = proj_uv.shape[0]
    w = D_MODEL

    def body(u_ref, v_ref, g_ref, b_ref, w_ref, bs_ref, o_ref):
        o_ref[...] = _gmlp_chunk(u_ref[...], v_ref[...], g_ref[...], b_ref[...], w_ref[...],
                                 bs_ref[...]).astype(o_ref.dtype)

    row = pl.BlockSpec((1, w), lambda i: (0, 0))
    res, landed = _pcall(
        body, name=name, grid=(t // CHUNK,),
        in_specs=[pl.BlockSpec((CHUNK, w), lambda i: (i, 0)), pl.BlockSpec((CHUNK, w), lambda i: (i, 1)), row, row,
                  pl.BlockSpec((GM_GROUPS, CHUNK, CHUNK), lambda i: (0, 0, 0)),
                  pl.BlockSpec((GM_GROUPS, CHUNK, 1), lambda i: (0, 0, 0))],
        out_specs=[pl.BlockSpec((CHUNK, w), lambda i: (i, 0))],
        out_shape=[jax.ShapeDtypeStruct((t, 2 * w), BF16)],
        semantics=("parallel",), moves=moves,
    )(proj_uv, proj_uv, ln_g, ln_b, w_s, b_s)
    return (res[0], landed) if moves else res[0]


def _gmlp_bwd(proj_uv, d_mix, ln_g, ln_b, w_s, b_s, *, name, moves=()):
    t = proj_uv.shape[0]
    w = D_MODEL

    def body(u_ref, v_ref, da_ref, g_ref, b_ref, w_ref, bs_ref, duv_ref, dg_ref, db_ref, dw_ref, dbs_ref):
        _, vjp = jax.vjp(_gmlp_chunk, u_ref[...], v_ref[...], g_ref[...], b_ref[...], w_ref[...], bs_ref[...])
        du, dv, dg, db, dw, dbs = vjp(da_ref[...])
        duv_ref[:, :w] = du.astype(duv_ref.dtype)
        duv_ref[:, w:] = dv.astype(duv_ref.dtype)

        @pl.when(pl.program_id(0) == 0)
        def _():
            dg_ref[...] = jnp.zeros_like(dg_ref)
            db_ref[...] = jnp.zeros_like(db_ref)
            dw_ref[...] = jnp.zeros_like(dw_ref)
            dbs_ref[...] = jnp.zeros_like(dbs_ref)

        dg_ref[...] += dg
        db_ref[...] += db
        dw_ref[...] += dw
        dbs_ref[...] += dbs

    row = pl.BlockSpec((1, w), lambda i: (0, 0))
    ws = pl.BlockSpec((GM_GROUPS, CHUNK, CHUNK), lambda i: (0, 0, 0))
    bs = pl.BlockSpec((GM_GROUPS, CHUNK, 1), lambda i: (0, 0, 0))
    res, landed = _pcall(
        body, name=name, grid=(t // CHUNK,),
        in_specs=[pl.BlockSpec((CHUNK, w), lambda i: (i, 0)), pl.BlockSpec((CHUNK, w), lambda i: (i, 1)),
                  pl.BlockSpec((CHUNK, w), lambda i: (i, 0)), row, row, ws, bs],
        out_specs=[pl.BlockSpec((CHUNK, 2 * w), lambda i: (i, 0)), row, row, ws, bs],
        out_shape=[jax.ShapeDtypeStruct((t, 2 * w), BF16), jax.ShapeDtypeStruct((1, w), F32),
                   jax.ShapeDtypeStruct((1, w), F32), jax.ShapeDtypeStruct((GM_GROUPS, CHUNK, CHUNK), F32),
                   jax.ShapeDtypeStruct((GM_GROUPS, CHUNK, 1), F32)],
        semantics=("arbitrary",), moves=moves,
    )(proj_uv, proj_uv, d_mix, ln_g, ln_b, w_s, b_s)
    return (res, landed) if moves else res


_HALO_PER_CHUNK = CHUNK // HALO
_DT_BLOCK = (CONV_DIM + D_MODEL) // LANES


def _ssd_fwd(proj_rest, mix, conv_w, conv_b, dt_bias, a_log, d_skip, norm_g, *, name, moves=()):
    t = proj_rest.shape[0]
    nc = t // CHUNK

    def body(x_ref, prev_ref, z_ref, dt_ref, mix_ref, cw_ref, cb_ref, dtb_ref, al_ref, ds_ref, ng_ref, y_ref, hs_ref,
             pre_ref, h_scr):
        del mix_ref
        i = pl.program_id(0)

        @pl.when(i == 0)
        def _():
            h_scr[...] = jnp.zeros_like(h_scr)

        prev8 = jnp.where(i == 0, 0.0, prev_ref[...])
        pre = _conv_pre(prev8, x_ref[...], cw_ref[...], cb_ref[...])
        pre_ref[...] = pre
        hs_ref[0] = h_scr[...]
        h_prev = tuple(h_scr[j] for j in range(_PAIRS))
        y, h_next = _ssd_chunk(pre, z_ref[...], dt_ref[...], h_prev, dtb_ref[...], al_ref[...], ds_ref[...],
                               ng_ref[...])
        y_ref[...] = y.astype(y_ref.dtype)
        for j in range(_PAIRS):
            h_scr[j] = h_next[j]

    small = pl.BlockSpec((1, LANES), lambda i: (0, 0))
    res, landed = _pcall(
        body, name=name, grid=(nc,),
        in_specs=[pl.BlockSpec((CHUNK, CONV_DIM), lambda i: (i, 0)),
                  pl.BlockSpec((HALO, CONV_DIM), lambda i: (jnp.maximum(i * _HALO_PER_CHUNK - 1, 0), 0)),
                  pl.BlockSpec((CHUNK, D_MODEL), lambda i: (i, CONV_DIM // D_MODEL)),
                  pl.BlockSpec((CHUNK, LANES), lambda i: (i, _DT_BLOCK)),
                  pl.BlockSpec(memory_space=pl.ANY),
                  pl.BlockSpec((SSM_CONV, CONV_DIM), lambda i: (0, 0)),
                  pl.BlockSpec((1, CONV_DIM), lambda i: (0, 0)),
                  small, small, small, pl.BlockSpec((1, D_MODEL), lambda i: (0, 0))],
        out_specs=[pl.BlockSpec((CHUNK, D_MODEL), lambda i: (i, 1)),
                   pl.BlockSpec((1, _PAIRS, SSM_STATE, LANES), lambda i: (i, 0, 0, 0)),
                   pl.BlockSpec((CHUNK, CONV_DIM), lambda i: (i, 0))],
        out_shape=[jax.ShapeDtypeStruct((t, 2 * D_MODEL), BF16),
                   jax.ShapeDtypeStruct((nc, _PAIRS, SSM_STATE, LANES), F32),
                   jax.ShapeDtypeStruct((t, CONV_DIM), F32)],
        scratch_shapes=[pltpu.VMEM((_PAIRS, SSM_STATE, LANES), F32)],
        semantics=("arbitrary",), moves=moves, aliases={4: 0},
    )(proj_rest, proj_rest, proj_rest, proj_rest, mix, conv_w, conv_b, dt_bias, a_log, d_skip, norm_g)
    return (res, landed) if moves else res


def _ssd_bwd(proj_rest, pre, h_states, d_mix, dt_bias, a_log, d_skip, norm_g, *, name, moves=()):
    t = proj_rest.shape[0]
    nc = t // CHUNK

    def body(pre_ref, z_ref, dt_ref, hs_ref, dy_ref, dtb_ref, al_ref, ds_ref, ng_ref,
             dpre_ref, dz_ref, ddt_ref, ddtb_ref, dal_ref, dds_ref, dng_ref, dh_scr):
        i = pl.program_id(0)

        @pl.when(i == 0)
        def _():
            dh_scr[...] = jnp.zeros_like(dh_scr)
            ddtb_ref[...] = jnp.zeros_like(ddtb_ref)
            dal_ref[...] = jnp.zeros_like(dal_ref)
            dds_ref[...] = jnp.zeros_like(dds_ref)
            dng_ref[...] = jnp.zeros_like(dng_ref)

        h_prev = tuple(hs_ref[0, j] for j in range(_PAIRS))
        _, vjp = jax.vjp(_ssd_chunk, pre_ref[...], z_ref[...], dt_ref[...], h_prev, dtb_ref[...], al_ref[...],
                         ds_ref[...], ng_ref[...])
        dpre, dz, ddt, dh_prev, ddtb, dal, dds, dng = vjp((dy_ref[...], tuple(dh_scr[j] for j in range(_PAIRS))))
        dpre_ref[...] = dpre
        dz_ref[...] = dz.astype(dz_ref.dtype)
        ddt_ref[...] = ddt.astype(ddt_ref.dtype)
        for j in range(_PAIRS):
            dh_scr[j] = dh_prev[j]
        ddtb_ref[...] += ddtb
        dal_ref[...] += dal
        dds_ref[...] += dds
        dng_ref[...] += dng

    rev = lambda i: nc - 1 - i
    small = pl.BlockSpec((1, LANES), lambda i: (0, 0))
    wide = pl.BlockSpec((1, D_MODEL), lambda i: (0, 0))
    res, landed = _pcall(
        body, name=name, grid=(nc,),
        in_specs=[pl.BlockSpec((CHUNK, CONV_DIM), lambda i: (rev(i), 0)),
                  pl.BlockSpec((CHUNK, D_MODEL), lambda i: (rev(i), CONV_DIM // D_MODEL)),
                  pl.BlockSpec((CHUNK, LANES), lambda i: (rev(i), _DT_BLOCK)),
                  pl.BlockSpec((1, _PAIRS, SSM_STATE, LANES), lambda i: (rev(i), 0, 0, 0)),
                  pl.BlockSpec((CHUNK, D_MODEL), lambda i: (rev(i), 1)),
                  small, small, small, wide],
        out_specs=[pl.BlockSpec((CHUNK, CONV_DIM), lambda i: (rev(i), 0)),
                   pl.BlockSpec((CHUNK, D_MODEL), lambda i: (rev(i), 0)),
                   pl.BlockSpec((CHUNK, LANES), lambda i: (rev(i), 0)),
                   small, small, small, wide],
        out_shape=[jax.ShapeDtypeStruct((t, CONV_DIM), F32), jax.ShapeDtypeStruct((t, D_MODEL), BF16),
                   jax.ShapeDtypeStruct((t, LANES), BF16),
                   jax.ShapeDtypeStruct((1, LANES), F32), jax.ShapeDtypeStruct((1, LANES), F32),
                   jax.ShapeDtypeStruct((1, LANES), F32), jax.ShapeDtypeStruct((1, D_MODEL), F32)],
        scratch_shapes=[pltpu.VMEM((_PAIRS, SSM_STATE, LANES), F32)],
        semantics=("arbitrary",), moves=moves,
    )(pre, proj_rest, proj_rest, h_states, d_mix, dt_bias, a_log, d_skip, norm_g)
    return (res, landed) if moves else res


def _conv_bwd(proj_rest, dpre, dz, ddt, conv_w, *, name, tb=256, moves=()):
    t = proj_rest.shape[0]
    nb = t // tb
    per = tb // HALO

    def body(x_ref, prev_ref, dpre_ref, next_ref, dz_ref, ddt_ref, cw_ref, drest_ref, dcw_ref, dcb_ref):
        i = pl.program_id(0)

        @pl.when(i == 0)
        def _():
            dcw_ref[...] = jnp.zeros_like(dcw_ref)
            dcb_ref[...] = jnp.zeros_like(dcb_ref)

        x = x_ref[...]
        dp = dpre_ref[...]
        w = cw_ref[...]
        prev8 = jnp.where(i == 0, 0.0, prev_ref[...])
        next8 = jnp.where(i == nb - 1, 0.0, next_ref[...])
        dx = dp * w[SSM_CONV - 1:SSM_CONV]
        for j in range(SSM_CONV - 1):
            dx = dx + _shift_up(dp, next8, SSM_CONV - 1 - j) * w[j:j + 1]
        drest_ref[:, :CONV_DIM] = dx.astype(drest_ref.dtype)
        drest_ref[:, CONV_DIM:CONV_DIM + D_MODEL] = dz_ref[...].astype(drest_ref.dtype)
        drest_ref[:, CONV_DIM + D_MODEL:] = ddt_ref[...].astype(drest_ref.dtype)
        for j in range(SSM_CONV):
            dcw_ref[j:j + 1, :] += jnp.sum(dp * _shift_down(prev8, x, SSM_CONV - 1 - j), axis=0, keepdims=True)
        dcb_ref[...] += jnp.sum(dp, axis=0, keepdims=True)

    res, landed = _pcall(
        body, name=name, grid=(nb,),
        in_specs=[pl.BlockSpec((tb, CONV_DIM), lambda i: (i, 0)),
                  pl.BlockSpec((HALO, CONV_DIM), lambda i: (jnp.maximum(i * per - 1, 0), 0)),
                  pl.BlockSpec((tb, CONV_DIM), lambda i: (i, 0)),
                  pl.BlockSpec((HALO, CONV_DIM), lambda i: (jnp.minimum((i + 1) * per, nb * per - 1), 0)),
                  pl.BlockSpec((tb, D_MODEL), lambda i: (i, 0)),
                  pl.BlockSpec((tb, LANES), lambda i: (i, 0)),
                  pl.BlockSpec((SSM_CONV, CONV_DIM), lambda i: (0, 0))],
        out_specs=[pl.BlockSpec((tb, REST_W), lambda i: (i, 0)),
                   pl.BlockSpec((SSM_CONV, CONV_DIM), lambda i: (0, 0)),
                   pl.BlockSpec((1, CONV_DIM), lambda i: (0, 0))],
        out_shape=[jax.ShapeDtypeStruct((t, REST_W), BF16), jax.ShapeDtypeStruct((SSM_CONV, CONV_DIM), F32),
                   jax.ShapeDtypeStruct((1, CONV_DIM), F32)],
        semantics=("arbitrary",), moves=moves,
    )(proj_rest, proj_rest, dpre, dpre, dz, ddt, conv_w)
    return (res, landed) if moves else res


_KV_BLOCK = D_MODEL // (2 * LANES)
_SINK_ROWS = _PAIRS_PER_KV * CHUNK


def _stack_pairs(ref, kv_head):
    base = kv_head * _PAIRS_PER_KV
    return jnp.concatenate([ref[:, (base + p) * LANES:(base + p + 1) * LANES] for p in range(_PAIRS_PER_KV)], axis=0)


def _attn_fwd(qkv, sinks, *, name, moves=()):
    t = qkv.shape[0]
    nb = t // CHUNK

    def body(q_ref, kvp_ref, kvc_ref, s_ref, o_ref, p_ref, st_ref):
        first = pl.program_id(0) == 0
        kv = jnp.concatenate([kvp_ref[...], kvc_ref[...]], axis=0)
        stats = jnp.zeros((_SINK_ROWS, LANES), F32)
        for j in range(ATTN_KV):
            q4 = _stack_pairs(q_ref, j)
            ks, vs = _kv_placed(kv[:, :LANES], j), _kv_placed(kv[:, LANES:], j)
            out = None
            for e in range(2):
                p, e_sink, den = _attn_probs(q4, ks[e], s_ref[j, e], first)
                inv = 1.0 / den
                o = _dg(p, vs[e], _NN) * inv
                out = o if out is None else out + o
                p_ref[0, 2 * j + e] = p.astype(p_ref.dtype)
                stats = stats + _lane_column(inv, 2 * j + e) + _lane_column(e_sink, 4 + 2 * j + e)
            for pair in range(_PAIRS_PER_KV):
                col = (j * _PAIRS_PER_KV + pair) * LANES
                o_ref[:, col:col + LANES] = out[pair * CHUNK:(pair + 1) * CHUNK].astype(o_ref.dtype)
        st_ref[0] = stats

    return _unpack(_pcall(
        body, name=name, grid=(nb,),
        in_specs=[pl.BlockSpec((CHUNK, D_MODEL), lambda i: (i, 0)),
                  pl.BlockSpec((CHUNK, 2 * LANES), lambda i: (jnp.maximum(i - 1, 0), _KV_BLOCK)),
                  pl.BlockSpec((CHUNK, 2 * LANES), lambda i: (i, _KV_BLOCK)),
                  pl.BlockSpec((ATTN_KV, 2, _SINK_ROWS, 1), lambda i: (0, 0, 0, 0))],
        out_specs=[pl.BlockSpec((CHUNK, D_MODEL), lambda i: (i, 0)),
                   pl.BlockSpec((1, 2 * ATTN_KV, _SINK_ROWS, 2 * CHUNK), lambda i: (i, 0, 0, 0)),
                   pl.BlockSpec((1, _SINK_ROWS, LANES), lambda i: (i, 0, 0))],
        out_shape=[jax.ShapeDtypeStruct((t, D_MODEL), BF16),
                   jax.ShapeDtypeStruct((nb, 2 * ATTN_KV, _SINK_ROWS, 2 * CHUNK), BF16),
                   jax.ShapeDtypeStruct((nb, _SINK_ROWS, LANES), F32)],
        semantics=("parallel",), moves=moves,
    )(qkv, qkv, qkv, sinks), moves, 3)


def _attn_bwd(qkv, probs, stats, attn, d_o, *, name, moves=()):
    t = qkv.shape[0]
    nb = t // CHUNK

    def body(q_ref, kvp_ref, kvc_ref, p_ref, st_ref, o_ref, do_ref, dqkv_ref, ds_ref, dkv_scr):
        @pl.when(pl.program_id(0) == 0)
        def _():
            dkv_scr[...] = jnp.zeros_like(dkv_scr)
            ds_ref[...] = jnp.zeros_like(ds_ref)

        kv = jnp.concatenate([kvp_ref[...], kvc_ref[...]], axis=0)
        table = st_ref[0]
        d_k = jnp.zeros((2 * CHUNK, LANES), F32)
        d_v = jnp.zeros((2 * CHUNK, LANES), F32)
        for j in range(ATTN_KV):
            q4, do4, o4 = _stack_pairs(q_ref, j), _stack_pairs(do_ref, j), _stack_pairs(o_ref, j).astype(F32)
            ks, vs = _kv_placed(kv[:, :LANES], j), _kv_placed(kv[:, LANES:], j)
            dq4, dk, dv = None, [], []
            for e in range(2):
                p = p_ref[0, 2 * j + e].astype(F32)
                inv, e_sink = _col_pick(table, 2 * j + e), _col_pick(table, 4 + 2 * j + e)
                do_e = jnp.where(_parity_lanes(e), do4, 0.0)
                d_num = do_e * inv
                d_den = -jnp.sum(do_e * o4, axis=1, keepdims=True) * inv
                ds = p * (_dg(d_num, vs[e], _NT) + d_den)
                ds_ref[j, e] += d_den * e_sink
                dq = _dg(ds, ks[e], _NN) * _ATTN_SCALE
                dq4 = dq if dq4 is None else dq4 + dq
                dk.append(_dg(ds, q4, _TN) * _ATTN_SCALE)
                dv.append(_dg(p, d_num, _TN))
            for pair in range(_PAIRS_PER_KV):
                col = (j * _PAIRS_PER_KV + pair) * LANES
                dqkv_ref[:, col:col + LANES] = dq4[pair * CHUNK:(pair + 1) * CHUNK]
            d_k = d_k + _kv_unplaced(dk[0], dk[1], j)
            d_v = d_v + _kv_unplaced(dv[0], dv[1], j)
        d_kv = jnp.concatenate([d_k, d_v], axis=1)
        dqkv_ref[:, D_MODEL:] = d_kv[CHUNK:] + dkv_scr[...]
        dkv_scr[...] = d_kv[:CHUNK]

    cur = lambda i: (nb - 1 - i, 0)
    sk = pl.BlockSpec((ATTN_KV, 2, _SINK_ROWS, 1), lambda i: (0, 0, 0, 0))
    res, landed = _pcall(
        body, name=name, grid=(nb,),
        in_specs=[pl.BlockSpec((CHUNK, D_MODEL), cur),
                  pl.BlockSpec((CHUNK, 2 * LANES), lambda i: (jnp.maximum(nb - 2 - i, 0), _KV_BLOCK)),
                  pl.BlockSpec((CHUNK, 2 * LANES), lambda i: (nb - 1 - i, _KV_BLOCK)),
                  pl.BlockSpec((1, 2 * ATTN_KV, _SINK_ROWS, 2 * CHUNK), lambda i: (nb - 1 - i, 0, 0, 0)),
                  pl.BlockSpec((1, _SINK_ROWS, LANES), lambda i: (nb - 1 - i, 0, 0)),
                  pl.BlockSpec((CHUNK, D_MODEL), cur), pl.BlockSpec((CHUNK, D_MODEL), cur)],
        out_specs=[pl.BlockSpec((CHUNK, QKV_DIM), cur), sk],
        out_shape=[jax.ShapeDtypeStruct((t, QKV_DIM), F32), jax.ShapeDtypeStruct((ATTN_KV, 2, _SINK_ROWS, 1), F32)],
        scratch_shapes=[pltpu.VMEM((CHUNK, 2 * LANES), F32)],
        semantics=("arbitrary",), moves=moves,
    )(qkv, qkv, qkv, probs, stats, attn, d_o)
    return (res, landed) if moves else res


def _adamw(parts, w, m, v, *, name, tb=512, moves=()):
    layers, r, c = w.shape
    n = parts[0].shape[0]
    tb = min(tb, r)
    assert r % tb == 0 and len(parts) == layers, (name, r, tb)
    nb = r // tb

    def body(*refs):
        p_refs = refs[:layers]
        w_ref, m_ref, v_ref, g_ref, d_ref, nm_ref, nv_ref = refs[layers:]
        for layer in range(layers):
            @pl.when(pl.program_id(0) == layer)
            def _(p_ref=p_refs[layer]):
                g = p_ref[0].astype(F32)
                for s in range(1, n):
                    g = g + p_ref[s].astype(F32)
                m_new = ADAM_B1 * m_ref[...] + (1.0 - ADAM_B1) * g
                v_new = ADAM_B2 * v_ref[...] + (1.0 - ADAM_B2) * jnp.square(g)
                m_hat = m_new / (1.0 - ADAM_B1 ** ADAM_STEP)
                v_hat = v_new / (1.0 - ADAM_B2 ** ADAM_STEP)
                g_ref[...] = g
                d_ref[...] = -ADAM_LR * (m_hat / (jnp.sqrt(v_hat) + ADAM_EPS) + ADAM_WD * w_ref[...])
                nm_ref[...] = m_new
                nv_ref[...] = v_new

    part_spec = lambda layer: pl.BlockSpec(
        (n, tb, c), lambda l, i: (0, jnp.clip(i + (l - layer) * nb, 0, nb - 1), 0))
    blk = pl.BlockSpec((None, tb, c), lambda l, i: (l, i, 0))
    res, landed = _pcall(
        body, name=name, grid=(layers, nb),
        in_specs=[part_spec(layer) for layer in range(layers)] + [blk, blk, blk],
        out_specs=[blk] * 4,
        out_shape=[jax.ShapeDtypeStruct((layers, r, c), F32)] * 4,
        semantics=("arbitrary", "arbitrary"), moves=moves,
    )(*parts, w, m, v)
    return (res, landed) if moves else res


def _as_rows(a):
    flat = a.reshape(-1)
    pad = (-flat.shape[0]) % PACK_W
    if pad:
        flat = jnp.pad(flat, (0, pad))
    return flat.reshape(-1, PACK_W)


def _cols_from_shards(g):
    return jnp.transpose(g, (1, 0, 2)).reshape(g.shape[1], -1)


def _shard_cols(shards, lo, hi):
    c = shards.shape[2]
    pieces = []
    for j in range(shards.shape[0]):
        a, b = max(lo, j * c), min(hi, (j + 1) * c)
        if a < b:
            pieces.append(shards[j, :, a - j * c:b - j * c])
    return pieces


def _cols_of(sources, lo, hi):
    pieces = []
    for arr, col0, first, last in sources:
        a, b = max(lo, first), min(hi, last)
        if a < b:
            pieces.append(arr[:, col0 + a - first:col0 + b - first])
    return pieces


def _pad_lanes(a):
    return jnp.pad(a, ((0, 0), (0, LANES - a.shape[1])))


def kernel(x, norm_mix_g, norm_mlp_g, final_norm_g, w_in_even, w_out_even, gm_ln_g, gm_ln_b, gm_w_s, gm_b_s, ssm_conv_w, ssm_conv_b, ssm_dt_bias, ssm_a_log, ssm_d, ssm_norm_g, w_qkv, b_qkv, w_o, b_o, attn_sinks, w_up, w_down, loss_target, m_norm_mix_g, m_norm_mlp_g, m_final_norm_g, m_w_in_even, m_w_out_even, m_gm_ln_g, m_gm_ln_b, m_gm_w_s, m_gm_b_s, m_ssm_conv_w, m_ssm_conv_b, m_ssm_dt_bias, m_ssm_a_log, m_ssm_d, m_ssm_norm_g, m_w_qkv, m_b_qkv, m_w_o, m_b_o, m_attn_sinks, m_w_up, m_w_down, v_norm_mix_g, v_norm_mlp_g, v_final_norm_g, v_w_in_even, v_w_out_even, v_gm_ln_g, v_gm_ln_b, v_gm_w_s, v_gm_b_s, v_ssm_conv_w, v_ssm_conv_b, v_ssm_dt_bias, v_ssm_a_log, v_ssm_d, v_ssm_norm_g, v_w_qkv, v_b_qkv, v_w_o, v_b_o, v_attn_sinks, v_w_up, v_w_down):
    names = ["norm_mix_g", "norm_mlp_g", "final_norm_g", "w_in_even", "w_out_even", "gm_ln_g", "gm_ln_b", "gm_w_s",
             "gm_b_s", "ssm_conv_w", "ssm_conv_b", "ssm_dt_bias", "ssm_a_log", "ssm_d", "ssm_norm_g", "w_qkv",
             "b_qkv", "w_o", "b_o", "attn_sinks", "w_up", "w_down"]
    env = locals()
    W = {n: env[n] for n in names}
    M = {n: env["m_" + n] for n in names}
    V = {n: env["v_" + n] for n in names}
    big = ["w_in_even", "w_out_even", "w_qkv", "w_o", "w_up", "w_down"]
    small_sharded = ["ssm_conv_w", "b_qkv", "b_o"]
    replicated = [n for n in names if n not in big and n not in small_sharded]
    me = 4 * lax.axis_index("x") + 2 * lax.axis_index("y") + lax.axis_index("c")
    t = x.shape[1]
    xs = x.reshape(t, D_MODEL)
    target = loss_target.reshape(t, D_MODEL)
    gather = lambda a: _Move("gather", a)
    over_ici = lambda a: _Move("gather_ici", a)
    over_d2d = lambda a: _Move("gather_d2d", a)
    by_core = lambda a: a.reshape((N_CHIP, N_CORE) + a.shape[1:])
    to_sibling = lambda a: [_Move("scatter_d2d", by_core(a))]
    my_core = lax.axis_index("c").astype(jnp.int32).reshape(1)
    pair = lambda a, theirs, name: _pair_add(by_core(a), theirs, my_core, name=name)
    to_chips = lambda a: _Move("scatter_ici", a)
    whole = lambda a: a.reshape((N_DEV,) + a.shape[2:])
    row = lambda a: a.reshape(1, D_MODEL)

    small_flat = jnp.concatenate([W[n].reshape(-1) for n in small_sharded])
    w_in_g, small_g = _exchange([over_ici(w_in_even[0].astype(BF16)), gather(_as_rows(small_flat))],
                                name="gather_w_in", then_d2d=[0])
    w_in_s = whole(w_in_g)
    z_lo, xbc_lo, dt_lo = 2 * D_MODEL, 3 * D_MODEL, 3 * D_MODEL + CONV_DIM
    w_uv = jnp.concatenate(_shard_cols(w_in_s, 0, z_lo), axis=1)
    w_rest = jnp.concatenate(_shard_cols(w_in_s, xbc_lo, dt_lo) + _shard_cols(w_in_s, z_lo, xbc_lo)
                             + _shard_cols(w_in_s, dt_lo, IN_EVEN)
                             + [jnp.zeros((D_MODEL, LANES - SSM_HEADS), BF16)], axis=1)
    small_all = small_g.reshape(N_DEV, -1)
    n_cw = SSM_CONV * CONV_DIM // N_DEV
    n_bq = QKV_DIM // N_DEV
    conv_w = _cols_from_shards(small_all[:, :n_cw].reshape(N_DEV, SSM_CONV, CONV_DIM // N_DEV))
    bqkv = small_all[:, n_cw:n_cw + n_bq].reshape(1, QKV_DIM)
    bo = small_all[:, n_cw + n_bq:n_cw + n_bq + D_MODEL // N_DEV].reshape(1, D_MODEL)

    conv_b = ssm_conv_b.reshape(1, CONV_DIM)
    dt_bias, a_log, d_skip = _pad_lanes(ssm_dt_bias), _pad_lanes(ssm_a_log), _pad_lanes(ssm_d)
    gm_w = gm_w_s[0]
    gm_b = gm_b_s[0].reshape(GM_GROUPS, CHUNK, 1)
    sink_rows = jnp.repeat(jnp.transpose(attn_sinks.reshape(ATTN_KV, _PAIRS_PER_KV, 2), (0, 2, 1)), CHUNK,
                           axis=2).reshape(ATTN_KV, 2, _SINK_ROWS, 1)
    w_up_b, w_down_b = w_up.astype(BF16), w_down.astype(BF16)

    w_down0_a, w_down0_b = w_down_b[0, :FF_SHARD // 2], w_down_b[0, FF_SHARD // 2:]
    (y0, proj_uv), (w_qkv_g,) = _norm_matmul(xs, row(norm_mix_g[0]), w_uv, name="proj_uv", emit_y=True,
                                             moves=[over_ici(jnp.transpose(w_qkv[0]).astype(BF16))])
    proj_rest, (w_out_g,) = _norm_matmul(xs, row(norm_mix_g[0]), w_rest, name="proj_rest", emit_y=False,
                                         moves=[over_ici(w_out_even[0].astype(BF16))])
    mix, (w_o_g, w_down0_a, w_out_g, w_qkv_g) = _gmlp_fwd(
        proj_uv, gm_ln_g, gm_ln_b, gm_w, gm_b, name="gmlp_fwd",
        moves=[over_ici(w_o[0].astype(BF16)), over_ici(w_down0_a), over_d2d(w_out_g), over_d2d(w_qkv_g)])
    (mix, h_states, conv_pre), (w_up0_g, w_o_g, w_down0_a) = _ssd_fwd(
        proj_rest, mix, conv_w, conv_b, dt_bias, a_log, d_skip, ssm_norm_g, name="ssd_fwd",
        moves=[over_ici(w_up_b[0]), over_d2d(w_o_g), over_d2d(w_down0_a)])
    w_out_f = whole(w_out_g).reshape(2 * D_MODEL, D_MODEL)
    (h1, y1), (w_down0_b, w_up0_g) = _residual_matmul(
        mix, w_out_f, xs, name="mix_out", norm_g=row(norm_mlp_g[0]),
        moves=[over_ici(w_down0_b), over_d2d(w_up0_g)])
    up0, (w_down0_b,) = _mlp_up(y1, whole(w_up0_g), name="mlp_up0", moves=[over_d2d(w_down0_b)])
    w_down_g = [[whole(w_down0_a), whole(w_down0_b)]]
    h2, y2 = _mlp_down(up0, w_down_g[0], h1, name="mlp_down0", norm_g=row(norm_mix_g[1]))
    wqkv = whole(w_qkv_g).reshape(QKV_DIM, D_MODEL)
    wo = whole(w_o_g).reshape(D_MODEL, D_MODEL)
    qkv = _residual_matmul(y2, wqkv, None, name="qkv", bias=bqkv, w_transposed=True)
    (attn, attn_p, attn_stats), (w_up1_g, w_down1_g) = _attn_fwd(
        qkv, sink_rows, name="attn_fwd", moves=[over_ici(w_up_b[1]), over_ici(w_down_b[1])])
    (h3, y3), (w_up1_g,) = _residual_matmul(attn, wo, h2, name="attn_out", bias=bo, norm_g=row(norm_mlp_g[1]),
                                            moves=[over_d2d(w_up1_g)])
    w_up_g = [whole(w_up0_g), whole(w_up1_g)]
    up1, (w_down1_g,) = _mlp_up(y3, w_up_g[1], name="mlp_up1", moves=[over_d2d(w_down1_g)])
    w_down_g.append([whole(w_down1_g)])
    loss_part, dh4, dh4_b, d_final_g = _mlp_down_loss(up1, w_down_g[1], h3, row(final_norm_g), target,
                                                      name="mlp_down1_loss")

    by_dev_rows = lambda a: a.reshape((N_DEV, a.shape[0] // N_DEV) + a.shape[1:])

    def mlp_bwd(dh, dh_b, h, y, up, layer, first_moves=()):
        res = _mlp_down_dx(dh_b, w_down_g[layer], up, name=f"mlp_down_dx{layer}", moves=first_moves)
        d_up, first_landed = res if first_moves else (res, [])
        g_down = _dw_by_rows(up, dh_b, name=f"mlp_down_dw{layer}", tk=FF_SHARD, square_relu=True)
        g_down = by_dev_rows(g_down)
        g_up, (theirs,) = _dw_by_cols(y, d_up, name=f"mlp_up_dw{layer}", tn=FF_SHARD, by_device=True,
                                      moves=to_sibling(g_down))
        q_down = pair(g_down, theirs, f"mlp_down_pair{layer}")
        (dh_new, dh_new_b, dg, cs), (r_down, theirs) = _dx_norm(
            d_up, w_up_g[layer], h, row(norm_mlp_g[layer]), dh, name=f"mlp_up_dx{layer}", by_device_cols=True,
            moves=[to_chips(q_down)] + to_sibling(g_up))
        q_up = pair(g_up, theirs, f"mlp_up_pair{layer}")
        return dh_new, dh_new_b, cs, dg, q_up, r_down, first_landed

    dh3, dh3_b, cs3, g_nmlp1, q_up1, r_down1, _ = mlp_bwd(dh4, dh4_b, h3, y3, up1, 1)
    g_bo = cs3
    g_wo = by_dev_rows(_dw_by_cols(attn, dh3_b, name="attn_out_dw", tn=FF_SHARD))
    d_attn, (theirs,) = _dx(dh3_b, wo, name="attn_out_dx", moves=to_sibling(g_wo))
    q_wo = pair(g_wo, theirs, "attn_out_pair")
    (dqkv, d_sink), (r_up1, r_wo) = _attn_bwd(qkv, attn_p, attn_stats, attn, d_attn, name="attn_bwd",
                                              moves=[to_chips(q_up1), to_chips(q_wo)])
    g_bqkv = _colsum(dqkv, name="qkv_db")
    g_wqkv = by_dev_rows(_dw_by_rows(dqkv, y2, name="qkv_dw", tk=QKV_DIM // 2))
    (dh2, dh2_b, g_nmix1, _), (theirs,) = _dx_norm(dqkv, wqkv, h2, row(norm_mix_g[1]), dh3, name="qkv_dx",
                                                   w_transposed=True, moves=to_sibling(g_wqkv))
    q_wqkv = pair(g_wqkv, theirs, "qkv_pair")
    dh1, dh1_b, _, g_nmlp0, q_up0, r_down0, (r_wqkv,) = mlp_bwd(dh2, dh2_b, h1, y1, up0, 0,
                                                                first_moves=[to_chips(q_wqkv)])

    d_mix = _dx(dh1_b, w_out_f, name="mix_out_dx")
    g_wout = by_dev_rows(_dw_by_rows(mix, dh1_b, name="mix_out_dw", tk=FF_SHARD))
    (d_uv, g_ln_g, g_ln_b, g_gm_w, g_gm_b), (r_up0, theirs) = _gmlp_bwd(
        proj_uv, d_mix, gm_ln_g, gm_ln_b, gm_w, gm_b, name="gmlp_bwd", moves=[to_chips(q_up0)] + to_sibling(g_wout))
    q_wout = pair(g_wout, theirs, "mix_out_pair")

    early = [("norm_mlp_g", None), ("final_norm_g", None), ("norm_mix_g", 1), ("gm_ln_g", None), ("gm_ln_b", None),
             ("gm_w_s", None), ("gm_b_s", None), ("attn_sinks", None)]
    late = [("norm_mix_g", 0), ("ssm_conv_b", None), ("ssm_dt_bias", None), ("ssm_a_log", None), ("ssm_d", None),
            ("ssm_norm_g", None)]
    early_sharded, late_sharded = ["b_qkv", "b_o"], ["ssm_conv_w"]
    small_grads = {
        ("norm_mlp_g", None): jnp.concatenate([g_nmlp0, g_nmlp1], axis=0),
        ("final_norm_g", None): d_final_g, ("norm_mix_g", 1): g_nmix1,
        ("gm_ln_g", None): g_ln_g, ("gm_ln_b", None): g_ln_b, ("gm_w_s", None): g_gm_w, ("gm_b_s", None): g_gm_b,
        ("attn_sinks", None): jnp.transpose(
            jnp.sum(d_sink.reshape(ATTN_KV, 2, _PAIRS_PER_KV, CHUNK), axis=3), (0, 2, 1)),
        "b_qkv": g_bqkv, "b_o": g_bo,
    }
    pack = lambda keys: _as_rows(jnp.concatenate([small_grads[key].reshape(-1) for key in keys]))
    (dpre, dz, ddt, g_dtb, g_alog, g_dskip, g_ssm_ng), (r_wout, early_recv) = _ssd_bwd(
        proj_rest, conv_pre, h_states, d_mix, dt_bias, a_log, d_skip, ssm_norm_g, name="ssd_bwd",
        moves=[to_chips(q_wout), gather(pack(early + early_sharded))])
    d_rest, g_conv_w, g_conv_b = _conv_bwd(proj_rest, dpre, dz, ddt, conv_w, name="conv_bwd")
    g_w_uv = _dw_by_cols(y0, d_uv, name="proj_uv_dw", tn=FF_SHARD)
    g_w_rest = _dw_by_cols(y0, d_rest, name="proj_rest_dw", tn=REST_W // 5)
    in_cols = [(g_w_uv, 0, 0, z_lo), (g_w_rest, CONV_DIM, z_lo, xbc_lo), (g_w_rest, 0, xbc_lo, dt_lo),
               (g_w_rest, CONV_DIM + D_MODEL, dt_lo, IN_EVEN)]
    in_shard = IN_EVEN // N_DEV
    g_w_in = jnp.stack([jnp.concatenate(_cols_of(in_cols, j * in_shard, (j + 1) * in_shard), axis=1)
                        for j in range(N_DEV)])
    dy0, (theirs,) = _dx(d_uv, w_uv, name="proj_uv_dx", moves=to_sibling(g_w_in))
    q_w_in = pair(g_w_in, theirs, "proj_pair")
    quarter = D_MODEL // 4
    in_part = lambda k: _Move("scatter_ici", q_w_in, rows=(k * quarter, quarter))
    (dx, _, g_nmix0, _), r_w_in = _dx_norm(d_rest, w_rest, xs, row(norm_mix_g[0]), dh1, name="proj_rest_dx",
                                           partial=dy0, moves=[in_part(0), in_part(1)])
    small_grads.update({
        ("loss", None): loss_part[:1, :1],
        ("norm_mix_g", 0): g_nmix0, ("ssm_conv_b", None): g_conv_b,
        ("ssm_dt_bias", None): g_dtb[:, :SSM_HEADS], ("ssm_a_log", None): g_alog[:, :SSM_HEADS],
        ("ssm_d", None): g_dskip[:, :SSM_HEADS], ("ssm_norm_g", None): g_ssm_ng, "ssm_conv_w": g_conv_w,
    })


    def update(n, parts, moves=(), transposed=False):
        shape = W[n].shape
        if transposed:
            as3 = lambda a: jnp.transpose(a[0])[None]
            back = lambda a: jnp.transpose(a[0])[None]
        else:
            as3 = lambda a: a.reshape((len(parts),) + parts[0].shape[1:])
            back = lambda a: a.reshape(shape)
        res = _adamw(parts, as3(W[n]), as3(M[n]), as3(V[n]), name="adamw_" + n, moves=moves)
        res, landed = res if moves else (res, [])
        return [back(a) for a in res], landed

    out = {}
    late_keys = late + late_sharded + [("loss", None)]
    out["w_o"], (late_recv,) = update("w_o", [r_wo], moves=[gather(pack(late_keys))])
    out["w_down"], _ = update("w_down", [r_down0, r_down1])
    out["w_up"], _ = update("w_up", [r_up0, r_up1])
    out["w_out_even"], _ = update("w_out_even", [r_wout])
    out["w_qkv"], r_w_in_late = update("w_qkv", [r_wqkv], moves=[in_part(2), in_part(3)], transposed=True)
    out["w_in_even"], _ = update("w_in_even", list(r_w_in) + list(r_w_in_late))

    def unpacked(recv, keys):
        flat, res, o = recv.reshape(N_DEV, -1), {}, 0
        for key in keys:
            res[key] = flat[:, o:o + small_grads[key].size]
            o += small_grads[key].size
        return res

    arrived = {**unpacked(early_recv, early + early_sharded), **unpacked(late_recv, late_keys)}
    piece = lambda tree, key: tree[key[0]] if key[1] is None else tree[key[0]][key[1]]

    def rows_by_device(cat):
        pad = (-cat.shape[1]) % PACK_W
        return jnp.pad(cat, ((0, 0), (0, pad))).reshape(N_DEV, -1, PACK_W)

    rep_keys = early + late
    rep_parts = rows_by_device(jnp.concatenate([arrived[key] for key in rep_keys], axis=1))
    flat_rep = lambda tree: _as_rows(jnp.concatenate([piece(tree, key).reshape(-1) for key in rep_keys]))[None]
    rep_res = _adamw([rep_parts], flat_rep(W), flat_rep(M), flat_rep(V), name="adamw_replicated")
    sh_keys = early_sharded + late_sharded
    shard_parts = []
    for n in sh_keys:
        full = arrived[n].reshape((N_DEV,) + small_grads[n].shape)
        c = full.shape[-1] // N_DEV
        shard_parts.append(lax.dynamic_slice_in_dim(full, me * c, c, axis=full.ndim - 1).reshape(N_DEV, -1))
    sh_rows = rows_by_device(jnp.concatenate(shard_parts, axis=1))
    flat_sh = lambda tree: _as_rows(jnp.concatenate([tree[n].reshape(-1) for n in sh_keys]))[None]
    sh_res = _adamw([sh_rows], flat_sh(W), flat_sh(M), flat_sh(V), name="adamw_small_sharded")

    def unpack_replicated(rows):
        flat, vals, o = rows.reshape(-1), {}, 0
        for key in rep_keys:
            size = piece(W, key).size
            vals[key] = flat[o:o + size]
            o += size
        res = {}
        for n in replicated:
            if (n, None) in vals:
                res[n] = vals[(n, None)].reshape(W[n].shape)
            else:
                res[n] = jnp.stack([vals[(n, r)] for r in range(W[n].shape[0])]).reshape(W[n].shape)
        return res

    def unpack_sharded(rows):
        flat, res, o = rows.reshape(-1), {}, 0
        for n in sh_keys:
            res[n] = flat[o:o + W[n].size].reshape(W[n].shape)
            o += W[n].size
        return res

    results = []
    for idx in range(4):
        d = {n: out[n][idx] for n in big}
        d.update(unpack_replicated(rep_res[idx]))
        d.update(unpack_sharded(sh_res[idx]))
        results.append(d)

    loss = jnp.sum(arrived[("loss", None)])
    grad_x = dx.reshape(x.shape)
    final = [loss, grad_x]
    for d in results:
        final.extend(d[n] for n in names)
    return tuple(final)
```

```python
import dataclasses
import functools

import jax
import jax.numpy as jnp
from jax import lax
from jax.experimental import pallas as pl
from jax.experimental.pallas import tpu as pltpu

F32 = jnp.float32
BF16 = jnp.bfloat16

N_DEV = 8
D_MODEL = 1024
D_FF = 4096
RMS_EPS = 1e-5
LN_EPS = 1e-5
CHUNK = 128
GM_GROUPS = 8
SSM_HEADS = 16
SSM_HEADDIM = 64
SSM_GROUPS = 4
SSM_STATE = 128
SSM_CONV = 4
CONV_DIM = 2048
IN_EVEN = 5136
REST_W = 3200
ATTN_HEADS = 16
ATTN_KV = 2
HEAD_DIM = 64
QKV_DIM = 1280
LANES = 128
HALO = 8
PACK_W = 1024

ADAM_LR = 0.001
ADAM_B1 = 0.9
ADAM_B2 = 0.999
ADAM_EPS = 1e-08
ADAM_WD = 0.01
ADAM_STEP = 10

VMEM_LIMIT_BYTES = 56 * 1024 * 1024


_NN = (((1,), (0,)), ((), ()))
_NT = (((1,), (1,)), ((), ()))
_TN = (((0,), (0,)), ((), ()))


def _dg(a, b, dims):
    return lax.dot_general(a.astype(BF16), b.astype(BF16), dims, preferred_element_type=F32)


@jax.custom_vjp
def _nn(a, b):
    return _dg(a, b, _NN)


@jax.custom_vjp
def _nt(a, b):
    return _dg(a, b, _NT)


@jax.custom_vjp
def _tn(a, b):
    return _dg(a, b, _TN)


_nn.defvjp(lambda a, b: (_dg(a, b, _NN), (a, b)), lambda r, g: (_nt(g, r[1]), _tn(r[0], g)))
_nt.defvjp(lambda a, b: (_dg(a, b, _NT), (a, b)), lambda r, g: (_nn(g, r[1]), _tn(g, r[0])))
_tn.defvjp(lambda a, b: (_dg(a, b, _TN), (a, b)), lambda r, g: (_nt(r[1], g), _nn(r[0], g)))


def _split3_dot(tri, x):
    x1 = x.astype(BF16)
    r1 = x - x1.astype(F32)
    x2 = r1.astype(BF16)
    x3 = (r1 - x2.astype(F32)).astype(BF16)
    t = tri.astype(BF16)
    dot = lambda p: lax.dot_general(t, p, _NN, preferred_element_type=F32)
    return dot(x1) + dot(x2) + dot(x3)


def _tri(lower):
    r = lax.broadcasted_iota(jnp.int32, (CHUNK, CHUNK), 0)
    c = lax.broadcasted_iota(jnp.int32, (CHUNK, CHUNK), 1)
    return jnp.where((r >= c) if lower else (r <= c), 1.0, 0.0).astype(F32)


@jax.custom_vjp
def _cumsum_rows(x):
    return _split3_dot(_tri(True), x)


_cumsum_rows.defvjp(lambda x: (_split3_dot(_tri(True), x), None), lambda _, g: (_split3_dot(_tri(False), g),))


def _sigmoid(x):
    return 1.0 / (1.0 + jnp.exp(-x))


def _silu(x):
    return x * _sigmoid(x)


def _softplus(x):
    return jnp.maximum(x, 0.0) + jnp.log(1.0 + jnp.exp(-jnp.abs(x)))


def _gelu_tanh(x):
    return 0.5 * x * (1.0 + jnp.tanh(0.7978845608028654 * (x + 0.044715 * (x * x * x))))


def _rmsnorm(x, g):
    return x * lax.rsqrt(jnp.mean(x * x, axis=-1, keepdims=True) + RMS_EPS) * g


def _gmlp_chunk(u, v, ln_g, ln_b, w_s, b_s):
    gu = _gelu_tanh(u)
    gv = _gelu_tanh(v)
    mu = jnp.mean(gv, axis=-1, keepdims=True)
    var = jnp.mean(jnp.square(gv - mu), axis=-1, keepdims=True)
    vn = (gv - mu) * lax.rsqrt(var + LN_EPS) * ln_g + ln_b
    r = lax.broadcasted_iota(jnp.int32, (CHUNK, CHUNK), 0)
    c = lax.broadcasted_iota(jnp.int32, (CHUNK, CHUNK), 1)
    causal = r >= c
    outs = []
    for g in range(GM_GROUPS):
        cols = slice(g * LANES, (g + 1) * LANES)
        mixed = _nn(jnp.where(causal, w_s[g], 0.0), vn[:, cols]) + b_s[g]
        outs.append(gu[:, cols] * mixed)
    return jnp.concatenate(outs, axis=1)


def _lane_pick(row, h):
    lane = lax.broadcasted_iota(jnp.int32, row.shape, 1)
    return jnp.sum(jnp.where(lane == h, row, 0.0), axis=1, keepdims=True)


def _col_pick(m, h):
    lane = lax.broadcasted_iota(jnp.int32, m.shape, 1)
    return jnp.sum(jnp.where(lane == h, m, 0.0), axis=1, keepdims=True)


def _row_pick(m, h):
    sub = lax.broadcasted_iota(jnp.int32, m.shape, 0)
    return jnp.sum(jnp.where(sub == h, m, 0.0), axis=0, keepdims=True)


_PAIRS = SSM_HEADS // 2


def _ssd_chunk(pre, z, dt_raw, h_prev, dt_bias, a_log, d_skip, norm_g):
    xbc = _silu(pre)
    dt = _softplus(dt_raw + dt_bias)
    da = dt * (-jnp.exp(a_log))
    a_cum = _cumsum_rows(da)
    a_cum_t = a_cum.T
    dt_t = dt.T
    r = lax.broadcasted_iota(jnp.int32, (CHUNK, CHUNK), 0)
    c = lax.broadcasted_iota(jnp.int32, (CHUNK, CHUNK), 1)
    causal = r >= c
    lane_lo = lax.broadcasted_iota(jnp.int32, (1, LANES), 1) < SSM_HEADDIM
    last_row = lax.broadcasted_iota(jnp.int32, (CHUNK, 1), 0) == CHUNK - 1
    ys, h_next = [], []
    for j in range(_PAIRS):
        g = j // 2
        xs = xbc[:, j * LANES:(j + 1) * LANES]
        bm = xbc[:, 1024 + g * SSM_STATE:1024 + (g + 1) * SSM_STATE]
        cm = xbc[:, 1536 + g * SSM_STATE:1536 + (g + 1) * SSM_STATE]
        cb = _nt(cm, bm)
        y_diag, to_end, e_cum, c_dec, d_row = [], [], [], [], []
        for h in (2 * j, 2 * j + 1):
            col = _col_pick(a_cum, h)
            row = _row_pick(a_cum_t, h)
            dt_col = _col_pick(dt, h)
            dt_row = _row_pick(dt_t, h)
            decay = jnp.exp(jnp.where(causal, col - row, -jnp.inf))
            y_diag.append(_nn(cb * decay * dt_row, xs))
            last = jnp.sum(jnp.where(last_row, col, 0.0), axis=0, keepdims=True)
            to_end.append(jnp.exp(last - col) * dt_col)
            e_cum.append(jnp.exp(col))
            c_dec.append(jnp.exp(last))
            d_row.append(_lane_pick(d_skip, h))
        pair = lambda lo_hi: jnp.where(lane_lo, lo_hi[0], lo_hi[1])
        states = _tn(bm, xs * pair(to_end))
        y_off = _nn(cm, h_prev[j]) * pair(e_cum)
        ys.append(pair(y_diag) + y_off + xs * pair(d_row))
        h_next.append(pair(c_dec) * h_prev[j] + states)
    y = jnp.concatenate(ys, axis=1) * _silu(z)
    width = D_MODEL // SSM_GROUPS
    y = jnp.concatenate(
        [_rmsnorm(y[:, g * width:(g + 1) * width], norm_g[:, g * width:(g + 1) * width]) for g in range(SSM_GROUPS)],
        axis=1)
    return y, tuple(h_next)


def _shift_down(prev8, x, k):
    if k == 0:
        return x
    win = jnp.concatenate([prev8, x], axis=0)
    return pltpu.roll(win, k, 0)[HALO:]


def _shift_up(x, next8, k):
    if k == 0:
        return x
    n = x.shape[0]
    win = jnp.concatenate([x, next8], axis=0)
    return pltpu.roll(win, n + HALO - k, 0)[:n]


def _conv_pre(prev8, x, w, b):
    out = b + x * w[SSM_CONV - 1:SSM_CONV]
    for i in range(SSM_CONV - 1):
        out = out + _shift_down(prev8, x, SSM_CONV - 1 - i) * w[i:i + 1]
    return out


def _swap_halves(x):
    return pltpu.roll(x, HEAD_DIM, 1)


_PAIRS_PER_KV = ATTN_HEADS // ATTN_KV // 2
_ATTN_SCALE = HEAD_DIM ** -0.5


def _parity_lanes(parity):
    lane = lax.broadcasted_iota(jnp.int32, (1, LANES), 1)
    return (lane >= HEAD_DIM * parity) & (lane < HEAD_DIM * (parity + 1))


def _kv_placed(pair, kv_head):
    mine = jnp.where(_parity_lanes(kv_head), pair, 0.0)
    lo = mine if kv_head == 0 else _swap_halves(mine)
    return lo, _swap_halves(lo)


def _kv_unplaced(d_lo, d_hi, kv_head):
    d = jnp.where(_parity_lanes(0), d_lo, 0.0) + _swap_halves(jnp.where(_parity_lanes(1), d_hi, 0.0))
    return d if kv_head == 0 else _swap_halves(d)


def _attn_probs(q4, k_e, sink, first):
    s = _dg(q4, k_e, _NT) * _ATTN_SCALE
    rows = lax.broadcasted_iota(jnp.int32, s.shape, 0) & (CHUNK - 1)
    cols = lax.broadcasted_iota(jnp.int32, s.shape, 1)
    valid = (cols <= rows + CHUNK) & (cols > rows) & (cols >= CHUNK * first.astype(jnp.int32))
    s = jnp.where(valid, s, -jnp.inf)
    m = jnp.maximum(jnp.max(s, axis=-1, keepdims=True), sink)
    p = jnp.exp(s - m)
    e_sink = jnp.exp(sink - m)
    return p, e_sink, jnp.sum(p, axis=-1, keepdims=True) + e_sink


def _lane_column(col, idx):
    lane = lax.broadcasted_iota(jnp.int32, (1, LANES), 1)
    return jnp.where(lane == idx, col, 0.0)


N_CHIP = 4
N_CORE = 2
_OTHER_CHIPS = (2, 4, 6)


@dataclasses.dataclass
class _Move:
    kind: str
    src: jax.Array

    def dst_shape(self):
        s = self.src.shape
        shape = {"gather": (N_DEV,) + s, "gather_ici": (N_CHIP, N_CORE) + s, "gather_d2d": s,
                 "scatter_d2d": (N_CHIP,) + s[2:], "scatter_ici": s}[self.kind]
        return jax.ShapeDtypeStruct(tuple(shape), self.src.dtype)


def _peer(x, y, c, k):
    return (1 - x if k & 4 else x, 1 - y if k & 2 else y, 1 - c if k & 1 else c)


def _move_copies(moves, srcs, dsts, send_sems, recv_sems, local_sems):
    x, y, c = lax.axis_index("x"), lax.axis_index("y"), lax.axis_index("c")
    chip = 2 * x + y
    me = 2 * chip + c
    sibling = (x, y, 1 - c)
    all_chips = pl.ds(0, N_CHIP)
    local, remote = [], []

    def push(n, k, src, dst, device):
        remote.append(pltpu.make_async_remote_copy(
            src_ref=src, dst_ref=dst, send_sem=send_sems.at[n, k], recv_sem=recv_sems.at[n, k],
            device_id=device, device_id_type=pl.DeviceIdType.MESH))

    for n, mv in enumerate(moves):
        s, d = srcs[n], dsts[n]
        if mv.kind == "gather":
            local.append(pltpu.make_async_copy(s, d.at[me], local_sems.at[n]))
            for k in range(1, N_DEV):
                push(n, k - 1, s, d.at[me], _peer(x, y, c, k))
        elif mv.kind == "gather_ici":
            local.append(pltpu.make_async_copy(s, d.at[chip, c], local_sems.at[n]))
            for k in _OTHER_CHIPS:
                push(n, k - 1, s, d.at[chip, c], _peer(x, y, c, k))
        elif mv.kind == "gather_d2d":
            push(n, 0, d.at[all_chips, c], d.at[all_chips, c], sibling)
        elif mv.kind == "scatter_d2d":
            push(n, 0, s.at[all_chips, 1 - c], d, sibling)
        else:
            assert mv.kind == "scatter_ici", mv.kind
            local.append(pltpu.make_async_copy(s.at[chip], d.at[chip], local_sems.at[n]))
            for k in _OTHER_CHIPS:
                px, py, _ = _peer(x, y, c, k)
                push(n, k - 1, s.at[2 * px + py], d.at[chip], (px, py, c))
    return local, remote


def _move_aliases(moves, n_in, n_out):
    return {n_in + n: n_out + n for n, mv in enumerate(moves) if mv.kind == "gather_d2d"}


def _pcall(body, *, name, grid, in_specs, out_specs, out_shape, scratch_shapes=(), semantics=(), moves=(),
           aliases=None):
    out_shape, out_specs = list(out_shape), list(out_specs)
    in_specs = list(in_specs)
    if not moves:
        call = pl.pallas_call(
            body, name=name, grid=grid, in_specs=in_specs, out_specs=out_specs, out_shape=out_shape,
            scratch_shapes=list(scratch_shapes), input_output_aliases=aliases or {},
            compiler_params=pltpu.CompilerParams(dimension_semantics=tuple(semantics),
                                                 vmem_limit_bytes=VMEM_LIMIT_BYTES))
        return (lambda *args: (list(call(*args)), []))
    n_in, n_out, n_scr, n_mv = len(in_specs), len(out_shape), len(scratch_shapes), len(moves)
    hbm = pl.BlockSpec(memory_space=pltpu.HBM)

    def carrier(*refs):
        ins, rest = refs[:n_in], refs[n_in:]
        srcs, rest = rest[:n_mv], rest[n_mv:]
        outs, rest = rest[:n_out], rest[n_out:]
        dsts, rest = rest[:n_mv], rest[n_mv:]
        scr, (send_sems, recv_sems, local_sems) = rest[:n_scr], rest[n_scr:]
        first = functools.reduce(jnp.logical_and, [pl.program_id(d) == 0 for d in range(len(grid))])
        last = functools.reduce(jnp.logical_and, [pl.program_id(d) == grid[d] - 1 for d in range(len(grid))])

        @pl.when(first)
        def _():
            local, remote = _move_copies(moves, srcs, dsts, send_sems, recv_sems, local_sems)
            for cp in local + remote:
                cp.start()

        body(*ins, *outs, *scr)

        @pl.when(last)
        def _():
            local, remote = _move_copies(moves, srcs, dsts, send_sems, recv_sems, local_sems)
            for cp in remote + local:
                cp.wait()

    call = pl.pallas_call(
        carrier, name=name, grid=grid,
        in_specs=in_specs + [hbm] * n_mv,
        out_specs=out_specs + [hbm] * n_mv,
        out_shape=out_shape + [mv.dst_shape() for mv in moves],
        scratch_shapes=list(scratch_shapes) + [pltpu.SemaphoreType.DMA((n_mv, N_DEV - 1)),
                                               pltpu.SemaphoreType.DMA((n_mv, N_DEV - 1)),
                                               pltpu.SemaphoreType.DMA((n_mv,))],
        input_output_aliases={**(aliases or {}), **_move_aliases(moves, n_in, n_out)},
        compiler_params=pltpu.CompilerParams(dimension_semantics=("arbitrary",) * len(grid),
                                             vmem_limit_bytes=VMEM_LIMIT_BYTES))

    def run(*args):
        res = list(call(*args, *[mv.src for mv in moves]))
        return res[:n_out], res[n_out:]

    return run


def _exchange(moves, *, name, then_d2d=()):
    n_mv, n_fwd = len(moves), len(then_d2d)
    hbm = pl.BlockSpec(memory_space=pltpu.HBM)
    copies_of = {"gather": N_DEV - 1, "gather_ici": len(_OTHER_CHIPS), "gather_d2d": 1, "scatter_d2d": 1,
                 "scatter_ici": len(_OTHER_CHIPS)}
    first_copy = [sum(copies_of[mv.kind] for mv in moves[:n]) for n in range(n_mv)]

    def body(*refs):
        srcs, dsts, sems = refs[:n_mv], refs[n_mv:2 * n_mv], refs[2 * n_mv:]
        local, remote = _move_copies(moves, srcs, dsts, *sems[:3])
        for cp in local + remote:
            cp.start()
        x, y, c = lax.axis_index("x"), lax.axis_index("y"), lax.axis_index("c")
        chip, sibling = 2 * x + y, (x, y, 1 - c)
        passed, passed_on = [], set()

        def to_sibling(f, k, src, slot):
            cp = pltpu.make_async_remote_copy(src_ref=src, dst_ref=slot, send_sem=sems[3].at[f, k],
                                              recv_sem=sems[4].at[f, k], device_id=sibling,
                                              device_id_type=pl.DeviceIdType.MESH)
            cp.start()
            passed.append(cp)

        for f, n in enumerate(then_d2d):
            assert moves[n].kind == "gather_ici"
            d = dsts[n]
            to_sibling(f, 0, srcs[n], d.at[chip, c])
            for i, k in enumerate(_OTHER_CHIPS):
                remote[first_copy[n] + i].wait_recv()
                passed_on.add(first_copy[n] + i)
                px, py, _ = _peer(x, y, c, k)
                to_sibling(f, k - 1, d.at[2 * px + py, c], d.at[2 * px + py, c])
        for i, cp in enumerate(remote):
            if i in passed_on:
                cp.wait_send()
            else:
                cp.wait()
        for cp in local + passed:
            cp.wait()

    sems = [pltpu.SemaphoreType.DMA((n_mv, N_DEV - 1)), pltpu.SemaphoreType.DMA((n_mv, N_DEV - 1)),
            pltpu.SemaphoreType.DMA((n_mv,))]
    if then_d2d:
        sems += [pltpu.SemaphoreType.DMA((n_fwd, N_DEV - 1)), pltpu.SemaphoreType.DMA((n_fwd, N_DEV - 1))]
    return list(pl.pallas_call(
        body, name=name, in_specs=[hbm] * n_mv, out_specs=[hbm] * n_mv,
        out_shape=[mv.dst_shape() for mv in moves], scratch_shapes=sems,
    )(*[mv.src for mv in moves]))


TM = 512
FF_SHARD = D_FF // N_DEV


def _whole(a):
    nd = a.ndim
    return pl.BlockSpec(a.shape, lambda i: (0,) * nd)


def _rows(width, col=0):
    return pl.BlockSpec((TM, width), lambda i: (i, col))


def _acc_row(width):
    return pl.BlockSpec((1, width), lambda i: (0, 0))


def _unpack(res_landed, moves, n_out):
    res, landed = res_landed
    res = res[0] if n_out == 1 else res
    return (res, landed) if moves else res


def _norm_matmul(x, g, w, *, name, emit_y, moves=()):
    t, d = x.shape
    n = w.shape[1]

    def body(x_ref, g_ref, w_ref, *outs):
        y = _rmsnorm(x_ref[...], g_ref[...]).astype(BF16)
        if emit_y:
            outs[0][...] = y
        outs[-1][...] = lax.dot_general(y, w_ref[...], _NN, preferred_element_type=F32)

    shapes = ([jax.ShapeDtypeStruct((t, d), BF16)] if emit_y else []) + [jax.ShapeDtypeStruct((t, n), F32)]
    specs = ([_rows(d)] if emit_y else []) + [_rows(n)]
    return _unpack(_pcall(body, name=name, grid=(t // TM,), in_specs=[_rows(d), _acc_row(d), _whole(w)],
                          out_specs=specs, out_shape=shapes, semantics=("parallel",), moves=moves)(x, g, w),
                   moves, len(shapes))


def _residual_matmul(a, w, res, *, name, bias=None, norm_g=None, w_transposed=False, moves=()):
    t, k = a.shape
    n = w.shape[0 if w_transposed else 1]
    contract = _NT if w_transposed else _NN
    has_res, has_bias, has_norm = res is not None, bias is not None, norm_g is not None

    def body(a_ref, w_ref, *rest):
        rest = list(rest)
        res_ref = rest.pop(0) if has_res else None
        b_ref = rest.pop(0) if has_bias else None
        g_ref = rest.pop(0) if has_norm else None
        h = lax.dot_general(a_ref[...].astype(BF16), w_ref[...], contract, preferred_element_type=F32)
        if has_res:
            h = h + res_ref[...]
        if has_bias:
            h = h + b_ref[...]
        rest[0][...] = h
        if has_norm:
            rest[1][...] = _rmsnorm(h, g_ref[...]).astype(BF16)

    rows_in = [res] if has_res else []
    extra = ([bias] if has_bias else []) + ([norm_g] if has_norm else [])
    shapes = [jax.ShapeDtypeStruct((t, n), F32)] + ([jax.ShapeDtypeStruct((t, n), BF16)] if has_norm else [])
    return _unpack(_pcall(body, name=name, grid=(t // TM,),
                          in_specs=[_rows(k), _whole(w)] + [_rows(n)] * len(rows_in) + [_acc_row(n)] * len(extra),
                          out_specs=[_rows(n)] * len(shapes), out_shape=shapes, semantics=("parallel",),
                          moves=moves)(a, w, *rows_in, *extra), moves, len(shapes))


def _mlp_up(y, w_cols, *, name, moves=()):
    t, d = y.shape

    def body(y_ref, w_ref, up_ref):
        yv = y_ref[...]
        for j in range(N_DEV):
            up_ref[:, j * FF_SHARD:(j + 1) * FF_SHARD] = lax.dot_general(
                yv, w_ref[j], _NN, preferred_element_type=F32).astype(up_ref.dtype)

    return _unpack(_pcall(body, name=name, grid=(t // TM,), in_specs=[_rows(d), _whole(w_cols)],
                          out_specs=[_rows(D_FF)], out_shape=[jax.ShapeDtypeStruct((t, D_FF), BF16)],
                          semantics=("parallel",), moves=moves)(y, w_cols), moves, 1)


def _sq_relu(u):
    return jnp.square(jnp.maximum(u.astype(F32), 0.0))


def _down_blocks(w_refs):
    for j in range(N_DEV):
        off = j * FF_SHARD
        for w_ref in w_refs:
            yield off, w_ref.shape[1], w_ref[j]
            off += w_ref.shape[1]


def _mlp_down(up, w_rows, res, *, name, norm_g=None, moves=()):
    t = up.shape[0]
    has_norm = norm_g is not None
    n_w = len(w_rows)

    def body(up_ref, *rest):
        w_refs, res_ref, rest = rest[:n_w], rest[n_w], rest[n_w + 1:]
        h = res_ref[...]
        for off, rows, w_blk in _down_blocks(w_refs):
            act = _sq_relu(up_ref[:, off:off + rows]).astype(BF16)
            h = h + lax.dot_general(act, w_blk, _NN, preferred_element_type=F32)
        if has_norm:
            g_ref, h_ref, y_ref = rest
            y_ref[...] = _rmsnorm(h, g_ref[...]).astype(BF16)
        else:
            (h_ref,) = rest
        h_ref[...] = h

    shapes = [jax.ShapeDtypeStruct((t, D_MODEL), F32)] + ([jax.ShapeDtypeStruct((t, D_MODEL), BF16)] if has_norm else [])
    return _unpack(_pcall(body, name=name, grid=(t // TM,),
                          in_specs=[_rows(D_FF)] + [_whole(w) for w in w_rows] + [_rows(D_MODEL)]
                          + ([_acc_row(D_MODEL)] if has_norm else []),
                          out_specs=[_rows(D_MODEL)] * len(shapes), out_shape=shapes, semantics=("parallel",),
                          moves=moves)(up, *w_rows, res, *([norm_g] if has_norm else [])), moves, len(shapes))


def _mlp_down_dx(dh, w_rows, up, *, name, moves=()):
    t = up.shape[0]
    n_w = len(w_rows)

    def body(dh_ref, *rest):
        w_refs, (up_ref, o_ref) = rest[:n_w], rest[n_w:]
        dhv = dh_ref[...]
        for off, rows, w_blk in _down_blocks(w_refs):
            cols = slice(off, off + rows)
            d_act = lax.dot_general(dhv, w_blk, _NT, preferred_element_type=F32)
            o_ref[:, cols] = (d_act * (2.0 * jnp.maximum(up_ref[:, cols].astype(F32), 0.0))).astype(o_ref.dtype)

    return _unpack(_pcall(body, name=name, grid=(t // TM,),
                          in_specs=[_rows(D_MODEL)] + [_whole(w) for w in w_rows] + [_rows(D_FF)],
                          out_specs=[_rows(D_FF)], out_shape=[jax.ShapeDtypeStruct((t, D_FF), BF16)],
                          semantics=("parallel",), moves=moves)(dh, *w_rows, up), moves, 1)


def _dw_by_cols(x, dy, *, name, tn, by_device=False, moves=()):
    t, k = x.shape
    n = dy.shape[1]
    assert n % tn == 0, (name, n, tn)

    def body(x_ref, dy_ref, o_ref):
        o_ref[...] = lax.dot_general(x_ref[...].astype(BF16), dy_ref[...].astype(BF16), _TN,
                                     preferred_element_type=F32).astype(o_ref.dtype)

    if by_device:
        out_spec, out_shape = pl.BlockSpec((None, k, tn), lambda j: (j, 0, 0)), (n // tn, k, tn)
    else:
        out_spec, out_shape = pl.BlockSpec((k, tn), lambda j: (0, j)), (k, n)
    return _unpack(_pcall(body, name=name, grid=(n // tn,),
                          in_specs=[_whole(x), pl.BlockSpec((t, tn), lambda j: (0, j))],
                          out_specs=[out_spec], out_shape=[jax.ShapeDtypeStruct(out_shape, BF16)],
                          semantics=("parallel",), moves=moves)(x, dy), moves, 1)


def _dw_by_rows(x, dy, *, name, tk, square_relu=False, moves=()):
    t, k = x.shape
    n = dy.shape[1]
    assert k % tk == 0, (name, k, tk)

    def body(x_ref, dy_ref, o_ref):
        xv = _sq_relu(x_ref[...]) if square_relu else x_ref[...]
        o_ref[...] = lax.dot_general(xv.astype(BF16), dy_ref[...].astype(BF16), _TN,
                                     preferred_element_type=F32).astype(o_ref.dtype)

    return _unpack(_pcall(body, name=name, grid=(k // tk,),
                          in_specs=[pl.BlockSpec((t, tk), lambda j: (0, j)), _whole(dy)],
                          out_specs=[pl.BlockSpec((tk, n), lambda j: (j, 0))],
                          out_shape=[jax.ShapeDtypeStruct((k, n), BF16)],
                          semantics=("parallel",), moves=moves)(x, dy), moves, 1)


def _dx(dy, w, *, name, partial=None, moves=()):
    t, k = dy.shape
    n = w.shape[0]
    has_partial = partial is not None

    def body(dy_ref, w_ref, *rest):
        out = lax.dot_general(dy_ref[...].astype(BF16), w_ref[...], _NT, preferred_element_type=F32)
        if has_partial:
            out = out + rest[0][...]
        rest[-1][...] = out

    return _unpack(_pcall(body, name=name, grid=(t // TM,),
                          in_specs=[_rows(k), _whole(w)] + ([_rows(n)] if has_partial else []),
                          out_specs=[_rows(n)], out_shape=[jax.ShapeDtypeStruct((t, n), F32)],
                          semantics=("parallel",), moves=moves)(dy, w, *([partial] if has_partial else [])),
                   moves, 1)


def _dx_norm(dy, w, h, g, dres, *, name, partial=None, by_device_cols=False, w_transposed=False, moves=()):
    t, k = dy.shape
    d = h.shape[1]
    has_partial = partial is not None

    def body(dy_ref, w_ref, h_ref, g_ref, dres_ref, *rest):
        if by_device_cols:
            kc = k // N_DEV
            d_y = jnp.zeros((TM, d), F32)
            for j in range(N_DEV):
                d_y = d_y + lax.dot_general(dy_ref[:, j * kc:(j + 1) * kc].astype(BF16), w_ref[j], _NT,
                                            preferred_element_type=F32)
        else:
            d_y = lax.dot_general(dy_ref[...].astype(BF16), w_ref[...], _NN if w_transposed else _NT,
                                  preferred_element_type=F32)
        if has_partial:
            d_y = d_y + rest[0][...]
        dh_ref, dhb_ref, dg_ref, cs_ref = rest[-4:]
        _, vjp = jax.vjp(_rmsnorm, h_ref[...], g_ref[...])
        dh, dg = vjp(d_y)
        dh = dh + dres_ref[...]
        dh_ref[...] = dh
        dhb_ref[...] = dh.astype(BF16)

        @pl.when(pl.program_id(0) == 0)
        def _():
            dg_ref[...] = jnp.zeros_like(dg_ref)
            cs_ref[...] = jnp.zeros_like(cs_ref)

        dg_ref[...] += dg
        cs_ref[...] += jnp.sum(dh, axis=0, keepdims=True)

    shapes = [jax.ShapeDtypeStruct((t, d), F32), jax.ShapeDtypeStruct((t, d), BF16),
              jax.ShapeDtypeStruct((1, d), F32), jax.ShapeDtypeStruct((1, d), F32)]
    return _unpack(_pcall(body, name=name, grid=(t // TM,),
                          in_specs=[_rows(k), _whole(w), _rows(d), _acc_row(d), _rows(d)]
                          + ([_rows(d)] if has_partial else []),
                          out_specs=[_rows(d), _rows(d), _acc_row(d), _acc_row(d)], out_shape=shapes,
                          semantics=("arbitrary",), moves=moves)(dy, w, h, g, dres, *([partial] if has_partial else [])),
                   moves, 4)


def _pair_add(by_core, theirs, core, *, name, tb=512):
    n_chip, _, r, c = by_core.shape
    tb = min(tb, r)
    assert r % tb == 0, (name, r, tb)

    def body(core_ref, a_ref, b_ref, o_ref):
        del core_ref
        o_ref[...] = (a_ref[...].astype(F32) + b_ref[...].astype(F32)).astype(o_ref.dtype)

    blk = pl.BlockSpec((None, tb, c), lambda ch, i, core_ref: (ch, i, 0))
    return pl.pallas_call(
        body, name=name,
        grid_spec=pltpu.PrefetchScalarGridSpec(
            num_scalar_prefetch=1, grid=(n_chip, r // tb),
            in_specs=[pl.BlockSpec((None, None, tb, c), lambda ch, i, core_ref: (ch, core_ref[0], i, 0)), blk],
            out_specs=blk),
        out_shape=jax.ShapeDtypeStruct((n_chip, r, c), by_core.dtype),
        compiler_params=pltpu.CompilerParams(dimension_semantics=("parallel", "parallel"),
                                             vmem_limit_bytes=VMEM_LIMIT_BYTES),
    )(core, by_core, theirs)


def _colsum(a, *, name, tb=512):
    t, d = a.shape

    def body(a_ref, o_ref):
        @pl.when(pl.program_id(0) == 0)
        def _():
            o_ref[...] = jnp.zeros_like(o_ref)

        o_ref[...] += jnp.sum(a_ref[...].astype(F32), axis=0, keepdims=True)

    return _pcall(
        body, name=name, grid=(t // tb,),
        in_specs=[pl.BlockSpec((tb, d), lambda i: (i, 0))],
        out_specs=[pl.BlockSpec((1, d), lambda i: (0, 0))],
        out_shape=[jax.ShapeDtypeStruct((1, d), F32)],
        semantics=("arbitrary",),
    )(a)[0][0]


def _mlp_down_loss(up, w_rows, res, g, target, *, name):
    t, d = res.shape
    n_w = len(w_rows)

    def body(up_ref, *rest):
        w_refs, (res_ref, g_ref, tgt_ref, loss_ref, dh_ref, dhb_ref, dg_ref) = rest[:n_w], rest[n_w:]
        h = res_ref[...]
        for off, rows, w_blk in _down_blocks(w_refs):
            act = _sq_relu(up_ref[:, off:off + rows]).astype(BF16)
            h = h + lax.dot_general(act, w_blk, _NN, preferred_element_type=F32)

        def f(hh, gg):
            err = jnp.square(_rmsnorm(hh, gg) - tgt_ref[...])
            return 0.5 * jnp.sum(jnp.mean(err, axis=-1, keepdims=True), axis=0, keepdims=True)

        val, vjp = jax.vjp(f, h, g_ref[...])
        dh, dg = vjp(jnp.ones((1, 1), F32))
        dh_ref[...] = dh
        dhb_ref[...] = dh.astype(BF16)

        @pl.when(pl.program_id(0) == 0)
        def _():
            loss_ref[...] = jnp.zeros_like(loss_ref)
            dg_ref[...] = jnp.zeros_like(dg_ref)

        loss_ref[...] += val
        dg_ref[...] += dg

    return _pcall(
        body, name=name, grid=(t // TM,),
        in_specs=[_rows(D_FF)] + [_whole(w) for w in w_rows] + [_rows(d), _acc_row(d), _rows(d)],
        out_specs=[pl.BlockSpec((8, LANES), lambda i: (0, 0)), _rows(d), _rows(d), _acc_row(d)],
        out_shape=[jax.ShapeDtypeStruct((8, LANES), F32), jax.ShapeDtypeStruct((t, d), F32),
                   jax.ShapeDtypeStruct((t, d), BF16), jax.ShapeDtypeStruct((1, d), F32)],
        semantics=("arbitrary",),
    )(up, *w_rows, res, g, target)[0]


def _gmlp_fwd(proj_uv, ln_g, ln_b, w_s, b_s, *, name, moves=()):
    t = proj_uv.shape[0]
    w = D_MODEL

    def body(u_ref, v_ref, g_ref, b_ref, w_ref, bs_ref, o_ref):
        o_ref[...] = _gmlp_chunk(u_ref[...], v_ref[...], g_ref[...], b_ref[...], w_ref[...],
                                 bs_ref[...]).astype(o_ref.dtype)

    row = pl.BlockSpec((1, w), lambda i: (0, 0))
    res, landed = _pcall(
        body, name=name, grid=(t // CHUNK,),
        in_specs=[pl.BlockSpec((CHUNK, w), lambda i: (i, 0)), pl.BlockSpec((CHUNK, w), lambda i: (i, 1)), row, row,
                  pl.BlockSpec((GM_GROUPS, CHUNK, CHUNK), lambda i: (0, 0, 0)),
                  pl.BlockSpec((GM_GROUPS, CHUNK, 1), lambda i: (0, 0, 0))],
        out_specs=[pl.BlockSpec((CHUNK, w), lambda i: (i, 0))],
        out_shape=[jax.ShapeDtypeStruct((t, 2 * w), BF16)],
        semantics=("parallel",), moves=moves,
    )(proj_uv, proj_uv, ln_g, ln_b, w_s, b_s)
    return (res[0], landed) if moves else res[0]


def _gmlp_bwd(proj_uv, d_mix, ln_g, ln_b, w_s, b_s, *, name, moves=()):
    t = proj_uv.shape[0]
    w = D_MODEL

    def body(u_ref, v_ref, da_ref, g_ref, b_ref, w_ref, bs_ref, duv_ref, dg_ref, db_ref, dw_ref, dbs_ref):
        _, vjp = jax.vjp(_gmlp_chunk, u_ref[...], v_ref[...], g_ref[...], b_ref[...], w_ref[...], bs_ref[...])
        du, dv, dg, db, dw, dbs = vjp(da_ref[...])
        duv_ref[:, :w] = du.astype(duv_ref.dtype)
        duv_ref[:, w:] = dv.astype(duv_ref.dtype)

        @pl.when(pl.program_id(0) == 0)
        def _():
            dg_ref[...] = jnp.zeros_like(dg_ref)
            db_ref[...] = jnp.zeros_like(db_ref)
            dw_ref[...] = jnp.zeros_like(dw_ref)
            dbs_ref[...] = jnp.zeros_like(dbs_ref)

        dg_ref[...] += dg
        db_ref[...] += db
        dw_ref[...] += dw
        dbs_ref[...] += dbs

    row = pl.BlockSpec((1, w), lambda i: (0, 0))
    ws = pl.BlockSpec((GM_GROUPS, CHUNK, CHUNK), lambda i: (0, 0, 0))
    bs = pl.BlockSpec((GM_GROUPS, CHUNK, 1), lambda i: (0, 0, 0))
    res, landed = _pcall(
        body, name=name, grid=(t // CHUNK,),
        in_specs=[pl.BlockSpec((CHUNK, w), lambda i: (i, 0)), pl.BlockSpec((CHUNK, w), lambda i: (i, 1)),
                  pl.BlockSpec((CHUNK, w), lambda i: (i, 0)), row, row, ws, bs],
        out_specs=[pl.BlockSpec((CHUNK, 2 * w), lambda i: (i, 0)), row, row, ws, bs],
        out_shape=[jax.ShapeDtypeStruct((t, 2 * w), BF16), jax.ShapeDtypeStruct((1, w), F32),
                   jax.ShapeDtypeStruct((1, w), F32), jax.ShapeDtypeStruct((GM_GROUPS, CHUNK, CHUNK), F32),
                   jax.ShapeDtypeStruct((GM_GROUPS, CHUNK, 1), F32)],
        semantics=("arbitrary",), moves=moves,
    )(proj_uv, proj_uv, d_mix, ln_g, ln_b, w_s, b_s)
    return (res, landed) if moves else res


_HALO_PER_CHUNK = CHUNK // HALO
_DT_BLOCK = (CONV_DIM + D_MODEL) // LANES


def _ssd_fwd(proj_rest, mix, conv_w, conv_b, dt_bias, a_log, d_skip, norm_g, *, name, moves=()):
    t = proj_rest.shape[0]
    nc = t // CHUNK

    def body(x_ref, prev_ref, z_ref, dt_ref, mix_ref, cw_ref, cb_ref, dtb_ref, al_ref, ds_ref, ng_ref, y_ref, hs_ref,
             pre_ref, h_scr):
        del mix_ref
        i = pl.program_id(0)

        @pl.when(i == 0)
        def _():
            h_scr[...] = jnp.zeros_like(h_scr)

        prev8 = jnp.where(i == 0, 0.0, prev_ref[...])
        pre = _conv_pre(prev8, x_ref[...], cw_ref[...], cb_ref[...])
        pre_ref[...] = pre
        hs_ref[0] = h_scr[...]
        h_prev = tuple(h_scr[j] for j in range(_PAIRS))
        y, h_next = _ssd_chunk(pre, z_ref[...], dt_ref[...], h_prev, dtb_ref[...], al_ref[...], ds_ref[...],
                               ng_ref[...])
        y_ref[...] = y.astype(y_ref.dtype)
        for j in range(_PAIRS):
            h_scr[j] = h_next[j]

    small = pl.BlockSpec((1, LANES), lambda i: (0, 0))
    res, landed = _pcall(
        body, name=name, grid=(nc,),
        in_specs=[pl.BlockSpec((CHUNK, CONV_DIM), lambda i: (i, 0)),
                  pl.BlockSpec((HALO, CONV_DIM), lambda i: (jnp.maximum(i * _HALO_PER_CHUNK - 1, 0), 0)),
                  pl.BlockSpec((CHUNK, D_MODEL), lambda i: (i, CONV_DIM // D_MODEL)),
                  pl.BlockSpec((CHUNK, LANES), lambda i: (i, _DT_BLOCK)),
                  pl.BlockSpec(memory_space=pl.ANY),
                  pl.BlockSpec((SSM_CONV, CONV_DIM), lambda i: (0, 0)),
                  pl.BlockSpec((1, CONV_DIM), lambda i: (0, 0)),
                  small, small, small, pl.BlockSpec((1, D_MODEL), lambda i: (0, 0))],
        out_specs=[pl.BlockSpec((CHUNK, D_MODEL), lambda i: (i, 1)),
                   pl.BlockSpec((1, _PAIRS, SSM_STATE, LANES), lambda i: (i, 0, 0, 0)),
                   pl.BlockSpec((CHUNK, CONV_DIM), lambda i: (i, 0))],
        out_shape=[jax.ShapeDtypeStruct((t, 2 * D_MODEL), BF16),
                   jax.ShapeDtypeStruct((nc, _PAIRS, SSM_STATE, LANES), F32),
                   jax.ShapeDtypeStruct((t, CONV_DIM), F32)],
        scratch_shapes=[pltpu.VMEM((_PAIRS, SSM_STATE, LANES), F32)],
        semantics=("arbitrary",), moves=moves, aliases={4: 0},
    )(proj_rest, proj_rest, proj_rest, proj_rest, mix, conv_w, conv_b, dt_bias, a_log, d_skip, norm_g)
    return (res, landed) if moves else res


def _ssd_bwd(proj_rest, pre, h_states, d_mix, dt_bias, a_log, d_skip, norm_g, *, name, moves=()):
    t = proj_rest.shape[0]
    nc = t // CHUNK

    def body(pre_ref, z_ref, dt_ref, hs_ref, dy_ref, dtb_ref, al_ref, ds_ref, ng_ref,
             dpre_ref, dz_ref, ddt_ref, ddtb_ref, dal_ref, dds_ref, dng_ref, dh_scr):
        i = pl.program_id(0)

        @pl.when(i == 0)
        def _():
            dh_scr[...] = jnp.zeros_like(dh_scr)
            ddtb_ref[...] = jnp.zeros_like(ddtb_ref)
            dal_ref[...] = jnp.zeros_like(dal_ref)
            dds_ref[...] = jnp.zeros_like(dds_ref)
            dng_ref[...] = jnp.zeros_like(dng_ref)

        h_prev = tuple(hs_ref[0, j] for j in range(_PAIRS))
        _, vjp = jax.vjp(_ssd_chunk, pre_ref[...], z_ref[...], dt_ref[...], h_prev, dtb_ref[...], al_ref[...],
                         ds_ref[...], ng_ref[...])
        dpre, dz, ddt, dh_prev, ddtb, dal, dds, dng = vjp((dy_ref[...], tuple(dh_scr[j] for j in range(_PAIRS))))
        dpre_ref[...] = dpre
        dz_ref[...] = dz.astype(dz_ref.dtype)
        ddt_ref[...] = ddt.astype(ddt_ref.dtype)
        for j in range(_PAIRS):
            dh_scr[j] = dh_prev[j]
        ddtb_ref[...] += ddtb
        dal_ref[...] += dal
        dds_ref[...] += dds
        dng_ref[...] += dng

    rev = lambda i: nc - 1 - i
    small = pl.BlockSpec((1, LANES), lambda i: (0, 0))
    wide = pl.BlockSpec((1, D_MODEL), lambda i: (0, 0))
    res, landed = _pcall(
        body, name=name, grid=(nc,),
        in_specs=[pl.BlockSpec((CHUNK, CONV_DIM), lambda i: (rev(i), 0)),
                  pl.BlockSpec((CHUNK, D_MODEL), lambda i: (rev(i), CONV_DIM // D_MODEL)),
                  pl.BlockSpec((CHUNK, LANES), lambda i: (rev(i), _DT_BLOCK)),
                  pl.BlockSpec((1, _PAIRS, SSM_STATE, LANES), lambda i: (rev(i), 0, 0, 0)),
                  pl.BlockSpec((CHUNK, D_MODEL), lambda i: (rev(i), 1)),
                  small, small, small, wide],
        out_specs=[pl.BlockSpec((CHUNK, CONV_DIM), lambda i: (rev(i), 0)),
                   pl.BlockSpec((CHUNK, D_MODEL), lambda i: (rev(i), 0)),
                   pl.BlockSpec((CHUNK, LANES), lambda i: (rev(i), 0)),
                   small, small, small, wide],
        out_shape=[jax.ShapeDtypeStruct((t, CONV_DIM), F32), jax.ShapeDtypeStruct((t, D_MODEL), BF16),
                   jax.ShapeDtypeStruct((t, LANES), BF16),
                   jax.ShapeDtypeStruct((1, LANES), F32), jax.ShapeDtypeStruct((1, LANES), F32),
                   jax.ShapeDtypeStruct((1, LANES), F32), jax.ShapeDtypeStruct((1, D_MODEL), F32)],
        scratch_shapes=[pltpu.VMEM((_PAIRS, SSM_STATE, LANES), F32)],
        semantics=("arbitrary",), moves=moves,
    )(pre, proj_rest, proj_rest, h_states, d_mix, dt_bias, a_log, d_skip, norm_g)
    return (res, landed) if moves else res


def _conv_bwd(proj_rest, dpre, dz, ddt, conv_w, *, name, tb=256, moves=()):
    t = proj_rest.shape[0]
    nb = t // tb
    per = tb // HALO

    def body(x_ref, prev_ref, dpre_ref, next_ref, dz_ref, ddt_ref, cw_ref, drest_ref, dcw_ref, dcb_ref):
        i = pl.program_id(0)

        @pl.when(i == 0)
        def _():
            dcw_ref[...] = jnp.zeros_like(dcw_ref)
            dcb_ref[...] = jnp.zeros_like(dcb_ref)

        x = x_ref[...]
        dp = dpre_ref[...]
        w = cw_ref[...]
        prev8 = jnp.where(i == 0, 0.0, prev_ref[...])
        next8 = jnp.where(i == nb - 1, 0.0, next_ref[...])
        dx = dp * w[SSM_CONV - 1:SSM_CONV]
        for j in range(SSM_CONV - 1):
            dx = dx + _shift_up(dp, next8, SSM_CONV - 1 - j) * w[j:j + 1]
        drest_ref[:, :CONV_DIM] = dx.astype(drest_ref.dtype)
        drest_ref[:, CONV_DIM:CONV_DIM + D_MODEL] = dz_ref[...].astype(drest_ref.dtype)
        drest_ref[:, CONV_DIM + D_MODEL:] = ddt_ref[...].astype(drest_ref.dtype)
        for j in range(SSM_CONV):
            dcw_ref[j:j + 1, :] += jnp.sum(dp * _shift_down(prev8, x, SSM_CONV - 1 - j), axis=0, keepdims=True)
        dcb_ref[...] += jnp.sum(dp, axis=0, keepdims=True)

    res, landed = _pcall(
        body, name=name, grid=(nb,),
        in_specs=[pl.BlockSpec((tb, CONV_DIM), lambda i: (i, 0)),
                  pl.BlockSpec((HALO, CONV_DIM), lambda i: (jnp.maximum(i * per - 1, 0), 0)),
                  pl.BlockSpec((tb, CONV_DIM), lambda i: (i, 0)),
                  pl.BlockSpec((HALO, CONV_DIM), lambda i: (jnp.minimum((i + 1) * per, nb * per - 1), 0)),
                  pl.BlockSpec((tb, D_MODEL), lambda i: (i, 0)),
                  pl.BlockSpec((tb, LANES), lambda i: (i, 0)),
                  pl.BlockSpec((SSM_CONV, CONV_DIM), lambda i: (0, 0))],
        out_specs=[pl.BlockSpec((tb, REST_W), lambda i: (i, 0)),
                   pl.BlockSpec((SSM_CONV, CONV_DIM), lambda i: (0, 0)),
                   pl.BlockSpec((1, CONV_DIM), lambda i: (0, 0))],
        out_shape=[jax.ShapeDtypeStruct((t, REST_W), BF16), jax.ShapeDtypeStruct((SSM_CONV, CONV_DIM), F32),
                   jax.ShapeDtypeStruct((1, CONV_DIM), F32)],
        semantics=("arbitrary",), moves=moves,
    )(proj_rest, proj_rest, dpre, dpre, dz, ddt, conv_w)
    return (res, landed) if moves else res


_KV_BLOCK = D_MODEL // (2 * LANES)
_SINK_ROWS = _PAIRS_PER_KV * CHUNK


def _stack_pairs(ref, kv_head):
    base = kv_head * _PAIRS_PER_KV
    return jnp.concatenate([ref[:, (base + p) * LANES:(base + p + 1) * LANES] for p in range(_PAIRS_PER_KV)], axis=0)


def _attn_fwd(qkv, sinks, *, name, moves=()):
    t = qkv.shape[0]
    nb = t // CHUNK

    def body(q_ref, kvp_ref, kvc_ref, s_ref, o_ref, p_ref, st_ref):
        first = pl.program_id(0) == 0
        kv = jnp.concatenate([kvp_ref[...], kvc_ref[...]], axis=0)
        stats = jnp.zeros((_SINK_ROWS, LANES), F32)
        for j in range(ATTN_KV):
            q4 = _stack_pairs(q_ref, j)
            ks, vs = _kv_placed(kv[:, :LANES], j), _kv_placed(kv[:, LANES:], j)
            out = None
            for e in range(2):
                p, e_sink, den = _attn_probs(q4, ks[e], s_ref[j, e], first)
                inv = 1.0 / den
                o = _dg(p, vs[e], _NN) * inv
                out = o if out is None else out + o
                p_ref[0, 2 * j + e] = p.astype(p_ref.dtype)
                stats = stats + _lane_column(inv, 2 * j + e) + _lane_column(e_sink, 4 + 2 * j + e)
            for pair in range(_PAIRS_PER_KV):
                col = (j * _PAIRS_PER_KV + pair) * LANES
                o_ref[:, col:col + LANES] = out[pair * CHUNK:(pair + 1) * CHUNK].astype(o_ref.dtype)
        st_ref[0] = stats

    return _unpack(_pcall(
        body, name=name, grid=(nb,),
        in_specs=[pl.BlockSpec((CHUNK, D_MODEL), lambda i: (i, 0)),
                  pl.BlockSpec((CHUNK, 2 * LANES), lambda i: (jnp.maximum(i - 1, 0), _KV_BLOCK)),
                  pl.BlockSpec((CHUNK, 2 * LANES), lambda i: (i, _KV_BLOCK)),
                  pl.BlockSpec((ATTN_KV, 2, _SINK_ROWS, 1), lambda i: (0, 0, 0, 0))],
        out_specs=[pl.BlockSpec((CHUNK, D_MODEL), lambda i: (i, 0)),
                   pl.BlockSpec((1, 2 * ATTN_KV, _SINK_ROWS, 2 * CHUNK), lambda i: (i, 0, 0, 0)),
                   pl.BlockSpec((1, _SINK_ROWS, LANES), lambda i: (i, 0, 0))],
        out_shape=[jax.ShapeDtypeStruct((t, D_MODEL), BF16),
                   jax.ShapeDtypeStruct((nb, 2 * ATTN_KV, _SINK_ROWS, 2 * CHUNK), BF16),
                   jax.ShapeDtypeStruct((nb, _SINK_ROWS, LANES), F32)],
        semantics=("parallel",), moves=moves,
    )(qkv, qkv, qkv, sinks), moves, 3)


def _attn_bwd(qkv, probs, stats, attn, d_o, *, name, moves=()):
    t = qkv.shape[0]
    nb = t // CHUNK

    def body(q_ref, kvp_ref, kvc_ref, p_ref, st_ref, o_ref, do_ref, dqkv_ref, ds_ref, dkv_scr):
        @pl.when(pl.program_id(0) == 0)
        def _():
            dkv_scr[...] = jnp.zeros_like(dkv_scr)
            ds_ref[...] = jnp.zeros_like(ds_ref)

        kv = jnp.concatenate([kvp_ref[...], kvc_ref[...]], axis=0)
        table = st_ref[0]
        d_k = jnp.zeros((2 * CHUNK, LANES), F32)
        d_v = jnp.zeros((2 * CHUNK, LANES), F32)
        for j in range(ATTN_KV):
            q4, do4, o4 = _stack_pairs(q_ref, j), _stack_pairs(do_ref, j), _stack_pairs(o_ref, j).astype(F32)
            ks, vs = _kv_placed(kv[:, :LANES], j), _kv_placed(kv[:, LANES:], j)
            dq4, dk, dv = None, [], []
            for e in range(2):
                p = p_ref[0, 2 * j + e].astype(F32)
                inv, e_sink = _col_pick(table, 2 * j + e), _col_pick(table, 4 + 2 * j + e)
                do_e = jnp.where(_parity_lanes(e), do4, 0.0)
                d_num = do_e * inv
                d_den = -jnp.sum(do_e * o4, axis=1, keepdims=True) * inv
                ds = p * (_dg(d_num, vs[e], _NT) + d_den)
                ds_ref[j, e] += d_den * e_sink
                dq = _dg(ds, ks[e], _NN) * _ATTN_SCALE
                dq4 = dq if dq4 is None else dq4 + dq
                dk.append(_dg(ds, q4, _TN) * _ATTN_SCALE)
                dv.append(_dg(p, d_num, _TN))
            for pair in range(_PAIRS_PER_KV):
                col = (j * _PAIRS_PER_KV + pair) * LANES
                dqkv_ref[:, col:col + LANES] = dq4[pair * CHUNK:(pair + 1) * CHUNK]
            d_k = d_k + _kv_unplaced(dk[0], dk[1], j)
            d_v = d_v + _kv_unplaced(dv[0], dv[1], j)
        d_kv = jnp.concatenate([d_k, d_v], axis=1)
        dqkv_ref[:, D_MODEL:] = d_kv[CHUNK:] + dkv_scr[...]
        dkv_scr[...] = d_kv[:CHUNK]

    cur = lambda i: (nb - 1 - i, 0)
    sk = pl.BlockSpec((ATTN_KV, 2, _SINK_ROWS, 1), lambda i: (0, 0, 0, 0))
    res, landed = _pcall(
        body, name=name, grid=(nb,),
        in_specs=[pl.BlockSpec((CHUNK, D_MODEL), cur),
                  pl.BlockSpec((CHUNK, 2 * LANES), lambda i: (jnp.maximum(nb - 2 - i, 0), _KV_BLOCK)),
                  pl.BlockSpec((CHUNK, 2 * LANES), lambda i: (nb - 1 - i, _KV_BLOCK)),
                  pl.BlockSpec((1, 2 * ATTN_KV, _SINK_ROWS, 2 * CHUNK), lambda i: (nb - 1 - i, 0, 0, 0)),
                  pl.BlockSpec((1, _SINK_ROWS, LANES), lambda i: (nb - 1 - i, 0, 0)),
                  pl.BlockSpec((CHUNK, D_MODEL), cur), pl.BlockSpec((CHUNK, D_MODEL), cur)],
        out_specs=[pl.BlockSpec((CHUNK, QKV_DIM), cur), sk],
        out_shape=[jax.ShapeDtypeStruct((t, QKV_DIM), F32), jax.ShapeDtypeStruct((ATTN_KV, 2, _SINK_ROWS, 1), F32)],
        scratch_shapes=[pltpu.VMEM((CHUNK, 2 * LANES), F32)],
        semantics=("arbitrary",), moves=moves,
    )(qkv, qkv, qkv, probs, stats, attn, d_o)
    return (res, landed) if moves else res


def _adamw(parts, w, m, v, *, name, tb=512, moves=()):
    layers, r, c = w.shape
    n = parts[0].shape[0]
    tb = min(tb, r)
    assert r % tb == 0 and len(parts) == layers, (name, r, tb)
    nb = r // tb

    def body(*refs):
        p_refs = refs[:layers]
        w_ref, m_ref, v_ref, g_ref, d_ref, nm_ref, nv_ref = refs[layers:]
        for layer in range(layers):
            @pl.when(pl.program_id(0) == layer)
            def _(p_ref=p_refs[layer]):
                g = p_ref[0].astype(F32)
                for s in range(1, n):
                    g = g + p_ref[s].astype(F32)
                m_new = ADAM_B1 * m_ref[...] + (1.0 - ADAM_B1) * g
                v_new = ADAM_B2 * v_ref[...] + (1.0 - ADAM_B2) * jnp.square(g)
                m_hat = m_new / (1.0 - ADAM_B1 ** ADAM_STEP)
                v_hat = v_new / (1.0 - ADAM_B2 ** ADAM_STEP)
                g_ref[...] = g
                d_ref[...] = -ADAM_LR * (m_hat / (jnp.sqrt(v_hat) + ADAM_EPS) + ADAM_WD * w_ref[...])
                nm_ref[...] = m_new
                nv_ref[...] = v_new

    part_spec = lambda layer: pl.BlockSpec(
        (n, tb, c), lambda l, i: (0, jnp.clip(i + (l - layer) * nb, 0, nb - 1), 0))
    blk = pl.BlockSpec((None, tb, c), lambda l, i: (l, i, 0))
    res, landed = _pcall(
        body, name=name, grid=(layers, nb),
        in_specs=[part_spec(layer) for layer in range(layers)] + [blk, blk, blk],
        out_specs=[blk] * 4,
        out_shape=[jax.ShapeDtypeStruct((layers, r, c), F32)] * 4,
        semantics=("arbitrary", "arbitrary"), moves=moves,
    )(*parts, w, m, v)
    return (res, landed) if moves else res


def _as_rows(a):
    flat = a.reshape(-1)
    pad = (-flat.shape[0]) % PACK_W
    if pad:
        flat = jnp.pad(flat, (0, pad))
    return flat.reshape(-1, PACK_W)


def _cols_from_shards(g):
    return jnp.transpose(g, (1, 0, 2)).reshape(g.shape[1], -1)


def _shard_cols(shards, lo, hi):
    c = shards.shape[2]
    pieces = []
    for j in range(shards.shape[0]):
        a, b = max(lo, j * c), min(hi, (j + 1) * c)
        if a < b:
            pieces.append(shards[j, :, a - j * c:b - j * c])
    return pieces


def _cols_of(sources, lo, hi):
    pieces = []
    for arr, col0, first, last in sources:
        a, b = max(lo, first), min(hi, last)
        if a < b:
            pieces.append(arr[:, col0 + a - first:col0 + b - first])
    return pieces


def _pad_lanes(a):
    return jnp.pad(a, ((0, 0), (0, LANES - a.shape[1])))


def kernel(x, norm_mix_g, norm_mlp_g, final_norm_g, w_in_even, w_out_even, gm_ln_g, gm_ln_b, gm_w_s, gm_b_s, ssm_conv_w, ssm_conv_b, ssm_dt_bias, ssm_a_log, ssm_d, ssm_norm_g, w_qkv, b_qkv, w_o, b_o, attn_sinks, w_up, w_down, loss_target, m_norm_mix_g, m_norm_mlp_g, m_final_norm_g, m_w_in_even, m_w_out_even, m_gm_ln_g, m_gm_ln_b, m_gm_w_s, m_gm_b_s, m_ssm_conv_w, m_ssm_conv_b, m_ssm_dt_bias, m_ssm_a_log, m_ssm_d, m_ssm_norm_g, m_w_qkv, m_b_qkv, m_w_o, m_b_o, m_attn_sinks, m_w_up, m_w_down, v_norm_mix_g, v_norm_mlp_g, v_final_norm_g, v_w_in_even, v_w_out_even, v_gm_ln_g, v_gm_ln_b, v_gm_w_s, v_gm_b_s, v_ssm_conv_w, v_ssm_conv_b, v_ssm_dt_bias, v_ssm_a_log, v_ssm_d, v_ssm_norm_g, v_w_qkv, v_b_qkv, v_w_o, v_b_o, v_attn_sinks, v_w_up, v_w_down):
    names = ["norm_mix_g", "norm_mlp_g", "final_norm_g", "w_in_even", "w_out_even", "gm_ln_g", "gm_ln_b", "gm_w_s",
             "gm_b_s", "ssm_conv_w", "ssm_conv_b", "ssm_dt_bias", "ssm_a_log", "ssm_d", "ssm_norm_g", "w_qkv",
             "b_qkv", "w_o", "b_o", "attn_sinks", "w_up", "w_down"]
    env = locals()
    W = {n: env[n] for n in names}
    M = {n: env["m_" + n] for n in names}
    V = {n: env["v_" + n] for n in names}
    big = ["w_in_even", "w_out_even", "w_qkv", "w_o", "w_up", "w_down"]
    small_sharded = ["ssm_conv_w", "b_qkv", "b_o"]
    replicated = [n for n in names if n not in big and n not in small_sharded]
    me = 4 * lax.axis_index("x") + 2 * lax.axis_index("y") + lax.axis_index("c")
    t = x.shape[1]
    xs = x.reshape(t, D_MODEL)
    target = loss_target.reshape(t, D_MODEL)
    gather = lambda a: _Move("gather", a)
    over_ici = lambda a: _Move("gather_ici", a)
    over_d2d = lambda a: _Move("gather_d2d", a)
    by_core = lambda a: a.reshape((N_CHIP, N_CORE) + a.shape[1:])
    to_sibling = lambda a: [_Move("scatter_d2d", by_core(a))]
    my_core = lax.axis_index("c").astype(jnp.int32).reshape(1)
    pair = lambda a, theirs, name: _pair_add(by_core(a), theirs, my_core, name=name)
    to_chips = lambda a: _Move("scatter_ici", a)
    whole = lambda a: a.reshape((N_DEV,) + a.shape[2:])
    row = lambda a: a.reshape(1, D_MODEL)

    small_flat = jnp.concatenate([W[n].reshape(-1) for n in small_sharded])
    w_in_g, small_g = _exchange([over_ici(w_in_even[0].astype(BF16)), gather(_as_rows(small_flat))],
                                name="gather_w_in", then_d2d=[0])
    w_in_s = whole(w_in_g)
    z_lo, xbc_lo, dt_lo = 2 * D_MODEL, 3 * D_MODEL, 3 * D_MODEL + CONV_DIM
    w_uv = jnp.concatenate(_shard_cols(w_in_s, 0, z_lo), axis=1)
    w_rest = jnp.concatenate(_shard_cols(w_in_s, xbc_lo, dt_lo) + _shard_cols(w_in_s, z_lo, xbc_lo)
                             + _shard_cols(w_in_s, dt_lo, IN_EVEN)
                             + [jnp.zeros((D_MODEL, LANES - SSM_HEADS), BF16)], axis=1)
    small_all = small_g.reshape(N_DEV, -1)
    n_cw = SSM_CONV * CONV_DIM // N_DEV
    n_bq = QKV_DIM // N_DEV
    conv_w = _cols_from_shards(small_all[:, :n_cw].reshape(N_DEV, SSM_CONV, CONV_DIM // N_DEV))
    bqkv = small_all[:, n_cw:n_cw + n_bq].reshape(1, QKV_DIM)
    bo = small_all[:, n_cw + n_bq:n_cw + n_bq + D_MODEL // N_DEV].reshape(1, D_MODEL)

    conv_b = ssm_conv_b.reshape(1, CONV_DIM)
    dt_bias, a_log, d_skip = _pad_lanes(ssm_dt_bias), _pad_lanes(ssm_a_log), _pad_lanes(ssm_d)
    gm_w = gm_w_s[0]
    gm_b = gm_b_s[0].reshape(GM_GROUPS, CHUNK, 1)
    sink_rows = jnp.repeat(jnp.transpose(attn_sinks.reshape(ATTN_KV, _PAIRS_PER_KV, 2), (0, 2, 1)), CHUNK,
                           axis=2).reshape(ATTN_KV, 2, _SINK_ROWS, 1)
    w_up_b, w_down_b = w_up.astype(BF16), w_down.astype(BF16)

    w_down0_a, w_down0_b = w_down_b[0, :FF_SHARD // 2], w_down_b[0, FF_SHARD // 2:]
    (y0, proj_uv), (w_qkv_g,) = _norm_matmul(xs, row(norm_mix_g[0]), w_uv, name="proj_uv", emit_y=True,
                                             moves=[over_ici(jnp.transpose(w_qkv[0]).astype(BF16))])
    proj_rest, (w_out_g,) = _norm_matmul(xs, row(norm_mix_g[0]), w_rest, name="proj_rest", emit_y=False,
                                         moves=[over_ici(w_out_even[0].astype(BF16))])
    mix, (w_o_g, w_down0_a, w_out_g, w_qkv_g) = _gmlp_fwd(
        proj_uv, gm_ln_g, gm_ln_b, gm_w, gm_b, name="gmlp_fwd",
        moves=[over_ici(w_o[0].astype(BF16)), over_ici(w_down0_a), over_d2d(w_out_g), over_d2d(w_qkv_g)])
    (mix, h_states, conv_pre), (w_up0_g, w_o_g, w_down0_a) = _ssd_fwd(
        proj_rest, mix, conv_w, conv_b, dt_bias, a_log, d_skip, ssm_norm_g, name="ssd_fwd",
        moves=[over_ici(w_up_b[0]), over_d2d(w_o_g), over_d2d(w_down0_a)])
    w_out_f = whole(w_out_g).reshape(2 * D_MODEL, D_MODEL)
    (h1, y1), (w_down0_b, w_up0_g) = _residual_matmul(
        mix, w_out_f, xs, name="mix_out", norm_g=row(norm_mlp_g[0]),
        moves=[over_ici(w_down0_b), over_d2d(w_up0_g)])
    up0, (w_down0_b,) = _mlp_up(y1, whole(w_up0_g), name="mlp_up0", moves=[over_d2d(w_down0_b)])
    w_down_g = [[whole(w_down0_a), whole(w_down0_b)]]
    h2, y2 = _mlp_down(up0, w_down_g[0], h1, name="mlp_down0", norm_g=row(norm_mix_g[1]))
    wqkv = whole(w_qkv_g).reshape(QKV_DIM, D_MODEL)
    wo = whole(w_o_g).reshape(D_MODEL, D_MODEL)
    qkv = _residual_matmul(y2, wqkv, None, name="qkv", bias=bqkv, w_transposed=True)
    (attn, attn_p, attn_stats), (w_up1_g, w_down1_g) = _attn_fwd(
        qkv, sink_rows, name="attn_fwd", moves=[over_ici(w_up_b[1]), over_ici(w_down_b[1])])
    (h3, y3), (w_up1_g,) = _residual_matmul(attn, wo, h2, name="attn_out", bias=bo, norm_g=row(norm_mlp_g[1]),
                                            moves=[over_d2d(w_up1_g)])
    w_up_g = [whole(w_up0_g), whole(w_up1_g)]
    up1, (w_down1_g,) = _mlp_up(y3, w_up_g[1], name="mlp_up1", moves=[over_d2d(w_down1_g)])
    w_down_g.append([whole(w_down1_g)])
    loss_part, dh4, dh4_b, d_final_g = _mlp_down_loss(up1, w_down_g[1], h3, row(final_norm_g), target,
                                                      name="mlp_down1_loss")

    by_dev_rows = lambda a: a.reshape((N_DEV, a.shape[0] // N_DEV) + a.shape[1:])

    def mlp_bwd(dh, dh_b, h, y, up, layer, first_moves=()):
        res = _mlp_down_dx(dh_b, w_down_g[layer], up, name=f"mlp_down_dx{layer}", moves=first_moves)
        d_up, first_landed = res if first_moves else (res, [])
        g_down = _dw_by_rows(up, dh_b, name=f"mlp_down_dw{layer}", tk=FF_SHARD, square_relu=True)
        g_down = by_dev_rows(g_down)
        g_up, (theirs,) = _dw_by_cols(y, d_up, name=f"mlp_up_dw{layer}", tn=FF_SHARD, by_device=True,
                                      moves=to_sibling(g_down))
        q_down = pair(g_down, theirs, f"mlp_down_pair{layer}")
        (dh_new, dh_new_b, dg, cs), (theirs,) = _dx_norm(
            d_up, w_up_g[layer], h, row(norm_mlp_g[layer]), dh, name=f"mlp_up_dx{layer}", by_device_cols=True,
            moves=to_sibling(g_up))
        q_up = pair(g_up, theirs, f"mlp_up_pair{layer}")
        return dh_new, dh_new_b, cs, dg, q_up, q_down, first_landed

    dh3, dh3_b, cs3, g_nmlp1, q_up1, q_down1, _ = mlp_bwd(dh4, dh4_b, h3, y3, up1, 1)
    g_bo = cs3
    g_wo = by_dev_rows(_dw_by_cols(attn, dh3_b, name="attn_out_dw", tn=FF_SHARD))
    d_attn, (theirs,) = _dx(dh3_b, wo, name="attn_out_dx", moves=to_sibling(g_wo))
    q_wo = pair(g_wo, theirs, "attn_out_pair")
    (dqkv, d_sink), (r_down1, r_up1) = _attn_bwd(qkv, attn_p, attn_stats, attn, d_attn, name="attn_bwd",
                                                 moves=[to_chips(q_down1), to_chips(q_up1)])
    g_bqkv = _colsum(dqkv, name="qkv_db")
    g_wqkv = by_dev_rows(_dw_by_rows(dqkv, y2, name="qkv_dw", tk=QKV_DIM // 2))
    (dh2, dh2_b, g_nmix1, _), (theirs,) = _dx_norm(dqkv, wqkv, h2, row(norm_mix_g[1]), dh3, name="qkv_dx",
                                                   w_transposed=True, moves=to_sibling(g_wqkv))
    q_wqkv = pair(g_wqkv, theirs, "qkv_pair")
    dh1, dh1_b, _, g_nmlp0, q_up0, q_down0, (r_wqkv, r_wo) = mlp_bwd(
        dh2, dh2_b, h1, y1, up0, 0, first_moves=[to_chips(q_wqkv), to_chips(q_wo)])

    d_mix = _dx(dh1_b, w_out_f, name="mix_out_dx")
    g_wout = by_dev_rows(_dw_by_rows(mix, dh1_b, name="mix_out_dw", tk=FF_SHARD))
    (d_uv, g_ln_g, g_ln_b, g_gm_w, g_gm_b), (r_down0, theirs) = _gmlp_bwd(
        proj_uv, d_mix, gm_ln_g, gm_ln_b, gm_w, gm_b, name="gmlp_bwd", moves=[to_chips(q_down0)] + to_sibling(g_wout))
    q_wout = pair(g_wout, theirs, "mix_out_pair")

    early = [("norm_mlp_g", None), ("final_norm_g", None), ("norm_mix_g", 1), ("gm_ln_g", None), ("gm_ln_b", None),
             ("gm_w_s", None), ("gm_b_s", None), ("attn_sinks", None)]
    late = [("norm_mix_g", 0), ("ssm_conv_b", None), ("ssm_dt_bias", None), ("ssm_a_log", None), ("ssm_d", None),
            ("ssm_norm_g", None)]
    early_sharded, late_sharded = ["b_qkv", "b_o"], ["ssm_conv_w"]
    small_grads = {
        ("norm_mlp_g", None): jnp.concatenate([g_nmlp0, g_nmlp1], axis=0),
        ("final_norm_g", None): d_final_g, ("norm_mix_g", 1): g_nmix1,
        ("gm_ln_g", None): g_ln_g, ("gm_ln_b", None): g_ln_b, ("gm_w_s", None): g_gm_w, ("gm_b_s", None): g_gm_b,
        ("attn_sinks", None): jnp.transpose(
            jnp.sum(d_sink.reshape(ATTN_KV, 2, _PAIRS_PER_KV, CHUNK), axis=3), (0, 2, 1)),
        "b_qkv": g_bqkv, "b_o": g_bo,
    }
    pack = lambda keys: _as_rows(jnp.concatenate([small_grads[key].reshape(-1) for key in keys]))
    (dpre, dz, ddt, g_dtb, g_alog, g_dskip, g_ssm_ng), (r_up0, r_wout, early_recv) = _ssd_bwd(
        proj_rest, conv_pre, h_states, d_mix, dt_bias, a_log, d_skip, ssm_norm_g, name="ssd_bwd",
        moves=[to_chips(q_up0), to_chips(q_wout), gather(pack(early + early_sharded))])
    d_rest, g_conv_w, g_conv_b = _conv_bwd(proj_rest, dpre, dz, ddt, conv_w, name="conv_bwd")
    g_w_uv = _dw_by_cols(y0, d_uv, name="proj_uv_dw", tn=FF_SHARD)
    g_w_rest = _dw_by_cols(y0, d_rest, name="proj_rest_dw", tn=REST_W // 5)
    in_cols = [(g_w_uv, 0, 0, z_lo), (g_w_rest, CONV_DIM, z_lo, xbc_lo), (g_w_rest, 0, xbc_lo, dt_lo),
               (g_w_rest, CONV_DIM + D_MODEL, dt_lo, IN_EVEN)]
    in_shard = IN_EVEN // N_DEV
    g_w_in = jnp.stack([jnp.concatenate(_cols_of(in_cols, j * in_shard, (j + 1) * in_shard), axis=1)
                        for j in range(N_DEV)])
    dy0, (theirs,) = _dx(d_uv, w_uv, name="proj_uv_dx", moves=to_sibling(g_w_in))
    q_w_in = pair(g_w_in, theirs, "proj_pair")
    (dx, _, g_nmix0, _), r_w_in = _dx_norm(d_rest, w_rest, xs, row(norm_mix_g[0]), dh1, name="proj_rest_dx",
                                           partial=dy0, moves=[to_chips(q_w_in)])
    small_grads.update({
        ("loss", None): loss_part[:1, :1],
        ("norm_mix_g", 0): g_nmix0, ("ssm_conv_b", None): g_conv_b,
        ("ssm_dt_bias", None): g_dtb[:, :SSM_HEADS], ("ssm_a_log", None): g_alog[:, :SSM_HEADS],
        ("ssm_d", None): g_dskip[:, :SSM_HEADS], ("ssm_norm_g", None): g_ssm_ng, "ssm_conv_w": g_conv_w,
    })


    def update(n, parts, moves=(), transposed=False):
        shape = W[n].shape
        if transposed:
            as3 = lambda a: jnp.transpose(a[0])[None]
            back = lambda a: jnp.transpose(a[0])[None]
        else:
            as3 = lambda a: a.reshape((len(parts),) + parts[0].shape[1:])
            back = lambda a: a.reshape(shape)
        res = _adamw(parts, as3(W[n]), as3(M[n]), as3(V[n]), name="adamw_" + n, moves=moves)
        res, landed = res if moves else (res, [])
        return [back(a) for a in res], landed

    out = {}
    late_keys = late + late_sharded + [("loss", None)]
    out["w_o"], (late_recv,) = update("w_o", [r_wo], moves=[gather(pack(late_keys))])
    out["w_down"], _ = update("w_down", [r_down0, r_down1])
    out["w_up"], _ = update("w_up", [r_up0, r_up1])
    out["w_out_even"], _ = update("w_out_even", [r_wout])
    out["w_qkv"], _ = update("w_qkv", [r_wqkv], transposed=True)
    out["w_in_even"], _ = update("w_in_even", list(r_w_in))

    def unpacked(recv, keys):
        flat, res, o = recv.reshape(N_DEV, -1), {}, 0
        for key in keys:
            res[key] = flat[:, o:o + small_grads[key].size]
            o += small_grads[key].size
        return res

    arrived = {**unpacked(early_recv, early + early_sharded), **unpacked(late_recv, late_keys)}
    piece = lambda tree, key: tree[key[0]] if key[1] is None else tree[key[0]][key[1]]

    def rows_by_device(cat):
        pad = (-cat.shape[1]) % PACK_W
        return jnp.pad(cat, ((0, 0), (0, pad))).reshape(N_DEV, -1, PACK_W)

    rep_keys = early + late
    rep_parts = rows_by_device(jnp.concatenate([arrived[key] for key in rep_keys], axis=1))
    flat_rep = lambda tree: _as_rows(jnp.concatenate([piece(tree, key).reshape(-1) for key in rep_keys]))[None]
    rep_res = _adamw([rep_parts], flat_rep(W), flat_rep(M), flat_rep(V), name="adamw_replicated")
    sh_keys = early_sharded + late_sharded
    shard_parts = []
    for n in sh_keys:
        full = arrived[n].reshape((N_DEV,) + small_grads[n].shape)
        c = full.shape[-1] // N_DEV
        shard_parts.append(lax.dynamic_slice_in_dim(full, me * c, c, axis=full.ndim - 1).reshape(N_DEV, -1))
    sh_rows = rows_by_device(jnp.concatenate(shard_parts, axis=1))
    flat_sh = lambda tree: _as_rows(jnp.concatenate([tree[n].reshape(-1) for n in sh_keys]))[None]
    sh_res = _adamw([sh_rows], flat_sh(W), flat_sh(M), flat_sh(V), name="adamw_small_sharded")

    def unpack_replicated(rows):
        flat, vals, o = rows.reshape(-1), {}, 0
        for key in rep_keys:
            size = piece(W, key).size
            vals[key] = flat[o:o + size]
            o += size
        res = {}
        for n in replicated:
            if (n, None) in vals:
                res[n] = vals[(n, None)].reshape(W[n].shape)
            else:
                res[n] = jnp.stack([vals[(n, r)] for r in range(W[n].shape[0])]).reshape(W[n].shape)
        return res

    def unpack_sharded(rows):
        flat, res, o = rows.reshape(-1), {}, 0
        for n in sh_keys:
            res[n] = flat[o:o + W[n].size].reshape(W[n].shape)
            o += W[n].size
        return res

    results = []
    for idx in range(4):
        d = {n: out[n][idx] for n in big}
        d.update(unpack_replicated(rep_res[idx]))
        d.update(unpack_sharded(sh_res[idx]))
        results.append(d)

    loss = jnp.sum(arrived[("loss", None)])
    grad_x = dx.reshape(x.shape)
    final = [loss, grad_x]
    for d in results:
        final.extend(d[n] for n in names)
    return tuple(final)
```

```python
import dataclasses
import functools

import jax
import jax.numpy as jnp
from jax import lax
from jax.experimental import pallas as pl
from jax.experimental.pallas import tpu as pltpu

F32 = jnp.float32
BF16 = jnp.bfloat16

N_DEV = 8
D_MODEL = 1024
D_FF = 4096
RMS_EPS = 1e-5
LN_EPS = 1e-5
CHUNK = 128
GM_GROUPS = 8
SSM_HEADS = 16
SSM_HEADDIM = 64
SSM_GROUPS = 4
SSM_STATE = 128
SSM_CONV = 4
CONV_DIM = 2048
IN_EVEN = 5136
REST_W = 3200
ATTN_HEADS = 16
ATTN_KV = 2
HEAD_DIM = 64
QKV_DIM = 1280
LANES = 128
HALO = 8
PACK_W = 1024

ADAM_LR = 0.001
ADAM_B1 = 0.9
ADAM_B2 = 0.999
ADAM_EPS = 1e-08
ADAM_WD = 0.01
ADAM_STEP = 10

VMEM_LIMIT_BYTES = 56 * 1024 * 1024


_NN = (((1,), (0,)), ((), ()))
_NT = (((1,), (1,)), ((), ()))
_TN = (((0,), (0,)), ((), ()))


def _dg(a, b, dims):
    return lax.dot_general(a.astype(BF16), b.astype(BF16), dims, preferred_element_type=F32)


@jax.custom_vjp
def _nn(a, b):
    return _dg(a, b, _NN)


@jax.custom_vjp
def _nt(a, b):
    return _dg(a, b, _NT)


@jax.custom_vjp
def _tn(a, b):
    return _dg(a, b, _TN)


_nn.defvjp(lambda a, b: (_dg(a, b, _NN), (a, b)), lambda r, g: (_nt(g, r[1]), _tn(r[0], g)))
_nt.defvjp(lambda a, b: (_dg(a, b, _NT), (a, b)), lambda r, g: (_nn(g, r[1]), _tn(g, r[0])))
_tn.defvjp(lambda a, b: (_dg(a, b, _TN), (a, b)), lambda r, g: (_nt(r[1], g), _nn(r[0], g)))


def _split3_dot(tri, x):
    x1 = x.astype(BF16)
    r1 = x - x1.astype(F32)
    x2 = r1.astype(BF16)
    x3 = (r1 - x2.astype(F32)).astype(BF16)
    t = tri.astype(BF16)
    dot = lambda p: lax.dot_general(t, p, _NN, preferred_element_type=F32)
    return dot(x1) + dot(x2) + dot(x3)


def _tri(lower):
    r = lax.broadcasted_iota(jnp.int32, (CHUNK, CHUNK), 0)
    c = lax.broadcasted_iota(jnp.int32, (CHUNK, CHUNK), 1)
    return jnp.where((r >= c) if lower else (r <= c), 1.0, 0.0).astype(F32)


@jax.custom_vjp
def _cumsum_rows(x):
    return _split3_dot(_tri(True), x)


_cumsum_rows.defvjp(lambda x: (_split3_dot(_tri(True), x), None), lambda _, g: (_split3_dot(_tri(False), g),))


def _sigmoid(x):
    return 1.0 / (1.0 + jnp.exp(-x))


def _silu(x):
    return x * _sigmoid(x)


def _softplus(x):
    return jnp.maximum(x, 0.0) + jnp.log(1.0 + jnp.exp(-jnp.abs(x)))


def _gelu_tanh(x):
    return 0.5 * x * (1.0 + jnp.tanh(0.7978845608028654 * (x + 0.044715 * (x * x * x))))


def _rmsnorm(x, g):
    return x * lax.rsqrt(jnp.mean(x * x, axis=-1, keepdims=True) + RMS_EPS) * g


def _gmlp_chunk(u, v, ln_g, ln_b, w_s, b_s):
    gu = _gelu_tanh(u)
    gv = _gelu_tanh(v)
    mu = jnp.mean(gv, axis=-1, keepdims=True)
    var = jnp.mean(jnp.square(gv - mu), axis=-1, keepdims=True)
    vn = (gv - mu) * lax.rsqrt(var + LN_EPS) * ln_g + ln_b
    r = lax.broadcasted_iota(jnp.int32, (CHUNK, CHUNK), 0)
    c = lax.broadcasted_iota(jnp.int32, (CHUNK, CHUNK), 1)
    causal = r >= c
    outs = []
    for g in range(GM_GROUPS):
        cols = slice(g * LANES, (g + 1) * LANES)
        mixed = _nn(jnp.where(causal, w_s[g], 0.0), vn[:, cols]) + b_s[g]
        outs.append(gu[:, cols] * mixed)
    return jnp.concatenate(outs, axis=1)


def _lane_pick(row, h):
    lane = lax.broadcasted_iota(jnp.int32, row.shape, 1)
    return jnp.sum(jnp.where(lane == h, row, 0.0), axis=1, keepdims=True)


def _col_pick(m, h):
    lane = lax.broadcasted_iota(jnp.int32, m.shape, 1)
    return jnp.sum(jnp.where(lane == h, m, 0.0), axis=1, keepdims=True)


def _row_pick(m, h):
    sub = lax.broadcasted_iota(jnp.int32, m.shape, 0)
    return jnp.sum(jnp.where(sub == h, m, 0.0), axis=0, keepdims=True)


_PAIRS = SSM_HEADS // 2


def _ssd_chunk(pre, z, dt_raw, h_prev, dt_bias, a_log, d_skip, norm_g):
    xbc = _silu(pre)
    dt = _softplus(dt_raw + dt_bias)
    da = dt * (-jnp.exp(a_log))
    a_cum = _cumsum_rows(da)
    a_cum_t = a_cum.T
    dt_t = dt.T
    r = lax.broadcasted_iota(jnp.int32, (CHUNK, CHUNK), 0)
    c = lax.broadcasted_iota(jnp.int32, (CHUNK, CHUNK), 1)
    causal = r >= c
    lane_lo = lax.broadcasted_iota(jnp.int32, (1, LANES), 1) < SSM_HEADDIM
    last_row = lax.broadcasted_iota(jnp.int32, (CHUNK, 1), 0) == CHUNK - 1
    ys, h_next = [], []
    for j in range(_PAIRS):
        g = j // 2
        xs = xbc[:, j * LANES:(j + 1) * LANES]
        bm = xbc[:, 1024 + g * SSM_STATE:1024 + (g + 1) * SSM_STATE]
        cm = xbc[:, 1536 + g * SSM_STATE:1536 + (g + 1) * SSM_STATE]
        cb = _nt(cm, bm)
        y_diag, to_end, e_cum, c_dec, d_row = [], [], [], [], []
        for h in (2 * j, 2 * j + 1):
            col = _col_pick(a_cum, h)
            row = _row_pick(a_cum_t, h)
            dt_col = _col_pick(dt, h)
            dt_row = _row_pick(dt_t, h)
            decay = jnp.exp(jnp.where(causal, col - row, -jnp.inf))
            y_diag.append(_nn(cb * decay * dt_row, xs))
            last = jnp.sum(jnp.where(last_row, col, 0.0), axis=0, keepdims=True)
            to_end.append(jnp.exp(last - col) * dt_col)
            e_cum.append(jnp.exp(col))
            c_dec.append(jnp.exp(last))
            d_row.append(_lane_pick(d_skip, h))
        pair = lambda lo_hi: jnp.where(lane_lo, lo_hi[0], lo_hi[1])
        states = _tn(bm, xs * pair(to_end))
        y_off = _nn(cm, h_prev[j]) * pair(e_cum)
        ys.append(pair(y_diag) + y_off + xs * pair(d_row))
        h_next.append(pair(c_dec) * h_prev[j] + states)
    y = jnp.concatenate(ys, axis=1) * _silu(z)
    width = D_MODEL // SSM_GROUPS
    y = jnp.concatenate(
        [_rmsnorm(y[:, g * width:(g + 1) * width], norm_g[:, g * width:(g + 1) * width]) for g in range(SSM_GROUPS)],
        axis=1)
    return y, tuple(h_next)


def _shift_down(prev8, x, k):
    if k == 0:
        return x
    win = jnp.concatenate([prev8, x], axis=0)
    return pltpu.roll(win, k, 0)[HALO:]


def _shift_up(x, next8, k):
    if k == 0:
        return x
    n = x.shape[0]
    win = jnp.concatenate([x, next8], axis=0)
    return pltpu.roll(win, n + HALO - k, 0)[:n]


def _conv_pre(prev8, x, w, b):
    out = b + x * w[SSM_CONV - 1:SSM_CONV]
    for i in range(SSM_CONV - 1):
        out = out + _shift_down(prev8, x, SSM_CONV - 1 - i) * w[i:i + 1]
    return out


def _swap_halves(x):
    return pltpu.roll(x, HEAD_DIM, 1)


_PAIRS_PER_KV = ATTN_HEADS // ATTN_KV // 2
_ATTN_SCALE = HEAD_DIM ** -0.5


def _parity_lanes(parity):
    lane = lax.broadcasted_iota(jnp.int32, (1, LANES), 1)
    return (lane >= HEAD_DIM * parity) & (lane < HEAD_DIM * (parity + 1))


def _kv_placed(pair, kv_head):
    mine = jnp.where(_parity_lanes(kv_head), pair, 0.0)
    lo = mine if kv_head == 0 else _swap_halves(mine)
    return lo, _swap_halves(lo)


def _kv_unplaced(d_lo, d_hi, kv_head):
    d = jnp.where(_parity_lanes(0), d_lo, 0.0) + _swap_halves(jnp.where(_parity_lanes(1), d_hi, 0.0))
    return d if kv_head == 0 else _swap_halves(d)


def _attn_probs(q4, k_e, sink, first):
    s = _dg(q4, k_e, _NT) * _ATTN_SCALE
    rows = lax.broadcasted_iota(jnp.int32, s.shape, 0) & (CHUNK - 1)
    cols = lax.broadcasted_iota(jnp.int32, s.shape, 1)
    valid = (cols <= rows + CHUNK) & (cols > rows) & (cols >= CHUNK * first.astype(jnp.int32))
    s = jnp.where(valid, s, -jnp.inf)
    m = jnp.maximum(jnp.max(s, axis=-1, keepdims=True), sink)
    p = jnp.exp(s - m)
    e_sink = jnp.exp(sink - m)
    return p, e_sink, jnp.sum(p, axis=-1, keepdims=True) + e_sink


def _lane_column(col, idx):
    lane = lax.broadcasted_iota(jnp.int32, (1, LANES), 1)
    return jnp.where(lane == idx, col, 0.0)


N_CHIP = 4
N_CORE = 2
_OTHER_CHIPS = (2, 4, 6)


@dataclasses.dataclass
class _Move:
    kind: str
    src: jax.Array

    def dst_shape(self):
        s = self.src.shape
        shape = {"gather": (N_DEV,) + s, "gather_ici": (N_CHIP, N_CORE) + s, "gather_d2d": s,
                 "scatter_d2d": (N_CHIP,) + s[2:], "scatter_ici": s}[self.kind]
        return jax.ShapeDtypeStruct(tuple(shape), self.src.dtype)


def _peer(x, y, c, k):
    return (1 - x if k & 4 else x, 1 - y if k & 2 else y, 1 - c if k & 1 else c)


def _move_copies(moves, srcs, dsts, send_sems, recv_sems, local_sems):
    x, y, c = lax.axis_index("x"), lax.axis_index("y"), lax.axis_index("c")
    chip = 2 * x + y
    me = 2 * chip + c
    sibling = (x, y, 1 - c)
    all_chips = pl.ds(0, N_CHIP)
    local, remote = [], []

    def push(n, k, src, dst, device):
        remote.append(pltpu.make_async_remote_copy(
            src_ref=src, dst_ref=dst, send_sem=send_sems.at[n, k], recv_sem=recv_sems.at[n, k],
            device_id=device, device_id_type=pl.DeviceIdType.MESH))

    for n, mv in enumerate(moves):
        s, d = srcs[n], dsts[n]
        if mv.kind == "gather":
            local.append(pltpu.make_async_copy(s, d.at[me], local_sems.at[n]))
            for k in range(1, N_DEV):
                push(n, k - 1, s, d.at[me], _peer(x, y, c, k))
        elif mv.kind == "gather_ici":
            local.append(pltpu.make_async_copy(s, d.at[chip, c], local_sems.at[n]))
            for k in _OTHER_CHIPS:
                push(n, k - 1, s, d.at[chip, c], _peer(x, y, c, k))
        elif mv.kind == "gather_d2d":
            push(n, 0, d.at[all_chips, c], d.at[all_chips, c], sibling)
        elif mv.kind == "scatter_d2d":
            push(n, 0, s.at[all_chips, 1 - c], d, sibling)
        else:
            assert mv.kind == "scatter_ici", mv.kind
            local.append(pltpu.make_async_copy(s.at[chip], d.at[chip], local_sems.at[n]))
            for k in _OTHER_CHIPS:
                px, py, _ = _peer(x, y, c, k)
                push(n, k - 1, s.at[2 * px + py], d.at[chip], (px, py, c))
    return local, remote


def _move_aliases(moves, n_in, n_out):
    return {n_in + n: n_out + n for n, mv in enumerate(moves) if mv.kind == "gather_d2d"}


def _pcall(body, *, name, grid, in_specs, out_specs, out_shape, scratch_shapes=(), semantics=(), moves=(),
           aliases=None):
    out_shape, out_specs = list(out_shape), list(out_specs)
    in_specs = list(in_specs)
    if not moves:
        call = pl.pallas_call(
            body, name=name, grid=grid, in_specs=in_specs, out_specs=out_specs, out_shape=out_shape,
            scratch_shapes=list(scratch_shapes), input_output_aliases=aliases or {},
            compiler_params=pltpu.CompilerParams(dimension_semantics=tuple(semantics),
                                                 vmem_limit_bytes=VMEM_LIMIT_BYTES))
        return (lambda *args: (list(call(*args)), []))
    n_in, n_out, n_scr, n_mv = len(in_specs), len(out_shape), len(scratch_shapes), len(moves)
    hbm = pl.BlockSpec(memory_space=pltpu.HBM)

    def carrier(*refs):
        ins, rest = refs[:n_in], refs[n_in:]
        srcs, rest = rest[:n_mv], rest[n_mv:]
        outs, rest = rest[:n_out], rest[n_out:]
        dsts, rest = rest[:n_mv], rest[n_mv:]
        scr, (send_sems, recv_sems, local_sems) = rest[:n_scr], rest[n_scr:]
        first = functools.reduce(jnp.logical_and, [pl.program_id(d) == 0 for d in range(len(grid))])
        last = functools.reduce(jnp.logical_and, [pl.program_id(d) == grid[d] - 1 for d in range(len(grid))])

        @pl.when(first)
        def _():
            local, remote = _move_copies(moves, srcs, dsts, send_sems, recv_sems, local_sems)
            for cp in local + remote:
                cp.start()

        body(*ins, *outs, *scr)

        @pl.when(last)
        def _():
            local, remote = _move_copies(moves, srcs, dsts, send_sems, recv_sems, local_sems)
            for cp in remote + local:
                cp.wait()

    call = pl.pallas_call(
        carrier, name=name, grid=grid,
        in_specs=in_specs + [hbm] * n_mv,
        out_specs=out_specs + [hbm] * n_mv,
        out_shape=out_shape + [mv.dst_shape() for mv in moves],
        scratch_shapes=list(scratch_shapes) + [pltpu.SemaphoreType.DMA((n_mv, N_DEV - 1)),
                                               pltpu.SemaphoreType.DMA((n_mv, N_DEV - 1)),
                                               pltpu.SemaphoreType.DMA((n_mv,))],
        input_output_aliases={**(aliases or {}), **_move_aliases(moves, n_in, n_out)},
        compiler_params=pltpu.CompilerParams(dimension_semantics=("arbitrary",) * len(grid),
                                             vmem_limit_bytes=VMEM_LIMIT_BYTES))

    def run(*args):
        res = list(call(*args, *[mv.src for mv in moves]))
        return res[:n_out], res[n_out:]

    return run


def _exchange(moves, *, name, then_d2d=()):
    n_mv, n_fwd = len(moves), len(then_d2d)
    hbm = pl.BlockSpec(memory_space=pltpu.HBM)
    copies_of = {"gather": N_DEV - 1, "gather_ici": len(_OTHER_CHIPS), "gather_d2d": 1, "scatter_d2d": 1,
                 "scatter_ici": len(_OTHER_CHIPS)}
    first_copy = [sum(copies_of[mv.kind] for mv in moves[:n]) for n in range(n_mv)]

    def body(*refs):
        srcs, dsts, sems = refs[:n_mv], refs[n_mv:2 * n_mv], refs[2 * n_mv:]
        local, remote = _move_copies(moves, srcs, dsts, *sems[:3])
        for cp in local + remote:
            cp.start()
        x, y, c = lax.axis_index("x"), lax.axis_index("y"), lax.axis_index("c")
        chip, sibling = 2 * x + y, (x, y, 1 - c)
        passed, passed_on = [], set()

        def to_sibling(f, k, src, slot):
            cp = pltpu.make_async_remote_copy(src_ref=src, dst_ref=slot, send_sem=sems[3].at[f, k],
                                              recv_sem=sems[4].at[f, k], device_id=sibling,
                                              device_id_type=pl.DeviceIdType.MESH)
            cp.start()
            passed.append(cp)

        for f, n in enumerate(then_d2d):
            assert moves[n].kind == "gather_ici"
            d = dsts[n]
            to_sibling(f, 0, srcs[n], d.at[chip, c])
            for i, k in enumerate(_OTHER_CHIPS):
                remote[first_copy[n] + i].wait_recv()
                passed_on.add(first_copy[n] + i)
                px, py, _ = _peer(x, y, c, k)
                to_sibling(f, k - 1, d.at[2 * px + py, c], d.at[2 * px + py, c])
        for i, cp in enumerate(remote):
            if i in passed_on:
                cp.wait_send()
            else:
                cp.wait()
        for cp in local + passed:
            cp.wait()

    sems = [pltpu.SemaphoreType.DMA((n_mv, N_DEV - 1)), pltpu.SemaphoreType.DMA((n_mv, N_DEV - 1)),
            pltpu.SemaphoreType.DMA((n_mv,))]
    if then_d2d:
        sems += [pltpu.SemaphoreType.DMA((n_fwd, N_DEV - 1)), pltpu.SemaphoreType.DMA((n_fwd, N_DEV - 1))]
    return list(pl.pallas_call(
        body, name=name, in_specs=[hbm] * n_mv, out_specs=[hbm] * n_mv,
        out_shape=[mv.dst_shape() for mv in moves], scratch_shapes=sems,
    )(*[mv.src for mv in moves]))


TM = 512
FF_SHARD = D_FF // N_DEV


def _whole(a):
    nd = a.ndim
    return pl.BlockSpec(a.shape, lambda i: (0,) * nd)


def _rows(width, col=0):
    return pl.BlockSpec((TM, width), lambda i: (i, col))


def _acc_row(width):
    return pl.BlockSpec((1, width), lambda i: (0, 0))


def _unpack(res_landed, moves, n_out):
    res, landed = res_landed
    res = res[0] if n_out == 1 else res
    return (res, landed) if moves else res


def _norm_matmul(x, g, w, *, name, emit_y, moves=()):
    t, d = x.shape
    n = w.shape[1]

    def body(x_ref, g_ref, w_ref, *outs):
        y = _rmsnorm(x_ref[...], g_ref[...]).astype(BF16)
        if emit_y:
            outs[0][...] = y
        outs[-1][...] = lax.dot_general(y, w_ref[...], _NN, preferred_element_type=F32)

    shapes = ([jax.ShapeDtypeStruct((t, d), BF16)] if emit_y else []) + [jax.ShapeDtypeStruct((t, n), F32)]
    specs = ([_rows(d)] if emit_y else []) + [_rows(n)]
    return _unpack(_pcall(body, name=name, grid=(t // TM,), in_specs=[_rows(d), _acc_row(d), _whole(w)],
                          out_specs=specs, out_shape=shapes, semantics=("parallel",), moves=moves)(x, g, w),
                   moves, len(shapes))


def _residual_matmul(a, w, res, *, name, bias=None, norm_g=None, w_transposed=False, moves=()):
    t, k = a.shape
    n = w.shape[0 if w_transposed else 1]
    contract = _NT if w_transposed else _NN
    has_res, has_bias, has_norm = res is not None, bias is not None, norm_g is not None

    def body(a_ref, w_ref, *rest):
        rest = list(rest)
        res_ref = rest.pop(0) if has_res else None
        b_ref = rest.pop(0) if has_bias else None
        g_ref = rest.pop(0) if has_norm else None
        h = lax.dot_general(a_ref[...].astype(BF16), w_ref[...], contract, preferred_element_type=F32)
        if has_res:
            h = h + res_ref[...]
        if has_bias:
            h = h + b_ref[...]
        rest[0][...] = h
        if has_norm:
            rest[1][...] = _rmsnorm(h, g_ref[...]).astype(BF16)

    rows_in = [res] if has_res else []
    extra = ([bias] if has_bias else []) + ([norm_g] if has_norm else [])
    shapes = [jax.ShapeDtypeStruct((t, n), F32)] + ([jax.ShapeDtypeStruct((t, n), BF16)] if has_norm else [])
    return _unpack(_pcall(body, name=name, grid=(t // TM,),
                          in_specs=[_rows(k), _whole(w)] + [_rows(n)] * len(rows_in) + [_acc_row(n)] * len(extra),
                          out_specs=[_rows(n)] * len(shapes), out_shape=shapes, semantics=("parallel",),
                          moves=moves)(a, w, *rows_in, *extra), moves, len(shapes))


def _mlp_up(y, w_cols, *, name, moves=()):
    t, d = y.shape

    def body(y_ref, w_ref, up_ref):
        yv = y_ref[...]
        for j in range(N_DEV):
            up_ref[:, j * FF_SHARD:(j + 1) * FF_SHARD] = lax.dot_general(
                yv, w_ref[j], _NN, preferred_element_type=F32).astype(up_ref.dtype)

    return _unpack(_pcall(body, name=name, grid=(t // TM,), in_specs=[_rows(d), _whole(w_cols)],
                          out_specs=[_rows(D_FF)], out_shape=[jax.ShapeDtypeStruct((t, D_FF), BF16)],
                          semantics=("parallel",), moves=moves)(y, w_cols), moves, 1)


def _sq_relu(u):
    return jnp.square(jnp.maximum(u.astype(F32), 0.0))


def _down_blocks(w_refs):
    for j in range(N_DEV):
        off = j * FF_SHARD
        for w_ref in w_refs:
            yield off, w_ref.shape[1], w_ref[j]
            off += w_ref.shape[1]


def _mlp_down(up, w_rows, res, *, name, norm_g=None, moves=()):
    t = up.shape[0]
    has_norm = norm_g is not None
    n_w = len(w_rows)

    def body(up_ref, *rest):
        w_refs, res_ref, rest = rest[:n_w], rest[n_w], rest[n_w + 1:]
        h = res_ref[...]
        for off, rows, w_blk in _down_blocks(w_refs):
            act = _sq_relu(up_ref[:, off:off + rows]).astype(BF16)
            h = h + lax.dot_general(act, w_blk, _NN, preferred_element_type=F32)
        if has_norm:
            g_ref, h_ref, y_ref = rest
            y_ref[...] = _rmsnorm(h, g_ref[...]).astype(BF16)
        else:
            (h_ref,) = rest
        h_ref[...] = h

    shapes = [jax.ShapeDtypeStruct((t, D_MODEL), F32)] + ([jax.ShapeDtypeStruct((t, D_MODEL), BF16)] if has_norm else [])
    return _unpack(_pcall(body, name=name, grid=(t // TM,),
                          in_specs=[_rows(D_FF)] + [_whole(w) for w in w_rows] + [_rows(D_MODEL)]
                          + ([_acc_row(D_MODEL)] if has_norm else []),
                          out_specs=[_rows(D_MODEL)] * len(shapes), out_shape=shapes, semantics=("parallel",),
                          moves=moves)(up, *w_rows, res, *([norm_g] if has_norm else [])), moves, len(shapes))


def _mlp_down_dx(dh, w_rows, up, *, name, moves=()):
    t = up.shape[0]
    n_w = len(w_rows)

    def body(dh_ref, *rest):
        w_refs, (up_ref, o_ref) = rest[:n_w], rest[n_w:]
        dhv = dh_ref[...]
        for off, rows, w_blk in _down_blocks(w_refs):
            cols = slice(off, off + rows)
            d_act = lax.dot_general(dhv, w_blk, _NT, preferred_element_type=F32)
            o_ref[:, cols] = (d_act * (2.0 * jnp.maximum(up_ref[:, cols].astype(F32), 0.0))).astype(o_ref.dtype)

    return _unpack(_pcall(body, name=name, grid=(t // TM,),
                          in_specs=[_rows(D_MODEL)] + [_whole(w) for w in w_rows] + [_rows(D_FF)],
                          out_specs=[_rows(D_FF)], out_shape=[jax.ShapeDtypeStruct((t, D_FF), BF16)],
                          semantics=("parallel",), moves=moves)(dh, *w_rows, up), moves, 1)


def _dw_by_cols(x, dy, *, name, tn, by_device=False, moves=()):
    t, k = x.shape
    n = dy.shape[1]
    assert n % tn == 0, (name, n, tn)

    def body(x_ref, dy_ref, o_ref):
        o_ref[...] = lax.dot_general(x_ref[...].astype(BF16), dy_ref[...].astype(BF16), _TN,
                                     preferred_element_type=F32).astype(o_ref.dtype)

    if by_device:
        out_spec, out_shape = pl.BlockSpec((None, k, tn), lambda j: (j, 0, 0)), (n // tn, k, tn)
    else:
        out_spec, out_shape = pl.BlockSpec((k, tn), lambda j: (0, j)), (k, n)
    return _unpack(_pcall(body, name=name, grid=(n // tn,),
                          in_specs=[_whole(x), pl.BlockSpec((t, tn), lambda j: (0, j))],
                          out_specs=[out_spec], out_shape=[jax.ShapeDtypeStruct(out_shape, BF16)],
                          semantics=("parallel",), moves=moves)(x, dy), moves, 1)


def _dw_by_rows(x, dy, *, name, tk, square_relu=False, column_sums=False, moves=()):
    t, k = x.shape
    n = dy.shape[1]
    assert k % tk == 0, (name, k, tk)

    def body(x_ref, dy_ref, o_ref, *sums):
        xv = _sq_relu(x_ref[...]) if square_relu else x_ref[...]
        o_ref[...] = lax.dot_general(xv.astype(BF16), dy_ref[...].astype(BF16), _TN,
                                     preferred_element_type=F32).astype(o_ref.dtype)
        if column_sums:
            sums[0][...] = jnp.sum(xv.astype(F32), axis=0, keepdims=True)

    shapes = [jax.ShapeDtypeStruct((k, n), BF16)] + ([jax.ShapeDtypeStruct((1, k), F32)] if column_sums else [])
    specs = [pl.BlockSpec((tk, n), lambda j: (j, 0))] + ([pl.BlockSpec((1, tk), lambda j: (0, j))] if column_sums else [])
    return _unpack(_pcall(body, name=name, grid=(k // tk,),
                          in_specs=[pl.BlockSpec((t, tk), lambda j: (0, j)), _whole(dy)],
                          out_specs=specs, out_shape=shapes,
                          semantics=("parallel",), moves=moves)(x, dy), moves, len(shapes))


def _dx(dy, w, *, name, partial=None, moves=()):
    t, k = dy.shape
    n = w.shape[0]
    has_partial = partial is not None

    def body(dy_ref, w_ref, *rest):
        out = lax.dot_general(dy_ref[...].astype(BF16), w_ref[...], _NT, preferred_element_type=F32)
        if has_partial:
            out = out + rest[0][...]
        rest[-1][...] = out

    return _unpack(_pcall(body, name=name, grid=(t // TM,),
                          in_specs=[_rows(k), _whole(w)] + ([_rows(n)] if has_partial else []),
                          out_specs=[_rows(n)], out_shape=[jax.ShapeDtypeStruct((t, n), F32)],
                          semantics=("parallel",), moves=moves)(dy, w, *([partial] if has_partial else [])),
                   moves, 1)


def _dx_norm(dy, w, h, g, dres, *, name, partial=None, by_device_cols=False, w_transposed=False, moves=()):
    t, k = dy.shape
    d = h.shape[1]
    has_partial = partial is not None

    def body(dy_ref, w_ref, h_ref, g_ref, dres_ref, *rest):
        if by_device_cols:
            kc = k // N_DEV
            d_y = jnp.zeros((TM, d), F32)
            for j in range(N_DEV):
                d_y = d_y + lax.dot_general(dy_ref[:, j * kc:(j + 1) * kc].astype(BF16), w_ref[j], _NT,
                                            preferred_element_type=F32)
        else:
            d_y = lax.dot_general(dy_ref[...].astype(BF16), w_ref[...], _NN if w_transposed else _NT,
                                  preferred_element_type=F32)
        if has_partial:
            d_y = d_y + rest[0][...]
        dh_ref, dhb_ref, dg_ref, cs_ref = rest[-4:]
        _, vjp = jax.vjp(_rmsnorm, h_ref[...], g_ref[...])
        dh, dg = vjp(d_y)
        dh = dh + dres_ref[...]
        dh_ref[...] = dh
        dhb_ref[...] = dh.astype(BF16)

        @pl.when(pl.program_id(0) == 0)
        def _():
            dg_ref[...] = jnp.zeros_like(dg_ref)
            cs_ref[...] = jnp.zeros_like(cs_ref)

        dg_ref[...] += dg
        cs_ref[...] += jnp.sum(dh, axis=0, keepdims=True)

    shapes = [jax.ShapeDtypeStruct((t, d), F32), jax.ShapeDtypeStruct((t, d), BF16),
              jax.ShapeDtypeStruct((1, d), F32), jax.ShapeDtypeStruct((1, d), F32)]
    return _unpack(_pcall(body, name=name, grid=(t // TM,),
                          in_specs=[_rows(k), _whole(w), _rows(d), _acc_row(d), _rows(d)]
                          + ([_rows(d)] if has_partial else []),
                          out_specs=[_rows(d), _rows(d), _acc_row(d), _acc_row(d)], out_shape=shapes,
                          semantics=("arbitrary",), moves=moves)(dy, w, h, g, dres, *([partial] if has_partial else [])),
                   moves, 4)


def _pair_add(by_core, theirs, core, *, name, tb=512):
    n_chip, _, r, c = by_core.shape
    tb = min(tb, r)
    assert r % tb == 0, (name, r, tb)

    def body(core_ref, a_ref, b_ref, o_ref):
        del core_ref
        o_ref[...] = (a_ref[...].astype(F32) + b_ref[...].astype(F32)).astype(o_ref.dtype)

    blk = pl.BlockSpec((None, tb, c), lambda ch, i, core_ref: (ch, i, 0))
    return pl.pallas_call(
        body, name=name,
        grid_spec=pltpu.PrefetchScalarGridSpec(
            num_scalar_prefetch=1, grid=(n_chip, r // tb),
            in_specs=[pl.BlockSpec((None, None, tb, c), lambda ch, i, core_ref: (ch, core_ref[0], i, 0)), blk],
            out_specs=blk),
        out_shape=jax.ShapeDtypeStruct((n_chip, r, c), by_core.dtype),
        compiler_params=pltpu.CompilerParams(dimension_semantics=("parallel", "parallel"),
                                             vmem_limit_bytes=VMEM_LIMIT_BYTES),
    )(core, by_core, theirs)


def _mlp_down_loss(up, w_rows, res, g, target, *, name):
    t, d = res.shape
    n_w = len(w_rows)

    def body(up_ref, *rest):
        w_refs, (res_ref, g_ref, tgt_ref, loss_ref, dh_ref, dhb_ref, dg_ref) = rest[:n_w], rest[n_w:]
        h = res_ref[...]
        for off, rows, w_blk in _down_blocks(w_refs):
            act = _sq_relu(up_ref[:, off:off + rows]).astype(BF16)
            h = h + lax.dot_general(act, w_blk, _NN, preferred_element_type=F32)

        def f(hh, gg):
            err = jnp.square(_rmsnorm(hh, gg) - tgt_ref[...])
            return 0.5 * jnp.sum(jnp.mean(err, axis=-1, keepdims=True), axis=0, keepdims=True)

        val, vjp = jax.vjp(f, h, g_ref[...])
        dh, dg = vjp(jnp.ones((1, 1), F32))
        dh_ref[...] = dh
        dhb_ref[...] = dh.astype(BF16)

        @pl.when(pl.program_id(0) == 0)
        def _():
            loss_ref[...] = jnp.zeros_like(loss_ref)
            dg_ref[...] = jnp.zeros_like(dg_ref)

        loss_ref[...] += val
        dg_ref[...] += dg

    return _pcall(
        body, name=name, grid=(t // TM,),
        in_specs=[_rows(D_FF)] + [_whole(w) for w in w_rows] + [_rows(d), _acc_row(d), _rows(d)],
        out_specs=[pl.BlockSpec((8, LANES), lambda i: (0, 0)), _rows(d), _rows(d), _acc_row(d)],
        out_shape=[jax.ShapeDtypeStruct((8, LANES), F32), jax.ShapeDtypeStruct((t, d), F32),
                   jax.ShapeDtypeStruct((t, d), BF16), jax.ShapeDtypeStruct((1, d), F32)],
        semantics=("arbitrary",),
    )(up, *w_rows, res, g, target)[0]


def _gmlp_fwd(proj_uv, ln_g, ln_b, w_s, b_s, *, name, moves=()):
    t = proj_uv.shape[0]
    w = D_MODEL

    def body(u_ref, v_ref, g_ref, b_ref, w_ref, bs_ref, o_ref):
        o_ref[...] = _gmlp_chunk(u_ref[...], v_ref[...], g_ref[...], b_ref[...], w_ref[...],
                                 bs_ref[...]).astype(o_ref.dtype)

    row = pl.BlockSpec((1, w), lambda i: (0, 0))
    res, landed = _pcall(
        body, name=name, grid=(t // CHUNK,),
        in_specs=[pl.BlockSpec((CHUNK, w), lambda i: (i, 0)), pl.BlockSpec((CHUNK, w), lambda i: (i, 1)), row, row,
                  pl.BlockSpec((GM_GROUPS, CHUNK, CHUNK), lambda i: (0, 0, 0)),
                  pl.BlockSpec((GM_GROUPS, CHUNK, 1), lambda i: (0, 0, 0))],
        out_specs=[pl.BlockSpec((CHUNK, w), lambda i: (i, 0))],
        out_shape=[jax.ShapeDtypeStruct((t, 2 * w), BF16)],
        semantics=("parallel",), moves=moves,
    )(proj_uv, proj_uv, ln_g, ln_b, w_s, b_s)
    return (res[0], landed) if moves else res[0]


def _gmlp_bwd(proj_uv, d_mix, ln_g, ln_b, w_s, b_s, *, name, moves=()):
    t = proj_uv.shape[0]
    w = D_MODEL

    def body(u_ref, v_ref, da_ref, g_ref, b_ref, w_ref, bs_ref, duv_ref, dg_ref, db_ref, dw_ref, dbs_ref):
        _, vjp = jax.vjp(_gmlp_chunk, u_ref[...], v_ref[...], g_ref[...], b_ref[...], w_ref[...], bs_ref[...])
        du, dv, dg, db, dw, dbs = vjp(da_ref[...])
        duv_ref[:, :w] = du.astype(duv_ref.dtype)
        duv_ref[:, w:] = dv.astype(duv_ref.dtype)

        @pl.when(pl.program_id(0) == 0)
        def _():
            dg_ref[...] = jnp.zeros_like(dg_ref)
            db_ref[...] = jnp.zeros_like(db_ref)
            dw_ref[...] = jnp.zeros_like(dw_ref)
            dbs_ref[...] = jnp.zeros_like(dbs_ref)

        dg_ref[...] += dg
        db_ref[...] += db
        dw_ref[...] += dw
        dbs_ref[...] += dbs

    row = pl.BlockSpec((1, w), lambda i: (0, 0))
    ws = pl.BlockSpec((GM_GROUPS, CHUNK, CHUNK), lambda i: (0, 0, 0))
    bs = pl.BlockSpec((GM_GROUPS, CHUNK, 1), lambda i: (0, 0, 0))
    res, landed = _pcall(
        body, name=name, grid=(t // CHUNK,),
        in_specs=[pl.BlockSpec((CHUNK, w), lambda i: (i, 0)), pl.BlockSpec((CHUNK, w), lambda i: (i, 1)),
                  pl.BlockSpec((CHUNK, w), lambda i: (i, 0)), row, row, ws, bs],
        out_specs=[pl.BlockSpec((CHUNK, 2 * w), lambda i: (i, 0)), row, row, ws, bs],
        out_shape=[jax.ShapeDtypeStruct((t, 2 * w), BF16), jax.ShapeDtypeStruct((1, w), F32),
                   jax.ShapeDtypeStruct((1, w), F32), jax.ShapeDtypeStruct((GM_GROUPS, CHUNK, CHUNK), F32),
                   jax.ShapeDtypeStruct((GM_GROUPS, CHUNK, 1), F32)],
        semantics=("arbitrary",), moves=moves,
    )(proj_uv, proj_uv, d_mix, ln_g, ln_b, w_s, b_s)
    return (res, landed) if moves else res


_HALO_PER_CHUNK = CHUNK // HALO
_DT_BLOCK = (CONV_DIM + D_MODEL) // LANES


def _ssd_fwd(proj_rest, mix, conv_w, conv_b, dt_bias, a_log, d_skip, norm_g, *, name, moves=()):
    t = proj_rest.shape[0]
    nc = t // CHUNK

    def body(x_ref, prev_ref, z_ref, dt_ref, mix_ref, cw_ref, cb_ref, dtb_ref, al_ref, ds_ref, ng_ref, y_ref, hs_ref,
             pre_ref, h_scr):
        del mix_ref
        i = pl.program_id(0)

        @pl.when(i == 0)
        def _():
            h_scr[...] = jnp.zeros_like(h_scr)

        prev8 = jnp.where(i == 0, 0.0, prev_ref[...])
        pre = _conv_pre(prev8, x_ref[...], cw_ref[...], cb_ref[...])
        pre_ref[...] = pre
        hs_ref[0] = h_scr[...]
        h_prev = tuple(h_scr[j] for j in range(_PAIRS))
        y, h_next = _ssd_chunk(pre, z_ref[...], dt_ref[...], h_prev, dtb_ref[...], al_ref[...], ds_ref[...],
                               ng_ref[...])
        y_ref[...] = y.astype(y_ref.dtype)
        for j in range(_PAIRS):
            h_scr[j] = h_next[j]

    small = pl.BlockSpec((1, LANES), lambda i: (0, 0))
    res, landed = _pcall(
        body, name=name, grid=(nc,),
        in_specs=[pl.BlockSpec((CHUNK, CONV_DIM), lambda i: (i, 0)),
                  pl.BlockSpec((HALO, CONV_DIM), lambda i: (jnp.maximum(i * _HALO_PER_CHUNK - 1, 0), 0)),
                  pl.BlockSpec((CHUNK, D_MODEL), lambda i: (i, CONV_DIM // D_MODEL)),
                  pl.BlockSpec((CHUNK, LANES), lambda i: (i, _DT_BLOCK)),
                  pl.BlockSpec(memory_space=pl.ANY),
                  pl.BlockSpec((SSM_CONV, CONV_DIM), lambda i: (0, 0)),
                  pl.BlockSpec((1, CONV_DIM), lambda i: (0, 0)),
                  small, small, small, pl.BlockSpec((1, D_MODEL), lambda i: (0, 0))],
        out_specs=[pl.BlockSpec((CHUNK, D_MODEL), lambda i: (i, 1)),
                   pl.BlockSpec((1, _PAIRS, SSM_STATE, LANES), lambda i: (i, 0, 0, 0)),
                   pl.BlockSpec((CHUNK, CONV_DIM), lambda i: (i, 0))],
        out_shape=[jax.ShapeDtypeStruct((t, 2 * D_MODEL), BF16),
                   jax.ShapeDtypeStruct((nc, _PAIRS, SSM_STATE, LANES), F32),
                   jax.ShapeDtypeStruct((t, CONV_DIM), F32)],
        scratch_shapes=[pltpu.VMEM((_PAIRS, SSM_STATE, LANES), F32)],
        semantics=("arbitrary",), moves=moves, aliases={4: 0},
    )(proj_rest, proj_rest, proj_rest, proj_rest, mix, conv_w, conv_b, dt_bias, a_log, d_skip, norm_g)
    return (res, landed) if moves else res


def _ssd_bwd(proj_rest, pre, h_states, d_mix, dt_bias, a_log, d_skip, norm_g, *, name, moves=()):
    t = proj_rest.shape[0]
    nc = t // CHUNK

    def body(pre_ref, z_ref, dt_ref, hs_ref, dy_ref, dtb_ref, al_ref, ds_ref, ng_ref,
             dpre_ref, dz_ref, ddt_ref, ddtb_ref, dal_ref, dds_ref, dng_ref, dh_scr):
        i = pl.program_id(0)

        @pl.when(i == 0)
        def _():
            dh_scr[...] = jnp.zeros_like(dh_scr)
            ddtb_ref[...] = jnp.zeros_like(ddtb_ref)
            dal_ref[...] = jnp.zeros_like(dal_ref)
            dds_ref[...] = jnp.zeros_like(dds_ref)
            dng_ref[...] = jnp.zeros_like(dng_ref)

        h_prev = tuple(hs_ref[0, j] for j in range(_PAIRS))
        _, vjp = jax.vjp(_ssd_chunk, pre_ref[...], z_ref[...], dt_ref[...], h_prev, dtb_ref[...], al_ref[...],
                         ds_ref[...], ng_ref[...])
        dpre, dz, ddt, dh_prev, ddtb, dal, dds, dng = vjp((dy_ref[...], tuple(dh_scr[j] for j in range(_PAIRS))))
        dpre_ref[...] = dpre
        dz_ref[...] = dz.astype(dz_ref.dtype)
        ddt_ref[...] = ddt.astype(ddt_ref.dtype)
        for j in range(_PAIRS):
            dh_scr[j] = dh_prev[j]
        ddtb_ref[...] += ddtb
        dal_ref[...] += dal
        dds_ref[...] += dds
        dng_ref[...] += dng

    rev = lambda i: nc - 1 - i
    small = pl.BlockSpec((1, LANES), lambda i: (0, 0))
    wide = pl.BlockSpec((1, D_MODEL), lambda i: (0, 0))
    res, landed = _pcall(
        body, name=name, grid=(nc,),
        in_specs=[pl.BlockSpec((CHUNK, CONV_DIM), lambda i: (rev(i), 0)),
                  pl.BlockSpec((CHUNK, D_MODEL), lambda i: (rev(i), CONV_DIM // D_MODEL)),
                  pl.BlockSpec((CHUNK, LANES), lambda i: (rev(i), _DT_BLOCK)),
                  pl.BlockSpec((1, _PAIRS, SSM_STATE, LANES), lambda i: (rev(i), 0, 0, 0)),
                  pl.BlockSpec((CHUNK, D_MODEL), lambda i: (rev(i), 1)),
                  small, small, small, wide],
        out_specs=[pl.BlockSpec((CHUNK, CONV_DIM), lambda i: (rev(i), 0)),
                   pl.BlockSpec((CHUNK, D_MODEL), lambda i: (rev(i), 0)),
                   pl.BlockSpec((CHUNK, LANES), lambda i: (rev(i), 0)),
                   small, small, small, wide],
        out_shape=[jax.ShapeDtypeStruct((t, CONV_DIM), F32), jax.ShapeDtypeStruct((t, D_MODEL), BF16),
                   jax.ShapeDtypeStruct((t, LANES), BF16),
                   jax.ShapeDtypeStruct((1, LANES), F32), jax.ShapeDtypeStruct((1, LANES), F32),
                   jax.ShapeDtypeStruct((1, LANES), F32), jax.ShapeDtypeStruct((1, D_MODEL), F32)],
        scratch_shapes=[pltpu.VMEM((_PAIRS, SSM_STATE, LANES), F32)],
        semantics=("arbitrary",), moves=moves,
    )(pre, proj_rest, proj_rest, h_states, d_mix, dt_bias, a_log, d_skip, norm_g)
    return (res, landed) if moves else res


def _conv_bwd(proj_rest, dpre, dz, ddt, conv_w, *, name, tb=256, moves=()):
    t = proj_rest.shape[0]
    nb = t // tb
    per = tb // HALO

    def body(x_ref, prev_ref, dpre_ref, next_ref, dz_ref, ddt_ref, cw_ref, drest_ref, dcw_ref, dcb_ref):
        i = pl.program_id(0)

        @pl.when(i == 0)
        def _():
            dcw_ref[...] = jnp.zeros_like(dcw_ref)
            dcb_ref[...] = jnp.zeros_like(dcb_ref)

        x = x_ref[...]
        dp = dpre_ref[...]
        w = cw_ref[...]
        prev8 = jnp.where(i == 0, 0.0, prev_ref[...])
        next8 = jnp.where(i == nb - 1, 0.0, next_ref[...])
        dx = dp * w[SSM_CONV - 1:SSM_CONV]
        for j in range(SSM_CONV - 1):
            dx = dx + _shift_up(dp, next8, SSM_CONV - 1 - j) * w[j:j + 1]
        drest_ref[:, :CONV_DIM] = dx.astype(drest_ref.dtype)
        drest_ref[:, CONV_DIM:CONV_DIM + D_MODEL] = dz_ref[...].astype(drest_ref.dtype)
        drest_ref[:, CONV_DIM + D_MODEL:] = ddt_ref[...].astype(drest_ref.dtype)
        for j in range(SSM_CONV):
            dcw_ref[j:j + 1, :] += jnp.sum(dp * _shift_down(prev8, x, SSM_CONV - 1 - j), axis=0, keepdims=True)
        dcb_ref[...] += jnp.sum(dp, axis=0, keepdims=True)

    res, landed = _pcall(
        body, name=name, grid=(nb,),
        in_specs=[pl.BlockSpec((tb, CONV_DIM), lambda i: (i, 0)),
                  pl.BlockSpec((HALO, CONV_DIM), lambda i: (jnp.maximum(i * per - 1, 0), 0)),
                  pl.BlockSpec((tb, CONV_DIM), lambda i: (i, 0)),
                  pl.BlockSpec((HALO, CONV_DIM), lambda i: (jnp.minimum((i + 1) * per, nb * per - 1), 0)),
                  pl.BlockSpec((tb, D_MODEL), lambda i: (i, 0)),
                  pl.BlockSpec((tb, LANES), lambda i: (i, 0)),
                  pl.BlockSpec((SSM_CONV, CONV_DIM), lambda i: (0, 0))],
        out_specs=[pl.BlockSpec((tb, REST_W), lambda i: (i, 0)),
                   pl.BlockSpec((SSM_CONV, CONV_DIM), lambda i: (0, 0)),
                   pl.BlockSpec((1, CONV_DIM), lambda i: (0, 0))],
        out_shape=[jax.ShapeDtypeStruct((t, REST_W), BF16), jax.ShapeDtypeStruct((SSM_CONV, CONV_DIM), F32),
                   jax.ShapeDtypeStruct((1, CONV_DIM), F32)],
        semantics=("arbitrary",), moves=moves,
    )(proj_rest, proj_rest, dpre, dpre, dz, ddt, conv_w)
    return (res, landed) if moves else res


_KV_BLOCK = D_MODEL // (2 * LANES)
_SINK_ROWS = _PAIRS_PER_KV * CHUNK


def _stack_pairs(ref, kv_head):
    base = kv_head * _PAIRS_PER_KV
    return jnp.concatenate([ref[:, (base + p) * LANES:(base + p + 1) * LANES] for p in range(_PAIRS_PER_KV)], axis=0)


def _attn_fwd(qkv, sinks, *, name, moves=()):
    t = qkv.shape[0]
    nb = t // CHUNK

    def body(q_ref, kvp_ref, kvc_ref, s_ref, o_ref, p_ref, st_ref):
        first = pl.program_id(0) == 0
        kv = jnp.concatenate([kvp_ref[...], kvc_ref[...]], axis=0)
        stats = jnp.zeros((_SINK_ROWS, LANES), F32)
        for j in range(ATTN_KV):
            q4 = _stack_pairs(q_ref, j)
            ks, vs = _kv_placed(kv[:, :LANES], j), _kv_placed(kv[:, LANES:], j)
            out = None
            for e in range(2):
                p, e_sink, den = _attn_probs(q4, ks[e], s_ref[j, e], first)
                inv = 1.0 / den
                o = _dg(p, vs[e], _NN) * inv
                out = o if out is None else out + o
                p_ref[0, 2 * j + e] = p.astype(p_ref.dtype)
                stats = stats + _lane_column(inv, 2 * j + e) + _lane_column(e_sink, 4 + 2 * j + e)
            for pair in range(_PAIRS_PER_KV):
                col = (j * _PAIRS_PER_KV + pair) * LANES
                o_ref[:, col:col + LANES] = out[pair * CHUNK:(pair + 1) * CHUNK].astype(o_ref.dtype)
        st_ref[0] = stats

    return _unpack(_pcall(
        body, name=name, grid=(nb,),
        in_specs=[pl.BlockSpec((CHUNK, D_MODEL), lambda i: (i, 0)),
                  pl.BlockSpec((CHUNK, 2 * LANES), lambda i: (jnp.maximum(i - 1, 0), _KV_BLOCK)),
                  pl.BlockSpec((CHUNK, 2 * LANES), lambda i: (i, _KV_BLOCK)),
                  pl.BlockSpec((ATTN_KV, 2, _SINK_ROWS, 1), lambda i: (0, 0, 0, 0))],
        out_specs=[pl.BlockSpec((CHUNK, D_MODEL), lambda i: (i, 0)),
                   pl.BlockSpec((1, 2 * ATTN_KV, _SINK_ROWS, 2 * CHUNK), lambda i: (i, 0, 0, 0)),
                   pl.BlockSpec((1, _SINK_ROWS, LANES), lambda i: (i, 0, 0))],
        out_shape=[jax.ShapeDtypeStruct((t, D_MODEL), BF16),
                   jax.ShapeDtypeStruct((nb, 2 * ATTN_KV, _SINK_ROWS, 2 * CHUNK), BF16),
                   jax.ShapeDtypeStruct((nb, _SINK_ROWS, LANES), F32)],
        semantics=("parallel",), moves=moves,
    )(qkv, qkv, qkv, sinks), moves, 3)


def _attn_bwd(qkv, probs, stats, attn, d_o, *, name, moves=()):
    t = qkv.shape[0]
    nb = t // CHUNK

    def body(q_ref, kvp_ref, kvc_ref, p_ref, st_ref, o_ref, do_ref, dqkv_ref, ds_ref, dkv_scr):
        @pl.when(pl.program_id(0) == 0)
        def _():
            dkv_scr[...] = jnp.zeros_like(dkv_scr)
            ds_ref[...] = jnp.zeros_like(ds_ref)

        kv = jnp.concatenate([kvp_ref[...], kvc_ref[...]], axis=0)
        table = st_ref[0]
        d_k = jnp.zeros((2 * CHUNK, LANES), F32)
        d_v = jnp.zeros((2 * CHUNK, LANES), F32)
        for j in range(ATTN_KV):
            q4, do4, o4 = _stack_pairs(q_ref, j), _stack_pairs(do_ref, j), _stack_pairs(o_ref, j).astype(F32)
            ks, vs = _kv_placed(kv[:, :LANES], j), _kv_placed(kv[:, LANES:], j)
            dq4, dk, dv = None, [], []
            for e in range(2):
                p = p_ref[0, 2 * j + e].astype(F32)
                inv, e_sink = _col_pick(table, 2 * j + e), _col_pick(table, 4 + 2 * j + e)
                do_e = jnp.where(_parity_lanes(e), do4, 0.0)
                d_num = do_e * inv
                d_den = -jnp.sum(do_e * o4, axis=1, keepdims=True) * inv
                ds = p * (_dg(d_num, vs[e], _NT) + d_den)
                ds_ref[j, e] += d_den * e_sink
                dq = _dg(ds, ks[e], _NN) * _ATTN_SCALE
                dq4 = dq if dq4 is None else dq4 + dq
                dk.append(_dg(ds, q4, _TN) * _ATTN_SCALE)
                dv.append(_dg(p, d_num, _TN))
            for pair in range(_PAIRS_PER_KV):
                col = (j * _PAIRS_PER_KV + pair) * LANES
                dqkv_ref[:, col:col + LANES] = dq4[pair * CHUNK:(pair + 1) * CHUNK]
            d_k = d_k + _kv_unplaced(dk[0], dk[1], j)
            d_v = d_v + _kv_unplaced(dv[0], dv[1], j)
        d_kv = jnp.concatenate([d_k, d_v], axis=1)
        dqkv_ref[:, D_MODEL:] = d_kv[CHUNK:] + dkv_scr[...]
        dkv_scr[...] = d_kv[:CHUNK]

    cur = lambda i: (nb - 1 - i, 0)
    sk = pl.BlockSpec((ATTN_KV, 2, _SINK_ROWS, 1), lambda i: (0, 0, 0, 0))
    res, landed = _pcall(
        body, name=name, grid=(nb,),
        in_specs=[pl.BlockSpec((CHUNK, D_MODEL), cur),
                  pl.BlockSpec((CHUNK, 2 * LANES), lambda i: (jnp.maximum(nb - 2 - i, 0), _KV_BLOCK)),
                  pl.BlockSpec((CHUNK, 2 * LANES), lambda i: (nb - 1 - i, _KV_BLOCK)),
                  pl.BlockSpec((1, 2 * ATTN_KV, _SINK_ROWS, 2 * CHUNK), lambda i: (nb - 1 - i, 0, 0, 0)),
                  pl.BlockSpec((1, _SINK_ROWS, LANES), lambda i: (nb - 1 - i, 0, 0)),
                  pl.BlockSpec((CHUNK, D_MODEL), cur), pl.BlockSpec((CHUNK, D_MODEL), cur)],
        out_specs=[pl.BlockSpec((CHUNK, QKV_DIM), cur), sk],
        out_shape=[jax.ShapeDtypeStruct((t, QKV_DIM), F32), jax.ShapeDtypeStruct((ATTN_KV, 2, _SINK_ROWS, 1), F32)],
        scratch_shapes=[pltpu.VMEM((CHUNK, 2 * LANES), F32)],
        semantics=("arbitrary",), moves=moves,
    )(qkv, qkv, qkv, probs, stats, attn, d_o)
    return (res, landed) if moves else res


def _adamw(parts, w, m, v, *, name, tb=512, moves=()):
    layers, r, c = w.shape
    n = parts[0].shape[0]
    tb = min(tb, r)
    assert r % tb == 0 and len(parts) == layers, (name, r, tb)
    nb = r // tb

    def body(*refs):
        p_refs = refs[:layers]
        w_ref, m_ref, v_ref, g_ref, d_ref, nm_ref, nv_ref = refs[layers:]
        for layer in range(layers):
            @pl.when(pl.program_id(0) == layer)
            def _(p_ref=p_refs[layer]):
                g = p_ref[0].astype(F32)
                for s in range(1, n):
                    g = g + p_ref[s].astype(F32)
                m_new = ADAM_B1 * m_ref[...] + (1.0 - ADAM_B1) * g
                v_new = ADAM_B2 * v_ref[...] + (1.0 - ADAM_B2) * jnp.square(g)
                m_hat = m_new / (1.0 - ADAM_B1 ** ADAM_STEP)
                v_hat = v_new / (1.0 - ADAM_B2 ** ADAM_STEP)
                g_ref[...] = g
                d_ref[...] = -ADAM_LR * (m_hat / (jnp.sqrt(v_hat) + ADAM_EPS) + ADAM_WD * w_ref[...])
                nm_ref[...] = m_new
                nv_ref[...] = v_new

    part_spec = lambda layer: pl.BlockSpec(
        (n, tb, c), lambda l, i: (0, jnp.clip(i + (l - layer) * nb, 0, nb - 1), 0))
    blk = pl.BlockSpec((None, tb, c), lambda l, i: (l, i, 0))
    res, landed = _pcall(
        body, name=name, grid=(layers, nb),
        in_specs=[part_spec(layer) for layer in range(layers)] + [blk, blk, blk],
        out_specs=[blk] * 4,
        out_shape=[jax.ShapeDtypeStruct((layers, r, c), F32)] * 4,
        semantics=("arbitrary", "arbitrary"), moves=moves,
    )(*parts, w, m, v)
    return (res, landed) if moves else res


def _as_rows(a):
    flat = a.reshape(-1)
    pad = (-flat.shape[0]) % PACK_W
    if pad:
        flat = jnp.pad(flat, (0, pad))
    return flat.reshape(-1, PACK_W)


def _cols_from_shards(g):
    return jnp.transpose(g, (1, 0, 2)).reshape(g.shape[1], -1)


def _shard_cols(shards, lo, hi):
    c = shards.shape[2]
    pieces = []
    for j in range(shards.shape[0]):
        a, b = max(lo, j * c), min(hi, (j + 1) * c)
        if a < b:
            pieces.append(shards[j, :, a - j * c:b - j * c])
    return pieces


def _cols_of(sources, lo, hi):
    pieces = []
    for arr, col0, first, last in sources:
        a, b = max(lo, first), min(hi, last)
        if a < b:
            pieces.append(arr[:, col0 + a - first:col0 + b - first])
    return pieces


def _pad_lanes(a):
    return jnp.pad(a, ((0, 0), (0, LANES - a.shape[1])))


def kernel(x, norm_mix_g, norm_mlp_g, final_norm_g, w_in_even, w_out_even, gm_ln_g, gm_ln_b, gm_w_s, gm_b_s, ssm_conv_w, ssm_conv_b, ssm_dt_bias, ssm_a_log, ssm_d, ssm_norm_g, w_qkv, b_qkv, w_o, b_o, attn_sinks, w_up, w_down, loss_target, m_norm_mix_g, m_norm_mlp_g, m_final_norm_g, m_w_in_even, m_w_out_even, m_gm_ln_g, m_gm_ln_b, m_gm_w_s, m_gm_b_s, m_ssm_conv_w, m_ssm_conv_b, m_ssm_dt_bias, m_ssm_a_log, m_ssm_d, m_ssm_norm_g, m_w_qkv, m_b_qkv, m_w_o, m_b_o, m_attn_sinks, m_w_up, m_w_down, v_norm_mix_g, v_norm_mlp_g, v_final_norm_g, v_w_in_even, v_w_out_even, v_gm_ln_g, v_gm_ln_b, v_gm_w_s, v_gm_b_s, v_ssm_conv_w, v_ssm_conv_b, v_ssm_dt_bias, v_ssm_a_log, v_ssm_d, v_ssm_norm_g, v_w_qkv, v_b_qkv, v_w_o, v_b_o, v_attn_sinks, v_w_up, v_w_down):
    names = ["norm_mix_g", "norm_mlp_g", "final_norm_g", "w_in_even", "w_out_even", "gm_ln_g", "gm_ln_b", "gm_w_s",
             "gm_b_s", "ssm_conv_w", "ssm_conv_b", "ssm_dt_bias", "ssm_a_log", "ssm_d", "ssm_norm_g", "w_qkv",
             "b_qkv", "w_o", "b_o", "attn_sinks", "w_up", "w_down"]
    env = locals()
    W = {n: env[n] for n in names}
    M = {n: env["m_" + n] for n in names}
    V = {n: env["v_" + n] for n in names}
    big = ["w_in_even", "w_out_even", "w_qkv", "w_o", "w_up", "w_down"]
    small_sharded = ["ssm_conv_w", "b_qkv", "b_o"]
    replicated = [n for n in names if n not in big and n not in small_sharded]
    me = 4 * lax.axis_index("x") + 2 * lax.axis_index("y") + lax.axis_index("c")
    t = x.shape[1]
    xs = x.reshape(t, D_MODEL)
    target = loss_target.reshape(t, D_MODEL)
    gather = lambda a: _Move("gather", a)
    over_ici = lambda a: _Move("gather_ici", a)
    over_d2d = lambda a: _Move("gather_d2d", a)
    by_core = lambda a: a.reshape((N_CHIP, N_CORE) + a.shape[1:])
    to_sibling = lambda a: [_Move("scatter_d2d", by_core(a))]
    my_core = lax.axis_index("c").astype(jnp.int32).reshape(1)
    pair = lambda a, theirs, name: _pair_add(by_core(a), theirs, my_core, name=name)
    to_chips = lambda a: _Move("scatter_ici", a)
    whole = lambda a: a.reshape((N_DEV,) + a.shape[2:])
    row = lambda a: a.reshape(1, D_MODEL)

    small_flat = jnp.concatenate([W[n].reshape(-1) for n in small_sharded])
    w_in_g, small_g = _exchange([over_ici(w_in_even[0].astype(BF16)), gather(_as_rows(small_flat))],
                                name="gather_w_in", then_d2d=[0])
    w_in_s = whole(w_in_g)
    z_lo, xbc_lo, dt_lo = 2 * D_MODEL, 3 * D_MODEL, 3 * D_MODEL + CONV_DIM
    w_uv = jnp.concatenate(_shard_cols(w_in_s, 0, z_lo), axis=1)
    w_rest = jnp.concatenate(_shard_cols(w_in_s, xbc_lo, dt_lo) + _shard_cols(w_in_s, z_lo, xbc_lo)
                             + _shard_cols(w_in_s, dt_lo, IN_EVEN)
                             + [jnp.zeros((D_MODEL, LANES - SSM_HEADS), BF16)], axis=1)
    small_all = small_g.reshape(N_DEV, -1)
    n_cw = SSM_CONV * CONV_DIM // N_DEV
    n_bq = QKV_DIM // N_DEV
    conv_w = _cols_from_shards(small_all[:, :n_cw].reshape(N_DEV, SSM_CONV, CONV_DIM // N_DEV))
    bqkv = small_all[:, n_cw:n_cw + n_bq].reshape(1, QKV_DIM)
    bo = small_all[:, n_cw + n_bq:n_cw + n_bq + D_MODEL // N_DEV].reshape(1, D_MODEL)

    conv_b = ssm_conv_b.reshape(1, CONV_DIM)
    dt_bias, a_log, d_skip = _pad_lanes(ssm_dt_bias), _pad_lanes(ssm_a_log), _pad_lanes(ssm_d)
    gm_w = gm_w_s[0]
    gm_b = gm_b_s[0].reshape(GM_GROUPS, CHUNK, 1)
    sink_rows = jnp.repeat(jnp.transpose(attn_sinks.reshape(ATTN_KV, _PAIRS_PER_KV, 2), (0, 2, 1)), CHUNK,
                           axis=2).reshape(ATTN_KV, 2, _SINK_ROWS, 1)
    w_up_b, w_down_b = w_up.astype(BF16), w_down.astype(BF16)

    w_down0_a, w_down0_b = w_down_b[0, :FF_SHARD // 2], w_down_b[0, FF_SHARD // 2:]
    (y0, proj_uv), (w_qkv_g,) = _norm_matmul(xs, row(norm_mix_g[0]), w_uv, name="proj_uv", emit_y=True,
                                             moves=[over_ici(jnp.transpose(w_qkv[0]).astype(BF16))])
    proj_rest, (w_out_g,) = _norm_matmul(xs, row(norm_mix_g[0]), w_rest, name="proj_rest", emit_y=False,
                                         moves=[over_ici(w_out_even[0].astype(BF16))])
    mix, (w_o_g, w_down0_a, w_out_g, w_qkv_g) = _gmlp_fwd(
        proj_uv, gm_ln_g, gm_ln_b, gm_w, gm_b, name="gmlp_fwd",
        moves=[over_ici(w_o[0].astype(BF16)), over_ici(w_down0_a), over_d2d(w_out_g), over_d2d(w_qkv_g)])
    (mix, h_states, conv_pre), (w_up0_g, w_o_g, w_down0_a) = _ssd_fwd(
        proj_rest, mix, conv_w, conv_b, dt_bias, a_log, d_skip, ssm_norm_g, name="ssd_fwd",
        moves=[over_ici(w_up_b[0]), over_d2d(w_o_g), over_d2d(w_down0_a)])
    w_out_f = whole(w_out_g).reshape(2 * D_MODEL, D_MODEL)
    (h1, y1), (w_down0_b, w_up0_g) = _residual_matmul(
        mix, w_out_f, xs, name="mix_out", norm_g=row(norm_mlp_g[0]),
        moves=[over_ici(w_down0_b), over_d2d(w_up0_g)])
    up0, (w_down0_b, w_up1_g) = _mlp_up(y1, whole(w_up0_g), name="mlp_up0",
                                        moves=[over_d2d(w_down0_b), over_ici(w_up_b[1])])
    w_down_g = [[whole(w_down0_a), whole(w_down0_b)]]
    (h2, y2), (w_up1_g,) = _mlp_down(up0, w_down_g[0], h1, name="mlp_down0", norm_g=row(norm_mix_g[1]),
                                     moves=[over_d2d(w_up1_g)])
    wqkv = whole(w_qkv_g).reshape(QKV_DIM, D_MODEL)
    wo = whole(w_o_g).reshape(D_MODEL, D_MODEL)
    qkv = _residual_matmul(y2, wqkv, None, name="qkv", bias=bqkv, w_transposed=True)
    (attn, attn_p, attn_stats), (w_down1_g,) = _attn_fwd(qkv, sink_rows, name="attn_fwd",
                                                         moves=[over_ici(w_down_b[1])])
    h3, y3 = _residual_matmul(attn, wo, h2, name="attn_out", bias=bo, norm_g=row(norm_mlp_g[1]))
    w_up_g = [whole(w_up0_g), whole(w_up1_g)]
    up1, (w_down1_g,) = _mlp_up(y3, w_up_g[1], name="mlp_up1", moves=[over_d2d(w_down1_g)])
    w_down_g.append([whole(w_down1_g)])
    loss_part, dh4, dh4_b, d_final_g = _mlp_down_loss(up1, w_down_g[1], h3, row(final_norm_g), target,
                                                      name="mlp_down1_loss")

    by_dev_rows = lambda a: a.reshape((N_DEV, a.shape[0] // N_DEV) + a.shape[1:])

    def mlp_bwd(dh, dh_b, h, y, up, layer, first_moves=()):
        res = _mlp_down_dx(dh_b, w_down_g[layer], up, name=f"mlp_down_dx{layer}", moves=first_moves)
        d_up, first_landed = res if first_moves else (res, [])
        g_down = _dw_by_rows(up, dh_b, name=f"mlp_down_dw{layer}", tk=FF_SHARD, square_relu=True)
        g_down = by_dev_rows(g_down)
        g_up, (theirs,) = _dw_by_cols(y, d_up, name=f"mlp_up_dw{layer}", tn=FF_SHARD, by_device=True,
                                      moves=to_sibling(g_down))
        q_down = pair(g_down, theirs, f"mlp_down_pair{layer}")
        (dh_new, dh_new_b, dg, cs), (theirs,) = _dx_norm(
            d_up, w_up_g[layer], h, row(norm_mlp_g[layer]), dh, name=f"mlp_up_dx{layer}", by_device_cols=True,
            moves=to_sibling(g_up))
        q_up = pair(g_up, theirs, f"mlp_up_pair{layer}")
        return dh_new, dh_new_b, cs, dg, q_up, q_down, first_landed

    dh3, dh3_b, cs3, g_nmlp1, q_up1, q_down1, _ = mlp_bwd(dh4, dh4_b, h3, y3, up1, 1)
    g_bo = cs3
    g_wo = by_dev_rows(_dw_by_cols(attn, dh3_b, name="attn_out_dw", tn=FF_SHARD))
    d_attn, (theirs,) = _dx(dh3_b, wo, name="attn_out_dx", moves=to_sibling(g_wo))
    q_wo = pair(g_wo, theirs, "attn_out_pair")
    (dqkv, d_sink), (r_down1, r_up1) = _attn_bwd(qkv, attn_p, attn_stats, attn, d_attn, name="attn_bwd",
                                                 moves=[to_chips(q_down1), to_chips(q_up1)])
    g_wqkv, g_bqkv = _dw_by_rows(dqkv, y2, name="qkv_dw", tk=QKV_DIM // 2, column_sums=True)
    g_wqkv = by_dev_rows(g_wqkv)
    (dh2, dh2_b, g_nmix1, _), (theirs,) = _dx_norm(dqkv, wqkv, h2, row(norm_mix_g[1]), dh3, name="qkv_dx",
                                                   w_transposed=True, moves=to_sibling(g_wqkv))
    q_wqkv = pair(g_wqkv, theirs, "qkv_pair")
    dh1, dh1_b, _, g_nmlp0, q_up0, q_down0, (r_wqkv, r_wo) = mlp_bwd(
        dh2, dh2_b, h1, y1, up0, 0, first_moves=[to_chips(q_wqkv), to_chips(q_wo)])

    d_mix = _dx(dh1_b, w_out_f, name="mix_out_dx")
    g_wout = by_dev_rows(_dw_by_rows(mix, dh1_b, name="mix_out_dw", tk=FF_SHARD))
    (d_uv, g_ln_g, g_ln_b, g_gm_w, g_gm_b), (r_down0, theirs) = _gmlp_bwd(
        proj_uv, d_mix, gm_ln_g, gm_ln_b, gm_w, gm_b, name="gmlp_bwd", moves=[to_chips(q_down0)] + to_sibling(g_wout))
    q_wout = pair(g_wout, theirs, "mix_out_pair")

    early = [("norm_mlp_g", None), ("final_norm_g", None), ("norm_mix_g", 1), ("gm_ln_g", None), ("gm_ln_b", None),
             ("gm_w_s", None), ("gm_b_s", None), ("attn_sinks", None)]
    late = [("norm_mix_g", 0), ("ssm_conv_b", None), ("ssm_dt_bias", None), ("ssm_a_log", None), ("ssm_d", None),
            ("ssm_norm_g", None)]
    early_sharded, late_sharded = ["b_qkv", "b_o"], ["ssm_conv_w"]
    small_grads = {
        ("norm_mlp_g", None): jnp.concatenate([g_nmlp0, g_nmlp1], axis=0),
        ("final_norm_g", None): d_final_g, ("norm_mix_g", 1): g_nmix1,
        ("gm_ln_g", None): g_ln_g, ("gm_ln_b", None): g_ln_b, ("gm_w_s", None): g_gm_w, ("gm_b_s", None): g_gm_b,
        ("attn_sinks", None): jnp.transpose(
            jnp.sum(d_sink.reshape(ATTN_KV, 2, _PAIRS_PER_KV, CHUNK), axis=3), (0, 2, 1)),
        "b_qkv": g_bqkv, "b_o": g_bo,
    }
    pack = lambda keys: _as_rows(jnp.concatenate([small_grads[key].reshape(-1) for key in keys]))
    (dpre, dz, ddt, g_dtb, g_alog, g_dskip, g_ssm_ng), (r_up0, r_wout, early_recv) = _ssd_bwd(
        proj_rest, conv_pre, h_states, d_mix, dt_bias, a_log, d_skip, ssm_norm_g, name="ssd_bwd",
        moves=[to_chips(q_up0), to_chips(q_wout), gather(pack(early + early_sharded))])
    d_rest, g_conv_w, g_conv_b = _conv_bwd(proj_rest, dpre, dz, ddt, conv_w, name="conv_bwd")
    g_w_uv = _dw_by_cols(y0, d_uv, name="proj_uv_dw", tn=FF_SHARD)
    g_w_rest = _dw_by_cols(y0, d_rest, name="proj_rest_dw", tn=REST_W // 5)
    in_cols = [(g_w_uv, 0, 0, z_lo), (g_w_rest, CONV_DIM, z_lo, xbc_lo), (g_w_rest, 0, xbc_lo, dt_lo),
               (g_w_rest, CONV_DIM + D_MODEL, dt_lo, IN_EVEN)]
    in_shard = IN_EVEN // N_DEV
    g_w_in = jnp.stack([jnp.concatenate(_cols_of(in_cols, j * in_shard, (j + 1) * in_shard), axis=1)
                        for j in range(N_DEV)])
    dy0, (theirs,) = _dx(d_uv, w_uv, name="proj_uv_dx", moves=to_sibling(g_w_in))
    q_w_in = pair(g_w_in, theirs, "proj_pair")
    (dx, _, g_nmix0, _), r_w_in = _dx_norm(d_rest, w_rest, xs, row(norm_mix_g[0]), dh1, name="proj_rest_dx",
                                           partial=dy0, moves=[to_chips(q_w_in)])
    small_grads.update({
        ("loss", None): loss_part[:1, :1],
        ("norm_mix_g", 0): g_nmix0, ("ssm_conv_b", None): g_conv_b,
        ("ssm_dt_bias", None): g_dtb[:, :SSM_HEADS], ("ssm_a_log", None): g_alog[:, :SSM_HEADS],
        ("ssm_d", None): g_dskip[:, :SSM_HEADS], ("ssm_norm_g", None): g_ssm_ng, "ssm_conv_w": g_conv_w,
    })


    def update(n, parts, moves=(), transposed=False):
        shape = W[n].shape
        if transposed:
            as3 = lambda a: jnp.transpose(a[0])[None]
            back = lambda a: jnp.transpose(a[0])[None]
        else:
            as3 = lambda a: a.reshape((len(parts),) + parts[0].shape[1:])
            back = lambda a: a.reshape(shape)
        res = _adamw(parts, as3(W[n]), as3(M[n]), as3(V[n]), name="adamw_" + n, moves=moves)
        res, landed = res if moves else (res, [])
        return [back(a) for a in res], landed

    out = {}
    late_keys = late + late_sharded + [("loss", None)]
    out["w_o"], (late_recv,) = update("w_o", [r_wo], moves=[gather(pack(late_keys))])
    out["w_down"], _ = update("w_down", [r_down0, r_down1])
    out["w_up"], _ = update("w_up", [r_up0, r_up1])
    out["w_out_even"], _ = update("w_out_even", [r_wout])
    out["w_qkv"], _ = update("w_qkv", [r_wqkv], transposed=True)
    out["w_in_even"], _ = update("w_in_even", list(r_w_in))

    def unpacked(recv, keys):
        flat, res, o = recv.reshape(N_DEV, -1), {}, 0
        for key in keys:
            res[key] = flat[:, o:o + small_grads[key].size]
            o += small_grads[key].size
        return res

    arrived = {**unpacked(early_recv, early + early_sharded), **unpacked(late_recv, late_keys)}
    piece = lambda tree, key: tree[key[0]] if key[1] is None else tree[key[0]][key[1]]

    def rows_by_device(cat):
        pad = (-cat.shape[1]) % PACK_W
        return jnp.pad(cat, ((0, 0), (0, pad))).reshape(N_DEV, -1, PACK_W)

    rep_keys = early + late
    rep_parts = rows_by_device(jnp.concatenate([arrived[key] for key in rep_keys], axis=1))
    flat_rep = lambda tree: _as_rows(jnp.concatenate([piece(tree, key).reshape(-1) for key in rep_keys]))[None]
    rep_res = _adamw([rep_parts], flat_rep(W), flat_rep(M), flat_rep(V), name="adamw_replicated")
    sh_keys = early_sharded + late_sharded
    shard_parts = []
    for n in sh_keys:
        full = arrived[n].reshape((N_DEV,) + small_grads[n].shape)
        c = full.shape[-1] // N_DEV
        shard_parts.append(lax.dynamic_slice_in_dim(full, me * c, c, axis=full.ndim - 1).reshape(N_DEV, -1))
    sh_rows = rows_by_device(jnp.concatenate(shard_parts, axis=1))
    flat_sh = lambda tree: _as_rows(jnp.concatenate([tree[n].reshape(-1) for n in sh_keys]))[None]
    sh_res = _adamw([sh_rows], flat_sh(W), flat_sh(M), flat_sh(V), name="adamw_small_sharded")

    def unpack_replicated(rows):
        flat, vals, o = rows.reshape(-1), {}, 0
        for key in rep_keys:
            size = piece(W, key).size
            vals[key] = flat[o:o + size]
            o += size
        res = {}
        for n in replicated:
            if (n, None) in vals:
                res[n] = vals[(n, None)].reshape(W[n].shape)
            else:
                res[n] = jnp.stack([vals[(n, r)] for r in range(W[n].shape[0])]).reshape(W[n].shape)
        return res

    def unpack_sharded(rows):
        flat, res, o = rows.reshape(-1), {}, 0
        for n in sh_keys:
            res[n] = flat[o:o + W[n].size].reshape(W[n].shape)
            o += W[n].size
        return res

    results = []
    for idx in range(4):
        d = {n: out[n][idx] for n in big}
        d.update(unpack_replicated(rep_res[idx]))
        d.update(unpack_sharded(sh_res[idx]))
        results.append(d)

    loss = jnp.sum(arrived[("loss", None)])
    grad_x = dx.reshape(x.shape)
    final = [loss, grad_x]
    for d in results:
        final.extend(d[n] for n in names)
    return tuple(final)
```

```python
import dataclasses
import functools

import jax
import jax.numpy as jnp
from jax import lax
from jax.experimental import pallas as pl
from jax.experimental.pallas import tpu as pltpu

F32 = jnp.float32
BF16 = jnp.bfloat16

N_DEV = 8
D_MODEL = 1024
D_FF = 4096
RMS_EPS = 1e-5
LN_EPS = 1e-5
CHUNK = 128
GM_GROUPS = 8
SSM_HEADS = 16
SSM_HEADDIM = 64
SSM_GROUPS = 4
SSM_STATE = 128
SSM_CONV = 4
CONV_DIM = 2048
IN_EVEN = 5136
REST_W = 3200
ATTN_HEADS = 16
ATTN_KV = 2
HEAD_DIM = 64
QKV_DIM = 1280
LANES = 128
HALO = 8
PACK_W = 1024

ADAM_LR = 0.001
ADAM_B1 = 0.9
ADAM_B2 = 0.999
ADAM_EPS = 1e-08
ADAM_WD = 0.01
ADAM_STEP = 10

VMEM_LIMIT_BYTES = 56 * 1024 * 1024


_NN = (((1,), (0,)), ((), ()))
_NT = (((1,), (1,)), ((), ()))
_TN = (((0,), (0,)), ((), ()))


def _dg(a, b, dims):
    return lax.dot_general(a.astype(BF16), b.astype(BF16), dims, preferred_element_type=F32)


@jax.custom_vjp
def _nn(a, b):
    return _dg(a, b, _NN)


@jax.custom_vjp
def _nt(a, b):
    return _dg(a, b, _NT)


@jax.custom_vjp
def _tn(a, b):
    return _dg(a, b, _TN)


_nn.defvjp(lambda a, b: (_dg(a, b, _NN), (a, b)), lambda r, g: (_nt(g, r[1]), _tn(r[0], g)))
_nt.defvjp(lambda a, b: (_dg(a, b, _NT), (a, b)), lambda r, g: (_nn(g, r[1]), _tn(g, r[0])))
_tn.defvjp(lambda a, b: (_dg(a, b, _TN), (a, b)), lambda r, g: (_nt(r[1], g), _nn(r[0], g)))


def _split3_dot(tri, x):
    x1 = x.astype(BF16)
    r1 = x - x1.astype(F32)
    x2 = r1.astype(BF16)
    x3 = (r1 - x2.astype(F32)).astype(BF16)
    t = tri.astype(BF16)
    dot = lambda p: lax.dot_general(t, p, _NN, preferred_element_type=F32)
    return dot(x1) + dot(x2) + dot(x3)


def _tri(lower):
    r = lax.broadcasted_iota(jnp.int32, (CHUNK, CHUNK), 0)
    c = lax.broadcasted_iota(jnp.int32, (CHUNK, CHUNK), 1)
    return jnp.where((r >= c) if lower else (r <= c), 1.0, 0.0).astype(F32)


@jax.custom_vjp
def _cumsum_rows(x):
    return _split3_dot(_tri(True), x)


_cumsum_rows.defvjp(lambda x: (_split3_dot(_tri(True), x), None), lambda _, g: (_split3_dot(_tri(False), g),))


def _sigmoid(x):
    return 1.0 / (1.0 + jnp.exp(-x))


def _silu(x):
    return x * _sigmoid(x)


def _softplus(x):
    return jnp.maximum(x, 0.0) + jnp.log(1.0 + jnp.exp(-jnp.abs(x)))


def _gelu_tanh(x):
    return 0.5 * x * (1.0 + jnp.tanh(0.7978845608028654 * (x + 0.044715 * (x * x * x))))


def _rmsnorm(x, g):
    return x * lax.rsqrt(jnp.mean(x * x, axis=-1, keepdims=True) + RMS_EPS) * g


def _gmlp_chunk(u, v, ln_g, ln_b, w_s, b_s):
    gu = _gelu_tanh(u)
    gv = _gelu_tanh(v)
    mu = jnp.mean(gv, axis=-1, keepdims=True)
    var = jnp.mean(jnp.square(gv - mu), axis=-1, keepdims=True)
    vn = (gv - mu) * lax.rsqrt(var + LN_EPS) * ln_g + ln_b
    r = lax.broadcasted_iota(jnp.int32, (CHUNK, CHUNK), 0)
    c = lax.broadcasted_iota(jnp.int32, (CHUNK, CHUNK), 1)
    causal = r >= c
    outs = []
    for g in range(GM_GROUPS):
        cols = slice(g * LANES, (g + 1) * LANES)
        mixed = _nn(jnp.where(causal, w_s[g], 0.0), vn[:, cols]) + b_s[g]
        outs.append(gu[:, cols] * mixed)
    return jnp.concatenate(outs, axis=1)


def _lane_pick(row, h):
    lane = lax.broadcasted_iota(jnp.int32, row.shape, 1)
    return jnp.sum(jnp.where(lane == h, row, 0.0), axis=1, keepdims=True)


def _col_pick(m, h):
    lane = lax.broadcasted_iota(jnp.int32, m.shape, 1)
    return jnp.sum(jnp.where(lane == h, m, 0.0), axis=1, keepdims=True)


def _row_pick(m, h):
    sub = lax.broadcasted_iota(jnp.int32, m.shape, 0)
    return jnp.sum(jnp.where(sub == h, m, 0.0), axis=0, keepdims=True)


_PAIRS = SSM_HEADS // 2


def _ssd_chunk(pre, z, dt_raw, h_prev, dt_bias, a_log, d_skip, norm_g):
    xbc = _silu(pre)
    dt = _softplus(dt_raw + dt_bias)
    da = dt * (-jnp.exp(a_log))
    a_cum = _cumsum_rows(da)
    a_cum_t = a_cum.T
    dt_t = dt.T
    r = lax.broadcasted_iota(jnp.int32, (CHUNK, CHUNK), 0)
    c = lax.broadcasted_iota(jnp.int32, (CHUNK, CHUNK), 1)
    causal = r >= c
    lane_lo = lax.broadcasted_iota(jnp.int32, (1, LANES), 1) < SSM_HEADDIM
    last_row = lax.broadcasted_iota(jnp.int32, (CHUNK, 1), 0) == CHUNK - 1
    ys, h_next = [], []
    for j in range(_PAIRS):
        g = j // 2
        xs = xbc[:, j * LANES:(j + 1) * LANES]
        bm = xbc[:, 1024 + g * SSM_STATE:1024 + (g + 1) * SSM_STATE]
        cm = xbc[:, 1536 + g * SSM_STATE:1536 + (g + 1) * SSM_STATE]
        cb = _nt(cm, bm)
        y_diag, to_end, e_cum, c_dec, d_row = [], [], [], [], []
        for h in (2 * j, 2 * j + 1):
            col = _col_pick(a_cum, h)
            row = _row_pick(a_cum_t, h)
            dt_col = _col_pick(dt, h)
            dt_row = _row_pick(dt_t, h)
            decay = jnp.exp(jnp.where(causal, col - row, -jnp.inf))
            y_diag.append(_nn(cb * decay * dt_row, xs))
            last = jnp.sum(jnp.where(last_row, col, 0.0), axis=0, keepdims=True)
            to_end.append(jnp.exp(last - col) * dt_col)
            e_cum.append(jnp.exp(col))
            c_dec.append(jnp.exp(last))
            d_row.append(_lane_pick(d_skip, h))
        pair = lambda lo_hi: jnp.where(lane_lo, lo_hi[0], lo_hi[1])
        states = _tn(bm, xs * pair(to_end))
        y_off = _nn(cm, h_prev[j]) * pair(e_cum)
        ys.append(pair(y_diag) + y_off + xs * pair(d_row))
        h_next.append(pair(c_dec) * h_prev[j] + states)
    y = jnp.concatenate(ys, axis=1) * _silu(z)
    width = D_MODEL // SSM_GROUPS
    y = jnp.concatenate(
        [_rmsnorm(y[:, g * width:(g + 1) * width], norm_g[:, g * width:(g + 1) * width]) for g in range(SSM_GROUPS)],
        axis=1)
    return y, tuple(h_next)


def _shift_down(prev8, x, k):
    if k == 0:
        return x
    win = jnp.concatenate([prev8, x], axis=0)
    return pltpu.roll(win, k, 0)[HALO:]


def _shift_up(x, next8, k):
    if k == 0:
        return x
    n = x.shape[0]
    win = jnp.concatenate([x, next8], axis=0)
    return pltpu.roll(win, n + HALO - k, 0)[:n]


def _conv_pre(prev8, x, w, b):
    out = b + x * w[SSM_CONV - 1:SSM_CONV]
    for i in range(SSM_CONV - 1):
        out = out + _shift_down(prev8, x, SSM_CONV - 1 - i) * w[i:i + 1]
    return out


def _swap_halves(x):
    return pltpu.roll(x, HEAD_DIM, 1)


_PAIRS_PER_KV = ATTN_HEADS // ATTN_KV // 2
_ATTN_SCALE = HEAD_DIM ** -0.5


def _parity_lanes(parity):
    lane = lax.broadcasted_iota(jnp.int32, (1, LANES), 1)
    return (lane >= HEAD_DIM * parity) & (lane < HEAD_DIM * (parity + 1))


def _kv_placed(pair, kv_head):
    mine = jnp.where(_parity_lanes(kv_head), pair, 0.0)
    lo = mine if kv_head == 0 else _swap_halves(mine)
    return lo, _swap_halves(lo)


def _kv_unplaced(d_lo, d_hi, kv_head):
    d = jnp.where(_parity_lanes(0), d_lo, 0.0) + _swap_halves(jnp.where(_parity_lanes(1), d_hi, 0.0))
    return d if kv_head == 0 else _swap_halves(d)


def _attn_probs(q4, k_e, sink, first):
    s = _dg(q4, k_e, _NT) * _ATTN_SCALE
    rows = lax.broadcasted_iota(jnp.int32, s.shape, 0) & (CHUNK - 1)
    cols = lax.broadcasted_iota(jnp.int32, s.shape, 1)
    valid = (cols <= rows + CHUNK) & (cols > rows) & (cols >= CHUNK * first.astype(jnp.int32))
    s = jnp.where(valid, s, -jnp.inf)
    m = jnp.maximum(jnp.max(s, axis=-1, keepdims=True), sink)
    p = jnp.exp(s - m)
    e_sink = jnp.exp(sink - m)
    return p, e_sink, jnp.sum(p, axis=-1, keepdims=True) + e_sink


def _lane_column(col, idx):
    lane = lax.broadcasted_iota(jnp.int32, (1, LANES), 1)
    return jnp.where(lane == idx, col, 0.0)


N_CHIP = 4
N_CORE = 2
_OTHER_CHIPS = (2, 4, 6)


@dataclasses.dataclass
class _Move:
    kind: str
    src: jax.Array

    def dst_shape(self):
        s = self.src.shape
        shape = {"gather": (N_DEV,) + s, "gather_ici": (N_CHIP, N_CORE) + s, "gather_d2d": s,
                 "scatter_d2d": (N_CHIP,) + s[2:], "scatter_ici": s}[self.kind]
        return jax.ShapeDtypeStruct(tuple(shape), self.src.dtype)


def _peer(x, y, c, k):
    return (1 - x if k & 4 else x, 1 - y if k & 2 else y, 1 - c if k & 1 else c)


def _move_copies(moves, srcs, dsts, send_sems, recv_sems, local_sems):
    x, y, c = lax.axis_index("x"), lax.axis_index("y"), lax.axis_index("c")
    chip = 2 * x + y
    me = 2 * chip + c
    sibling = (x, y, 1 - c)
    all_chips = pl.ds(0, N_CHIP)
    local, remote = [], []

    def push(n, k, src, dst, device):
        remote.append(pltpu.make_async_remote_copy(
            src_ref=src, dst_ref=dst, send_sem=send_sems.at[n, k], recv_sem=recv_sems.at[n, k],
            device_id=device, device_id_type=pl.DeviceIdType.MESH))

    for n, mv in enumerate(moves):
        s, d = srcs[n], dsts[n]
        if mv.kind == "gather":
            local.append(pltpu.make_async_copy(s, d.at[me], local_sems.at[n]))
            for k in range(1, N_DEV):
                push(n, k - 1, s, d.at[me], _peer(x, y, c, k))
        elif mv.kind == "gather_ici":
            local.append(pltpu.make_async_copy(s, d.at[chip, c], local_sems.at[n]))
            for k in _OTHER_CHIPS:
                push(n, k - 1, s, d.at[chip, c], _peer(x, y, c, k))
        elif mv.kind == "gather_d2d":
            push(n, 0, d.at[all_chips, c], d.at[all_chips, c], sibling)
        elif mv.kind == "scatter_d2d":
            push(n, 0, s.at[all_chips, 1 - c], d, sibling)
        else:
            assert mv.kind == "scatter_ici", mv.kind
            local.append(pltpu.make_async_copy(s.at[chip], d.at[chip], local_sems.at[n]))
            for k in _OTHER_CHIPS:
                px, py, _ = _peer(x, y, c, k)
                push(n, k - 1, s.at[2 * px + py], d.at[chip], (px, py, c))
    return local, remote


def _move_aliases(moves, n_in, n_out):
    return {n_in + n: n_out + n for n, mv in enumerate(moves) if mv.kind == "gather_d2d"}


def _pcall(body, *, name, grid, in_specs, out_specs, out_shape, scratch_shapes=(), semantics=(), moves=(),
           aliases=None):
    out_shape, out_specs = list(out_shape), list(out_specs)
    in_specs = list(in_specs)
    if not moves:
        call = pl.pallas_call(
            body, name=name, grid=grid, in_specs=in_specs, out_specs=out_specs, out_shape=out_shape,
            scratch_shapes=list(scratch_shapes), input_output_aliases=aliases or {},
            compiler_params=pltpu.CompilerParams(dimension_semantics=tuple(semantics),
                                                 vmem_limit_bytes=VMEM_LIMIT_BYTES))
        return (lambda *args: (list(call(*args)), []))
    n_in, n_out, n_scr, n_mv = len(in_specs), len(out_shape), len(scratch_shapes), len(moves)
    hbm = pl.BlockSpec(memory_space=pltpu.HBM)

    def carrier(*refs):
        ins, rest = refs[:n_in], refs[n_in:]
        srcs, rest = rest[:n_mv], rest[n_mv:]
        outs, rest = rest[:n_out], rest[n_out:]
        dsts, rest = rest[:n_mv], rest[n_mv:]
        scr, (send_sems, recv_sems, local_sems) = rest[:n_scr], rest[n_scr:]
        first = functools.reduce(jnp.logical_and, [pl.program_id(d) == 0 for d in range(len(grid))])
        last = functools.reduce(jnp.logical_and, [pl.program_id(d) == grid[d] - 1 for d in range(len(grid))])

        @pl.when(first)
        def _():
            local, remote = _move_copies(moves, srcs, dsts, send_sems, recv_sems, local_sems)
            for cp in local + remote:
                cp.start()

        body(*ins, *outs, *scr)

        @pl.when(last)
        def _():
            local, remote = _move_copies(moves, srcs, dsts, send_sems, recv_sems, local_sems)
            for cp in remote + local:
                cp.wait()

    call = pl.pallas_call(
        carrier, name=name, grid=grid,
        in_specs=in_specs + [hbm] * n_mv,
        out_specs=out_specs + [hbm] * n_mv,
        out_shape=out_shape + [mv.dst_shape() for mv in moves],
        scratch_shapes=list(scratch_shapes) + [pltpu.SemaphoreType.DMA((n_mv, N_DEV - 1)),
                                               pltpu.SemaphoreType.DMA((n_mv, N_DEV - 1)),
                                               pltpu.SemaphoreType.DMA((n_mv,))],
        input_output_aliases={**(aliases or {}), **_move_aliases(moves, n_in, n_out)},
        compiler_params=pltpu.CompilerParams(dimension_semantics=("arbitrary",) * len(grid),
                                             vmem_limit_bytes=VMEM_LIMIT_BYTES))

    def run(*args):
        res = list(call(*args, *[mv.src for mv in moves]))
        return res[:n_out], res[n_out:]

    return run


def _exchange(moves, *, name, then_d2d=()):
    n_mv, n_fwd = len(moves), len(then_d2d)
    hbm = pl.BlockSpec(memory_space=pltpu.HBM)
    copies_of = {"gather": N_DEV - 1, "gather_ici": len(_OTHER_CHIPS), "gather_d2d": 1, "scatter_d2d": 1,
                 "scatter_ici": len(_OTHER_CHIPS)}
    first_copy = [sum(copies_of[mv.kind] for mv in moves[:n]) for n in range(n_mv)]

    def body(*refs):
        srcs, dsts, sems = refs[:n_mv], refs[n_mv:2 * n_mv], refs[2 * n_mv:]
        local, remote = _move_copies(moves, srcs, dsts, *sems[:3])
        for cp in local + remote:
            cp.start()
        x, y, c = lax.axis_index("x"), lax.axis_index("y"), lax.axis_index("c")
        chip, sibling = 2 * x + y, (x, y, 1 - c)
        passed, passed_on = [], set()

        def to_sibling(f, k, src, slot):
            cp = pltpu.make_async_remote_copy(src_ref=src, dst_ref=slot, send_sem=sems[3].at[f, k],
                                              recv_sem=sems[4].at[f, k], device_id=sibling,
                                              device_id_type=pl.DeviceIdType.MESH)
            cp.start()
            passed.append(cp)

        for f, n in enumerate(then_d2d):
            assert moves[n].kind == "gather_ici"
            d = dsts[n]
            to_sibling(f, 0, srcs[n], d.at[chip, c])
            for i, k in enumerate(_OTHER_CHIPS):
                remote[first_copy[n] + i].wait_recv()
                passed_on.add(first_copy[n] + i)
                px, py, _ = _peer(x, y, c, k)
                to_sibling(f, k - 1, d.at[2 * px + py, c], d.at[2 * px + py, c])
        for i, cp in enumerate(remote):
            if i in passed_on:
                cp.wait_send()
            else:
                cp.wait()
        for cp in local + passed:
            cp.wait()

    sems = [pltpu.SemaphoreType.DMA((n_mv, N_DEV - 1)), pltpu.SemaphoreType.DMA((n_mv, N_DEV - 1)),
            pltpu.SemaphoreType.DMA((n_mv,))]
    if then_d2d:
        sems += [pltpu.SemaphoreType.DMA((n_fwd, N_DEV - 1)), pltpu.SemaphoreType.DMA((n_fwd, N_DEV - 1))]
    return list(pl.pallas_call(
        body, name=name, in_specs=[hbm] * n_mv, out_specs=[hbm] * n_mv,
        out_shape=[mv.dst_shape() for mv in moves], scratch_shapes=sems,
    )(*[mv.src for mv in moves]))


TM = 512
FF_SHARD = D_FF // N_DEV


def _whole(a):
    nd = a.ndim
    return pl.BlockSpec(a.shape, lambda i: (0,) * nd)


def _rows(width, col=0):
    return pl.BlockSpec((TM, width), lambda i: (i, col))


def _acc_row(width):
    return pl.BlockSpec((1, width), lambda i: (0, 0))


def _unpack(res_landed, moves, n_out):
    res, landed = res_landed
    res = res[0] if n_out == 1 else res
    return (res, landed) if moves else res


def _norm_matmul(x, g, w, *, name, emit_y, moves=()):
    t, d = x.shape
    n = w.shape[1]

    def body(x_ref, g_ref, w_ref, *outs):
        y = _rmsnorm(x_ref[...], g_ref[...]).astype(BF16)
        if emit_y:
            outs[0][...] = y
        outs[-1][...] = lax.dot_general(y, w_ref[...], _NN, preferred_element_type=F32)

    shapes = ([jax.ShapeDtypeStruct((t, d), BF16)] if emit_y else []) + [jax.ShapeDtypeStruct((t, n), F32)]
    specs = ([_rows(d)] if emit_y else []) + [_rows(n)]
    return _unpack(_pcall(body, name=name, grid=(t // TM,), in_specs=[_rows(d), _acc_row(d), _whole(w)],
                          out_specs=specs, out_shape=shapes, semantics=("parallel",), moves=moves)(x, g, w),
                   moves, len(shapes))


def _residual_matmul(a, w, res, *, name, bias=None, norm_g=None, w_transposed=False, moves=()):
    t, k = a.shape
    n = w.shape[0 if w_transposed else 1]
    contract = _NT if w_transposed else _NN
    has_res, has_bias, has_norm = res is not None, bias is not None, norm_g is not None

    def body(a_ref, w_ref, *rest):
        rest = list(rest)
        res_ref = rest.pop(0) if has_res else None
        b_ref = rest.pop(0) if has_bias else None
        g_ref = rest.pop(0) if has_norm else None
        h = lax.dot_general(a_ref[...].astype(BF16), w_ref[...], contract, preferred_element_type=F32)
        if has_res:
            h = h + res_ref[...]
        if has_bias:
            h = h + b_ref[...]
        rest[0][...] = h
        if has_norm:
            rest[1][...] = _rmsnorm(h, g_ref[...]).astype(BF16)

    rows_in = [res] if has_res else []
    extra = ([bias] if has_bias else []) + ([norm_g] if has_norm else [])
    shapes = [jax.ShapeDtypeStruct((t, n), F32)] + ([jax.ShapeDtypeStruct((t, n), BF16)] if has_norm else [])
    return _unpack(_pcall(body, name=name, grid=(t // TM,),
                          in_specs=[_rows(k), _whole(w)] + [_rows(n)] * len(rows_in) + [_acc_row(n)] * len(extra),
                          out_specs=[_rows(n)] * len(shapes), out_shape=shapes, semantics=("parallel",),
                          moves=moves)(a, w, *rows_in, *extra), moves, len(shapes))


def _mlp_up(y, w_cols, *, name, moves=()):
    t, d = y.shape

    def body(y_ref, w_ref, up_ref):
        yv = y_ref[...]
        for j in range(N_DEV):
            up_ref[:, j * FF_SHARD:(j + 1) * FF_SHARD] = lax.dot_general(
                yv, w_ref[j], _NN, preferred_element_type=F32).astype(up_ref.dtype)

    return _unpack(_pcall(body, name=name, grid=(t // TM,), in_specs=[_rows(d), _whole(w_cols)],
                          out_specs=[_rows(D_FF)], out_shape=[jax.ShapeDtypeStruct((t, D_FF), BF16)],
                          semantics=("parallel",), moves=moves)(y, w_cols), moves, 1)


def _sq_relu(u):
    return jnp.square(jnp.maximum(u.astype(F32), 0.0))


def _down_blocks(w_refs):
    for j in range(N_DEV):
        off = j * FF_SHARD
        for w_ref in w_refs:
            yield off, w_ref.shape[1], w_ref[j]
            off += w_ref.shape[1]


def _mlp_down(up, w_rows, res, *, name, norm_g=None, moves=()):
    t = up.shape[0]
    has_norm = norm_g is not None
    n_w = len(w_rows)

    def body(up_ref, *rest):
        w_refs, res_ref, rest = rest[:n_w], rest[n_w], rest[n_w + 1:]
        h = res_ref[...]
        for off, rows, w_blk in _down_blocks(w_refs):
            act = _sq_relu(up_ref[:, off:off + rows]).astype(BF16)
            h = h + lax.dot_general(act, w_blk, _NN, preferred_element_type=F32)
        if has_norm:
            g_ref, h_ref, y_ref = rest
            y_ref[...] = _rmsnorm(h, g_ref[...]).astype(BF16)
        else:
            (h_ref,) = rest
        h_ref[...] = h

    shapes = [jax.ShapeDtypeStruct((t, D_MODEL), F32)] + ([jax.ShapeDtypeStruct((t, D_MODEL), BF16)] if has_norm else [])
    return _unpack(_pcall(body, name=name, grid=(t // TM,),
                          in_specs=[_rows(D_FF)] + [_whole(w) for w in w_rows] + [_rows(D_MODEL)]
                          + ([_acc_row(D_MODEL)] if has_norm else []),
                          out_specs=[_rows(D_MODEL)] * len(shapes), out_shape=shapes, semantics=("parallel",),
                          moves=moves)(up, *w_rows, res, *([norm_g] if has_norm else [])), moves, len(shapes))


def _mlp_down_dx(dh, w_rows, up, *, name, moves=()):
    t = up.shape[0]
    n_w = len(w_rows)

    def body(dh_ref, *rest):
        w_refs, (up_ref, o_ref) = rest[:n_w], rest[n_w:]
        dhv = dh_ref[...]
        for off, rows, w_blk in _down_blocks(w_refs):
            cols = slice(off, off + rows)
            d_act = lax.dot_general(dhv, w_blk, _NT, preferred_element_type=F32)
            o_ref[:, cols] = (d_act * (2.0 * jnp.maximum(up_ref[:, cols].astype(F32), 0.0))).astype(o_ref.dtype)

    return _unpack(_pcall(body, name=name, grid=(t // TM,),
                          in_specs=[_rows(D_MODEL)] + [_whole(w) for w in w_rows] + [_rows(D_FF)],
                          out_specs=[_rows(D_FF)], out_shape=[jax.ShapeDtypeStruct((t, D_FF), BF16)],
                          semantics=("parallel",), moves=moves)(dh, *w_rows, up), moves, 1)


def _dw_by_cols(x, dy, *, name, tn, by_device=False, moves=()):
    t, k = x.shape
    n = dy.shape[1]
    assert n % tn == 0, (name, n, tn)

    def body(x_ref, dy_ref, o_ref):
        o_ref[...] = lax.dot_general(x_ref[...].astype(BF16), dy_ref[...].astype(BF16), _TN,
                                     preferred_element_type=F32).astype(o_ref.dtype)

    if by_device:
        out_spec, out_shape = pl.BlockSpec((None, k, tn), lambda j: (j, 0, 0)), (n // tn, k, tn)
    else:
        out_spec, out_shape = pl.BlockSpec((k, tn), lambda j: (0, j)), (k, n)
    return _unpack(_pcall(body, name=name, grid=(n // tn,),
                          in_specs=[_whole(x), pl.BlockSpec((t, tn), lambda j: (0, j))],
                          out_specs=[out_spec], out_shape=[jax.ShapeDtypeStruct(out_shape, BF16)],
                          semantics=("parallel",), moves=moves)(x, dy), moves, 1)


def _dw_by_rows(x, dy, *, name, tk, square_relu=False, column_sums=False, moves=()):
    t, k = x.shape
    n = dy.shape[1]
    assert k % tk == 0, (name, k, tk)

    def body(x_ref, dy_ref, o_ref, *sums):
        xv = _sq_relu(x_ref[...]) if square_relu else x_ref[...]
        o_ref[...] = lax.dot_general(xv.astype(BF16), dy_ref[...].astype(BF16), _TN,
                                     preferred_element_type=F32).astype(o_ref.dtype)
        if column_sums:
            sums[0][...] = jnp.sum(xv.astype(F32), axis=0, keepdims=True)

    shapes = [jax.ShapeDtypeStruct((k, n), BF16)] + ([jax.ShapeDtypeStruct((1, k), F32)] if column_sums else [])
    specs = [pl.BlockSpec((tk, n), lambda j: (j, 0))] + ([pl.BlockSpec((1, tk), lambda j: (0, j))] if column_sums else [])
    return _unpack(_pcall(body, name=name, grid=(k // tk,),
                          in_specs=[pl.BlockSpec((t, tk), lambda j: (0, j)), _whole(dy)],
                          out_specs=specs, out_shape=shapes,
                          semantics=("parallel",), moves=moves)(x, dy), moves, len(shapes))


def _dx(dy, w, *, name, partial=None, moves=()):
    t, k = dy.shape
    n = w.shape[0]
    has_partial = partial is not None

    def body(dy_ref, w_ref, *rest):
        out = lax.dot_general(dy_ref[...].astype(BF16), w_ref[...], _NT, preferred_element_type=F32)
        if has_partial:
            out = out + rest[0][...]
        rest[-1][...] = out

    return _unpack(_pcall(body, name=name, grid=(t // TM,),
                          in_specs=[_rows(k), _whole(w)] + ([_rows(n)] if has_partial else []),
                          out_specs=[_rows(n)], out_shape=[jax.ShapeDtypeStruct((t, n), F32)],
                          semantics=("parallel",), moves=moves)(dy, w, *([partial] if has_partial else [])),
                   moves, 1)


def _dx_norm(dy, w, h, g, dres, *, name, partial=None, by_device_cols=False, w_transposed=False, moves=()):
    t, k = dy.shape
    d = h.shape[1]
    has_partial = partial is not None

    def body(dy_ref, w_ref, h_ref, g_ref, dres_ref, *rest):
        if by_device_cols:
            kc = k // N_DEV
            d_y = jnp.zeros((TM, d), F32)
            for j in range(N_DEV):
                d_y = d_y + lax.dot_general(dy_ref[:, j * kc:(j + 1) * kc].astype(BF16), w_ref[j], _NT,
                                            preferred_element_type=F32)
        else:
            d_y = lax.dot_general(dy_ref[...].astype(BF16), w_ref[...], _NN if w_transposed else _NT,
                                  preferred_element_type=F32)
        if has_partial:
            d_y = d_y + rest[0][...]
        dh_ref, dhb_ref, dg_ref, cs_ref = rest[-4:]
        _, vjp = jax.vjp(_rmsnorm, h_ref[...], g_ref[...])
        dh, dg = vjp(d_y)
        dh = dh + dres_ref[...]
        dh_ref[...] = dh
        dhb_ref[...] = dh.astype(BF16)

        @pl.when(pl.program_id(0) == 0)
        def _():
            dg_ref[...] = jnp.zeros_like(dg_ref)
            cs_ref[...] = jnp.zeros_like(cs_ref)

        dg_ref[...] += dg
        cs_ref[...] += jnp.sum(dh, axis=0, keepdims=True)

    shapes = [jax.ShapeDtypeStruct((t, d), F32), jax.ShapeDtypeStruct((t, d), BF16),
              jax.ShapeDtypeStruct((1, d), F32), jax.ShapeDtypeStruct((1, d), F32)]
    return _unpack(_pcall(body, name=name, grid=(t // TM,),
                          in_specs=[_rows(k), _whole(w), _rows(d), _acc_row(d), _rows(d)]
                          + ([_rows(d)] if has_partial else []),
                          out_specs=[_rows(d), _rows(d), _acc_row(d), _acc_row(d)], out_shape=shapes,
                          semantics=("arbitrary",), moves=moves)(dy, w, h, g, dres, *([partial] if has_partial else [])),
                   moves, 4)


def _pair_add(by_core, theirs, core, *, name, tb=512):
    n_chip, _, r, c = by_core.shape
    tb = min(tb, r)
    assert r % tb == 0, (name, r, tb)

    def body(core_ref, a_ref, b_ref, o_ref):
        del core_ref
        o_ref[...] = (a_ref[...].astype(F32) + b_ref[...].astype(F32)).astype(o_ref.dtype)

    blk = pl.BlockSpec((None, tb, c), lambda ch, i, core_ref: (ch, i, 0))
    return pl.pallas_call(
        body, name=name,
        grid_spec=pltpu.PrefetchScalarGridSpec(
            num_scalar_prefetch=1, grid=(n_chip, r // tb),
            in_specs=[pl.BlockSpec((None, None, tb, c), lambda ch, i, core_ref: (ch, core_ref[0], i, 0)), blk],
            out_specs=blk),
        out_shape=jax.ShapeDtypeStruct((n_chip, r, c), by_core.dtype),
        compiler_params=pltpu.CompilerParams(dimension_semantics=("parallel", "parallel"),
                                             vmem_limit_bytes=VMEM_LIMIT_BYTES),
    )(core, by_core, theirs)


def _mlp_down_loss(up, w_rows, res, g, target, *, name):
    t, d = res.shape
    n_w = len(w_rows)

    def body(up_ref, *rest):
        w_refs, (res_ref, g_ref, tgt_ref, loss_ref, dh_ref, dhb_ref, dg_ref) = rest[:n_w], rest[n_w:]
        h = res_ref[...]
        for off, rows, w_blk in _down_blocks(w_refs):
            act = _sq_relu(up_ref[:, off:off + rows]).astype(BF16)
            h = h + lax.dot_general(act, w_blk, _NN, preferred_element_type=F32)

        def f(hh, gg):
            err = jnp.square(_rmsnorm(hh, gg) - tgt_ref[...])
            return 0.5 * jnp.sum(jnp.mean(err, axis=-1, keepdims=True), axis=0, keepdims=True)

        val, vjp = jax.vjp(f, h, g_ref[...])
        dh, dg = vjp(jnp.ones((1, 1), F32))
        dh_ref[...] = dh
        dhb_ref[...] = dh.astype(BF16)

        @pl.when(pl.program_id(0) == 0)
        def _():
            loss_ref[...] = jnp.zeros_like(loss_ref)
            dg_ref[...] = jnp.zeros_like(dg_ref)

        loss_ref[...] += val
        dg_ref[...] += dg

    return _pcall(
        body, name=name, grid=(t // TM,),
        in_specs=[_rows(D_FF)] + [_whole(w) for w in w_rows] + [_rows(d), _acc_row(d), _rows(d)],
        out_specs=[pl.BlockSpec((8, LANES), lambda i: (0, 0)), _rows(d), _rows(d), _acc_row(d)],
        out_shape=[jax.ShapeDtypeStruct((8, LANES), F32), jax.ShapeDtypeStruct((t, d), F32),
                   jax.ShapeDtypeStruct((t, d), BF16), jax.ShapeDtypeStruct((1, d), F32)],
        semantics=("arbitrary",),
    )(up, *w_rows, res, g, target)[0]


def _gmlp_fwd(proj_uv, ln_g, ln_b, w_s, b_s, *, name, moves=()):
    t = proj_uv.shape[0]
    w = D_MODEL

    def body(u_ref, v_ref, g_ref, b_ref, w_ref, bs_ref, o_ref):
        o_ref[...] = _gmlp_chunk(u_ref[...], v_ref[...], g_ref[...], b_ref[...], w_ref[...],
                                 bs_ref[...]).astype(o_ref.dtype)

    row = pl.BlockSpec((1, w), lambda i: (0, 0))
    res, landed = _pcall(
        body, name=name, grid=(t // CHUNK,),
        in_specs=[pl.BlockSpec((CHUNK, w), lambda i: (i, 0)), pl.BlockSpec((CHUNK, w), lambda i: (i, 1)), row, row,
                  pl.BlockSpec((GM_GROUPS, CHUNK, CHUNK), lambda i: (0, 0, 0)),
                  pl.BlockSpec((GM_GROUPS, CHUNK, 1), lambda i: (0, 0, 0))],
        out_specs=[pl.BlockSpec((CHUNK, w), lambda i: (i, 0))],
        out_shape=[jax.ShapeDtypeStruct((t, 2 * w), BF16)],
        semantics=("parallel",), moves=moves,
    )(proj_uv, proj_uv, ln_g, ln_b, w_s, b_s)
    return (res[0], landed) if moves else res[0]


def _gmlp_bwd(proj_uv, d_mix, ln_g, ln_b, w_s, b_s, *, name, moves=()):
    t = proj_uv.shape[0]
    w = D_MODEL

    def body(u_ref, v_ref, da_ref, g_ref, b_ref, w_ref, bs_ref, duv_ref, dg_ref, db_ref, dw_ref, dbs_ref):
        _, vjp = jax.vjp(_gmlp_chunk, u_ref[...], v_ref[...], g_ref[...], b_ref[...], w_ref[...], bs_ref[...])
        du, dv, dg, db, dw, dbs = vjp(da_ref[...])
        duv_ref[:, :w] = du.astype(duv_ref.dtype)
        duv_ref[:, w:] = dv.astype(duv_ref.dtype)

        @pl.when(pl.program_id(0) == 0)
        def _():
            dg_ref[...] = jnp.zeros_like(dg_ref)
            db_ref[...] = jnp.zeros_like(db_ref)
            dw_ref[...] = jnp.zeros_like(dw_ref)
            dbs_ref[...] = jnp.zeros_like(dbs_ref)

        dg_ref[...] += dg
        db_ref[...] += db
        dw_ref[...] += dw
        dbs_ref[...] += dbs

    row = pl.BlockSpec((1, w), lambda i: (0, 0))
    ws = pl.BlockSpec((GM_GROUPS, CHUNK, CHUNK), lambda i: (0, 0, 0))
    bs = pl.BlockSpec((GM_GROUPS, CHUNK, 1), lambda i: (0, 0, 0))
    res, landed = _pcall(
        body, name=name, grid=(t // CHUNK,),
        in_specs=[pl.BlockSpec((CHUNK, w), lambda i: (i, 0)), pl.BlockSpec((CHUNK, w), lambda i: (i, 1)),
                  pl.BlockSpec((CHUNK, w), lambda i: (i, 0)), row, row, ws, bs],
        out_specs=[pl.BlockSpec((CHUNK, 2 * w), lambda i: (i, 0)), row, row, ws, bs],
        out_shape=[jax.ShapeDtypeStruct((t, 2 * w), BF16), jax.ShapeDtypeStruct((1, w), F32),
                   jax.ShapeDtypeStruct((1, w), F32), jax.ShapeDtypeStruct((GM_GROUPS, CHUNK, CHUNK), F32),
                   jax.ShapeDtypeStruct((GM_GROUPS, CHUNK, 1), F32)],
        semantics=("arbitrary",), moves=moves,
    )(proj_uv, proj_uv, d_mix, ln_g, ln_b, w_s, b_s)
    return (res, landed) if moves else res


_HALO_PER_CHUNK = CHUNK // HALO
_DT_BLOCK = (CONV_DIM + D_MODEL) // LANES


def _ssd_fwd(proj_rest, mix, conv_w, conv_b, dt_bias, a_log, d_skip, norm_g, *, name, moves=()):
    t = proj_rest.shape[0]
    nc = t // CHUNK

    def body(x_ref, prev_ref, z_ref, dt_ref, mix_ref, cw_ref, cb_ref, dtb_ref, al_ref, ds_ref, ng_ref, y_ref, hs_ref,
             pre_ref, h_scr):
        del mix_ref
        i = pl.program_id(0)

        @pl.when(i == 0)
        def _():
            h_scr[...] = jnp.zeros_like(h_scr)

        prev8 = jnp.where(i == 0, 0.0, prev_ref[...])
        pre = _conv_pre(prev8, x_ref[...], cw_ref[...], cb_ref[...])
        pre_ref[...] = pre
        hs_ref[0] = h_scr[...]
        h_prev = tuple(h_scr[j] for j in range(_PAIRS))
        y, h_next = _ssd_chunk(pre, z_ref[...], dt_ref[...], h_prev, dtb_ref[...], al_ref[...], ds_ref[...],
                               ng_ref[...])
        y_ref[...] = y.astype(y_ref.dtype)
        for j in range(_PAIRS):
            h_scr[j] = h_next[j]

    small = pl.BlockSpec((1, LANES), lambda i: (0, 0))
    res, landed = _pcall(
        body, name=name, grid=(nc,),
        in_specs=[pl.BlockSpec((CHUNK, CONV_DIM), lambda i: (i, 0)),
                  pl.BlockSpec((HALO, CONV_DIM), lambda i: (jnp.maximum(i * _HALO_PER_CHUNK - 1, 0), 0)),
                  pl.BlockSpec((CHUNK, D_MODEL), lambda i: (i, CONV_DIM // D_MODEL)),
                  pl.BlockSpec((CHUNK, LANES), lambda i: (i, _DT_BLOCK)),
                  pl.BlockSpec(memory_space=pl.ANY),
                  pl.BlockSpec((SSM_CONV, CONV_DIM), lambda i: (0, 0)),
                  pl.BlockSpec((1, CONV_DIM), lambda i: (0, 0)),
                  small, small, small, pl.BlockSpec((1, D_MODEL), lambda i: (0, 0))],
        out_specs=[pl.BlockSpec((CHUNK, D_MODEL), lambda i: (i, 1)),
                   pl.BlockSpec((1, _PAIRS, SSM_STATE, LANES), lambda i: (i, 0, 0, 0)),
                   pl.BlockSpec((CHUNK, CONV_DIM), lambda i: (i, 0))],
        out_shape=[jax.ShapeDtypeStruct((t, 2 * D_MODEL), BF16),
                   jax.ShapeDtypeStruct((nc, _PAIRS, SSM_STATE, LANES), F32),
                   jax.ShapeDtypeStruct((t, CONV_DIM), F32)],
        scratch_shapes=[pltpu.VMEM((_PAIRS, SSM_STATE, LANES), F32)],
        semantics=("arbitrary",), moves=moves, aliases={4: 0},
    )(proj_rest, proj_rest, proj_rest, proj_rest, mix, conv_w, conv_b, dt_bias, a_log, d_skip, norm_g)
    return (res, landed) if moves else res


def _ssd_bwd(proj_rest, pre, h_states, d_mix, dt_bias, a_log, d_skip, norm_g, *, name, moves=()):
    t = proj_rest.shape[0]
    nc = t // CHUNK

    def body(pre_ref, z_ref, dt_ref, hs_ref, dy_ref, dtb_ref, al_ref, ds_ref, ng_ref,
             dpre_ref, dz_ref, ddt_ref, ddtb_ref, dal_ref, dds_ref, dng_ref, dh_scr):
        i = pl.program_id(0)

        @pl.when(i == 0)
        def _():
            dh_scr[...] = jnp.zeros_like(dh_scr)
            ddtb_ref[...] = jnp.zeros_like(ddtb_ref)
            dal_ref[...] = jnp.zeros_like(dal_ref)
            dds_ref[...] = jnp.zeros_like(dds_ref)
            dng_ref[...] = jnp.zeros_like(dng_ref)

        h_prev = tuple(hs_ref[0, j] for j in range(_PAIRS))
        _, vjp = jax.vjp(_ssd_chunk, pre_ref[...], z_ref[...], dt_ref[...], h_prev, dtb_ref[...], al_ref[...],
                         ds_ref[...], ng_ref[...])
        dpre, dz, ddt, dh_prev, ddtb, dal, dds, dng = vjp((dy_ref[...], tuple(dh_scr[j] for j in range(_PAIRS))))
        dpre_ref[...] = dpre
        dz_ref[...] = dz.astype(dz_ref.dtype)
        ddt_ref[...] = ddt.astype(ddt_ref.dtype)
        for j in range(_PAIRS):
            dh_scr[j] = dh_prev[j]
        ddtb_ref[...] += ddtb
        dal_ref[...] += dal
        dds_ref[...] += dds
        dng_ref[...] += dng

    rev = lambda i: nc - 1 - i
    small = pl.BlockSpec((1, LANES), lambda i: (0, 0))
    wide = pl.BlockSpec((1, D_MODEL), lambda i: (0, 0))
    res, landed = _pcall(
        body, name=name, grid=(nc,),
        in_specs=[pl.BlockSpec((CHUNK, CONV_DIM), lambda i: (rev(i), 0)),
                  pl.BlockSpec((CHUNK, D_MODEL), lambda i: (rev(i), CONV_DIM // D_MODEL)),
                  pl.BlockSpec((CHUNK, LANES), lambda i: (rev(i), _DT_BLOCK)),
                  pl.BlockSpec((1, _PAIRS, SSM_STATE, LANES), lambda i: (rev(i), 0, 0, 0)),
                  pl.BlockSpec((CHUNK, D_MODEL), lambda i: (rev(i), 1)),
                  small, small, small, wide],
        out_specs=[pl.BlockSpec((CHUNK, CONV_DIM), lambda i: (rev(i), 0)),
                   pl.BlockSpec((CHUNK, D_MODEL), lambda i: (rev(i), 0)),
                   pl.BlockSpec((CHUNK, LANES), lambda i: (rev(i), 0)),
                   small, small, small, wide],
        out_shape=[jax.ShapeDtypeStruct((t, CONV_DIM), F32), jax.ShapeDtypeStruct((t, D_MODEL), BF16),
                   jax.ShapeDtypeStruct((t, LANES), BF16),
                   jax.ShapeDtypeStruct((1, LANES), F32), jax.ShapeDtypeStruct((1, LANES), F32),
                   jax.ShapeDtypeStruct((1, LANES), F32), jax.ShapeDtypeStruct((1, D_MODEL), F32)],
        scratch_shapes=[pltpu.VMEM((_PAIRS, SSM_STATE, LANES), F32)],
        semantics=("arbitrary",), moves=moves,
    )(pre, proj_rest, proj_rest, h_states, d_mix, dt_bias, a_log, d_skip, norm_g)
    return (res, landed) if moves else res


def _conv_bwd(proj_rest, dpre, dz, ddt, conv_w, *, name, tb=256, moves=()):
    t = proj_rest.shape[0]
    nb = t // tb
    per = tb // HALO

    def body(x_ref, prev_ref, dpre_ref, next_ref, dz_ref, ddt_ref, cw_ref, drest_ref, dcw_ref, dcb_ref):
        i = pl.program_id(0)

        @pl.when(i == 0)
        def _():
            dcw_ref[...] = jnp.zeros_like(dcw_ref)
            dcb_ref[...] = jnp.zeros_like(dcb_ref)

        x = x_ref[...]
        dp = dpre_ref[...]
        w = cw_ref[...]
        prev8 = jnp.where(i == 0, 0.0, prev_ref[...])
        next8 = jnp.where(i == nb - 1, 0.0, next_ref[...])
        dx = dp * w[SSM_CONV - 1:SSM_CONV]
        for j in range(SSM_CONV - 1):
            dx = dx + _shift_up(dp, next8, SSM_CONV - 1 - j) * w[j:j + 1]
        drest_ref[:, :CONV_DIM] = dx.astype(drest_ref.dtype)
        drest_ref[:, CONV_DIM:CONV_DIM + D_MODEL] = dz_ref[...].astype(drest_ref.dtype)
        drest_ref[:, CONV_DIM + D_MODEL:] = ddt_ref[...].astype(drest_ref.dtype)
        for j in range(SSM_CONV):
            dcw_ref[j:j + 1, :] += jnp.sum(dp * _shift_down(prev8, x, SSM_CONV - 1 - j), axis=0, keepdims=True)
        dcb_ref[...] += jnp.sum(dp, axis=0, keepdims=True)

    res, landed = _pcall(
        body, name=name, grid=(nb,),
        in_specs=[pl.BlockSpec((tb, CONV_DIM), lambda i: (i, 0)),
                  pl.BlockSpec((HALO, CONV_DIM), lambda i: (jnp.maximum(i * per - 1, 0), 0)),
                  pl.BlockSpec((tb, CONV_DIM), lambda i: (i, 0)),
                  pl.BlockSpec((HALO, CONV_DIM), lambda i: (jnp.minimum((i + 1) * per, nb * per - 1), 0)),
                  pl.BlockSpec((tb, D_MODEL), lambda i: (i, 0)),
                  pl.BlockSpec((tb, LANES), lambda i: (i, 0)),
                  pl.BlockSpec((SSM_CONV, CONV_DIM), lambda i: (0, 0))],
        out_specs=[pl.BlockSpec((tb, REST_W), lambda i: (i, 0)),
                   pl.BlockSpec((SSM_CONV, CONV_DIM), lambda i: (0, 0)),
                   pl.BlockSpec((1, CONV_DIM), lambda i: (0, 0))],
        out_shape=[jax.ShapeDtypeStruct((t, REST_W), BF16), jax.ShapeDtypeStruct((SSM_CONV, CONV_DIM), F32),
                   jax.ShapeDtypeStruct((1, CONV_DIM), F32)],
        semantics=("arbitrary",), moves=moves,
    )(proj_rest, proj_rest, dpre, dpre, dz, ddt, conv_w)
    return (res, landed) if moves else res


_KV_BLOCK = D_MODEL // (2 * LANES)
_SINK_ROWS = _PAIRS_PER_KV * CHUNK


def _stack_pairs(ref, kv_head):
    base = kv_head * _PAIRS_PER_KV
    return jnp.concatenate([ref[:, (base + p) * LANES:(base + p + 1) * LANES] for p in range(_PAIRS_PER_KV)], axis=0)


def _attn_fwd(qkv, sinks, *, name, moves=()):
    t = qkv.shape[0]
    nb = t // CHUNK

    def body(q_ref, kvp_ref, kvc_ref, s_ref, o_ref, p_ref, st_ref):
        first = pl.program_id(0) == 0
        kv = jnp.concatenate([kvp_ref[...], kvc_ref[...]], axis=0)
        stats = jnp.zeros((_SINK_ROWS, LANES), F32)
        for j in range(ATTN_KV):
            q4 = _stack_pairs(q_ref, j)
            ks, vs = _kv_placed(kv[:, :LANES], j), _kv_placed(kv[:, LANES:], j)
            out = None
            for e in range(2):
                p, e_sink, den = _attn_probs(q4, ks[e], s_ref[j, e], first)
                inv = 1.0 / den
                o = _dg(p, vs[e], _NN) * inv
                out = o if out is None else out + o
                p_ref[0, 2 * j + e] = p.astype(p_ref.dtype)
                stats = stats + _lane_column(inv, 2 * j + e) + _lane_column(e_sink, 4 + 2 * j + e)
            for pair in range(_PAIRS_PER_KV):
                col = (j * _PAIRS_PER_KV + pair) * LANES
                o_ref[:, col:col + LANES] = out[pair * CHUNK:(pair + 1) * CHUNK].astype(o_ref.dtype)
        st_ref[0] = stats

    return _unpack(_pcall(
        body, name=name, grid=(nb,),
        in_specs=[pl.BlockSpec((CHUNK, D_MODEL), lambda i: (i, 0)),
                  pl.BlockSpec((CHUNK, 2 * LANES), lambda i: (jnp.maximum(i - 1, 0), _KV_BLOCK)),
                  pl.BlockSpec((CHUNK, 2 * LANES), lambda i: (i, _KV_BLOCK)),
                  pl.BlockSpec((ATTN_KV, 2, _SINK_ROWS, 1), lambda i: (0, 0, 0, 0))],
        out_specs=[pl.BlockSpec((CHUNK, D_MODEL), lambda i: (i, 0)),
                   pl.BlockSpec((1, 2 * ATTN_KV, _SINK_ROWS, 2 * CHUNK), lambda i: (i, 0, 0, 0)),
                   pl.BlockSpec((1, _SINK_ROWS, LANES), lambda i: (i, 0, 0))],
        out_shape=[jax.ShapeDtypeStruct((t, D_MODEL), BF16),
                   jax.ShapeDtypeStruct((nb, 2 * ATTN_KV, _SINK_ROWS, 2 * CHUNK), BF16),
                   jax.ShapeDtypeStruct((nb, _SINK_ROWS, LANES), F32)],
        semantics=("parallel",), moves=moves,
    )(qkv, qkv, qkv, sinks), moves, 3)


def _attn_bwd(qkv, probs, stats, attn, d_o, *, name, moves=()):
    t = qkv.shape[0]
    nb = t // CHUNK

    def body(q_ref, kvp_ref, kvc_ref, p_ref, st_ref, o_ref, do_ref, dqkv_ref, ds_ref, dkv_scr):
        @pl.when(pl.program_id(0) == 0)
        def _():
            dkv_scr[...] = jnp.zeros_like(dkv_scr)
            ds_ref[...] = jnp.zeros_like(ds_ref)

        kv = jnp.concatenate([kvp_ref[...], kvc_ref[...]], axis=0)
        table = st_ref[0]
        d_k = jnp.zeros((2 * CHUNK, LANES), F32)
        d_v = jnp.zeros((2 * CHUNK, LANES), F32)
        for j in range(ATTN_KV):
            q4, do4, o4 = _stack_pairs(q_ref, j), _stack_pairs(do_ref, j), _stack_pairs(o_ref, j).astype(F32)
            ks, vs = _kv_placed(kv[:, :LANES], j), _kv_placed(kv[:, LANES:], j)
            dq4, dk, dv = None, [], []
            for e in range(2):
                p = p_ref[0, 2 * j + e].astype(F32)
                inv, e_sink = _col_pick(table, 2 * j + e), _col_pick(table, 4 + 2 * j + e)
                do_e = jnp.where(_parity_lanes(e), do4, 0.0)
                d_num = do_e * inv
                d_den = -jnp.sum(do_e * o4, axis=1, keepdims=True) * inv
                ds = p * (_dg(d_num, vs[e], _NT) + d_den)
                ds_ref[j, e] += d_den * e_sink
                dq = _dg(ds, ks[e], _NN) * _ATTN_SCALE
                dq4 = dq if dq4 is None else dq4 + dq
                dk.append(_dg(ds, q4, _TN) * _ATTN_SCALE)
                dv.append(_dg(p, d_num, _TN))
            for pair in range(_PAIRS_PER_KV):
                col = (j * _PAIRS_PER_KV + pair) * LANES
                dqkv_ref[:, col:col + LANES] = dq4[pair * CHUNK:(pair + 1) * CHUNK]
            d_k = d_k + _kv_unplaced(dk[0], dk[1], j)
            d_v = d_v + _kv_unplaced(dv[0], dv[1], j)
        d_kv = jnp.concatenate([d_k, d_v], axis=1)
        dqkv_ref[:, D_MODEL:] = d_kv[CHUNK:] + dkv_scr[...]
        dkv_scr[...] = d_kv[:CHUNK]

    cur = lambda i: (nb - 1 - i, 0)
    sk = pl.BlockSpec((ATTN_KV, 2, _SINK_ROWS, 1), lambda i: (0, 0, 0, 0))
    res, landed = _pcall(
        body, name=name, grid=(nb,),
        in_specs=[pl.BlockSpec((CHUNK, D_MODEL), cur),
                  pl.BlockSpec((CHUNK, 2 * LANES), lambda i: (jnp.maximum(nb - 2 - i, 0), _KV_BLOCK)),
                  pl.BlockSpec((CHUNK, 2 * LANES), lambda i: (nb - 1 - i, _KV_BLOCK)),
                  pl.BlockSpec((1, 2 * ATTN_KV, _SINK_ROWS, 2 * CHUNK), lambda i: (nb - 1 - i, 0, 0, 0)),
                  pl.BlockSpec((1, _SINK_ROWS, LANES), lambda i: (nb - 1 - i, 0, 0)),
                  pl.BlockSpec((CHUNK, D_MODEL), cur), pl.BlockSpec((CHUNK, D_MODEL), cur)],
        out_specs=[pl.BlockSpec((CHUNK, QKV_DIM), cur), sk],
        out_shape=[jax.ShapeDtypeStruct((t, QKV_DIM), F32), jax.ShapeDtypeStruct((ATTN_KV, 2, _SINK_ROWS, 1), F32)],
        scratch_shapes=[pltpu.VMEM((CHUNK, 2 * LANES), F32)],
        semantics=("arbitrary",), moves=moves,
    )(qkv, qkv, qkv, probs, stats, attn, d_o)
    return (res, landed) if moves else res


def _adamw(parts, w, m, v, *, name, tb=512, moves=()):
    layers, r, c = w.shape
    n = parts[0].shape[0]
    tb = min(tb, r)
    assert r % tb == 0 and len(parts) == layers, (name, r, tb)
    nb = r // tb

    def body(*refs):
        p_refs = refs[:layers]
        w_ref, m_ref, v_ref, g_ref, d_ref, nm_ref, nv_ref = refs[layers:]
        for layer in range(layers):
            @pl.when(pl.program_id(0) == layer)
            def _(p_ref=p_refs[layer]):
                g = p_ref[0].astype(F32)
                for s in range(1, n):
                    g = g + p_ref[s].astype(F32)
                m_new = ADAM_B1 * m_ref[...] + (1.0 - ADAM_B1) * g
                v_new = ADAM_B2 * v_ref[...] + (1.0 - ADAM_B2) * jnp.square(g)
                m_hat = m_new / (1.0 - ADAM_B1 ** ADAM_STEP)
                v_hat = v_new / (1.0 - ADAM_B2 ** ADAM_STEP)
                g_ref[...] = g
                d_ref[...] = -ADAM_LR * (m_hat / (jnp.sqrt(v_hat) + ADAM_EPS) + ADAM_WD * w_ref[...])
                nm_ref[...] = m_new
                nv_ref[...] = v_new

    part_spec = lambda layer: pl.BlockSpec(
        (n, tb, c), lambda l, i: (0, jnp.clip(i + (l - layer) * nb, 0, nb - 1), 0))
    blk = pl.BlockSpec((None, tb, c), lambda l, i: (l, i, 0))
    res, landed = _pcall(
        body, name=name, grid=(layers, nb),
        in_specs=[part_spec(layer) for layer in range(layers)] + [blk, blk, blk],
        out_specs=[blk] * 4,
        out_shape=[jax.ShapeDtypeStruct((layers, r, c), F32)] * 4,
        semantics=("arbitrary", "arbitrary"), moves=moves,
    )(*parts, w, m, v)
    return (res, landed) if moves else res


def _as_rows(a):
    flat = a.reshape(-1)
    pad = (-flat.shape[0]) % PACK_W
    if pad:
        flat = jnp.pad(flat, (0, pad))
    return flat.reshape(-1, PACK_W)


def _cols_from_shards(g):
    return jnp.transpose(g, (1, 0, 2)).reshape(g.shape[1], -1)


def _shard_cols(shards, lo, hi):
    c = shards.shape[2]
    pieces = []
    for j in range(shards.shape[0]):
        a, b = max(lo, j * c), min(hi, (j + 1) * c)
        if a < b:
            pieces.append(shards[j, :, a - j * c:b - j * c])
    return pieces


def _cols_of(sources, lo, hi):
    pieces = []
    for arr, col0, first, last in sources:
        a, b = max(lo, first), min(hi, last)
        if a < b:
            pieces.append(arr[:, col0 + a - first:col0 + b - first])
    return pieces


def _pad_lanes(a):
    return jnp.pad(a, ((0, 0), (0, LANES - a.shape[1])))


def kernel(x, norm_mix_g, norm_mlp_g, final_norm_g, w_in_even, w_out_even, gm_ln_g, gm_ln_b, gm_w_s, gm_b_s, ssm_conv_w, ssm_conv_b, ssm_dt_bias, ssm_a_log, ssm_d, ssm_norm_g, w_qkv, b_qkv, w_o, b_o, attn_sinks, w_up, w_down, loss_target, m_norm_mix_g, m_norm_mlp_g, m_final_norm_g, m_w_in_even, m_w_out_even, m_gm_ln_g, m_gm_ln_b, m_gm_w_s, m_gm_b_s, m_ssm_conv_w, m_ssm_conv_b, m_ssm_dt_bias, m_ssm_a_log, m_ssm_d, m_ssm_norm_g, m_w_qkv, m_b_qkv, m_w_o, m_b_o, m_attn_sinks, m_w_up, m_w_down, v_norm_mix_g, v_norm_mlp_g, v_final_norm_g, v_w_in_even, v_w_out_even, v_gm_ln_g, v_gm_ln_b, v_gm_w_s, v_gm_b_s, v_ssm_conv_w, v_ssm_conv_b, v_ssm_dt_bias, v_ssm_a_log, v_ssm_d, v_ssm_norm_g, v_w_qkv, v_b_qkv, v_w_o, v_b_o, v_attn_sinks, v_w_up, v_w_down):
    names = ["norm_mix_g", "norm_mlp_g", "final_norm_g", "w_in_even", "w_out_even", "gm_ln_g", "gm_ln_b", "gm_w_s",
             "gm_b_s", "ssm_conv_w", "ssm_conv_b", "ssm_dt_bias", "ssm_a_log", "ssm_d", "ssm_norm_g", "w_qkv",
             "b_qkv", "w_o", "b_o", "attn_sinks", "w_up", "w_down"]
    env = locals()
    W = {n: env[n] for n in names}
    M = {n: env["m_" + n] for n in names}
    V = {n: env["v_" + n] for n in names}
    big = ["w_in_even", "w_out_even", "w_qkv", "w_o", "w_up", "w_down"]
    small_sharded = ["ssm_conv_w", "b_qkv", "b_o"]
    replicated = [n for n in names if n not in big and n not in small_sharded]
    me = 4 * lax.axis_index("x") + 2 * lax.axis_index("y") + lax.axis_index("c")
    t = x.shape[1]
    xs = x.reshape(t, D_MODEL)
    target = loss_target.reshape(t, D_MODEL)
    gather = lambda a: _Move("gather", a)
    over_ici = lambda a: _Move("gather_ici", a)
    over_d2d = lambda a: _Move("gather_d2d", a)
    by_core = lambda a: a.reshape((N_CHIP, N_CORE) + a.shape[1:])
    to_sibling = lambda a: [_Move("scatter_d2d", by_core(a))]
    my_core = lax.axis_index("c").astype(jnp.int32).reshape(1)
    pair = lambda a, theirs, name: _pair_add(by_core(a), theirs, my_core, name=name)
    to_chips = lambda a: _Move("scatter_ici", a)
    whole = lambda a: a.reshape((N_DEV,) + a.shape[2:])
    row = lambda a: a.reshape(1, D_MODEL)

    small_flat = jnp.concatenate([W[n].reshape(-1) for n in small_sharded])
    w_in_g, small_g = _exchange([over_ici(w_in_even[0].astype(BF16)), gather(_as_rows(small_flat))],
                                name="gather_w_in", then_d2d=[0])
    w_in_s = whole(w_in_g)
    z_lo, xbc_lo, dt_lo = 2 * D_MODEL, 3 * D_MODEL, 3 * D_MODEL + CONV_DIM
    w_uv = jnp.concatenate(_shard_cols(w_in_s, 0, z_lo), axis=1)
    w_rest = jnp.concatenate(_shard_cols(w_in_s, xbc_lo, dt_lo) + _shard_cols(w_in_s, z_lo, xbc_lo)
                             + _shard_cols(w_in_s, dt_lo, IN_EVEN)
                             + [jnp.zeros((D_MODEL, LANES - SSM_HEADS), BF16)], axis=1)
    small_all = small_g.reshape(N_DEV, -1)
    n_cw = SSM_CONV * CONV_DIM // N_DEV
    n_bq = QKV_DIM // N_DEV
    conv_w = _cols_from_shards(small_all[:, :n_cw].reshape(N_DEV, SSM_CONV, CONV_DIM // N_DEV))
    bqkv = small_all[:, n_cw:n_cw + n_bq].reshape(1, QKV_DIM)
    bo = small_all[:, n_cw + n_bq:n_cw + n_bq + D_MODEL // N_DEV].reshape(1, D_MODEL)

    conv_b = ssm_conv_b.reshape(1, CONV_DIM)
    dt_bias, a_log, d_skip = _pad_lanes(ssm_dt_bias), _pad_lanes(ssm_a_log), _pad_lanes(ssm_d)
    gm_w = gm_w_s[0]
    gm_b = gm_b_s[0].reshape(GM_GROUPS, CHUNK, 1)
    sink_rows = jnp.repeat(jnp.transpose(attn_sinks.reshape(ATTN_KV, _PAIRS_PER_KV, 2), (0, 2, 1)), CHUNK,
                           axis=2).reshape(ATTN_KV, 2, _SINK_ROWS, 1)
    w_up_b, w_down_b = w_up.astype(BF16), w_down.astype(BF16)

    w_down0_a, w_down0_b = w_down_b[0, :FF_SHARD // 2], w_down_b[0, FF_SHARD // 2:]
    (y0, proj_uv), (w_qkv_g,) = _norm_matmul(xs, row(norm_mix_g[0]), w_uv, name="proj_uv", emit_y=True,
                                             moves=[over_ici(jnp.transpose(w_qkv[0]).astype(BF16))])
    proj_rest, (w_out_g,) = _norm_matmul(xs, row(norm_mix_g[0]), w_rest, name="proj_rest", emit_y=False,
                                         moves=[over_ici(w_out_even[0].astype(BF16))])
    mix, (w_down0_a, w_out_g, w_qkv_g) = _gmlp_fwd(
        proj_uv, gm_ln_g, gm_ln_b, gm_w, gm_b, name="gmlp_fwd",
        moves=[over_ici(w_down0_a), over_d2d(w_out_g), over_d2d(w_qkv_g)])
    (mix, h_states, conv_pre), (w_up0_g, w_down0_a) = _ssd_fwd(
        proj_rest, mix, conv_w, conv_b, dt_bias, a_log, d_skip, ssm_norm_g, name="ssd_fwd",
        moves=[over_ici(w_up_b[0]), over_d2d(w_down0_a)])
    w_out_f = whole(w_out_g).reshape(2 * D_MODEL, D_MODEL)
    (h1, y1), (w_down0_b, w_up0_g) = _residual_matmul(
        mix, w_out_f, xs, name="mix_out", norm_g=row(norm_mlp_g[0]),
        moves=[over_ici(w_down0_b), over_d2d(w_up0_g)])
    up0, (w_down0_b,) = _mlp_up(y1, whole(w_up0_g), name="mlp_up0", moves=[over_d2d(w_down0_b)])
    w_down_g = [[whole(w_down0_a), whole(w_down0_b)]]
    (h2, y2), (w_o_g,) = _mlp_down(up0, w_down_g[0], h1, name="mlp_down0", norm_g=row(norm_mix_g[1]),
                                   moves=[over_ici(w_o[0].astype(BF16))])
    wqkv = whole(w_qkv_g).reshape(QKV_DIM, D_MODEL)
    qkv, (w_o_g,) = _residual_matmul(y2, wqkv, None, name="qkv", bias=bqkv, w_transposed=True,
                                     moves=[over_d2d(w_o_g)])
    wo = whole(w_o_g).reshape(D_MODEL, D_MODEL)
    (attn, attn_p, attn_stats), (w_up1_g, w_down1_g) = _attn_fwd(
        qkv, sink_rows, name="attn_fwd", moves=[over_ici(w_up_b[1]), over_ici(w_down_b[1])])
    (h3, y3), (w_up1_g,) = _residual_matmul(attn, wo, h2, name="attn_out", bias=bo, norm_g=row(norm_mlp_g[1]),
                                            moves=[over_d2d(w_up1_g)])
    w_up_g = [whole(w_up0_g), whole(w_up1_g)]
    up1, (w_down1_g,) = _mlp_up(y3, w_up_g[1], name="mlp_up1", moves=[over_d2d(w_down1_g)])
    w_down_g.append([whole(w_down1_g)])
    loss_part, dh4, dh4_b, d_final_g = _mlp_down_loss(up1, w_down_g[1], h3, row(final_norm_g), target,
                                                      name="mlp_down1_loss")

    by_dev_rows = lambda a: a.reshape((N_DEV, a.shape[0] // N_DEV) + a.shape[1:])

    def mlp_bwd(dh, dh_b, h, y, up, layer, first_moves=()):
        res = _mlp_down_dx(dh_b, w_down_g[layer], up, name=f"mlp_down_dx{layer}", moves=first_moves)
        d_up, first_landed = res if first_moves else (res, [])
        g_down = _dw_by_rows(up, dh_b, name=f"mlp_down_dw{layer}", tk=FF_SHARD, square_relu=True)
        g_down = by_dev_rows(g_down)
        g_up, (theirs,) = _dw_by_cols(y, d_up, name=f"mlp_up_dw{layer}", tn=FF_SHARD, by_device=True,
                                      moves=to_sibling(g_down))
        q_down = pair(g_down, theirs, f"mlp_down_pair{layer}")
        (dh_new, dh_new_b, dg, cs), (theirs,) = _dx_norm(
            d_up, w_up_g[layer], h, row(norm_mlp_g[layer]), dh, name=f"mlp_up_dx{layer}", by_device_cols=True,
            moves=to_sibling(g_up))
        q_up = pair(g_up, theirs, f"mlp_up_pair{layer}")
        return dh_new, dh_new_b, cs, dg, q_up, q_down, first_landed

    dh3, dh3_b, cs3, g_nmlp1, q_up1, q_down1, _ = mlp_bwd(dh4, dh4_b, h3, y3, up1, 1)
    g_bo = cs3
    g_wo = by_dev_rows(_dw_by_cols(attn, dh3_b, name="attn_out_dw", tn=FF_SHARD))
    d_attn, (theirs,) = _dx(dh3_b, wo, name="attn_out_dx", moves=to_sibling(g_wo))
    q_wo = pair(g_wo, theirs, "attn_out_pair")
    (dqkv, d_sink), (r_down1, r_up1) = _attn_bwd(qkv, attn_p, attn_stats, attn, d_attn, name="attn_bwd",
                                                 moves=[to_chips(q_down1), to_chips(q_up1)])
    g_wqkv, g_bqkv = _dw_by_rows(dqkv, y2, name="qkv_dw", tk=QKV_DIM // 2, column_sums=True)
    g_wqkv = by_dev_rows(g_wqkv)
    (dh2, dh2_b, g_nmix1, _), (theirs,) = _dx_norm(dqkv, wqkv, h2, row(norm_mix_g[1]), dh3, name="qkv_dx",
                                                   w_transposed=True, moves=to_sibling(g_wqkv))
    q_wqkv = pair(g_wqkv, theirs, "qkv_pair")
    dh1, dh1_b, _, g_nmlp0, q_up0, q_down0, (r_wqkv, r_wo) = mlp_bwd(
        dh2, dh2_b, h1, y1, up0, 0, first_moves=[to_chips(q_wqkv), to_chips(q_wo)])

    d_mix = _dx(dh1_b, w_out_f, name="mix_out_dx")
    g_wout = by_dev_rows(_dw_by_rows(mix, dh1_b, name="mix_out_dw", tk=FF_SHARD))
    (d_uv, g_ln_g, g_ln_b, g_gm_w, g_gm_b), (r_down0, theirs) = _gmlp_bwd(
        proj_uv, d_mix, gm_ln_g, gm_ln_b, gm_w, gm_b, name="gmlp_bwd", moves=[to_chips(q_down0)] + to_sibling(g_wout))
    q_wout = pair(g_wout, theirs, "mix_out_pair")

    early = [("norm_mlp_g", None), ("final_norm_g", None), ("norm_mix_g", 1), ("gm_ln_g", None), ("gm_ln_b", None),
             ("gm_w_s", None), ("gm_b_s", None), ("attn_sinks", None)]
    late = [("norm_mix_g", 0), ("ssm_conv_b", None), ("ssm_dt_bias", None), ("ssm_a_log", None), ("ssm_d", None),
            ("ssm_norm_g", None)]
    early_sharded, late_sharded = ["b_qkv", "b_o"], ["ssm_conv_w"]
    small_grads = {
        ("norm_mlp_g", None): jnp.concatenate([g_nmlp0, g_nmlp1], axis=0),
        ("final_norm_g", None): d_final_g, ("norm_mix_g", 1): g_nmix1,
        ("gm_ln_g", None): g_ln_g, ("gm_ln_b", None): g_ln_b, ("gm_w_s", None): g_gm_w, ("gm_b_s", None): g_gm_b,
        ("attn_sinks", None): jnp.transpose(
            jnp.sum(d_sink.reshape(ATTN_KV, 2, _PAIRS_PER_KV, CHUNK), axis=3), (0, 2, 1)),
        "b_qkv": g_bqkv, "b_o": g_bo,
    }
    pack = lambda keys: _as_rows(jnp.concatenate([small_grads[key].reshape(-1) for key in keys]))
    (dpre, dz, ddt, g_dtb, g_alog, g_dskip, g_ssm_ng), (r_up0, r_wout, early_recv) = _ssd_bwd(
        proj_rest, conv_pre, h_states, d_mix, dt_bias, a_log, d_skip, ssm_norm_g, name="ssd_bwd",
        moves=[to_chips(q_up0), to_chips(q_wout), gather(pack(early + early_sharded))])
    d_rest, g_conv_w, g_conv_b = _conv_bwd(proj_rest, dpre, dz, ddt, conv_w, name="conv_bwd")
    g_w_uv = _dw_by_cols(y0, d_uv, name="proj_uv_dw", tn=FF_SHARD)
    g_w_rest = _dw_by_cols(y0, d_rest, name="proj_rest_dw", tn=REST_W // 5)
    in_cols = [(g_w_uv, 0, 0, z_lo), (g_w_rest, CONV_DIM, z_lo, xbc_lo), (g_w_rest, 0, xbc_lo, dt_lo),
               (g_w_rest, CONV_DIM + D_MODEL, dt_lo, IN_EVEN)]
    in_shard = IN_EVEN // N_DEV
    g_w_in = jnp.stack([jnp.concatenate(_cols_of(in_cols, j * in_shard, (j + 1) * in_shard), axis=1)
                        for j in range(N_DEV)])
    dy0, (theirs,) = _dx(d_uv, w_uv, name="proj_uv_dx", moves=to_sibling(g_w_in))
    q_w_in = pair(g_w_in, theirs, "proj_pair")
    (dx, _, g_nmix0, _), r_w_in = _dx_norm(d_rest, w_rest, xs, row(norm_mix_g[0]), dh1, name="proj_rest_dx",
                                           partial=dy0, moves=[to_chips(q_w_in)])
    small_grads.update({
        ("loss", None): loss_part[:1, :1],
        ("norm_mix_g", 0): g_nmix0, ("ssm_conv_b", None): g_conv_b,
        ("ssm_dt_bias", None): g_dtb[:, :SSM_HEADS], ("ssm_a_log", None): g_alog[:, :SSM_HEADS],
        ("ssm_d", None): g_dskip[:, :SSM_HEADS], ("ssm_norm_g", None): g_ssm_ng, "ssm_conv_w": g_conv_w,
    })


    def update(n, parts, moves=(), transposed=False):
        shape = W[n].shape
        if transposed:
            as3 = lambda a: jnp.transpose(a[0])[None]
            back = lambda a: jnp.transpose(a[0])[None]
        else:
            as3 = lambda a: a.reshape((len(parts),) + parts[0].shape[1:])
            back = lambda a: a.reshape(shape)
        res = _adamw(parts, as3(W[n]), as3(M[n]), as3(V[n]), name="adamw_" + n, moves=moves)
        res, landed = res if moves else (res, [])
        return [back(a) for a in res], landed

    out = {}
    late_keys = late + late_sharded + [("loss", None)]
    out["w_o"], (late_recv,) = update("w_o", [r_wo], moves=[gather(pack(late_keys))])
    out["w_down"], _ = update("w_down", [r_down0, r_down1])
    out["w_up"], _ = update("w_up", [r_up0, r_up1])
    out["w_out_even"], _ = update("w_out_even", [r_wout])
    out["w_qkv"], _ = update("w_qkv", [r_wqkv], transposed=True)
    out["w_in_even"], _ = update("w_in_even", list(r_w_in))

    def unpacked(recv, keys):
        flat, res, o = recv.reshape(N_DEV, -1), {}, 0
        for key in keys:
            res[key] = flat[:, o:o + small_grads[key].size]
            o += small_grads[key].size
        return res

    arrived = {**unpacked(early_recv, early + early_sharded), **unpacked(late_recv, late_keys)}
    piece = lambda tree, key: tree[key[0]] if key[1] is None else tree[key[0]][key[1]]

    def rows_by_device(cat):
        pad = (-cat.shape[1]) % PACK_W
        return jnp.pad(cat, ((0, 0), (0, pad))).reshape(N_DEV, -1, PACK_W)

    rep_keys = early + late
    rep_parts = rows_by_device(jnp.concatenate([arrived[key] for key in rep_keys], axis=1))
    flat_rep = lambda tree: _as_rows(jnp.concatenate([piece(tree, key).reshape(-1) for key in rep_keys]))[None]
    rep_res = _adamw([rep_parts], flat_rep(W), flat_rep(M), flat_rep(V), name="adamw_replicated")
    sh_keys = early_sharded + late_sharded
    shard_parts = []
    for n in sh_keys:
        full = arrived[n].reshape((N_DEV,) + small_grads[n].shape)
        c = full.shape[-1] // N_DEV
        shard_parts.append(lax.dynamic_slice_in_dim(full, me * c, c, axis=full.ndim - 1).reshape(N_DEV, -1))
    sh_rows = rows_by_device(jnp.concatenate(shard_parts, axis=1))
    flat_sh = lambda tree: _as_rows(jnp.concatenate([tree[n].reshape(-1) for n in sh_keys]))[None]
    sh_res = _adamw([sh_rows], flat_sh(W), flat_sh(M), flat_sh(V), name="adamw_small_sharded")

    def unpack_replicated(rows):
        flat, vals, o = rows.reshape(-1), {}, 0
        for key in rep_keys:
            size = piece(W, key).size
            vals[key] = flat[o:o + size]
            o += size
        res = {}
        for n in replicated:
            if (n, None) in vals:
                res[n] = vals[(n, None)].reshape(W[n].shape)
            else:
                res[n] = jnp.stack([vals[(n, r)] for r in range(W[n].shape[0])]).reshape(W[n].shape)
        return res

    def unpack_sharded(rows):
        flat, res, o = rows.reshape(-1), {}, 0
        for n in sh_keys:
            res[n] = flat[o:o + W[n].size].reshape(W[n].shape)
            o += W[n].size
        return res

    results = []
    for idx in range(4):
        d = {n: out[n][idx] for n in big}
        d.update(unpack_replicated(rep_res[idx]))
        d.update(unpack_sharded(sh_res[idx]))
        results.append(d)

    loss = jnp.sum(arrived[("loss", None)])
    grad_x = dx.reshape(x.shape)
    final = [loss, grad_x]
    for d in results:
        final.extend(d[n] for n in names)
    return tuple(final)
```

```python
import dataclasses
import functools

import jax
import jax.numpy as jnp
from jax import lax
from jax.experimental import pallas as pl
from jax.experimental.pallas import tpu as pltpu

F32 = jnp.float32
BF16 = jnp.bfloat16

N_DEV = 8
D_MODEL = 1024
D_FF = 4096
RMS_EPS = 1e-5
LN_EPS = 1e-5
CHUNK = 128
GM_GROUPS = 8
SSM_HEADS = 16
SSM_HEADDIM = 64
SSM_GROUPS = 4
SSM_STATE = 128
SSM_CONV = 4
CONV_DIM = 2048
IN_EVEN = 5136
REST_W = 3200
ATTN_HEADS = 16
ATTN_KV = 2
HEAD_DIM = 64
QKV_DIM = 1280
LANES = 128
HALO = 8
PACK_W = 1024

ADAM_LR = 0.001
ADAM_B1 = 0.9
ADAM_B2 = 0.999
ADAM_EPS = 1e-08
ADAM_WD = 0.01
ADAM_STEP = 10

VMEM_LIMIT_BYTES = 56 * 1024 * 1024


_NN = (((1,), (0,)), ((), ()))
_NT = (((1,), (1,)), ((), ()))
_TN = (((0,), (0,)), ((), ()))


def _dg(a, b, dims):
    return lax.dot_general(a.astype(BF16), b.astype(BF16), dims, preferred_element_type=F32)


@jax.custom_vjp
def _nn(a, b):
    return _dg(a, b, _NN)


@jax.custom_vjp
def _nt(a, b):
    return _dg(a, b, _NT)


@jax.custom_vjp
def _tn(a, b):
    return _dg(a, b, _TN)


_nn.defvjp(lambda a, b: (_dg(a, b, _NN), (a, b)), lambda r, g: (_nt(g, r[1]), _tn(r[0], g)))
_nt.defvjp(lambda a, b: (_dg(a, b, _NT), (a, b)), lambda r, g: (_nn(g, r[1]), _tn(g, r[0])))
_tn.defvjp(lambda a, b: (_dg(a, b, _TN), (a, b)), lambda r, g: (_nt(r[1], g), _nn(r[0], g)))


def _split3_dot(tri, x):
    x1 = x.astype(BF16)
    r1 = x - x1.astype(F32)
    x2 = r1.astype(BF16)
    x3 = (r1 - x2.astype(F32)).astype(BF16)
    t = tri.astype(BF16)
    dot = lambda p: lax.dot_general(t, p, _NN, preferred_element_type=F32)
    return dot(x1) + dot(x2) + dot(x3)


def _tri(lower):
    r = lax.broadcasted_iota(jnp.int32, (CHUNK, CHUNK), 0)
    c = lax.broadcasted_iota(jnp.int32, (CHUNK, CHUNK), 1)
    return jnp.where((r >= c) if lower else (r <= c), 1.0, 0.0).astype(F32)


@jax.custom_vjp
def _cumsum_rows(x):
    return _split3_dot(_tri(True), x)


_cumsum_rows.defvjp(lambda x: (_split3_dot(_tri(True), x), None), lambda _, g: (_split3_dot(_tri(False), g),))


def _sigmoid(x):
    return 1.0 / (1.0 + jnp.exp(-x))


def _silu(x):
    return x * _sigmoid(x)


def _softplus(x):
    return jnp.maximum(x, 0.0) + jnp.log(1.0 + jnp.exp(-jnp.abs(x)))


def _gelu_tanh(x):
    return 0.5 * x * (1.0 + jnp.tanh(0.7978845608028654 * (x + 0.044715 * (x * x * x))))


def _rmsnorm(x, g):
    return x * lax.rsqrt(jnp.mean(x * x, axis=-1, keepdims=True) + RMS_EPS) * g


def _gmlp_chunk(u, v, ln_g, ln_b, w_s, b_s):
    gu = _gelu_tanh(u)
    gv = _gelu_tanh(v)
    mu = jnp.mean(gv, axis=-1, keepdims=True)
    var = jnp.mean(jnp.square(gv - mu), axis=-1, keepdims=True)
    vn = (gv - mu) * lax.rsqrt(var + LN_EPS) * ln_g + ln_b
    r = lax.broadcasted_iota(jnp.int32, (CHUNK, CHUNK), 0)
    c = lax.broadcasted_iota(jnp.int32, (CHUNK, CHUNK), 1)
    causal = r >= c
    outs = []
    for g in range(GM_GROUPS):
        cols = slice(g * LANES, (g + 1) * LANES)
        mixed = _nn(jnp.where(causal, w_s[g], 0.0), vn[:, cols]) + b_s[g]
        outs.append(gu[:, cols] * mixed)
    return jnp.concatenate(outs, axis=1)


def _lane_pick(row, h):
    lane = lax.broadcasted_iota(jnp.int32, row.shape, 1)
    return jnp.sum(jnp.where(lane == h, row, 0.0), axis=1, keepdims=True)


def _col_pick(m, h):
    lane = lax.broadcasted_iota(jnp.int32, m.shape, 1)
    return jnp.sum(jnp.where(lane == h, m, 0.0), axis=1, keepdims=True)


def _row_pick(m, h):
    sub = lax.broadcasted_iota(jnp.int32, m.shape, 0)
    return jnp.sum(jnp.where(sub == h, m, 0.0), axis=0, keepdims=True)


_PAIRS = SSM_HEADS // 2


def _ssd_chunk(pre, z, dt_raw, h_prev, dt_bias, a_log, d_skip, norm_g):
    xbc = _silu(pre)
    dt = _softplus(dt_raw + dt_bias)
    da = dt * (-jnp.exp(a_log))
    a_cum = _cumsum_rows(da)
    a_cum_t = a_cum.T
    dt_t = dt.T
    r = lax.broadcasted_iota(jnp.int32, (CHUNK, CHUNK), 0)
    c = lax.broadcasted_iota(jnp.int32, (CHUNK, CHUNK), 1)
    causal = r >= c
    lane_lo = lax.broadcasted_iota(jnp.int32, (1, LANES), 1) < SSM_HEADDIM
    last_row = lax.broadcasted_iota(jnp.int32, (CHUNK, 1), 0) == CHUNK - 1
    ys, h_next = [], []
    for j in range(_PAIRS):
        g = j // 2
        xs = xbc[:, j * LANES:(j + 1) * LANES]
        bm = xbc[:, 1024 + g * SSM_STATE:1024 + (g + 1) * SSM_STATE]
        cm = xbc[:, 1536 + g * SSM_STATE:1536 + (g + 1) * SSM_STATE]
        cb = _nt(cm, bm)
        y_diag, to_end, e_cum, c_dec, d_row = [], [], [], [], []
        for h in (2 * j, 2 * j + 1):
            col = _col_pick(a_cum, h)
            row = _row_pick(a_cum_t, h)
            dt_col = _col_pick(dt, h)
            dt_row = _row_pick(dt_t, h)
            decay = jnp.exp(jnp.where(causal, col - row, -jnp.inf))
            y_diag.append(_nn(cb * decay * dt_row, xs))
            last = jnp.sum(jnp.where(last_row, col, 0.0), axis=0, keepdims=True)
            to_end.append(jnp.exp(last - col) * dt_col)
            e_cum.append(jnp.exp(col))
            c_dec.append(jnp.exp(last))
            d_row.append(_lane_pick(d_skip, h))
        pair = lambda lo_hi: jnp.where(lane_lo, lo_hi[0], lo_hi[1])
        states = _tn(bm, xs * pair(to_end))
        y_off = _nn(cm, h_prev[j]) * pair(e_cum)
        ys.append(pair(y_diag) + y_off + xs * pair(d_row))
        h_next.append(pair(c_dec) * h_prev[j] + states)
    y = jnp.concatenate(ys, axis=1) * _silu(z)
    width = D_MODEL // SSM_GROUPS
    y = jnp.concatenate(
        [_rmsnorm(y[:, g * width:(g + 1) * width], norm_g[:, g * width:(g + 1) * width]) for g in range(SSM_GROUPS)],
        axis=1)
    return y, tuple(h_next)


def _shift_down(prev8, x, k):
    if k == 0:
        return x
    win = jnp.concatenate([prev8, x], axis=0)
    return pltpu.roll(win, k, 0)[HALO:]


def _shift_up(x, next8, k):
    if k == 0:
        return x
    n = x.shape[0]
    win = jnp.concatenate([x, next8], axis=0)
    return pltpu.roll(win, n + HALO - k, 0)[:n]


def _conv_pre(prev8, x, w, b):
    out = b + x * w[SSM_CONV - 1:SSM_CONV]
    for i in range(SSM_CONV - 1):
        out = out + _shift_down(prev8, x, SSM_CONV - 1 - i) * w[i:i + 1]
    return out


def _swap_halves(x):
    return pltpu.roll(x, HEAD_DIM, 1)


_PAIRS_PER_KV = ATTN_HEADS // ATTN_KV // 2
_ATTN_SCALE = HEAD_DIM ** -0.5


def _parity_lanes(parity):
    lane = lax.broadcasted_iota(jnp.int32, (1, LANES), 1)
    return (lane >= HEAD_DIM * parity) & (lane < HEAD_DIM * (parity + 1))


def _kv_placed(pair, kv_head):
    mine = jnp.where(_parity_lanes(kv_head), pair, 0.0)
    lo = mine if kv_head == 0 else _swap_halves(mine)
    return lo, _swap_halves(lo)


def _kv_unplaced(d_lo, d_hi, kv_head):
    d = jnp.where(_parity_lanes(0), d_lo, 0.0) + _swap_halves(jnp.where(_parity_lanes(1), d_hi, 0.0))
    return d if kv_head == 0 else _swap_halves(d)


def _band_mask(first):
    shape = (_PAIRS_PER_KV * CHUNK, 2 * CHUNK)
    rows = lax.broadcasted_iota(jnp.int32, shape, 0) & (CHUNK - 1)
    cols = lax.broadcasted_iota(jnp.int32, shape, 1)
    return (cols <= rows + CHUNK) & (cols > rows) & (cols >= CHUNK * first.astype(jnp.int32))


def _attn_probs(s, sink):
    m = jnp.maximum(jnp.max(s, axis=-1, keepdims=True), sink)
    p = jnp.exp(s - m)
    e_sink = jnp.exp(sink - m)
    return p, e_sink, jnp.sum(p, axis=-1, keepdims=True) + e_sink


def _lane_column(col, idx):
    lane = lax.broadcasted_iota(jnp.int32, (1, LANES), 1)
    return jnp.where(lane == idx, col, 0.0)


N_CHIP = 4
N_CORE = 2
_OTHER_CHIPS = (2, 4, 6)


@dataclasses.dataclass
class _Move:
    kind: str
    src: jax.Array

    def dst_shape(self):
        s = self.src.shape
        shape = {"gather": (N_DEV,) + s, "gather_ici": (N_CHIP, N_CORE) + s, "gather_d2d": s,
                 "scatter_d2d": (N_CHIP,) + s[2:], "scatter_ici": s}[self.kind]
        return jax.ShapeDtypeStruct(tuple(shape), self.src.dtype)


def _peer(x, y, c, k):
    return (1 - x if k & 4 else x, 1 - y if k & 2 else y, 1 - c if k & 1 else c)


def _move_copies(moves, srcs, dsts, send_sems, recv_sems, local_sems):
    x, y, c = lax.axis_index("x"), lax.axis_index("y"), lax.axis_index("c")
    chip = 2 * x + y
    me = 2 * chip + c
    sibling = (x, y, 1 - c)
    all_chips = pl.ds(0, N_CHIP)
    local, remote = [], []

    def push(n, k, src, dst, device):
        remote.append(pltpu.make_async_remote_copy(
            src_ref=src, dst_ref=dst, send_sem=send_sems.at[n, k], recv_sem=recv_sems.at[n, k],
            device_id=device, device_id_type=pl.DeviceIdType.MESH))

    for n, mv in enumerate(moves):
        s, d = srcs[n], dsts[n]
        if mv.kind == "gather":
            local.append(pltpu.make_async_copy(s, d.at[me], local_sems.at[n]))
            for k in range(1, N_DEV):
                push(n, k - 1, s, d.at[me], _peer(x, y, c, k))
        elif mv.kind == "gather_ici":
            local.append(pltpu.make_async_copy(s, d.at[chip, c], local_sems.at[n]))
            for k in _OTHER_CHIPS:
                push(n, k - 1, s, d.at[chip, c], _peer(x, y, c, k))
        elif mv.kind == "gather_d2d":
            push(n, 0, d.at[all_chips, c], d.at[all_chips, c], sibling)
        elif mv.kind == "scatter_d2d":
            push(n, 0, s.at[all_chips, 1 - c], d, sibling)
        else:
            assert mv.kind == "scatter_ici", mv.kind
            local.append(pltpu.make_async_copy(s.at[chip], d.at[chip], local_sems.at[n]))
            for k in _OTHER_CHIPS:
                px, py, _ = _peer(x, y, c, k)
                push(n, k - 1, s.at[2 * px + py], d.at[chip], (px, py, c))
    return local, remote


def _move_aliases(moves, n_in, n_out):
    return {n_in + n: n_out + n for n, mv in enumerate(moves) if mv.kind == "gather_d2d"}


def _pcall(body, *, name, grid, in_specs, out_specs, out_shape, scratch_shapes=(), semantics=(), moves=(),
           aliases=None):
    out_shape, out_specs = list(out_shape), list(out_specs)
    in_specs = list(in_specs)
    if not moves:
        call = pl.pallas_call(
            body, name=name, grid=grid, in_specs=in_specs, out_specs=out_specs, out_shape=out_shape,
            scratch_shapes=list(scratch_shapes), input_output_aliases=aliases or {},
            compiler_params=pltpu.CompilerParams(dimension_semantics=tuple(semantics),
                                                 vmem_limit_bytes=VMEM_LIMIT_BYTES))
        return (lambda *args: (list(call(*args)), []))
    n_in, n_out, n_scr, n_mv = len(in_specs), len(out_shape), len(scratch_shapes), len(moves)
    hbm = pl.BlockSpec(memory_space=pltpu.HBM)

    def carrier(*refs):
        ins, rest = refs[:n_in], refs[n_in:]
        srcs, rest = rest[:n_mv], rest[n_mv:]
        outs, rest = rest[:n_out], rest[n_out:]
        dsts, rest = rest[:n_mv], rest[n_mv:]
        scr, (send_sems, recv_sems, local_sems) = rest[:n_scr], rest[n_scr:]
        first = functools.reduce(jnp.logical_and, [pl.program_id(d) == 0 for d in range(len(grid))])
        last = functools.reduce(jnp.logical_and, [pl.program_id(d) == grid[d] - 1 for d in range(len(grid))])

        @pl.when(first)
        def _():
            local, remote = _move_copies(moves, srcs, dsts, send_sems, recv_sems, local_sems)
            for cp in local + remote:
                cp.start()

        body(*ins, *outs, *scr)

        @pl.when(last)
        def _():
            local, remote = _move_copies(moves, srcs, dsts, send_sems, recv_sems, local_sems)
            for cp in remote + local:
                cp.wait()

    call = pl.pallas_call(
        carrier, name=name, grid=grid,
        in_specs=in_specs + [hbm] * n_mv,
        out_specs=out_specs + [hbm] * n_mv,
        out_shape=out_shape + [mv.dst_shape() for mv in moves],
        scratch_shapes=list(scratch_shapes) + [pltpu.SemaphoreType.DMA((n_mv, N_DEV - 1)),
                                               pltpu.SemaphoreType.DMA((n_mv, N_DEV - 1)),
                                               pltpu.SemaphoreType.DMA((n_mv,))],
        input_output_aliases={**(aliases or {}), **_move_aliases(moves, n_in, n_out)},
        compiler_params=pltpu.CompilerParams(dimension_semantics=("arbitrary",) * len(grid),
                                             vmem_limit_bytes=VMEM_LIMIT_BYTES))

    def run(*args):
        res = list(call(*args, *[mv.src for mv in moves]))
        return res[:n_out], res[n_out:]

    return run


def _exchange(moves, *, name, then_d2d=()):
    n_mv, n_fwd = len(moves), len(then_d2d)
    hbm = pl.BlockSpec(memory_space=pltpu.HBM)
    copies_of = {"gather": N_DEV - 1, "gather_ici": len(_OTHER_CHIPS), "gather_d2d": 1, "scatter_d2d": 1,
                 "scatter_ici": len(_OTHER_CHIPS)}
    first_copy = [sum(copies_of[mv.kind] for mv in moves[:n]) for n in range(n_mv)]

    def body(*refs):
        srcs, dsts, sems = refs[:n_mv], refs[n_mv:2 * n_mv], refs[2 * n_mv:]
        local, remote = _move_copies(moves, srcs, dsts, *sems[:3])
        for cp in local + remote:
            cp.start()
        x, y, c = lax.axis_index("x"), lax.axis_index("y"), lax.axis_index("c")
        chip, sibling = 2 * x + y, (x, y, 1 - c)
        passed, passed_on = [], set()

        def to_sibling(f, k, src, slot):
            cp = pltpu.make_async_remote_copy(src_ref=src, dst_ref=slot, send_sem=sems[3].at[f, k],
                                              recv_sem=sems[4].at[f, k], device_id=sibling,
                                              device_id_type=pl.DeviceIdType.MESH)
            cp.start()
            passed.append(cp)

        for f, n in enumerate(then_d2d):
            assert moves[n].kind == "gather_ici"
            d = dsts[n]
            to_sibling(f, 0, srcs[n], d.at[chip, c])
            for i, k in enumerate(_OTHER_CHIPS):
                remote[first_copy[n] + i].wait_recv()
                passed_on.add(first_copy[n] + i)
                px, py, _ = _peer(x, y, c, k)
                to_sibling(f, k - 1, d.at[2 * px + py, c], d.at[2 * px + py, c])
        for i, cp in enumerate(remote):
            if i in passed_on:
                cp.wait_send()
            else:
                cp.wait()
        for cp in local + passed:
            cp.wait()

    sems = [pltpu.SemaphoreType.DMA((n_mv, N_DEV - 1)), pltpu.SemaphoreType.DMA((n_mv, N_DEV - 1)),
            pltpu.SemaphoreType.DMA((n_mv,))]
    if then_d2d:
        sems += [pltpu.SemaphoreType.DMA((n_fwd, N_DEV - 1)), pltpu.SemaphoreType.DMA((n_fwd, N_DEV - 1))]
    return list(pl.pallas_call(
        body, name=name, in_specs=[hbm] * n_mv, out_specs=[hbm] * n_mv,
        out_shape=[mv.dst_shape() for mv in moves], scratch_shapes=sems,
    )(*[mv.src for mv in moves]))


TM = 512
FF_SHARD = D_FF // N_DEV


def _whole(a):
    nd = a.ndim
    return pl.BlockSpec(a.shape, lambda i: (0,) * nd)


def _rows(width, col=0):
    return pl.BlockSpec((TM, width), lambda i: (i, col))


def _acc_row(width):
    return pl.BlockSpec((1, width), lambda i: (0, 0))


def _unpack(res_landed, moves, n_out):
    res, landed = res_landed
    res = res[0] if n_out == 1 else res
    return (res, landed) if moves else res


def _norm_matmul(x, g, w, *, name, emit_y, moves=()):
    t, d = x.shape
    n = w.shape[1]

    def body(x_ref, g_ref, w_ref, *outs):
        y = _rmsnorm(x_ref[...], g_ref[...]).astype(BF16)
        if emit_y:
            outs[0][...] = y
        outs[-1][...] = lax.dot_general(y, w_ref[...], _NN, preferred_element_type=F32)

    shapes = ([jax.ShapeDtypeStruct((t, d), BF16)] if emit_y else []) + [jax.ShapeDtypeStruct((t, n), F32)]
    specs = ([_rows(d)] if emit_y else []) + [_rows(n)]
    return _unpack(_pcall(body, name=name, grid=(t // TM,), in_specs=[_rows(d), _acc_row(d), _whole(w)],
                          out_specs=specs, out_shape=shapes, semantics=("parallel",), moves=moves)(x, g, w),
                   moves, len(shapes))


def _residual_matmul(a, w, res, *, name, bias=None, norm_g=None, w_transposed=False, moves=()):
    t, k = a.shape
    n = w.shape[0 if w_transposed else 1]
    contract = _NT if w_transposed else _NN
    has_res, has_bias, has_norm = res is not None, bias is not None, norm_g is not None

    def body(a_ref, w_ref, *rest):
        rest = list(rest)
        res_ref = rest.pop(0) if has_res else None
        b_ref = rest.pop(0) if has_bias else None
        g_ref = rest.pop(0) if has_norm else None
        h = lax.dot_general(a_ref[...].astype(BF16), w_ref[...], contract, preferred_element_type=F32)
        if has_res:
            h = h + res_ref[...]
        if has_bias:
            h = h + b_ref[...]
        rest[0][...] = h
        if has_norm:
            rest[1][...] = _rmsnorm(h, g_ref[...]).astype(BF16)

    rows_in = [res] if has_res else []
    extra = ([bias] if has_bias else []) + ([norm_g] if has_norm else [])
    shapes = [jax.ShapeDtypeStruct((t, n), F32)] + ([jax.ShapeDtypeStruct((t, n), BF16)] if has_norm else [])
    return _unpack(_pcall(body, name=name, grid=(t // TM,),
                          in_specs=[_rows(k), _whole(w)] + [_rows(n)] * len(rows_in) + [_acc_row(n)] * len(extra),
                          out_specs=[_rows(n)] * len(shapes), out_shape=shapes, semantics=("parallel",),
                          moves=moves)(a, w, *rows_in, *extra), moves, len(shapes))


def _mlp_up(y, w_cols, *, name, moves=()):
    t, d = y.shape

    def body(y_ref, w_ref, up_ref):
        yv = y_ref[...]
        for j in range(N_DEV):
            up_ref[:, j * FF_SHARD:(j + 1) * FF_SHARD] = lax.dot_general(
                yv, w_ref[j], _NN, preferred_element_type=F32).astype(up_ref.dtype)

    return _unpack(_pcall(body, name=name, grid=(t // TM,), in_specs=[_rows(d), _whole(w_cols)],
                          out_specs=[_rows(D_FF)], out_shape=[jax.ShapeDtypeStruct((t, D_FF), BF16)],
                          semantics=("parallel",), moves=moves)(y, w_cols), moves, 1)


def _sq_relu(u):
    return jnp.square(jnp.maximum(u.astype(F32), 0.0))


def _down_blocks(w_refs):
    for j in range(N_DEV):
        off = j * FF_SHARD
        for w_ref in w_refs:
            yield off, w_ref.shape[1], w_ref[j]
            off += w_ref.shape[1]


def _mlp_down(up, w_rows, res, *, name, norm_g=None, moves=()):
    t = up.shape[0]
    has_norm = norm_g is not None
    n_w = len(w_rows)

    def body(up_ref, *rest):
        w_refs, res_ref, rest = rest[:n_w], rest[n_w], rest[n_w + 1:]
        h = res_ref[...]
        for off, rows, w_blk in _down_blocks(w_refs):
            act = _sq_relu(up_ref[:, off:off + rows]).astype(BF16)
            h = h + lax.dot_general(act, w_blk, _NN, preferred_element_type=F32)
        if has_norm:
            g_ref, h_ref, y_ref = rest
            y_ref[...] = _rmsnorm(h, g_ref[...]).astype(BF16)
        else:
            (h_ref,) = rest
        h_ref[...] = h

    shapes = [jax.ShapeDtypeStruct((t, D_MODEL), F32)] + ([jax.ShapeDtypeStruct((t, D_MODEL), BF16)] if has_norm else [])
    return _unpack(_pcall(body, name=name, grid=(t // TM,),
                          in_specs=[_rows(D_FF)] + [_whole(w) for w in w_rows] + [_rows(D_MODEL)]
                          + ([_acc_row(D_MODEL)] if has_norm else []),
                          out_specs=[_rows(D_MODEL)] * len(shapes), out_shape=shapes, semantics=("parallel",),
                          moves=moves)(up, *w_rows, res, *([norm_g] if has_norm else [])), moves, len(shapes))


def _mlp_down_dx(dh, w_rows, up, *, name, moves=()):
    t = up.shape[0]
    n_w = len(w_rows)

    def body(dh_ref, *rest):
        w_refs, (up_ref, o_ref) = rest[:n_w], rest[n_w:]
        dhv = dh_ref[...]
        for off, rows, w_blk in _down_blocks(w_refs):
            cols = slice(off, off + rows)
            d_act = lax.dot_general(dhv, w_blk, _NT, preferred_element_type=F32)
            o_ref[:, cols] = (d_act * (2.0 * jnp.maximum(up_ref[:, cols].astype(F32), 0.0))).astype(o_ref.dtype)

    return _unpack(_pcall(body, name=name, grid=(t // TM,),
                          in_specs=[_rows(D_MODEL)] + [_whole(w) for w in w_rows] + [_rows(D_FF)],
                          out_specs=[_rows(D_FF)], out_shape=[jax.ShapeDtypeStruct((t, D_FF), BF16)],
                          semantics=("parallel",), moves=moves)(dh, *w_rows, up), moves, 1)


def _dw_by_cols(x, dy, *, name, tn, by_device=False, moves=()):
    t, k = x.shape
    n = dy.shape[1]
    assert n % tn == 0, (name, n, tn)

    def body(x_ref, dy_ref, o_ref):
        o_ref[...] = lax.dot_general(x_ref[...].astype(BF16), dy_ref[...].astype(BF16), _TN,
                                     preferred_element_type=F32).astype(o_ref.dtype)

    if by_device:
        out_spec, out_shape = pl.BlockSpec((None, k, tn), lambda j: (j, 0, 0)), (n // tn, k, tn)
    else:
        out_spec, out_shape = pl.BlockSpec((k, tn), lambda j: (0, j)), (k, n)
    return _unpack(_pcall(body, name=name, grid=(n // tn,),
                          in_specs=[_whole(x), pl.BlockSpec((t, tn), lambda j: (0, j))],
                          out_specs=[out_spec], out_shape=[jax.ShapeDtypeStruct(out_shape, BF16)],
                          semantics=("parallel",), moves=moves)(x, dy), moves, 1)


def _dw_by_rows(x, dy, *, name, tk, square_relu=False, column_sums=False, moves=()):
    t, k = x.shape
    n = dy.shape[1]
    assert k % tk == 0, (name, k, tk)

    def body(x_ref, dy_ref, o_ref, *sums):
        xv = _sq_relu(x_ref[...]) if square_relu else x_ref[...]
        o_ref[...] = lax.dot_general(xv.astype(BF16), dy_ref[...].astype(BF16), _TN,
                                     preferred_element_type=F32).astype(o_ref.dtype)
        if column_sums:
            sums[0][...] = jnp.sum(xv.astype(F32), axis=0, keepdims=True)

    shapes = [jax.ShapeDtypeStruct((k, n), BF16)] + ([jax.ShapeDtypeStruct((1, k), F32)] if column_sums else [])
    specs = [pl.BlockSpec((tk, n), lambda j: (j, 0))] + ([pl.BlockSpec((1, tk), lambda j: (0, j))] if column_sums else [])
    return _unpack(_pcall(body, name=name, grid=(k // tk,),
                          in_specs=[pl.BlockSpec((t, tk), lambda j: (0, j)), _whole(dy)],
                          out_specs=specs, out_shape=shapes,
                          semantics=("parallel",), moves=moves)(x, dy), moves, len(shapes))


def _dx(dy, w, *, name, partial=None, moves=()):
    t, k = dy.shape
    n = w.shape[0]
    has_partial = partial is not None

    def body(dy_ref, w_ref, *rest):
        out = lax.dot_general(dy_ref[...].astype(BF16), w_ref[...], _NT, preferred_element_type=F32)
        if has_partial:
            out = out + rest[0][...]
        rest[-1][...] = out

    return _unpack(_pcall(body, name=name, grid=(t // TM,),
                          in_specs=[_rows(k), _whole(w)] + ([_rows(n)] if has_partial else []),
                          out_specs=[_rows(n)], out_shape=[jax.ShapeDtypeStruct((t, n), F32)],
                          semantics=("parallel",), moves=moves)(dy, w, *([partial] if has_partial else [])),
                   moves, 1)


def _dx_norm(dy, w, h, g, dres, *, name, partial=None, by_device_cols=False, w_transposed=False, moves=()):
    t, k = dy.shape
    d = h.shape[1]
    has_partial = partial is not None

    def body(dy_ref, w_ref, h_ref, g_ref, dres_ref, *rest):
        if by_device_cols:
            kc = k // N_DEV
            d_y = jnp.zeros((TM, d), F32)
            for j in range(N_DEV):
                d_y = d_y + lax.dot_general(dy_ref[:, j * kc:(j + 1) * kc].astype(BF16), w_ref[j], _NT,
                                            preferred_element_type=F32)
        else:
            d_y = lax.dot_general(dy_ref[...].astype(BF16), w_ref[...], _NN if w_transposed else _NT,
                                  preferred_element_type=F32)
        if has_partial:
            d_y = d_y + rest[0][...]
        dh_ref, dhb_ref, dg_ref, cs_ref = rest[-4:]
        _, vjp = jax.vjp(_rmsnorm, h_ref[...], g_ref[...])
        dh, dg = vjp(d_y)
        dh = dh + dres_ref[...]
        dh_ref[...] = dh
        dhb_ref[...] = dh.astype(BF16)

        @pl.when(pl.program_id(0) == 0)
        def _():
            dg_ref[...] = jnp.zeros_like(dg_ref)
            cs_ref[...] = jnp.zeros_like(cs_ref)

        dg_ref[...] += dg
        cs_ref[...] += jnp.sum(dh, axis=0, keepdims=True)

    shapes = [jax.ShapeDtypeStruct((t, d), F32), jax.ShapeDtypeStruct((t, d), BF16),
              jax.ShapeDtypeStruct((1, d), F32), jax.ShapeDtypeStruct((1, d), F32)]
    return _unpack(_pcall(body, name=name, grid=(t // TM,),
                          in_specs=[_rows(k), _whole(w), _rows(d), _acc_row(d), _rows(d)]
                          + ([_rows(d)] if has_partial else []),
                          out_specs=[_rows(d), _rows(d), _acc_row(d), _acc_row(d)], out_shape=shapes,
                          semantics=("arbitrary",), moves=moves)(dy, w, h, g, dres, *([partial] if has_partial else [])),
                   moves, 4)


def _pair_add(by_core, theirs, core, *, name, tb=512):
    n_chip, _, r, c = by_core.shape
    tb = min(tb, r)
    assert r % tb == 0, (name, r, tb)

    def body(core_ref, a_ref, b_ref, o_ref):
        del core_ref
        o_ref[...] = (a_ref[...].astype(F32) + b_ref[...].astype(F32)).astype(o_ref.dtype)

    blk = pl.BlockSpec((None, tb, c), lambda ch, i, core_ref: (ch, i, 0))
    return pl.pallas_call(
        body, name=name,
        grid_spec=pltpu.PrefetchScalarGridSpec(
            num_scalar_prefetch=1, grid=(n_chip, r // tb),
            in_specs=[pl.BlockSpec((None, None, tb, c), lambda ch, i, core_ref: (ch, core_ref[0], i, 0)), blk],
            out_specs=blk),
        out_shape=jax.ShapeDtypeStruct((n_chip, r, c), by_core.dtype),
        compiler_params=pltpu.CompilerParams(dimension_semantics=("parallel", "parallel"),
                                             vmem_limit_bytes=VMEM_LIMIT_BYTES),
    )(core, by_core, theirs)


def _mlp_down_loss(up, w_rows, res, g, target, *, name):
    t, d = res.shape
    n_w = len(w_rows)

    def body(up_ref, *rest):
        w_refs, (res_ref, g_ref, tgt_ref, loss_ref, dh_ref, dhb_ref, dg_ref) = rest[:n_w], rest[n_w:]
        h = res_ref[...]
        for off, rows, w_blk in _down_blocks(w_refs):
            act = _sq_relu(up_ref[:, off:off + rows]).astype(BF16)
            h = h + lax.dot_general(act, w_blk, _NN, preferred_element_type=F32)

        def f(hh, gg):
            err = jnp.square(_rmsnorm(hh, gg) - tgt_ref[...])
            return 0.5 * jnp.sum(jnp.mean(err, axis=-1, keepdims=True), axis=0, keepdims=True)

        val, vjp = jax.vjp(f, h, g_ref[...])
        dh, dg = vjp(jnp.ones((1, 1), F32))
        dh_ref[...] = dh
        dhb_ref[...] = dh.astype(BF16)

        @pl.when(pl.program_id(0) == 0)
        def _():
            loss_ref[...] = jnp.zeros_like(loss_ref)
            dg_ref[...] = jnp.zeros_like(dg_ref)

        loss_ref[...] += val
        dg_ref[...] += dg

    return _pcall(
        body, name=name, grid=(t // TM,),
        in_specs=[_rows(D_FF)] + [_whole(w) for w in w_rows] + [_rows(d), _acc_row(d), _rows(d)],
        out_specs=[pl.BlockSpec((8, LANES), lambda i: (0, 0)), _rows(d), _rows(d), _acc_row(d)],
        out_shape=[jax.ShapeDtypeStruct((8, LANES), F32), jax.ShapeDtypeStruct((t, d), F32),
                   jax.ShapeDtypeStruct((t, d), BF16), jax.ShapeDtypeStruct((1, d), F32)],
        semantics=("arbitrary",),
    )(up, *w_rows, res, g, target)[0]


def _gmlp_fwd(proj_uv, ln_g, ln_b, w_s, b_s, *, name, moves=()):
    t = proj_uv.shape[0]
    w = D_MODEL

    def body(u_ref, v_ref, g_ref, b_ref, w_ref, bs_ref, o_ref):
        o_ref[...] = _gmlp_chunk(u_ref[...], v_ref[...], g_ref[...], b_ref[...], w_ref[...],
                                 bs_ref[...]).astype(o_ref.dtype)

    row = pl.BlockSpec((1, w), lambda i: (0, 0))
    res, landed = _pcall(
        body, name=name, grid=(t // CHUNK,),
        in_specs=[pl.BlockSpec((CHUNK, w), lambda i: (i, 0)), pl.BlockSpec((CHUNK, w), lambda i: (i, 1)), row, row,
                  pl.BlockSpec((GM_GROUPS, CHUNK, CHUNK), lambda i: (0, 0, 0)),
                  pl.BlockSpec((GM_GROUPS, CHUNK, 1), lambda i: (0, 0, 0))],
        out_specs=[pl.BlockSpec((CHUNK, w), lambda i: (i, 0))],
        out_shape=[jax.ShapeDtypeStruct((t, 2 * w), BF16)],
        semantics=("parallel",), moves=moves,
    )(proj_uv, proj_uv, ln_g, ln_b, w_s, b_s)
    return (res[0], landed) if moves else res[0]


def _gmlp_bwd(proj_uv, d_mix, ln_g, ln_b, w_s, b_s, *, name, moves=()):
    t = proj_uv.shape[0]
    w = D_MODEL

    def body(u_ref, v_ref, da_ref, g_ref, b_ref, w_ref, bs_ref, duv_ref, dg_ref, db_ref, dw_ref, dbs_ref):
        _, vjp = jax.vjp(_gmlp_chunk, u_ref[...], v_ref[...], g_ref[...], b_ref[...], w_ref[...], bs_ref[...])
        du, dv, dg, db, dw, dbs = vjp(da_ref[...])
        duv_ref[:, :w] = du.astype(duv_ref.dtype)
        duv_ref[:, w:] = dv.astype(duv_ref.dtype)

        @pl.when(pl.program_id(0) == 0)
        def _():
            dg_ref[...] = jnp.zeros_like(dg_ref)
            db_ref[...] = jnp.zeros_like(db_ref)
            dw_ref[...] = jnp.zeros_like(dw_ref)
            dbs_ref[...] = jnp.zeros_like(dbs_ref)

        dg_ref[...] += dg
        db_ref[...] += db
        dw_ref[...] += dw
        dbs_ref[...] += dbs

    row = pl.BlockSpec((1, w), lambda i: (0, 0))
    ws = pl.BlockSpec((GM_GROUPS, CHUNK, CHUNK), lambda i: (0, 0, 0))
    bs = pl.BlockSpec((GM_GROUPS, CHUNK, 1), lambda i: (0, 0, 0))
    res, landed = _pcall(
        body, name=name, grid=(t // CHUNK,),
        in_specs=[pl.BlockSpec((CHUNK, w), lambda i: (i, 0)), pl.BlockSpec((CHUNK, w), lambda i: (i, 1)),
                  pl.BlockSpec((CHUNK, w), lambda i: (i, 0)), row, row, ws, bs],
        out_specs=[pl.BlockSpec((CHUNK, 2 * w), lambda i: (i, 0)), row, row, ws, bs],
        out_shape=[jax.ShapeDtypeStruct((t, 2 * w), BF16), jax.ShapeDtypeStruct((1, w), F32),
                   jax.ShapeDtypeStruct((1, w), F32), jax.ShapeDtypeStruct((GM_GROUPS, CHUNK, CHUNK), F32),
                   jax.ShapeDtypeStruct((GM_GROUPS, CHUNK, 1), F32)],
        semantics=("arbitrary",), moves=moves,
    )(proj_uv, proj_uv, d_mix, ln_g, ln_b, w_s, b_s)
    return (res, landed) if moves else res


_HALO_PER_CHUNK = CHUNK // HALO
_DT_BLOCK = (CONV_DIM + D_MODEL) // LANES


def _ssd_fwd(proj_rest, mix, conv_w, conv_b, dt_bias, a_log, d_skip, norm_g, *, name, moves=()):
    t = proj_rest.shape[0]
    nc = t // CHUNK

    def body(x_ref, prev_ref, z_ref, dt_ref, mix_ref, cw_ref, cb_ref, dtb_ref, al_ref, ds_ref, ng_ref, y_ref, hs_ref,
             pre_ref, h_scr):
        del mix_ref
        i = pl.program_id(0)

        @pl.when(i == 0)
        def _():
            h_scr[...] = jnp.zeros_like(h_scr)

        prev8 = jnp.where(i == 0, 0.0, prev_ref[...])
        pre = _conv_pre(prev8, x_ref[...], cw_ref[...], cb_ref[...])
        pre_ref[...] = pre
        hs_ref[0] = h_scr[...]
        h_prev = tuple(h_scr[j] for j in range(_PAIRS))
        y, h_next = _ssd_chunk(pre, z_ref[...], dt_ref[...], h_prev, dtb_ref[...], al_ref[...], ds_ref[...],
                               ng_ref[...])
        y_ref[...] = y.astype(y_ref.dtype)
        for j in range(_PAIRS):
            h_scr[j] = h_next[j]

    small = pl.BlockSpec((1, LANES), lambda i: (0, 0))
    res, landed = _pcall(
        body, name=name, grid=(nc,),
        in_specs=[pl.BlockSpec((CHUNK, CONV_DIM), lambda i: (i, 0)),
                  pl.BlockSpec((HALO, CONV_DIM), lambda i: (jnp.maximum(i * _HALO_PER_CHUNK - 1, 0), 0)),
                  pl.BlockSpec((CHUNK, D_MODEL), lambda i: (i, CONV_DIM // D_MODEL)),
                  pl.BlockSpec((CHUNK, LANES), lambda i: (i, _DT_BLOCK)),
                  pl.BlockSpec(memory_space=pl.ANY),
                  pl.BlockSpec((SSM_CONV, CONV_DIM), lambda i: (0, 0)),
                  pl.BlockSpec((1, CONV_DIM), lambda i: (0, 0)),
                  small, small, small, pl.BlockSpec((1, D_MODEL), lambda i: (0, 0))],
        out_specs=[pl.BlockSpec((CHUNK, D_MODEL), lambda i: (i, 1)),
                   pl.BlockSpec((1, _PAIRS, SSM_STATE, LANES), lambda i: (i, 0, 0, 0)),
                   pl.BlockSpec((CHUNK, CONV_DIM), lambda i: (i, 0))],
        out_shape=[jax.ShapeDtypeStruct((t, 2 * D_MODEL), BF16),
                   jax.ShapeDtypeStruct((nc, _PAIRS, SSM_STATE, LANES), F32),
                   jax.ShapeDtypeStruct((t, CONV_DIM), F32)],
        scratch_shapes=[pltpu.VMEM((_PAIRS, SSM_STATE, LANES), F32)],
        semantics=("arbitrary",), moves=moves, aliases={4: 0},
    )(proj_rest, proj_rest, proj_rest, proj_rest, mix, conv_w, conv_b, dt_bias, a_log, d_skip, norm_g)
    return (res, landed) if moves else res


def _ssd_bwd(proj_rest, pre, h_states, d_mix, dt_bias, a_log, d_skip, norm_g, *, name, moves=()):
    t = proj_rest.shape[0]
    nc = t // CHUNK

    def body(pre_ref, z_ref, dt_ref, hs_ref, dy_ref, dtb_ref, al_ref, ds_ref, ng_ref,
             dpre_ref, dz_ref, ddt_ref, ddtb_ref, dal_ref, dds_ref, dng_ref, dh_scr):
        i = pl.program_id(0)

        @pl.when(i == 0)
        def _():
            dh_scr[...] = jnp.zeros_like(dh_scr)
            ddtb_ref[...] = jnp.zeros_like(ddtb_ref)
            dal_ref[...] = jnp.zeros_like(dal_ref)
            dds_ref[...] = jnp.zeros_like(dds_ref)
            dng_ref[...] = jnp.zeros_like(dng_ref)

        h_prev = tuple(hs_ref[0, j] for j in range(_PAIRS))
        _, vjp = jax.vjp(_ssd_chunk, pre_ref[...], z_ref[...], dt_ref[...], h_prev, dtb_ref[...], al_ref[...],
                         ds_ref[...], ng_ref[...])
        dpre, dz, ddt, dh_prev, ddtb, dal, dds, dng = vjp((dy_ref[...], tuple(dh_scr[j] for j in range(_PAIRS))))
        dpre_ref[...] = dpre
        dz_ref[...] = dz.astype(dz_ref.dtype)
        ddt_ref[...] = ddt.astype(ddt_ref.dtype)
        for j in range(_PAIRS):
            dh_scr[j] = dh_prev[j]
        ddtb_ref[...] += ddtb
        dal_ref[...] += dal
        dds_ref[...] += dds
        dng_ref[...] += dng

    rev = lambda i: nc - 1 - i
    small = pl.BlockSpec((1, LANES), lambda i: (0, 0))
    wide = pl.BlockSpec((1, D_MODEL), lambda i: (0, 0))
    res, landed = _pcall(
        body, name=name, grid=(nc,),
        in_specs=[pl.BlockSpec((CHUNK, CONV_DIM), lambda i: (rev(i), 0)),
                  pl.BlockSpec((CHUNK, D_MODEL), lambda i: (rev(i), CONV_DIM // D_MODEL)),
                  pl.BlockSpec((CHUNK, LANES), lambda i: (rev(i), _DT_BLOCK)),
                  pl.BlockSpec((1, _PAIRS, SSM_STATE, LANES), lambda i: (rev(i), 0, 0, 0)),
                  pl.BlockSpec((CHUNK, D_MODEL), lambda i: (rev(i), 1)),
                  small, small, small, wide],
        out_specs=[pl.BlockSpec((CHUNK, CONV_DIM), lambda i: (rev(i), 0)),
                   pl.BlockSpec((CHUNK, D_MODEL), lambda i: (rev(i), 0)),
                   pl.BlockSpec((CHUNK, LANES), lambda i: (rev(i), 0)),
                   small, small, small, wide],
        out_shape=[jax.ShapeDtypeStruct((t, CONV_DIM), F32), jax.ShapeDtypeStruct((t, D_MODEL), BF16),
                   jax.ShapeDtypeStruct((t, LANES), BF16),
                   jax.ShapeDtypeStruct((1, LANES), F32), jax.ShapeDtypeStruct((1, LANES), F32),
                   jax.ShapeDtypeStruct((1, LANES), F32), jax.ShapeDtypeStruct((1, D_MODEL), F32)],
        scratch_shapes=[pltpu.VMEM((_PAIRS, SSM_STATE, LANES), F32)],
        semantics=("arbitrary",), moves=moves,
    )(pre, proj_rest, proj_rest, h_states, d_mix, dt_bias, a_log, d_skip, norm_g)
    return (res, landed) if moves else res


def _conv_bwd(proj_rest, dpre, dz, ddt, conv_w, *, name, tb=256, moves=()):
    t = proj_rest.shape[0]
    nb = t // tb
    per = tb // HALO

    def body(x_ref, prev_ref, dpre_ref, next_ref, dz_ref, ddt_ref, cw_ref, drest_ref, dcw_ref, dcb_ref):
        i = pl.program_id(0)

        @pl.when(i == 0)
        def _():
            dcw_ref[...] = jnp.zeros_like(dcw_ref)
            dcb_ref[...] = jnp.zeros_like(dcb_ref)

        x = x_ref[...]
        dp = dpre_ref[...]
        w = cw_ref[...]
        prev8 = jnp.where(i == 0, 0.0, prev_ref[...])
        next8 = jnp.where(i == nb - 1, 0.0, next_ref[...])
        dx = dp * w[SSM_CONV - 1:SSM_CONV]
        for j in range(SSM_CONV - 1):
            dx = dx + _shift_up(dp, next8, SSM_CONV - 1 - j) * w[j:j + 1]
        drest_ref[:, :CONV_DIM] = dx.astype(drest_ref.dtype)
        drest_ref[:, CONV_DIM:CONV_DIM + D_MODEL] = dz_ref[...].astype(drest_ref.dtype)
        drest_ref[:, CONV_DIM + D_MODEL:] = ddt_ref[...].astype(drest_ref.dtype)
        for j in range(SSM_CONV):
            dcw_ref[j:j + 1, :] += jnp.sum(dp * _shift_down(prev8, x, SSM_CONV - 1 - j), axis=0, keepdims=True)
        dcb_ref[...] += jnp.sum(dp, axis=0, keepdims=True)

    res, landed = _pcall(
        body, name=name, grid=(nb,),
        in_specs=[pl.BlockSpec((tb, CONV_DIM), lambda i: (i, 0)),
                  pl.BlockSpec((HALO, CONV_DIM), lambda i: (jnp.maximum(i * per - 1, 0), 0)),
                  pl.BlockSpec((tb, CONV_DIM), lambda i: (i, 0)),
                  pl.BlockSpec((HALO, CONV_DIM), lambda i: (jnp.minimum((i + 1) * per, nb * per - 1), 0)),
                  pl.BlockSpec((tb, D_MODEL), lambda i: (i, 0)),
                  pl.BlockSpec((tb, LANES), lambda i: (i, 0)),
                  pl.BlockSpec((SSM_CONV, CONV_DIM), lambda i: (0, 0))],
        out_specs=[pl.BlockSpec((tb, REST_W), lambda i: (i, 0)),
                   pl.BlockSpec((SSM_CONV, CONV_DIM), lambda i: (0, 0)),
                   pl.BlockSpec((1, CONV_DIM), lambda i: (0, 0))],
        out_shape=[jax.ShapeDtypeStruct((t, REST_W), BF16), jax.ShapeDtypeStruct((SSM_CONV, CONV_DIM), F32),
                   jax.ShapeDtypeStruct((1, CONV_DIM), F32)],
        semantics=("arbitrary",), moves=moves,
    )(proj_rest, proj_rest, dpre, dpre, dz, ddt, conv_w)
    return (res, landed) if moves else res


_KV_BLOCK = D_MODEL // (2 * LANES)
_SINK_ROWS = _PAIRS_PER_KV * CHUNK


def _stack_pairs(ref, kv_head):
    base = kv_head * _PAIRS_PER_KV
    return jnp.concatenate([ref[:, (base + p) * LANES:(base + p + 1) * LANES] for p in range(_PAIRS_PER_KV)], axis=0)


def _attn_fwd(qkv, sinks, *, name, moves=()):
    t = qkv.shape[0]
    nb = t // CHUNK

    def body(q_ref, kvp_ref, kvc_ref, s_ref, o_ref, p_ref, st_ref):
        valid = _band_mask(pl.program_id(0) == 0)
        kv = jnp.concatenate([kvp_ref[...], kvc_ref[...]], axis=0)
        stats = jnp.zeros((_SINK_ROWS, LANES), F32)
        passes = [(j, e) for j in range(ATTN_KV) for e in range(2)]
        vs = [_kv_placed(kv[:, LANES:], j) for j in range(ATTN_KV)]
        scores = []
        for j in range(ATTN_KV):
            q4 = _stack_pairs(q_ref, j) * _ATTN_SCALE
            ks = _kv_placed(kv[:, :LANES], j)
            scores += [jnp.where(valid, _dg(q4, ks[e], _NT), -jnp.inf) for e in range(2)]
        probs = [_attn_probs(s, s_ref[j, e]) for s, (j, e) in zip(scores, passes)]
        outs = [None] * ATTN_KV
        for (p, e_sink, den), (j, e) in zip(probs, passes):
            inv = 1.0 / den
            o = _dg(p, vs[j][e], _NN) * inv
            outs[j] = o if outs[j] is None else outs[j] + o
            p_ref[0, 2 * j + e] = p.astype(p_ref.dtype)
            stats = stats + _lane_column(inv, 2 * j + e) + _lane_column(e_sink, 4 + 2 * j + e)
        for j in range(ATTN_KV):
            for pair in range(_PAIRS_PER_KV):
                col = (j * _PAIRS_PER_KV + pair) * LANES
                o_ref[:, col:col + LANES] = outs[j][pair * CHUNK:(pair + 1) * CHUNK].astype(o_ref.dtype)
        st_ref[0] = stats

    return _unpack(_pcall(
        body, name=name, grid=(nb,),
        in_specs=[pl.BlockSpec((CHUNK, D_MODEL), lambda i: (i, 0)),
                  pl.BlockSpec((CHUNK, 2 * LANES), lambda i: (jnp.maximum(i - 1, 0), _KV_BLOCK)),
                  pl.BlockSpec((CHUNK, 2 * LANES), lambda i: (i, _KV_BLOCK)),
                  pl.BlockSpec((ATTN_KV, 2, _SINK_ROWS, 1), lambda i: (0, 0, 0, 0))],
        out_specs=[pl.BlockSpec((CHUNK, D_MODEL), lambda i: (i, 0)),
                   pl.BlockSpec((1, 2 * ATTN_KV, _SINK_ROWS, 2 * CHUNK), lambda i: (i, 0, 0, 0)),
                   pl.BlockSpec((1, _SINK_ROWS, LANES), lambda i: (i, 0, 0))],
        out_shape=[jax.ShapeDtypeStruct((t, D_MODEL), BF16),
                   jax.ShapeDtypeStruct((nb, 2 * ATTN_KV, _SINK_ROWS, 2 * CHUNK), BF16),
                   jax.ShapeDtypeStruct((nb, _SINK_ROWS, LANES), F32)],
        semantics=("parallel",), moves=moves,
    )(qkv, qkv, qkv, sinks), moves, 3)


def _attn_bwd(qkv, probs, stats, attn, d_o, *, name, moves=()):
    t = qkv.shape[0]
    nb = t // CHUNK

    def body(q_ref, kvp_ref, kvc_ref, p_ref, st_ref, o_ref, do_ref, dqkv_ref, ds_ref, dkv_scr):
        @pl.when(pl.program_id(0) == 0)
        def _():
            dkv_scr[...] = jnp.zeros_like(dkv_scr)
            ds_ref[...] = jnp.zeros_like(ds_ref)

        kv = jnp.concatenate([kvp_ref[...], kvc_ref[...]], axis=0)
        table = st_ref[0]
        d_k = jnp.zeros((2 * CHUNK, LANES), F32)
        d_v = jnp.zeros((2 * CHUNK, LANES), F32)
        passes = [(j, e) for j in range(ATTN_KV) for e in range(2)]
        q4s = [_stack_pairs(q_ref, j) for j in range(ATTN_KV)]
        ks = [_kv_placed(kv[:, :LANES], j) for j in range(ATTN_KV)]
        vs = [_kv_placed(kv[:, LANES:], j) for j in range(ATTN_KV)]
        d_nums, d_dens = [], []
        for j in range(ATTN_KV):
            do4, o4 = _stack_pairs(do_ref, j), _stack_pairs(o_ref, j).astype(F32)
            for e in range(2):
                inv, e_sink = _col_pick(table, 2 * j + e), _col_pick(table, 4 + 2 * j + e)
                do_e = jnp.where(_parity_lanes(e), do4, 0.0)
                d_nums.append(do_e * inv)
                d_dens.append(-jnp.sum(do_e * o4, axis=1, keepdims=True) * inv)
                ds_ref[j, e] += d_dens[-1] * e_sink
        d_ps = [_dg(d_num, vs[j][e], _NT) for d_num, (j, e) in zip(d_nums, passes)]
        ps = [p_ref[0, 2 * j + e].astype(F32) for j, e in passes]
        d_ss = [p * (d_p + d_den) for p, d_p, d_den in zip(ps, d_ps, d_dens)]
        for j in range(ATTN_KV):
            ds_lo, ds_hi = d_ss[2 * j], d_ss[2 * j + 1]
            dq4 = (_dg(ds_lo, ks[j][0], _NN) + _dg(ds_hi, ks[j][1], _NN)) * _ATTN_SCALE
            for pair in range(_PAIRS_PER_KV):
                col = (j * _PAIRS_PER_KV + pair) * LANES
                dqkv_ref[:, col:col + LANES] = dq4[pair * CHUNK:(pair + 1) * CHUNK]
            dk = [_dg(ds, q4s[j], _TN) * _ATTN_SCALE for ds in (ds_lo, ds_hi)]
            dv = [_dg(ps[2 * j + e], d_nums[2 * j + e], _TN) for e in range(2)]
            d_k = d_k + _kv_unplaced(dk[0], dk[1], j)
            d_v = d_v + _kv_unplaced(dv[0], dv[1], j)
        d_kv = jnp.concatenate([d_k, d_v], axis=1)
        dqkv_ref[:, D_MODEL:] = d_kv[CHUNK:] + dkv_scr[...]
        dkv_scr[...] = d_kv[:CHUNK]

    cur = lambda i: (nb - 1 - i, 0)
    sk = pl.BlockSpec((ATTN_KV, 2, _SINK_ROWS, 1), lambda i: (0, 0, 0, 0))
    res, landed = _pcall(
        body, name=name, grid=(nb,),
        in_specs=[pl.BlockSpec((CHUNK, D_MODEL), cur),
                  pl.BlockSpec((CHUNK, 2 * LANES), lambda i: (jnp.maximum(nb - 2 - i, 0), _KV_BLOCK)),
                  pl.BlockSpec((CHUNK, 2 * LANES), lambda i: (nb - 1 - i, _KV_BLOCK)),
                  pl.BlockSpec((1, 2 * ATTN_KV, _SINK_ROWS, 2 * CHUNK), lambda i: (nb - 1 - i, 0, 0, 0)),
                  pl.BlockSpec((1, _SINK_ROWS, LANES), lambda i: (nb - 1 - i, 0, 0)),
                  pl.BlockSpec((CHUNK, D_MODEL), cur), pl.BlockSpec((CHUNK, D_MODEL), cur)],
        out_specs=[pl.BlockSpec((CHUNK, QKV_DIM), cur), sk],
        out_shape=[jax.ShapeDtypeStruct((t, QKV_DIM), F32), jax.ShapeDtypeStruct((ATTN_KV, 2, _SINK_ROWS, 1), F32)],
        scratch_shapes=[pltpu.VMEM((CHUNK, 2 * LANES), F32)],
        semantics=("arbitrary",), moves=moves,
    )(qkv, qkv, qkv, probs, stats, attn, d_o)
    return (res, landed) if moves else res


def _adamw(parts, w, m, v, *, name, tb=512, moves=()):
    layers, r, c = w.shape
    n = parts[0].shape[0]
    tb = min(tb, r)
    assert r % tb == 0 and len(parts) == layers, (name, r, tb)
    nb = r // tb

    def body(*refs):
        p_refs = refs[:layers]
        w_ref, m_ref, v_ref, g_ref, d_ref, nm_ref, nv_ref = refs[layers:]
        for layer in range(layers):
            @pl.when(pl.program_id(0) == layer)
            def _(p_ref=p_refs[layer]):
                g = p_ref[0].astype(F32)
                for s in range(1, n):
                    g = g + p_ref[s].astype(F32)
                m_new = ADAM_B1 * m_ref[...] + (1.0 - ADAM_B1) * g
                v_new = ADAM_B2 * v_ref[...] + (1.0 - ADAM_B2) * jnp.square(g)
                m_hat = m_new / (1.0 - ADAM_B1 ** ADAM_STEP)
                v_hat = v_new / (1.0 - ADAM_B2 ** ADAM_STEP)
                g_ref[...] = g
                d_ref[...] = -ADAM_LR * (m_hat / (jnp.sqrt(v_hat) + ADAM_EPS) + ADAM_WD * w_ref[...])
                nm_ref[...] = m_new
                nv_ref[...] = v_new

    part_spec = lambda layer: pl.BlockSpec(
        (n, tb, c), lambda l, i: (0, jnp.clip(i + (l - layer) * nb, 0, nb - 1), 0))
    blk = pl.BlockSpec((None, tb, c), lambda l, i: (l, i, 0))
    res, landed = _pcall(
        body, name=name, grid=(layers, nb),
        in_specs=[part_spec(layer) for layer in range(layers)] + [blk, blk, blk],
        out_specs=[blk] * 4,
        out_shape=[jax.ShapeDtypeStruct((layers, r, c), F32)] * 4,
        semantics=("arbitrary", "arbitrary"), moves=moves,
    )(*parts, w, m, v)
    return (res, landed) if moves else res


def _as_rows(a):
    flat = a.reshape(-1)
    pad = (-flat.shape[0]) % PACK_W
    if pad:
        flat = jnp.pad(flat, (0, pad))
    return flat.reshape(-1, PACK_W)


def _cols_from_shards(g):
    return jnp.transpose(g, (1, 0, 2)).reshape(g.shape[1], -1)


def _shard_cols(shards, lo, hi):
    c = shards.shape[2]
    pieces = []
    for j in range(shards.shape[0]):
        a, b = max(lo, j * c), min(hi, (j + 1) * c)
        if a < b:
            pieces.append(shards[j, :, a - j * c:b - j * c])
    return pieces


def _cols_of(sources, lo, hi):
    pieces = []
    for arr, col0, first, last in sources:
        a, b = max(lo, first), min(hi, last)
        if a < b:
            pieces.append(arr[:, col0 + a - first:col0 + b - first])
    return pieces


def _pad_lanes(a):
    return jnp.pad(a, ((0, 0), (0, LANES - a.shape[1])))


def kernel(x, norm_mix_g, norm_mlp_g, final_norm_g, w_in_even, w_out_even, gm_ln_g, gm_ln_b, gm_w_s, gm_b_s, ssm_conv_w, ssm_conv_b, ssm_dt_bias, ssm_a_log, ssm_d, ssm_norm_g, w_qkv, b_qkv, w_o, b_o, attn_sinks, w_up, w_down, loss_target, m_norm_mix_g, m_norm_mlp_g, m_final_norm_g, m_w_in_even, m_w_out_even, m_gm_ln_g, m_gm_ln_b, m_gm_w_s, m_gm_b_s, m_ssm_conv_w, m_ssm_conv_b, m_ssm_dt_bias, m_ssm_a_log, m_ssm_d, m_ssm_norm_g, m_w_qkv, m_b_qkv, m_w_o, m_b_o, m_attn_sinks, m_w_up, m_w_down, v_norm_mix_g, v_norm_mlp_g, v_final_norm_g, v_w_in_even, v_w_out_even, v_gm_ln_g, v_gm_ln_b, v_gm_w_s, v_gm_b_s, v_ssm_conv_w, v_ssm_conv_b, v_ssm_dt_bias, v_ssm_a_log, v_ssm_d, v_ssm_norm_g, v_w_qkv, v_b_qkv, v_w_o, v_b_o, v_attn_sinks, v_w_up, v_w_down):
    names = ["norm_mix_g", "norm_mlp_g", "final_norm_g", "w_in_even", "w_out_even", "gm_ln_g", "gm_ln_b", "gm_w_s",
             "gm_b_s", "ssm_conv_w", "ssm_conv_b", "ssm_dt_bias", "ssm_a_log", "ssm_d", "ssm_norm_g", "w_qkv",
             "b_qkv", "w_o", "b_o", "attn_sinks", "w_up", "w_down"]
    env = locals()
    W = {n: env[n] for n in names}
    M = {n: env["m_" + n] for n in names}
    V = {n: env["v_" + n] for n in names}
    big = ["w_in_even", "w_out_even", "w_qkv", "w_o", "w_up", "w_down"]
    small_sharded = ["ssm_conv_w", "b_qkv", "b_o"]
    replicated = [n for n in names if n not in big and n not in small_sharded]
    me = 4 * lax.axis_index("x") + 2 * lax.axis_index("y") + lax.axis_index("c")
    t = x.shape[1]
    xs = x.reshape(t, D_MODEL)
    target = loss_target.reshape(t, D_MODEL)
    gather = lambda a: _Move("gather", a)
    over_ici = lambda a: _Move("gather_ici", a)
    over_d2d = lambda a: _Move("gather_d2d", a)
    by_core = lambda a: a.reshape((N_CHIP, N_CORE) + a.shape[1:])
    to_sibling = lambda a: [_Move("scatter_d2d", by_core(a))]
    my_core = lax.axis_index("c").astype(jnp.int32).reshape(1)
    pair = lambda a, theirs, name: _pair_add(by_core(a), theirs, my_core, name=name)
    to_chips = lambda a: _Move("scatter_ici", a)
    whole = lambda a: a.reshape((N_DEV,) + a.shape[2:])
    row = lambda a: a.reshape(1, D_MODEL)

    small_flat = jnp.concatenate([W[n].reshape(-1) for n in small_sharded])
    w_in_g, small_g = _exchange([over_ici(w_in_even[0].astype(BF16)), gather(_as_rows(small_flat))],
                                name="gather_w_in", then_d2d=[0])
    w_in_s = whole(w_in_g)
    z_lo, xbc_lo, dt_lo = 2 * D_MODEL, 3 * D_MODEL, 3 * D_MODEL + CONV_DIM
    w_uv = jnp.concatenate(_shard_cols(w_in_s, 0, z_lo), axis=1)
    w_rest = jnp.concatenate(_shard_cols(w_in_s, xbc_lo, dt_lo) + _shard_cols(w_in_s, z_lo, xbc_lo)
                             + _shard_cols(w_in_s, dt_lo, IN_EVEN)
                             + [jnp.zeros((D_MODEL, LANES - SSM_HEADS), BF16)], axis=1)
    small_all = small_g.reshape(N_DEV, -1)
    n_cw = SSM_CONV * CONV_DIM // N_DEV
    n_bq = QKV_DIM // N_DEV
    conv_w = _cols_from_shards(small_all[:, :n_cw].reshape(N_DEV, SSM_CONV, CONV_DIM // N_DEV))
    bqkv = small_all[:, n_cw:n_cw + n_bq].reshape(1, QKV_DIM)
    bo = small_all[:, n_cw + n_bq:n_cw + n_bq + D_MODEL // N_DEV].reshape(1, D_MODEL)

    conv_b = ssm_conv_b.reshape(1, CONV_DIM)
    dt_bias, a_log, d_skip = _pad_lanes(ssm_dt_bias), _pad_lanes(ssm_a_log), _pad_lanes(ssm_d)
    gm_w = gm_w_s[0]
    gm_b = gm_b_s[0].reshape(GM_GROUPS, CHUNK, 1)
    sink_rows = jnp.repeat(jnp.transpose(attn_sinks.reshape(ATTN_KV, _PAIRS_PER_KV, 2), (0, 2, 1)), CHUNK,
                           axis=2).reshape(ATTN_KV, 2, _SINK_ROWS, 1)
    w_up_b, w_down_b = w_up.astype(BF16), w_down.astype(BF16)

    w_down0_a, w_down0_b = w_down_b[0, :FF_SHARD // 2], w_down_b[0, FF_SHARD // 2:]
    (y0, proj_uv), (w_qkv_g,) = _norm_matmul(xs, row(norm_mix_g[0]), w_uv, name="proj_uv", emit_y=True,
                                             moves=[over_ici(jnp.transpose(w_qkv[0]).astype(BF16))])
    proj_rest, (w_out_g,) = _norm_matmul(xs, row(norm_mix_g[0]), w_rest, name="proj_rest", emit_y=False,
                                         moves=[over_ici(w_out_even[0].astype(BF16))])
    mix, (w_down0_a, w_out_g, w_qkv_g) = _gmlp_fwd(
        proj_uv, gm_ln_g, gm_ln_b, gm_w, gm_b, name="gmlp_fwd",
        moves=[over_ici(w_down0_a), over_d2d(w_out_g), over_d2d(w_qkv_g)])
    (mix, h_states, conv_pre), (w_up0_g, w_down0_a) = _ssd_fwd(
        proj_rest, mix, conv_w, conv_b, dt_bias, a_log, d_skip, ssm_norm_g, name="ssd_fwd",
        moves=[over_ici(w_up_b[0]), over_d2d(w_down0_a)])
    w_out_f = whole(w_out_g).reshape(2 * D_MODEL, D_MODEL)
    (h1, y1), (w_down0_b, w_up0_g) = _residual_matmul(
        mix, w_out_f, xs, name="mix_out", norm_g=row(norm_mlp_g[0]),
        moves=[over_ici(w_down0_b), over_d2d(w_up0_g)])
    up0, (w_down0_b,) = _mlp_up(y1, whole(w_up0_g), name="mlp_up0", moves=[over_d2d(w_down0_b)])
    w_down_g = [[whole(w_down0_a), whole(w_down0_b)]]
    (h2, y2), (w_o_g,) = _mlp_down(up0, w_down_g[0], h1, name="mlp_down0", norm_g=row(norm_mix_g[1]),
                                   moves=[over_ici(w_o[0].astype(BF16))])
    wqkv = whole(w_qkv_g).reshape(QKV_DIM, D_MODEL)
    qkv, (w_o_g,) = _residual_matmul(y2, wqkv, None, name="qkv", bias=bqkv, w_transposed=True,
                                     moves=[over_d2d(w_o_g)])
    wo = whole(w_o_g).reshape(D_MODEL, D_MODEL)
    (attn, attn_p, attn_stats), (w_up1_g, w_down1_g) = _attn_fwd(
        qkv, sink_rows, name="attn_fwd", moves=[over_ici(w_up_b[1]), over_ici(w_down_b[1])])
    (h3, y3), (w_up1_g,) = _residual_matmul(attn, wo, h2, name="attn_out", bias=bo, norm_g=row(norm_mlp_g[1]),
                                            moves=[over_d2d(w_up1_g)])
    w_up_g = [whole(w_up0_g), whole(w_up1_g)]
    up1, (w_down1_g,) = _mlp_up(y3, w_up_g[1], name="mlp_up1", moves=[over_d2d(w_down1_g)])
    w_down_g.append([whole(w_down1_g)])
    loss_part, dh4, dh4_b, d_final_g = _mlp_down_loss(up1, w_down_g[1], h3, row(final_norm_g), target,
                                                      name="mlp_down1_loss")

    by_dev_rows = lambda a: a.reshape((N_DEV, a.shape[0] // N_DEV) + a.shape[1:])

    def mlp_bwd(dh, dh_b, h, y, up, layer, first_moves=()):
        res = _mlp_down_dx(dh_b, w_down_g[layer], up, name=f"mlp_down_dx{layer}", moves=first_moves)
        d_up, first_landed = res if first_moves else (res, [])
        g_down = _dw_by_rows(up, dh_b, name=f"mlp_down_dw{layer}", tk=FF_SHARD, square_relu=True)
        g_down = by_dev_rows(g_down)
        g_up, (theirs,) = _dw_by_cols(y, d_up, name=f"mlp_up_dw{layer}", tn=FF_SHARD, by_device=True,
                                      moves=to_sibling(g_down))
        q_down = pair(g_down, theirs, f"mlp_down_pair{layer}")
        (dh_new, dh_new_b, dg, cs), (theirs,) = _dx_norm(
            d_up, w_up_g[layer], h, row(norm_mlp_g[layer]), dh, name=f"mlp_up_dx{layer}", by_device_cols=True,
            moves=to_sibling(g_up))
        q_up = pair(g_up, theirs, f"mlp_up_pair{layer}")
        return dh_new, dh_new_b, cs, dg, q_up, q_down, first_landed

    dh3, dh3_b, cs3, g_nmlp1, q_up1, q_down1, _ = mlp_bwd(dh4, dh4_b, h3, y3, up1, 1)
    g_bo = cs3
    g_wo = by_dev_rows(_dw_by_cols(attn, dh3_b, name="attn_out_dw", tn=FF_SHARD))
    d_attn, (theirs,) = _dx(dh3_b, wo, name="attn_out_dx", moves=to_sibling(g_wo))
    q_wo = pair(g_wo, theirs, "attn_out_pair")
    (dqkv, d_sink), (r_down1, r_up1) = _attn_bwd(qkv, attn_p, attn_stats, attn, d_attn, name="attn_bwd",
                                                 moves=[to_chips(q_down1), to_chips(q_up1)])
    g_wqkv, g_bqkv = _dw_by_rows(dqkv, y2, name="qkv_dw", tk=QKV_DIM // 2, column_sums=True)
    g_wqkv = by_dev_rows(g_wqkv)
    (dh2, dh2_b, g_nmix1, _), (theirs,) = _dx_norm(dqkv, wqkv, h2, row(norm_mix_g[1]), dh3, name="qkv_dx",
                                                   w_transposed=True, moves=to_sibling(g_wqkv))
    q_wqkv = pair(g_wqkv, theirs, "qkv_pair")
    dh1, dh1_b, _, g_nmlp0, q_up0, q_down0, (r_wqkv, r_wo) = mlp_bwd(
        dh2, dh2_b, h1, y1, up0, 0, first_moves=[to_chips(q_wqkv), to_chips(q_wo)])

    d_mix = _dx(dh1_b, w_out_f, name="mix_out_dx")
    g_wout = by_dev_rows(_dw_by_rows(mix, dh1_b, name="mix_out_dw", tk=FF_SHARD))
    (d_uv, g_ln_g, g_ln_b, g_gm_w, g_gm_b), (r_down0, theirs) = _gmlp_bwd(
        proj_uv, d_mix, gm_ln_g, gm_ln_b, gm_w, gm_b, name="gmlp_bwd", moves=[to_chips(q_down0)] + to_sibling(g_wout))
    q_wout = pair(g_wout, theirs, "mix_out_pair")

    early = [("norm_mlp_g", None), ("final_norm_g", None), ("norm_mix_g", 1), ("gm_ln_g", None), ("gm_ln_b", None),
             ("gm_w_s", None), ("gm_b_s", None), ("attn_sinks", None)]
    late = [("norm_mix_g", 0), ("ssm_conv_b", None), ("ssm_dt_bias", None), ("ssm_a_log", None), ("ssm_d", None),
            ("ssm_norm_g", None)]
    early_sharded, late_sharded = ["b_qkv", "b_o"], ["ssm_conv_w"]
    small_grads = {
        ("norm_mlp_g", None): jnp.concatenate([g_nmlp0, g_nmlp1], axis=0),
        ("final_norm_g", None): d_final_g, ("norm_mix_g", 1): g_nmix1,
        ("gm_ln_g", None): g_ln_g, ("gm_ln_b", None): g_ln_b, ("gm_w_s", None): g_gm_w, ("gm_b_s", None): g_gm_b,
        ("attn_sinks", None): jnp.transpose(
            jnp.sum(d_sink.reshape(ATTN_KV, 2, _PAIRS_PER_KV, CHUNK), axis=3), (0, 2, 1)),
        "b_qkv": g_bqkv, "b_o": g_bo,
    }
    pack = lambda keys: _as_rows(jnp.concatenate([small_grads[key].reshape(-1) for key in keys]))
    (dpre, dz, ddt, g_dtb, g_alog, g_dskip, g_ssm_ng), (r_up0, r_wout, early_recv) = _ssd_bwd(
        proj_rest, conv_pre, h_states, d_mix, dt_bias, a_log, d_skip, ssm_norm_g, name="ssd_bwd",
        moves=[to_chips(q_up0), to_chips(q_wout), gather(pack(early + early_sharded))])
    d_rest, g_conv_w, g_conv_b = _conv_bwd(proj_rest, dpre, dz, ddt, conv_w, name="conv_bwd")
    g_w_uv = _dw_by_cols(y0, d_uv, name="proj_uv_dw", tn=FF_SHARD)
    g_w_rest = _dw_by_cols(y0, d_rest, name="proj_rest_dw", tn=REST_W // 5)
    in_cols = [(g_w_uv, 0, 0, z_lo), (g_w_rest, CONV_DIM, z_lo, xbc_lo), (g_w_rest, 0, xbc_lo, dt_lo),
               (g_w_rest, CONV_DIM + D_MODEL, dt_lo, IN_EVEN)]
    in_shard = IN_EVEN // N_DEV
    g_w_in = jnp.stack([jnp.concatenate(_cols_of(in_cols, j * in_shard, (j + 1) * in_shard), axis=1)
                        for j in range(N_DEV)])
    dy0, (theirs,) = _dx(d_uv, w_uv, name="proj_uv_dx", moves=to_sibling(g_w_in))
    q_w_in = pair(g_w_in, theirs, "proj_pair")
    (dx, _, g_nmix0, _), r_w_in = _dx_norm(d_rest, w_rest, xs, row(norm_mix_g[0]), dh1, name="proj_rest_dx",
                                           partial=dy0, moves=[to_chips(q_w_in)])
    small_grads.update({
        ("loss", None): loss_part[:1, :1],
        ("norm_mix_g", 0): g_nmix0, ("ssm_conv_b", None): g_conv_b,
        ("ssm_dt_bias", None): g_dtb[:, :SSM_HEADS], ("ssm_a_log", None): g_alog[:, :SSM_HEADS],
        ("ssm_d", None): g_dskip[:, :SSM_HEADS], ("ssm_norm_g", None): g_ssm_ng, "ssm_conv_w": g_conv_w,
    })


    def update(n, parts, moves=(), transposed=False):
        shape = W[n].shape
        if transposed:
            as3 = lambda a: jnp.transpose(a[0])[None]
            back = lambda a: jnp.transpose(a[0])[None]
        else:
            as3 = lambda a: a.reshape((len(parts),) + parts[0].shape[1:])
            back = lambda a: a.reshape(shape)
        res = _adamw(parts, as3(W[n]), as3(M[n]), as3(V[n]), name="adamw_" + n, moves=moves)
        res, landed = res if moves else (res, [])
        return [back(a) for a in res], landed

    out = {}
    late_keys = late + late_sharded + [("loss", None)]
    out["w_o"], (late_recv,) = update("w_o", [r_wo], moves=[gather(pack(late_keys))])
    out["w_down"], _ = update("w_down", [r_down0, r_down1])
    out["w_up"], _ = update("w_up", [r_up0, r_up1])
    out["w_out_even"], _ = update("w_out_even", [r_wout])
    out["w_qkv"], _ = update("w_qkv", [r_wqkv], transposed=True)
    out["w_in_even"], _ = update("w_in_even", list(r_w_in))

    def unpacked(recv, keys):
        flat, res, o = recv.reshape(N_DEV, -1), {}, 0
        for key in keys:
            res[key] = flat[:, o:o + small_grads[key].size]
            o += small_grads[key].size
        return res

    arrived = {**unpacked(early_recv, early + early_sharded), **unpacked(late_recv, late_keys)}
    piece = lambda tree, key: tree[key[0]] if key[1] is None else tree[key[0]][key[1]]

    def rows_by_device(cat):
        pad = (-cat.shape[1]) % PACK_W
        return jnp.pad(cat, ((0, 0), (0, pad))).reshape(N_DEV, -1, PACK_W)

    rep_keys = early + late
    rep_parts = rows_by_device(jnp.concatenate([arrived[key] for key in rep_keys], axis=1))
    flat_rep = lambda tree: _as_rows(jnp.concatenate([piece(tree, key).reshape(-1) for key in rep_keys]))[None]
    rep_res = _adamw([rep_parts], flat_rep(W), flat_rep(M), flat_rep(V), name="adamw_replicated")
    sh_keys = early_sharded + late_sharded
    shard_parts = []
    for n in sh_keys:
        full = arrived[n].reshape((N_DEV,) + small_grads[n].shape)
        c = full.shape[-1] // N_DEV
        shard_parts.append(lax.dynamic_slice_in_dim(full, me * c, c, axis=full.ndim - 1).reshape(N_DEV, -1))
    sh_rows = rows_by_device(jnp.concatenate(shard_parts, axis=1))
    flat_sh = lambda tree: _as_rows(jnp.concatenate([tree[n].reshape(-1) for n in sh_keys]))[None]
    sh_res = _adamw([sh_rows], flat_sh(W), flat_sh(M), flat_sh(V), name="adamw_small_sharded")

    def unpack_replicated(rows):
        flat, vals, o = rows.reshape(-1), {}, 0
        for key in rep_keys:
            size = piece(W, key).size
            vals[key] = flat[o:o + size]
            o += size
        res = {}
        for n in replicated:
            if (n, None) in vals:
                res[n] = vals[(n, None)].reshape(W[n].shape)
            else:
                res[n] = jnp.stack([vals[(n, r)] for r in range(W[n].shape[0])]).reshape(W[n].shape)
        return res

    def unpack_sharded(rows):
        flat, res, o = rows.reshape(-1), {}, 0
        for n in sh_keys:
            res[n] = flat[o:o + W[n].size].reshape(W[n].shape)
            o += W[n].size
        return res

    results = []
    for idx in range(4):
        d = {n: out[n][idx] for n in big}
        d.update(unpack_replicated(rep_res[idx]))
        d.update(unpack_sharded(sh_res[idx]))
        results.append(d)

    loss = jnp.sum(arrived[("loss", None)])
    grad_x = dx.reshape(x.shape)
    final = [loss, grad_x]
    for d in results:
        final.extend(d[n] for n in names)
    return tuple(final)
```

```python
import dataclasses
import functools

import jax
import jax.numpy as jnp
from jax import lax
from jax.experimental import pallas as pl
from jax.experimental.pallas import tpu as pltpu

F32 = jnp.float32
BF16 = jnp.bfloat16

N_DEV = 8
D_MODEL = 1024
D_FF = 4096
RMS_EPS = 1e-5
LN_EPS = 1e-5
CHUNK = 128
GM_GROUPS = 8
SSM_HEADS = 16
SSM_HEADDIM = 64
SSM_GROUPS = 4
SSM_STATE = 128
SSM_CONV = 4
CONV_DIM = 2048
IN_EVEN = 5136
REST_W = 3200
ATTN_HEADS = 16
ATTN_KV = 2
HEAD_DIM = 64
QKV_DIM = 1280
LANES = 128
HALO = 8
PACK_W = 1024

ADAM_LR = 0.001
ADAM_B1 = 0.9
ADAM_B2 = 0.999
ADAM_EPS = 1e-08
ADAM_WD = 0.01
ADAM_STEP = 10

VMEM_LIMIT_BYTES = 56 * 1024 * 1024


_NN = (((1,), (0,)), ((), ()))
_NT = (((1,), (1,)), ((), ()))
_TN = (((0,), (0,)), ((), ()))


def _dg(a, b, dims):
    return lax.dot_general(a.astype(BF16), b.astype(BF16), dims, preferred_element_type=F32)


@jax.custom_vjp
def _nn(a, b):
    return _dg(a, b, _NN)


@jax.custom_vjp
def _nt(a, b):
    return _dg(a, b, _NT)


@jax.custom_vjp
def _tn(a, b):
    return _dg(a, b, _TN)


_nn.defvjp(lambda a, b: (_dg(a, b, _NN), (a, b)), lambda r, g: (_nt(g, r[1]), _tn(r[0], g)))
_nt.defvjp(lambda a, b: (_dg(a, b, _NT), (a, b)), lambda r, g: (_nn(g, r[1]), _tn(g, r[0])))
_tn.defvjp(lambda a, b: (_dg(a, b, _TN), (a, b)), lambda r, g: (_nt(r[1], g), _nn(r[0], g)))


def _split3_dot(tri, x):
    x1 = x.astype(BF16)
    r1 = x - x1.astype(F32)
    x2 = r1.astype(BF16)
    x3 = (r1 - x2.astype(F32)).astype(BF16)
    t = tri.astype(BF16)
    dot = lambda p: lax.dot_general(t, p, _NN, preferred_element_type=F32)
    return dot(x1) + dot(x2) + dot(x3)


def _tri(lower):
    r = lax.broadcasted_iota(jnp.int32, (CHUNK, CHUNK), 0)
    c = lax.broadcasted_iota(jnp.int32, (CHUNK, CHUNK), 1)
    return jnp.where((r >= c) if lower else (r <= c), 1.0, 0.0).astype(F32)


@jax.custom_vjp
def _cumsum_rows(x):
    return _split3_dot(_tri(True), x)


_cumsum_rows.defvjp(lambda x: (_split3_dot(_tri(True), x), None), lambda _, g: (_split3_dot(_tri(False), g),))


def _sigmoid(x):
    return 1.0 / (1.0 + jnp.exp(-x))


def _silu(x):
    return x * _sigmoid(x)


def _softplus(x):
    return jnp.maximum(x, 0.0) + jnp.log(1.0 + jnp.exp(-jnp.abs(x)))


def _gelu_tanh(x):
    return 0.5 * x * (1.0 + jnp.tanh(0.7978845608028654 * (x + 0.044715 * (x * x * x))))


def _rmsnorm(x, g):
    return x * lax.rsqrt(jnp.mean(x * x, axis=-1, keepdims=True) + RMS_EPS) * g


def _gmlp_chunk(u, v, ln_g, ln_b, w_s, b_s):
    gu = _gelu_tanh(u)
    gv = _gelu_tanh(v)
    mu = jnp.mean(gv, axis=-1, keepdims=True)
    var = jnp.mean(jnp.square(gv - mu), axis=-1, keepdims=True)
    vn = (gv - mu) * lax.rsqrt(var + LN_EPS) * ln_g + ln_b
    r = lax.broadcasted_iota(jnp.int32, (CHUNK, CHUNK), 0)
    c = lax.broadcasted_iota(jnp.int32, (CHUNK, CHUNK), 1)
    causal = r >= c
    outs = []
    for g in range(GM_GROUPS):
        cols = slice(g * LANES, (g + 1) * LANES)
        mixed = _nn(jnp.where(causal, w_s[g], 0.0), vn[:, cols]) + b_s[g]
        outs.append(gu[:, cols] * mixed)
    return jnp.concatenate(outs, axis=1)


def _lane_pick(row, h):
    lane = lax.broadcasted_iota(jnp.int32, row.shape, 1)
    return jnp.sum(jnp.where(lane == h, row, 0.0), axis=1, keepdims=True)


def _col_pick(m, h):
    lane = lax.broadcasted_iota(jnp.int32, m.shape, 1)
    return jnp.sum(jnp.where(lane == h, m, 0.0), axis=1, keepdims=True)


def _row_pick(m, h):
    sub = lax.broadcasted_iota(jnp.int32, m.shape, 0)
    return jnp.sum(jnp.where(sub == h, m, 0.0), axis=0, keepdims=True)


_PAIRS = SSM_HEADS // 2


def _ssd_chunk(pre, z, dt_raw, h_prev, dt_bias, a_log, d_skip, norm_g):
    xbc = _silu(pre)
    dt = _softplus(dt_raw + dt_bias)
    da = dt * (-jnp.exp(a_log))
    a_cum = _cumsum_rows(da)
    a_cum_t = a_cum.T
    dt_t = dt.T
    r = lax.broadcasted_iota(jnp.int32, (CHUNK, CHUNK), 0)
    c = lax.broadcasted_iota(jnp.int32, (CHUNK, CHUNK), 1)
    causal = r >= c
    lane_lo = lax.broadcasted_iota(jnp.int32, (1, LANES), 1) < SSM_HEADDIM
    last_row = lax.broadcasted_iota(jnp.int32, (CHUNK, 1), 0) == CHUNK - 1
    bms = [xbc[:, 1024 + g * SSM_STATE:1024 + (g + 1) * SSM_STATE] for g in range(SSM_GROUPS)]
    cms = [xbc[:, 1536 + g * SSM_STATE:1536 + (g + 1) * SSM_STATE] for g in range(SSM_GROUPS)]
    cbs = [_nt(cms[g], bms[g]) for g in range(SSM_GROUPS)]
    w_intra, to_end, e_cum, c_dec, d_row = [], [], [], [], []
    for h in range(SSM_HEADS):
        col = _col_pick(a_cum, h)
        row = _row_pick(a_cum_t, h)
        dt_col = _col_pick(dt, h)
        dt_row = _row_pick(dt_t, h)
        decay = jnp.exp(jnp.where(causal, col - row, -jnp.inf))
        w_intra.append(cbs[h // (SSM_HEADS // SSM_GROUPS)] * decay * dt_row)
        last = jnp.sum(jnp.where(last_row, col, 0.0), axis=0, keepdims=True)
        to_end.append(jnp.exp(last - col) * dt_col)
        e_cum.append(jnp.exp(col))
        c_dec.append(jnp.exp(last))
        d_row.append(_lane_pick(d_skip, h))
    pair = lambda vals, j: jnp.where(lane_lo, vals[2 * j], vals[2 * j + 1])
    ys, h_next = [], []
    for j in range(_PAIRS):
        g = j // 2
        xs = xbc[:, j * LANES:(j + 1) * LANES]
        y_diag = jnp.where(lane_lo, _nn(w_intra[2 * j], xs), _nn(w_intra[2 * j + 1], xs))
        states = _tn(bms[g], xs * pair(to_end, j))
        y_off = _nn(cms[g], h_prev[j]) * pair(e_cum, j)
        ys.append(y_diag + y_off + xs * pair(d_row, j))
        h_next.append(pair(c_dec, j) * h_prev[j] + states)
    y = jnp.concatenate(ys, axis=1) * _silu(z)
    width = D_MODEL // SSM_GROUPS
    y = jnp.concatenate(
        [_rmsnorm(y[:, g * width:(g + 1) * width], norm_g[:, g * width:(g + 1) * width]) for g in range(SSM_GROUPS)],
        axis=1)
    return y, tuple(h_next)


def _shift_down(prev8, x, k):
    if k == 0:
        return x
    win = jnp.concatenate([prev8, x], axis=0)
    return pltpu.roll(win, k, 0)[HALO:]


def _shift_up(x, next8, k):
    if k == 0:
        return x
    n = x.shape[0]
    win = jnp.concatenate([x, next8], axis=0)
    return pltpu.roll(win, n + HALO - k, 0)[:n]


def _conv_pre(prev8, x, w, b):
    out = b + x * w[SSM_CONV - 1:SSM_CONV]
    for i in range(SSM_CONV - 1):
        out = out + _shift_down(prev8, x, SSM_CONV - 1 - i) * w[i:i + 1]
    return out


def _swap_halves(x):
    return pltpu.roll(x, HEAD_DIM, 1)


_PAIRS_PER_KV = ATTN_HEADS // ATTN_KV // 2
_ATTN_SCALE = HEAD_DIM ** -0.5


def _parity_lanes(parity):
    lane = lax.broadcasted_iota(jnp.int32, (1, LANES), 1)
    return (lane >= HEAD_DIM * parity) & (lane < HEAD_DIM * (parity + 1))


def _kv_placed(pair, kv_head):
    mine = jnp.where(_parity_lanes(kv_head), pair, 0.0)
    lo = mine if kv_head == 0 else _swap_halves(mine)
    return lo, _swap_halves(lo)


def _kv_unplaced(d_lo, d_hi, kv_head):
    d = jnp.where(_parity_lanes(0), d_lo, 0.0) + _swap_halves(jnp.where(_parity_lanes(1), d_hi, 0.0))
    return d if kv_head == 0 else _swap_halves(d)


def _band_mask(first):
    shape = (_PAIRS_PER_KV * CHUNK, 2 * CHUNK)
    rows = lax.broadcasted_iota(jnp.int32, shape, 0) & (CHUNK - 1)
    cols = lax.broadcasted_iota(jnp.int32, shape, 1)
    return (cols <= rows + CHUNK) & (cols > rows) & (cols >= CHUNK * first.astype(jnp.int32))


def _attn_probs(s, sink):
    m = jnp.maximum(jnp.max(s, axis=-1, keepdims=True), sink)
    p = jnp.exp(s - m)
    e_sink = jnp.exp(sink - m)
    return p, e_sink, jnp.sum(p, axis=-1, keepdims=True) + e_sink


def _lane_column(col, idx):
    lane = lax.broadcasted_iota(jnp.int32, (1, LANES), 1)
    return jnp.where(lane == idx, col, 0.0)


N_CHIP = 4
N_CORE = 2
_OTHER_CHIPS = (2, 4, 6)


@dataclasses.dataclass
class _Move:
    kind: str
    src: jax.Array

    def dst_shape(self):
        s = self.src.shape
        shape = {"gather": (N_DEV,) + s, "gather_ici": (N_CHIP, N_CORE) + s, "gather_d2d": s,
                 "scatter_d2d": (N_CHIP,) + s[2:], "scatter_ici": s}[self.kind]
        return jax.ShapeDtypeStruct(tuple(shape), self.src.dtype)


def _peer(x, y, c, k):
    return (1 - x if k & 4 else x, 1 - y if k & 2 else y, 1 - c if k & 1 else c)


def _move_copies(moves, srcs, dsts, send_sems, recv_sems, local_sems):
    x, y, c = lax.axis_index("x"), lax.axis_index("y"), lax.axis_index("c")
    chip = 2 * x + y
    me = 2 * chip + c
    sibling = (x, y, 1 - c)
    all_chips = pl.ds(0, N_CHIP)
    local, remote = [], []

    def push(n, k, src, dst, device):
        remote.append(pltpu.make_async_remote_copy(
            src_ref=src, dst_ref=dst, send_sem=send_sems.at[n, k], recv_sem=recv_sems.at[n, k],
            device_id=device, device_id_type=pl.DeviceIdType.MESH))

    for n, mv in enumerate(moves):
        s, d = srcs[n], dsts[n]
        if mv.kind == "gather":
            local.append(pltpu.make_async_copy(s, d.at[me], local_sems.at[n]))
            for k in range(1, N_DEV):
                push(n, k - 1, s, d.at[me], _peer(x, y, c, k))
        elif mv.kind == "gather_ici":
            local.append(pltpu.make_async_copy(s, d.at[chip, c], local_sems.at[n]))
            for k in _OTHER_CHIPS:
                push(n, k - 1, s, d.at[chip, c], _peer(x, y, c, k))
        elif mv.kind == "gather_d2d":
            push(n, 0, d.at[all_chips, c], d.at[all_chips, c], sibling)
        elif mv.kind == "scatter_d2d":
            push(n, 0, s.at[all_chips, 1 - c], d, sibling)
        else:
            assert mv.kind == "scatter_ici", mv.kind
            local.append(pltpu.make_async_copy(s.at[chip], d.at[chip], local_sems.at[n]))
            for k in _OTHER_CHIPS:
                px, py, _ = _peer(x, y, c, k)
                push(n, k - 1, s.at[2 * px + py], d.at[chip], (px, py, c))
    return local, remote


def _move_aliases(moves, n_in, n_out):
    return {n_in + n: n_out + n for n, mv in enumerate(moves) if mv.kind == "gather_d2d"}


def _pcall(body, *, name, grid, in_specs, out_specs, out_shape, scratch_shapes=(), semantics=(), moves=(),
           aliases=None):
    out_shape, out_specs = list(out_shape), list(out_specs)
    in_specs = list(in_specs)
    if not moves:
        call = pl.pallas_call(
            body, name=name, grid=grid, in_specs=in_specs, out_specs=out_specs, out_shape=out_shape,
            scratch_shapes=list(scratch_shapes), input_output_aliases=aliases or {},
            compiler_params=pltpu.CompilerParams(dimension_semantics=tuple(semantics),
                                                 vmem_limit_bytes=VMEM_LIMIT_BYTES))
        return (lambda *args: (list(call(*args)), []))
    n_in, n_out, n_scr, n_mv = len(in_specs), len(out_shape), len(scratch_shapes), len(moves)
    hbm = pl.BlockSpec(memory_space=pltpu.HBM)

    def carrier(*refs):
        ins, rest = refs[:n_in], refs[n_in:]
        srcs, rest = rest[:n_mv], rest[n_mv:]
        outs, rest = rest[:n_out], rest[n_out:]
        dsts, rest = rest[:n_mv], rest[n_mv:]
        scr, (send_sems, recv_sems, local_sems) = rest[:n_scr], rest[n_scr:]
        first = functools.reduce(jnp.logical_and, [pl.program_id(d) == 0 for d in range(len(grid))])
        last = functools.reduce(jnp.logical_and, [pl.program_id(d) == grid[d] - 1 for d in range(len(grid))])

        @pl.when(first)
        def _():
            local, remote = _move_copies(moves, srcs, dsts, send_sems, recv_sems, local_sems)
            for cp in local + remote:
                cp.start()

        body(*ins, *outs, *scr)

        @pl.when(last)
        def _():
            local, remote = _move_copies(moves, srcs, dsts, send_sems, recv_sems, local_sems)
            for cp in remote + local:
                cp.wait()

    call = pl.pallas_call(
        carrier, name=name, grid=grid,
        in_specs=in_specs + [hbm] * n_mv,
        out_specs=out_specs + [hbm] * n_mv,
        out_shape=out_shape + [mv.dst_shape() for mv in moves],
        scratch_shapes=list(scratch_shapes) + [pltpu.SemaphoreType.DMA((n_mv, N_DEV - 1)),
                                               pltpu.SemaphoreType.DMA((n_mv, N_DEV - 1)),
                                               pltpu.SemaphoreType.DMA((n_mv,))],
        input_output_aliases={**(aliases or {}), **_move_aliases(moves, n_in, n_out)},
        compiler_params=pltpu.CompilerParams(dimension_semantics=("arbitrary",) * len(grid),
                                             vmem_limit_bytes=VMEM_LIMIT_BYTES))

    def run(*args):
        res = list(call(*args, *[mv.src for mv in moves]))
        return res[:n_out], res[n_out:]

    return run


def _exchange(moves, *, name, then_d2d=()):
    n_mv, n_fwd = len(moves), len(then_d2d)
    hbm = pl.BlockSpec(memory_space=pltpu.HBM)
    copies_of = {"gather": N_DEV - 1, "gather_ici": len(_OTHER_CHIPS), "gather_d2d": 1, "scatter_d2d": 1,
                 "scatter_ici": len(_OTHER_CHIPS)}
    first_copy = [sum(copies_of[mv.kind] for mv in moves[:n]) for n in range(n_mv)]

    def body(*refs):
        srcs, dsts, sems = refs[:n_mv], refs[n_mv:2 * n_mv], refs[2 * n_mv:]
        local, remote = _move_copies(moves, srcs, dsts, *sems[:3])
        for cp in local + remote:
            cp.start()
        x, y, c = lax.axis_index("x"), lax.axis_index("y"), lax.axis_index("c")
        chip, sibling = 2 * x + y, (x, y, 1 - c)
        passed, passed_on = [], set()

        def to_sibling(f, k, src, slot):
            cp = pltpu.make_async_remote_copy(src_ref=src, dst_ref=slot, send_sem=sems[3].at[f, k],
                                              recv_sem=sems[4].at[f, k], device_id=sibling,
                                              device_id_type=pl.DeviceIdType.MESH)
            cp.start()
            passed.append(cp)

        for f, n in enumerate(then_d2d):
            assert moves[n].kind == "gather_ici"
            d = dsts[n]
            to_sibling(f, 0, srcs[n], d.at[chip, c])
            for i, k in enumerate(_OTHER_CHIPS):
                remote[first_copy[n] + i].wait_recv()
                passed_on.add(first_copy[n] + i)
                px, py, _ = _peer(x, y, c, k)
                to_sibling(f, k - 1, d.at[2 * px + py, c], d.at[2 * px + py, c])
        for i, cp in enumerate(remote):
            if i in passed_on:
                cp.wait_send()
            else:
                cp.wait()
        for cp in local + passed:
            cp.wait()

    sems = [pltpu.SemaphoreType.DMA((n_mv, N_DEV - 1)), pltpu.SemaphoreType.DMA((n_mv, N_DEV - 1)),
            pltpu.SemaphoreType.DMA((n_mv,))]
    if then_d2d:
        sems += [pltpu.SemaphoreType.DMA((n_fwd, N_DEV - 1)), pltpu.SemaphoreType.DMA((n_fwd, N_DEV - 1))]
    return list(pl.pallas_call(
        body, name=name, in_specs=[hbm] * n_mv, out_specs=[hbm] * n_mv,
        out_shape=[mv.dst_shape() for mv in moves], scratch_shapes=sems,
    )(*[mv.src for mv in moves]))


TM = 512
FF_SHARD = D_FF // N_DEV


def _whole(a):
    nd = a.ndim
    return pl.BlockSpec(a.shape, lambda i: (0,) * nd)


def _rows(width, col=0):
    return pl.BlockSpec((TM, width), lambda i: (i, col))


def _acc_row(width):
    return pl.BlockSpec((1, width), lambda i: (0, 0))


def _unpack(res_landed, moves, n_out):
    res, landed = res_landed
    res = res[0] if n_out == 1 else res
    return (res, landed) if moves else res


def _norm_matmul(x, g, w, *, name, emit_y, moves=()):
    t, d = x.shape
    n = w.shape[1]

    def body(x_ref, g_ref, w_ref, *outs):
        y = _rmsnorm(x_ref[...], g_ref[...]).astype(BF16)
        if emit_y:
            outs[0][...] = y
        outs[-1][...] = lax.dot_general(y, w_ref[...], _NN, preferred_element_type=F32)

    shapes = ([jax.ShapeDtypeStruct((t, d), BF16)] if emit_y else []) + [jax.ShapeDtypeStruct((t, n), F32)]
    specs = ([_rows(d)] if emit_y else []) + [_rows(n)]
    return _unpack(_pcall(body, name=name, grid=(t // TM,), in_specs=[_rows(d), _acc_row(d), _whole(w)],
                          out_specs=specs, out_shape=shapes, semantics=("parallel",), moves=moves)(x, g, w),
                   moves, len(shapes))


def _residual_matmul(a, w, res, *, name, bias=None, norm_g=None, w_transposed=False, moves=()):
    t, k = a.shape
    n = w.shape[0 if w_transposed else 1]
    contract = _NT if w_transposed else _NN
    has_res, has_bias, has_norm = res is not None, bias is not None, norm_g is not None

    def body(a_ref, w_ref, *rest):
        rest = list(rest)
        res_ref = rest.pop(0) if has_res else None
        b_ref = rest.pop(0) if has_bias else None
        g_ref = rest.pop(0) if has_norm else None
        h = lax.dot_general(a_ref[...].astype(BF16), w_ref[...], contract, preferred_element_type=F32)
        if has_res:
            h = h + res_ref[...]
        if has_bias:
            h = h + b_ref[...]
        rest[0][...] = h
        if has_norm:
            rest[1][...] = _rmsnorm(h, g_ref[...]).astype(BF16)

    rows_in = [res] if has_res else []
    extra = ([bias] if has_bias else []) + ([norm_g] if has_norm else [])
    shapes = [jax.ShapeDtypeStruct((t, n), F32)] + ([jax.ShapeDtypeStruct((t, n), BF16)] if has_norm else [])
    return _unpack(_pcall(body, name=name, grid=(t // TM,),
                          in_specs=[_rows(k), _whole(w)] + [_rows(n)] * len(rows_in) + [_acc_row(n)] * len(extra),
                          out_specs=[_rows(n)] * len(shapes), out_shape=shapes, semantics=("parallel",),
                          moves=moves)(a, w, *rows_in, *extra), moves, len(shapes))


def _mlp_up(y, w_cols, *, name, moves=()):
    t, d = y.shape

    def body(y_ref, w_ref, up_ref):
        yv = y_ref[...]
        for j in range(N_DEV):
            up_ref[:, j * FF_SHARD:(j + 1) * FF_SHARD] = lax.dot_general(
                yv, w_ref[j], _NN, preferred_element_type=F32).astype(up_ref.dtype)

    return _unpack(_pcall(body, name=name, grid=(t // TM,), in_specs=[_rows(d), _whole(w_cols)],
                          out_specs=[_rows(D_FF)], out_shape=[jax.ShapeDtypeStruct((t, D_FF), BF16)],
                          semantics=("parallel",), moves=moves)(y, w_cols), moves, 1)


def _sq_relu(u):
    return jnp.square(jnp.maximum(u.astype(F32), 0.0))


def _down_blocks(w_refs):
    for j in range(N_DEV):
        off = j * FF_SHARD
        for w_ref in w_refs:
            yield off, w_ref.shape[1], w_ref[j]
            off += w_ref.shape[1]


def _mlp_down(up, w_rows, res, *, name, norm_g=None, moves=()):
    t = up.shape[0]
    has_norm = norm_g is not None
    n_w = len(w_rows)

    def body(up_ref, *rest):
        w_refs, res_ref, rest = rest[:n_w], rest[n_w], rest[n_w + 1:]
        h = res_ref[...]
        for off, rows, w_blk in _down_blocks(w_refs):
            act = _sq_relu(up_ref[:, off:off + rows]).astype(BF16)
            h = h + lax.dot_general(act, w_blk, _NN, preferred_element_type=F32)
        if has_norm:
            g_ref, h_ref, y_ref = rest
            y_ref[...] = _rmsnorm(h, g_ref[...]).astype(BF16)
        else:
            (h_ref,) = rest
        h_ref[...] = h

    shapes = [jax.ShapeDtypeStruct((t, D_MODEL), F32)] + ([jax.ShapeDtypeStruct((t, D_MODEL), BF16)] if has_norm else [])
    return _unpack(_pcall(body, name=name, grid=(t // TM,),
                          in_specs=[_rows(D_FF)] + [_whole(w) for w in w_rows] + [_rows(D_MODEL)]
                          + ([_acc_row(D_MODEL)] if has_norm else []),
                          out_specs=[_rows(D_MODEL)] * len(shapes), out_shape=shapes, semantics=("parallel",),
                          moves=moves)(up, *w_rows, res, *([norm_g] if has_norm else [])), moves, len(shapes))


def _mlp_down_dx(dh, w_rows, up, *, name, moves=()):
    t = up.shape[0]
    n_w = len(w_rows)

    def body(dh_ref, *rest):
        w_refs, (up_ref, o_ref) = rest[:n_w], rest[n_w:]
        dhv = dh_ref[...]
        for off, rows, w_blk in _down_blocks(w_refs):
            cols = slice(off, off + rows)
            d_act = lax.dot_general(dhv, w_blk, _NT, preferred_element_type=F32)
            o_ref[:, cols] = (d_act * (2.0 * jnp.maximum(up_ref[:, cols].astype(F32), 0.0))).astype(o_ref.dtype)

    return _unpack(_pcall(body, name=name, grid=(t // TM,),
                          in_specs=[_rows(D_MODEL)] + [_whole(w) for w in w_rows] + [_rows(D_FF)],
                          out_specs=[_rows(D_FF)], out_shape=[jax.ShapeDtypeStruct((t, D_FF), BF16)],
                          semantics=("parallel",), moves=moves)(dh, *w_rows, up), moves, 1)


def _dw_by_cols(x, dy, *, name, tn, by_device=False, moves=()):
    t, k = x.shape
    n = dy.shape[1]
    assert n % tn == 0, (name, n, tn)

    def body(x_ref, dy_ref, o_ref):
        o_ref[...] = lax.dot_general(x_ref[...].astype(BF16), dy_ref[...].astype(BF16), _TN,
                                     preferred_element_type=F32).astype(o_ref.dtype)

    if by_device:
        out_spec, out_shape = pl.BlockSpec((None, k, tn), lambda j: (j, 0, 0)), (n // tn, k, tn)
    else:
        out_spec, out_shape = pl.BlockSpec((k, tn), lambda j: (0, j)), (k, n)
    return _unpack(_pcall(body, name=name, grid=(n // tn,),
                          in_specs=[_whole(x), pl.BlockSpec((t, tn), lambda j: (0, j))],
                          out_specs=[out_spec], out_shape=[jax.ShapeDtypeStruct(out_shape, BF16)],
                          semantics=("parallel",), moves=moves)(x, dy), moves, 1)


def _dw_by_rows(x, dy, *, name, tk, square_relu=False, column_sums=False, moves=()):
    t, k = x.shape
    n = dy.shape[1]
    assert k % tk == 0, (name, k, tk)

    def body(x_ref, dy_ref, o_ref, *sums):
        xv = _sq_relu(x_ref[...]) if square_relu else x_ref[...]
        o_ref[...] = lax.dot_general(xv.astype(BF16), dy_ref[...].astype(BF16), _TN,
                                     preferred_element_type=F32).astype(o_ref.dtype)
        if column_sums:
            sums[0][...] = jnp.sum(xv.astype(F32), axis=0, keepdims=True)

    shapes = [jax.ShapeDtypeStruct((k, n), BF16)] + ([jax.ShapeDtypeStruct((1, k), F32)] if column_sums else [])
    specs = [pl.BlockSpec((tk, n), lambda j: (j, 0))] + ([pl.BlockSpec((1, tk), lambda j: (0, j))] if column_sums else [])
    return _unpack(_pcall(body, name=name, grid=(k // tk,),
                          in_specs=[pl.BlockSpec((t, tk), lambda j: (0, j)), _whole(dy)],
                          out_specs=specs, out_shape=shapes,
                          semantics=("parallel",), moves=moves)(x, dy), moves, len(shapes))


def _dx(dy, w, *, name, partial=None, moves=()):
    t, k = dy.shape
    n = w.shape[0]
    has_partial = partial is not None

    def body(dy_ref, w_ref, *rest):
        out = lax.dot_general(dy_ref[...].astype(BF16), w_ref[...], _NT, preferred_element_type=F32)
        if has_partial:
            out = out + rest[0][...]
        rest[-1][...] = out

    return _unpack(_pcall(body, name=name, grid=(t // TM,),
                          in_specs=[_rows(k), _whole(w)] + ([_rows(n)] if has_partial else []),
                          out_specs=[_rows(n)], out_shape=[jax.ShapeDtypeStruct((t, n), F32)],
                          semantics=("parallel",), moves=moves)(dy, w, *([partial] if has_partial else [])),
                   moves, 1)


def _dx_norm(dy, w, h, g, dres, *, name, partial=None, by_device_cols=False, w_transposed=False, moves=()):
    t, k = dy.shape
    d = h.shape[1]
    has_partial = partial is not None

    def body(dy_ref, w_ref, h_ref, g_ref, dres_ref, *rest):
        if by_device_cols:
            kc = k // N_DEV
            d_y = jnp.zeros((TM, d), F32)
            for j in range(N_DEV):
                d_y = d_y + lax.dot_general(dy_ref[:, j * kc:(j + 1) * kc].astype(BF16), w_ref[j], _NT,
                                            preferred_element_type=F32)
        else:
            d_y = lax.dot_general(dy_ref[...].astype(BF16), w_ref[...], _NN if w_transposed else _NT,
                                  preferred_element_type=F32)
        if has_partial:
            d_y = d_y + rest[0][...]
        dh_ref, dhb_ref, dg_ref, cs_ref = rest[-4:]
        _, vjp = jax.vjp(_rmsnorm, h_ref[...], g_ref[...])
        dh, dg = vjp(d_y)
        dh = dh + dres_ref[...]
        dh_ref[...] = dh
        dhb_ref[...] = dh.astype(BF16)

        @pl.when(pl.program_id(0) == 0)
        def _():
            dg_ref[...] = jnp.zeros_like(dg_ref)
            cs_ref[...] = jnp.zeros_like(cs_ref)

        dg_ref[...] += dg
        cs_ref[...] += jnp.sum(dh, axis=0, keepdims=True)

    shapes = [jax.ShapeDtypeStruct((t, d), F32), jax.ShapeDtypeStruct((t, d), BF16),
              jax.ShapeDtypeStruct((1, d), F32), jax.ShapeDtypeStruct((1, d), F32)]
    return _unpack(_pcall(body, name=name, grid=(t // TM,),
                          in_specs=[_rows(k), _whole(w), _rows(d), _acc_row(d), _rows(d)]
                          + ([_rows(d)] if has_partial else []),
                          out_specs=[_rows(d), _rows(d), _acc_row(d), _acc_row(d)], out_shape=shapes,
                          semantics=("arbitrary",), moves=moves)(dy, w, h, g, dres, *([partial] if has_partial else [])),
                   moves, 4)


def _pair_add(by_core, theirs, core, *, name, tb=512):
    n_chip, _, r, c = by_core.shape
    tb = min(tb, r)
    assert r % tb == 0, (name, r, tb)

    def body(core_ref, a_ref, b_ref, o_ref):
        del core_ref
        o_ref[...] = (a_ref[...].astype(F32) + b_ref[...].astype(F32)).astype(o_ref.dtype)

    blk = pl.BlockSpec((None, tb, c), lambda ch, i, core_ref: (ch, i, 0))
    return pl.pallas_call(
        body, name=name,
        grid_spec=pltpu.PrefetchScalarGridSpec(
            num_scalar_prefetch=1, grid=(n_chip, r // tb),
            in_specs=[pl.BlockSpec((None, None, tb, c), lambda ch, i, core_ref: (ch, core_ref[0], i, 0)), blk],
            out_specs=blk),
        out_shape=jax.ShapeDtypeStruct((n_chip, r, c), by_core.dtype),
        compiler_params=pltpu.CompilerParams(dimension_semantics=("parallel", "parallel"),
                                             vmem_limit_bytes=VMEM_LIMIT_BYTES),
    )(core, by_core, theirs)


def _mlp_down_loss(up, w_rows, res, g, target, *, name):
    t, d = res.shape
    n_w = len(w_rows)

    def body(up_ref, *rest):
        w_refs, (res_ref, g_ref, tgt_ref, loss_ref, dh_ref, dhb_ref, dg_ref) = rest[:n_w], rest[n_w:]
        h = res_ref[...]
        for off, rows, w_blk in _down_blocks(w_refs):
            act = _sq_relu(up_ref[:, off:off + rows]).astype(BF16)
            h = h + lax.dot_general(act, w_blk, _NN, preferred_element_type=F32)

        def f(hh, gg):
            err = jnp.square(_rmsnorm(hh, gg) - tgt_ref[...])
            return 0.5 * jnp.sum(jnp.mean(err, axis=-1, keepdims=True), axis=0, keepdims=True)

        val, vjp = jax.vjp(f, h, g_ref[...])
        dh, dg = vjp(jnp.ones((1, 1), F32))
        dh_ref[...] = dh
        dhb_ref[...] = dh.astype(BF16)

        @pl.when(pl.program_id(0) == 0)
        def _():
            loss_ref[...] = jnp.zeros_like(loss_ref)
            dg_ref[...] = jnp.zeros_like(dg_ref)

        loss_ref[...] += val
        dg_ref[...] += dg

    return _pcall(
        body, name=name, grid=(t // TM,),
        in_specs=[_rows(D_FF)] + [_whole(w) for w in w_rows] + [_rows(d), _acc_row(d), _rows(d)],
        out_specs=[pl.BlockSpec((8, LANES), lambda i: (0, 0)), _rows(d), _rows(d), _acc_row(d)],
        out_shape=[jax.ShapeDtypeStruct((8, LANES), F32), jax.ShapeDtypeStruct((t, d), F32),
                   jax.ShapeDtypeStruct((t, d), BF16), jax.ShapeDtypeStruct((1, d), F32)],
        semantics=("arbitrary",),
    )(up, *w_rows, res, g, target)[0]


def _gmlp_fwd(proj_uv, ln_g, ln_b, w_s, b_s, *, name, moves=()):
    t = proj_uv.shape[0]
    w = D_MODEL

    def body(u_ref, v_ref, g_ref, b_ref, w_ref, bs_ref, o_ref):
        o_ref[...] = _gmlp_chunk(u_ref[...], v_ref[...], g_ref[...], b_ref[...], w_ref[...],
                                 bs_ref[...]).astype(o_ref.dtype)

    row = pl.BlockSpec((1, w), lambda i: (0, 0))
    res, landed = _pcall(
        body, name=name, grid=(t // CHUNK,),
        in_specs=[pl.BlockSpec((CHUNK, w), lambda i: (i, 0)), pl.BlockSpec((CHUNK, w), lambda i: (i, 1)), row, row,
                  pl.BlockSpec((GM_GROUPS, CHUNK, CHUNK), lambda i: (0, 0, 0)),
                  pl.BlockSpec((GM_GROUPS, CHUNK, 1), lambda i: (0, 0, 0))],
        out_specs=[pl.BlockSpec((CHUNK, w), lambda i: (i, 0))],
        out_shape=[jax.ShapeDtypeStruct((t, 2 * w), BF16)],
        semantics=("parallel",), moves=moves,
    )(proj_uv, proj_uv, ln_g, ln_b, w_s, b_s)
    return (res[0], landed) if moves else res[0]


def _gmlp_bwd(proj_uv, d_mix, ln_g, ln_b, w_s, b_s, *, name, moves=()):
    t = proj_uv.shape[0]
    w = D_MODEL

    def body(u_ref, v_ref, da_ref, g_ref, b_ref, w_ref, bs_ref, duv_ref, dg_ref, db_ref, dw_ref, dbs_ref):
        _, vjp = jax.vjp(_gmlp_chunk, u_ref[...], v_ref[...], g_ref[...], b_ref[...], w_ref[...], bs_ref[...])
        du, dv, dg, db, dw, dbs = vjp(da_ref[...])
        duv_ref[:, :w] = du.astype(duv_ref.dtype)
        duv_ref[:, w:] = dv.astype(duv_ref.dtype)

        @pl.when(pl.program_id(0) == 0)
        def _():
            dg_ref[...] = jnp.zeros_like(dg_ref)
            db_ref[...] = jnp.zeros_like(db_ref)
            dw_ref[...] = jnp.zeros_like(dw_ref)
            dbs_ref[...] = jnp.zeros_like(dbs_ref)

        dg_ref[...] += dg
        db_ref[...] += db
        dw_ref[...] += dw
        dbs_ref[...] += dbs

    row = pl.BlockSpec((1, w), lambda i: (0, 0))
    ws = pl.BlockSpec((GM_GROUPS, CHUNK, CHUNK), lambda i: (0, 0, 0))
    bs = pl.BlockSpec((GM_GROUPS, CHUNK, 1), lambda i: (0, 0, 0))
    res, landed = _pcall(
        body, name=name, grid=(t // CHUNK,),
        in_specs=[pl.BlockSpec((CHUNK, w), lambda i: (i, 0)), pl.BlockSpec((CHUNK, w), lambda i: (i, 1)),
                  pl.BlockSpec((CHUNK, w), lambda i: (i, 0)), row, row, ws, bs],
        out_specs=[pl.BlockSpec((CHUNK, 2 * w), lambda i: (i, 0)), row, row, ws, bs],
        out_shape=[jax.ShapeDtypeStruct((t, 2 * w), BF16), jax.ShapeDtypeStruct((1, w), F32),
                   jax.ShapeDtypeStruct((1, w), F32), jax.ShapeDtypeStruct((GM_GROUPS, CHUNK, CHUNK), F32),
                   jax.ShapeDtypeStruct((GM_GROUPS, CHUNK, 1), F32)],
        semantics=("arbitrary",), moves=moves,
    )(proj_uv, proj_uv, d_mix, ln_g, ln_b, w_s, b_s)
    return (res, landed) if moves else res


_HALO_PER_CHUNK = CHUNK // HALO
_DT_BLOCK = (CONV_DIM + D_MODEL) // LANES


def _ssd_fwd(proj_rest, mix, conv_w, conv_b, dt_bias, a_log, d_skip, norm_g, *, name, moves=()):
    t = proj_rest.shape[0]
    nc = t // CHUNK

    def body(x_ref, prev_ref, z_ref, dt_ref, mix_ref, cw_ref, cb_ref, dtb_ref, al_ref, ds_ref, ng_ref, y_ref, hs_ref,
             pre_ref, h_scr):
        del mix_ref
        i = pl.program_id(0)

        @pl.when(i == 0)
        def _():
            h_scr[...] = jnp.zeros_like(h_scr)

        prev8 = jnp.where(i == 0, 0.0, prev_ref[...])
        pre = _conv_pre(prev8, x_ref[...], cw_ref[...], cb_ref[...])
        pre_ref[...] = pre
        hs_ref[0] = h_scr[...]
        h_prev = tuple(h_scr[j] for j in range(_PAIRS))
        y, h_next = _ssd_chunk(pre, z_ref[...], dt_ref[...], h_prev, dtb_ref[...], al_ref[...], ds_ref[...],
                               ng_ref[...])
        y_ref[...] = y.astype(y_ref.dtype)
        for j in range(_PAIRS):
            h_scr[j] = h_next[j]

    small = pl.BlockSpec((1, LANES), lambda i: (0, 0))
    res, landed = _pcall(
        body, name=name, grid=(nc,),
        in_specs=[pl.BlockSpec((CHUNK, CONV_DIM), lambda i: (i, 0)),
                  pl.BlockSpec((HALO, CONV_DIM), lambda i: (jnp.maximum(i * _HALO_PER_CHUNK - 1, 0), 0)),
                  pl.BlockSpec((CHUNK, D_MODEL), lambda i: (i, CONV_DIM // D_MODEL)),
                  pl.BlockSpec((CHUNK, LANES), lambda i: (i, _DT_BLOCK)),
                  pl.BlockSpec(memory_space=pl.ANY),
                  pl.BlockSpec((SSM_CONV, CONV_DIM), lambda i: (0, 0)),
                  pl.BlockSpec((1, CONV_DIM), lambda i: (0, 0)),
                  small, small, small, pl.BlockSpec((1, D_MODEL), lambda i: (0, 0))],
        out_specs=[pl.BlockSpec((CHUNK, D_MODEL), lambda i: (i, 1)),
                   pl.BlockSpec((1, _PAIRS, SSM_STATE, LANES), lambda i: (i, 0, 0, 0)),
                   pl.BlockSpec((CHUNK, CONV_DIM), lambda i: (i, 0))],
        out_shape=[jax.ShapeDtypeStruct((t, 2 * D_MODEL), BF16),
                   jax.ShapeDtypeStruct((nc, _PAIRS, SSM_STATE, LANES), F32),
                   jax.ShapeDtypeStruct((t, CONV_DIM), F32)],
        scratch_shapes=[pltpu.VMEM((_PAIRS, SSM_STATE, LANES), F32)],
        semantics=("arbitrary",), moves=moves, aliases={4: 0},
    )(proj_rest, proj_rest, proj_rest, proj_rest, mix, conv_w, conv_b, dt_bias, a_log, d_skip, norm_g)
    return (res, landed) if moves else res


def _ssd_bwd(proj_rest, pre, h_states, d_mix, dt_bias, a_log, d_skip, norm_g, *, name, moves=()):
    t = proj_rest.shape[0]
    nc = t // CHUNK

    def body(pre_ref, z_ref, dt_ref, hs_ref, dy_ref, dtb_ref, al_ref, ds_ref, ng_ref,
             dpre_ref, dz_ref, ddt_ref, ddtb_ref, dal_ref, dds_ref, dng_ref, dh_scr):
        i = pl.program_id(0)

        @pl.when(i == 0)
        def _():
            dh_scr[...] = jnp.zeros_like(dh_scr)
            ddtb_ref[...] = jnp.zeros_like(ddtb_ref)
            dal_ref[...] = jnp.zeros_like(dal_ref)
            dds_ref[...] = jnp.zeros_like(dds_ref)
            dng_ref[...] = jnp.zeros_like(dng_ref)

        h_prev = tuple(hs_ref[0, j] for j in range(_PAIRS))
        _, vjp = jax.vjp(_ssd_chunk, pre_ref[...], z_ref[...], dt_ref[...], h_prev, dtb_ref[...], al_ref[...],
                         ds_ref[...], ng_ref[...])
        dpre, dz, ddt, dh_prev, ddtb, dal, dds, dng = vjp((dy_ref[...], tuple(dh_scr[j] for j in range(_PAIRS))))
        dpre_ref[...] = dpre
        dz_ref[...] = dz.astype(dz_ref.dtype)
        ddt_ref[...] = ddt.astype(ddt_ref.dtype)
        for j in range(_PAIRS):
            dh_scr[j] = dh_prev[j]
        ddtb_ref[...] += ddtb
        dal_ref[...] += dal
        dds_ref[...] += dds
        dng_ref[...] += dng

    rev = lambda i: nc - 1 - i
    small = pl.BlockSpec((1, LANES), lambda i: (0, 0))
    wide = pl.BlockSpec((1, D_MODEL), lambda i: (0, 0))
    res, landed = _pcall(
        body, name=name, grid=(nc,),
        in_specs=[pl.BlockSpec((CHUNK, CONV_DIM), lambda i: (rev(i), 0)),
                  pl.BlockSpec((CHUNK, D_MODEL), lambda i: (rev(i), CONV_DIM // D_MODEL)),
                  pl.BlockSpec((CHUNK, LANES), lambda i: (rev(i), _DT_BLOCK)),
                  pl.BlockSpec((1, _PAIRS, SSM_STATE, LANES), lambda i: (rev(i), 0, 0, 0)),
                  pl.BlockSpec((CHUNK, D_MODEL), lambda i: (rev(i), 1)),
                  small, small, small, wide],
        out_specs=[pl.BlockSpec((CHUNK, CONV_DIM), lambda i: (rev(i), 0)),
                   pl.BlockSpec((CHUNK, D_MODEL), lambda i: (rev(i), 0)),
                   pl.BlockSpec((CHUNK, LANES), lambda i: (rev(i), 0)),
                   small, small, small, wide],
        out_shape=[jax.ShapeDtypeStruct((t, CONV_DIM), F32), jax.ShapeDtypeStruct((t, D_MODEL), BF16),
                   jax.ShapeDtypeStruct((t, LANES), BF16),
                   jax.ShapeDtypeStruct((1, LANES), F32), jax.ShapeDtypeStruct((1, LANES), F32),
                   jax.ShapeDtypeStruct((1, LANES), F32), jax.ShapeDtypeStruct((1, D_MODEL), F32)],
        scratch_shapes=[pltpu.VMEM((_PAIRS, SSM_STATE, LANES), F32)],
        semantics=("arbitrary",), moves=moves,
    )(pre, proj_rest, proj_rest, h_states, d_mix, dt_bias, a_log, d_skip, norm_g)
    return (res, landed) if moves else res


def _conv_bwd(proj_rest, dpre, dz, ddt, conv_w, *, name, tb=256, moves=()):
    t = proj_rest.shape[0]
    nb = t // tb
    per = tb // HALO

    def body(x_ref, prev_ref, dpre_ref, next_ref, dz_ref, ddt_ref, cw_ref, drest_ref, dcw_ref, dcb_ref):
        i = pl.program_id(0)

        @pl.when(i == 0)
        def _():
            dcw_ref[...] = jnp.zeros_like(dcw_ref)
            dcb_ref[...] = jnp.zeros_like(dcb_ref)

        x = x_ref[...]
        dp = dpre_ref[...]
        w = cw_ref[...]
        prev8 = jnp.where(i == 0, 0.0, prev_ref[...])
        next8 = jnp.where(i == nb - 1, 0.0, next_ref[...])
        dx = dp * w[SSM_CONV - 1:SSM_CONV]
        for j in range(SSM_CONV - 1):
            dx = dx + _shift_up(dp, next8, SSM_CONV - 1 - j) * w[j:j + 1]
        drest_ref[:, :CONV_DIM] = dx.astype(drest_ref.dtype)
        drest_ref[:, CONV_DIM:CONV_DIM + D_MODEL] = dz_ref[...].astype(drest_ref.dtype)
        drest_ref[:, CONV_DIM + D_MODEL:] = ddt_ref[...].astype(drest_ref.dtype)
        for j in range(SSM_CONV):
            dcw_ref[j:j + 1, :] += jnp.sum(dp * _shift_down(prev8, x, SSM_CONV - 1 - j), axis=0, keepdims=True)
        dcb_ref[...] += jnp.sum(dp, axis=0, keepdims=True)

    res, landed = _pcall(
        body, name=name, grid=(nb,),
        in_specs=[pl.BlockSpec((tb, CONV_DIM), lambda i: (i, 0)),
                  pl.BlockSpec((HALO, CONV_DIM), lambda i: (jnp.maximum(i * per - 1, 0), 0)),
                  pl.BlockSpec((tb, CONV_DIM), lambda i: (i, 0)),
                  pl.BlockSpec((HALO, CONV_DIM), lambda i: (jnp.minimum((i + 1) * per, nb * per - 1), 0)),
                  pl.BlockSpec((tb, D_MODEL), lambda i: (i, 0)),
                  pl.BlockSpec((tb, LANES), lambda i: (i, 0)),
                  pl.BlockSpec((SSM_CONV, CONV_DIM), lambda i: (0, 0))],
        out_specs=[pl.BlockSpec((tb, REST_W), lambda i: (i, 0)),
                   pl.BlockSpec((SSM_CONV, CONV_DIM), lambda i: (0, 0)),
                   pl.BlockSpec((1, CONV_DIM), lambda i: (0, 0))],
        out_shape=[jax.ShapeDtypeStruct((t, REST_W), BF16), jax.ShapeDtypeStruct((SSM_CONV, CONV_DIM), F32),
                   jax.ShapeDtypeStruct((1, CONV_DIM), F32)],
        semantics=("arbitrary",), moves=moves,
    )(proj_rest, proj_rest, dpre, dpre, dz, ddt, conv_w)
    return (res, landed) if moves else res


_KV_BLOCK = D_MODEL // (2 * LANES)
_SINK_ROWS = _PAIRS_PER_KV * CHUNK


def _stack_pairs(ref, kv_head):
    base = kv_head * _PAIRS_PER_KV
    return jnp.concatenate([ref[:, (base + p) * LANES:(base + p + 1) * LANES] for p in range(_PAIRS_PER_KV)], axis=0)


def _attn_fwd(qkv, sinks, *, name, moves=()):
    t = qkv.shape[0]
    nb = t // CHUNK

    def body(q_ref, kvp_ref, kvc_ref, s_ref, o_ref, p_ref, st_ref):
        valid = _band_mask(pl.program_id(0) == 0)
        kv = jnp.concatenate([kvp_ref[...], kvc_ref[...]], axis=0)
        stats = jnp.zeros((_SINK_ROWS, LANES), F32)
        passes = [(j, e) for j in range(ATTN_KV) for e in range(2)]
        vs = [_kv_placed(kv[:, LANES:], j) for j in range(ATTN_KV)]
        scores = []
        for j in range(ATTN_KV):
            q4 = _stack_pairs(q_ref, j) * _ATTN_SCALE
            ks = _kv_placed(kv[:, :LANES], j)
            scores += [jnp.where(valid, _dg(q4, ks[e], _NT), -jnp.inf) for e in range(2)]
        probs = [_attn_probs(s, s_ref[j, e]) for s, (j, e) in zip(scores, passes)]
        outs = [None] * ATTN_KV
        for (p, e_sink, den), (j, e) in zip(probs, passes):
            inv = 1.0 / den
            o = _dg(p, vs[j][e], _NN) * inv
            outs[j] = o if outs[j] is None else outs[j] + o
            p_ref[0, 2 * j + e] = p.astype(p_ref.dtype)
            stats = stats + _lane_column(inv, 2 * j + e) + _lane_column(e_sink, 4 + 2 * j + e)
        for j in range(ATTN_KV):
            for pair in range(_PAIRS_PER_KV):
                col = (j * _PAIRS_PER_KV + pair) * LANES
                o_ref[:, col:col + LANES] = outs[j][pair * CHUNK:(pair + 1) * CHUNK].astype(o_ref.dtype)
        st_ref[0] = stats

    return _unpack(_pcall(
        body, name=name, grid=(nb,),
        in_specs=[pl.BlockSpec((CHUNK, D_MODEL), lambda i: (i, 0)),
                  pl.BlockSpec((CHUNK, 2 * LANES), lambda i: (jnp.maximum(i - 1, 0), _KV_BLOCK)),
                  pl.BlockSpec((CHUNK, 2 * LANES), lambda i: (i, _KV_BLOCK)),
                  pl.BlockSpec((ATTN_KV, 2, _SINK_ROWS, 1), lambda i: (0, 0, 0, 0))],
        out_specs=[pl.BlockSpec((CHUNK, D_MODEL), lambda i: (i, 0)),
                   pl.BlockSpec((1, 2 * ATTN_KV, _SINK_ROWS, 2 * CHUNK), lambda i: (i, 0, 0, 0)),
                   pl.BlockSpec((1, _SINK_ROWS, LANES), lambda i: (i, 0, 0))],
        out_shape=[jax.ShapeDtypeStruct((t, D_MODEL), BF16),
                   jax.ShapeDtypeStruct((nb, 2 * ATTN_KV, _SINK_ROWS, 2 * CHUNK), BF16),
                   jax.ShapeDtypeStruct((nb, _SINK_ROWS, LANES), F32)],
        semantics=("parallel",), moves=moves,
    )(qkv, qkv, qkv, sinks), moves, 3)


def _attn_bwd(qkv, probs, stats, attn, d_o, *, name, moves=()):
    t = qkv.shape[0]
    nb = t // CHUNK

    def body(q_ref, kvp_ref, kvc_ref, p_ref, st_ref, o_ref, do_ref, dqkv_ref, ds_ref, dkv_scr):
        @pl.when(pl.program_id(0) == 0)
        def _():
            dkv_scr[...] = jnp.zeros_like(dkv_scr)
            ds_ref[...] = jnp.zeros_like(ds_ref)

        kv = jnp.concatenate([kvp_ref[...], kvc_ref[...]], axis=0)
        table = st_ref[0]
        d_k = jnp.zeros((2 * CHUNK, LANES), F32)
        d_v = jnp.zeros((2 * CHUNK, LANES), F32)
        passes = [(j, e) for j in range(ATTN_KV) for e in range(2)]
        q4s = [_stack_pairs(q_ref, j) for j in range(ATTN_KV)]
        ks = [_kv_placed(kv[:, :LANES], j) for j in range(ATTN_KV)]
        vs = [_kv_placed(kv[:, LANES:], j) for j in range(ATTN_KV)]
        d_nums, d_dens = [], []
        for j in range(ATTN_KV):
            do4, o4 = _stack_pairs(do_ref, j), _stack_pairs(o_ref, j).astype(F32)
            for e in range(2):
                inv, e_sink = _col_pick(table, 2 * j + e), _col_pick(table, 4 + 2 * j + e)
                do_e = jnp.where(_parity_lanes(e), do4, 0.0)
                d_nums.append(do_e * inv)
                d_dens.append(-jnp.sum(do_e * o4, axis=1, keepdims=True) * inv)
                ds_ref[j, e] += d_dens[-1] * e_sink
        d_ps = [_dg(d_num, vs[j][e], _NT) for d_num, (j, e) in zip(d_nums, passes)]
        ps = [p_ref[0, 2 * j + e].astype(F32) for j, e in passes]
        d_ss = [p * (d_p + d_den) for p, d_p, d_den in zip(ps, d_ps, d_dens)]
        for j in range(ATTN_KV):
            ds_lo, ds_hi = d_ss[2 * j], d_ss[2 * j + 1]
            dq4 = (_dg(ds_lo, ks[j][0], _NN) + _dg(ds_hi, ks[j][1], _NN)) * _ATTN_SCALE
            for pair in range(_PAIRS_PER_KV):
                col = (j * _PAIRS_PER_KV + pair) * LANES
                dqkv_ref[:, col:col + LANES] = dq4[pair * CHUNK:(pair + 1) * CHUNK]
            dk = [_dg(ds, q4s[j], _TN) * _ATTN_SCALE for ds in (ds_lo, ds_hi)]
            dv = [_dg(ps[2 * j + e], d_nums[2 * j + e], _TN) for e in range(2)]
            d_k = d_k + _kv_unplaced(dk[0], dk[1], j)
            d_v = d_v + _kv_unplaced(dv[0], dv[1], j)
        d_kv = jnp.concatenate([d_k, d_v], axis=1)
        dqkv_ref[:, D_MODEL:] = d_kv[CHUNK:] + dkv_scr[...]
        dkv_scr[...] = d_kv[:CHUNK]

    cur = lambda i: (nb - 1 - i, 0)
    sk = pl.BlockSpec((ATTN_KV, 2, _SINK_ROWS, 1), lambda i: (0, 0, 0, 0))
    res, landed = _pcall(
        body, name=name, grid=(nb,),
        in_specs=[pl.BlockSpec((CHUNK, D_MODEL), cur),
                  pl.BlockSpec((CHUNK, 2 * LANES), lambda i: (jnp.maximum(nb - 2 - i, 0), _KV_BLOCK)),
                  pl.BlockSpec((CHUNK, 2 * LANES), lambda i: (nb - 1 - i, _KV_BLOCK)),
                  pl.BlockSpec((1, 2 * ATTN_KV, _SINK_ROWS, 2 * CHUNK), lambda i: (nb - 1 - i, 0, 0, 0)),
                  pl.BlockSpec((1, _SINK_ROWS, LANES), lambda i: (nb - 1 - i, 0, 0)),
                  pl.BlockSpec((CHUNK, D_MODEL), cur), pl.BlockSpec((CHUNK, D_MODEL), cur)],
        out_specs=[pl.BlockSpec((CHUNK, QKV_DIM), cur), sk],
        out_shape=[jax.ShapeDtypeStruct((t, QKV_DIM), F32), jax.ShapeDtypeStruct((ATTN_KV, 2, _SINK_ROWS, 1), F32)],
        scratch_shapes=[pltpu.VMEM((CHUNK, 2 * LANES), F32)],
        semantics=("arbitrary",), moves=moves,
    )(qkv, qkv, qkv, probs, stats, attn, d_o)
    return (res, landed) if moves else res


def _adamw(parts, w, m, v, *, name, tb=512, moves=()):
    layers, r, c = w.shape
    n = parts[0].shape[0]
    tb = min(tb, r)
    assert r % tb == 0 and len(parts) == layers, (name, r, tb)
    nb = r // tb

    def body(*refs):
        p_refs = refs[:layers]
        w_ref, m_ref, v_ref, g_ref, d_ref, nm_ref, nv_ref = refs[layers:]
        for layer in range(layers):
            @pl.when(pl.program_id(0) == layer)
            def _(p_ref=p_refs[layer]):
                g = p_ref[0].astype(F32)
                for s in range(1, n):
                    g = g + p_ref[s].astype(F32)
                m_new = ADAM_B1 * m_ref[...] + (1.0 - ADAM_B1) * g
                v_new = ADAM_B2 * v_ref[...] + (1.0 - ADAM_B2) * jnp.square(g)
                m_hat = m_new / (1.0 - ADAM_B1 ** ADAM_STEP)
                v_hat = v_new / (1.0 - ADAM_B2 ** ADAM_STEP)
                g_ref[...] = g
                d_ref[...] = -ADAM_LR * (m_hat / (jnp.sqrt(v_hat) + ADAM_EPS) + ADAM_WD * w_ref[...])
                nm_ref[...] = m_new
                nv_ref[...] = v_new

    part_spec = lambda layer: pl.BlockSpec(
        (n, tb, c), lambda l, i: (0, jnp.clip(i + (l - layer) * nb, 0, nb - 1), 0))
    blk = pl.BlockSpec((None, tb, c), lambda l, i: (l, i, 0))
    res, landed = _pcall(
        body, name=name, grid=(layers, nb),
        in_specs=[part_spec(layer) for layer in range(layers)] + [blk, blk, blk],
        out_specs=[blk] * 4,
        out_shape=[jax.ShapeDtypeStruct((layers, r, c), F32)] * 4,
        semantics=("arbitrary", "arbitrary"), moves=moves,
    )(*parts, w, m, v)
    return (res, landed) if moves else res


def _as_rows(a):
    flat = a.reshape(-1)
    pad = (-flat.shape[0]) % PACK_W
    if pad:
        flat = jnp.pad(flat, (0, pad))
    return flat.reshape(-1, PACK_W)


def _cols_from_shards(g):
    return jnp.transpose(g, (1, 0, 2)).reshape(g.shape[1], -1)


def _shard_cols(shards, lo, hi):
    c = shards.shape[2]
    pieces = []
    for j in range(shards.shape[0]):
        a, b = max(lo, j * c), min(hi, (j + 1) * c)
        if a < b:
            pieces.append(shards[j, :, a - j * c:b - j * c])
    return pieces


def _cols_of(sources, lo, hi):
    pieces = []
    for arr, col0, first, last in sources:
        a, b = max(lo, first), min(hi, last)
        if a < b:
            pieces.append(arr[:, col0 + a - first:col0 + b - first])
    return pieces


def _pad_lanes(a):
    return jnp.pad(a, ((0, 0), (0, LANES - a.shape[1])))


def kernel(x, norm_mix_g, norm_mlp_g, final_norm_g, w_in_even, w_out_even, gm_ln_g, gm_ln_b, gm_w_s, gm_b_s, ssm_conv_w, ssm_conv_b, ssm_dt_bias, ssm_a_log, ssm_d, ssm_norm_g, w_qkv, b_qkv, w_o, b_o, attn_sinks, w_up, w_down, loss_target, m_norm_mix_g, m_norm_mlp_g, m_final_norm_g, m_w_in_even, m_w_out_even, m_gm_ln_g, m_gm_ln_b, m_gm_w_s, m_gm_b_s, m_ssm_conv_w, m_ssm_conv_b, m_ssm_dt_bias, m_ssm_a_log, m_ssm_d, m_ssm_norm_g, m_w_qkv, m_b_qkv, m_w_o, m_b_o, m_attn_sinks, m_w_up, m_w_down, v_norm_mix_g, v_norm_mlp_g, v_final_norm_g, v_w_in_even, v_w_out_even, v_gm_ln_g, v_gm_ln_b, v_gm_w_s, v_gm_b_s, v_ssm_conv_w, v_ssm_conv_b, v_ssm_dt_bias, v_ssm_a_log, v_ssm_d, v_ssm_norm_g, v_w_qkv, v_b_qkv, v_w_o, v_b_o, v_attn_sinks, v_w_up, v_w_down):
    names = ["norm_mix_g", "norm_mlp_g", "final_norm_g", "w_in_even", "w_out_even", "gm_ln_g", "gm_ln_b", "gm_w_s",
             "gm_b_s", "ssm_conv_w", "ssm_conv_b", "ssm_dt_bias", "ssm_a_log", "ssm_d", "ssm_norm_g", "w_qkv",
             "b_qkv", "w_o", "b_o", "attn_sinks", "w_up", "w_down"]
    env = locals()
    W = {n: env[n] for n in names}
    M = {n: env["m_" + n] for n in names}
    V = {n: env["v_" + n] for n in names}
    big = ["w_in_even", "w_out_even", "w_qkv", "w_o", "w_up", "w_down"]
    small_sharded = ["ssm_conv_w", "b_qkv", "b_o"]
    replicated = [n for n in names if n not in big and n not in small_sharded]
    me = 4 * lax.axis_index("x") + 2 * lax.axis_index("y") + lax.axis_index("c")
    t = x.shape[1]
    xs = x.reshape(t, D_MODEL)
    target = loss_target.reshape(t, D_MODEL)
    gather = lambda a: _Move("gather", a)
    over_ici = lambda a: _Move("gather_ici", a)
    over_d2d = lambda a: _Move("gather_d2d", a)
    by_core = lambda a: a.reshape((N_CHIP, N_CORE) + a.shape[1:])
    to_sibling = lambda a: [_Move("scatter_d2d", by_core(a))]
    my_core = lax.axis_index("c").astype(jnp.int32).reshape(1)
    pair = lambda a, theirs, name: _pair_add(by_core(a), theirs, my_core, name=name)
    to_chips = lambda a: _Move("scatter_ici", a)
    whole = lambda a: a.reshape((N_DEV,) + a.shape[2:])
    row = lambda a: a.reshape(1, D_MODEL)

    small_flat = jnp.concatenate([W[n].reshape(-1) for n in small_sharded])
    w_in_g, small_g = _exchange([over_ici(w_in_even[0].astype(BF16)), gather(_as_rows(small_flat))],
                                name="gather_w_in", then_d2d=[0])
    w_in_s = whole(w_in_g)
    z_lo, xbc_lo, dt_lo = 2 * D_MODEL, 3 * D_MODEL, 3 * D_MODEL + CONV_DIM
    w_uv = jnp.concatenate(_shard_cols(w_in_s, 0, z_lo), axis=1)
    w_rest = jnp.concatenate(_shard_cols(w_in_s, xbc_lo, dt_lo) + _shard_cols(w_in_s, z_lo, xbc_lo)
                             + _shard_cols(w_in_s, dt_lo, IN_EVEN)
                             + [jnp.zeros((D_MODEL, LANES - SSM_HEADS), BF16)], axis=1)
    small_all = small_g.reshape(N_DEV, -1)
    n_cw = SSM_CONV * CONV_DIM // N_DEV
    n_bq = QKV_DIM // N_DEV
    conv_w = _cols_from_shards(small_all[:, :n_cw].reshape(N_DEV, SSM_CONV, CONV_DIM // N_DEV))
    bqkv = small_all[:, n_cw:n_cw + n_bq].reshape(1, QKV_DIM)
    bo = small_all[:, n_cw + n_bq:n_cw + n_bq + D_MODEL // N_DEV].reshape(1, D_MODEL)

    conv_b = ssm_conv_b.reshape(1, CONV_DIM)
    dt_bias, a_log, d_skip = _pad_lanes(ssm_dt_bias), _pad_lanes(ssm_a_log), _pad_lanes(ssm_d)
    gm_w = gm_w_s[0]
    gm_b = gm_b_s[0].reshape(GM_GROUPS, CHUNK, 1)
    sink_rows = jnp.repeat(jnp.transpose(attn_sinks.reshape(ATTN_KV, _PAIRS_PER_KV, 2), (0, 2, 1)), CHUNK,
                           axis=2).reshape(ATTN_KV, 2, _SINK_ROWS, 1)
    w_up_b, w_down_b = w_up.astype(BF16), w_down.astype(BF16)

    w_down0_a, w_down0_b = w_down_b[0, :FF_SHARD // 2], w_down_b[0, FF_SHARD // 2:]
    (y0, proj_uv), (w_qkv_g,) = _norm_matmul(xs, row(norm_mix_g[0]), w_uv, name="proj_uv", emit_y=True,
                                             moves=[over_ici(jnp.transpose(w_qkv[0]).astype(BF16))])
    proj_rest, (w_out_g,) = _norm_matmul(xs, row(norm_mix_g[0]), w_rest, name="proj_rest", emit_y=False,
                                         moves=[over_ici(w_out_even[0].astype(BF16))])
    mix, (w_down0_a, w_out_g, w_qkv_g) = _gmlp_fwd(
        proj_uv, gm_ln_g, gm_ln_b, gm_w, gm_b, name="gmlp_fwd",
        moves=[over_ici(w_down0_a), over_d2d(w_out_g), over_d2d(w_qkv_g)])
    (mix, h_states, conv_pre), (w_up0_g, w_down0_a) = _ssd_fwd(
        proj_rest, mix, conv_w, conv_b, dt_bias, a_log, d_skip, ssm_norm_g, name="ssd_fwd",
        moves=[over_ici(w_up_b[0]), over_d2d(w_down0_a)])
    w_out_f = whole(w_out_g).reshape(2 * D_MODEL, D_MODEL)
    (h1, y1), (w_down0_b, w_up0_g) = _residual_matmul(
        mix, w_out_f, xs, name="mix_out", norm_g=row(norm_mlp_g[0]),
        moves=[over_ici(w_down0_b), over_d2d(w_up0_g)])
    up0, (w_down0_b,) = _mlp_up(y1, whole(w_up0_g), name="mlp_up0", moves=[over_d2d(w_down0_b)])
    w_down_g = [[whole(w_down0_a), whole(w_down0_b)]]
    (h2, y2), (w_o_g,) = _mlp_down(up0, w_down_g[0], h1, name="mlp_down0", norm_g=row(norm_mix_g[1]),
                                   moves=[over_ici(w_o[0].astype(BF16))])
    wqkv = whole(w_qkv_g).reshape(QKV_DIM, D_MODEL)
    qkv, (w_o_g,) = _residual_matmul(y2, wqkv, None, name="qkv", bias=bqkv, w_transposed=True,
                                     moves=[over_d2d(w_o_g)])
    wo = whole(w_o_g).reshape(D_MODEL, D_MODEL)
    (attn, attn_p, attn_stats), (w_up1_g, w_down1_g) = _attn_fwd(
        qkv, sink_rows, name="attn_fwd", moves=[over_ici(w_up_b[1]), over_ici(w_down_b[1])])
    (h3, y3), (w_up1_g,) = _residual_matmul(attn, wo, h2, name="attn_out", bias=bo, norm_g=row(norm_mlp_g[1]),
                                            moves=[over_d2d(w_up1_g)])
    w_up_g = [whole(w_up0_g), whole(w_up1_g)]
    up1, (w_down1_g,) = _mlp_up(y3, w_up_g[1], name="mlp_up1", moves=[over_d2d(w_down1_g)])
    w_down_g.append([whole(w_down1_g)])
    loss_part, dh4, dh4_b, d_final_g = _mlp_down_loss(up1, w_down_g[1], h3, row(final_norm_g), target,
                                                      name="mlp_down1_loss")

    by_dev_rows = lambda a: a.reshape((N_DEV, a.shape[0] // N_DEV) + a.shape[1:])

    def mlp_bwd(dh, dh_b, h, y, up, layer, first_moves=()):
        res = _mlp_down_dx(dh_b, w_down_g[layer], up, name=f"mlp_down_dx{layer}", moves=first_moves)
        d_up, first_landed = res if first_moves else (res, [])
        g_down = _dw_by_rows(up, dh_b, name=f"mlp_down_dw{layer}", tk=FF_SHARD, square_relu=True)
        g_down = by_dev_rows(g_down)
        g_up, (theirs,) = _dw_by_cols(y, d_up, name=f"mlp_up_dw{layer}", tn=FF_SHARD, by_device=True,
                                      moves=to_sibling(g_down))
        q_down = pair(g_down, theirs, f"mlp_down_pair{layer}")
        (dh_new, dh_new_b, dg, cs), (theirs,) = _dx_norm(
            d_up, w_up_g[layer], h, row(norm_mlp_g[layer]), dh, name=f"mlp_up_dx{layer}", by_device_cols=True,
            moves=to_sibling(g_up))
        q_up = pair(g_up, theirs, f"mlp_up_pair{layer}")
        return dh_new, dh_new_b, cs, dg, q_up, q_down, first_landed

    dh3, dh3_b, cs3, g_nmlp1, q_up1, q_down1, _ = mlp_bwd(dh4, dh4_b, h3, y3, up1, 1)
    g_bo = cs3
    g_wo = by_dev_rows(_dw_by_cols(attn, dh3_b, name="attn_out_dw", tn=FF_SHARD))
    d_attn, (theirs,) = _dx(dh3_b, wo, name="attn_out_dx", moves=to_sibling(g_wo))
    q_wo = pair(g_wo, theirs, "attn_out_pair")
    (dqkv, d_sink), (r_down1, r_up1) = _attn_bwd(qkv, attn_p, attn_stats, attn, d_attn, name="attn_bwd",
                                                 moves=[to_chips(q_down1), to_chips(q_up1)])
    g_wqkv, g_bqkv = _dw_by_rows(dqkv, y2, name="qkv_dw", tk=QKV_DIM // 2, column_sums=True)
    g_wqkv = by_dev_rows(g_wqkv)
    (dh2, dh2_b, g_nmix1, _), (theirs,) = _dx_norm(dqkv, wqkv, h2, row(norm_mix_g[1]), dh3, name="qkv_dx",
                                                   w_transposed=True, moves=to_sibling(g_wqkv))
    q_wqkv = pair(g_wqkv, theirs, "qkv_pair")
    dh1, dh1_b, _, g_nmlp0, q_up0, q_down0, (r_wqkv, r_wo) = mlp_bwd(
        dh2, dh2_b, h1, y1, up0, 0, first_moves=[to_chips(q_wqkv), to_chips(q_wo)])

    d_mix = _dx(dh1_b, w_out_f, name="mix_out_dx")
    g_wout = by_dev_rows(_dw_by_rows(mix, dh1_b, name="mix_out_dw", tk=FF_SHARD))
    (d_uv, g_ln_g, g_ln_b, g_gm_w, g_gm_b), (r_down0, theirs) = _gmlp_bwd(
        proj_uv, d_mix, gm_ln_g, gm_ln_b, gm_w, gm_b, name="gmlp_bwd", moves=[to_chips(q_down0)] + to_sibling(g_wout))
    q_wout = pair(g_wout, theirs, "mix_out_pair")

    early = [("norm_mlp_g", None), ("final_norm_g", None), ("norm_mix_g", 1), ("gm_ln_g", None), ("gm_ln_b", None),
             ("gm_w_s", None), ("gm_b_s", None), ("attn_sinks", None)]
    late = [("norm_mix_g", 0), ("ssm_conv_b", None), ("ssm_dt_bias", None), ("ssm_a_log", None), ("ssm_d", None),
            ("ssm_norm_g", None)]
    early_sharded, late_sharded = ["b_qkv", "b_o"], ["ssm_conv_w"]
    small_grads = {
        ("norm_mlp_g", None): jnp.concatenate([g_nmlp0, g_nmlp1], axis=0),
        ("final_norm_g", None): d_final_g, ("norm_mix_g", 1): g_nmix1,
        ("gm_ln_g", None): g_ln_g, ("gm_ln_b", None): g_ln_b, ("gm_w_s", None): g_gm_w, ("gm_b_s", None): g_gm_b,
        ("attn_sinks", None): jnp.transpose(
            jnp.sum(d_sink.reshape(ATTN_KV, 2, _PAIRS_PER_KV, CHUNK), axis=3), (0, 2, 1)),
        "b_qkv": g_bqkv, "b_o": g_bo,
    }
    pack = lambda keys: _as_rows(jnp.concatenate([small_grads[key].reshape(-1) for key in keys]))
    (dpre, dz, ddt, g_dtb, g_alog, g_dskip, g_ssm_ng), (r_up0, r_wout, early_recv) = _ssd_bwd(
        proj_rest, conv_pre, h_states, d_mix, dt_bias, a_log, d_skip, ssm_norm_g, name="ssd_bwd",
        moves=[to_chips(q_up0), to_chips(q_wout), gather(pack(early + early_sharded))])
    d_rest, g_conv_w, g_conv_b = _conv_bwd(proj_rest, dpre, dz, ddt, conv_w, name="conv_bwd")
    g_w_uv = _dw_by_cols(y0, d_uv, name="proj_uv_dw", tn=FF_SHARD)
    g_w_rest = _dw_by_cols(y0, d_rest, name="proj_rest_dw", tn=REST_W // 5)
    in_cols = [(g_w_uv, 0, 0, z_lo), (g_w_rest, CONV_DIM, z_lo, xbc_lo), (g_w_rest, 0, xbc_lo, dt_lo),
               (g_w_rest, CONV_DIM + D_MODEL, dt_lo, IN_EVEN)]
    in_shard = IN_EVEN // N_DEV
    g_w_in = jnp.stack([jnp.concatenate(_cols_of(in_cols, j * in_shard, (j + 1) * in_shard), axis=1)
                        for j in range(N_DEV)])
    dy0, (theirs,) = _dx(d_uv, w_uv, name="proj_uv_dx", moves=to_sibling(g_w_in))
    q_w_in = pair(g_w_in, theirs, "proj_pair")
    (dx, _, g_nmix0, _), r_w_in = _dx_norm(d_rest, w_rest, xs, row(norm_mix_g[0]), dh1, name="proj_rest_dx",
                                           partial=dy0, moves=[to_chips(q_w_in)])
    small_grads.update({
        ("loss", None): loss_part[:1, :1],
        ("norm_mix_g", 0): g_nmix0, ("ssm_conv_b", None): g_conv_b,
        ("ssm_dt_bias", None): g_dtb[:, :SSM_HEADS], ("ssm_a_log", None): g_alog[:, :SSM_HEADS],
        ("ssm_d", None): g_dskip[:, :SSM_HEADS], ("ssm_norm_g", None): g_ssm_ng, "ssm_conv_w": g_conv_w,
    })


    def update(n, parts, moves=(), transposed=False):
        shape = W[n].shape
        if transposed:
            as3 = lambda a: jnp.transpose(a[0])[None]
            back = lambda a: jnp.transpose(a[0])[None]
        else:
            as3 = lambda a: a.reshape((len(parts),) + parts[0].shape[1:])
            back = lambda a: a.reshape(shape)
        res = _adamw(parts, as3(W[n]), as3(M[n]), as3(V[n]), name="adamw_" + n, moves=moves)
        res, landed = res if moves else (res, [])
        return [back(a) for a in res], landed

    out = {}
    late_keys = late + late_sharded + [("loss", None)]
    out["w_o"], (late_recv,) = update("w_o", [r_wo], moves=[gather(pack(late_keys))])
    out["w_down"], _ = update("w_down", [r_down0, r_down1])
    out["w_up"], _ = update("w_up", [r_up0, r_up1])
    out["w_out_even"], _ = update("w_out_even", [r_wout])
    out["w_qkv"], _ = update("w_qkv", [r_wqkv], transposed=True)
    out["w_in_even"], _ = update("w_in_even", list(r_w_in))

    def unpacked(recv, keys):
        flat, res, o = recv.reshape(N_DEV, -1), {}, 0
        for key in keys:
            res[key] = flat[:, o:o + small_grads[key].size]
            o += small_grads[key].size
        return res

    arrived = {**unpacked(early_recv, early + early_sharded), **unpacked(late_recv, late_keys)}
    piece = lambda tree, key: tree[key[0]] if key[1] is None else tree[key[0]][key[1]]

    def rows_by_device(cat):
        pad = (-cat.shape[1]) % PACK_W
        return jnp.pad(cat, ((0, 0), (0, pad))).reshape(N_DEV, -1, PACK_W)

    rep_keys = early + late
    rep_parts = rows_by_device(jnp.concatenate([arrived[key] for key in rep_keys], axis=1))
    flat_rep = lambda tree: _as_rows(jnp.concatenate([piece(tree, key).reshape(-1) for key in rep_keys]))[None]
    rep_res = _adamw([rep_parts], flat_rep(W), flat_rep(M), flat_rep(V), name="adamw_replicated")
    sh_keys = early_sharded + late_sharded
    shard_parts = []
    for n in sh_keys:
        full = arrived[n].reshape((N_DEV,) + small_grads[n].shape)
        c = full.shape[-1] // N_DEV
        shard_parts.append(lax.dynamic_slice_in_dim(full, me * c, c, axis=full.ndim - 1).reshape(N_DEV, -1))
    sh_rows = rows_by_device(jnp.concatenate(shard_parts, axis=1))
    flat_sh = lambda tree: _as_rows(jnp.concatenate([tree[n].reshape(-1) for n in sh_keys]))[None]
    sh_res = _adamw([sh_rows], flat_sh(W), flat_sh(M), flat_sh(V), name="adamw_small_sharded")

    def unpack_replicated(rows):
        flat, vals, o = rows.reshape(-1), {}, 0
        for key in rep_keys:
            size = piece(W, key).size
            vals[key] = flat[o:o + size]
            o += size
        res = {}
        for n in replicated:
            if (n, None) in vals:
                res[n] = vals[(n, None)].reshape(W[n].shape)
            else:
                res[n] = jnp.stack([vals[(n, r)] for r in range(W[n].shape[0])]).reshape(W[n].shape)
        return res

    def unpack_sharded(rows):
        flat, res, o = rows.reshape(-1), {}, 0
        for n in sh_keys:
            res[n] = flat[o:o + W[n].size].reshape(W[n].shape)
            o += W[n].size
        return res

    results = []
    for idx in range(4):
        d = {n: out[n][idx] for n in big}
        d.update(unpack_replicated(rep_res[idx]))
        d.update(unpack_sharded(sh_res[idx]))
        results.append(d)

    loss = jnp.sum(arrived[("loss", None)])
    grad_x = dx.reshape(x.shape)
    final = [loss, grad_x]
    for d in results:
        final.extend(d[n] for n in names)
    return tuple(final)
```

```python
import dataclasses
import functools

import jax
import jax.numpy as jnp
from jax import lax
from jax.experimental import pallas as pl
from jax.experimental.pallas import tpu as pltpu

F32 = jnp.float32
BF16 = jnp.bfloat16

N_DEV = 8
D_MODEL = 1024
D_FF = 4096
RMS_EPS = 1e-5
LN_EPS = 1e-5
CHUNK = 128
GM_GROUPS = 8
SSM_HEADS = 16
SSM_HEADDIM = 64
SSM_GROUPS = 4
SSM_STATE = 128
SSM_CONV = 4
CONV_DIM = 2048
IN_EVEN = 5136
REST_W = 3200
ATTN_HEADS = 16
ATTN_KV = 2
HEAD_DIM = 64
QKV_DIM = 1280
LANES = 128
HALO = 8
PACK_W = 1024

ADAM_LR = 0.001
ADAM_B1 = 0.9
ADAM_B2 = 0.999
ADAM_EPS = 1e-08
ADAM_WD = 0.01
ADAM_STEP = 10

VMEM_LIMIT_BYTES = 56 * 1024 * 1024


_NN = (((1,), (0,)), ((), ()))
_NT = (((1,), (1,)), ((), ()))
_TN = (((0,), (0,)), ((), ()))


def _dg(a, b, dims):
    return lax.dot_general(a.astype(BF16), b.astype(BF16), dims, preferred_element_type=F32)


@jax.custom_vjp
def _nn(a, b):
    return _dg(a, b, _NN)


@jax.custom_vjp
def _nt(a, b):
    return _dg(a, b, _NT)


@jax.custom_vjp
def _tn(a, b):
    return _dg(a, b, _TN)


_nn.defvjp(lambda a, b: (_dg(a, b, _NN), (a, b)), lambda r, g: (_nt(g, r[1]), _tn(r[0], g)))
_nt.defvjp(lambda a, b: (_dg(a, b, _NT), (a, b)), lambda r, g: (_nn(g, r[1]), _tn(g, r[0])))
_tn.defvjp(lambda a, b: (_dg(a, b, _TN), (a, b)), lambda r, g: (_nt(r[1], g), _nn(r[0], g)))


def _split3_dot(tri, x):
    x1 = x.astype(BF16)
    r1 = x - x1.astype(F32)
    x2 = r1.astype(BF16)
    x3 = (r1 - x2.astype(F32)).astype(BF16)
    t = tri.astype(BF16)
    dot = lambda p: lax.dot_general(t, p, _NN, preferred_element_type=F32)
    return dot(x1) + dot(x2) + dot(x3)


def _tri(lower):
    r = lax.broadcasted_iota(jnp.int32, (CHUNK, CHUNK), 0)
    c = lax.broadcasted_iota(jnp.int32, (CHUNK, CHUNK), 1)
    return jnp.where((r >= c) if lower else (r <= c), 1.0, 0.0).astype(F32)


@jax.custom_vjp
def _cumsum_rows(x):
    return _split3_dot(_tri(True), x)


_cumsum_rows.defvjp(lambda x: (_split3_dot(_tri(True), x), None), lambda _, g: (_split3_dot(_tri(False), g),))


def _sigmoid(x):
    return 1.0 / (1.0 + jnp.exp(-x))


def _silu(x):
    return x * _sigmoid(x)


def _softplus(x):
    return jnp.maximum(x, 0.0) + jnp.log(1.0 + jnp.exp(-jnp.abs(x)))


def _gelu_tanh(x):
    return 0.5 * x * (1.0 + jnp.tanh(0.7978845608028654 * (x + 0.044715 * (x * x * x))))


def _rmsnorm(x, g):
    return x * lax.rsqrt(jnp.mean(x * x, axis=-1, keepdims=True) + RMS_EPS) * g


def _gmlp_chunk(u, v, ln_g, ln_b, w_s, b_s):
    gu = _gelu_tanh(u)
    gv = _gelu_tanh(v)
    mu = jnp.mean(gv, axis=-1, keepdims=True)
    var = jnp.mean(jnp.square(gv - mu), axis=-1, keepdims=True)
    vn = (gv - mu) * lax.rsqrt(var + LN_EPS) * ln_g + ln_b
    r = lax.broadcasted_iota(jnp.int32, (CHUNK, CHUNK), 0)
    c = lax.broadcasted_iota(jnp.int32, (CHUNK, CHUNK), 1)
    causal = r >= c
    outs = []
    for g in range(GM_GROUPS):
        cols = slice(g * LANES, (g + 1) * LANES)
        mixed = _nn(jnp.where(causal, w_s[g], 0.0), vn[:, cols]) + b_s[g]
        outs.append(gu[:, cols] * mixed)
    return jnp.concatenate(outs, axis=1)


def _lane_pick(row, h):
    lane = lax.broadcasted_iota(jnp.int32, row.shape, 1)
    return jnp.sum(jnp.where(lane == h, row, 0.0), axis=1, keepdims=True)


def _col_pick(m, h):
    lane = lax.broadcasted_iota(jnp.int32, m.shape, 1)
    return jnp.sum(jnp.where(lane == h, m, 0.0), axis=1, keepdims=True)


def _row_pick(m, h):
    sub = lax.broadcasted_iota(jnp.int32, m.shape, 0)
    return jnp.sum(jnp.where(sub == h, m, 0.0), axis=0, keepdims=True)


_PAIRS = SSM_HEADS // 2


def _ssd_chunk(pre, z, dt_raw, h_prev, dt_bias, a_log, d_skip, norm_g):
    xbc = _silu(pre)
    dt = _softplus(dt_raw + dt_bias)
    da = dt * (-jnp.exp(a_log))
    a_cum = _cumsum_rows(da)
    a_cum_t = a_cum.T
    dt_t = dt.T
    r = lax.broadcasted_iota(jnp.int32, (CHUNK, CHUNK), 0)
    c = lax.broadcasted_iota(jnp.int32, (CHUNK, CHUNK), 1)
    causal = r >= c
    lane_lo = lax.broadcasted_iota(jnp.int32, (1, LANES), 1) < SSM_HEADDIM
    last_row = lax.broadcasted_iota(jnp.int32, (CHUNK, 1), 0) == CHUNK - 1
    bms = [xbc[:, 1024 + g * SSM_STATE:1024 + (g + 1) * SSM_STATE] for g in range(SSM_GROUPS)]
    cms = [xbc[:, 1536 + g * SSM_STATE:1536 + (g + 1) * SSM_STATE] for g in range(SSM_GROUPS)]
    cbs = [_nt(cms[g], bms[g]) for g in range(SSM_GROUPS)]
    w_intra, to_end, e_cum, c_dec, d_row = [], [], [], [], []
    for h in range(SSM_HEADS):
        col = _col_pick(a_cum, h)
        row = _row_pick(a_cum_t, h)
        dt_col = _col_pick(dt, h)
        dt_row = _row_pick(dt_t, h)
        decay = jnp.exp(jnp.where(causal, col - row, -jnp.inf))
        w_intra.append(cbs[h // (SSM_HEADS // SSM_GROUPS)] * decay * dt_row)
        last = jnp.sum(jnp.where(last_row, col, 0.0), axis=0, keepdims=True)
        to_end.append(jnp.exp(last - col) * dt_col)
        e_cum.append(jnp.exp(col))
        c_dec.append(jnp.exp(last))
        d_row.append(_lane_pick(d_skip, h))
    pair = lambda vals, j: jnp.where(lane_lo, vals[2 * j], vals[2 * j + 1])
    ys, h_next = [], []
    for j in range(_PAIRS):
        g = j // 2
        xs = xbc[:, j * LANES:(j + 1) * LANES]
        y_diag = jnp.where(lane_lo, _nn(w_intra[2 * j], xs), _nn(w_intra[2 * j + 1], xs))
        states = _tn(bms[g], xs * pair(to_end, j))
        y_off = _nn(cms[g], h_prev[j]) * pair(e_cum, j)
        ys.append(y_diag + y_off + xs * pair(d_row, j))
        h_next.append(pair(c_dec, j) * h_prev[j] + states)
    y = jnp.concatenate(ys, axis=1) * _silu(z)
    width = D_MODEL // SSM_GROUPS
    y = jnp.concatenate(
        [_rmsnorm(y[:, g * width:(g + 1) * width], norm_g[:, g * width:(g + 1) * width]) for g in range(SSM_GROUPS)],
        axis=1)
    return y, tuple(h_next)


def _shift_down(prev8, x, k):
    if k == 0:
        return x
    win = jnp.concatenate([prev8, x], axis=0)
    return pltpu.roll(win, k, 0)[HALO:]


def _shift_up(x, next8, k):
    if k == 0:
        return x
    n = x.shape[0]
    win = jnp.concatenate([x, next8], axis=0)
    return pltpu.roll(win, n + HALO - k, 0)[:n]


def _conv_pre(prev8, x, w, b):
    out = b + x * w[SSM_CONV - 1:SSM_CONV]
    for i in range(SSM_CONV - 1):
        out = out + _shift_down(prev8, x, SSM_CONV - 1 - i) * w[i:i + 1]
    return out


def _swap_halves(x):
    return pltpu.roll(x, HEAD_DIM, 1)


_PAIRS_PER_KV = ATTN_HEADS // ATTN_KV // 2
_ATTN_SCALE = HEAD_DIM ** -0.5


def _parity_lanes(parity):
    lane = lax.broadcasted_iota(jnp.int32, (1, LANES), 1)
    return (lane >= HEAD_DIM * parity) & (lane < HEAD_DIM * (parity + 1))


def _kv_placed(pair, kv_head):
    mine = jnp.where(_parity_lanes(kv_head), pair, 0.0)
    lo = mine if kv_head == 0 else _swap_halves(mine)
    return lo, _swap_halves(lo)


def _kv_unplaced(d_lo, d_hi, kv_head):
    d = jnp.where(_parity_lanes(0), d_lo, 0.0) + _swap_halves(jnp.where(_parity_lanes(1), d_hi, 0.0))
    return d if kv_head == 0 else _swap_halves(d)


def _band_mask(first):
    shape = (_PAIRS_PER_KV * CHUNK, 2 * CHUNK)
    rows = lax.broadcasted_iota(jnp.int32, shape, 0) & (CHUNK - 1)
    cols = lax.broadcasted_iota(jnp.int32, shape, 1)
    return (cols <= rows + CHUNK) & (cols > rows) & (cols >= CHUNK * first.astype(jnp.int32))


def _attn_probs(s, sink):
    m = jnp.maximum(jnp.max(s, axis=-1, keepdims=True), sink)
    p = jnp.exp(s - m)
    e_sink = jnp.exp(sink - m)
    return p, e_sink, jnp.sum(p, axis=-1, keepdims=True) + e_sink


def _lane_column(col, idx):
    lane = lax.broadcasted_iota(jnp.int32, (1, LANES), 1)
    return jnp.where(lane == idx, col, 0.0)


N_CHIP = 4
N_CORE = 2
_OTHER_CHIPS = (2, 4, 6)


@dataclasses.dataclass
class _Move:
    kind: str
    src: jax.Array

    def dst_shape(self):
        s = self.src.shape
        shape = {"gather": (N_DEV,) + s, "gather_ici": (N_CHIP, N_CORE) + s, "gather_d2d": s,
                 "scatter_d2d": (N_CHIP,) + s[2:], "scatter_ici": s, "scatter": s}[self.kind]
        return jax.ShapeDtypeStruct(tuple(shape), self.src.dtype)


def _peer(x, y, c, k):
    return (1 - x if k & 4 else x, 1 - y if k & 2 else y, 1 - c if k & 1 else c)


def _move_copies(moves, srcs, dsts, send_sems, recv_sems, local_sems):
    x, y, c = lax.axis_index("x"), lax.axis_index("y"), lax.axis_index("c")
    chip = 2 * x + y
    me = 2 * chip + c
    sibling = (x, y, 1 - c)
    all_chips = pl.ds(0, N_CHIP)
    local, remote = [], []

    def push(n, k, src, dst, device):
        remote.append(pltpu.make_async_remote_copy(
            src_ref=src, dst_ref=dst, send_sem=send_sems.at[n, k], recv_sem=recv_sems.at[n, k],
            device_id=device, device_id_type=pl.DeviceIdType.MESH))

    for n, mv in enumerate(moves):
        s, d = srcs[n], dsts[n]
        if mv.kind == "gather":
            local.append(pltpu.make_async_copy(s, d.at[me], local_sems.at[n]))
            for k in range(1, N_DEV):
                push(n, k - 1, s, d.at[me], _peer(x, y, c, k))
        elif mv.kind == "gather_ici":
            local.append(pltpu.make_async_copy(s, d.at[chip, c], local_sems.at[n]))
            for k in _OTHER_CHIPS:
                push(n, k - 1, s, d.at[chip, c], _peer(x, y, c, k))
        elif mv.kind == "gather_d2d":
            push(n, 0, d.at[all_chips, c], d.at[all_chips, c], sibling)
        elif mv.kind == "scatter_d2d":
            push(n, 0, s.at[all_chips, 1 - c], d, sibling)
        elif mv.kind == "scatter":
            local.append(pltpu.make_async_copy(s.at[me], d.at[me], local_sems.at[n]))
            for k in range(1, N_DEV):
                px, py, pc = _peer(x, y, c, k)
                push(n, k - 1, s.at[2 * (2 * px + py) + pc], d.at[me], (px, py, pc))
        else:
            assert mv.kind == "scatter_ici", mv.kind
            local.append(pltpu.make_async_copy(s.at[chip], d.at[chip], local_sems.at[n]))
            for k in _OTHER_CHIPS:
                px, py, _ = _peer(x, y, c, k)
                push(n, k - 1, s.at[2 * px + py], d.at[chip], (px, py, c))
    return local, remote


def _move_aliases(moves, n_in, n_out):
    return {n_in + n: n_out + n for n, mv in enumerate(moves) if mv.kind == "gather_d2d"}


def _pcall(body, *, name, grid, in_specs, out_specs, out_shape, scratch_shapes=(), semantics=(), moves=(),
           aliases=None):
    out_shape, out_specs = list(out_shape), list(out_specs)
    in_specs = list(in_specs)
    if not moves:
        call = pl.pallas_call(
            body, name=name, grid=grid, in_specs=in_specs, out_specs=out_specs, out_shape=out_shape,
            scratch_shapes=list(scratch_shapes), input_output_aliases=aliases or {},
            compiler_params=pltpu.CompilerParams(dimension_semantics=tuple(semantics),
                                                 vmem_limit_bytes=VMEM_LIMIT_BYTES))
        return (lambda *args: (list(call(*args)), []))
    n_in, n_out, n_scr, n_mv = len(in_specs), len(out_shape), len(scratch_shapes), len(moves)
    hbm = pl.BlockSpec(memory_space=pltpu.HBM)

    def carrier(*refs):
        ins, rest = refs[:n_in], refs[n_in:]
        srcs, rest = rest[:n_mv], rest[n_mv:]
        outs, rest = rest[:n_out], rest[n_out:]
        dsts, rest = rest[:n_mv], rest[n_mv:]
        scr, (send_sems, recv_sems, local_sems) = rest[:n_scr], rest[n_scr:]
        first = functools.reduce(jnp.logical_and, [pl.program_id(d) == 0 for d in range(len(grid))])
        last = functools.reduce(jnp.logical_and, [pl.program_id(d) == grid[d] - 1 for d in range(len(grid))])

        @pl.when(first)
        def _():
            local, remote = _move_copies(moves, srcs, dsts, send_sems, recv_sems, local_sems)
            for cp in local + remote:
                cp.start()

        body(*ins, *outs, *scr)

        @pl.when(last)
        def _():
            local, remote = _move_copies(moves, srcs, dsts, send_sems, recv_sems, local_sems)
            for cp in remote + local:
                cp.wait()

    call = pl.pallas_call(
        carrier, name=name, grid=grid,
        in_specs=in_specs + [hbm] * n_mv,
        out_specs=out_specs + [hbm] * n_mv,
        out_shape=out_shape + [mv.dst_shape() for mv in moves],
        scratch_shapes=list(scratch_shapes) + [pltpu.SemaphoreType.DMA((n_mv, N_DEV - 1)),
                                               pltpu.SemaphoreType.DMA((n_mv, N_DEV - 1)),
                                               pltpu.SemaphoreType.DMA((n_mv,))],
        input_output_aliases={**(aliases or {}), **_move_aliases(moves, n_in, n_out)},
        compiler_params=pltpu.CompilerParams(dimension_semantics=("arbitrary",) * len(grid),
                                             vmem_limit_bytes=VMEM_LIMIT_BYTES))

    def run(*args):
        res = list(call(*args, *[mv.src for mv in moves]))
        return res[:n_out], res[n_out:]

    return run


def _exchange(moves, *, name, then_d2d=()):
    n_mv, n_fwd = len(moves), len(then_d2d)
    hbm = pl.BlockSpec(memory_space=pltpu.HBM)
    copies_of = {"gather": N_DEV - 1, "gather_ici": len(_OTHER_CHIPS), "gather_d2d": 1, "scatter_d2d": 1,
                 "scatter_ici": len(_OTHER_CHIPS)}
    first_copy = [sum(copies_of[mv.kind] for mv in moves[:n]) for n in range(n_mv)]

    def body(*refs):
        srcs, dsts, sems = refs[:n_mv], refs[n_mv:2 * n_mv], refs[2 * n_mv:]
        local, remote = _move_copies(moves, srcs, dsts, *sems[:3])
        for cp in local + remote:
            cp.start()
        x, y, c = lax.axis_index("x"), lax.axis_index("y"), lax.axis_index("c")
        chip, sibling = 2 * x + y, (x, y, 1 - c)
        passed, passed_on = [], set()

        def to_sibling(f, k, src, slot):
            cp = pltpu.make_async_remote_copy(src_ref=src, dst_ref=slot, send_sem=sems[3].at[f, k],
                                              recv_sem=sems[4].at[f, k], device_id=sibling,
                                              device_id_type=pl.DeviceIdType.MESH)
            cp.start()
            passed.append(cp)

        for f, n in enumerate(then_d2d):
            assert moves[n].kind == "gather_ici"
            d = dsts[n]
            to_sibling(f, 0, srcs[n], d.at[chip, c])
            for i, k in enumerate(_OTHER_CHIPS):
                remote[first_copy[n] + i].wait_recv()
                passed_on.add(first_copy[n] + i)
                px, py, _ = _peer(x, y, c, k)
                to_sibling(f, k - 1, d.at[2 * px + py, c], d.at[2 * px + py, c])
        for i, cp in enumerate(remote):
            if i in passed_on:
                cp.wait_send()
            else:
                cp.wait()
        for cp in local + passed:
            cp.wait()

    sems = [pltpu.SemaphoreType.DMA((n_mv, N_DEV - 1)), pltpu.SemaphoreType.DMA((n_mv, N_DEV - 1)),
            pltpu.SemaphoreType.DMA((n_mv,))]
    if then_d2d:
        sems += [pltpu.SemaphoreType.DMA((n_fwd, N_DEV - 1)), pltpu.SemaphoreType.DMA((n_fwd, N_DEV - 1))]
    return list(pl.pallas_call(
        body, name=name, in_specs=[hbm] * n_mv, out_specs=[hbm] * n_mv,
        out_shape=[mv.dst_shape() for mv in moves], scratch_shapes=sems,
    )(*[mv.src for mv in moves]))


TM = 512
FF_SHARD = D_FF // N_DEV


def _whole(a):
    nd = a.ndim
    return pl.BlockSpec(a.shape, lambda i: (0,) * nd)


def _rows(width, col=0):
    return pl.BlockSpec((TM, width), lambda i: (i, col))


def _acc_row(width):
    return pl.BlockSpec((1, width), lambda i: (0, 0))


def _unpack(res_landed, moves, n_out):
    res, landed = res_landed
    res = res[0] if n_out == 1 else res
    return (res, landed) if moves else res


def _norm_matmul(x, g, w, *, name, emit_y, moves=()):
    t, d = x.shape
    n = w.shape[1]

    def body(x_ref, g_ref, w_ref, *outs):
        y = _rmsnorm(x_ref[...], g_ref[...]).astype(BF16)
        if emit_y:
            outs[0][...] = y
        outs[-1][...] = lax.dot_general(y, w_ref[...], _NN, preferred_element_type=F32)

    shapes = ([jax.ShapeDtypeStruct((t, d), BF16)] if emit_y else []) + [jax.ShapeDtypeStruct((t, n), F32)]
    specs = ([_rows(d)] if emit_y else []) + [_rows(n)]
    return _unpack(_pcall(body, name=name, grid=(t // TM,), in_specs=[_rows(d), _acc_row(d), _whole(w)],
                          out_specs=specs, out_shape=shapes, semantics=("parallel",), moves=moves)(x, g, w),
                   moves, len(shapes))


def _residual_matmul(a, w, res, *, name, bias=None, norm_g=None, w_transposed=False, moves=()):
    t, k = a.shape
    n = w.shape[0 if w_transposed else 1]
    contract = _NT if w_transposed else _NN
    has_res, has_bias, has_norm = res is not None, bias is not None, norm_g is not None

    def body(a_ref, w_ref, *rest):
        rest = list(rest)
        res_ref = rest.pop(0) if has_res else None
        b_ref = rest.pop(0) if has_bias else None
        g_ref = rest.pop(0) if has_norm else None
        h = lax.dot_general(a_ref[...].astype(BF16), w_ref[...], contract, preferred_element_type=F32)
        if has_res:
            h = h + res_ref[...]
        if has_bias:
            h = h + b_ref[...]
        rest[0][...] = h
        if has_norm:
            rest[1][...] = _rmsnorm(h, g_ref[...]).astype(BF16)

    rows_in = [res] if has_res else []
    extra = ([bias] if has_bias else []) + ([norm_g] if has_norm else [])
    shapes = [jax.ShapeDtypeStruct((t, n), F32)] + ([jax.ShapeDtypeStruct((t, n), BF16)] if has_norm else [])
    return _unpack(_pcall(body, name=name, grid=(t // TM,),
                          in_specs=[_rows(k), _whole(w)] + [_rows(n)] * len(rows_in) + [_acc_row(n)] * len(extra),
                          out_specs=[_rows(n)] * len(shapes), out_shape=shapes, semantics=("parallel",),
                          moves=moves)(a, w, *rows_in, *extra), moves, len(shapes))


def _mlp_up(y, w_cols, *, name, moves=()):
    t, d = y.shape

    def body(y_ref, w_ref, up_ref):
        yv = y_ref[...]
        for j in range(N_DEV):
            up_ref[:, j * FF_SHARD:(j + 1) * FF_SHARD] = lax.dot_general(
                yv, w_ref[j], _NN, preferred_element_type=F32).astype(up_ref.dtype)

    return _unpack(_pcall(body, name=name, grid=(t // TM,), in_specs=[_rows(d), _whole(w_cols)],
                          out_specs=[_rows(D_FF)], out_shape=[jax.ShapeDtypeStruct((t, D_FF), BF16)],
                          semantics=("parallel",), moves=moves)(y, w_cols), moves, 1)


def _sq_relu(u):
    return jnp.square(jnp.maximum(u.astype(F32), 0.0))


def _down_blocks(w_refs):
    for j in range(N_DEV):
        off = j * FF_SHARD
        for w_ref in w_refs:
            yield off, w_ref.shape[1], w_ref[j]
            off += w_ref.shape[1]


def _mlp_down(up, w_rows, res, *, name, norm_g=None, moves=()):
    t = up.shape[0]
    has_norm = norm_g is not None
    n_w = len(w_rows)

    def body(up_ref, *rest):
        w_refs, res_ref, rest = rest[:n_w], rest[n_w], rest[n_w + 1:]
        h = res_ref[...]
        for off, rows, w_blk in _down_blocks(w_refs):
            act = _sq_relu(up_ref[:, off:off + rows]).astype(BF16)
            h = h + lax.dot_general(act, w_blk, _NN, preferred_element_type=F32)
        if has_norm:
            g_ref, h_ref, y_ref = rest
            y_ref[...] = _rmsnorm(h, g_ref[...]).astype(BF16)
        else:
            (h_ref,) = rest
        h_ref[...] = h

    shapes = [jax.ShapeDtypeStruct((t, D_MODEL), F32)] + ([jax.ShapeDtypeStruct((t, D_MODEL), BF16)] if has_norm else [])
    return _unpack(_pcall(body, name=name, grid=(t // TM,),
                          in_specs=[_rows(D_FF)] + [_whole(w) for w in w_rows] + [_rows(D_MODEL)]
                          + ([_acc_row(D_MODEL)] if has_norm else []),
                          out_specs=[_rows(D_MODEL)] * len(shapes), out_shape=shapes, semantics=("parallel",),
                          moves=moves)(up, *w_rows, res, *([norm_g] if has_norm else [])), moves, len(shapes))


def _mlp_down_dx(dh, w_rows, up, *, name, moves=()):
    t = up.shape[0]
    n_w = len(w_rows)

    def body(dh_ref, *rest):
        w_refs, (up_ref, o_ref) = rest[:n_w], rest[n_w:]
        dhv = dh_ref[...]
        for off, rows, w_blk in _down_blocks(w_refs):
            cols = slice(off, off + rows)
            d_act = lax.dot_general(dhv, w_blk, _NT, preferred_element_type=F32)
            o_ref[:, cols] = (d_act * (2.0 * jnp.maximum(up_ref[:, cols].astype(F32), 0.0))).astype(o_ref.dtype)

    return _unpack(_pcall(body, name=name, grid=(t // TM,),
                          in_specs=[_rows(D_MODEL)] + [_whole(w) for w in w_rows] + [_rows(D_FF)],
                          out_specs=[_rows(D_FF)], out_shape=[jax.ShapeDtypeStruct((t, D_FF), BF16)],
                          semantics=("parallel",), moves=moves)(dh, *w_rows, up), moves, 1)


def _dw_by_cols(x, dy, *, name, tn, by_device=False, moves=()):
    t, k = x.shape
    n = dy.shape[1]
    assert n % tn == 0, (name, n, tn)

    def body(x_ref, dy_ref, o_ref):
        o_ref[...] = lax.dot_general(x_ref[...].astype(BF16), dy_ref[...].astype(BF16), _TN,
                                     preferred_element_type=F32).astype(o_ref.dtype)

    if by_device:
        out_spec, out_shape = pl.BlockSpec((None, k, tn), lambda j: (j, 0, 0)), (n // tn, k, tn)
    else:
        out_spec, out_shape = pl.BlockSpec((k, tn), lambda j: (0, j)), (k, n)
    return _unpack(_pcall(body, name=name, grid=(n // tn,),
                          in_specs=[_whole(x), pl.BlockSpec((t, tn), lambda j: (0, j))],
                          out_specs=[out_spec], out_shape=[jax.ShapeDtypeStruct(out_shape, BF16)],
                          semantics=("parallel",), moves=moves)(x, dy), moves, 1)


def _dw_by_rows(x, dy, *, name, tk, square_relu=False, column_sums=False, moves=()):
    t, k = x.shape
    n = dy.shape[1]
    assert k % tk == 0, (name, k, tk)

    def body(x_ref, dy_ref, o_ref, *sums):
        xv = _sq_relu(x_ref[...]) if square_relu else x_ref[...]
        o_ref[...] = lax.dot_general(xv.astype(BF16), dy_ref[...].astype(BF16), _TN,
                                     preferred_element_type=F32).astype(o_ref.dtype)
        if column_sums:
            sums[0][...] = jnp.sum(xv.astype(F32), axis=0, keepdims=True)

    shapes = [jax.ShapeDtypeStruct((k, n), BF16)] + ([jax.ShapeDtypeStruct((1, k), F32)] if column_sums else [])
    specs = [pl.BlockSpec((tk, n), lambda j: (j, 0))] + ([pl.BlockSpec((1, tk), lambda j: (0, j))] if column_sums else [])
    return _unpack(_pcall(body, name=name, grid=(k // tk,),
                          in_specs=[pl.BlockSpec((t, tk), lambda j: (0, j)), _whole(dy)],
                          out_specs=specs, out_shape=shapes,
                          semantics=("parallel",), moves=moves)(x, dy), moves, len(shapes))


def _dx(dy, w, *, name, partial=None, moves=()):
    t, k = dy.shape
    n = w.shape[0]
    has_partial = partial is not None

    def body(dy_ref, w_ref, *rest):
        out = lax.dot_general(dy_ref[...].astype(BF16), w_ref[...], _NT, preferred_element_type=F32)
        if has_partial:
            out = out + rest[0][...]
        rest[-1][...] = out

    return _unpack(_pcall(body, name=name, grid=(t // TM,),
                          in_specs=[_rows(k), _whole(w)] + ([_rows(n)] if has_partial else []),
                          out_specs=[_rows(n)], out_shape=[jax.ShapeDtypeStruct((t, n), F32)],
                          semantics=("parallel",), moves=moves)(dy, w, *([partial] if has_partial else [])),
                   moves, 1)


def _dx_norm(dy, w, h, g, dres, *, name, partial=None, by_device_cols=False, w_transposed=False, moves=()):
    t, k = dy.shape
    d = h.shape[1]
    has_partial = partial is not None

    def body(dy_ref, w_ref, h_ref, g_ref, dres_ref, *rest):
        if by_device_cols:
            kc = k // N_DEV
            d_y = jnp.zeros((TM, d), F32)
            for j in range(N_DEV):
                d_y = d_y + lax.dot_general(dy_ref[:, j * kc:(j + 1) * kc].astype(BF16), w_ref[j], _NT,
                                            preferred_element_type=F32)
        else:
            d_y = lax.dot_general(dy_ref[...].astype(BF16), w_ref[...], _NN if w_transposed else _NT,
                                  preferred_element_type=F32)
        if has_partial:
            d_y = d_y + rest[0][...]
        dh_ref, dhb_ref, dg_ref, cs_ref = rest[-4:]
        _, vjp = jax.vjp(_rmsnorm, h_ref[...], g_ref[...])
        dh, dg = vjp(d_y)
        dh = dh + dres_ref[...]
        dh_ref[...] = dh
        dhb_ref[...] = dh.astype(BF16)

        @pl.when(pl.program_id(0) == 0)
        def _():
            dg_ref[...] = jnp.zeros_like(dg_ref)
            cs_ref[...] = jnp.zeros_like(cs_ref)

        dg_ref[...] += dg
        cs_ref[...] += jnp.sum(dh, axis=0, keepdims=True)

    shapes = [jax.ShapeDtypeStruct((t, d), F32), jax.ShapeDtypeStruct((t, d), BF16),
              jax.ShapeDtypeStruct((1, d), F32), jax.ShapeDtypeStruct((1, d), F32)]
    return _unpack(_pcall(body, name=name, grid=(t // TM,),
                          in_specs=[_rows(k), _whole(w), _rows(d), _acc_row(d), _rows(d)]
                          + ([_rows(d)] if has_partial else []),
                          out_specs=[_rows(d), _rows(d), _acc_row(d), _acc_row(d)], out_shape=shapes,
                          semantics=("arbitrary",), moves=moves)(dy, w, h, g, dres, *([partial] if has_partial else [])),
                   moves, 4)


def _pair_add(by_core, theirs, core, *, name, tb=512):
    n_chip, _, r, c = by_core.shape
    tb = min(tb, r)
    assert r % tb == 0, (name, r, tb)

    def body(core_ref, a_ref, b_ref, o_ref):
        del core_ref
        o_ref[...] = (a_ref[...].astype(F32) + b_ref[...].astype(F32)).astype(o_ref.dtype)

    blk = pl.BlockSpec((None, tb, c), lambda ch, i, core_ref: (ch, i, 0))
    return pl.pallas_call(
        body, name=name,
        grid_spec=pltpu.PrefetchScalarGridSpec(
            num_scalar_prefetch=1, grid=(n_chip, r // tb),
            in_specs=[pl.BlockSpec((None, None, tb, c), lambda ch, i, core_ref: (ch, core_ref[0], i, 0)), blk],
            out_specs=blk),
        out_shape=jax.ShapeDtypeStruct((n_chip, r, c), by_core.dtype),
        compiler_params=pltpu.CompilerParams(dimension_semantics=("parallel", "parallel"),
                                             vmem_limit_bytes=VMEM_LIMIT_BYTES),
    )(core, by_core, theirs)


def _mlp_down_loss(up, w_rows, res, g, target, *, name):
    t, d = res.shape
    n_w = len(w_rows)

    def body(up_ref, *rest):
        w_refs, (res_ref, g_ref, tgt_ref, loss_ref, dh_ref, dhb_ref, dg_ref) = rest[:n_w], rest[n_w:]
        h = res_ref[...]
        for off, rows, w_blk in _down_blocks(w_refs):
            act = _sq_relu(up_ref[:, off:off + rows]).astype(BF16)
            h = h + lax.dot_general(act, w_blk, _NN, preferred_element_type=F32)

        def f(hh, gg):
            err = jnp.square(_rmsnorm(hh, gg) - tgt_ref[...])
            return 0.5 * jnp.sum(jnp.mean(err, axis=-1, keepdims=True), axis=0, keepdims=True)

        val, vjp = jax.vjp(f, h, g_ref[...])
        dh, dg = vjp(jnp.ones((1, 1), F32))
        dh_ref[...] = dh
        dhb_ref[...] = dh.astype(BF16)

        @pl.when(pl.program_id(0) == 0)
        def _():
            loss_ref[...] = jnp.zeros_like(loss_ref)
            dg_ref[...] = jnp.zeros_like(dg_ref)

        loss_ref[...] += val
        dg_ref[...] += dg

    return _pcall(
        body, name=name, grid=(t // TM,),
        in_specs=[_rows(D_FF)] + [_whole(w) for w in w_rows] + [_rows(d), _acc_row(d), _rows(d)],
        out_specs=[pl.BlockSpec((8, LANES), lambda i: (0, 0)), _rows(d), _rows(d), _acc_row(d)],
        out_shape=[jax.ShapeDtypeStruct((8, LANES), F32), jax.ShapeDtypeStruct((t, d), F32),
                   jax.ShapeDtypeStruct((t, d), BF16), jax.ShapeDtypeStruct((1, d), F32)],
        semantics=("arbitrary",),
    )(up, *w_rows, res, g, target)[0]


def _gmlp_fwd(proj_uv, ln_g, ln_b, w_s, b_s, *, name, moves=()):
    t = proj_uv.shape[0]
    w = D_MODEL

    def body(u_ref, v_ref, g_ref, b_ref, w_ref, bs_ref, o_ref):
        o_ref[...] = _gmlp_chunk(u_ref[...], v_ref[...], g_ref[...], b_ref[...], w_ref[...],
                                 bs_ref[...]).astype(o_ref.dtype)

    row = pl.BlockSpec((1, w), lambda i: (0, 0))
    res, landed = _pcall(
        body, name=name, grid=(t // CHUNK,),
        in_specs=[pl.BlockSpec((CHUNK, w), lambda i: (i, 0)), pl.BlockSpec((CHUNK, w), lambda i: (i, 1)), row, row,
                  pl.BlockSpec((GM_GROUPS, CHUNK, CHUNK), lambda i: (0, 0, 0)),
                  pl.BlockSpec((GM_GROUPS, CHUNK, 1), lambda i: (0, 0, 0))],
        out_specs=[pl.BlockSpec((CHUNK, w), lambda i: (i, 0))],
        out_shape=[jax.ShapeDtypeStruct((t, 2 * w), BF16)],
        semantics=("parallel",), moves=moves,
    )(proj_uv, proj_uv, ln_g, ln_b, w_s, b_s)
    return (res[0], landed) if moves else res[0]


def _gmlp_bwd(proj_uv, d_mix, ln_g, ln_b, w_s, b_s, *, name, moves=()):
    t = proj_uv.shape[0]
    w = D_MODEL

    def body(u_ref, v_ref, da_ref, g_ref, b_ref, w_ref, bs_ref, duv_ref, dg_ref, db_ref, dw_ref, dbs_ref):
        _, vjp = jax.vjp(_gmlp_chunk, u_ref[...], v_ref[...], g_ref[...], b_ref[...], w_ref[...], bs_ref[...])
        du, dv, dg, db, dw, dbs = vjp(da_ref[...])
        duv_ref[:, :w] = du.astype(duv_ref.dtype)
        duv_ref[:, w:] = dv.astype(duv_ref.dtype)

        @pl.when(pl.program_id(0) == 0)
        def _():
            dg_ref[...] = jnp.zeros_like(dg_ref)
            db_ref[...] = jnp.zeros_like(db_ref)
            dw_ref[...] = jnp.zeros_like(dw_ref)
            dbs_ref[...] = jnp.zeros_like(dbs_ref)

        dg_ref[...] += dg
        db_ref[...] += db
        dw_ref[...] += dw
        dbs_ref[...] += dbs

    row = pl.BlockSpec((1, w), lambda i: (0, 0))
    ws = pl.BlockSpec((GM_GROUPS, CHUNK, CHUNK), lambda i: (0, 0, 0))
    bs = pl.BlockSpec((GM_GROUPS, CHUNK, 1), lambda i: (0, 0, 0))
    res, landed = _pcall(
        body, name=name, grid=(t // CHUNK,),
        in_specs=[pl.BlockSpec((CHUNK, w), lambda i: (i, 0)), pl.BlockSpec((CHUNK, w), lambda i: (i, 1)),
                  pl.BlockSpec((CHUNK, w), lambda i: (i, 0)), row, row, ws, bs],
        out_specs=[pl.BlockSpec((CHUNK, 2 * w), lambda i: (i, 0)), row, row, ws, bs],
        out_shape=[jax.ShapeDtypeStruct((t, 2 * w), BF16), jax.ShapeDtypeStruct((1, w), F32),
                   jax.ShapeDtypeStruct((1, w), F32), jax.ShapeDtypeStruct((GM_GROUPS, CHUNK, CHUNK), F32),
                   jax.ShapeDtypeStruct((GM_GROUPS, CHUNK, 1), F32)],
        semantics=("arbitrary",), moves=moves,
    )(proj_uv, proj_uv, d_mix, ln_g, ln_b, w_s, b_s)
    return (res, landed) if moves else res


_HALO_PER_CHUNK = CHUNK // HALO
_DT_BLOCK = (CONV_DIM + D_MODEL) // LANES


def _ssd_fwd(proj_rest, mix, conv_w, conv_b, dt_bias, a_log, d_skip, norm_g, *, name, moves=()):
    t = proj_rest.shape[0]
    nc = t // CHUNK

    def body(x_ref, prev_ref, z_ref, dt_ref, mix_ref, cw_ref, cb_ref, dtb_ref, al_ref, ds_ref, ng_ref, y_ref, hs_ref,
             pre_ref, h_scr):
        del mix_ref
        i = pl.program_id(0)

        @pl.when(i == 0)
        def _():
            h_scr[...] = jnp.zeros_like(h_scr)

        prev8 = jnp.where(i == 0, 0.0, prev_ref[...])
        pre = _conv_pre(prev8, x_ref[...], cw_ref[...], cb_ref[...])
        pre_ref[...] = pre
        hs_ref[0] = h_scr[...]
        h_prev = tuple(h_scr[j] for j in range(_PAIRS))
        y, h_next = _ssd_chunk(pre, z_ref[...], dt_ref[...], h_prev, dtb_ref[...], al_ref[...], ds_ref[...],
                               ng_ref[...])
        y_ref[...] = y.astype(y_ref.dtype)
        for j in range(_PAIRS):
            h_scr[j] = h_next[j]

    small = pl.BlockSpec((1, LANES), lambda i: (0, 0))
    res, landed = _pcall(
        body, name=name, grid=(nc,),
        in_specs=[pl.BlockSpec((CHUNK, CONV_DIM), lambda i: (i, 0)),
                  pl.BlockSpec((HALO, CONV_DIM), lambda i: (jnp.maximum(i * _HALO_PER_CHUNK - 1, 0), 0)),
                  pl.BlockSpec((CHUNK, D_MODEL), lambda i: (i, CONV_DIM // D_MODEL)),
                  pl.BlockSpec((CHUNK, LANES), lambda i: (i, _DT_BLOCK)),
                  pl.BlockSpec(memory_space=pl.ANY),
                  pl.BlockSpec((SSM_CONV, CONV_DIM), lambda i: (0, 0)),
                  pl.BlockSpec((1, CONV_DIM), lambda i: (0, 0)),
                  small, small, small, pl.BlockSpec((1, D_MODEL), lambda i: (0, 0))],
        out_specs=[pl.BlockSpec((CHUNK, D_MODEL), lambda i: (i, 1)),
                   pl.BlockSpec((1, _PAIRS, SSM_STATE, LANES), lambda i: (i, 0, 0, 0)),
                   pl.BlockSpec((CHUNK, CONV_DIM), lambda i: (i, 0))],
        out_shape=[jax.ShapeDtypeStruct((t, 2 * D_MODEL), BF16),
                   jax.ShapeDtypeStruct((nc, _PAIRS, SSM_STATE, LANES), F32),
                   jax.ShapeDtypeStruct((t, CONV_DIM), F32)],
        scratch_shapes=[pltpu.VMEM((_PAIRS, SSM_STATE, LANES), F32)],
        semantics=("arbitrary",), moves=moves, aliases={4: 0},
    )(proj_rest, proj_rest, proj_rest, proj_rest, mix, conv_w, conv_b, dt_bias, a_log, d_skip, norm_g)
    return (res, landed) if moves else res


def _ssd_bwd(proj_rest, pre, h_states, d_mix, dt_bias, a_log, d_skip, norm_g, *, name, moves=()):
    t = proj_rest.shape[0]
    nc = t // CHUNK

    def body(pre_ref, z_ref, dt_ref, hs_ref, dy_ref, dtb_ref, al_ref, ds_ref, ng_ref,
             dpre_ref, dz_ref, ddt_ref, ddtb_ref, dal_ref, dds_ref, dng_ref, dh_scr):
        i = pl.program_id(0)

        @pl.when(i == 0)
        def _():
            dh_scr[...] = jnp.zeros_like(dh_scr)
            ddtb_ref[...] = jnp.zeros_like(ddtb_ref)
            dal_ref[...] = jnp.zeros_like(dal_ref)
            dds_ref[...] = jnp.zeros_like(dds_ref)
            dng_ref[...] = jnp.zeros_like(dng_ref)

        h_prev = tuple(hs_ref[0, j] for j in range(_PAIRS))
        _, vjp = jax.vjp(_ssd_chunk, pre_ref[...], z_ref[...], dt_ref[...], h_prev, dtb_ref[...], al_ref[...],
                         ds_ref[...], ng_ref[...])
        dpre, dz, ddt, dh_prev, ddtb, dal, dds, dng = vjp((dy_ref[...], tuple(dh_scr[j] for j in range(_PAIRS))))
        dpre_ref[...] = dpre
        dz_ref[...] = dz.astype(dz_ref.dtype)
        ddt_ref[...] = ddt.astype(ddt_ref.dtype)
        for j in range(_PAIRS):
            dh_scr[j] = dh_prev[j]
        ddtb_ref[...] += ddtb
        dal_ref[...] += dal
        dds_ref[...] += dds
        dng_ref[...] += dng

    rev = lambda i: nc - 1 - i
    small = pl.BlockSpec((1, LANES), lambda i: (0, 0))
    wide = pl.BlockSpec((1, D_MODEL), lambda i: (0, 0))
    res, landed = _pcall(
        body, name=name, grid=(nc,),
        in_specs=[pl.BlockSpec((CHUNK, CONV_DIM), lambda i: (rev(i), 0)),
                  pl.BlockSpec((CHUNK, D_MODEL), lambda i: (rev(i), CONV_DIM // D_MODEL)),
                  pl.BlockSpec((CHUNK, LANES), lambda i: (rev(i), _DT_BLOCK)),
                  pl.BlockSpec((1, _PAIRS, SSM_STATE, LANES), lambda i: (rev(i), 0, 0, 0)),
                  pl.BlockSpec((CHUNK, D_MODEL), lambda i: (rev(i), 1)),
                  small, small, small, wide],
        out_specs=[pl.BlockSpec((CHUNK, CONV_DIM), lambda i: (rev(i), 0)),
                   pl.BlockSpec((CHUNK, D_MODEL), lambda i: (rev(i), 0)),
                   pl.BlockSpec((CHUNK, LANES), lambda i: (rev(i), 0)),
                   small, small, small, wide],
        out_shape=[jax.ShapeDtypeStruct((t, CONV_DIM), F32), jax.ShapeDtypeStruct((t, D_MODEL), BF16),
                   jax.ShapeDtypeStruct((t, LANES), BF16),
                   jax.ShapeDtypeStruct((1, LANES), F32), jax.ShapeDtypeStruct((1, LANES), F32),
                   jax.ShapeDtypeStruct((1, LANES), F32), jax.ShapeDtypeStruct((1, D_MODEL), F32)],
        scratch_shapes=[pltpu.VMEM((_PAIRS, SSM_STATE, LANES), F32)],
        semantics=("arbitrary",), moves=moves,
    )(pre, proj_rest, proj_rest, h_states, d_mix, dt_bias, a_log, d_skip, norm_g)
    return (res, landed) if moves else res


def _conv_bwd(proj_rest, dpre, dz, ddt, conv_w, *, name, tb=256, moves=()):
    t = proj_rest.shape[0]
    nb = t // tb
    per = tb // HALO

    def body(x_ref, prev_ref, dpre_ref, next_ref, dz_ref, ddt_ref, cw_ref, drest_ref, dcw_ref, dcb_ref):
        i = pl.program_id(0)

        @pl.when(i == 0)
        def _():
            dcw_ref[...] = jnp.zeros_like(dcw_ref)
            dcb_ref[...] = jnp.zeros_like(dcb_ref)

        x = x_ref[...]
        dp = dpre_ref[...]
        w = cw_ref[...]
        prev8 = jnp.where(i == 0, 0.0, prev_ref[...])
        next8 = jnp.where(i == nb - 1, 0.0, next_ref[...])
        dx = dp * w[SSM_CONV - 1:SSM_CONV]
        for j in range(SSM_CONV - 1):
            dx = dx + _shift_up(dp, next8, SSM_CONV - 1 - j) * w[j:j + 1]
        drest_ref[:, :CONV_DIM] = dx.astype(drest_ref.dtype)
        drest_ref[:, CONV_DIM:CONV_DIM + D_MODEL] = dz_ref[...].astype(drest_ref.dtype)
        drest_ref[:, CONV_DIM + D_MODEL:] = ddt_ref[...].astype(drest_ref.dtype)
        for j in range(SSM_CONV):
            dcw_ref[j:j + 1, :] += jnp.sum(dp * _shift_down(prev8, x, SSM_CONV - 1 - j), axis=0, keepdims=True)
        dcb_ref[...] += jnp.sum(dp, axis=0, keepdims=True)

    res, landed = _pcall(
        body, name=name, grid=(nb,),
        in_specs=[pl.BlockSpec((tb, CONV_DIM), lambda i: (i, 0)),
                  pl.BlockSpec((HALO, CONV_DIM), lambda i: (jnp.maximum(i * per - 1, 0), 0)),
                  pl.BlockSpec((tb, CONV_DIM), lambda i: (i, 0)),
                  pl.BlockSpec((HALO, CONV_DIM), lambda i: (jnp.minimum((i + 1) * per, nb * per - 1), 0)),
                  pl.BlockSpec((tb, D_MODEL), lambda i: (i, 0)),
                  pl.BlockSpec((tb, LANES), lambda i: (i, 0)),
                  pl.BlockSpec((SSM_CONV, CONV_DIM), lambda i: (0, 0))],
        out_specs=[pl.BlockSpec((tb, REST_W), lambda i: (i, 0)),
                   pl.BlockSpec((SSM_CONV, CONV_DIM), lambda i: (0, 0)),
                   pl.BlockSpec((1, CONV_DIM), lambda i: (0, 0))],
        out_shape=[jax.ShapeDtypeStruct((t, REST_W), BF16), jax.ShapeDtypeStruct((SSM_CONV, CONV_DIM), F32),
                   jax.ShapeDtypeStruct((1, CONV_DIM), F32)],
        semantics=("arbitrary",), moves=moves,
    )(proj_rest, proj_rest, dpre, dpre, dz, ddt, conv_w)
    return (res, landed) if moves else res


_KV_BLOCK = D_MODEL // (2 * LANES)
_SINK_ROWS = _PAIRS_PER_KV * CHUNK


def _stack_pairs(ref, kv_head):
    base = kv_head * _PAIRS_PER_KV
    return jnp.concatenate([ref[:, (base + p) * LANES:(base + p + 1) * LANES] for p in range(_PAIRS_PER_KV)], axis=0)


def _attn_fwd(qkv, sinks, *, name, moves=()):
    t = qkv.shape[0]
    nb = t // CHUNK

    def body(q_ref, kvp_ref, kvc_ref, s_ref, o_ref, p_ref, st_ref):
        valid = _band_mask(pl.program_id(0) == 0)
        kv = jnp.concatenate([kvp_ref[...], kvc_ref[...]], axis=0)
        stats = jnp.zeros((_SINK_ROWS, LANES), F32)
        passes = [(j, e) for j in range(ATTN_KV) for e in range(2)]
        vs = [_kv_placed(kv[:, LANES:], j) for j in range(ATTN_KV)]
        scores = []
        for j in range(ATTN_KV):
            q4 = _stack_pairs(q_ref, j) * _ATTN_SCALE
            ks = _kv_placed(kv[:, :LANES], j)
            scores += [jnp.where(valid, _dg(q4, ks[e], _NT), -jnp.inf) for e in range(2)]
        probs = [_attn_probs(s, s_ref[j, e]) for s, (j, e) in zip(scores, passes)]
        outs = [None] * ATTN_KV
        for (p, e_sink, den), (j, e) in zip(probs, passes):
            inv = 1.0 / den
            o = _dg(p, vs[j][e], _NN) * inv
            outs[j] = o if outs[j] is None else outs[j] + o
            p_ref[0, 2 * j + e] = p.astype(p_ref.dtype)
            stats = stats + _lane_column(inv, 2 * j + e) + _lane_column(e_sink, 4 + 2 * j + e)
        for j in range(ATTN_KV):
            for pair in range(_PAIRS_PER_KV):
                col = (j * _PAIRS_PER_KV + pair) * LANES
                o_ref[:, col:col + LANES] = outs[j][pair * CHUNK:(pair + 1) * CHUNK].astype(o_ref.dtype)
        st_ref[0] = stats

    return _unpack(_pcall(
        body, name=name, grid=(nb,),
        in_specs=[pl.BlockSpec((CHUNK, D_MODEL), lambda i: (i, 0)),
                  pl.BlockSpec((CHUNK, 2 * LANES), lambda i: (jnp.maximum(i - 1, 0), _KV_BLOCK)),
                  pl.BlockSpec((CHUNK, 2 * LANES), lambda i: (i, _KV_BLOCK)),
                  pl.BlockSpec((ATTN_KV, 2, _SINK_ROWS, 1), lambda i: (0, 0, 0, 0))],
        out_specs=[pl.BlockSpec((CHUNK, D_MODEL), lambda i: (i, 0)),
                   pl.BlockSpec((1, 2 * ATTN_KV, _SINK_ROWS, 2 * CHUNK), lambda i: (i, 0, 0, 0)),
                   pl.BlockSpec((1, _SINK_ROWS, LANES), lambda i: (i, 0, 0))],
        out_shape=[jax.ShapeDtypeStruct((t, D_MODEL), BF16),
                   jax.ShapeDtypeStruct((nb, 2 * ATTN_KV, _SINK_ROWS, 2 * CHUNK), BF16),
                   jax.ShapeDtypeStruct((nb, _SINK_ROWS, LANES), F32)],
        semantics=("parallel",), moves=moves,
    )(qkv, qkv, qkv, sinks), moves, 3)


def _attn_bwd(qkv, probs, stats, attn, d_o, *, name, moves=()):
    t = qkv.shape[0]
    nb = t // CHUNK

    def body(q_ref, kvp_ref, kvc_ref, p_ref, st_ref, o_ref, do_ref, dqkv_ref, ds_ref, dkv_scr):
        @pl.when(pl.program_id(0) == 0)
        def _():
            dkv_scr[...] = jnp.zeros_like(dkv_scr)
            ds_ref[...] = jnp.zeros_like(ds_ref)

        kv = jnp.concatenate([kvp_ref[...], kvc_ref[...]], axis=0)
        table = st_ref[0]
        d_k = jnp.zeros((2 * CHUNK, LANES), F32)
        d_v = jnp.zeros((2 * CHUNK, LANES), F32)
        passes = [(j, e) for j in range(ATTN_KV) for e in range(2)]
        q4s = [_stack_pairs(q_ref, j) for j in range(ATTN_KV)]
        ks = [_kv_placed(kv[:, :LANES], j) for j in range(ATTN_KV)]
        vs = [_kv_placed(kv[:, LANES:], j) for j in range(ATTN_KV)]
        d_nums, d_dens = [], []
        for j in range(ATTN_KV):
            do4, o4 = _stack_pairs(do_ref, j), _stack_pairs(o_ref, j).astype(F32)
            for e in range(2):
                inv, e_sink = _col_pick(table, 2 * j + e), _col_pick(table, 4 + 2 * j + e)
                do_e = jnp.where(_parity_lanes(e), do4, 0.0)
                d_nums.append(do_e * inv)
                d_dens.append(-jnp.sum(do_e * o4, axis=1, keepdims=True) * inv)
                ds_ref[j, e] += d_dens[-1] * e_sink
        d_ps = [_dg(d_num, vs[j][e], _NT) for d_num, (j, e) in zip(d_nums, passes)]
        ps = [p_ref[0, 2 * j + e].astype(F32) for j, e in passes]
        d_ss = [p * (d_p + d_den) for p, d_p, d_den in zip(ps, d_ps, d_dens)]
        for j in range(ATTN_KV):
            ds_lo, ds_hi = d_ss[2 * j], d_ss[2 * j + 1]
            dq4 = (_dg(ds_lo, ks[j][0], _NN) + _dg(ds_hi, ks[j][1], _NN)) * _ATTN_SCALE
            for pair in range(_PAIRS_PER_KV):
                col = (j * _PAIRS_PER_KV + pair) * LANES
                dqkv_ref[:, col:col + LANES] = dq4[pair * CHUNK:(pair + 1) * CHUNK]
            dk = [_dg(ds, q4s[j], _TN) * _ATTN_SCALE for ds in (ds_lo, ds_hi)]
            dv = [_dg(ps[2 * j + e], d_nums[2 * j + e], _TN) for e in range(2)]
            d_k = d_k + _kv_unplaced(dk[0], dk[1], j)
            d_v = d_v + _kv_unplaced(dv[0], dv[1], j)
        d_kv = jnp.concatenate([d_k, d_v], axis=1)
        dqkv_ref[:, D_MODEL:] = d_kv[CHUNK:] + dkv_scr[...]
        dkv_scr[...] = d_kv[:CHUNK]

    cur = lambda i: (nb - 1 - i, 0)
    sk = pl.BlockSpec((ATTN_KV, 2, _SINK_ROWS, 1), lambda i: (0, 0, 0, 0))
    res, landed = _pcall(
        body, name=name, grid=(nb,),
        in_specs=[pl.BlockSpec((CHUNK, D_MODEL), cur),
                  pl.BlockSpec((CHUNK, 2 * LANES), lambda i: (jnp.maximum(nb - 2 - i, 0), _KV_BLOCK)),
                  pl.BlockSpec((CHUNK, 2 * LANES), lambda i: (nb - 1 - i, _KV_BLOCK)),
                  pl.BlockSpec((1, 2 * ATTN_KV, _SINK_ROWS, 2 * CHUNK), lambda i: (nb - 1 - i, 0, 0, 0)),
                  pl.BlockSpec((1, _SINK_ROWS, LANES), lambda i: (nb - 1 - i, 0, 0)),
                  pl.BlockSpec((CHUNK, D_MODEL), cur), pl.BlockSpec((CHUNK, D_MODEL), cur)],
        out_specs=[pl.BlockSpec((CHUNK, QKV_DIM), cur), sk],
        out_shape=[jax.ShapeDtypeStruct((t, QKV_DIM), F32), jax.ShapeDtypeStruct((ATTN_KV, 2, _SINK_ROWS, 1), F32)],
        scratch_shapes=[pltpu.VMEM((CHUNK, 2 * LANES), F32)],
        semantics=("arbitrary",), moves=moves,
    )(qkv, qkv, qkv, probs, stats, attn, d_o)
    return (res, landed) if moves else res


def _adamw(parts, w, m, v, *, name, tb=512, moves=()):
    layers, r, c = w.shape
    n = parts[0].shape[0]
    tb = min(tb, r)
    assert r % tb == 0 and len(parts) == layers, (name, r, tb)
    nb = r // tb

    def body(*refs):
        p_refs = refs[:layers]
        w_ref, m_ref, v_ref, g_ref, d_ref, nm_ref, nv_ref = refs[layers:]
        for layer in range(layers):
            @pl.when(pl.program_id(0) == layer)
            def _(p_ref=p_refs[layer]):
                g = p_ref[0].astype(F32)
                for s in range(1, n):
                    g = g + p_ref[s].astype(F32)
                m_new = ADAM_B1 * m_ref[...] + (1.0 - ADAM_B1) * g
                v_new = ADAM_B2 * v_ref[...] + (1.0 - ADAM_B2) * jnp.square(g)
                m_hat = m_new / (1.0 - ADAM_B1 ** ADAM_STEP)
                v_hat = v_new / (1.0 - ADAM_B2 ** ADAM_STEP)
                g_ref[...] = g
                d_ref[...] = -ADAM_LR * (m_hat / (jnp.sqrt(v_hat) + ADAM_EPS) + ADAM_WD * w_ref[...])
                nm_ref[...] = m_new
                nv_ref[...] = v_new

    part_spec = lambda layer: pl.BlockSpec(
        (n, tb, c), lambda l, i: (0, jnp.clip(i + (l - layer) * nb, 0, nb - 1), 0))
    blk = pl.BlockSpec((None, tb, c), lambda l, i: (l, i, 0))
    res, landed = _pcall(
        body, name=name, grid=(layers, nb),
        in_specs=[part_spec(layer) for layer in range(layers)] + [blk, blk, blk],
        out_specs=[blk] * 4,
        out_shape=[jax.ShapeDtypeStruct((layers, r, c), F32)] * 4,
        semantics=("arbitrary", "arbitrary"), moves=moves,
    )(*parts, w, m, v)
    return (res, landed) if moves else res


def _as_rows(a):
    flat = a.reshape(-1)
    pad = (-flat.shape[0]) % PACK_W
    if pad:
        flat = jnp.pad(flat, (0, pad))
    return flat.reshape(-1, PACK_W)


def _cols_from_shards(g):
    return jnp.transpose(g, (1, 0, 2)).reshape(g.shape[1], -1)


def _shard_cols(shards, lo, hi):
    c = shards.shape[2]
    pieces = []
    for j in range(shards.shape[0]):
        a, b = max(lo, j * c), min(hi, (j + 1) * c)
        if a < b:
            pieces.append(shards[j, :, a - j * c:b - j * c])
    return pieces


def _cols_of(sources, lo, hi):
    pieces = []
    for arr, col0, first, last in sources:
        a, b = max(lo, first), min(hi, last)
        if a < b:
            pieces.append(arr[:, col0 + a - first:col0 + b - first])
    return pieces


def _pad_lanes(a):
    return jnp.pad(a, ((0, 0), (0, LANES - a.shape[1])))


def kernel(x, norm_mix_g, norm_mlp_g, final_norm_g, w_in_even, w_out_even, gm_ln_g, gm_ln_b, gm_w_s, gm_b_s, ssm_conv_w, ssm_conv_b, ssm_dt_bias, ssm_a_log, ssm_d, ssm_norm_g, w_qkv, b_qkv, w_o, b_o, attn_sinks, w_up, w_down, loss_target, m_norm_mix_g, m_norm_mlp_g, m_final_norm_g, m_w_in_even, m_w_out_even, m_gm_ln_g, m_gm_ln_b, m_gm_w_s, m_gm_b_s, m_ssm_conv_w, m_ssm_conv_b, m_ssm_dt_bias, m_ssm_a_log, m_ssm_d, m_ssm_norm_g, m_w_qkv, m_b_qkv, m_w_o, m_b_o, m_attn_sinks, m_w_up, m_w_down, v_norm_mix_g, v_norm_mlp_g, v_final_norm_g, v_w_in_even, v_w_out_even, v_gm_ln_g, v_gm_ln_b, v_gm_w_s, v_gm_b_s, v_ssm_conv_w, v_ssm_conv_b, v_ssm_dt_bias, v_ssm_a_log, v_ssm_d, v_ssm_norm_g, v_w_qkv, v_b_qkv, v_w_o, v_b_o, v_attn_sinks, v_w_up, v_w_down):
    names = ["norm_mix_g", "norm_mlp_g", "final_norm_g", "w_in_even", "w_out_even", "gm_ln_g", "gm_ln_b", "gm_w_s",
             "gm_b_s", "ssm_conv_w", "ssm_conv_b", "ssm_dt_bias", "ssm_a_log", "ssm_d", "ssm_norm_g", "w_qkv",
             "b_qkv", "w_o", "b_o", "attn_sinks", "w_up", "w_down"]
    env = locals()
    W = {n: env[n] for n in names}
    M = {n: env["m_" + n] for n in names}
    V = {n: env["v_" + n] for n in names}
    big = ["w_in_even", "w_out_even", "w_qkv", "w_o", "w_up", "w_down"]
    small_sharded = ["ssm_conv_w", "b_qkv", "b_o"]
    replicated = [n for n in names if n not in big and n not in small_sharded]
    me = 4 * lax.axis_index("x") + 2 * lax.axis_index("y") + lax.axis_index("c")
    t = x.shape[1]
    xs = x.reshape(t, D_MODEL)
    target = loss_target.reshape(t, D_MODEL)
    gather = lambda a: _Move("gather", a)
    over_ici = lambda a: _Move("gather_ici", a)
    over_d2d = lambda a: _Move("gather_d2d", a)
    by_core = lambda a: a.reshape((N_CHIP, N_CORE) + a.shape[1:])
    to_sibling = lambda a: [_Move("scatter_d2d", by_core(a))]
    my_core = lax.axis_index("c").astype(jnp.int32).reshape(1)
    pair = lambda a, theirs, name: _pair_add(by_core(a), theirs, my_core, name=name)
    to_chips = lambda a: _Move("scatter_ici", a)
    whole = lambda a: a.reshape((N_DEV,) + a.shape[2:])
    row = lambda a: a.reshape(1, D_MODEL)

    small_flat = jnp.concatenate([W[n].reshape(-1) for n in small_sharded])
    w_in_g, small_g = _exchange([over_ici(w_in_even[0].astype(BF16)), gather(_as_rows(small_flat))],
                                name="gather_w_in", then_d2d=[0])
    w_in_s = whole(w_in_g)
    z_lo, xbc_lo, dt_lo = 2 * D_MODEL, 3 * D_MODEL, 3 * D_MODEL + CONV_DIM
    w_uv = jnp.concatenate(_shard_cols(w_in_s, 0, z_lo), axis=1)
    w_rest = jnp.concatenate(_shard_cols(w_in_s, xbc_lo, dt_lo) + _shard_cols(w_in_s, z_lo, xbc_lo)
                             + _shard_cols(w_in_s, dt_lo, IN_EVEN)
                             + [jnp.zeros((D_MODEL, LANES - SSM_HEADS), BF16)], axis=1)
    small_all = small_g.reshape(N_DEV, -1)
    n_cw = SSM_CONV * CONV_DIM // N_DEV
    n_bq = QKV_DIM // N_DEV
    conv_w = _cols_from_shards(small_all[:, :n_cw].reshape(N_DEV, SSM_CONV, CONV_DIM // N_DEV))
    bqkv = small_all[:, n_cw:n_cw + n_bq].reshape(1, QKV_DIM)
    bo = small_all[:, n_cw + n_bq:n_cw + n_bq + D_MODEL // N_DEV].reshape(1, D_MODEL)

    conv_b = ssm_conv_b.reshape(1, CONV_DIM)
    dt_bias, a_log, d_skip = _pad_lanes(ssm_dt_bias), _pad_lanes(ssm_a_log), _pad_lanes(ssm_d)
    gm_w = gm_w_s[0]
    gm_b = gm_b_s[0].reshape(GM_GROUPS, CHUNK, 1)
    sink_rows = jnp.repeat(jnp.transpose(attn_sinks.reshape(ATTN_KV, _PAIRS_PER_KV, 2), (0, 2, 1)), CHUNK,
                           axis=2).reshape(ATTN_KV, 2, _SINK_ROWS, 1)
    w_up_b, w_down_b = w_up.astype(BF16), w_down.astype(BF16)

    w_down0_a, w_down0_b = w_down_b[0, :FF_SHARD // 2], w_down_b[0, FF_SHARD // 2:]
    (y0, proj_uv), (w_qkv_g,) = _norm_matmul(xs, row(norm_mix_g[0]), w_uv, name="proj_uv", emit_y=True,
                                             moves=[over_ici(jnp.transpose(w_qkv[0]).astype(BF16))])
    proj_rest, (w_out_g,) = _norm_matmul(xs, row(norm_mix_g[0]), w_rest, name="proj_rest", emit_y=False,
                                         moves=[over_ici(w_out_even[0].astype(BF16))])
    mix, (w_down0_a, w_out_g, w_qkv_g) = _gmlp_fwd(
        proj_uv, gm_ln_g, gm_ln_b, gm_w, gm_b, name="gmlp_fwd",
        moves=[over_ici(w_down0_a), over_d2d(w_out_g), over_d2d(w_qkv_g)])
    (mix, h_states, conv_pre), (w_up0_g, w_down0_a) = _ssd_fwd(
        proj_rest, mix, conv_w, conv_b, dt_bias, a_log, d_skip, ssm_norm_g, name="ssd_fwd",
        moves=[over_ici(w_up_b[0]), over_d2d(w_down0_a)])
    w_out_f = whole(w_out_g).reshape(2 * D_MODEL, D_MODEL)
    (h1, y1), (w_down0_b, w_up0_g) = _residual_matmul(
        mix, w_out_f, xs, name="mix_out", norm_g=row(norm_mlp_g[0]),
        moves=[over_ici(w_down0_b), over_d2d(w_up0_g)])
    up0, (w_down0_b,) = _mlp_up(y1, whole(w_up0_g), name="mlp_up0", moves=[over_d2d(w_down0_b)])
    w_down_g = [[whole(w_down0_a), whole(w_down0_b)]]
    (h2, y2), (w_o_g,) = _mlp_down(up0, w_down_g[0], h1, name="mlp_down0", norm_g=row(norm_mix_g[1]),
                                   moves=[over_ici(w_o[0].astype(BF16))])
    wqkv = whole(w_qkv_g).reshape(QKV_DIM, D_MODEL)
    qkv, (w_o_g,) = _residual_matmul(y2, wqkv, None, name="qkv", bias=bqkv, w_transposed=True,
                                     moves=[over_d2d(w_o_g)])
    wo = whole(w_o_g).reshape(D_MODEL, D_MODEL)
    (attn, attn_p, attn_stats), (w_up1_g, w_down1_g) = _attn_fwd(
        qkv, sink_rows, name="attn_fwd", moves=[over_ici(w_up_b[1]), over_ici(w_down_b[1])])
    (h3, y3), (w_up1_g,) = _residual_matmul(attn, wo, h2, name="attn_out", bias=bo, norm_g=row(norm_mlp_g[1]),
                                            moves=[over_d2d(w_up1_g)])
    w_up_g = [whole(w_up0_g), whole(w_up1_g)]
    up1, (w_down1_g,) = _mlp_up(y3, w_up_g[1], name="mlp_up1", moves=[over_d2d(w_down1_g)])
    w_down_g.append([whole(w_down1_g)])
    loss_part, dh4, dh4_b, d_final_g = _mlp_down_loss(up1, w_down_g[1], h3, row(final_norm_g), target,
                                                      name="mlp_down1_loss")

    by_dev_rows = lambda a: a.reshape((N_DEV, a.shape[0] // N_DEV) + a.shape[1:])

    def mlp_bwd(dh, dh_b, h, y, up, layer, first_moves=()):
        res = _mlp_down_dx(dh_b, w_down_g[layer], up, name=f"mlp_down_dx{layer}", moves=first_moves)
        d_up, first_landed = res if first_moves else (res, [])
        g_down = _dw_by_rows(up, dh_b, name=f"mlp_down_dw{layer}", tk=FF_SHARD, square_relu=True)
        g_down = by_dev_rows(g_down)
        g_up, (theirs,) = _dw_by_cols(y, d_up, name=f"mlp_up_dw{layer}", tn=FF_SHARD, by_device=True,
                                      moves=to_sibling(g_down))
        q_down = pair(g_down, theirs, f"mlp_down_pair{layer}")
        (dh_new, dh_new_b, dg, cs), (theirs,) = _dx_norm(
            d_up, w_up_g[layer], h, row(norm_mlp_g[layer]), dh, name=f"mlp_up_dx{layer}", by_device_cols=True,
            moves=to_sibling(g_up))
        q_up = pair(g_up, theirs, f"mlp_up_pair{layer}")
        return dh_new, dh_new_b, cs, dg, q_up, q_down, first_landed

    dh3, dh3_b, cs3, g_nmlp1, q_up1, q_down1, _ = mlp_bwd(dh4, dh4_b, h3, y3, up1, 1)
    g_bo = cs3
    g_wo = by_dev_rows(_dw_by_cols(attn, dh3_b, name="attn_out_dw", tn=FF_SHARD))
    d_attn = _dx(dh3_b, wo, name="attn_out_dx")
    (dqkv, d_sink), (r_down1, r_up1) = _attn_bwd(qkv, attn_p, attn_stats, attn, d_attn, name="attn_bwd",
                                                 moves=[to_chips(q_down1), to_chips(q_up1)])
    g_wqkv, g_bqkv = _dw_by_rows(dqkv, y2, name="qkv_dw", tk=QKV_DIM // 2, column_sums=True)
    g_wqkv = by_dev_rows(g_wqkv)
    dh2, dh2_b, g_nmix1, _ = _dx_norm(dqkv, wqkv, h2, row(norm_mix_g[1]), dh3, name="qkv_dx", w_transposed=True)
    dh1, dh1_b, _, g_nmlp0, q_up0, q_down0, (r_wqkv, r_wo) = mlp_bwd(
        dh2, dh2_b, h1, y1, up0, 0, first_moves=[_Move("scatter", g_wqkv), _Move("scatter", g_wo)])

    d_mix = _dx(dh1_b, w_out_f, name="mix_out_dx")
    g_wout = by_dev_rows(_dw_by_rows(mix, dh1_b, name="mix_out_dw", tk=FF_SHARD))
    (d_uv, g_ln_g, g_ln_b, g_gm_w, g_gm_b), (r_down0, theirs) = _gmlp_bwd(
        proj_uv, d_mix, gm_ln_g, gm_ln_b, gm_w, gm_b, name="gmlp_bwd", moves=[to_chips(q_down0)] + to_sibling(g_wout))
    q_wout = pair(g_wout, theirs, "mix_out_pair")

    early = [("norm_mlp_g", None), ("final_norm_g", None), ("norm_mix_g", 1), ("gm_ln_g", None), ("gm_ln_b", None),
             ("gm_w_s", None), ("gm_b_s", None), ("attn_sinks", None)]
    late = [("norm_mix_g", 0), ("ssm_conv_b", None), ("ssm_dt_bias", None), ("ssm_a_log", None), ("ssm_d", None),
            ("ssm_norm_g", None)]
    early_sharded, late_sharded = ["b_qkv", "b_o"], ["ssm_conv_w"]
    small_grads = {
        ("norm_mlp_g", None): jnp.concatenate([g_nmlp0, g_nmlp1], axis=0),
        ("final_norm_g", None): d_final_g, ("norm_mix_g", 1): g_nmix1,
        ("gm_ln_g", None): g_ln_g, ("gm_ln_b", None): g_ln_b, ("gm_w_s", None): g_gm_w, ("gm_b_s", None): g_gm_b,
        ("attn_sinks", None): jnp.transpose(
            jnp.sum(d_sink.reshape(ATTN_KV, 2, _PAIRS_PER_KV, CHUNK), axis=3), (0, 2, 1)),
        "b_qkv": g_bqkv, "b_o": g_bo,
    }
    pack = lambda keys: _as_rows(jnp.concatenate([small_grads[key].reshape(-1) for key in keys]))
    (dpre, dz, ddt, g_dtb, g_alog, g_dskip, g_ssm_ng), (r_up0, r_wout, early_recv) = _ssd_bwd(
        proj_rest, conv_pre, h_states, d_mix, dt_bias, a_log, d_skip, ssm_norm_g, name="ssd_bwd",
        moves=[to_chips(q_up0), to_chips(q_wout), gather(pack(early + early_sharded))])
    d_rest, g_conv_w, g_conv_b = _conv_bwd(proj_rest, dpre, dz, ddt, conv_w, name="conv_bwd")
    g_w_uv = _dw_by_cols(y0, d_uv, name="proj_uv_dw", tn=FF_SHARD)
    g_w_rest = _dw_by_cols(y0, d_rest, name="proj_rest_dw", tn=REST_W // 5)
    in_cols = [(g_w_uv, 0, 0, z_lo), (g_w_rest, CONV_DIM, z_lo, xbc_lo), (g_w_rest, 0, xbc_lo, dt_lo),
               (g_w_rest, CONV_DIM + D_MODEL, dt_lo, IN_EVEN)]
    in_shard = IN_EVEN // N_DEV
    g_w_in = jnp.stack([jnp.concatenate(_cols_of(in_cols, j * in_shard, (j + 1) * in_shard), axis=1)
                        for j in range(N_DEV)])
    dy0, (theirs,) = _dx(d_uv, w_uv, name="proj_uv_dx", moves=to_sibling(g_w_in))
    q_w_in = pair(g_w_in, theirs, "proj_pair")
    (dx, _, g_nmix0, _), r_w_in = _dx_norm(d_rest, w_rest, xs, row(norm_mix_g[0]), dh1, name="proj_rest_dx",
                                           partial=dy0, moves=[to_chips(q_w_in)])
    small_grads.update({
        ("loss", None): loss_part[:1, :1],
        ("norm_mix_g", 0): g_nmix0, ("ssm_conv_b", None): g_conv_b,
        ("ssm_dt_bias", None): g_dtb[:, :SSM_HEADS], ("ssm_a_log", None): g_alog[:, :SSM_HEADS],
        ("ssm_d", None): g_dskip[:, :SSM_HEADS], ("ssm_norm_g", None): g_ssm_ng, "ssm_conv_w": g_conv_w,
    })


    def update(n, parts, moves=(), transposed=False):
        shape = W[n].shape
        if transposed:
            as3 = lambda a: jnp.transpose(a[0])[None]
            back = lambda a: jnp.transpose(a[0])[None]
        else:
            as3 = lambda a: a.reshape((len(parts),) + parts[0].shape[1:])
            back = lambda a: a.reshape(shape)
        res = _adamw(parts, as3(W[n]), as3(M[n]), as3(V[n]), name="adamw_" + n, moves=moves)
        res, landed = res if moves else (res, [])
        return [back(a) for a in res], landed

    out = {}
    late_keys = late + late_sharded + [("loss", None)]
    out["w_o"], (late_recv,) = update("w_o", [r_wo], moves=[gather(pack(late_keys))])
    out["w_down"], _ = update("w_down", [r_down0, r_down1])
    out["w_up"], _ = update("w_up", [r_up0, r_up1])
    out["w_out_even"], _ = update("w_out_even", [r_wout])
    out["w_qkv"], _ = update("w_qkv", [r_wqkv], transposed=True)
    out["w_in_even"], _ = update("w_in_even", list(r_w_in))

    def unpacked(recv, keys):
        flat, res, o = recv.reshape(N_DEV, -1), {}, 0
        for key in keys:
            res[key] = flat[:, o:o + small_grads[key].size]
            o += small_grads[key].size
        return res

    arrived = {**unpacked(early_recv, early + early_sharded), **unpacked(late_recv, late_keys)}
    piece = lambda tree, key: tree[key[0]] if key[1] is None else tree[key[0]][key[1]]

    def rows_by_device(cat):
        pad = (-cat.shape[1]) % PACK_W
        return jnp.pad(cat, ((0, 0), (0, pad))).reshape(N_DEV, -1, PACK_W)

    rep_keys = early + late
    rep_parts = rows_by_device(jnp.concatenate([arrived[key] for key in rep_keys], axis=1))
    flat_rep = lambda tree: _as_rows(jnp.concatenate([piece(tree, key).reshape(-1) for key in rep_keys]))[None]
    rep_res = _adamw([rep_parts], flat_rep(W), flat_rep(M), flat_rep(V), name="adamw_replicated")
    sh_keys = early_sharded + late_sharded
    shard_parts = []
    for n in sh_keys:
        full = arrived[n].reshape((N_DEV,) + small_grads[n].shape)
        c = full.shape[-1] // N_DEV
        shard_parts.append(lax.dynamic_slice_in_dim(full, me * c, c, axis=full.ndim - 1).reshape(N_DEV, -1))
    sh_rows = rows_by_device(jnp.concatenate(shard_parts, axis=1))
    flat_sh = lambda tree: _as_rows(jnp.concatenate([tree[n].reshape(-1) for n in sh_keys]))[None]
    sh_res = _adamw([sh_rows], flat_sh(W), flat_sh(M), flat_sh(V), name="adamw_small_sharded")

    def unpack_replicated(rows):
        flat, vals, o = rows.reshape(-1), {}, 0
        for key in rep_keys:
            size = piece(W, key).size
            vals[key] = flat[o:o + size]
            o += size
        res = {}
        for n in replicated:
            if (n, None) in vals:
                res[n] = vals[(n, None)].reshape(W[n].shape)
            else:
                res[n] = jnp.stack([vals[(n, r)] for r in range(W[n].shape[0])]).reshape(W[n].shape)
        return res

    def unpack_sharded(rows):
        flat, res, o = rows.reshape(-1), {}, 0
        for n in sh_keys:
            res[n] = flat[o:o + W[n].size].reshape(W[n].shape)
            o += W[n].size
        return res

    results = []
    for idx in range(4):
        d = {n: out[n][idx] for n in big}
        d.update(unpack_replicated(rep_res[idx]))
        d.update(unpack_sharded(sh_res[idx]))
        results.append(d)

    loss = jnp.sum(arrived[("loss", None)])
    grad_x = dx.reshape(x.shape)
    final = [loss, grad_x]
    for d in results:
        final.extend(d[n] for n in names)
    return tuple(final)
```
